```python
import math
import jax, jax.numpy as jnp
from jax import lax
import numpy as np

D_MODEL = 1024
BATCH = 4
SEQ = 8192
DEPTH = 1

CHUNK = 64
QBLOCK = 128
EPS = 1e-6

DN_HEADS = 4
DN_DK = 128
DN_DV = 128
CONV_K = 4
SA_HEADS = 4
SA_DQK = 128
SA_DV = 128
KV_RANK = 256
IDX_HEADS = 4
IDX_DIM = 64
IDX_TOPK_MAX = 256
SM_SCALE = SA_DQK ** -0.5
IDX_W_SCALE = (IDX_HEADS * IDX_DIM) ** -0.5
N_EXPERTS = 64
TOP_K = 8
N_GROUPS = 8
TOPK_GROUPS = 4
D_EXPERT = 256
D_SHARED = 256
ROUTED_SCALE = 2.5
MOE_BLOCK = 128

DN_QK = DN_HEADS * DN_DK
DN_V = DN_HEADS * DN_DV
CONV_DIM = 2 * DN_QK + DN_V
SA_Q = SA_HEADS * SA_DQK
IDX_Q = IDX_HEADS * IDX_DIM
IN_SIZES = (DN_QK, DN_QK, DN_V, DN_V, DN_HEADS, DN_HEADS, SA_Q, KV_RANK, IDX_Q, IDX_DIM, IDX_HEADS)
D_IN = 2 * DN_QK + 2 * DN_V + 2 * DN_HEADS + SA_Q + KV_RANK + IDX_Q + IDX_DIM + IDX_HEADS
MIX_WIDTH = DN_V + SA_HEADS * SA_DV

kernel_name = "hybrid_deltanet_dsa_moe_adaln_block"


def rms_norm(x, g):
    x32 = x.astype(jnp.float32)
    y = x32 * lax.rsqrt(jnp.mean(x32 * x32, axis=-1, keepdims=True) + EPS)
    return (y * g.astype(jnp.float32)).astype(x.dtype)


def layer_norm(x, g, b):
    x32 = x.astype(jnp.float32)
    mu = jnp.mean(x32, axis=-1, keepdims=True)
    xc = x32 - mu
    y = xc * lax.rsqrt(jnp.mean(xc * xc, axis=-1, keepdims=True) + EPS)
    return (y * g.astype(jnp.float32) + b.astype(jnp.float32)).astype(x.dtype)


def l2_normalize(x):
    x32 = x.astype(jnp.float32)
    return (x32 * lax.rsqrt(jnp.sum(x32 * x32, axis=-1, keepdims=True) + EPS)).astype(x.dtype)


def split_cols(a, sizes):
    offsets = []
    acc = 0
    for s in sizes[:-1]:
        acc += s
        offsets.append(acc)
    return jnp.split(a, offsets, axis=-1)


def causal_depthwise_conv(x, w):
    ch = x.shape[-1]
    return lax.conv_general_dilated(
        x, w[:, None, :].astype(x.dtype), window_strides=(1,),
        padding=[(w.shape[0] - 1, 0)], dimension_numbers=("NWC", "WIO", "NWC"),
        feature_group_count=ch)


def gated_delta_rule(q, k, v, g, beta):
    B, T, H, DK = q.shape
    DV = v.shape[-1]
    n = T // CHUNK
    f32 = jnp.float32

    def blk(a):
        a = a.astype(f32).reshape(B, n, CHUNK, H, *a.shape[3:])
        return jnp.swapaxes(a, 2, 3)

    q, k, v, g, beta = blk(q), blk(k), blk(v), blk(g), blk(beta)
    g = jnp.cumsum(g, axis=-1)
    pos = jnp.arange(CHUNK)
    causal = pos[:, None] >= pos[None, :]
    strict = pos[:, None] > pos[None, :]
    decay = jnp.exp(jnp.where(causal, g[..., :, None] - g[..., None, :], -jnp.inf))
    kb = k * beta[..., None]
    a_strict = jnp.where(strict, jnp.einsum("bnhid,bnhjd->bnhij", kb, k) * decay, 0.0)
    lhs = a_strict + jnp.eye(CHUNK, dtype=f32)
    rhs = jnp.concatenate([v * beta[..., None], kb * jnp.exp(g)[..., None]], axis=-1)
    sol = lax.linalg.triangular_solve(lhs, rhs, left_side=True, lower=True)
    u, w = sol[..., :DV], sol[..., DV:]
    qk = jnp.einsum("bnhid,bnhjd->bnhij", q, k) * decay
    q_dec = q * jnp.exp(g)[..., None]
    k_dec = k * jnp.exp(g[..., -1:] - g)[..., None]
    g_tot = jnp.exp(g[..., -1])

    def step(S, inp):
        q_c, qk_c, u_c, w_c, k_c, gt = inp
        v_new = u_c - jnp.einsum("bhcd,bhde->bhce", w_c, S)
        o = jnp.einsum("bhcd,bhde->bhce", q_c, S) + jnp.einsum("bhij,bhje->bhie", qk_c, v_new)
        S = S * gt[..., None, None] + jnp.einsum("bhcd,bhce->bhde", k_c, v_new)
        return S, o

    S0 = jnp.zeros((B, H, DK, DV), f32)
    xs = tuple(jnp.moveaxis(a, 1, 0) for a in (q_dec, qk, u, w, k_dec, g_tot))
    _, o = lax.scan(step, S0, xs)
    return jnp.swapaxes(jnp.moveaxis(o, 0, 1), 2, 3).reshape(B, T, H, DV)


def sparse_attention(q_lat, ckv, q_ix, k_ix, w_ix, topk):
    B, T, H, R = q_lat.shape
    nb = T // QBLOCK
    key_pos = jnp.arange(T)

    def to_blocks(a):
        return jnp.moveaxis(a.reshape(B, nb, QBLOCK, *a.shape[2:]), 1, 0)

    def attend_block(args):
        ql, qi, wi, start = args
        t = start + jnp.arange(QBLOCK)
        limit = (t // CHUNK + 1) * CHUNK
        admissible = key_pos[None, :] < limit[:, None]
        rel = jax.nn.relu(jnp.einsum("bqhd,bsd->bqhs", qi, k_ix))
        score = jnp.einsum("bqh,bqhs->bqs", wi, rel).astype(jnp.float32)
        score = jnp.where(admissible[None], score, -jnp.inf)
        _, idx = lax.top_k(score, topk)
        valid = idx < limit[None, :, None]
        kv = jax.vmap(lambda a, i: a[i])(ckv, idx)
        logits = jnp.einsum("bqhr,bqkr->bqhk", ql, kv).astype(jnp.float32) * SM_SCALE
        logits = jnp.where(valid[:, :, None, :], logits, -jnp.inf)
        p = jax.nn.softmax(logits, axis=-1).astype(kv.dtype)
        return jnp.einsum("bqhk,bqkr->bqhr", p, kv)

    starts = jnp.arange(nb) * QBLOCK
    out = lax.map(attend_block, (to_blocks(q_lat), to_blocks(q_ix), to_blocks(w_ix), starts))
    return jnp.moveaxis(out, 0, 1).reshape(B, T, H, R)


def token_mixers(h, w_in, conv_w, a_log, dt_bias, dn_norm_g, kv_norm_g, w_uk, w_uv,
                 idx_k_ln_g, idx_k_ln_b, topk):
    B, T, _ = h.shape
    proj = h @ w_in
    (q_dn, k_dn, v_dn, z_dn, b_dn, a_dn, q_sa, ckv, q_ix, k_ix, w_ix) = split_cols(proj, IN_SIZES)

    qkv = jax.nn.silu(causal_depthwise_conv(jnp.concatenate([q_dn, k_dn, v_dn], axis=-1), conv_w))
    q_dn, k_dn, v_dn = jnp.split(qkv, [DN_QK, 2 * DN_QK], axis=-1)
    q_dn = l2_normalize(q_dn.reshape(B, T, DN_HEADS, DN_DK)) * (DN_DK ** -0.5)
    k_dn = l2_normalize(k_dn.reshape(B, T, DN_HEADS, DN_DK))
    v_dn = v_dn.reshape(B, T, DN_HEADS, DN_DV)
    beta = jax.nn.sigmoid(b_dn.astype(jnp.float32))
    g = -jnp.exp(a_log.astype(jnp.float32)) * jax.nn.softplus(
        a_dn.astype(jnp.float32) + dt_bias.astype(jnp.float32))
    o_dn = gated_delta_rule(q_dn, k_dn, v_dn, g, beta).astype(h.dtype)
    o_dn = rms_norm(o_dn, dn_norm_g) * jax.nn.silu(z_dn.reshape(B, T, DN_HEADS, DN_DV))

    q_sa = q_sa.reshape(B, T, SA_HEADS, SA_DQK)
    ckv = rms_norm(ckv, kv_norm_g)
    q_lat = jnp.einsum("bthd,hrd->bthr", q_sa, w_uk)
    q_ix = q_ix.reshape(B, T, IDX_HEADS, IDX_DIM)
    k_ix = layer_norm(k_ix, idx_k_ln_g, idx_k_ln_b)
    w_ix = w_ix * IDX_W_SCALE
    o_lat = sparse_attention(q_lat, ckv, q_ix, k_ix, w_ix, topk)
    o_sa = jnp.einsum("bthr,hrd->bthd", o_lat, w_uv)

    return jnp.concatenate([o_dn.reshape(B, T, DN_V), o_sa.reshape(B, T, SA_HEADS * SA_DV)], axis=-1)


def moe_ffn(h, router_w, router_b, w_gate, w_up, w_down, s_gate, s_up, s_down):
    B, T, D = h.shape
    xt = h.reshape(-1, D)
    n_tok = xt.shape[0]
    scores = jax.nn.sigmoid((xt @ router_w).astype(jnp.float32))
    choice = scores + router_b.astype(jnp.float32)
    grouped = choice.reshape(n_tok, N_GROUPS, N_EXPERTS // N_GROUPS)
    group_score = jnp.sum(lax.top_k(grouped, 2)[0], axis=-1)
    _, top_groups = lax.top_k(group_score, TOPK_GROUPS)
    group_mask = jnp.any(top_groups[..., None] == jnp.arange(N_GROUPS), axis=1)
    expert_mask = jnp.repeat(group_mask, N_EXPERTS // N_GROUPS, axis=1)
    _, top_e = lax.top_k(jnp.where(expert_mask, choice, -jnp.inf), TOP_K)
    gate = jnp.take_along_axis(scores, top_e, axis=1)
    gate = gate / jnp.sum(gate, axis=-1, keepdims=True) * ROUTED_SCALE

    n_assign = n_tok * TOP_K
    flat_e = top_e.reshape(-1)
    order = jnp.argsort(flat_e)
    sorted_e = flat_e[order]
    sorted_tok = (order // TOP_K).astype(jnp.int32)
    sorted_gate = gate.reshape(-1)[order]
    counts = jnp.bincount(flat_e, length=N_EXPERTS)
    padded = (counts + MOE_BLOCK - 1) // MOE_BLOCK * MOE_BLOCK
    starts = jnp.cumsum(counts) - counts
    pad_ends = jnp.cumsum(padded)
    pad_starts = pad_ends - padded
    dest = pad_starts[sorted_e] + jnp.arange(n_assign) - starts[sorted_e]
    cap = n_assign + N_EXPERTS * MOE_BLOCK
    n_blocks = cap // MOE_BLOCK
    buf_tok = jnp.full((cap,), n_tok, jnp.int32).at[dest].set(sorted_tok)
    buf_gate = jnp.zeros((cap,), jnp.float32).at[dest].set(sorted_gate)
    blk_e = jnp.minimum(jnp.searchsorted(pad_ends, jnp.arange(n_blocks) * MOE_BLOCK, side="right"),
                        N_EXPERTS - 1)

    def expert_block(args):
        tok, gt, e = args
        xb = jnp.take(xt, tok, axis=0, mode="fill", fill_value=0)
        hb = jax.nn.silu(xb @ w_gate[e]) * (xb @ w_up[e])
        return (hb @ w_down[e]) * gt[:, None].astype(xb.dtype)

    y = lax.map(expert_block, (buf_tok.reshape(n_blocks, MOE_BLOCK),
                               buf_gate.reshape(n_blocks, MOE_BLOCK), blk_e))
    routed = jax.ops.segment_sum(y.reshape(cap, D), buf_tok, num_segments=n_tok)
    shared = (jax.nn.silu(xt @ s_gate) * (xt @ s_up)) @ s_down
    return (routed + shared).reshape(B, T, D)


def setup_inputs(seed: int = 0) -> dict:
    key = jax.random.key(seed)
    ks = jax.random.split(key, 32)
    f32 = jnp.float32
    L = DEPTH

    def nrm(k, shape, fan_in):
        return jax.random.normal(k, shape, f32) * (fan_in ** -0.5)

    def gain(k, shape):
        return 1.0 + 0.02 * jax.random.normal(k, shape, f32)

    x = jax.random.normal(ks[0], (BATCH, SEQ, D_MODEL), f32)
    c = jax.random.normal(ks[1], (BATCH, D_MODEL), f32)
    ada_w = 0.02 * jax.random.normal(ks[2], (L, D_MODEL, 6 * D_MODEL), f32)
    ada_b = 0.02 * jax.random.normal(ks[3], (L, 6 * D_MODEL), f32)
    norm1_g = gain(ks[4], (L, D_MODEL))
    w_in = nrm(ks[5], (L, D_MODEL, D_IN), D_MODEL)
    conv_w = nrm(ks[6], (L, CONV_K, CONV_DIM), CONV_K)
    a_log = jnp.log(jax.random.uniform(ks[7], (L, DN_HEADS), f32, 1.0, 16.0))
    dt = jnp.exp(jax.random.uniform(ks[8], (L, DN_HEADS), f32, math.log(1e-3), math.log(1e-1)))
    dt_bias = dt + jnp.log(-jnp.expm1(-dt))
    dn_norm_g = gain(ks[9], (L, DN_DV))
    kv_norm_g = gain(ks[10], (L, KV_RANK))
    w_uk = nrm(ks[11], (L, SA_HEADS, KV_RANK, SA_DQK), KV_RANK)
    w_uv = nrm(ks[12], (L, SA_HEADS, KV_RANK, SA_DV), KV_RANK)
    idx_k_ln_g = gain(ks[13], (L, IDX_DIM))
    idx_k_ln_b = 0.02 * jax.random.normal(ks[14], (L, IDX_DIM), f32)
    w_out = nrm(ks[15], (L, MIX_WIDTH, D_MODEL), MIX_WIDTH)
    norm2_g = gain(ks[16], (L, D_MODEL))
    router_w = nrm(ks[17], (L, D_MODEL, N_EXPERTS), D_MODEL)
    router_b = 0.01 * jax.random.normal(ks[18], (L, N_EXPERTS), f32)
    exp_w_gate = nrm(ks[19], (L, N_EXPERTS, D_MODEL, D_EXPERT), D_MODEL)
    exp_w_up = nrm(ks[20], (L, N_EXPERTS, D_MODEL, D_EXPERT), D_MODEL)
    exp_w_down = nrm(ks[21], (L, N_EXPERTS, D_EXPERT, D_MODEL), D_EXPERT)
    sh_w_gate = nrm(ks[22], (L, D_MODEL, D_SHARED), D_MODEL)
    sh_w_up = nrm(ks[23], (L, D_MODEL, D_SHARED), D_MODEL)
    sh_w_down = nrm(ks[24], (L, D_SHARED, D_MODEL), D_SHARED)
    final_g = gain(ks[25], (D_MODEL,))
    return {"x": x, "c": c, "ada_w": ada_w, "ada_b": ada_b, "norm1_g": norm1_g, "w_in": w_in,
            "conv_w": conv_w, "a_log": a_log, "dt_bias": dt_bias, "dn_norm_g": dn_norm_g,
            "kv_norm_g": kv_norm_g, "w_uk": w_uk, "w_uv": w_uv, "idx_k_ln_g": idx_k_ln_g,
            "idx_k_ln_b": idx_k_ln_b, "w_out": w_out, "norm2_g": norm2_g, "router_w": router_w,
            "router_b": router_b, "exp_w_gate": exp_w_gate, "exp_w_up": exp_w_up,
            "exp_w_down": exp_w_down, "sh_w_gate": sh_w_gate, "sh_w_up": sh_w_up,
            "sh_w_down": sh_w_down, "final_g": final_g}


def reference(x, c, ada_w, ada_b, norm1_g, w_in, conv_w, a_log, dt_bias, dn_norm_g, kv_norm_g,
              w_uk, w_uv, idx_k_ln_g, idx_k_ln_b, w_out, norm2_g, router_w, router_b,
              exp_w_gate, exp_w_up, exp_w_down, sh_w_gate, sh_w_up, sh_w_down, final_g):
    T = x.shape[1]
    topk = min(IDX_TOPK_MAX, T // 4)
    cond = jax.nn.silu(c)
    for l in range(DEPTH):
        mod = (cond @ ada_w[l] + ada_b[l])[:, None, :]
        sh1, sc1, gt1, sh2, sc2, gt2 = jnp.split(mod, 6, axis=-1)
        h = rms_norm(x, norm1_g[l]) * (1 + sc1) + sh1
        mix = token_mixers(h, w_in[l], conv_w[l], a_log[l], dt_bias[l], dn_norm_g[l], kv_norm_g[l],
                           w_uk[l], w_uv[l], idx_k_ln_g[l], idx_k_ln_b[l], topk)
        x = x + gt1 * (mix @ w_out[l])
        h = rms_norm(x, norm2_g[l]) * (1 + sc2) + sh2
        x = x + gt2 * moe_ffn(h, router_w[l], router_b[l], exp_w_gate[l], exp_w_up[l], exp_w_down[l],
                              sh_w_gate[l], sh_w_up[l], sh_w_down[l])
    return rms_norm(x, final_g)
```

```python
import functools

import jax
import jax.numpy as jnp
from jax import lax
from jax.experimental import pallas as pl
from jax.experimental.pallas import tpu as pltpu

F32 = jnp.float32
BF16 = jnp.bfloat16
I32 = jnp.int32
HIGHEST = lax.Precision.HIGHEST

EPS = 1e-6
CHUNK = 64
DN_HEADS = 4
DN_DK = 128
DN_DV = 128
CONV_K = 4
SA_HEADS = 4
SA_DQK = 128
SA_DV = 128
KV_RANK = 256
IDX_HEADS = 4
IDX_DIM = 64
IDX_TOPK_MAX = 256
SM_SCALE = SA_DQK ** -0.5
IDX_W_SCALE = (IDX_HEADS * IDX_DIM) ** -0.5
N_EXPERTS = 64
TOP_K = 8
N_GROUPS = 8
TOPK_GROUPS = 4
D_EXPERT = 256
ROUTED_SCALE = 2.5
GATE_W = 128

DN_QK = DN_HEADS * DN_DK
DN_V = DN_HEADS * DN_DV
CONV_DIM = 2 * DN_QK + DN_V
SA_Q = SA_HEADS * SA_DQK
IDX_Q = IDX_HEADS * IDX_DIM

MISC_W = 128
M_KIX = 0
M_BETA = IDX_DIM
M_A = M_BETA + DN_HEADS
M_WIX = M_A + DN_HEADS

QBLOCK = 128
KEY_TILE = 512
INT_MIN = -2 ** 31
NEG_BIG = -1e30
VMEM_LIMIT = 56 * 1024 * 1024


def _nt_dot(a, b, precision=None):
    return lax.dot_general(a, b, (((1,), (1,)), ((), ())), preferred_element_type=F32,
                           precision=precision)


def _dot(a, b, precision=None):
    return jnp.dot(a, b, preferred_element_type=F32, precision=precision)


def _silu(x):
    return x * jax.nn.sigmoid(x)


def _softplus(x):
    return jnp.maximum(x, 0.0) + jnp.log(1.0 + jnp.exp(-jnp.abs(x)))


def _ada_kernel(c_ref, w_ref, b_ref, o_ref):
    cond = _silu(c_ref[...])
    o_ref[...] = _dot(cond, w_ref[...], HIGHEST) + b_ref[...]


def _ada(c_pad, ada_w, ada_b):
    rows, d = c_pad.shape
    n_out = ada_w.shape[1]
    return pl.pallas_call(
        _ada_kernel,
        grid=(n_out // d,),
        in_specs=[pl.BlockSpec((rows, d), lambda j: (0, 0)),
                  pl.BlockSpec((d, d), lambda j: (0, j)),
                  pl.BlockSpec((1, d), lambda j: (0, j))],
        out_specs=pl.BlockSpec((rows, d), lambda j: (0, j)),
        out_shape=jax.ShapeDtypeStruct((rows, n_out), F32),
        compiler_params=pltpu.CompilerParams(vmem_limit_bytes=VMEM_LIMIT),
        name="ada",
    )(c_pad, ada_w, ada_b)


def _inproj_kernel(x_ref, sc_ref, sh_ref, g1_ref, wc_ref, wz_ref, wq_ref, wkv_ref, wqi_ref, wm_ref,
                   convw_ref, ukt_ref, kvg_ref, lng_ref, lnb_ref, alog_ref, dtb_ref, tri_ref,
                   q_ref, k_ref, v_ref, z_ref, qlat_ref, ckv_ref, qix_ref, kix_ref, misc_ref,
                   conv_buf):
    tm = x_ref.shape[1]
    i = pl.program_id(1)

    x = x_ref[0]
    h = x * lax.rsqrt(jnp.mean(x * x, axis=-1, keepdims=True) + EPS) * g1_ref[...]
    h = h * (1.0 + sc_ref[0]) + sh_ref[0]
    hb = h.astype(BF16)

    @pl.when(i == 0)
    def _():
        conv_buf[0:8, :] = jnp.zeros((8, CONV_DIM), F32)

    conv_buf[8:8 + tm, :] = _dot(hb, wc_ref[...])
    for grp, dst in ((0, q_ref), (1, k_ref), (2, v_ref)):
        cols = slice(grp * DN_QK, (grp + 1) * DN_QK)
        y = jnp.zeros((tm, DN_QK), F32)
        for j in range(CONV_K):
            y = y + convw_ref[j:j + 1, cols] * conv_buf[8 - (CONV_K - 1) + j:8 - (CONV_K - 1) + j + tm, cols]
        y = _silu(y)
        if grp < 2:
            outs = []
            for hd in range(DN_HEADS):
                yh = y[:, hd * DN_DK:(hd + 1) * DN_DK]
                yh = yh * lax.rsqrt(jnp.sum(yh * yh, axis=-1, keepdims=True) + EPS)
                if grp == 0:
                    yh = yh * (DN_DK ** -0.5)
                outs.append(yh)
            y = jnp.concatenate(outs, axis=-1)
        dst[0] = y
    conv_buf[0:8, :] = conv_buf[tm:tm + 8, :]

    z_ref[0] = _dot(hb, wz_ref[...])

    q_sa = _dot(hb, wq_ref[...]).astype(BF16)
    for hd in range(SA_HEADS):
        ql = _dot(q_sa[:, hd * SA_DQK:(hd + 1) * SA_DQK], ukt_ref[hd]) * SM_SCALE
        qlat_ref[0, :, hd * KV_RANK:(hd + 1) * KV_RANK] = ql.astype(BF16)

    ckv = _dot(hb, wkv_ref[...])
    ckv = ckv * lax.rsqrt(jnp.mean(ckv * ckv, axis=-1, keepdims=True) + EPS) * kvg_ref[...]
    ckv_ref[0] = ckv.astype(BF16)

    qix_ref[0] = _dot(hb, wqi_ref[...]).astype(BF16)

    m = _dot(hb, wm_ref[...])
    lane = lax.broadcasted_iota(I32, (tm, MISC_W), 1)
    is_k = lane < IDX_DIM
    mu = jnp.sum(jnp.where(is_k, m, 0.0), axis=-1, keepdims=True) * (1.0 / IDX_DIM)
    kc = jnp.where(is_k, m - mu, 0.0)
    var = jnp.sum(kc * kc, axis=-1, keepdims=True) * (1.0 / IDX_DIM)
    kn = kc * lax.rsqrt(var + EPS) * lng_ref[...] + lnb_ref[...]
    kix_ref[0] = kn[:, :IDX_DIM].astype(BF16)

    beta = jax.nn.sigmoid(m)
    g = -jnp.exp(alog_ref[...]) * _softplus(m + dtb_ref[...])
    is_a = (lane >= M_A) & (lane < M_A + DN_HEADS)
    g = jnp.where(is_a, g, 0.0)
    gc = _dot(tri_ref[...], g, HIGHEST)
    is_b = (lane >= M_BETA) & (lane < M_BETA + DN_HEADS)
    is_w = (lane >= M_WIX) & (lane < M_WIX + IDX_HEADS)
    misc_ref[0] = jnp.where(is_b, beta, jnp.where(is_a, gc, jnp.where(is_w, m * IDX_W_SCALE, 0.0)))


def _inproj(x, sc1, sh1, g1, wc, wz, wq, wkv, wqi, wm, conv_w, ukt, kvg, lng, lnb, alog, dtb, tri, tm):
    B, T, D = x.shape
    nt = T // tm

    def full(a):
        nd = a.ndim
        return pl.BlockSpec(a.shape, lambda b, i, _n=nd: (0,) * _n)

    def rows(w):
        return pl.BlockSpec((1, tm, w), lambda b, i: (b, i, 0))

    per_b = pl.BlockSpec((1, 1, D), lambda b, i: (b, 0, 0))
    outs = [(DN_QK, F32), (DN_QK, F32), (DN_V, F32), (DN_V, F32), (SA_HEADS * KV_RANK, BF16),
            (KV_RANK, BF16), (IDX_Q, BF16), (IDX_DIM, BF16), (MISC_W, F32)]
    return pl.pallas_call(
        _inproj_kernel,
        grid=(B, nt),
        in_specs=[rows(D), per_b, per_b, full(g1), full(wc), full(wz), full(wq), full(wkv), full(wqi),
                  full(wm), full(conv_w), full(ukt), full(kvg), full(lng), full(lnb), full(alog),
                  full(dtb), full(tri)],
        out_specs=[rows(w) for w, _ in outs],
        out_shape=[jax.ShapeDtypeStruct((B, T, w), dt) for w, dt in outs],
        scratch_shapes=[pltpu.VMEM((tm + 8, CONV_DIM), F32)],
        compiler_params=pltpu.CompilerParams(dimension_semantics=("arbitrary", "arbitrary"),
                                             vmem_limit_bytes=VMEM_LIMIT),
        name="inproj",
    )(x, sc1, sh1, g1, wc, wz, wq, wkv, wqi, wm, conv_w, ukt, kvg, lng, lnb, alog, dtb, tri)


def _deltanet_kernel(q_ref, k_ref, v_ref, z_ref, misc_ref, ng_ref, o_ref, s_ref):
    R = q_ref.shape[1]
    n_chunks = R // CHUNK

    @pl.when(pl.program_id(1) == 0)
    def _():
        s_ref[...] = jnp.zeros(s_ref.shape, F32)

    misc = misc_ref[0]
    misc_t = misc.T
    row = lax.broadcasted_iota(I32, (R, R), 0)
    col = lax.broadcasted_iota(I32, (R, R), 1)
    same = (row // CHUNK) == (col // CHUNK)
    lower = same & (row >= col)
    strict = same & (row > col)
    eye = (row == col).astype(F32)

    def mm(a, b):
        return _dot(a.astype(BF16), b.astype(BF16))

    def mm3(a, b):
        ah = a.astype(BF16)
        bh = b.astype(BF16)
        al = (a - ah.astype(F32)).astype(BF16)
        bl = (b - bh.astype(F32)).astype(BF16)
        return _dot(ah, bh) + (_dot(ah, bl) + _dot(al, bh))

    for hd in range(DN_HEADS):
        cs = slice(hd * DN_DK, (hd + 1) * DN_DK)
        qh = q_ref[0, :, cs]
        kh = k_ref[0, :, cs]
        vh = v_ref[0, :, cs]
        beta = misc[:, M_BETA + hd:M_BETA + hd + 1]
        gc_c = misc[:, M_A + hd:M_A + hd + 1]
        gc_r = misc_t[M_A + hd:M_A + hd + 1, :]
        decay = jnp.where(lower, jnp.exp(jnp.where(lower, gc_c - gc_r, 0.0)), 0.0)
        kb = kh * beta
        khb = kh.astype(BF16)
        a = jnp.where(strict, _nt_dot(kb.astype(BF16), khb) * decay, 0.0)
        p = eye - a
        xp = mm3(a, a)
        n_sq = 1
        while True:
            p = p + mm3(p, xp)
            n_sq *= 2
            if n_sq * 2 >= CHUNK:
                break
            xp = mm3(xp, xp)
        eg = jnp.exp(gc_c)
        rhs = jnp.concatenate([vh * beta, kb * eg], axis=-1)
        sol = mm(p, rhs)
        u = sol[:, :DN_DV]
        w = sol[:, DN_DV:]
        qk = jnp.where(lower, _nt_dot(qh.astype(BF16), khb) * decay, 0.0)
        q_dec = qh * eg

        s = s_ref[hd]
        o_parts = []
        for c in range(n_chunks):
            rs = slice(c * CHUNK, (c + 1) * CHUNK)
            gl = gc_c[(c + 1) * CHUNK - 1:(c + 1) * CHUNK, :]
            k_dec = kh[rs] * jnp.exp(gl - gc_c[rs])
            v_new = u[rs] - mm(w[rs], s)
            o_c = mm(q_dec[rs], s) + mm(qk[rs, rs], v_new)
            s = s * jnp.exp(gl) + mm(k_dec.T, v_new)
            o_parts.append(o_c)
        s_ref[hd] = s
        o = jnp.concatenate(o_parts, axis=0)
        o = o * lax.rsqrt(jnp.mean(o * o, axis=-1, keepdims=True) + EPS) * ng_ref[...]
        o_ref[0, :, cs] = (o * _silu(z_ref[0, :, cs])).astype(BF16)


def _deltanet(q, k, v, z, misc, ng, R):
    B, T, _ = q.shape

    def rows(w):
        return pl.BlockSpec((1, R, w), lambda b, i: (b, i, 0))

    return pl.pallas_call(
        _deltanet_kernel,
        grid=(B, T // R),
        in_specs=[rows(DN_QK), rows(DN_QK), rows(DN_V), rows(DN_V), rows(MISC_W),
                  pl.BlockSpec((1, DN_DV), lambda b, i: (0, 0))],
        out_specs=rows(DN_V),
        out_shape=jax.ShapeDtypeStruct((B, T, DN_V), BF16),
        scratch_shapes=[pltpu.VMEM((DN_HEADS, DN_DK, DN_DV), F32)],
        compiler_params=pltpu.CompilerParams(dimension_semantics=("arbitrary", "arbitrary"),
                                             vmem_limit_bytes=VMEM_LIMIT),
        name="deltanet",
    )(q, k, v, z, misc, ng)


def _dsa_kernel(qix_ref, misc_ref, kix_ref, qlat_ref, ckv_ref, o_ref, keys_ref, m_ref, l_ref, acc_ref,
                *, topk, pos_bits):
    i = pl.program_id(1)
    QB = QBLOCK
    KT = KEY_TILE
    n_kt = (i * QB + QB + KT - 1) // KT

    rowi = lax.broadcasted_iota(I32, (QB, KT), 0)
    coli = lax.broadcasted_iota(I32, (QB, KT), 1)
    limit = i * QB + (rowi // CHUNK + 1) * CHUNK

    misc = misc_ref[0]
    qix = qix_ref[0]

    def score_body(kt, carry):
        k0 = pl.multiple_of(kt * KT, KT)
        kx = kix_ref[0, pl.ds(k0, KT), :]
        sc = jnp.zeros((QB, KT), F32)
        for hd in range(IDX_HEADS):
            rel = jnp.maximum(_nt_dot(qix[:, hd * IDX_DIM:(hd + 1) * IDX_DIM], kx), 0.0)
            sc = sc + misc[:, M_WIX + hd:M_WIX + hd + 1] * rel
        sc = jnp.where(sc == 0.0, 0.0, sc)
        bits = pltpu.bitcast(sc, I32)
        key = jnp.where(bits < 0, bits ^ 0x7FFFFFFF, bits)
        key = jnp.where(k0 + coli < limit, key, INT_MIN)
        keys_ref[:, pl.ds(k0, KT)] = key
        return carry

    lax.fori_loop(0, n_kt, score_body, 0)

    def count(pred):
        def body(kt, acc):
            k0 = pl.multiple_of(kt * KT, KT)
            hit = pred(keys_ref[:, pl.ds(k0, KT)], k0).astype(I32)
            for j in range(KT // 128):
                acc = acc + hit[:, j * 128:(j + 1) * 128]
            return acc
        acc = lax.fori_loop(0, n_kt, body, jnp.zeros((QB, 128), I32))
        return jnp.sum(acc, axis=1, keepdims=True)

    def bit_body(b, tau):
        cand = tau + jnp.left_shift(jnp.int32(1), 31 - b)
        cnt = count(lambda kk, k0: kk >= cand)
        return jnp.where(cnt >= topk, cand, tau)

    tau = lax.fori_loop(0, 32, bit_body, jnp.full((QB, 1), INT_MIN, I32))

    n_gt = count(lambda kk, k0: kk > tau)
    n_ge = count(lambda kk, k0: kk >= tau)
    need = topk - n_gt
    any_tie = jnp.max(n_ge) > topk

    def tie_search():
        def pos_body(b, qpos):
            cand = qpos + jnp.left_shift(jnp.int32(1), pos_bits - 1 - b)
            cnt = count(lambda kk, k0: (kk == tau) & (k0 + coli < cand))
            return jnp.where(cnt < need, cand, qpos)
        return lax.fori_loop(0, pos_bits, pos_body, jnp.zeros((QB, 1), I32)) + 1

    pstar = lax.cond(any_tie, tie_search, lambda: jnp.full((QB, 1), 2 ** pos_bits, I32))

    m_ref[...] = jnp.full(m_ref.shape, NEG_BIG, F32)
    l_ref[...] = jnp.zeros(l_ref.shape, F32)
    acc_ref[...] = jnp.zeros(acc_ref.shape, F32)

    def attn_body(kt, carry):
        k0 = pl.multiple_of(kt * KT, KT)
        kk = keys_ref[:, pl.ds(k0, KT)]
        sel = ((kk > tau) | ((kk == tau) & (k0 + coli < pstar))) & (kk > INT_MIN)
        kv = ckv_ref[0, pl.ds(k0, KT), :]
        for hd in range(SA_HEADS):
            s = _nt_dot(qlat_ref[0, :, hd * KV_RANK:(hd + 1) * KV_RANK], kv)
            s = jnp.where(sel, s, NEG_BIG)
            m_prev = m_ref[hd]
            m_new = jnp.maximum(m_prev, jnp.max(s, axis=-1, keepdims=True))
            alpha = jnp.exp(m_prev - m_new)
            p = jnp.exp(s - m_new)
            l_ref[hd] = alpha * l_ref[hd] + jnp.sum(p, axis=-1, keepdims=True)
            acc_ref[hd] = alpha * acc_ref[hd] + _dot(p.astype(BF16), kv)
            m_ref[hd] = m_new
        return carry

    lax.fori_loop(0, n_kt, attn_body, 0)

    for hd in range(SA_HEADS):
        o_ref[0, :, hd * KV_RANK:(hd + 1) * KV_RANK] = (acc_ref[hd] / l_ref[hd]).astype(BF16)


def _dsa(qix, misc, kix, qlat, ckv, topk):
    B, T, _ = qix.shape
    t_pad = -(-T // KEY_TILE) * KEY_TILE
    pos_bits = max(1, (t_pad).bit_length())

    def rows(w):
        return pl.BlockSpec((1, QBLOCK, w), lambda b, i: (b, i, 0))

    def per_b(w):
        return pl.BlockSpec((1, T, w), lambda b, i: (b, 0, 0))

    return pl.pallas_call(
        functools.partial(_dsa_kernel, topk=topk, pos_bits=pos_bits),
        grid=(B, T // QBLOCK),
        in_specs=[rows(IDX_Q), rows(MISC_W), per_b(IDX_DIM), rows(SA_HEADS * KV_RANK), per_b(KV_RANK)],
        out_specs=rows(SA_HEADS * KV_RANK),
        out_shape=jax.ShapeDtypeStruct((B, T, SA_HEADS * KV_RANK), BF16),
        scratch_shapes=[pltpu.VMEM((QBLOCK, t_pad), I32),
                        pltpu.VMEM((SA_HEADS, QBLOCK, 1), F32),
                        pltpu.VMEM((SA_HEADS, QBLOCK, 1), F32),
                        pltpu.VMEM((SA_HEADS, QBLOCK, KV_RANK), F32)],
        compiler_params=pltpu.CompilerParams(dimension_semantics=("arbitrary", "arbitrary"),
                                             vmem_limit_bytes=VMEM_LIMIT),
        name="dsa",
    )(qix, misc, kix, qlat, ckv)


def _first_max(v, idx, axis):
    m = jnp.max(v, axis=axis, keepdims=True)
    big = jnp.int32(2 ** 30)
    first = jnp.min(jnp.where(v == m, idx, big), axis=axis, keepdims=True)
    return m, idx == first


def _outproj_kernel(x_ref, odn_ref, olat_ref, uv_ref, wo_ref, gt_ref, sc_ref, sh_ref, g2_ref, rwt_ref,
                    rb_ref, x1_ref, h2_ref, gates_ref):
    tm = x_ref.shape[1]
    parts = [odn_ref[0]]
    for hd in range(SA_HEADS):
        parts.append(_dot(olat_ref[0, :, hd * KV_RANK:(hd + 1) * KV_RANK], uv_ref[hd]).astype(BF16))
    mix = jnp.concatenate(parts, axis=-1)
    x1 = x_ref[0] + gt_ref[0] * _dot(mix, wo_ref[...])
    x1_ref[0] = x1
    h2 = x1 * lax.rsqrt(jnp.mean(x1 * x1, axis=-1, keepdims=True) + EPS) * g2_ref[...]
    h2 = h2 * (1.0 + sc_ref[0]) + sh_ref[0]
    h2_ref[0] = h2.astype(BF16)

    per_g = N_EXPERTS // N_GROUPS
    s = jax.nn.sigmoid(_nt_dot(rwt_ref[...], h2, HIGHEST))
    choice = s + rb_ref[...]
    ig = lax.broadcasted_iota(I32, (per_g, tm), 0)
    gscore = []
    for gidx in range(N_GROUPS):
        cg = choice[gidx * per_g:(gidx + 1) * per_g]
        m1, hot1 = _first_max(cg, ig, 0)
        gscore.append(m1 + jnp.max(jnp.where(hot1, -jnp.inf, cg), axis=0, keepdims=True))
    gsel = [jnp.zeros((1, tm), jnp.bool_) for _ in range(N_GROUPS)]
    for _ in range(TOPK_GROUPS):
        best = functools.reduce(jnp.maximum, gscore)
        found = jnp.zeros((1, tm), jnp.bool_)
        for gidx in range(N_GROUPS):
            hot = (gscore[gidx] == best) & jnp.logical_not(found)
            found = found | hot
            gsel[gidx] = gsel[gidx] | hot
            gscore[gidx] = jnp.where(hot, -jnp.inf, gscore[gidx])
    masked = jnp.concatenate(
        [jnp.where(gsel[gidx], choice[gidx * per_g:(gidx + 1) * per_g], -jnp.inf) for gidx in range(N_GROUPS)],
        axis=0)
    ei = lax.broadcasted_iota(I32, masked.shape, 0)
    gate = jnp.zeros(masked.shape, F32)
    for _ in range(TOP_K):
        _, hot = _first_max(masked, ei, 0)
        gate = jnp.where(hot, s, gate)
        masked = jnp.where(hot, -jnp.inf, masked)
    gate = gate / jnp.sum(gate, axis=0, keepdims=True) * ROUTED_SCALE
    gate = jnp.concatenate([gate, jnp.zeros((GATE_W - N_EXPERTS, tm), F32)], axis=0)
    gates_ref[0] = gate.T


def _outproj(x, odn, olat, uv, wo, gt1, sc2, sh2, g2, rwt, rb, tm):
    B, T, D = x.shape

    def full(a):
        nd = a.ndim
        return pl.BlockSpec(a.shape, lambda b, i, _n=nd: (0,) * _n)

    def rows(w):
        return pl.BlockSpec((1, tm, w), lambda b, i: (b, i, 0))

    per_b = pl.BlockSpec((1, 1, D), lambda b, i: (b, 0, 0))
    return pl.pallas_call(
        _outproj_kernel,
        grid=(B, T // tm),
        in_specs=[rows(D), rows(DN_V), rows(SA_HEADS * KV_RANK), full(uv), full(wo), per_b, per_b, per_b,
                  full(g2), full(rwt), full(rb)],
        out_specs=[rows(D), rows(D), rows(GATE_W)],
        out_shape=[jax.ShapeDtypeStruct((B, T, D), F32), jax.ShapeDtypeStruct((B, T, D), BF16),
                   jax.ShapeDtypeStruct((B, T, GATE_W), F32)],
        compiler_params=pltpu.CompilerParams(dimension_semantics=("arbitrary", "arbitrary"),
                                             vmem_limit_bytes=VMEM_LIMIT),
        name="outproj",
    )(x, odn, olat, uv, wo, gt1, sc2, sh2, g2, rwt, rb)


def _moe_kernel(h_ref, gates_ref, wg_ref, wu_ref, wd_ref, sg_ref, su_ref, sd_ref, x1_ref, gt_ref, fg_ref,
                o_ref, acc_ref, *, group, final_norm):
    g = pl.program_id(1)
    hb = h_ref[0]

    @pl.when(g == 0)
    def _():
        sh = (_silu(_dot(hb, sg_ref[...])) * _dot(hb, su_ref[...])).astype(BF16)
        acc_ref[...] = _dot(sh, sd_ref[...])

    width = group * D_EXPERT
    er = lax.broadcasted_iota(I32, (GATE_W, width), 0)
    ec = lax.broadcasted_iota(I32, (GATE_W, width), 1)
    expand = (er == g * group + ec // D_EXPERT).astype(F32)
    gexp = _dot(gates_ref[0], expand, HIGHEST)
    hs = []
    for e in range(group):
        a = _silu(_dot(hb, wg_ref[e])) * _dot(hb, wu_ref[e])
        hs.append((a * gexp[:, e * D_EXPERT:(e + 1) * D_EXPERT]).astype(BF16))
    hcat = jnp.concatenate(hs, axis=-1)
    acc_ref[...] += _dot(hcat, wd_ref[...].reshape(width, wd_ref.shape[-1]))

    @pl.when(g == pl.num_programs(1) - 1)
    def _():
        y = x1_ref[0] + gt_ref[0] * acc_ref[...]
        if final_norm:
            y = y * lax.rsqrt(jnp.mean(y * y, axis=-1, keepdims=True) + EPS) * fg_ref[...]
        o_ref[0] = y


def _moe(h2, gates, wg, wu, wd, sg, su, sd, x1, gt2, fg, tm, group, final_norm):
    B, T, D = x1.shape
    nt = T // tm

    def full(a):
        nd = a.ndim
        return pl.BlockSpec(a.shape, lambda t, g, _n=nd: (0,) * _n)

    def rows(w):
        return pl.BlockSpec((1, tm, w), lambda t, g: (t // nt, t % nt, 0))

    return pl.pallas_call(
        functools.partial(_moe_kernel, group=group, final_norm=final_norm),
        grid=(B * nt, N_EXPERTS // group),
        in_specs=[rows(D), rows(GATE_W),
                  pl.BlockSpec((group, D, D_EXPERT), lambda t, g: (g, 0, 0)),
                  pl.BlockSpec((group, D, D_EXPERT), lambda t, g: (g, 0, 0)),
                  pl.BlockSpec((group, D_EXPERT, D), lambda t, g: (g, 0, 0)),
                  full(sg), full(su), full(sd), rows(D),
                  pl.BlockSpec((1, 1, D), lambda t, g: (t // nt, 0, 0)), full(fg)],
        out_specs=rows(D),
        out_shape=jax.ShapeDtypeStruct((B, T, D), F32),
        scratch_shapes=[pltpu.VMEM((tm, D), F32)],
        compiler_params=pltpu.CompilerParams(dimension_semantics=("arbitrary", "arbitrary"),
                                             vmem_limit_bytes=VMEM_LIMIT),
        name="moe",
    )(h2, gates, wg, wu, wd, sg, su, sd, x1, gt2, fg)


def _misc_lanes(vec, start):
    return jnp.zeros((1, MISC_W), F32).at[0, start:start + vec.shape[0]].set(vec.astype(F32))


def kernel(x, c, ada_w, ada_b, norm1_g, w_in, conv_w, a_log, dt_bias, dn_norm_g, kv_norm_g, w_uk, w_uv,
           idx_k_ln_g, idx_k_ln_b, w_out, norm2_g, router_w, router_b, exp_w_gate, exp_w_up, exp_w_down,
           sh_w_gate, sh_w_up, sh_w_down, final_g):
    B, T, D = x.shape
    depth = ada_w.shape[0]
    topk = min(IDX_TOPK_MAX, T // 4)
    tm = min(512, T)
    tm_moe = min(1024, T)
    r_dn = min(256, T)

    cond_in = jnp.zeros((8, D), F32).at[:B].set(c)
    pos = jnp.arange(tm)
    tri = ((pos[:, None] // CHUNK == pos[None, :] // CHUNK) & (pos[:, None] >= pos[None, :])).astype(F32)

    for l in range(depth):
        mod = _ada(cond_in, ada_w[l], ada_b[l][None, :])[:B]
        sh1, sc1, gt1, sh2, sc2, gt2 = [m[:, None, :] for m in jnp.split(mod, 6, axis=-1)]

        offs = [0]
        for s in (DN_QK, DN_QK, DN_V, DN_V, DN_HEADS, DN_HEADS, SA_Q, KV_RANK, IDX_Q, IDX_DIM, IDX_HEADS):
            offs.append(offs[-1] + s)
        w = w_in[l]
        wc = w[:, offs[0]:offs[3]].astype(BF16)
        wz = w[:, offs[3]:offs[4]].astype(BF16)
        wq = w[:, offs[6]:offs[7]].astype(BF16)
        wkv = w[:, offs[7]:offs[8]].astype(BF16)
        wqi = w[:, offs[8]:offs[9]].astype(BF16)
        wm = jnp.concatenate([w[:, offs[9]:offs[10]], w[:, offs[4]:offs[5]], w[:, offs[5]:offs[6]],
                              w[:, offs[10]:offs[11]],
                              jnp.zeros((D, MISC_W - IDX_DIM - 2 * DN_HEADS - IDX_HEADS), F32)],
                             axis=1).astype(BF16)
        ukt = jnp.swapaxes(w_uk[l], 1, 2).astype(BF16)

        q, k, v, z, qlat, ckv, qix, kix, misc = _inproj(
            x, sc1, sh1, norm1_g[l][None, :], wc, wz, wq, wkv, wqi, wm, conv_w[l], ukt,
            kv_norm_g[l][None, :], _misc_lanes(idx_k_ln_g[l], M_KIX), _misc_lanes(idx_k_ln_b[l], M_KIX),
            _misc_lanes(a_log[l], M_A), _misc_lanes(dt_bias[l], M_A), tri, tm)

        odn = _deltanet(q, k, v, z, misc, dn_norm_g[l][None, :], r_dn)
        olat = _dsa(qix, misc, kix, qlat, ckv, topk)

        x1, h2, gates = _outproj(x, odn, olat, w_uv[l].astype(BF16), w_out[l].astype(BF16), gt1, sc2, sh2,
                                 norm2_g[l][None, :], router_w[l].T, router_b[l][:, None], tm)

        x = _moe(h2, gates, exp_w_gate[l].astype(BF16), exp_w_up[l].astype(BF16),
                 exp_w_down[l].astype(BF16), sh_w_gate[l].astype(BF16), sh_w_up[l].astype(BF16),
                 sh_w_down[l].astype(BF16), x1, gt2, final_g[None, :], tm_moe, 4, l == depth - 1)
    return x
```

```python
import functools

import jax
import jax.numpy as jnp
from jax import lax
from jax.experimental import pallas as pl
from jax.experimental.pallas import tpu as pltpu

F32 = jnp.float32
BF16 = jnp.bfloat16
I32 = jnp.int32
HIGHEST = lax.Precision.HIGHEST

EPS = 1e-6
CHUNK = 64
DN_HEADS = 4
DN_DK = 128
DN_DV = 128
CONV_K = 4
SA_HEADS = 4
SA_DQK = 128
SA_DV = 128
KV_RANK = 256
IDX_HEADS = 4
IDX_DIM = 64
IDX_TOPK_MAX = 256
SM_SCALE = SA_DQK ** -0.5
LOG2E = 1.4426950408889634
IDX_W_SCALE = (IDX_HEADS * IDX_DIM) ** -0.5
N_EXPERTS = 64
TOP_K = 8
N_GROUPS = 8
TOPK_GROUPS = 4
D_EXPERT = 256
ROUTED_SCALE = 2.5
GATE_W = 128

DN_QK = DN_HEADS * DN_DK
DN_V = DN_HEADS * DN_DV
CONV_DIM = 2 * DN_QK + DN_V
SA_Q = SA_HEADS * SA_DQK
IDX_Q = IDX_HEADS * IDX_DIM

MISC_W = 128
M_KIX = 0
M_BETA = IDX_DIM
M_A = M_BETA + DN_HEADS
M_WIX = M_A + DN_HEADS

QBLOCK = 128
KEY_TILE = 512
INT_MIN = -2 ** 31
NEG_BIG = -1e30
VMEM_LIMIT = 56 * 1024 * 1024


def _nt_dot(a, b, precision=None):
    return lax.dot_general(a, b, (((1,), (1,)), ((), ())), preferred_element_type=F32,
                           precision=precision)


def _dot(a, b, precision=None):
    return jnp.dot(a, b, preferred_element_type=F32, precision=precision)


def _silu(x):
    return x * jax.nn.sigmoid(x)


def _softplus(x):
    return jnp.maximum(x, 0.0) + jnp.log(1.0 + jnp.exp(-jnp.abs(x)))


def _ada_kernel(c_ref, w_ref, b_ref, o_ref):
    cond = _silu(c_ref[...])
    o_ref[...] = _dot(cond, w_ref[...], HIGHEST) + b_ref[...]


def _ada(c_pad, ada_w, ada_b):
    rows, d = c_pad.shape
    n_out = ada_w.shape[1]
    return pl.pallas_call(
        _ada_kernel,
        grid=(n_out // d,),
        in_specs=[pl.BlockSpec((rows, d), lambda j: (0, 0)),
                  pl.BlockSpec((d, d), lambda j: (0, j)),
                  pl.BlockSpec((1, d), lambda j: (0, j))],
        out_specs=pl.BlockSpec((rows, d), lambda j: (0, j)),
        out_shape=jax.ShapeDtypeStruct((rows, n_out), F32),
        compiler_params=pltpu.CompilerParams(vmem_limit_bytes=VMEM_LIMIT),
        name="ada",
    )(c_pad, ada_w, ada_b)


def _inproj_kernel(x_ref, sc_ref, sh_ref, g1_ref, wc_ref, wz_ref, wq_ref, wkv_ref, wqi_ref, wm_ref,
                   convw_ref, ukt_ref, kvg_ref, lng_ref, lnb_ref, alog_ref, dtb_ref, tri_ref,
                   q_ref, k_ref, v_ref, z_ref, qlat_ref, ckv_ref, qix_ref, kix_ref, misc_ref,
                   conv_buf):
    tm = x_ref.shape[1]
    i = pl.program_id(1)

    x = x_ref[0]
    h = x * lax.rsqrt(jnp.mean(x * x, axis=-1, keepdims=True) + EPS) * g1_ref[...]
    h = h * (1.0 + sc_ref[0]) + sh_ref[0]
    hb = h.astype(BF16)

    @pl.when(i == 0)
    def _():
        conv_buf[0:8, :] = jnp.zeros((8, CONV_DIM), F32)

    conv_buf[8:8 + tm, :] = _dot(hb, wc_ref[...])
    for grp, dst in ((0, q_ref), (1, k_ref), (2, v_ref)):
        cols = slice(grp * DN_QK, (grp + 1) * DN_QK)
        y = jnp.zeros((tm, DN_QK), F32)
        for j in range(CONV_K):
            y = y + convw_ref[j:j + 1, cols] * conv_buf[8 - (CONV_K - 1) + j:8 - (CONV_K - 1) + j + tm, cols]
        y = _silu(y)
        if grp < 2:
            outs = []
            for hd in range(DN_HEADS):
                yh = y[:, hd * DN_DK:(hd + 1) * DN_DK]
                yh = yh * lax.rsqrt(jnp.sum(yh * yh, axis=-1, keepdims=True) + EPS)
                if grp == 0:
                    yh = yh * (DN_DK ** -0.5)
                outs.append(yh)
            y = jnp.concatenate(outs, axis=-1)
        dst[0] = y
    conv_buf[0:8, :] = conv_buf[tm:tm + 8, :]

    z_ref[0] = _dot(hb, wz_ref[...])

    q_sa = _dot(hb, wq_ref[...]).astype(BF16)
    for hd in range(SA_HEADS):
        ql = _dot(q_sa[:, hd * SA_DQK:(hd + 1) * SA_DQK], ukt_ref[hd]) * (SM_SCALE * LOG2E)
        qlat_ref[0, hd] = ql.astype(BF16)

    ckv = _dot(hb, wkv_ref[...])
    ckv = ckv * lax.rsqrt(jnp.mean(ckv * ckv, axis=-1, keepdims=True) + EPS) * kvg_ref[...]
    ckv_ref[0] = ckv.astype(BF16)

    q_ix = _dot(hb, wqi_ref[...]).astype(BF16)
    for hd in range(IDX_HEADS):
        qix_ref[0, hd] = q_ix[:, hd * IDX_DIM:(hd + 1) * IDX_DIM]

    m = _dot(hb, wm_ref[...])
    lane = lax.broadcasted_iota(I32, (tm, MISC_W), 1)
    is_k = lane < IDX_DIM
    mu = jnp.sum(jnp.where(is_k, m, 0.0), axis=-1, keepdims=True) * (1.0 / IDX_DIM)
    kc = jnp.where(is_k, m - mu, 0.0)
    var = jnp.sum(kc * kc, axis=-1, keepdims=True) * (1.0 / IDX_DIM)
    kn = kc * lax.rsqrt(var + EPS) * lng_ref[...] + lnb_ref[...]
    kix_ref[0] = kn[:, :IDX_DIM].astype(BF16)

    beta = jax.nn.sigmoid(m)
    g = -jnp.exp(alog_ref[...]) * _softplus(m + dtb_ref[...])
    is_a = (lane >= M_A) & (lane < M_A + DN_HEADS)
    g = jnp.where(is_a, g, 0.0)
    gc = _dot(tri_ref[...], g, HIGHEST)
    is_b = (lane >= M_BETA) & (lane < M_BETA + DN_HEADS)
    is_w = (lane >= M_WIX) & (lane < M_WIX + IDX_HEADS)
    misc_ref[0] = jnp.where(is_b, beta, jnp.where(is_a, gc, jnp.where(is_w, m * IDX_W_SCALE, 0.0)))


def _inproj(x, sc1, sh1, g1, wc, wz, wq, wkv, wqi, wm, conv_w, ukt, kvg, lng, lnb, alog, dtb, tri, tm):
    B, T, D = x.shape
    nt = T // tm

    def full(a):
        nd = a.ndim
        return pl.BlockSpec(a.shape, lambda b, i, _n=nd: (0,) * _n)

    def rows(w):
        return pl.BlockSpec((1, tm, w), lambda b, i: (b, i, 0))

    per_b = pl.BlockSpec((1, 1, D), lambda b, i: (b, 0, 0))
    def head_rows(h, w):
        return pl.BlockSpec((1, h, tm, w), lambda b, i: (b, 0, i, 0))

    outs = [(None, DN_QK, F32), (None, DN_QK, F32), (None, DN_V, F32), (None, DN_V, F32),
            (SA_HEADS, KV_RANK, BF16), (None, KV_RANK, BF16), (IDX_HEADS, IDX_DIM, BF16),
            (None, IDX_DIM, BF16), (None, MISC_W, F32)]
    return pl.pallas_call(
        _inproj_kernel,
        grid=(B, nt),
        in_specs=[rows(D), per_b, per_b, full(g1), full(wc), full(wz), full(wq), full(wkv), full(wqi),
                  full(wm), full(conv_w), full(ukt), full(kvg), full(lng), full(lnb), full(alog),
                  full(dtb), full(tri)],
        out_specs=[rows(w) if h is None else head_rows(h, w) for h, w, _ in outs],
        out_shape=[jax.ShapeDtypeStruct((B, T, w) if h is None else (B, h, T, w), dt) for h, w, dt in outs],
        scratch_shapes=[pltpu.VMEM((tm + 8, CONV_DIM), F32)],
        compiler_params=pltpu.CompilerParams(dimension_semantics=("arbitrary", "arbitrary"),
                                             vmem_limit_bytes=VMEM_LIMIT),
        name="inproj",
    )(x, sc1, sh1, g1, wc, wz, wq, wkv, wqi, wm, conv_w, ukt, kvg, lng, lnb, alog, dtb, tri)


def _deltanet_kernel(q_ref, k_ref, v_ref, z_ref, misc_ref, ng_ref, o_ref, s_ref):
    R = q_ref.shape[1]
    n_chunks = R // CHUNK

    @pl.when(pl.program_id(1) == 0)
    def _():
        s_ref[...] = jnp.zeros(s_ref.shape, F32)

    misc = misc_ref[0]
    misc_t = misc.T
    row = lax.broadcasted_iota(I32, (R, R), 0)
    col = lax.broadcasted_iota(I32, (R, R), 1)
    same = (row // CHUNK) == (col // CHUNK)
    lower = same & (row >= col)
    strict = same & (row > col)
    eye = (row == col).astype(F32)

    def mm(a, b):
        return _dot(a.astype(BF16), b.astype(BF16))

    def mm3(a, b):
        ah = a.astype(BF16)
        bh = b.astype(BF16)
        al = (a - ah.astype(F32)).astype(BF16)
        bl = (b - bh.astype(F32)).astype(BF16)
        return _dot(ah, bh) + (_dot(ah, bl) + _dot(al, bh))

    for hd in range(DN_HEADS):
        cs = slice(hd * DN_DK, (hd + 1) * DN_DK)
        qh = q_ref[0, :, cs]
        kh = k_ref[0, :, cs]
        vh = v_ref[0, :, cs]
        beta = misc[:, M_BETA + hd:M_BETA + hd + 1]
        gc_c = misc[:, M_A + hd:M_A + hd + 1]
        gc_r = misc_t[M_A + hd:M_A + hd + 1, :]
        decay = jnp.where(lower, jnp.exp(jnp.where(lower, gc_c - gc_r, 0.0)), 0.0)
        kb = kh * beta
        khb = kh.astype(BF16)
        a = jnp.where(strict, _nt_dot(kb.astype(BF16), khb) * decay, 0.0)
        p = eye - a
        xp = mm3(a, a)
        n_sq = 1
        while True:
            p = p + mm3(p, xp)
            n_sq *= 2
            if n_sq * 2 >= CHUNK:
                break
            xp = mm3(xp, xp)
        eg = jnp.exp(gc_c)
        rhs = jnp.concatenate([vh * beta, kb * eg], axis=-1)
        sol = mm(p, rhs)
        u = sol[:, :DN_DV]
        w = sol[:, DN_DV:]
        qk = jnp.where(lower, _nt_dot(qh.astype(BF16), khb) * decay, 0.0)
        q_dec = qh * eg

        s = s_ref[hd]
        o_parts = []
        for c in range(n_chunks):
            rs = slice(c * CHUNK, (c + 1) * CHUNK)
            gl = gc_c[(c + 1) * CHUNK - 1:(c + 1) * CHUNK, :]
            k_dec = kh[rs] * jnp.exp(gl - gc_c[rs])
            v_new = u[rs] - mm(w[rs], s)
            o_c = mm(q_dec[rs], s) + mm(qk[rs, rs], v_new)
            s = s * jnp.exp(gl) + mm(k_dec.T, v_new)
            o_parts.append(o_c)
        s_ref[hd] = s
        o = jnp.concatenate(o_parts, axis=0)
        o = o * lax.rsqrt(jnp.mean(o * o, axis=-1, keepdims=True) + EPS) * ng_ref[...]
        o_ref[0, :, cs] = (o * _silu(z_ref[0, :, cs])).astype(BF16)


def _deltanet(q, k, v, z, misc, ng, R):
    B, T, _ = q.shape

    def rows(w):
        return pl.BlockSpec((1, R, w), lambda b, i: (b, i, 0))

    return pl.pallas_call(
        _deltanet_kernel,
        grid=(B, T // R),
        in_specs=[rows(DN_QK), rows(DN_QK), rows(DN_V), rows(DN_V), rows(MISC_W),
                  pl.BlockSpec((1, DN_DV), lambda b, i: (0, 0))],
        out_specs=rows(DN_V),
        out_shape=jax.ShapeDtypeStruct((B, T, DN_V), BF16),
        scratch_shapes=[pltpu.VMEM((DN_HEADS, DN_DK, DN_DV), F32)],
        compiler_params=pltpu.CompilerParams(dimension_semantics=("arbitrary", "arbitrary"),
                                             vmem_limit_bytes=VMEM_LIMIT),
        name="deltanet",
    )(q, k, v, z, misc, ng)


def _dsa_kernel(qix_ref, misc_ref, kix_ref, qlat_ref, ckv_ref, o_ref, keys_ref, bias_ref, mx_ref, l_ref,
                acc_ref, *, topk, pos_bits):
    i = pl.program_id(1)
    QB = QBLOCK
    KT = KEY_TILE
    n_kt = (i * QB + QB + KT - 1) // KT

    rowi = lax.broadcasted_iota(I32, (QB, KT), 0)
    coli = lax.broadcasted_iota(I32, (QB, KT), 1)
    limit = i * QB + (rowi // CHUNK + 1) * CHUNK

    misc = misc_ref[0]
    qix = qix_ref[0].reshape(IDX_HEADS * QB, IDX_DIM)
    q_st = qlat_ref[0].reshape(SA_HEADS * QB, KV_RANK)

    def score_body(kt, carry):
        k0 = pl.multiple_of(kt * KT, KT)
        kx = kix_ref[0, pl.ds(k0, KT), :]
        rel = jnp.maximum(_nt_dot(qix, kx), 0.0)
        sc = jnp.zeros((QB, KT), F32)
        for hd in range(IDX_HEADS):
            sc = sc + misc[:, M_WIX + hd:M_WIX + hd + 1] * rel[hd * QB:(hd + 1) * QB]
        sc = jnp.where(sc == 0.0, 0.0, sc)
        bits = pltpu.bitcast(sc, I32)
        key = jnp.where(bits < 0, bits ^ 0x7FFFFFFF, bits)
        key = jnp.where(k0 + coli < limit, key, INT_MIN)
        keys_ref[:, pl.ds(k0, KT)] = key
        return carry

    lax.fori_loop(0, n_kt, score_body, 0)

    def count(pred):
        def body(kt, acc):
            k0 = pl.multiple_of(kt * KT, KT)
            hit = pred(keys_ref[:, pl.ds(k0, KT)], k0).astype(I32)
            for j in range(KT // 128):
                acc = acc + hit[:, j * 128:(j + 1) * 128]
            return acc
        acc = lax.fori_loop(0, n_kt, body, jnp.zeros((QB, 128), I32))
        return jnp.sum(acc, axis=1, keepdims=True)

    def bit_body(b, tau):
        cand = tau + jnp.left_shift(jnp.int32(1), 31 - b)
        cnt = count(lambda kk, k0: kk >= cand)
        return jnp.where(cnt >= topk, cand, tau)

    tau = lax.fori_loop(0, 32, bit_body, jnp.full((QB, 1), INT_MIN, I32))

    n_gt = count(lambda kk, k0: kk > tau)
    n_ge = count(lambda kk, k0: kk >= tau)
    need = topk - n_gt
    any_tie = jnp.max(n_ge) > topk

    def tie_search():
        def pos_body(b, qpos):
            cand = qpos + jnp.left_shift(jnp.int32(1), pos_bits - 1 - b)
            cnt = count(lambda kk, k0: (kk == tau) & (k0 + coli < cand))
            return jnp.where(cnt < need, cand, qpos)
        return lax.fori_loop(0, pos_bits, pos_body, jnp.zeros((QB, 1), I32)) + 1

    pstar = lax.cond(any_tie, tie_search, lambda: jnp.full((QB, 1), 2 ** pos_bits, I32))

    pstar = jnp.where(tau == INT_MIN, 0, pstar)

    def bias_body(kt, carry):
        k0 = pl.multiple_of(kt * KT, KT)
        kk = keys_ref[:, pl.ds(k0, KT)]
        tie = jnp.where(k0 + coli < pstar, 0.0, NEG_BIG)
        bias_ref[:, pl.ds(k0, KT)] = jnp.where(kk > tau, 0.0, jnp.where(kk == tau, tie, NEG_BIG))
        return carry

    lax.fori_loop(0, n_kt, bias_body, 0)

    def logit_chunks(k0):
        kv = ckv_ref[0, pl.ds(k0, KT), :]
        s = _nt_dot(q_st, kv)
        bias = bias_ref[:, pl.ds(k0, KT)]
        chunks = []
        for j in range(KT // 128):
            bj = bias[:, j * 128:(j + 1) * 128]
            chunks.append(s[:, j * 128:(j + 1) * 128] + jnp.concatenate([bj] * SA_HEADS, axis=0))
        return kv, chunks

    mx_ref[...] = jnp.full(mx_ref.shape, NEG_BIG, F32)

    def max_body(kt, carry):
        _, chunks = logit_chunks(pl.multiple_of(kt * KT, KT))
        mx_ref[...] = functools.reduce(jnp.maximum, chunks, mx_ref[...])
        return carry

    lax.fori_loop(0, n_kt, max_body, 0)
    m_row = jnp.max(mx_ref[...], axis=-1, keepdims=True)
    mx_ref[...] = jnp.broadcast_to(m_row, mx_ref.shape)
    l_ref[...] = jnp.zeros(l_ref.shape, F32)
    acc_ref[...] = jnp.zeros(acc_ref.shape, F32)

    def pv_body(kt, carry):
        kv, chunks = logit_chunks(pl.multiple_of(kt * KT, KT))
        m_b = mx_ref[...]
        ps = [jnp.exp2(c - m_b) for c in chunks]
        l_ref[...] = functools.reduce(jnp.add, ps, l_ref[...])
        p = jnp.concatenate([pj.astype(BF16) for pj in ps], axis=1)
        acc_ref[...] += _dot(p, kv)
        return carry

    lax.fori_loop(0, n_kt, pv_body, 0)
    l_row = jnp.sum(l_ref[...], axis=-1, keepdims=True)
    o_ref[0] = (acc_ref[...] / l_row).astype(BF16).reshape(SA_HEADS, QB, KV_RANK)


def _dsa(qix, misc, kix, qlat, ckv, topk):
    B, T, _ = kix.shape
    t_pad = -(-T // KEY_TILE) * KEY_TILE
    pos_bits = max(1, (t_pad).bit_length())

    def rows(w):
        return pl.BlockSpec((1, QBLOCK, w), lambda b, i: (b, i, 0))

    def head_rows(h, w):
        return pl.BlockSpec((1, h, QBLOCK, w), lambda b, i: (b, 0, i, 0))

    def per_b(w):
        return pl.BlockSpec((1, T, w), lambda b, i: (b, 0, 0))

    return pl.pallas_call(
        functools.partial(_dsa_kernel, topk=topk, pos_bits=pos_bits),
        grid=(B, T // QBLOCK),
        in_specs=[head_rows(IDX_HEADS, IDX_DIM), rows(MISC_W), per_b(IDX_DIM), head_rows(SA_HEADS, KV_RANK),
                  per_b(KV_RANK)],
        out_specs=head_rows(SA_HEADS, KV_RANK),
        out_shape=jax.ShapeDtypeStruct((B, SA_HEADS, T, KV_RANK), BF16),
        scratch_shapes=[pltpu.VMEM((QBLOCK, t_pad), I32),
                        pltpu.VMEM((QBLOCK, t_pad), F32),
                        pltpu.VMEM((SA_HEADS * QBLOCK, 128), F32),
                        pltpu.VMEM((SA_HEADS * QBLOCK, 128), F32),
                        pltpu.VMEM((SA_HEADS * QBLOCK, KV_RANK), F32)],
        compiler_params=pltpu.CompilerParams(dimension_semantics=("arbitrary", "arbitrary"),
                                             vmem_limit_bytes=VMEM_LIMIT),
        name="dsa",
    )(qix, misc, kix, qlat, ckv)


def _first_max(v, idx, axis):
    m = jnp.max(v, axis=axis, keepdims=True)
    big = jnp.int32(2 ** 30)
    first = jnp.min(jnp.where(v == m, idx, big), axis=axis, keepdims=True)
    return m, idx == first


def _outproj_kernel(x_ref, odn_ref, olat_ref, uv_ref, wo_ref, gt_ref, sc_ref, sh_ref, g2_ref, rwt_ref,
                    rb_ref, x1_ref, h2_ref, gates_ref):
    tm = x_ref.shape[1]
    parts = [odn_ref[0]]
    for hd in range(SA_HEADS):
        parts.append(_dot(olat_ref[0, hd], uv_ref[hd]).astype(BF16))
    mix = jnp.concatenate(parts, axis=-1)
    x1 = x_ref[0] + gt_ref[0] * _dot(mix, wo_ref[...])
    x1_ref[0] = x1
    h2 = x1 * lax.rsqrt(jnp.mean(x1 * x1, axis=-1, keepdims=True) + EPS) * g2_ref[...]
    h2 = h2 * (1.0 + sc_ref[0]) + sh_ref[0]
    h2_ref[0] = h2.astype(BF16)

    per_g = N_EXPERTS // N_GROUPS
    s = jax.nn.sigmoid(_nt_dot(rwt_ref[...], h2, HIGHEST))
    choice = s + rb_ref[...]
    ig = lax.broadcasted_iota(I32, (per_g, tm), 0)
    gscore = []
    for gidx in range(N_GROUPS):
        cg = choice[gidx * per_g:(gidx + 1) * per_g]
        m1, hot1 = _first_max(cg, ig, 0)
        gscore.append(m1 + jnp.max(jnp.where(hot1, -jnp.inf, cg), axis=0, keepdims=True))
    gsel = [jnp.zeros((1, tm), jnp.bool_) for _ in range(N_GROUPS)]
    for _ in range(TOPK_GROUPS):
        best = functools.reduce(jnp.maximum, gscore)
        found = jnp.zeros((1, tm), jnp.bool_)
        for gidx in range(N_GROUPS):
            hot = (gscore[gidx] == best) & jnp.logical_not(found)
            found = found | hot
            gsel[gidx] = gsel[gidx] | hot
            gscore[gidx] = jnp.where(hot, -jnp.inf, gscore[gidx])
    masked = jnp.concatenate(
        [jnp.where(gsel[gidx], choice[gidx * per_g:(gidx + 1) * per_g], -jnp.inf) for gidx in range(N_GROUPS)],
        axis=0)
    ei = lax.broadcasted_iota(I32, masked.shape, 0)
    gate = jnp.zeros(masked.shape, F32)
    for _ in range(TOP_K):
        _, hot = _first_max(masked, ei, 0)
        gate = jnp.where(hot, s, gate)
        masked = jnp.where(hot, -jnp.inf, masked)
    gate = gate / jnp.sum(gate, axis=0, keepdims=True) * ROUTED_SCALE
    gate = jnp.concatenate([gate, jnp.zeros((GATE_W - N_EXPERTS, tm), F32)], axis=0)
    gates_ref[0] = gate.T


def _outproj(x, odn, olat, uv, wo, gt1, sc2, sh2, g2, rwt, rb, tm):
    B, T, D = x.shape

    def full(a):
        nd = a.ndim
        return pl.BlockSpec(a.shape, lambda b, i, _n=nd: (0,) * _n)

    def rows(w):
        return pl.BlockSpec((1, tm, w), lambda b, i: (b, i, 0))

    per_b = pl.BlockSpec((1, 1, D), lambda b, i: (b, 0, 0))
    return pl.pallas_call(
        _outproj_kernel,
        grid=(B, T // tm),
        in_specs=[rows(D), rows(DN_V),
                  pl.BlockSpec((1, SA_HEADS, tm, KV_RANK), lambda b, i: (b, 0, i, 0)),
                  full(uv), full(wo), per_b, per_b, per_b,
                  full(g2), full(rwt), full(rb)],
        out_specs=[rows(D), rows(D), rows(GATE_W)],
        out_shape=[jax.ShapeDtypeStruct((B, T, D), F32), jax.ShapeDtypeStruct((B, T, D), BF16),
                   jax.ShapeDtypeStruct((B, T, GATE_W), F32)],
        compiler_params=pltpu.CompilerParams(dimension_semantics=("arbitrary", "arbitrary"),
                                             vmem_limit_bytes=VMEM_LIMIT),
        name="outproj",
    )(x, odn, olat, uv, wo, gt1, sc2, sh2, g2, rwt, rb)


def _moe_kernel(h_ref, gates_ref, wg_ref, wu_ref, wd_ref, sg_ref, su_ref, sd_ref, x1_ref, gt_ref, fg_ref,
                o_ref, acc_ref, *, group, final_norm):
    g = pl.program_id(1)
    hb = h_ref[0]

    @pl.when(g == 0)
    def _():
        sh = (_silu(_dot(hb, sg_ref[...])) * _dot(hb, su_ref[...])).astype(BF16)
        acc_ref[...] = _dot(sh, sd_ref[...])

    width = group * D_EXPERT
    er = lax.broadcasted_iota(I32, (GATE_W, width), 0)
    ec = lax.broadcasted_iota(I32, (GATE_W, width), 1)
    expand = (er == g * group + ec // D_EXPERT).astype(F32)
    gexp = _dot(gates_ref[0], expand, HIGHEST)
    hs = []
    for e in range(group):
        a = _silu(_dot(hb, wg_ref[e])) * _dot(hb, wu_ref[e])
        hs.append((a * gexp[:, e * D_EXPERT:(e + 1) * D_EXPERT]).astype(BF16))
    hcat = jnp.concatenate(hs, axis=-1)
    acc_ref[...] += _dot(hcat, wd_ref[...].reshape(width, wd_ref.shape[-1]))

    @pl.when(g == pl.num_programs(1) - 1)
    def _():
        y = x1_ref[0] + gt_ref[0] * acc_ref[...]
        if final_norm:
            y = y * lax.rsqrt(jnp.mean(y * y, axis=-1, keepdims=True) + EPS) * fg_ref[...]
        o_ref[0] = y


def _moe(h2, gates, wg, wu, wd, sg, su, sd, x1, gt2, fg, tm, group, final_norm):
    B, T, D = x1.shape
    nt = T // tm

    def full(a):
        nd = a.ndim
        return pl.BlockSpec(a.shape, lambda t, g, _n=nd: (0,) * _n)

    def rows(w):
        return pl.BlockSpec((1, tm, w), lambda t, g: (t // nt, t % nt, 0))

    return pl.pallas_call(
        functools.partial(_moe_kernel, group=group, final_norm=final_norm),
        grid=(B * nt, N_EXPERTS // group),
        in_specs=[rows(D), rows(GATE_W),
                  pl.BlockSpec((group, D, D_EXPERT), lambda t, g: (g, 0, 0)),
                  pl.BlockSpec((group, D, D_EXPERT), lambda t, g: (g, 0, 0)),
                  pl.BlockSpec((group, D_EXPERT, D), lambda t, g: (g, 0, 0)),
                  full(sg), full(su), full(sd), rows(D),
                  pl.BlockSpec((1, 1, D), lambda t, g: (t // nt, 0, 0)), full(fg)],
        out_specs=rows(D),
        out_shape=jax.ShapeDtypeStruct((B, T, D), F32),
        scratch_shapes=[pltpu.VMEM((tm, D), F32)],
        compiler_params=pltpu.CompilerParams(dimension_semantics=("arbitrary", "arbitrary"),
                                             vmem_limit_bytes=VMEM_LIMIT),
        name="moe",
    )(h2, gates, wg, wu, wd, sg, su, sd, x1, gt2, fg)


def _misc_lanes(vec, start):
    return jnp.zeros((1, MISC_W), F32).at[0, start:start + vec.shape[0]].set(vec.astype(F32))


def kernel(x, c, ada_w, ada_b, norm1_g, w_in, conv_w, a_log, dt_bias, dn_norm_g, kv_norm_g, w_uk, w_uv,
           idx_k_ln_g, idx_k_ln_b, w_out, norm2_g, router_w, router_b, exp_w_gate, exp_w_up, exp_w_down,
           sh_w_gate, sh_w_up, sh_w_down, final_g):
    B, T, D = x.shape
    depth = ada_w.shape[0]
    topk = min(IDX_TOPK_MAX, T // 4)
    tm = min(512, T)
    tm_moe = min(1024, T)
    r_dn = min(256, T)

    cond_in = jnp.zeros((8, D), F32).at[:B].set(c)
    pos = jnp.arange(tm)
    tri = ((pos[:, None] // CHUNK == pos[None, :] // CHUNK) & (pos[:, None] >= pos[None, :])).astype(F32)

    for l in range(depth):
        mod = _ada(cond_in, ada_w[l], ada_b[l][None, :])[:B]
        sh1, sc1, gt1, sh2, sc2, gt2 = [m[:, None, :] for m in jnp.split(mod, 6, axis=-1)]

        offs = [0]
        for s in (DN_QK, DN_QK, DN_V, DN_V, DN_HEADS, DN_HEADS, SA_Q, KV_RANK, IDX_Q, IDX_DIM, IDX_HEADS):
            offs.append(offs[-1] + s)
        w = w_in[l]
        wc = w[:, offs[0]:offs[3]].astype(BF16)
        wz = w[:, offs[3]:offs[4]].astype(BF16)
        wq = w[:, offs[6]:offs[7]].astype(BF16)
        wkv = w[:, offs[7]:offs[8]].astype(BF16)
        wqi = w[:, offs[8]:offs[9]].astype(BF16)
        wm = jnp.concatenate([w[:, offs[9]:offs[10]], w[:, offs[4]:offs[5]], w[:, offs[5]:offs[6]],
                              w[:, offs[10]:offs[11]],
                              jnp.zeros((D, MISC_W - IDX_DIM - 2 * DN_HEADS - IDX_HEADS), F32)],
                             axis=1).astype(BF16)
        ukt = jnp.swapaxes(w_uk[l], 1, 2).astype(BF16)

        q, k, v, z, qlat, ckv, qix, kix, misc = _inproj(
            x, sc1, sh1, norm1_g[l][None, :], wc, wz, wq, wkv, wqi, wm, conv_w[l], ukt,
            kv_norm_g[l][None, :], _misc_lanes(idx_k_ln_g[l], M_KIX), _misc_lanes(idx_k_ln_b[l], M_KIX),
            _misc_lanes(a_log[l], M_A), _misc_lanes(dt_bias[l], M_A), tri, tm)

        odn = _deltanet(q, k, v, z, misc, dn_norm_g[l][None, :], r_dn)
        olat = _dsa(qix, misc, kix, qlat, ckv, topk)

        x1, h2, gates = _outproj(x, odn, olat, w_uv[l].astype(BF16), w_out[l].astype(BF16), gt1, sc2, sh2,
                                 norm2_g[l][None, :], router_w[l].T, router_b[l][:, None], tm)

        x = _moe(h2, gates, exp_w_gate[l].astype(BF16), exp_w_up[l].astype(BF16),
                 exp_w_down[l].astype(BF16), sh_w_gate[l].astype(BF16), sh_w_up[l].astype(BF16),
                 sh_w_down[l].astype(BF16), x1, gt2, final_g[None, :], tm_moe, 4, l == depth - 1)
    return x
```

```python
import functools

import jax
import jax.numpy as jnp
from jax import lax
from jax.experimental import pallas as pl
from jax.experimental.pallas import tpu as pltpu

F32 = jnp.float32
BF16 = jnp.bfloat16
I32 = jnp.int32
HIGHEST = lax.Precision.HIGHEST

EPS = 1e-6
CHUNK = 64
DN_HEADS = 4
DN_DK = 128
DN_DV = 128
CONV_K = 4
SA_HEADS = 4
SA_DQK = 128
SA_DV = 128
KV_RANK = 256
IDX_HEADS = 4
IDX_DIM = 64
IDX_TOPK_MAX = 256
SM_SCALE = SA_DQK ** -0.5
LOG2E = 1.4426950408889634
IDX_W_SCALE = (IDX_HEADS * IDX_DIM) ** -0.5
N_EXPERTS = 64
TOP_K = 8
N_GROUPS = 8
TOPK_GROUPS = 4
D_EXPERT = 256
ROUTED_SCALE = 2.5
GATE_W = 128

DN_QK = DN_HEADS * DN_DK
DN_V = DN_HEADS * DN_DV
CONV_DIM = 2 * DN_QK + DN_V
SA_Q = SA_HEADS * SA_DQK
IDX_Q = IDX_HEADS * IDX_DIM

MISC_W = 128
M_KIX = 0
M_BETA = IDX_DIM
M_A = M_BETA + DN_HEADS
M_WIX = M_A + DN_HEADS

QBLOCK = 128
KEY_TILE = 512
PLANE_COLS = 32 * 128
PLANE_SHIFT = 12
INT_MIN = -2 ** 31
NEG_BIG = -1e30
VMEM_LIMIT = 56 * 1024 * 1024


def _nt_dot(a, b, precision=None):
    return lax.dot_general(a, b, (((1,), (1,)), ((), ())), preferred_element_type=F32,
                           precision=precision)


def _dot(a, b, precision=None):
    return jnp.dot(a, b, preferred_element_type=F32, precision=precision)


def _silu(x):
    return x * jax.nn.sigmoid(x)


def _softplus(x):
    return jnp.maximum(x, 0.0) + jnp.log(1.0 + jnp.exp(-jnp.abs(x)))


def _ada_kernel(c_ref, w_ref, b_ref, o_ref):
    cond = _silu(c_ref[...])
    o_ref[...] = _dot(cond, w_ref[...], HIGHEST) + b_ref[...]


def _ada(c_pad, ada_w, ada_b):
    rows, d = c_pad.shape
    n_out = ada_w.shape[1]
    return pl.pallas_call(
        _ada_kernel,
        grid=(n_out // d,),
        in_specs=[pl.BlockSpec((rows, d), lambda j: (0, 0)),
                  pl.BlockSpec((d, d), lambda j: (0, j)),
                  pl.BlockSpec((1, d), lambda j: (0, j))],
        out_specs=pl.BlockSpec((rows, d), lambda j: (0, j)),
        out_shape=jax.ShapeDtypeStruct((rows, n_out), F32),
        compiler_params=pltpu.CompilerParams(vmem_limit_bytes=VMEM_LIMIT),
        name="ada",
    )(c_pad, ada_w, ada_b)


def _inproj_kernel(x_ref, sc_ref, sh_ref, g1_ref, wc_ref, wz_ref, wq_ref, wkv_ref, wqi_ref, wm_ref,
                   convw_ref, ukt_ref, kvg_ref, lng_ref, lnb_ref, alog_ref, dtb_ref, tri_ref,
                   q_ref, k_ref, v_ref, z_ref, qlat_ref, ckv_ref, qix_ref, kix_ref, misc_ref,
                   conv_buf):
    tm = x_ref.shape[1]
    i = pl.program_id(1)

    x = x_ref[0]
    h = x * lax.rsqrt(jnp.mean(x * x, axis=-1, keepdims=True) + EPS) * g1_ref[...]
    h = h * (1.0 + sc_ref[0]) + sh_ref[0]
    hb = h.astype(BF16)

    @pl.when(i == 0)
    def _():
        conv_buf[0:8, :] = jnp.zeros((8, CONV_DIM), F32)

    conv_buf[8:8 + tm, :] = _dot(hb, wc_ref[...])
    for grp, dst in ((0, q_ref), (1, k_ref), (2, v_ref)):
        cols = slice(grp * DN_QK, (grp + 1) * DN_QK)
        y = jnp.zeros((tm, DN_QK), F32)
        for j in range(CONV_K):
            y = y + convw_ref[j:j + 1, cols] * conv_buf[8 - (CONV_K - 1) + j:8 - (CONV_K - 1) + j + tm, cols]
        y = _silu(y)
        if grp < 2:
            outs = []
            for hd in range(DN_HEADS):
                yh = y[:, hd * DN_DK:(hd + 1) * DN_DK]
                yh = yh * lax.rsqrt(jnp.sum(yh * yh, axis=-1, keepdims=True) + EPS)
                if grp == 0:
                    yh = yh * (DN_DK ** -0.5)
                outs.append(yh)
            y = jnp.concatenate(outs, axis=-1)
        dst[0] = y
    conv_buf[0:8, :] = conv_buf[tm:tm + 8, :]

    z_ref[0] = _dot(hb, wz_ref[...])

    q_sa = _dot(hb, wq_ref[...]).astype(BF16)
    for hd in range(SA_HEADS):
        ql = _dot(q_sa[:, hd * SA_DQK:(hd + 1) * SA_DQK], ukt_ref[hd]) * (SM_SCALE * LOG2E)
        qlat_ref[0, hd] = ql.astype(BF16)

    ckv = _dot(hb, wkv_ref[...])
    ckv = ckv * lax.rsqrt(jnp.mean(ckv * ckv, axis=-1, keepdims=True) + EPS) * kvg_ref[...]
    ckv_ref[0] = ckv.astype(BF16)

    q_ix = _dot(hb, wqi_ref[...]).astype(BF16)
    for hd in range(IDX_HEADS):
        qix_ref[0, hd] = q_ix[:, hd * IDX_DIM:(hd + 1) * IDX_DIM]

    m = _dot(hb, wm_ref[...])
    lane = lax.broadcasted_iota(I32, (tm, MISC_W), 1)
    is_k = lane < IDX_DIM
    mu = jnp.sum(jnp.where(is_k, m, 0.0), axis=-1, keepdims=True) * (1.0 / IDX_DIM)
    kc = jnp.where(is_k, m - mu, 0.0)
    var = jnp.sum(kc * kc, axis=-1, keepdims=True) * (1.0 / IDX_DIM)
    kn = kc * lax.rsqrt(var + EPS) * lng_ref[...] + lnb_ref[...]
    kix_ref[0] = kn[:, :IDX_DIM].astype(BF16)

    beta = jax.nn.sigmoid(m)
    g = -jnp.exp(alog_ref[...]) * _softplus(m + dtb_ref[...])
    is_a = (lane >= M_A) & (lane < M_A + DN_HEADS)
    g = jnp.where(is_a, g, 0.0)
    gc = _dot(tri_ref[...], g, HIGHEST)
    is_b = (lane >= M_BETA) & (lane < M_BETA + DN_HEADS)
    is_w = (lane >= M_WIX) & (lane < M_WIX + IDX_HEADS)
    misc_ref[0] = jnp.where(is_b, beta, jnp.where(is_a, gc, jnp.where(is_w, m * IDX_W_SCALE, 0.0)))


def _inproj(x, sc1, sh1, g1, wc, wz, wq, wkv, wqi, wm, conv_w, ukt, kvg, lng, lnb, alog, dtb, tri, tm):
    B, T, D = x.shape
    nt = T // tm

    def full(a):
        nd = a.ndim
        return pl.BlockSpec(a.shape, lambda b, i, _n=nd: (0,) * _n)

    def rows(w):
        return pl.BlockSpec((1, tm, w), lambda b, i: (b, i, 0))

    per_b = pl.BlockSpec((1, 1, D), lambda b, i: (b, 0, 0))
    def head_rows(h, w):
        return pl.BlockSpec((1, h, tm, w), lambda b, i: (b, 0, i, 0))

    outs = [(None, DN_QK, F32), (None, DN_QK, F32), (None, DN_V, F32), (None, DN_V, F32),
            (SA_HEADS, KV_RANK, BF16), (None, KV_RANK, BF16), (IDX_HEADS, IDX_DIM, BF16),
            (None, IDX_DIM, BF16), (None, MISC_W, F32)]
    return pl.pallas_call(
        _inproj_kernel,
        grid=(B, nt),
        in_specs=[rows(D), per_b, per_b, full(g1), full(wc), full(wz), full(wq), full(wkv), full(wqi),
                  full(wm), full(conv_w), full(ukt), full(kvg), full(lng), full(lnb), full(alog),
                  full(dtb), full(tri)],
        out_specs=[rows(w) if h is None else head_rows(h, w) for h, w, _ in outs],
        out_shape=[jax.ShapeDtypeStruct((B, T, w) if h is None else (B, h, T, w), dt) for h, w, dt in outs],
        scratch_shapes=[pltpu.VMEM((tm + 8, CONV_DIM), F32)],
        compiler_params=pltpu.CompilerParams(dimension_semantics=("arbitrary", "arbitrary"),
                                             vmem_limit_bytes=VMEM_LIMIT),
        name="inproj",
    )(x, sc1, sh1, g1, wc, wz, wq, wkv, wqi, wm, conv_w, ukt, kvg, lng, lnb, alog, dtb, tri)


def _deltanet_kernel(q_ref, k_ref, v_ref, z_ref, misc_ref, ng_ref, o_ref, s_ref):
    R = q_ref.shape[1]
    n_chunks = R // CHUNK

    @pl.when(pl.program_id(1) == 0)
    def _():
        s_ref[...] = jnp.zeros(s_ref.shape, F32)

    misc = misc_ref[0]
    misc_t = misc.T
    row = lax.broadcasted_iota(I32, (R, R), 0)
    col = lax.broadcasted_iota(I32, (R, R), 1)
    same = (row // CHUNK) == (col // CHUNK)
    lower = same & (row >= col)
    strict = same & (row > col)
    eye = (row == col).astype(F32)

    def mm(a, b):
        return _dot(a.astype(BF16), b.astype(BF16))

    def mm3(a, b):
        ah = a.astype(BF16)
        bh = b.astype(BF16)
        al = (a - ah.astype(F32)).astype(BF16)
        bl = (b - bh.astype(F32)).astype(BF16)
        return _dot(ah, bh) + (_dot(ah, bl) + _dot(al, bh))

    for hd in range(DN_HEADS):
        cs = slice(hd * DN_DK, (hd + 1) * DN_DK)
        qh = q_ref[0, :, cs]
        kh = k_ref[0, :, cs]
        vh = v_ref[0, :, cs]
        beta = misc[:, M_BETA + hd:M_BETA + hd + 1]
        gc_c = misc[:, M_A + hd:M_A + hd + 1]
        gc_r = misc_t[M_A + hd:M_A + hd + 1, :]
        decay = jnp.where(lower, jnp.exp(jnp.where(lower, gc_c - gc_r, 0.0)), 0.0)
        kb = kh * beta
        khb = kh.astype(BF16)
        a = jnp.where(strict, _nt_dot(kb.astype(BF16), khb) * decay, 0.0)
        p = eye - a
        xp = mm3(a, a)
        n_sq = 1
        while True:
            p = p + mm3(p, xp)
            n_sq *= 2
            if n_sq * 2 >= CHUNK:
                break
            xp = mm3(xp, xp)
        eg = jnp.exp(gc_c)
        rhs = jnp.concatenate([vh * beta, kb * eg], axis=-1)
        sol = mm(p, rhs)
        u = sol[:, :DN_DV]
        w = sol[:, DN_DV:]
        qk = jnp.where(lower, _nt_dot(qh.astype(BF16), khb) * decay, 0.0)
        q_dec = qh * eg

        s = s_ref[hd]
        o_parts = []
        for c in range(n_chunks):
            rs = slice(c * CHUNK, (c + 1) * CHUNK)
            gl = gc_c[(c + 1) * CHUNK - 1:(c + 1) * CHUNK, :]
            k_dec = kh[rs] * jnp.exp(gl - gc_c[rs])
            v_new = u[rs] - mm(w[rs], s)
            o_c = mm(q_dec[rs], s) + mm(qk[rs, rs], v_new)
            s = s * jnp.exp(gl) + mm(k_dec.T, v_new)
            o_parts.append(o_c)
        s_ref[hd] = s
        o = jnp.concatenate(o_parts, axis=0)
        o = o * lax.rsqrt(jnp.mean(o * o, axis=-1, keepdims=True) + EPS) * ng_ref[...]
        o_ref[0, :, cs] = (o * _silu(z_ref[0, :, cs])).astype(BF16)


def _deltanet(q, k, v, z, misc, ng, R):
    B, T, _ = q.shape

    def rows(w):
        return pl.BlockSpec((1, R, w), lambda b, i: (b, i, 0))

    return pl.pallas_call(
        _deltanet_kernel,
        grid=(B, T // R),
        in_specs=[rows(DN_QK), rows(DN_QK), rows(DN_V), rows(DN_V), rows(MISC_W),
                  pl.BlockSpec((1, DN_DV), lambda b, i: (0, 0))],
        out_specs=rows(DN_V),
        out_shape=jax.ShapeDtypeStruct((B, T, DN_V), BF16),
        scratch_shapes=[pltpu.VMEM((DN_HEADS, DN_DK, DN_DV), F32)],
        compiler_params=pltpu.CompilerParams(dimension_semantics=("arbitrary", "arbitrary"),
                                             vmem_limit_bytes=VMEM_LIMIT),
        name="deltanet",
    )(q, k, v, z, misc, ng)


def _bit_transpose32(words):
    w = list(words)
    j = 16
    m = 0x0000FFFF
    while j:
        k = 0
        m_i32 = jnp.int32(m - (1 << 32) if m >= (1 << 31) else m)
        while k < 32:
            t = (w[k] ^ lax.shift_right_logical(w[k + j], jnp.full_like(w[k], j))) & m_i32
            w[k] = w[k] ^ t
            w[k + j] = w[k + j] ^ jnp.left_shift(t, j)
            k = (k + j + 1) & ~j
        j >>= 1
        m = (m ^ (m << j)) & 0xFFFFFFFF
    return w


def _dsa_kernel(qix_ref, misc_ref, kix_ref, qlat_ref, ckv_ref, o_ref, keys_ref, planes_ref, bias_ref,
                mx_ref, l_ref, acc_ref, *, topk, pos_bits, n_cg_max):
    i = pl.program_id(1)
    QB = QBLOCK
    KT = KEY_TILE
    n_kt = (i * QB + QB + KT - 1) // KT

    rowi = lax.broadcasted_iota(I32, (QB, KT), 0)
    coli = lax.broadcasted_iota(I32, (QB, KT), 1)
    limit = i * QB + (rowi // CHUNK + 1) * CHUNK

    misc = misc_ref[0]
    qix = qix_ref[0].reshape(IDX_HEADS * QB, IDX_DIM)
    q_st = qlat_ref[0].reshape(SA_HEADS * QB, KV_RANK)

    def score_body(kt, carry):
        k0 = pl.multiple_of(kt * KT, KT)
        kx = kix_ref[0, pl.ds(k0, KT), :]
        rel = jnp.maximum(_nt_dot(qix, kx), 0.0)
        sc = jnp.zeros((QB, KT), F32)
        for hd in range(IDX_HEADS):
            sc = sc + misc[:, M_WIX + hd:M_WIX + hd + 1] * rel[hd * QB:(hd + 1) * QB]
        sc = jnp.where(sc == 0.0, 0.0, sc)
        bits = pltpu.bitcast(sc, I32)
        key = jnp.where(bits < 0, bits ^ 0x7FFFFFFF, bits)
        keys_ref[:, pl.ds(k0, KT)] = jnp.where(k0 + coli < limit, key, INT_MIN)
        return carry

    lax.fori_loop(0, n_kt, score_body, 0)

    n_cg = (n_kt * KT + PLANE_COLS - 1) // PLANE_COLS

    def fill_body(kt, carry):
        keys_ref[:, pl.ds(pl.multiple_of(kt * KT, KT), KT)] = jnp.full((QB, KT), INT_MIN, I32)
        return carry

    lax.fori_loop(n_kt, n_cg * (PLANE_COLS // KT), fill_body, 0)

    @pl.when(i == 0)
    def _():
        planes_ref[...] = jnp.zeros(planes_ref.shape, I32)

    def plane_body(step, carry):
        c = step // (QB // 8)
        r0 = pl.multiple_of((step % (QB // 8)) * 8, 8)
        words = [keys_ref[pl.ds(r0, 8), pl.ds(pl.multiple_of(c * PLANE_COLS + j * 128, 128), 128)]
                 for j in range(32)]
        for b, plane in enumerate(_bit_transpose32(words)):
            planes_ref[c, b, pl.ds(r0, 8), :] = ~plane if b == 0 else plane
        return carry

    lax.fori_loop(0, n_cg * (QB // 8), plane_body, 0)

    def lane_count(words):
        pc = functools.reduce(jnp.add, [lax.population_count(x) for x in words])
        return jnp.sum(pc, axis=1, keepdims=True)

    def sel_body(b, carry):
        cand, n_gt, tau_u = carry
        hi = [cand[c] & planes_ref[c, b] for c in range(n_cg_max)]
        cnt = lane_count(hi)
        one = (n_gt + cnt) >= topk
        cand = tuple(jnp.where(one, hi[c], cand[c] ^ hi[c]) for c in range(n_cg_max))
        bit = lax.shift_right_logical(jnp.int32(INT_MIN), b)
        return cand, jnp.where(one, n_gt, n_gt + cnt), jnp.where(one, tau_u | bit, tau_u)

    cand0 = tuple(jnp.where(c < n_cg, jnp.full((QB, 128), -1, I32), 0) for c in range(n_cg_max))
    cand, n_gt, tau_u = lax.fori_loop(
        0, 32, sel_body, (cand0, jnp.zeros((QB, 1), I32), jnp.zeros((QB, 1), I32)))
    tau = tau_u ^ INT_MIN
    sentinel = tau == INT_MIN
    cand = tuple(jnp.where(sentinel, 0, cand[c]) for c in range(n_cg_max))
    need = topk - n_gt
    any_tie = jnp.max(jnp.where(lane_count(cand) > need, 1, 0)) > 0

    def fast_bias():
        floor = jnp.where(sentinel, INT_MIN + 1, tau)

        def body(kt, carry):
            k0 = pl.multiple_of(kt * KT, KT)
            bias_ref[:, pl.ds(k0, KT)] = jnp.where(keys_ref[:, pl.ds(k0, KT)] >= floor, 0.0, NEG_BIG)
            return carry

        lax.fori_loop(0, n_kt, body, 0)

    def tie_bias():
        lane = lax.broadcasted_iota(I32, (QB, 128), 1)

        def pos_mask(p, c):
            cg = lax.shift_right_logical(p, jnp.full_like(p, PLANE_SHIFT))
            j0 = lax.shift_right_logical(p, jnp.full_like(p, 7)) & 31
            below = ~lax.shift_right_logical(jnp.full_like(p, -1), j0)
            bit = lax.shift_right_logical(jnp.full_like(p, INT_MIN), j0)
            word = below | jnp.where(lane < (p & 127), bit, 0)
            return jnp.where(cg > c, -1, jnp.where(cg == c, word, 0))

        def pos_body(b, q):
            cq = q + jnp.left_shift(jnp.int32(1), pos_bits - 1 - b)
            cnt = lane_count([cand[c] & pos_mask(cq, c) for c in range(n_cg_max)])
            return jnp.where(cnt < need, cq, q)

        pstar = lax.fori_loop(0, pos_bits, pos_body, jnp.zeros((QB, 1), I32)) + 1
        pstar = jnp.where(sentinel, 0, pstar)

        def body(kt, carry):
            k0 = pl.multiple_of(kt * KT, KT)
            kk = keys_ref[:, pl.ds(k0, KT)]
            tie = jnp.where(k0 + coli < pstar, 0.0, NEG_BIG)
            bias_ref[:, pl.ds(k0, KT)] = jnp.where(kk > tau, 0.0, jnp.where(kk == tau, tie, NEG_BIG))
            return carry

        lax.fori_loop(0, n_kt, body, 0)

    lax.cond(any_tie, tie_bias, fast_bias)

    def logit_chunks(k0):
        kv = ckv_ref[0, pl.ds(k0, KT), :]
        s = _nt_dot(q_st, kv)
        bias = bias_ref[:, pl.ds(k0, KT)]
        chunks = []
        for j in range(KT // 128):
            bj = bias[:, j * 128:(j + 1) * 128]
            chunks.append(s[:, j * 128:(j + 1) * 128] + jnp.concatenate([bj] * SA_HEADS, axis=0))
        return kv, chunks

    mx_ref[...] = jnp.full(mx_ref.shape, NEG_BIG, F32)

    def max_body(kt, carry):
        _, chunks = logit_chunks(pl.multiple_of(kt * KT, KT))
        mx_ref[...] = functools.reduce(jnp.maximum, chunks, mx_ref[...])
        return carry

    lax.fori_loop(0, n_kt, max_body, 0)
    m_row = jnp.max(mx_ref[...], axis=-1, keepdims=True)
    mx_ref[...] = jnp.broadcast_to(m_row, mx_ref.shape)
    l_ref[...] = jnp.zeros(l_ref.shape, F32)
    acc_ref[...] = jnp.zeros(acc_ref.shape, F32)

    def pv_body(kt, carry):
        kv, chunks = logit_chunks(pl.multiple_of(kt * KT, KT))
        m_b = mx_ref[...]
        ps = [jnp.exp2(c - m_b) for c in chunks]
        l_ref[...] = functools.reduce(jnp.add, ps, l_ref[...])
        p = jnp.concatenate([pj.astype(BF16) for pj in ps], axis=1)
        acc_ref[...] += _dot(p, kv)
        return carry

    lax.fori_loop(0, n_kt, pv_body, 0)
    l_row = jnp.sum(l_ref[...], axis=-1, keepdims=True)
    o_ref[0] = (acc_ref[...] / l_row).astype(BF16).reshape(SA_HEADS, QB, KV_RANK)


def _dsa(qix, misc, kix, qlat, ckv, topk):
    B, T, _ = kix.shape
    n_cg_max = -(-T // PLANE_COLS)
    t_pad = n_cg_max * PLANE_COLS
    pos_bits = (t_pad - 1).bit_length()

    def rows(w):
        return pl.BlockSpec((1, QBLOCK, w), lambda b, i: (b, i, 0))

    def head_rows(h, w):
        return pl.BlockSpec((1, h, QBLOCK, w), lambda b, i: (b, 0, i, 0))

    def per_b(w):
        return pl.BlockSpec((1, T, w), lambda b, i: (b, 0, 0))

    return pl.pallas_call(
        functools.partial(_dsa_kernel, topk=topk, pos_bits=pos_bits, n_cg_max=n_cg_max),
        grid=(B, T // QBLOCK),
        in_specs=[head_rows(IDX_HEADS, IDX_DIM), rows(MISC_W), per_b(IDX_DIM), head_rows(SA_HEADS, KV_RANK),
                  per_b(KV_RANK)],
        out_specs=head_rows(SA_HEADS, KV_RANK),
        out_shape=jax.ShapeDtypeStruct((B, SA_HEADS, T, KV_RANK), BF16),
        scratch_shapes=[pltpu.VMEM((QBLOCK, t_pad), I32),
                        pltpu.VMEM((n_cg_max, 32, QBLOCK, 128), I32),
                        pltpu.VMEM((QBLOCK, t_pad), F32),
                        pltpu.VMEM((SA_HEADS * QBLOCK, 128), F32),
                        pltpu.VMEM((SA_HEADS * QBLOCK, 128), F32),
                        pltpu.VMEM((SA_HEADS * QBLOCK, KV_RANK), F32)],
        compiler_params=pltpu.CompilerParams(dimension_semantics=("arbitrary", "arbitrary"),
                                             vmem_limit_bytes=VMEM_LIMIT),
        name="dsa",
    )(qix, misc, kix, qlat, ckv)


def _first_max(v, idx, axis):
    m = jnp.max(v, axis=axis, keepdims=True)
    big = jnp.int32(2 ** 30)
    first = jnp.min(jnp.where(v == m, idx, big), axis=axis, keepdims=True)
    return m, idx == first


def _outproj_kernel(x_ref, odn_ref, olat_ref, uv_ref, wo_ref, gt_ref, sc_ref, sh_ref, g2_ref, rwt_ref,
                    rb_ref, x1_ref, h2_ref, gates_ref):
    tm = x_ref.shape[1]
    parts = [odn_ref[0]]
    for hd in range(SA_HEADS):
        parts.append(_dot(olat_ref[0, hd], uv_ref[hd]).astype(BF16))
    mix = jnp.concatenate(parts, axis=-1)
    x1 = x_ref[0] + gt_ref[0] * _dot(mix, wo_ref[...])
    x1_ref[0] = x1
    h2 = x1 * lax.rsqrt(jnp.mean(x1 * x1, axis=-1, keepdims=True) + EPS) * g2_ref[...]
    h2 = h2 * (1.0 + sc_ref[0]) + sh_ref[0]
    h2_ref[0] = h2.astype(BF16)

    per_g = N_EXPERTS // N_GROUPS
    s = jax.nn.sigmoid(_nt_dot(rwt_ref[...], h2, HIGHEST))
    choice = s + rb_ref[...]
    ig = lax.broadcasted_iota(I32, (per_g, tm), 0)
    gscore = []
    for gidx in range(N_GROUPS):
        cg = choice[gidx * per_g:(gidx + 1) * per_g]
        m1, hot1 = _first_max(cg, ig, 0)
        gscore.append(m1 + jnp.max(jnp.where(hot1, -jnp.inf, cg), axis=0, keepdims=True))
    gsel = [jnp.zeros((1, tm), jnp.bool_) for _ in range(N_GROUPS)]
    for _ in range(TOPK_GROUPS):
        best = functools.reduce(jnp.maximum, gscore)
        found = jnp.zeros((1, tm), jnp.bool_)
        for gidx in range(N_GROUPS):
            hot = (gscore[gidx] == best) & jnp.logical_not(found)
            found = found | hot
            gsel[gidx] = gsel[gidx] | hot
            gscore[gidx] = jnp.where(hot, -jnp.inf, gscore[gidx])
    masked = jnp.concatenate(
        [jnp.where(gsel[gidx], choice[gidx * per_g:(gidx + 1) * per_g], -jnp.inf) for gidx in range(N_GROUPS)],
        axis=0)
    ei = lax.broadcasted_iota(I32, masked.shape, 0)
    gate = jnp.zeros(masked.shape, F32)
    for _ in range(TOP_K):
        _, hot = _first_max(masked, ei, 0)
        gate = jnp.where(hot, s, gate)
        masked = jnp.where(hot, -jnp.inf, masked)
    gate = gate / jnp.sum(gate, axis=0, keepdims=True) * ROUTED_SCALE
    gate = jnp.concatenate([gate, jnp.zeros((GATE_W - N_EXPERTS, tm), F32)], axis=0)
    gates_ref[0] = gate.T


def _outproj(x, odn, olat, uv, wo, gt1, sc2, sh2, g2, rwt, rb, tm):
    B, T, D = x.shape

    def full(a):
        nd = a.ndim
        return pl.BlockSpec(a.shape, lambda b, i, _n=nd: (0,) * _n)

    def rows(w):
        return pl.BlockSpec((1, tm, w), lambda b, i: (b, i, 0))

    per_b = pl.BlockSpec((1, 1, D), lambda b, i: (b, 0, 0))
    return pl.pallas_call(
        _outproj_kernel,
        grid=(B, T // tm),
        in_specs=[rows(D), rows(DN_V),
                  pl.BlockSpec((1, SA_HEADS, tm, KV_RANK), lambda b, i: (b, 0, i, 0)),
                  full(uv), full(wo), per_b, per_b, per_b,
                  full(g2), full(rwt), full(rb)],
        out_specs=[rows(D), rows(D), rows(GATE_W)],
        out_shape=[jax.ShapeDtypeStruct((B, T, D), F32), jax.ShapeDtypeStruct((B, T, D), BF16),
                   jax.ShapeDtypeStruct((B, T, GATE_W), F32)],
        compiler_params=pltpu.CompilerParams(dimension_semantics=("arbitrary", "arbitrary"),
                                             vmem_limit_bytes=VMEM_LIMIT),
        name="outproj",
    )(x, odn, olat, uv, wo, gt1, sc2, sh2, g2, rwt, rb)


def _moe_kernel(h_ref, gates_ref, wg_ref, wu_ref, wd_ref, sg_ref, su_ref, sd_ref, x1_ref, gt_ref, fg_ref,
                o_ref, acc_ref, *, group, final_norm):
    g = pl.program_id(1)
    hb = h_ref[0]

    @pl.when(g == 0)
    def _():
        sh = (_silu(_dot(hb, sg_ref[...])) * _dot(hb, su_ref[...])).astype(BF16)
        acc_ref[...] = _dot(sh, sd_ref[...])

    width = group * D_EXPERT
    er = lax.broadcasted_iota(I32, (GATE_W, width), 0)
    ec = lax.broadcasted_iota(I32, (GATE_W, width), 1)
    expand = (er == g * group + ec // D_EXPERT).astype(F32)
    gexp = _dot(gates_ref[0], expand, HIGHEST)
    hs = []
    for e in range(group):
        a = _silu(_dot(hb, wg_ref[e])) * _dot(hb, wu_ref[e])
        hs.append((a * gexp[:, e * D_EXPERT:(e + 1) * D_EXPERT]).astype(BF16))
    hcat = jnp.concatenate(hs, axis=-1)
    acc_ref[...] += _dot(hcat, wd_ref[...].reshape(width, wd_ref.shape[-1]))

    @pl.when(g == pl.num_programs(1) - 1)
    def _():
        y = x1_ref[0] + gt_ref[0] * acc_ref[...]
        if final_norm:
            y = y * lax.rsqrt(jnp.mean(y * y, axis=-1, keepdims=True) + EPS) * fg_ref[...]
        o_ref[0] = y


def _moe(h2, gates, wg, wu, wd, sg, su, sd, x1, gt2, fg, tm, group, final_norm):
    B, T, D = x1.shape
    nt = T // tm

    def full(a):
        nd = a.ndim
        return pl.BlockSpec(a.shape, lambda t, g, _n=nd: (0,) * _n)

    def rows(w):
        return pl.BlockSpec((1, tm, w), lambda t, g: (t // nt, t % nt, 0))

    return pl.pallas_call(
        functools.partial(_moe_kernel, group=group, final_norm=final_norm),
        grid=(B * nt, N_EXPERTS // group),
        in_specs=[rows(D), rows(GATE_W),
                  pl.BlockSpec((group, D, D_EXPERT), lambda t, g: (g, 0, 0)),
                  pl.BlockSpec((group, D, D_EXPERT), lambda t, g: (g, 0, 0)),
                  pl.BlockSpec((group, D_EXPERT, D), lambda t, g: (g, 0, 0)),
                  full(sg), full(su), full(sd), rows(D),
                  pl.BlockSpec((1, 1, D), lambda t, g: (t // nt, 0, 0)), full(fg)],
        out_specs=rows(D),
        out_shape=jax.ShapeDtypeStruct((B, T, D), F32),
        scratch_shapes=[pltpu.VMEM((tm, D), F32)],
        compiler_params=pltpu.CompilerParams(dimension_semantics=("arbitrary", "arbitrary"),
                                             vmem_limit_bytes=VMEM_LIMIT),
        name="moe",
    )(h2, gates, wg, wu, wd, sg, su, sd, x1, gt2, fg)


def _misc_lanes(vec, start):
    return jnp.zeros((1, MISC_W), F32).at[0, start:start + vec.shape[0]].set(vec.astype(F32))


def kernel(x, c, ada_w, ada_b, norm1_g, w_in, conv_w, a_log, dt_bias, dn_norm_g, kv_norm_g, w_uk, w_uv,
           idx_k_ln_g, idx_k_ln_b, w_out, norm2_g, router_w, router_b, exp_w_gate, exp_w_up, exp_w_down,
           sh_w_gate, sh_w_up, sh_w_down, final_g):
    B, T, D = x.shape
    depth = ada_w.shape[0]
    topk = min(IDX_TOPK_MAX, T // 4)
    tm = min(512, T)
    tm_moe = min(1024, T)
    r_dn = min(256, T)

    cond_in = jnp.zeros((8, D), F32).at[:B].set(c)
    pos = jnp.arange(tm)
    tri = ((pos[:, None] // CHUNK == pos[None, :] // CHUNK) & (pos[:, None] >= pos[None, :])).astype(F32)

    for l in range(depth):
        mod = _ada(cond_in, ada_w[l], ada_b[l][None, :])[:B]
        sh1, sc1, gt1, sh2, sc2, gt2 = [m[:, None, :] for m in jnp.split(mod, 6, axis=-1)]

        offs = [0]
        for s in (DN_QK, DN_QK, DN_V, DN_V, DN_HEADS, DN_HEADS, SA_Q, KV_RANK, IDX_Q, IDX_DIM, IDX_HEADS):
            offs.append(offs[-1] + s)
        w = w_in[l]
        wc = w[:, offs[0]:offs[3]].astype(BF16)
        wz = w[:, offs[3]:offs[4]].astype(BF16)
        wq = w[:, offs[6]:offs[7]].astype(BF16)
        wkv = w[:, offs[7]:offs[8]].astype(BF16)
        wqi = w[:, offs[8]:offs[9]].astype(BF16)
        wm = jnp.concatenate([w[:, offs[9]:offs[10]], w[:, offs[4]:offs[5]], w[:, offs[5]:offs[6]],
                              w[:, offs[10]:offs[11]],
                              jnp.zeros((D, MISC_W - IDX_DIM - 2 * DN_HEADS - IDX_HEADS), F32)],
                             axis=1).astype(BF16)
        ukt = jnp.swapaxes(w_uk[l], 1, 2).astype(BF16)

        q, k, v, z, qlat, ckv, qix, kix, misc = _inproj(
            x, sc1, sh1, norm1_g[l][None, :], wc, wz, wq, wkv, wqi, wm, conv_w[l], ukt,
            kv_norm_g[l][None, :], _misc_lanes(idx_k_ln_g[l], M_KIX), _misc_lanes(idx_k_ln_b[l], M_KIX),
            _misc_lanes(a_log[l], M_A), _misc_lanes(dt_bias[l], M_A), tri, tm)

        odn = _deltanet(q, k, v, z, misc, dn_norm_g[l][None, :], r_dn)
        olat = _dsa(qix, misc, kix, qlat, ckv, topk)

        x1, h2, gates = _outproj(x, odn, olat, w_uv[l].astype(BF16), w_out[l].astype(BF16), gt1, sc2, sh2,
                                 norm2_g[l][None, :], router_w[l].T, router_b[l][:, None], tm)

        x = _moe(h2, gates, exp_w_gate[l].astype(BF16), exp_w_up[l].astype(BF16),
                 exp_w_down[l].astype(BF16), sh_w_gate[l].astype(BF16), sh_w_up[l].astype(BF16),
                 sh_w_down[l].astype(BF16), x1, gt2, final_g[None, :], tm_moe, 4, l == depth - 1)
    return x
```

```python
import functools

import jax
import jax.numpy as jnp
from jax import lax
from jax.experimental import pallas as pl
from jax.experimental.pallas import tpu as pltpu

F32 = jnp.float32
BF16 = jnp.bfloat16
I32 = jnp.int32
HIGHEST = lax.Precision.HIGHEST

EPS = 1e-6
CHUNK = 64
DN_HEADS = 4
DN_DK = 128
DN_DV = 128
CONV_K = 4
SA_HEADS = 4
SA_DQK = 128
SA_DV = 128
KV_RANK = 256
IDX_HEADS = 4
IDX_DIM = 64
IDX_TOPK_MAX = 256
SM_SCALE = SA_DQK ** -0.5
LOG2E = 1.4426950408889634
IDX_W_SCALE = (IDX_HEADS * IDX_DIM) ** -0.5
N_EXPERTS = 64
TOP_K = 8
N_GROUPS = 8
TOPK_GROUPS = 4
D_EXPERT = 256
ROUTED_SCALE = 2.5
GATE_W = 128
RUN_ALIGN = 16
ROW_BLOCK = 512
MOE_CHUNK = 512

DN_QK = DN_HEADS * DN_DK
DN_V = DN_HEADS * DN_DV
CONV_DIM = 2 * DN_QK + DN_V
SA_Q = SA_HEADS * SA_DQK
IDX_Q = IDX_HEADS * IDX_DIM

MISC_W = 128
M_KIX = 0
M_BETA = IDX_DIM
M_A = M_BETA + DN_HEADS
M_WIX = M_A + DN_HEADS

QBLOCK = 128
KEY_TILE = 512
PLANE_COLS = 32 * 128
PLANE_SHIFT = 12
INT_MIN = -2 ** 31
NEG_BIG = -1e30
VMEM_LIMIT = 56 * 1024 * 1024


def _nt_dot(a, b, precision=None):
    return lax.dot_general(a, b, (((1,), (1,)), ((), ())), preferred_element_type=F32,
                           precision=precision)


def _dot(a, b, precision=None):
    return jnp.dot(a, b, preferred_element_type=F32, precision=precision)


def _silu(x):
    return x * jax.nn.sigmoid(x)


def _softplus(x):
    return jnp.maximum(x, 0.0) + jnp.log(1.0 + jnp.exp(-jnp.abs(x)))


def _ada_kernel(c_ref, w_ref, b_ref, o_ref):
    cond = _silu(c_ref[...])
    o_ref[...] = _dot(cond, w_ref[...], HIGHEST) + b_ref[...]


def _ada(c_pad, ada_w, ada_b):
    rows, d = c_pad.shape
    n_out = ada_w.shape[1]
    return pl.pallas_call(
        _ada_kernel,
        grid=(n_out // d,),
        in_specs=[pl.BlockSpec((rows, d), lambda j: (0, 0)),
                  pl.BlockSpec((d, d), lambda j: (0, j)),
                  pl.BlockSpec((1, d), lambda j: (0, j))],
        out_specs=pl.BlockSpec((rows, d), lambda j: (0, j)),
        out_shape=jax.ShapeDtypeStruct((rows, n_out), F32),
        compiler_params=pltpu.CompilerParams(vmem_limit_bytes=VMEM_LIMIT),
        name="ada",
    )(c_pad, ada_w, ada_b)


def _inproj_kernel(x_ref, sc_ref, sh_ref, g1_ref, wc_ref, wz_ref, wq_ref, wkv_ref, wqi_ref, wm_ref,
                   convw_ref, ukt_ref, kvg_ref, lng_ref, lnb_ref, alog_ref, dtb_ref, tri_ref,
                   q_ref, k_ref, v_ref, z_ref, qlat_ref, ckv_ref, qix_ref, kix_ref, misc_ref,
                   conv_buf):
    tm = x_ref.shape[1]
    i = pl.program_id(1)

    x = x_ref[0]
    h = x * lax.rsqrt(jnp.mean(x * x, axis=-1, keepdims=True) + EPS) * g1_ref[...]
    h = h * (1.0 + sc_ref[0]) + sh_ref[0]
    hb = h.astype(BF16)

    @pl.when(i == 0)
    def _():
        conv_buf[0:8, :] = jnp.zeros((8, CONV_DIM), F32)

    conv_buf[8:8 + tm, :] = _dot(hb, wc_ref[...])
    for grp, dst in ((0, q_ref), (1, k_ref), (2, v_ref)):
        cols = slice(grp * DN_QK, (grp + 1) * DN_QK)
        y = jnp.zeros((tm, DN_QK), F32)
        for j in range(CONV_K):
            y = y + convw_ref[j:j + 1, cols] * conv_buf[8 - (CONV_K - 1) + j:8 - (CONV_K - 1) + j + tm, cols]
        y = _silu(y)
        if grp < 2:
            outs = []
            for hd in range(DN_HEADS):
                yh = y[:, hd * DN_DK:(hd + 1) * DN_DK]
                yh = yh * lax.rsqrt(jnp.sum(yh * yh, axis=-1, keepdims=True) + EPS)
                if grp == 0:
                    yh = yh * (DN_DK ** -0.5)
                outs.append(yh)
            y = jnp.concatenate(outs, axis=-1)
        dst[0] = y
    conv_buf[0:8, :] = conv_buf[tm:tm + 8, :]

    z_ref[0] = _dot(hb, wz_ref[...])

    q_sa = _dot(hb, wq_ref[...]).astype(BF16)
    for hd in range(SA_HEADS):
        ql = _dot(q_sa[:, hd * SA_DQK:(hd + 1) * SA_DQK], ukt_ref[hd]) * (SM_SCALE * LOG2E)
        qlat_ref[0, hd] = ql.astype(BF16)

    ckv = _dot(hb, wkv_ref[...])
    ckv = ckv * lax.rsqrt(jnp.mean(ckv * ckv, axis=-1, keepdims=True) + EPS) * kvg_ref[...]
    ckv_ref[0] = ckv.astype(BF16)

    q_ix = _dot(hb, wqi_ref[...]).astype(BF16)
    for hd in range(IDX_HEADS):
        qix_ref[0, hd] = q_ix[:, hd * IDX_DIM:(hd + 1) * IDX_DIM]

    m = _dot(hb, wm_ref[...])
    lane = lax.broadcasted_iota(I32, (tm, MISC_W), 1)
    is_k = lane < IDX_DIM
    mu = jnp.sum(jnp.where(is_k, m, 0.0), axis=-1, keepdims=True) * (1.0 / IDX_DIM)
    kc = jnp.where(is_k, m - mu, 0.0)
    var = jnp.sum(kc * kc, axis=-1, keepdims=True) * (1.0 / IDX_DIM)
    kn = kc * lax.rsqrt(var + EPS) * lng_ref[...] + lnb_ref[...]
    kix_ref[0] = kn[:, :IDX_DIM].astype(BF16)

    beta = jax.nn.sigmoid(m)
    g = -jnp.exp(alog_ref[...]) * _softplus(m + dtb_ref[...])
    is_a = (lane >= M_A) & (lane < M_A + DN_HEADS)
    g = jnp.where(is_a, g, 0.0)
    gc = _dot(tri_ref[...], g, HIGHEST)
    is_b = (lane >= M_BETA) & (lane < M_BETA + DN_HEADS)
    is_w = (lane >= M_WIX) & (lane < M_WIX + IDX_HEADS)
    misc_ref[0] = jnp.where(is_b, beta, jnp.where(is_a, gc, jnp.where(is_w, m * IDX_W_SCALE, 0.0)))


def _inproj(x, sc1, sh1, g1, wc, wz, wq, wkv, wqi, wm, conv_w, ukt, kvg, lng, lnb, alog, dtb, tri, tm):
    B, T, D = x.shape
    nt = T // tm

    def full(a):
        nd = a.ndim
        return pl.BlockSpec(a.shape, lambda b, i, _n=nd: (0,) * _n)

    def rows(w):
        return pl.BlockSpec((1, tm, w), lambda b, i: (b, i, 0))

    per_b = pl.BlockSpec((1, 1, D), lambda b, i: (b, 0, 0))
    def head_rows(h, w):
        return pl.BlockSpec((1, h, tm, w), lambda b, i: (b, 0, i, 0))

    outs = [(None, DN_QK, F32), (None, DN_QK, F32), (None, DN_V, F32), (None, DN_V, F32),
            (SA_HEADS, KV_RANK, BF16), (None, KV_RANK, BF16), (IDX_HEADS, IDX_DIM, BF16),
            (None, IDX_DIM, BF16), (None, MISC_W, F32)]
    return pl.pallas_call(
        _inproj_kernel,
        grid=(B, nt),
        in_specs=[rows(D), per_b, per_b, full(g1), full(wc), full(wz), full(wq), full(wkv), full(wqi),
                  full(wm), full(conv_w), full(ukt), full(kvg), full(lng), full(lnb), full(alog),
                  full(dtb), full(tri)],
        out_specs=[rows(w) if h is None else head_rows(h, w) for h, w, _ in outs],
        out_shape=[jax.ShapeDtypeStruct((B, T, w) if h is None else (B, h, T, w), dt) for h, w, dt in outs],
        scratch_shapes=[pltpu.VMEM((tm + 8, CONV_DIM), F32)],
        compiler_params=pltpu.CompilerParams(dimension_semantics=("arbitrary", "arbitrary"),
                                             vmem_limit_bytes=VMEM_LIMIT),
        name="inproj",
    )(x, sc1, sh1, g1, wc, wz, wq, wkv, wqi, wm, conv_w, ukt, kvg, lng, lnb, alog, dtb, tri)


def _deltanet_kernel(q_ref, k_ref, v_ref, z_ref, misc_ref, ng_ref, o_ref, s_ref):
    R = q_ref.shape[1]
    n_chunks = R // CHUNK

    @pl.when(pl.program_id(1) == 0)
    def _():
        s_ref[...] = jnp.zeros(s_ref.shape, F32)

    misc = misc_ref[0]
    misc_t = misc.T
    row = lax.broadcasted_iota(I32, (R, R), 0)
    col = lax.broadcasted_iota(I32, (R, R), 1)
    same = (row // CHUNK) == (col // CHUNK)
    lower = same & (row >= col)
    strict = same & (row > col)
    eye = (row == col).astype(F32)

    def mm(a, b):
        return _dot(a.astype(BF16), b.astype(BF16))

    def mm3(a, b):
        ah = a.astype(BF16)
        bh = b.astype(BF16)
        al = (a - ah.astype(F32)).astype(BF16)
        bl = (b - bh.astype(F32)).astype(BF16)
        return _dot(ah, bh) + (_dot(ah, bl) + _dot(al, bh))

    for hd in range(DN_HEADS):
        cs = slice(hd * DN_DK, (hd + 1) * DN_DK)
        qh = q_ref[0, :, cs]
        kh = k_ref[0, :, cs]
        vh = v_ref[0, :, cs]
        beta = misc[:, M_BETA + hd:M_BETA + hd + 1]
        gc_c = misc[:, M_A + hd:M_A + hd + 1]
        gc_r = misc_t[M_A + hd:M_A + hd + 1, :]
        decay = jnp.where(lower, jnp.exp(jnp.where(lower, gc_c - gc_r, 0.0)), 0.0)
        kb = kh * beta
        khb = kh.astype(BF16)
        a = jnp.where(strict, _nt_dot(kb.astype(BF16), khb) * decay, 0.0)
        p = eye - a
        xp = mm3(a, a)
        n_sq = 1
        while True:
            p = p + mm3(p, xp)
            n_sq *= 2
            if n_sq * 2 >= CHUNK:
                break
            xp = mm3(xp, xp)
        eg = jnp.exp(gc_c)
        rhs = jnp.concatenate([vh * beta, kb * eg], axis=-1)
        sol = mm(p, rhs)
        u = sol[:, :DN_DV]
        w = sol[:, DN_DV:]
        qk = jnp.where(lower, _nt_dot(qh.astype(BF16), khb) * decay, 0.0)
        q_dec = qh * eg

        s = s_ref[hd]
        o_parts = []
        for c in range(n_chunks):
            rs = slice(c * CHUNK, (c + 1) * CHUNK)
            gl = gc_c[(c + 1) * CHUNK - 1:(c + 1) * CHUNK, :]
            k_dec = kh[rs] * jnp.exp(gl - gc_c[rs])
            v_new = u[rs] - mm(w[rs], s)
            o_c = mm(q_dec[rs], s) + mm(qk[rs, rs], v_new)
            s = s * jnp.exp(gl) + mm(k_dec.T, v_new)
            o_parts.append(o_c)
        s_ref[hd] = s
        o = jnp.concatenate(o_parts, axis=0)
        o = o * lax.rsqrt(jnp.mean(o * o, axis=-1, keepdims=True) + EPS) * ng_ref[...]
        o_ref[0, :, cs] = (o * _silu(z_ref[0, :, cs])).astype(BF16)


def _deltanet(q, k, v, z, misc, ng, R):
    B, T, _ = q.shape

    def rows(w):
        return pl.BlockSpec((1, R, w), lambda b, i: (b, i, 0))

    return pl.pallas_call(
        _deltanet_kernel,
        grid=(B, T // R),
        in_specs=[rows(DN_QK), rows(DN_QK), rows(DN_V), rows(DN_V), rows(MISC_W),
                  pl.BlockSpec((1, DN_DV), lambda b, i: (0, 0))],
        out_specs=rows(DN_V),
        out_shape=jax.ShapeDtypeStruct((B, T, DN_V), BF16),
        scratch_shapes=[pltpu.VMEM((DN_HEADS, DN_DK, DN_DV), F32)],
        compiler_params=pltpu.CompilerParams(dimension_semantics=("arbitrary", "arbitrary"),
                                             vmem_limit_bytes=VMEM_LIMIT),
        name="deltanet",
    )(q, k, v, z, misc, ng)


def _bit_transpose32(words):
    w = list(words)
    j = 16
    m = 0x0000FFFF
    while j:
        k = 0
        m_i32 = jnp.int32(m - (1 << 32) if m >= (1 << 31) else m)
        while k < 32:
            t = (w[k] ^ lax.shift_right_logical(w[k + j], jnp.full_like(w[k], j))) & m_i32
            w[k] = w[k] ^ t
            w[k + j] = w[k + j] ^ jnp.left_shift(t, j)
            k = (k + j + 1) & ~j
        j >>= 1
        m = (m ^ (m << j)) & 0xFFFFFFFF
    return w


def _dsa_kernel(qix_ref, misc_ref, kix_ref, qlat_ref, ckv_ref, o_ref, keys_ref, planes_ref, bias_ref,
                mx_ref, l_ref, acc_ref, *, topk, pos_bits, n_cg_max):
    i = pl.program_id(1)
    QB = QBLOCK
    KT = KEY_TILE
    n_kt = (i * QB + QB + KT - 1) // KT

    rowi = lax.broadcasted_iota(I32, (QB, KT), 0)
    coli = lax.broadcasted_iota(I32, (QB, KT), 1)
    limit = i * QB + (rowi // CHUNK + 1) * CHUNK

    misc = misc_ref[0]
    qix = qix_ref[0].reshape(IDX_HEADS * QB, IDX_DIM)
    q_st = qlat_ref[0].reshape(SA_HEADS * QB, KV_RANK)

    def score_body(kt, carry):
        k0 = pl.multiple_of(kt * KT, KT)
        kx = kix_ref[0, pl.ds(k0, KT), :]
        rel = jnp.maximum(_nt_dot(qix, kx), 0.0)
        sc = jnp.zeros((QB, KT), F32)
        for hd in range(IDX_HEADS):
            sc = sc + misc[:, M_WIX + hd:M_WIX + hd + 1] * rel[hd * QB:(hd + 1) * QB]
        sc = jnp.where(sc == 0.0, 0.0, sc)
        bits = pltpu.bitcast(sc, I32)
        key = jnp.where(bits < 0, bits ^ 0x7FFFFFFF, bits)
        keys_ref[:, pl.ds(k0, KT)] = jnp.where(k0 + coli < limit, key, INT_MIN)
        return carry

    lax.fori_loop(0, n_kt, score_body, 0)

    n_cg = (n_kt * KT + PLANE_COLS - 1) // PLANE_COLS

    def fill_body(kt, carry):
        keys_ref[:, pl.ds(pl.multiple_of(kt * KT, KT), KT)] = jnp.full((QB, KT), INT_MIN, I32)
        return carry

    lax.fori_loop(n_kt, n_cg * (PLANE_COLS // KT), fill_body, 0)

    @pl.when(i == 0)
    def _():
        planes_ref[...] = jnp.zeros(planes_ref.shape, I32)

    def plane_body(step, carry):
        c = step // (QB // 8)
        r0 = pl.multiple_of((step % (QB // 8)) * 8, 8)
        words = [keys_ref[pl.ds(r0, 8), pl.ds(pl.multiple_of(c * PLANE_COLS + j * 128, 128), 128)]
                 for j in range(32)]
        for b, plane in enumerate(_bit_transpose32(words)):
            planes_ref[c, b, pl.ds(r0, 8), :] = ~plane if b == 0 else plane
        return carry

    lax.fori_loop(0, n_cg * (QB // 8), plane_body, 0)

    def lane_count(words):
        pc = functools.reduce(jnp.add, [lax.population_count(x) for x in words])
        return jnp.sum(pc, axis=1, keepdims=True)

    def sel_body(b, carry):
        cand, n_gt, tau_u = carry
        hi = [cand[c] & planes_ref[c, b] for c in range(n_cg_max)]
        cnt = lane_count(hi)
        one = (n_gt + cnt) >= topk
        cand = tuple(jnp.where(one, hi[c], cand[c] ^ hi[c]) for c in range(n_cg_max))
        bit = lax.shift_right_logical(jnp.int32(INT_MIN), b)
        return cand, jnp.where(one, n_gt, n_gt + cnt), jnp.where(one, tau_u | bit, tau_u)

    cand0 = tuple(jnp.where(c < n_cg, jnp.full((QB, 128), -1, I32), 0) for c in range(n_cg_max))
    cand, n_gt, tau_u = lax.fori_loop(
        0, 32, sel_body, (cand0, jnp.zeros((QB, 1), I32), jnp.zeros((QB, 1), I32)))
    tau = tau_u ^ INT_MIN
    sentinel = tau == INT_MIN
    cand = tuple(jnp.where(sentinel, 0, cand[c]) for c in range(n_cg_max))
    need = topk - n_gt
    any_tie = jnp.max(jnp.where(lane_count(cand) > need, 1, 0)) > 0

    def fast_bias():
        floor = jnp.where(sentinel, INT_MIN + 1, tau)

        def body(kt, carry):
            k0 = pl.multiple_of(kt * KT, KT)
            bias_ref[:, pl.ds(k0, KT)] = jnp.where(keys_ref[:, pl.ds(k0, KT)] >= floor, 0.0, NEG_BIG)
            return carry

        lax.fori_loop(0, n_kt, body, 0)

    def tie_bias():
        lane = lax.broadcasted_iota(I32, (QB, 128), 1)

        def pos_mask(p, c):
            cg = lax.shift_right_logical(p, jnp.full_like(p, PLANE_SHIFT))
            j0 = lax.shift_right_logical(p, jnp.full_like(p, 7)) & 31
            below = ~lax.shift_right_logical(jnp.full_like(p, -1), j0)
            bit = lax.shift_right_logical(jnp.full_like(p, INT_MIN), j0)
            word = below | jnp.where(lane < (p & 127), bit, 0)
            return jnp.where(cg > c, -1, jnp.where(cg == c, word, 0))

        def pos_body(b, q):
            cq = q + jnp.left_shift(jnp.int32(1), pos_bits - 1 - b)
            cnt = lane_count([cand[c] & pos_mask(cq, c) for c in range(n_cg_max)])
            return jnp.where(cnt < need, cq, q)

        pstar = lax.fori_loop(0, pos_bits, pos_body, jnp.zeros((QB, 1), I32)) + 1
        pstar = jnp.where(sentinel, 0, pstar)

        def body(kt, carry):
            k0 = pl.multiple_of(kt * KT, KT)
            kk = keys_ref[:, pl.ds(k0, KT)]
            tie = jnp.where(k0 + coli < pstar, 0.0, NEG_BIG)
            bias_ref[:, pl.ds(k0, KT)] = jnp.where(kk > tau, 0.0, jnp.where(kk == tau, tie, NEG_BIG))
            return carry

        lax.fori_loop(0, n_kt, body, 0)

    lax.cond(any_tie, tie_bias, fast_bias)

    def logit_chunks(k0):
        kv = ckv_ref[0, pl.ds(k0, KT), :]
        s = _nt_dot(q_st, kv)
        bias = bias_ref[:, pl.ds(k0, KT)]
        chunks = []
        for j in range(KT // 128):
            bj = bias[:, j * 128:(j + 1) * 128]
            chunks.append(s[:, j * 128:(j + 1) * 128] + jnp.concatenate([bj] * SA_HEADS, axis=0))
        return kv, chunks

    mx_ref[...] = jnp.full(mx_ref.shape, NEG_BIG, F32)

    def max_body(kt, carry):
        _, chunks = logit_chunks(pl.multiple_of(kt * KT, KT))
        mx_ref[...] = functools.reduce(jnp.maximum, chunks, mx_ref[...])
        return carry

    lax.fori_loop(0, n_kt, max_body, 0)
    m_row = jnp.max(mx_ref[...], axis=-1, keepdims=True)
    mx_ref[...] = jnp.broadcast_to(m_row, mx_ref.shape)
    l_ref[...] = jnp.zeros(l_ref.shape, F32)
    acc_ref[...] = jnp.zeros(acc_ref.shape, F32)

    def pv_body(kt, carry):
        kv, chunks = logit_chunks(pl.multiple_of(kt * KT, KT))
        m_b = mx_ref[...]
        ps = [jnp.exp2(c - m_b) for c in chunks]
        l_ref[...] = functools.reduce(jnp.add, ps, l_ref[...])
        p = jnp.concatenate([pj.astype(BF16) for pj in ps], axis=1)
        acc_ref[...] += _dot(p, kv)
        return carry

    lax.fori_loop(0, n_kt, pv_body, 0)
    l_row = jnp.sum(l_ref[...], axis=-1, keepdims=True)
    o_ref[0] = (acc_ref[...] / l_row).astype(BF16).reshape(SA_HEADS, QB, KV_RANK)


def _dsa(qix, misc, kix, qlat, ckv, topk):
    B, T, _ = kix.shape
    n_cg_max = -(-T // PLANE_COLS)
    t_pad = n_cg_max * PLANE_COLS
    pos_bits = (t_pad - 1).bit_length()

    def rows(w):
        return pl.BlockSpec((1, QBLOCK, w), lambda b, i: (b, i, 0))

    def head_rows(h, w):
        return pl.BlockSpec((1, h, QBLOCK, w), lambda b, i: (b, 0, i, 0))

    def per_b(w):
        return pl.BlockSpec((1, T, w), lambda b, i: (b, 0, 0))

    return pl.pallas_call(
        functools.partial(_dsa_kernel, topk=topk, pos_bits=pos_bits, n_cg_max=n_cg_max),
        grid=(B, T // QBLOCK),
        in_specs=[head_rows(IDX_HEADS, IDX_DIM), rows(MISC_W), per_b(IDX_DIM), head_rows(SA_HEADS, KV_RANK),
                  per_b(KV_RANK)],
        out_specs=head_rows(SA_HEADS, KV_RANK),
        out_shape=jax.ShapeDtypeStruct((B, SA_HEADS, T, KV_RANK), BF16),
        scratch_shapes=[pltpu.VMEM((QBLOCK, t_pad), I32),
                        pltpu.VMEM((n_cg_max, 32, QBLOCK, 128), I32),
                        pltpu.VMEM((QBLOCK, t_pad), F32),
                        pltpu.VMEM((SA_HEADS * QBLOCK, 128), F32),
                        pltpu.VMEM((SA_HEADS * QBLOCK, 128), F32),
                        pltpu.VMEM((SA_HEADS * QBLOCK, KV_RANK), F32)],
        compiler_params=pltpu.CompilerParams(dimension_semantics=("arbitrary", "arbitrary"),
                                             vmem_limit_bytes=VMEM_LIMIT),
        name="dsa",
    )(qix, misc, kix, qlat, ckv)


def _first_max(v, idx, axis):
    m = jnp.max(v, axis=axis, keepdims=True)
    big = jnp.int32(2 ** 30)
    first = jnp.min(jnp.where(v == m, idx, big), axis=axis, keepdims=True)
    return m, idx == first


def _outproj_kernel(x_ref, odn_ref, olat_ref, uv_ref, wo_ref, gt_ref, sc_ref, sh_ref, g2_ref, rwt_ref,
                    rb_ref, lstrict_ref, ustrict_ref, x1_ref, h2_ref, posrow_ref, poscol_ref, gatecol_ref,
                    cpad_ref):
    tm = x_ref.shape[1]
    parts = [odn_ref[0]]
    for hd in range(SA_HEADS):
        parts.append(_dot(olat_ref[0, hd], uv_ref[hd]).astype(BF16))
    mix = jnp.concatenate(parts, axis=-1)
    x1 = x_ref[0] + gt_ref[0] * _dot(mix, wo_ref[...])
    x1_ref[0] = x1
    h2 = x1 * lax.rsqrt(jnp.mean(x1 * x1, axis=-1, keepdims=True) + EPS) * g2_ref[...]
    h2 = h2 * (1.0 + sc_ref[0]) + sh_ref[0]
    h2_ref[0] = h2.astype(BF16)

    per_g = N_EXPERTS // N_GROUPS
    s = jax.nn.sigmoid(_nt_dot(rwt_ref[...], h2, HIGHEST))
    choice = s + rb_ref[...]
    ig = lax.broadcasted_iota(I32, (per_g, tm), 0)
    gscore = []
    for gidx in range(N_GROUPS):
        cg = choice[gidx * per_g:(gidx + 1) * per_g]
        m1, hot1 = _first_max(cg, ig, 0)
        gscore.append(m1 + jnp.max(jnp.where(hot1, -jnp.inf, cg), axis=0, keepdims=True))
    gsel = [jnp.zeros((1, tm), jnp.bool_) for _ in range(N_GROUPS)]
    for _ in range(TOPK_GROUPS):
        best = functools.reduce(jnp.maximum, gscore)
        found = jnp.zeros((1, tm), jnp.bool_)
        for gidx in range(N_GROUPS):
            hot = (gscore[gidx] == best) & jnp.logical_not(found)
            found = found | hot
            gsel[gidx] = gsel[gidx] | hot
            gscore[gidx] = jnp.where(hot, -jnp.inf, gscore[gidx])
    masked = jnp.concatenate(
        [jnp.where(gsel[gidx], choice[gidx * per_g:(gidx + 1) * per_g], -jnp.inf) for gidx in range(N_GROUPS)],
        axis=0)
    ei = lax.broadcasted_iota(I32, masked.shape, 0)
    gate = jnp.zeros(masked.shape, F32)
    hots = []
    for _ in range(TOP_K):
        _, hot = _first_max(masked, ei, 0)
        hots.append(hot)
        gate = jnp.where(hot, s, gate)
        masked = jnp.where(hot, -jnp.inf, masked)
    gate = gate / jnp.sum(gate, axis=0, keepdims=True) * ROUTED_SCALE

    picked = jnp.where(functools.reduce(jnp.logical_or, hots), 1.0, 0.0)
    cnt = jnp.sum(picked, axis=1, keepdims=True)
    cpad = jnp.floor((cnt + (RUN_ALIGN - 1)) * (1.0 / RUN_ALIGN)) * RUN_ALIGN
    cpad_b = jnp.broadcast_to(cpad, (N_EXPERTS, GATE_W))
    lbase = _dot(lstrict_ref[...], cpad_b, HIGHEST)[:, :1]
    rank = _dot(picked.astype(BF16), ustrict_ref[...])
    pos = lbase + rank
    ri = lax.broadcasted_iota(I32, (GATE_W, tm), 0)
    pos_rows = jnp.zeros((GATE_W, tm), F32)
    gate_rows = jnp.zeros((GATE_W, tm), F32)
    for k, hot in enumerate(hots):
        pos_rows = jnp.where(ri == k, jnp.sum(jnp.where(hot, pos, 0.0), axis=0, keepdims=True), pos_rows)
        gate_rows = jnp.where(ri == k, jnp.sum(jnp.where(hot, gate, 0.0), axis=0, keepdims=True), gate_rows)
    posrow_ref[0, 0] = pos_rows[:TOP_K].astype(I32)
    poscol_ref[0] = pos_rows.T.astype(I32)
    gatecol_ref[0] = gate_rows.T
    cpad_ref[0, 0] = cpad_b.astype(I32)


def _outproj(x, odn, olat, uv, wo, gt1, sc2, sh2, g2, rwt, rb, tm):
    B, T, D = x.shape
    nt = T // tm
    ex = jnp.arange(N_EXPERTS)
    lstrict = (ex[:, None] > ex[None, :]).astype(F32)
    tok = jnp.arange(tm)
    ustrict = (tok[:, None] < tok[None, :]).astype(BF16)

    def full(a):
        nd = a.ndim
        return pl.BlockSpec(a.shape, lambda b, i, _n=nd: (0,) * _n)

    def rows(w):
        return pl.BlockSpec((1, tm, w), lambda b, i: (b, i, 0))

    def per_tile(h, w):
        return pl.BlockSpec((1, 1, h, w), lambda b, i: (b, i, 0, 0))

    per_b = pl.BlockSpec((1, 1, D), lambda b, i: (b, 0, 0))
    return pl.pallas_call(
        _outproj_kernel,
        grid=(B, nt),
        in_specs=[rows(D), rows(DN_V),
                  pl.BlockSpec((1, SA_HEADS, tm, KV_RANK), lambda b, i: (b, 0, i, 0)),
                  full(uv), full(wo), per_b, per_b, per_b,
                  full(g2), full(rwt), full(rb), full(lstrict), full(ustrict)],
        out_specs=[rows(D), rows(D), per_tile(TOP_K, tm), rows(GATE_W), rows(GATE_W),
                   per_tile(N_EXPERTS, GATE_W)],
        out_shape=[jax.ShapeDtypeStruct((B, T, D), F32), jax.ShapeDtypeStruct((B, T, D), BF16),
                   jax.ShapeDtypeStruct((B, nt, TOP_K, tm), I32),
                   jax.ShapeDtypeStruct((B, T, GATE_W), I32),
                   jax.ShapeDtypeStruct((B, T, GATE_W), F32),
                   jax.ShapeDtypeStruct((B, nt, N_EXPERTS, GATE_W), I32)],
        compiler_params=pltpu.CompilerParams(dimension_semantics=("arbitrary", "arbitrary"),
                                             vmem_limit_bytes=VMEM_LIMIT),
        name="outproj",
    )(x, odn, olat, uv, wo, gt1, sc2, sh2, g2, rwt, rb, lstrict, ustrict)


def _piece_sizes(max_rows):
    sizes = []
    z = RUN_ALIGN
    while z <= max_rows:
        sizes.append(z)
        z *= 2
    return sizes[::-1]


def _for_run_pieces(length, max_rows, fn):
    for z in _piece_sizes(max_rows):
        start = length & ~(2 * z - 1)

        @pl.when((length & z) != 0)
        def _(start=start, z=z):
            fn(start, z)


def _dispatch_kernel(off_ref, cp_ref, lb_ref, foff_ref, flen_ref, h_ref, posrow_ref, xs_hbm, buf, zbuf, sem,
                     zsem, *, n_steps, tile):
    s = pl.program_id(0)
    slot = s % 2
    jmax = buf.shape[1]

    def run_copies(step, slot_, act):
        def body(e, carry):
            idx = step * N_EXPERTS + e
            lb = lb_ref[idx]
            of = off_ref[idx]

            def piece(start, z):
                act(pltpu.make_async_copy(
                    buf.at[slot_, pl.ds(pl.multiple_of(lb + start, RUN_ALIGN), z)],
                    xs_hbm.at[pl.ds(pl.multiple_of(of + start, RUN_ALIGN), z)], sem.at[slot_]))

            _for_run_pieces(cp_ref[idx], tile, piece)
            return carry

        lax.fori_loop(0, N_EXPERTS, body, 0)

    def fill_copies(act):
        def body(e, carry):
            fo = foff_ref[e]

            def piece(start, z):
                act(pltpu.make_async_copy(
                    zbuf.at[pl.ds(0, z)], xs_hbm.at[pl.ds(pl.multiple_of(fo + start, RUN_ALIGN), z)], zsem.at[0]))

            _for_run_pieces(flen_ref[e], ROW_BLOCK // 2, piece)
            return carry

        lax.fori_loop(0, N_EXPERTS, body, 0)

        def tail(r, carry):
            act(pltpu.make_async_copy(
                zbuf, xs_hbm.at[pl.ds(pl.multiple_of(foff_ref[N_EXPERTS] + r * zbuf.shape[0], RUN_ALIGN),
                                      zbuf.shape[0])], zsem.at[0]))
            return carry

        lax.fori_loop(0, (xs_hbm.shape[0] - foff_ref[N_EXPERTS]) // zbuf.shape[0], tail, 0)

    @pl.when(s == 0)
    def _():
        zbuf[...] = jnp.zeros(zbuf.shape, BF16)
        fill_copies(lambda c: c.start())

    @pl.when(s >= 2)
    def _():
        run_copies(s - 2, slot, lambda c: c.wait())

    h = h_ref[...]
    last = s * N_EXPERTS + N_EXPERTS - 1
    jused = lb_ref[last] + cp_ref[last]
    for jc in range(jmax // MOE_CHUNK):
        @pl.when(jc * MOE_CHUNK < jused)
        def _(jc=jc):
            ji = lax.broadcasted_iota(I32, (MOE_CHUNK, tile), 0) + jc * MOE_CHUNK
            p = jnp.zeros((MOE_CHUNK, tile), F32)
            for k in range(TOP_K):
                p = jnp.where(ji == posrow_ref[0, k:k + 1, :], 1.0, p)
            buf[slot, jc * MOE_CHUNK:(jc + 1) * MOE_CHUNK, :] = _dot(p.astype(BF16), h).astype(BF16)

    run_copies(s, slot, lambda c: c.start())

    @pl.when(s == n_steps - 1)
    def _():
        if n_steps >= 2:
            run_copies(s - 1, 1 - slot, lambda c: c.wait())
        run_copies(s, slot, lambda c: c.wait())
        fill_copies(lambda c: c.wait())


def _dispatch(h2, posrow, off, cp, lb, foff, flen, cap, tile, jmax):
    n_tok, D = h2.shape
    n_steps = n_tok // tile
    return pl.pallas_call(
        functools.partial(_dispatch_kernel, n_steps=n_steps, tile=tile),
        grid_spec=pltpu.PrefetchScalarGridSpec(
            num_scalar_prefetch=5,
            grid=(n_steps,),
            in_specs=[pl.BlockSpec((tile, D), lambda s, *_: (s, 0)),
                      pl.BlockSpec((1, TOP_K, tile), lambda s, *_: (s, 0, 0))],
            out_specs=pl.BlockSpec(memory_space=pl.ANY),
            scratch_shapes=[pltpu.VMEM((2, jmax, D), BF16), pltpu.VMEM((ROW_BLOCK // 2, D), BF16),
                            pltpu.SemaphoreType.DMA((2,)), pltpu.SemaphoreType.DMA((1,))]),
        out_shape=jax.ShapeDtypeStruct((cap, D), BF16),
        compiler_params=pltpu.CompilerParams(dimension_semantics=("arbitrary",), vmem_limit_bytes=VMEM_LIMIT),
        name="moe_dispatch",
    )(off, cp, lb, foff, flen, h2, posrow)


def _expert_kernel(blk_e_ref, nused_ref, xs_ref, wg_ref, wu_ref, wd_ref, ys_ref):
    used = pl.program_id(0) < nused_ref[0]

    @pl.when(used)
    def _():
        xb = xs_ref[...]
        a = _silu(_dot(xb, wg_ref[0])) * _dot(xb, wu_ref[0])
        ys_ref[...] = _dot(a.astype(BF16), wd_ref[0]).astype(BF16)

    @pl.when(jnp.logical_not(used))
    def _():
        ys_ref[...] = jnp.zeros(ys_ref.shape, BF16)


def _experts(xs, blk_e, n_used, wg, wu, wd):
    cap, D = xs.shape

    def row_block(i, be, nu):
        return (jnp.minimum(i, nu[0] - 1), 0)

    def out_block(i, be, nu):
        return (i, 0)

    def weight(i, be, nu):
        return (be[i], 0, 0)

    return pl.pallas_call(
        _expert_kernel,
        grid_spec=pltpu.PrefetchScalarGridSpec(
            num_scalar_prefetch=2,
            grid=(cap // ROW_BLOCK,),
            in_specs=[pl.BlockSpec((ROW_BLOCK, D), row_block),
                      pl.BlockSpec((1, D, D_EXPERT), weight), pl.BlockSpec((1, D, D_EXPERT), weight),
                      pl.BlockSpec((1, D_EXPERT, D), weight)],
            out_specs=pl.BlockSpec((ROW_BLOCK, D), out_block)),
        out_shape=jax.ShapeDtypeStruct((cap, D), BF16),
        compiler_params=pltpu.CompilerParams(dimension_semantics=("arbitrary",), vmem_limit_bytes=VMEM_LIMIT),
        name="moe_experts",
    )(blk_e, n_used, xs, wg, wu, wd)


def _combine_kernel(off_ref, cp_ref, lb_ref, ys_hbm, poscol_ref, gatecol_ref, h_ref, sg_ref, su_ref, sd_ref,
                    x1_ref, gt_ref, fg_ref, o_ref, buf, sem, acc_ref, *, n_steps, tile, final_norm):
    s = pl.program_id(0)
    slot = s % 2
    jmax = buf.shape[1]

    def run_copies(step, slot_, act):
        def body(e, carry):
            idx = step * N_EXPERTS + e
            lb = lb_ref[idx]
            of = off_ref[idx]

            def piece(start, z):
                act(pltpu.make_async_copy(
                    ys_hbm.at[pl.ds(pl.multiple_of(of + start, RUN_ALIGN), z)],
                    buf.at[slot_, pl.ds(pl.multiple_of(lb + start, RUN_ALIGN), z)], sem.at[slot_]))

            _for_run_pieces(cp_ref[idx], tile, piece)
            return carry

        lax.fori_loop(0, N_EXPERTS, body, 0)

    @pl.when(s == 0)
    def _():
        run_copies(0, 0, lambda c: c.start())

    @pl.when(s + 1 < n_steps)
    def _():
        run_copies(s + 1, 1 - slot, lambda c: c.start())

    hb = h_ref[...]
    shared = (_silu(_dot(hb, sg_ref[...])) * _dot(hb, su_ref[...])).astype(BF16)
    acc_ref[...] = _dot(shared, sd_ref[...])

    run_copies(s, slot, lambda c: c.wait())
    last = s * N_EXPERTS + N_EXPERTS - 1
    jused = lb_ref[last] + cp_ref[last]

    def zero_body(r, carry):
        buf[slot, pl.ds(pl.multiple_of(jused + r * RUN_ALIGN, RUN_ALIGN), RUN_ALIGN), :] = jnp.zeros(
            (RUN_ALIGN, buf.shape[2]), BF16)
        return carry

    chunk_end = (jused + MOE_CHUNK - 1) // MOE_CHUNK * MOE_CHUNK
    lax.fori_loop(0, (chunk_end - jused) // RUN_ALIGN, zero_body, 0)

    for jc in range(jmax // MOE_CHUNK):
        @pl.when(jc * MOE_CHUNK < jused)
        def _(jc=jc):
            ji = lax.broadcasted_iota(I32, (tile, MOE_CHUNK), 1) + jc * MOE_CHUNK
            g = jnp.zeros((tile, MOE_CHUNK), F32)
            for k in range(TOP_K):
                g = jnp.where(ji == poscol_ref[:, k:k + 1], gatecol_ref[:, k:k + 1], g)
            acc_ref[...] += _dot(g.astype(BF16), buf[slot, jc * MOE_CHUNK:(jc + 1) * MOE_CHUNK, :])

    y = x1_ref[...] + gt_ref[0] * acc_ref[...]
    if final_norm:
        y = y * lax.rsqrt(jnp.mean(y * y, axis=-1, keepdims=True) + EPS) * fg_ref[...]
    o_ref[...] = y


def _combine(ys, poscol, gatecol, h2, sg, su, sd, x1, gt2, fg, off, cp, lb, tile, jmax, tiles_per_batch,
             final_norm):
    n_tok, D = h2.shape
    n_steps = n_tok // tile

    def full(a):
        nd = a.ndim
        return pl.BlockSpec(a.shape, lambda s, *_, _n=nd: (0,) * _n)

    def rows(w):
        return pl.BlockSpec((tile, w), lambda s, *_: (s, 0))

    return pl.pallas_call(
        functools.partial(_combine_kernel, n_steps=n_steps, tile=tile, final_norm=final_norm),
        grid_spec=pltpu.PrefetchScalarGridSpec(
            num_scalar_prefetch=3,
            grid=(n_steps,),
            in_specs=[pl.BlockSpec(memory_space=pl.ANY), rows(GATE_W), rows(GATE_W), rows(D),
                      full(sg), full(su), full(sd), rows(D),
                      pl.BlockSpec((1, 1, D), lambda s, *_: (s // tiles_per_batch, 0, 0)), full(fg)],
            out_specs=rows(D),
            scratch_shapes=[pltpu.VMEM((2, jmax, D), BF16), pltpu.SemaphoreType.DMA((2,)),
                            pltpu.VMEM((tile, D), F32)]),
        out_shape=jax.ShapeDtypeStruct((n_tok, D), F32),
        compiler_params=pltpu.CompilerParams(dimension_semantics=("arbitrary",), vmem_limit_bytes=VMEM_LIMIT),
        name="moe_combine",
    )(off, cp, lb, ys, poscol, gatecol, h2, sg, su, sd, x1, gt2, fg)


def _moe(h2, posrow, poscol, gatecol, cpad, wg, wu, wd, sg, su, sd, x1, gt2, fg, tile, final_norm):
    B, T, D = x1.shape
    n_tok = B * T
    n_tiles = n_tok // tile
    jmax = -(-(TOP_K * tile + N_EXPERTS * (RUN_ALIGN - 1)) // MOE_CHUNK) * MOE_CHUNK
    cap = -(-(TOP_K * n_tok + n_tiles * N_EXPERTS * (RUN_ALIGN - 1) + N_EXPERTS * (ROW_BLOCK - RUN_ALIGN))
            // ROW_BLOCK) * ROW_BLOCK

    cp = cpad[..., 0].reshape(n_tiles, N_EXPERTS)
    lb = jnp.cumsum(cp, axis=1) - cp
    rows_e = jnp.sum(cp, axis=0)
    region = (rows_e + ROW_BLOCK - 1) // ROW_BLOCK * ROW_BLOCK
    region_end = jnp.cumsum(region)
    base = region_end - region
    off = base[None, :] + jnp.cumsum(cp, axis=0) - cp
    n_used = (region_end[-1] // ROW_BLOCK).astype(I32)
    blk = jnp.arange(cap // ROW_BLOCK, dtype=I32)
    blk_e = jnp.searchsorted(region_end // ROW_BLOCK, jnp.minimum(blk, n_used - 1), side="right").astype(I32)
    blk_e = jnp.minimum(blk_e, N_EXPERTS - 1)
    flat = lambda a: a.reshape(-1).astype(I32)

    xs = _dispatch(h2.reshape(n_tok, D), posrow.reshape(n_tiles, TOP_K, tile), flat(off), flat(cp), flat(lb),
                   flat(jnp.concatenate([base + rows_e, region_end[-1:]])), flat(region - rows_e), cap, tile,
                   jmax)
    ys = _experts(xs, blk_e, n_used.reshape(1), wg, wu, wd)
    out = _combine(ys, poscol.reshape(n_tok, GATE_W), gatecol.reshape(n_tok, GATE_W), h2.reshape(n_tok, D),
                   sg, su, sd, x1.reshape(n_tok, D), gt2, fg, flat(off), flat(cp), flat(lb), tile, jmax,
                   T // tile, final_norm)
    return out.reshape(B, T, D)


def _misc_lanes(vec, start):
    return jnp.zeros((1, MISC_W), F32).at[0, start:start + vec.shape[0]].set(vec.astype(F32))


def kernel(x, c, ada_w, ada_b, norm1_g, w_in, conv_w, a_log, dt_bias, dn_norm_g, kv_norm_g, w_uk, w_uv,
           idx_k_ln_g, idx_k_ln_b, w_out, norm2_g, router_w, router_b, exp_w_gate, exp_w_up, exp_w_down,
           sh_w_gate, sh_w_up, sh_w_down, final_g):
    B, T, D = x.shape
    depth = ada_w.shape[0]
    topk = min(IDX_TOPK_MAX, T // 4)
    tm = min(512, T)
    r_dn = min(256, T)

    cond_in = jnp.zeros((8, D), F32).at[:B].set(c)
    pos = jnp.arange(tm)
    tri = ((pos[:, None] // CHUNK == pos[None, :] // CHUNK) & (pos[:, None] >= pos[None, :])).astype(F32)

    for l in range(depth):
        mod = _ada(cond_in, ada_w[l], ada_b[l][None, :])[:B]
        sh1, sc1, gt1, sh2, sc2, gt2 = [m[:, None, :] for m in jnp.split(mod, 6, axis=-1)]

        offs = [0]
        for s in (DN_QK, DN_QK, DN_V, DN_V, DN_HEADS, DN_HEADS, SA_Q, KV_RANK, IDX_Q, IDX_DIM, IDX_HEADS):
            offs.append(offs[-1] + s)
        w = w_in[l]
        wc = w[:, offs[0]:offs[3]].astype(BF16)
        wz = w[:, offs[3]:offs[4]].astype(BF16)
        wq = w[:, offs[6]:offs[7]].astype(BF16)
        wkv = w[:, offs[7]:offs[8]].astype(BF16)
        wqi = w[:, offs[8]:offs[9]].astype(BF16)
        wm = jnp.concatenate([w[:, offs[9]:offs[10]], w[:, offs[4]:offs[5]], w[:, offs[5]:offs[6]],
                              w[:, offs[10]:offs[11]],
                              jnp.zeros((D, MISC_W - IDX_DIM - 2 * DN_HEADS - IDX_HEADS), F32)],
                             axis=1).astype(BF16)
        ukt = jnp.swapaxes(w_uk[l], 1, 2).astype(BF16)

        q, k, v, z, qlat, ckv, qix, kix, misc = _inproj(
            x, sc1, sh1, norm1_g[l][None, :], wc, wz, wq, wkv, wqi, wm, conv_w[l], ukt,
            kv_norm_g[l][None, :], _misc_lanes(idx_k_ln_g[l], M_KIX), _misc_lanes(idx_k_ln_b[l], M_KIX),
            _misc_lanes(a_log[l], M_A), _misc_lanes(dt_bias[l], M_A), tri, tm)

        odn = _deltanet(q, k, v, z, misc, dn_norm_g[l][None, :], r_dn)
        olat = _dsa(qix, misc, kix, qlat, ckv, topk)

        x1, h2, posrow, poscol, gatecol, cpad = _outproj(
            x, odn, olat, w_uv[l].astype(BF16), w_out[l].astype(BF16), gt1, sc2, sh2,
            norm2_g[l][None, :], router_w[l].T, router_b[l][:, None], tm)

        x = _moe(h2, posrow, poscol, gatecol, cpad, exp_w_gate[l].astype(BF16), exp_w_up[l].astype(BF16),
                 exp_w_down[l].astype(BF16), sh_w_gate[l].astype(BF16), sh_w_up[l].astype(BF16),
                 sh_w_down[l].astype(BF16), x1, gt2, final_g[None, :], tm, l == depth - 1)
    return x
```

```python
import functools

import jax
import jax.numpy as jnp
from jax import lax
from jax.experimental import pallas as pl
from jax.experimental.pallas import tpu as pltpu

F32 = jnp.float32
BF16 = jnp.bfloat16
I32 = jnp.int32
HIGHEST = lax.Precision.HIGHEST

EPS = 1e-6
CHUNK = 64
DN_HEADS = 4
DN_DK = 128
DN_DV = 128
CONV_K = 4
SA_HEADS = 4
SA_DQK = 128
SA_DV = 128
KV_RANK = 256
IDX_HEADS = 4
IDX_DIM = 64
IDX_TOPK_MAX = 256
SM_SCALE = SA_DQK ** -0.5
LOG2E = 1.4426950408889634
IDX_W_SCALE = (IDX_HEADS * IDX_DIM) ** -0.5
N_EXPERTS = 64
TOP_K = 8
N_GROUPS = 8
TOPK_GROUPS = 4
D_EXPERT = 256
ROUTED_SCALE = 2.5
GATE_W = 128
RUN_ALIGN = 16
ROW_BLOCK = 512
MOE_CHUNK = 512

DN_QK = DN_HEADS * DN_DK
DN_V = DN_HEADS * DN_DV
CONV_DIM = 2 * DN_QK + DN_V
SA_Q = SA_HEADS * SA_DQK
IDX_Q = IDX_HEADS * IDX_DIM

MISC_W = 128
M_KIX = 0
M_BETA = IDX_DIM
M_A = M_BETA + DN_HEADS
M_WIX = M_A + DN_HEADS

DN_SUB = 2 * CHUNK
QBLOCK = 128
KEY_TILE = 1024
PLANE_COLS = 32 * 128
PLANE_SHIFT = 12
INT_MIN = -2 ** 31
NEG_BIG = -1e30
VMEM_LIMIT = 56 * 1024 * 1024


def _nt_dot(a, b, precision=None):
    return lax.dot_general(a, b, (((1,), (1,)), ((), ())), preferred_element_type=F32,
                           precision=precision)


def _dot(a, b, precision=None):
    return jnp.dot(a, b, preferred_element_type=F32, precision=precision)


def _silu(x):
    return x * jax.nn.sigmoid(x)


def _softplus(x):
    return jnp.maximum(x, 0.0) + jnp.log(1.0 + jnp.exp(-jnp.abs(x)))


def _ada_kernel(c_ref, w_ref, b_ref, o_ref):
    cond = _silu(c_ref[...])
    o_ref[...] = _dot(cond, w_ref[...], HIGHEST) + b_ref[...]


def _ada(c_pad, ada_w, ada_b):
    rows, d = c_pad.shape
    n_out = ada_w.shape[1]
    return pl.pallas_call(
        _ada_kernel,
        grid=(n_out // d,),
        in_specs=[pl.BlockSpec((rows, d), lambda j: (0, 0)),
                  pl.BlockSpec((d, d), lambda j: (0, j)),
                  pl.BlockSpec((1, d), lambda j: (0, j))],
        out_specs=pl.BlockSpec((rows, d), lambda j: (0, j)),
        out_shape=jax.ShapeDtypeStruct((rows, n_out), F32),
        compiler_params=pltpu.CompilerParams(vmem_limit_bytes=VMEM_LIMIT),
        name="ada",
    )(c_pad, ada_w, ada_b)


def _inproj_kernel(x_ref, sc_ref, sh_ref, g1_ref, wc_ref, wz_ref, wq_ref, wkv_ref, wqi_ref, wm_ref,
                   convw_ref, ukt_ref, kvg_ref, lng_ref, lnb_ref, alog_ref, dtb_ref, tri_ref,
                   q_ref, k_ref, v_ref, z_ref, qlat_ref, ckv_ref, qix_ref, kix_ref, misc_ref,
                   conv_buf):
    tm = x_ref.shape[1]
    i = pl.program_id(1)

    x = x_ref[0]
    h = x * lax.rsqrt(jnp.mean(x * x, axis=-1, keepdims=True) + EPS) * g1_ref[...]
    h = h * (1.0 + sc_ref[0]) + sh_ref[0]
    hb = h.astype(BF16)

    @pl.when(i == 0)
    def _():
        conv_buf[0:8, :] = jnp.zeros((8, CONV_DIM), F32)

    conv_buf[8:8 + tm, :] = _dot(hb, wc_ref[...])
    for grp, dst in ((0, q_ref), (1, k_ref), (2, v_ref)):
        cols = slice(grp * DN_QK, (grp + 1) * DN_QK)
        y = jnp.zeros((tm, DN_QK), F32)
        for j in range(CONV_K):
            y = y + convw_ref[j:j + 1, cols] * conv_buf[8 - (CONV_K - 1) + j:8 - (CONV_K - 1) + j + tm, cols]
        y = _silu(y)
        if grp < 2:
            outs = []
            for hd in range(DN_HEADS):
                yh = y[:, hd * DN_DK:(hd + 1) * DN_DK]
                yh = yh * lax.rsqrt(jnp.sum(yh * yh, axis=-1, keepdims=True) + EPS)
                if grp == 0:
                    yh = yh * (DN_DK ** -0.5)
                outs.append(yh)
            y = jnp.concatenate(outs, axis=-1)
        dst[0] = y
    conv_buf[0:8, :] = conv_buf[tm:tm + 8, :]

    z_ref[0] = _dot(hb, wz_ref[...])

    q_sa = _dot(hb, wq_ref[...]).astype(BF16)
    for hd in range(SA_HEADS):
        ql = _dot(q_sa[:, hd * SA_DQK:(hd + 1) * SA_DQK], ukt_ref[hd]) * (SM_SCALE * LOG2E)
        qlat_ref[0, hd] = ql.astype(BF16)

    ckv = _dot(hb, wkv_ref[...])
    ckv = ckv * lax.rsqrt(jnp.mean(ckv * ckv, axis=-1, keepdims=True) + EPS) * kvg_ref[...]
    ckv_ref[0] = ckv.astype(BF16)

    q_ix = _dot(hb, wqi_ref[...]).astype(BF16)
    for hd in range(IDX_HEADS):
        qix_ref[0, hd] = q_ix[:, hd * IDX_DIM:(hd + 1) * IDX_DIM]

    m = _dot(hb, wm_ref[...])
    lane = lax.broadcasted_iota(I32, (tm, MISC_W), 1)
    is_k = lane < IDX_DIM
    mu = jnp.sum(jnp.where(is_k, m, 0.0), axis=-1, keepdims=True) * (1.0 / IDX_DIM)
    kc = jnp.where(is_k, m - mu, 0.0)
    var = jnp.sum(kc * kc, axis=-1, keepdims=True) * (1.0 / IDX_DIM)
    kn = kc * lax.rsqrt(var + EPS) * lng_ref[...] + lnb_ref[...]
    kix_ref[0] = kn[:, :IDX_DIM].astype(BF16)

    beta = jax.nn.sigmoid(m)
    g = -jnp.exp(alog_ref[...]) * _softplus(m + dtb_ref[...])
    is_a = (lane >= M_A) & (lane < M_A + DN_HEADS)
    g = jnp.where(is_a, g, 0.0)
    gc = _dot(tri_ref[...], g, HIGHEST)
    is_b = (lane >= M_BETA) & (lane < M_BETA + DN_HEADS)
    is_w = (lane >= M_WIX) & (lane < M_WIX + IDX_HEADS)
    misc_ref[0] = jnp.where(is_b, beta, jnp.where(is_a, gc, jnp.where(is_w, m * IDX_W_SCALE, 0.0)))


def _inproj(x, sc1, sh1, g1, wc, wz, wq, wkv, wqi, wm, conv_w, ukt, kvg, lng, lnb, alog, dtb, tri, tm):
    B, T, D = x.shape
    nt = T // tm

    def full(a):
        nd = a.ndim
        return pl.BlockSpec(a.shape, lambda b, i, _n=nd: (0,) * _n)

    def rows(w):
        return pl.BlockSpec((1, tm, w), lambda b, i: (b, i, 0))

    per_b = pl.BlockSpec((1, 1, D), lambda b, i: (b, 0, 0))
    def head_rows(h, w):
        return pl.BlockSpec((1, h, tm, w), lambda b, i: (b, 0, i, 0))

    outs = [(None, DN_QK, F32), (None, DN_QK, F32), (None, DN_V, F32), (None, DN_V, F32),
            (SA_HEADS, KV_RANK, BF16), (None, KV_RANK, BF16), (IDX_HEADS, IDX_DIM, BF16),
            (None, IDX_DIM, BF16), (None, MISC_W, F32)]
    return pl.pallas_call(
        _inproj_kernel,
        grid=(B, nt),
        in_specs=[rows(D), per_b, per_b, full(g1), full(wc), full(wz), full(wq), full(wkv), full(wqi),
                  full(wm), full(conv_w), full(ukt), full(kvg), full(lng), full(lnb), full(alog),
                  full(dtb), full(tri)],
        out_specs=[rows(w) if h is None else head_rows(h, w) for h, w, _ in outs],
        out_shape=[jax.ShapeDtypeStruct((B, T, w) if h is None else (B, h, T, w), dt) for h, w, dt in outs],
        scratch_shapes=[pltpu.VMEM((tm + 8, CONV_DIM), F32)],
        compiler_params=pltpu.CompilerParams(dimension_semantics=("arbitrary", "arbitrary"),
                                             vmem_limit_bytes=VMEM_LIMIT),
        name="inproj",
    )(x, sc1, sh1, g1, wc, wz, wq, wkv, wqi, wm, conv_w, ukt, kvg, lng, lnb, alog, dtb, tri)


def _deltanet_kernel(q_ref, k_ref, v_ref, z_ref, misc_ref, ng_ref, o_ref, s_ref):
    R = q_ref.shape[1]
    n_chunks = R // CHUNK

    @pl.when(pl.program_id(1) == 0)
    def _():
        s_ref[...] = jnp.zeros(s_ref.shape, F32)

    misc = misc_ref[0]
    misc_t = misc.T
    SB = min(DN_SUB, R)
    row = lax.broadcasted_iota(I32, (SB, SB), 0)
    col = lax.broadcasted_iota(I32, (SB, SB), 1)
    same = (row // CHUNK) == (col // CHUNK)
    lower = same & (row >= col)
    strict = same & (row > col)
    eye = (row == col).astype(F32)

    def mm(a, b):
        return _dot(a.astype(BF16), b.astype(BF16))

    def mm3(a, b):
        ah = a.astype(BF16)
        bh = b.astype(BF16)
        al = (a - ah.astype(F32)).astype(BF16)
        bl = (b - bh.astype(F32)).astype(BF16)
        return _dot(jnp.concatenate([ah, ah, al], axis=1), jnp.concatenate([bh, bl, bh], axis=0))

    heads = range(DN_HEADS)
    subs = range(R // SB)
    chains = [(hd, sb) for hd in heads for sb in subs]
    cols = [slice(hd * DN_DK, (hd + 1) * DN_DK) for hd in heads]
    qh = [q_ref[0, :, cols[hd]] for hd in heads]
    kh = [k_ref[0, :, cols[hd]] for hd in heads]
    beta = [misc[:, M_BETA + hd:M_BETA + hd + 1] for hd in heads]
    gc_c = [misc[:, M_A + hd:M_A + hd + 1] for hd in heads]
    eg = [jnp.exp(gc_c[hd]) for hd in heads]
    kb = [kh[hd] * beta[hd] for hd in heads]
    rhs = [jnp.concatenate([v_ref[0, :, cols[hd]] * beta[hd], kb[hd] * eg[hd]], axis=-1) for hd in heads]
    q_dec = [qh[hd] * eg[hd] for hd in heads]

    def rows_of(sb):
        return slice(sb * SB, (sb + 1) * SB)

    decay, a, qk_sb = {}, {}, {}
    for hd, sb in chains:
        bs = rows_of(sb)
        gc_r = misc_t[M_A + hd:M_A + hd + 1, bs]
        decay[hd, sb] = jnp.where(lower, jnp.exp(jnp.where(lower, gc_c[hd][bs] - gc_r, 0.0)), 0.0)
    for hd, sb in chains:
        bs = rows_of(sb)
        khb = kh[hd][bs].astype(BF16)
        a[hd, sb] = jnp.where(strict, _nt_dot(kb[hd][bs].astype(BF16), khb) * decay[hd, sb], 0.0)
        qk_sb[hd, sb] = jnp.where(lower, _nt_dot(qh[hd][bs].astype(BF16), khb) * decay[hd, sb], 0.0)
    p = {ch: eye - a[ch] for ch in chains}
    xp = {ch: mm3(a[ch], a[ch]) for ch in chains}
    n_sq = 1
    while True:
        p = {ch: p[ch] + mm3(p[ch], xp[ch]) for ch in chains}
        n_sq *= 2
        if n_sq * 2 >= CHUNK:
            break
        xp = {ch: mm3(xp[ch], xp[ch]) for ch in chains}
    sol = {(hd, sb): mm(p[hd, sb], rhs[hd][rows_of(sb)]) for hd, sb in chains}

    def chunk_of(c):
        per = SB // CHUNK
        return c // per, slice((c % per) * CHUNK, (c % per + 1) * CHUNK)

    s = [s_ref[hd] for hd in heads]
    o_parts = [[] for _ in heads]
    for c in range(n_chunks):
        rs = slice(c * CHUNK, (c + 1) * CHUNK)
        sb, r = chunk_of(c)
        gl = [gc_c[hd][(c + 1) * CHUNK - 1:(c + 1) * CHUNK, :] for hd in heads]
        k_dec = [kh[hd][rs] * jnp.exp(gl[hd] - gc_c[hd][rs]) for hd in heads]
        v_new = [sol[hd, sb][r, :DN_DV] - mm(sol[hd, sb][r, DN_DV:], s[hd]) for hd in heads]
        for hd in heads:
            o_parts[hd].append(mm(q_dec[hd][rs], s[hd]) + mm(qk_sb[hd, sb][r, r], v_new[hd]))
        s = [s[hd] * jnp.exp(gl[hd]) + mm(k_dec[hd].T, v_new[hd]) for hd in heads]
    for hd in heads:
        s_ref[hd] = s[hd]
        o = jnp.concatenate(o_parts[hd], axis=0)
        o = o * lax.rsqrt(jnp.mean(o * o, axis=-1, keepdims=True) + EPS) * ng_ref[...]
        o_ref[0, :, cols[hd]] = (o * _silu(z_ref[0, :, cols[hd]])).astype(BF16)


def _deltanet(q, k, v, z, misc, ng, R):
    B, T, _ = q.shape

    def rows(w):
        return pl.BlockSpec((1, R, w), lambda b, i: (b, i, 0))

    return pl.pallas_call(
        _deltanet_kernel,
        grid=(B, T // R),
        in_specs=[rows(DN_QK), rows(DN_QK), rows(DN_V), rows(DN_V), rows(MISC_W),
                  pl.BlockSpec((1, DN_DV), lambda b, i: (0, 0))],
        out_specs=rows(DN_V),
        out_shape=jax.ShapeDtypeStruct((B, T, DN_V), BF16),
        scratch_shapes=[pltpu.VMEM((DN_HEADS, DN_DK, DN_DV), F32)],
        compiler_params=pltpu.CompilerParams(dimension_semantics=("arbitrary", "arbitrary"),
                                             vmem_limit_bytes=VMEM_LIMIT),
        name="deltanet",
    )(q, k, v, z, misc, ng)


def _bit_transpose32(words):
    w = list(words)
    j = 16
    m = 0x0000FFFF
    while j:
        k = 0
        m_i32 = jnp.int32(m - (1 << 32) if m >= (1 << 31) else m)
        while k < 32:
            t = (w[k] ^ lax.shift_right_logical(w[k + j], jnp.full_like(w[k], j))) & m_i32
            w[k] = w[k] ^ t
            w[k + j] = w[k + j] ^ jnp.left_shift(t, j)
            k = (k + j + 1) & ~j
        j >>= 1
        m = (m ^ (m << j)) & 0xFFFFFFFF
    return w


def _dsa_kernel(qix_ref, misc_ref, kix_ref, qlat_ref, ckv_ref, o_ref, keys_ref, planes_ref, bias_ref,
                mx_ref, l_ref, acc_ref, *, topk, pos_bits, n_cg_max):
    i = pl.program_id(1)
    QB = QBLOCK
    KT = KEY_TILE
    n_kt = (i * QB + QB + KT - 1) // KT

    rowi = lax.broadcasted_iota(I32, (QB, KT), 0)
    coli = lax.broadcasted_iota(I32, (QB, KT), 1)
    limit = i * QB + (rowi // CHUNK + 1) * CHUNK

    misc = misc_ref[0]
    qix = qix_ref[0].reshape(IDX_HEADS * QB, IDX_DIM)
    q_st = qlat_ref[0].reshape(SA_HEADS * QB, KV_RANK)

    def score_body(kt, carry):
        k0 = pl.multiple_of(kt * KT, KT)
        kx = kix_ref[0, pl.ds(k0, KT), :]
        rel = jnp.maximum(_nt_dot(qix, kx), 0.0)
        sc = jnp.zeros((QB, KT), F32)
        for hd in range(IDX_HEADS):
            sc = sc + misc[:, M_WIX + hd:M_WIX + hd + 1] * rel[hd * QB:(hd + 1) * QB]
        sc = jnp.where(sc == 0.0, 0.0, sc)
        bits = pltpu.bitcast(sc, I32)
        key = jnp.where(bits < 0, bits ^ 0x7FFFFFFF, bits)
        keys_ref[:, pl.ds(k0, KT)] = jnp.where(k0 + coli < limit, key, INT_MIN)
        return carry

    lax.fori_loop(0, n_kt, score_body, 0)

    n_cg = (n_kt * KT + PLANE_COLS - 1) // PLANE_COLS

    def fill_body(kt, carry):
        keys_ref[:, pl.ds(pl.multiple_of(kt * KT, KT), KT)] = jnp.full((QB, KT), INT_MIN, I32)
        return carry

    lax.fori_loop(n_kt, n_cg * (PLANE_COLS // KT), fill_body, 0)

    @pl.when(i == 0)
    def _():
        planes_ref[...] = jnp.zeros(planes_ref.shape, I32)

    def plane_body(step, carry):
        c = step // (QB // 8)
        r0 = pl.multiple_of((step % (QB // 8)) * 8, 8)
        words = [keys_ref[pl.ds(r0, 8), pl.ds(pl.multiple_of(c * PLANE_COLS + j * 128, 128), 128)]
                 for j in range(32)]
        for b, plane in enumerate(_bit_transpose32(words)):
            planes_ref[c, b, pl.ds(r0, 8), :] = ~plane if b == 0 else plane
        return carry

    lax.fori_loop(0, n_cg * (QB // 8), plane_body, 0)

    def lane_count(words):
        pc = functools.reduce(jnp.add, [lax.population_count(x) for x in words])
        return jnp.sum(pc, axis=1, keepdims=True)

    def sel_body(step, carry):
        cand, n_gt, tau_u = carry
        groups = range(n_cg_max)
        hi = [cand[c] & planes_ref[c, 2 * step] for c in groups]
        lo = [cand[c] ^ hi[c] for c in groups]
        d3 = [hi[c] & planes_ref[c, 2 * step + 1] for c in groups]
        d2 = [hi[c] ^ d3[c] for c in groups]
        d1 = [lo[c] & planes_ref[c, 2 * step + 1] for c in groups]
        d0 = [lo[c] ^ d1[c] for c in groups]
        a3 = n_gt + lane_count(d3)
        a2 = a3 + lane_count(d2)
        a1 = a2 + lane_count(d1)
        is3 = a3 >= topk
        is2 = a2 >= topk
        is1 = a1 >= topk
        cand = tuple(jnp.where(is3, d3[c], jnp.where(is2, d2[c], jnp.where(is1, d1[c], d0[c]))) for c in groups)
        n_gt = jnp.where(is3, n_gt, jnp.where(is2, a3, jnp.where(is1, a2, a1)))
        digit = jnp.where(is3, 3, jnp.where(is2, 2, jnp.where(is1, 1, 0)))
        return cand, n_gt, tau_u | jnp.left_shift(digit, 30 - 2 * step)

    cand0 = tuple(jnp.where(c < n_cg, jnp.full((QB, 128), -1, I32), 0) for c in range(n_cg_max))
    cand, n_gt, tau_u = lax.fori_loop(
        0, 16, sel_body, (cand0, jnp.zeros((QB, 1), I32), jnp.zeros((QB, 1), I32)))
    tau = tau_u ^ INT_MIN
    sentinel = tau == INT_MIN
    cand = tuple(jnp.where(sentinel, 0, cand[c]) for c in range(n_cg_max))
    need = topk - n_gt
    any_tie = jnp.max(jnp.where(lane_count(cand) > need, 1, 0)) > 0

    def fast_bias():
        floor = jnp.where(sentinel, INT_MIN + 1, tau)

        def body(kt, carry):
            k0 = pl.multiple_of(kt * KT, KT)
            bias_ref[:, pl.ds(k0, KT)] = jnp.where(keys_ref[:, pl.ds(k0, KT)] >= floor, 0.0, NEG_BIG)
            return carry

        lax.fori_loop(0, n_kt, body, 0)

    def tie_bias():
        lane = lax.broadcasted_iota(I32, (QB, 128), 1)

        def pos_mask(p, c):
            cg = lax.shift_right_logical(p, jnp.full_like(p, PLANE_SHIFT))
            j0 = lax.shift_right_logical(p, jnp.full_like(p, 7)) & 31
            below = ~lax.shift_right_logical(jnp.full_like(p, -1), j0)
            bit = lax.shift_right_logical(jnp.full_like(p, INT_MIN), j0)
            word = below | jnp.where(lane < (p & 127), bit, 0)
            return jnp.where(cg > c, -1, jnp.where(cg == c, word, 0))

        def pos_body(b, q):
            cq = q + jnp.left_shift(jnp.int32(1), pos_bits - 1 - b)
            cnt = lane_count([cand[c] & pos_mask(cq, c) for c in range(n_cg_max)])
            return jnp.where(cnt < need, cq, q)

        pstar = lax.fori_loop(0, pos_bits, pos_body, jnp.zeros((QB, 1), I32)) + 1
        pstar = jnp.where(sentinel, 0, pstar)

        def body(kt, carry):
            k0 = pl.multiple_of(kt * KT, KT)
            kk = keys_ref[:, pl.ds(k0, KT)]
            tie = jnp.where(k0 + coli < pstar, 0.0, NEG_BIG)
            bias_ref[:, pl.ds(k0, KT)] = jnp.where(kk > tau, 0.0, jnp.where(kk == tau, tie, NEG_BIG))
            return carry

        lax.fori_loop(0, n_kt, body, 0)

    lax.cond(any_tie, tie_bias, fast_bias)

    def logit_chunks(k0):
        kv = ckv_ref[0, pl.ds(k0, KT), :]
        s = _nt_dot(q_st, kv)
        bias = bias_ref[:, pl.ds(k0, KT)]
        chunks = []
        for j in range(KT // 128):
            bj = bias[:, j * 128:(j + 1) * 128]
            chunks.append(s[:, j * 128:(j + 1) * 128] + jnp.concatenate([bj] * SA_HEADS, axis=0))
        return kv, chunks

    mx_ref[...] = jnp.full(mx_ref.shape, NEG_BIG, F32)

    def max_body(kt, carry):
        _, chunks = logit_chunks(pl.multiple_of(kt * KT, KT))
        mx_ref[...] = functools.reduce(jnp.maximum, chunks, mx_ref[...])
        return carry

    lax.fori_loop(0, n_kt, max_body, 0)
    m_row = jnp.max(mx_ref[...], axis=-1, keepdims=True)
    mx_ref[...] = jnp.broadcast_to(m_row, mx_ref.shape)
    l_ref[...] = jnp.zeros(l_ref.shape, F32)
    acc_ref[...] = jnp.zeros(acc_ref.shape, F32)

    def pv_body(kt, carry):
        kv, chunks = logit_chunks(pl.multiple_of(kt * KT, KT))
        m_b = mx_ref[...]
        ps = [jnp.exp2(c - m_b) for c in chunks]
        l_ref[...] = functools.reduce(jnp.add, ps, l_ref[...])
        p = jnp.concatenate([pj.astype(BF16) for pj in ps], axis=1)
        acc_ref[...] += _dot(p, kv)
        return carry

    lax.fori_loop(0, n_kt, pv_body, 0)
    l_row = jnp.sum(l_ref[...], axis=-1, keepdims=True)
    o_ref[0] = (acc_ref[...] / l_row).astype(BF16).reshape(SA_HEADS, QB, KV_RANK)


def _dsa(qix, misc, kix, qlat, ckv, topk):
    B, T, _ = kix.shape
    n_cg_max = -(-T // PLANE_COLS)
    t_pad = n_cg_max * PLANE_COLS
    pos_bits = (t_pad - 1).bit_length()

    def rows(w):
        return pl.BlockSpec((1, QBLOCK, w), lambda b, i: (b, i, 0))

    def head_rows(h, w):
        return pl.BlockSpec((1, h, QBLOCK, w), lambda b, i: (b, 0, i, 0))

    def per_b(w):
        return pl.BlockSpec((1, T, w), lambda b, i: (b, 0, 0))

    return pl.pallas_call(
        functools.partial(_dsa_kernel, topk=topk, pos_bits=pos_bits, n_cg_max=n_cg_max),
        grid=(B, T // QBLOCK),
        in_specs=[head_rows(IDX_HEADS, IDX_DIM), rows(MISC_W), per_b(IDX_DIM), head_rows(SA_HEADS, KV_RANK),
                  per_b(KV_RANK)],
        out_specs=head_rows(SA_HEADS, KV_RANK),
        out_shape=jax.ShapeDtypeStruct((B, SA_HEADS, T, KV_RANK), BF16),
        scratch_shapes=[pltpu.VMEM((QBLOCK, t_pad), I32),
                        pltpu.VMEM((n_cg_max, 32, QBLOCK, 128), I32),
                        pltpu.VMEM((QBLOCK, t_pad), F32),
                        pltpu.VMEM((SA_HEADS * QBLOCK, 128), F32),
                        pltpu.VMEM((SA_HEADS * QBLOCK, 128), F32),
                        pltpu.VMEM((SA_HEADS * QBLOCK, KV_RANK), F32)],
        compiler_params=pltpu.CompilerParams(dimension_semantics=("arbitrary", "arbitrary"),
                                             vmem_limit_bytes=VMEM_LIMIT),
        name="dsa",
    )(qix, misc, kix, qlat, ckv)


def _first_max(v, idx, axis):
    m = jnp.max(v, axis=axis, keepdims=True)
    big = jnp.int32(2 ** 30)
    first = jnp.min(jnp.where(v == m, idx, big), axis=axis, keepdims=True)
    return m, idx == first


def _outproj_kernel(x_ref, odn_ref, olat_ref, uv_ref, wo_ref, gt_ref, sc_ref, sh_ref, g2_ref, rwt_ref,
                    rb_ref, lstrict_ref, ustrict_ref, x1_ref, h2_ref, posrow_ref, poscol_ref, gatecol_ref,
                    cpad_ref):
    tm = x_ref.shape[1]
    parts = [odn_ref[0]]
    for hd in range(SA_HEADS):
        parts.append(_dot(olat_ref[0, hd], uv_ref[hd]).astype(BF16))
    mix = jnp.concatenate(parts, axis=-1)
    x1 = x_ref[0] + gt_ref[0] * _dot(mix, wo_ref[...])
    x1_ref[0] = x1
    h2 = x1 * lax.rsqrt(jnp.mean(x1 * x1, axis=-1, keepdims=True) + EPS) * g2_ref[...]
    h2 = h2 * (1.0 + sc_ref[0]) + sh_ref[0]
    h2_ref[0] = h2.astype(BF16)

    per_g = N_EXPERTS // N_GROUPS
    s = jax.nn.sigmoid(_nt_dot(rwt_ref[...], h2, HIGHEST))
    choice = s + rb_ref[...]
    ig = lax.broadcasted_iota(I32, (per_g, tm), 0)
    gscore = []
    for gidx in range(N_GROUPS):
        cg = choice[gidx * per_g:(gidx + 1) * per_g]
        m1, hot1 = _first_max(cg, ig, 0)
        gscore.append(m1 + jnp.max(jnp.where(hot1, -jnp.inf, cg), axis=0, keepdims=True))
    gsel = [jnp.zeros((1, tm), jnp.bool_) for _ in range(N_GROUPS)]
    for _ in range(TOPK_GROUPS):
        best = functools.reduce(jnp.maximum, gscore)
        found = jnp.zeros((1, tm), jnp.bool_)
        for gidx in range(N_GROUPS):
            hot = (gscore[gidx] == best) & jnp.logical_not(found)
            found = found | hot
            gsel[gidx] = gsel[gidx] | hot
            gscore[gidx] = jnp.where(hot, -jnp.inf, gscore[gidx])
    masked = jnp.concatenate(
        [jnp.where(gsel[gidx], choice[gidx * per_g:(gidx + 1) * per_g], -jnp.inf) for gidx in range(N_GROUPS)],
        axis=0)
    ei = lax.broadcasted_iota(I32, masked.shape, 0)
    gate = jnp.zeros(masked.shape, F32)
    hots = []
    for _ in range(TOP_K):
        _, hot = _first_max(masked, ei, 0)
        hots.append(hot)
        gate = jnp.where(hot, s, gate)
        masked = jnp.where(hot, -jnp.inf, masked)
    gate = gate / jnp.sum(gate, axis=0, keepdims=True) * ROUTED_SCALE

    picked = jnp.where(functools.reduce(jnp.logical_or, hots), 1.0, 0.0)
    cnt = jnp.sum(picked, axis=1, keepdims=True)
    cpad = jnp.floor((cnt + (RUN_ALIGN - 1)) * (1.0 / RUN_ALIGN)) * RUN_ALIGN
    cpad_b = jnp.broadcast_to(cpad, (N_EXPERTS, GATE_W))
    lbase = _dot(lstrict_ref[...], cpad_b, HIGHEST)[:, :1]
    rank = _dot(picked.astype(BF16), ustrict_ref[...])
    pos = lbase + rank
    ri = lax.broadcasted_iota(I32, (GATE_W, tm), 0)
    pos_rows = jnp.zeros((GATE_W, tm), F32)
    gate_rows = jnp.zeros((GATE_W, tm), F32)
    for k, hot in enumerate(hots):
        pos_rows = jnp.where(ri == k, jnp.sum(jnp.where(hot, pos, 0.0), axis=0, keepdims=True), pos_rows)
        gate_rows = jnp.where(ri == k, jnp.sum(jnp.where(hot, gate, 0.0), axis=0, keepdims=True), gate_rows)
    posrow_ref[0, 0] = pos_rows[:TOP_K].astype(I32)
    poscol_ref[0] = pos_rows.T.astype(I32)
    gatecol_ref[0] = gate_rows.T
    cpad_ref[0, 0] = cpad_b.astype(I32)


def _outproj(x, odn, olat, uv, wo, gt1, sc2, sh2, g2, rwt, rb, tm):
    B, T, D = x.shape
    nt = T // tm
    ex = jnp.arange(N_EXPERTS)
    lstrict = (ex[:, None] > ex[None, :]).astype(F32)
    tok = jnp.arange(tm)
    ustrict = (tok[:, None] < tok[None, :]).astype(BF16)

    def full(a):
        nd = a.ndim
        return pl.BlockSpec(a.shape, lambda b, i, _n=nd: (0,) * _n)

    def rows(w):
        return pl.BlockSpec((1, tm, w), lambda b, i: (b, i, 0))

    def per_tile(h, w):
        return pl.BlockSpec((1, 1, h, w), lambda b, i: (b, i, 0, 0))

    per_b = pl.BlockSpec((1, 1, D), lambda b, i: (b, 0, 0))
    return pl.pallas_call(
        _outproj_kernel,
        grid=(B, nt),
        in_specs=[rows(D), rows(DN_V),
                  pl.BlockSpec((1, SA_HEADS, tm, KV_RANK), lambda b, i: (b, 0, i, 0)),
                  full(uv), full(wo), per_b, per_b, per_b,
                  full(g2), full(rwt), full(rb), full(lstrict), full(ustrict)],
        out_specs=[rows(D), rows(D), per_tile(TOP_K, tm), rows(GATE_W), rows(GATE_W),
                   per_tile(N_EXPERTS, GATE_W)],
        out_shape=[jax.ShapeDtypeStruct((B, T, D), F32), jax.ShapeDtypeStruct((B, T, D), BF16),
                   jax.ShapeDtypeStruct((B, nt, TOP_K, tm), I32),
                   jax.ShapeDtypeStruct((B, T, GATE_W), I32),
                   jax.ShapeDtypeStruct((B, T, GATE_W), F32),
                   jax.ShapeDtypeStruct((B, nt, N_EXPERTS, GATE_W), I32)],
        compiler_params=pltpu.CompilerParams(dimension_semantics=("arbitrary", "arbitrary"),
                                             vmem_limit_bytes=VMEM_LIMIT),
        name="outproj",
    )(x, odn, olat, uv, wo, gt1, sc2, sh2, g2, rwt, rb, lstrict, ustrict)


def _piece_sizes(max_rows):
    sizes = []
    z = RUN_ALIGN
    while z <= max_rows:
        sizes.append(z)
        z *= 2
    return sizes[::-1]


def _for_run_pieces(length, max_rows, fn):
    for z in _piece_sizes(max_rows):
        start = length & ~(2 * z - 1)

        @pl.when((length & z) != 0)
        def _(start=start, z=z):
            fn(start, z)


def _plan_kernel(cp_ref, off_ref, lb_ref, foff_ref, flen_ref, blk_ref, nused_ref):
    cp = cp_ref[...].astype(F32)
    n, ne = cp.shape
    ei = lax.broadcasted_iota(I32, (ne, ne), 0)
    ej = lax.broadcasted_iota(I32, (ne, ne), 1)
    si = lax.broadcasted_iota(I32, (n, n), 0)
    sj = lax.broadcasted_iota(I32, (n, n), 1)
    lb = _dot(cp, (ei < ej).astype(F32), HIGHEST)
    earlier_tiles = _dot((sj < si).astype(F32), cp, HIGHEST)
    rows_e = jnp.sum(cp, axis=0, keepdims=True)
    region = jnp.floor((rows_e + (ROW_BLOCK - 1)) * (1.0 / ROW_BLOCK)) * ROW_BLOCK
    region_b = jnp.broadcast_to(region, (ne, ne))
    rend_row = _dot(region_b, (ei <= ej).astype(F32), HIGHEST)[:1]
    rend_col = jnp.sum(jnp.where(ej <= ei, region_b, 0.0), axis=1, keepdims=True)
    base = rend_row - region
    total = jnp.max(rend_row, axis=1, keepdims=True)
    off_ref[...] = (base + earlier_tiles).astype(I32)
    lb_ref[...] = lb.astype(I32)
    lane = lax.broadcasted_iota(I32, (1, GATE_W), 1)
    pad = jnp.zeros((1, GATE_W - ne), F32)
    foff_ref[...] = jnp.where(lane == ne, total, jnp.concatenate([base + rows_e, pad], axis=1)).astype(I32)
    flen_ref[...] = jnp.concatenate([region - rows_e, pad], axis=1).astype(I32)
    n_used = total * (1.0 / ROW_BLOCK)
    nused_ref[...] = jnp.broadcast_to(n_used, nused_ref.shape).astype(I32)
    bi = lax.broadcasted_iota(I32, (ne, blk_ref.shape[1]), 1).astype(F32)
    ended = jnp.where(rend_col * (1.0 / ROW_BLOCK) <= jnp.minimum(bi, n_used - 1.0), 1.0, 0.0)
    blk_ref[...] = jnp.minimum(jnp.sum(ended, axis=0, keepdims=True), ne - 1.0).astype(I32)


def _plan(cp, n_blocks):
    n, ne = cp.shape
    nb_pad = -(-n_blocks // 128) * 128
    return pl.pallas_call(
        _plan_kernel,
        out_shape=[jax.ShapeDtypeStruct((n, ne), I32), jax.ShapeDtypeStruct((n, ne), I32),
                   jax.ShapeDtypeStruct((1, GATE_W), I32), jax.ShapeDtypeStruct((1, GATE_W), I32),
                   jax.ShapeDtypeStruct((1, nb_pad), I32), jax.ShapeDtypeStruct((1, GATE_W), I32)],
        name="moe_plan",
    )(cp)


def _dispatch_kernel(off_ref, cp_ref, lb_ref, foff_ref, flen_ref, h_ref, posrow_ref, xs_hbm, buf, zbuf, sem,
                     zsem, *, n_steps, tile):
    s = pl.program_id(0)
    slot = s % 2
    jmax = buf.shape[1]

    def run_copies(step, slot_, act):
        def body(e, carry):
            idx = step * N_EXPERTS + e
            lb = lb_ref[idx]
            of = off_ref[idx]

            def piece(start, z):
                act(pltpu.make_async_copy(
                    buf.at[slot_, pl.ds(pl.multiple_of(lb + start, RUN_ALIGN), z)],
                    xs_hbm.at[pl.ds(pl.multiple_of(of + start, RUN_ALIGN), z)], sem.at[slot_]))

            _for_run_pieces(cp_ref[idx], tile, piece)
            return carry

        lax.fori_loop(0, N_EXPERTS, body, 0)

    def fill_copies(act):
        def body(e, carry):
            fo = foff_ref[e]

            def piece(start, z):
                act(pltpu.make_async_copy(
                    zbuf.at[pl.ds(0, z)], xs_hbm.at[pl.ds(pl.multiple_of(fo + start, RUN_ALIGN), z)], zsem.at[0]))

            _for_run_pieces(flen_ref[e], ROW_BLOCK // 2, piece)
            return carry

        lax.fori_loop(0, N_EXPERTS, body, 0)

        def tail(r, carry):
            act(pltpu.make_async_copy(
                zbuf, xs_hbm.at[pl.ds(pl.multiple_of(foff_ref[N_EXPERTS] + r * zbuf.shape[0], RUN_ALIGN),
                                      zbuf.shape[0])], zsem.at[0]))
            return carry

        lax.fori_loop(0, (xs_hbm.shape[0] - foff_ref[N_EXPERTS]) // zbuf.shape[0], tail, 0)

    @pl.when(s == 0)
    def _():
        zbuf[...] = jnp.zeros(zbuf.shape, BF16)
        fill_copies(lambda c: c.start())

    @pl.when(s >= 2)
    def _():
        run_copies(s - 2, slot, lambda c: c.wait())

    h = h_ref[...]
    last = s * N_EXPERTS + N_EXPERTS - 1
    jused = lb_ref[last] + cp_ref[last]
    for jc in range(jmax // MOE_CHUNK):
        @pl.when(jc * MOE_CHUNK < jused)
        def _(jc=jc):
            ji = lax.broadcasted_iota(I32, (MOE_CHUNK, tile), 0) + jc * MOE_CHUNK
            p = jnp.zeros((MOE_CHUNK, tile), F32)
            for k in range(TOP_K):
                p = jnp.where(ji == posrow_ref[0, k:k + 1, :], 1.0, p)
            buf[slot, jc * MOE_CHUNK:(jc + 1) * MOE_CHUNK, :] = _dot(p.astype(BF16), h).astype(BF16)

    run_copies(s, slot, lambda c: c.start())

    @pl.when(s == n_steps - 1)
    def _():
        if n_steps >= 2:
            run_copies(s - 1, 1 - slot, lambda c: c.wait())
        run_copies(s, slot, lambda c: c.wait())
        fill_copies(lambda c: c.wait())


def _dispatch(h2, posrow, off, cp, lb, foff, flen, cap, tile, jmax):
    n_tok, D = h2.shape
    n_steps = n_tok // tile
    return pl.pallas_call(
        functools.partial(_dispatch_kernel, n_steps=n_steps, tile=tile),
        grid_spec=pltpu.PrefetchScalarGridSpec(
            num_scalar_prefetch=5,
            grid=(n_steps,),
            in_specs=[pl.BlockSpec((tile, D), lambda s, *_: (s, 0)),
                      pl.BlockSpec((1, TOP_K, tile), lambda s, *_: (s, 0, 0))],
            out_specs=pl.BlockSpec(memory_space=pl.ANY),
            scratch_shapes=[pltpu.VMEM((2, jmax, D), BF16), pltpu.VMEM((ROW_BLOCK // 2, D), BF16),
                            pltpu.SemaphoreType.DMA((2,)), pltpu.SemaphoreType.DMA((1,))]),
        out_shape=jax.ShapeDtypeStruct((cap, D), BF16),
        compiler_params=pltpu.CompilerParams(dimension_semantics=("arbitrary",), vmem_limit_bytes=VMEM_LIMIT),
        name="moe_dispatch",
    )(off, cp, lb, foff, flen, h2, posrow)


def _expert_kernel(blk_e_ref, nused_ref, xs_ref, wg_ref, wu_ref, wd_ref, ys_ref):
    used = pl.program_id(0) < nused_ref[0]

    @pl.when(used)
    def _():
        xb = xs_ref[...]
        a = _silu(_dot(xb, wg_ref[0])) * _dot(xb, wu_ref[0])
        ys_ref[...] = _dot(a.astype(BF16), wd_ref[0]).astype(BF16)

    @pl.when(jnp.logical_not(used))
    def _():
        ys_ref[...] = jnp.zeros(ys_ref.shape, BF16)


def _experts(xs, blk_e, n_used, wg, wu, wd):
    cap, D = xs.shape

    def row_block(i, be, nu):
        return (jnp.minimum(i, nu[0] - 1), 0)

    def out_block(i, be, nu):
        return (i, 0)

    def weight(i, be, nu):
        return (be[i], 0, 0)

    return pl.pallas_call(
        _expert_kernel,
        grid_spec=pltpu.PrefetchScalarGridSpec(
            num_scalar_prefetch=2,
            grid=(cap // ROW_BLOCK,),
            in_specs=[pl.BlockSpec((ROW_BLOCK, D), row_block),
                      pl.BlockSpec((1, D, D_EXPERT), weight), pl.BlockSpec((1, D, D_EXPERT), weight),
                      pl.BlockSpec((1, D_EXPERT, D), weight)],
            out_specs=pl.BlockSpec((ROW_BLOCK, D), out_block)),
        out_shape=jax.ShapeDtypeStruct((cap, D), BF16),
        compiler_params=pltpu.CompilerParams(dimension_semantics=("arbitrary",), vmem_limit_bytes=VMEM_LIMIT),
        name="moe_experts",
    )(blk_e, n_used, xs, wg, wu, wd)


def _combine_kernel(off_ref, cp_ref, lb_ref, ys_hbm, poscol_ref, gatecol_ref, h_ref, sg_ref, su_ref, sd_ref,
                    x1_ref, gt_ref, fg_ref, o_ref, buf, sem, acc_ref, *, n_steps, tile, final_norm):
    s = pl.program_id(0)
    slot = s % 2
    jmax = buf.shape[1]

    def run_copies(step, slot_, act):
        def body(e, carry):
            idx = step * N_EXPERTS + e
            lb = lb_ref[idx]
            of = off_ref[idx]

            def piece(start, z):
                act(pltpu.make_async_copy(
                    ys_hbm.at[pl.ds(pl.multiple_of(of + start, RUN_ALIGN), z)],
                    buf.at[slot_, pl.ds(pl.multiple_of(lb + start, RUN_ALIGN), z)], sem.at[slot_]))

            _for_run_pieces(cp_ref[idx], tile, piece)
            return carry

        lax.fori_loop(0, N_EXPERTS, body, 0)

    @pl.when(s == 0)
    def _():
        run_copies(0, 0, lambda c: c.start())

    @pl.when(s + 1 < n_steps)
    def _():
        run_copies(s + 1, 1 - slot, lambda c: c.start())

    hb = h_ref[...]
    shared = (_silu(_dot(hb, sg_ref[...])) * _dot(hb, su_ref[...])).astype(BF16)
    acc_ref[...] = _dot(shared, sd_ref[...])

    run_copies(s, slot, lambda c: c.wait())
    last = s * N_EXPERTS + N_EXPERTS - 1
    jused = lb_ref[last] + cp_ref[last]

    def zero_body(r, carry):
        buf[slot, pl.ds(pl.multiple_of(jused + r * RUN_ALIGN, RUN_ALIGN), RUN_ALIGN), :] = jnp.zeros(
            (RUN_ALIGN, buf.shape[2]), BF16)
        return carry

    chunk_end = (jused + MOE_CHUNK - 1) // MOE_CHUNK * MOE_CHUNK
    lax.fori_loop(0, (chunk_end - jused) // RUN_ALIGN, zero_body, 0)

    for jc in range(jmax // MOE_CHUNK):
        @pl.when(jc * MOE_CHUNK < jused)
        def _(jc=jc):
            ji = lax.broadcasted_iota(I32, (tile, MOE_CHUNK), 1) + jc * MOE_CHUNK
            g = jnp.zeros((tile, MOE_CHUNK), F32)
            for k in range(TOP_K):
                g = jnp.where(ji == poscol_ref[:, k:k + 1], gatecol_ref[:, k:k + 1], g)
            acc_ref[...] += _dot(g.astype(BF16), buf[slot, jc * MOE_CHUNK:(jc + 1) * MOE_CHUNK, :])

    y = x1_ref[...] + gt_ref[0] * acc_ref[...]
    if final_norm:
        y = y * lax.rsqrt(jnp.mean(y * y, axis=-1, keepdims=True) + EPS) * fg_ref[...]
    o_ref[...] = y


def _combine(ys, poscol, gatecol, h2, sg, su, sd, x1, gt2, fg, off, cp, lb, tile, jmax, tiles_per_batch,
             final_norm):
    n_tok, D = h2.shape
    n_steps = n_tok // tile

    def full(a):
        nd = a.ndim
        return pl.BlockSpec(a.shape, lambda s, *_, _n=nd: (0,) * _n)

    def rows(w):
        return pl.BlockSpec((tile, w), lambda s, *_: (s, 0))

    return pl.pallas_call(
        functools.partial(_combine_kernel, n_steps=n_steps, tile=tile, final_norm=final_norm),
        grid_spec=pltpu.PrefetchScalarGridSpec(
            num_scalar_prefetch=3,
            grid=(n_steps,),
            in_specs=[pl.BlockSpec(memory_space=pl.ANY), rows(GATE_W), rows(GATE_W), rows(D),
                      full(sg), full(su), full(sd), rows(D),
                      pl.BlockSpec((1, 1, D), lambda s, *_: (s // tiles_per_batch, 0, 0)), full(fg)],
            out_specs=rows(D),
            scratch_shapes=[pltpu.VMEM((2, jmax, D), BF16), pltpu.SemaphoreType.DMA((2,)),
                            pltpu.VMEM((tile, D), F32)]),
        out_shape=jax.ShapeDtypeStruct((n_tok, D), F32),
        compiler_params=pltpu.CompilerParams(dimension_semantics=("arbitrary",), vmem_limit_bytes=VMEM_LIMIT),
        name="moe_combine",
    )(off, cp, lb, ys, poscol, gatecol, h2, sg, su, sd, x1, gt2, fg)


def _moe(h2, posrow, poscol, gatecol, cpad, wg, wu, wd, sg, su, sd, x1, gt2, fg, tile, final_norm):
    B, T, D = x1.shape
    n_tok = B * T
    n_tiles = n_tok // tile
    jmax = -(-(TOP_K * tile + N_EXPERTS * (RUN_ALIGN - 1)) // MOE_CHUNK) * MOE_CHUNK
    cap = -(-(TOP_K * n_tok + n_tiles * N_EXPERTS * (RUN_ALIGN - 1) + N_EXPERTS * (ROW_BLOCK - RUN_ALIGN))
            // ROW_BLOCK) * ROW_BLOCK

    cp = cpad[..., 0].reshape(n_tiles, N_EXPERTS)
    off, lb, foff, flen, blk_e, n_used = _plan(cp, cap // ROW_BLOCK)
    flat = lambda a: a.reshape(-1)

    xs = _dispatch(h2.reshape(n_tok, D), posrow.reshape(n_tiles, TOP_K, tile), flat(off), flat(cp), flat(lb),
                   foff[0], flen[0], cap, tile, jmax)
    ys = _experts(xs, blk_e[0], n_used[0], wg, wu, wd)
    out = _combine(ys, poscol.reshape(n_tok, GATE_W), gatecol.reshape(n_tok, GATE_W), h2.reshape(n_tok, D),
                   sg, su, sd, x1.reshape(n_tok, D), gt2, fg, flat(off), flat(cp), flat(lb), tile, jmax,
                   T // tile, final_norm)
    return out.reshape(B, T, D)


def _misc_lanes(vec, start):
    return jnp.zeros((1, MISC_W), F32).at[0, start:start + vec.shape[0]].set(vec.astype(F32))


def kernel(x, c, ada_w, ada_b, norm1_g, w_in, conv_w, a_log, dt_bias, dn_norm_g, kv_norm_g, w_uk, w_uv,
           idx_k_ln_g, idx_k_ln_b, w_out, norm2_g, router_w, router_b, exp_w_gate, exp_w_up, exp_w_down,
           sh_w_gate, sh_w_up, sh_w_down, final_g):
    B, T, D = x.shape
    depth = ada_w.shape[0]
    topk = min(IDX_TOPK_MAX, T // 4)
    tm = min(512, T)
    r_dn = min(256, T)

    cond_in = jnp.zeros((8, D), F32).at[:B].set(c)
    pos = jnp.arange(tm)
    tri = ((pos[:, None] // CHUNK == pos[None, :] // CHUNK) & (pos[:, None] >= pos[None, :])).astype(F32)

    for l in range(depth):
        mod = _ada(cond_in, ada_w[l], ada_b[l][None, :])[:B]
        sh1, sc1, gt1, sh2, sc2, gt2 = [m[:, None, :] for m in jnp.split(mod, 6, axis=-1)]

        offs = [0]
        for s in (DN_QK, DN_QK, DN_V, DN_V, DN_HEADS, DN_HEADS, SA_Q, KV_RANK, IDX_Q, IDX_DIM, IDX_HEADS):
            offs.append(offs[-1] + s)
        w = w_in[l]
        wc = w[:, offs[0]:offs[3]].astype(BF16)
        wz = w[:, offs[3]:offs[4]].astype(BF16)
        wq = w[:, offs[6]:offs[7]].astype(BF16)
        wkv = w[:, offs[7]:offs[8]].astype(BF16)
        wqi = w[:, offs[8]:offs[9]].astype(BF16)
        wm = jnp.concatenate([w[:, offs[9]:offs[10]], w[:, offs[4]:offs[5]], w[:, offs[5]:offs[6]],
                              w[:, offs[10]:offs[11]],
                              jnp.zeros((D, MISC_W - IDX_DIM - 2 * DN_HEADS - IDX_HEADS), F32)],
                             axis=1).astype(BF16)
        ukt = jnp.swapaxes(w_uk[l], 1, 2).astype(BF16)

        q, k, v, z, qlat, ckv, qix, kix, misc = _inproj(
            x, sc1, sh1, norm1_g[l][None, :], wc, wz, wq, wkv, wqi, wm, conv_w[l], ukt,
            kv_norm_g[l][None, :], _misc_lanes(idx_k_ln_g[l], M_KIX), _misc_lanes(idx_k_ln_b[l], M_KIX),
            _misc_lanes(a_log[l], M_A), _misc_lanes(dt_bias[l], M_A), tri, tm)

        odn = _deltanet(q, k, v, z, misc, dn_norm_g[l][None, :], r_dn)
        olat = _dsa(qix, misc, kix, qlat, ckv, topk)

        x1, h2, posrow, poscol, gatecol, cpad = _outproj(
            x, odn, olat, w_uv[l].astype(BF16), w_out[l].astype(BF16), gt1, sc2, sh2,
            norm2_g[l][None, :], router_w[l].T, router_b[l][:, None], tm)

        x = _moe(h2, posrow, poscol, gatecol, cpad, exp_w_gate[l].astype(BF16), exp_w_up[l].astype(BF16),
                 exp_w_down[l].astype(BF16), sh_w_gate[l].astype(BF16), sh_w_up[l].astype(BF16),
                 sh_w_down[l].astype(BF16), x1, gt2, final_g[None, :], tm, l == depth - 1)
    return x
```

```python
import functools

import jax
import jax.numpy as jnp
from jax import lax
from jax.experimental import pallas as pl
from jax.experimental.pallas import tpu as pltpu

F32 = jnp.float32
BF16 = jnp.bfloat16
I32 = jnp.int32
HIGHEST = lax.Precision.HIGHEST

EPS = 1e-6
CHUNK = 64
DN_HEADS = 4
DN_DK = 128
DN_DV = 128
CONV_K = 4
SA_HEADS = 4
SA_DQK = 128
SA_DV = 128
KV_RANK = 256
IDX_HEADS = 4
IDX_DIM = 64
IDX_TOPK_MAX = 256
SM_SCALE = SA_DQK ** -0.5
LOG2E = 1.4426950408889634
IDX_W_SCALE = (IDX_HEADS * IDX_DIM) ** -0.5
N_EXPERTS = 64
TOP_K = 8
N_GROUPS = 8
TOPK_GROUPS = 4
D_EXPERT = 256
ROUTED_SCALE = 2.5
GATE_W = 128
RUN_ALIGN = 16
ROW_BLOCK = 512
MOE_CHUNK = 512

DN_QK = DN_HEADS * DN_DK
DN_V = DN_HEADS * DN_DV
CONV_DIM = 2 * DN_QK + DN_V
SA_Q = SA_HEADS * SA_DQK
IDX_Q = IDX_HEADS * IDX_DIM

MISC_W = 128
M_KIX = 0
M_BETA = IDX_DIM
M_A = M_BETA + DN_HEADS
M_WIX = M_A + DN_HEADS

DN_SUB = 2 * CHUNK
QBLOCK = 256
SOFTMAX_TINY = 2.0 ** -100
KEY_TILE = 1024
PLANE_COLS = 32 * 128
PLANE_SHIFT = 12
INT_MIN = -2 ** 31
NEG_BIG = -1e30
VMEM_LIMIT = 56 * 1024 * 1024


def _nt_dot(a, b, precision=None):
    return lax.dot_general(a, b, (((1,), (1,)), ((), ())), preferred_element_type=F32,
                           precision=precision)


def _dot(a, b, precision=None):
    return jnp.dot(a, b, preferred_element_type=F32, precision=precision)


def _silu(x):
    return x * jax.nn.sigmoid(x)


def _softplus(x):
    return jnp.maximum(x, 0.0) + jnp.log(1.0 + jnp.exp(-jnp.abs(x)))


def _ada_kernel(c_ref, w_ref, b_ref, o_ref):
    cond = _silu(c_ref[...])
    o_ref[...] = _dot(cond, w_ref[...], HIGHEST) + b_ref[...]


def _ada(c_pad, ada_w, ada_b):
    rows, d = c_pad.shape
    n_out = ada_w.shape[1]
    return pl.pallas_call(
        _ada_kernel,
        grid=(n_out // d,),
        in_specs=[pl.BlockSpec((rows, d), lambda j: (0, 0)),
                  pl.BlockSpec((d, d), lambda j: (0, j)),
                  pl.BlockSpec((1, d), lambda j: (0, j))],
        out_specs=pl.BlockSpec((rows, d), lambda j: (0, j)),
        out_shape=jax.ShapeDtypeStruct((rows, n_out), F32),
        compiler_params=pltpu.CompilerParams(vmem_limit_bytes=VMEM_LIMIT),
        name="ada",
    )(c_pad, ada_w, ada_b)


def _inproj_kernel(x_ref, sc_ref, sh_ref, g1_ref, wc_ref, wz_ref, wq_ref, wkv_ref, wqi_ref, wm_ref,
                   convw_ref, ukt_ref, kvg_ref, lng_ref, lnb_ref, alog_ref, dtb_ref, tri_ref,
                   q_ref, k_ref, v_ref, z_ref, qlat_ref, ckv_ref, qix_ref, kix_ref, misc_ref,
                   conv_buf):
    tm = x_ref.shape[1]
    i = pl.program_id(1)

    x = x_ref[0]
    h = x * lax.rsqrt(jnp.mean(x * x, axis=-1, keepdims=True) + EPS) * g1_ref[...]
    h = h * (1.0 + sc_ref[0]) + sh_ref[0]
    hb = h.astype(BF16)

    @pl.when(i == 0)
    def _():
        conv_buf[0:8, :] = jnp.zeros((8, CONV_DIM), F32)

    conv_buf[8:8 + tm, :] = _dot(hb, wc_ref[...])
    for grp, dst in ((0, q_ref), (1, k_ref), (2, v_ref)):
        cols = slice(grp * DN_QK, (grp + 1) * DN_QK)
        y = jnp.zeros((tm, DN_QK), F32)
        for j in range(CONV_K):
            y = y + convw_ref[j:j + 1, cols] * conv_buf[8 - (CONV_K - 1) + j:8 - (CONV_K - 1) + j + tm, cols]
        y = _silu(y)
        if grp < 2:
            outs = []
            for hd in range(DN_HEADS):
                yh = y[:, hd * DN_DK:(hd + 1) * DN_DK]
                yh = yh * lax.rsqrt(jnp.sum(yh * yh, axis=-1, keepdims=True) + EPS)
                if grp == 0:
                    yh = yh * (DN_DK ** -0.5)
                outs.append(yh)
            y = jnp.concatenate(outs, axis=-1)
        dst[0] = y
    conv_buf[0:8, :] = conv_buf[tm:tm + 8, :]

    z_ref[0] = _dot(hb, wz_ref[...])

    q_sa = _dot(hb, wq_ref[...]).astype(BF16)
    for hd in range(SA_HEADS):
        ql = _dot(q_sa[:, hd * SA_DQK:(hd + 1) * SA_DQK], ukt_ref[hd]) * (SM_SCALE * LOG2E)
        qlat_ref[0, hd] = ql.astype(BF16)

    ckv = _dot(hb, wkv_ref[...])
    ckv = ckv * lax.rsqrt(jnp.mean(ckv * ckv, axis=-1, keepdims=True) + EPS) * kvg_ref[...]
    ckv_ref[0] = ckv.astype(BF16)

    q_ix = _dot(hb, wqi_ref[...]).astype(BF16)
    for hd in range(IDX_HEADS):
        qix_ref[0, hd] = q_ix[:, hd * IDX_DIM:(hd + 1) * IDX_DIM]

    m = _dot(hb, wm_ref[...])
    lane = lax.broadcasted_iota(I32, (tm, MISC_W), 1)
    is_k = lane < IDX_DIM
    mu = jnp.sum(jnp.where(is_k, m, 0.0), axis=-1, keepdims=True) * (1.0 / IDX_DIM)
    kc = jnp.where(is_k, m - mu, 0.0)
    var = jnp.sum(kc * kc, axis=-1, keepdims=True) * (1.0 / IDX_DIM)
    kn = kc * lax.rsqrt(var + EPS) * lng_ref[...] + lnb_ref[...]
    kix_ref[0] = kn[:, :IDX_DIM].astype(BF16)

    beta = jax.nn.sigmoid(m)
    g = -jnp.exp(alog_ref[...]) * _softplus(m + dtb_ref[...])
    is_a = (lane >= M_A) & (lane < M_A + DN_HEADS)
    g = jnp.where(is_a, g, 0.0)
    gc = _dot(tri_ref[...], g, HIGHEST)
    is_b = (lane >= M_BETA) & (lane < M_BETA + DN_HEADS)
    is_w = (lane >= M_WIX) & (lane < M_WIX + IDX_HEADS)
    misc_ref[0] = jnp.where(is_b, beta, jnp.where(is_a, gc, jnp.where(is_w, m * IDX_W_SCALE, 0.0)))


def _inproj(x, sc1, sh1, g1, wc, wz, wq, wkv, wqi, wm, conv_w, ukt, kvg, lng, lnb, alog, dtb, tri, tm):
    B, T, D = x.shape
    nt = T // tm

    def full(a):
        nd = a.ndim
        return pl.BlockSpec(a.shape, lambda b, i, _n=nd: (0,) * _n)

    def rows(w):
        return pl.BlockSpec((1, tm, w), lambda b, i: (b, i, 0))

    per_b = pl.BlockSpec((1, 1, D), lambda b, i: (b, 0, 0))
    def head_rows(h, w):
        return pl.BlockSpec((1, h, tm, w), lambda b, i: (b, 0, i, 0))

    outs = [(None, DN_QK, F32), (None, DN_QK, F32), (None, DN_V, F32), (None, DN_V, F32),
            (SA_HEADS, KV_RANK, BF16), (None, KV_RANK, BF16), (IDX_HEADS, IDX_DIM, BF16),
            (None, IDX_DIM, BF16), (None, MISC_W, F32)]
    return pl.pallas_call(
        _inproj_kernel,
        grid=(B, nt),
        in_specs=[rows(D), per_b, per_b, full(g1), full(wc), full(wz), full(wq), full(wkv), full(wqi),
                  full(wm), full(conv_w), full(ukt), full(kvg), full(lng), full(lnb), full(alog),
                  full(dtb), full(tri)],
        out_specs=[rows(w) if h is None else head_rows(h, w) for h, w, _ in outs],
        out_shape=[jax.ShapeDtypeStruct((B, T, w) if h is None else (B, h, T, w), dt) for h, w, dt in outs],
        scratch_shapes=[pltpu.VMEM((tm + 8, CONV_DIM), F32)],
        compiler_params=pltpu.CompilerParams(dimension_semantics=("arbitrary", "arbitrary"),
                                             vmem_limit_bytes=VMEM_LIMIT),
        name="inproj",
    )(x, sc1, sh1, g1, wc, wz, wq, wkv, wqi, wm, conv_w, ukt, kvg, lng, lnb, alog, dtb, tri)


def _deltanet_kernel(q_ref, k_ref, v_ref, z_ref, misc_ref, ng_ref, o_ref, s_ref):
    R = q_ref.shape[1]
    n_chunks = R // CHUNK

    @pl.when(pl.program_id(1) == 0)
    def _():
        s_ref[...] = jnp.zeros(s_ref.shape, F32)

    misc = misc_ref[0]
    misc_t = misc.T
    SB = min(DN_SUB, R)
    row = lax.broadcasted_iota(I32, (SB, SB), 0)
    col = lax.broadcasted_iota(I32, (SB, SB), 1)
    same = (row // CHUNK) == (col // CHUNK)
    lower = same & (row >= col)
    strict = same & (row > col)
    eye = (row == col).astype(F32)

    def mm(a, b):
        return _dot(a.astype(BF16), b.astype(BF16))

    def mm3(a, b):
        ah = a.astype(BF16)
        bh = b.astype(BF16)
        al = (a - ah.astype(F32)).astype(BF16)
        bl = (b - bh.astype(F32)).astype(BF16)
        return _dot(jnp.concatenate([ah, ah, al], axis=1), jnp.concatenate([bh, bl, bh], axis=0))

    heads = range(DN_HEADS)
    subs = range(R // SB)
    chains = [(hd, sb) for hd in heads for sb in subs]
    cols = [slice(hd * DN_DK, (hd + 1) * DN_DK) for hd in heads]
    qh = [q_ref[0, :, cols[hd]] for hd in heads]
    kh = [k_ref[0, :, cols[hd]] for hd in heads]
    beta = [misc[:, M_BETA + hd:M_BETA + hd + 1] for hd in heads]
    gc_c = [misc[:, M_A + hd:M_A + hd + 1] for hd in heads]
    eg = [jnp.exp(gc_c[hd]) for hd in heads]
    kb = [kh[hd] * beta[hd] for hd in heads]
    rhs = [jnp.concatenate([v_ref[0, :, cols[hd]] * beta[hd], kb[hd] * eg[hd]], axis=-1) for hd in heads]
    q_dec = [qh[hd] * eg[hd] for hd in heads]

    def rows_of(sb):
        return slice(sb * SB, (sb + 1) * SB)

    decay, a, qk_sb = {}, {}, {}
    for hd, sb in chains:
        bs = rows_of(sb)
        gc_r = misc_t[M_A + hd:M_A + hd + 1, bs]
        decay[hd, sb] = jnp.where(lower, jnp.exp(jnp.where(lower, gc_c[hd][bs] - gc_r, 0.0)), 0.0)
    for hd, sb in chains:
        bs = rows_of(sb)
        khb = kh[hd][bs].astype(BF16)
        a[hd, sb] = jnp.where(strict, _nt_dot(kb[hd][bs].astype(BF16), khb) * decay[hd, sb], 0.0)
        qk_sb[hd, sb] = jnp.where(lower, _nt_dot(qh[hd][bs].astype(BF16), khb) * decay[hd, sb], 0.0)
    p = {ch: eye - a[ch] for ch in chains}
    xp = {ch: mm3(a[ch], a[ch]) for ch in chains}
    n_sq = 1
    while True:
        p = {ch: p[ch] + mm3(p[ch], xp[ch]) for ch in chains}
        n_sq *= 2
        if n_sq * 2 >= CHUNK:
            break
        xp = {ch: mm3(xp[ch], xp[ch]) for ch in chains}
    sol = {(hd, sb): mm(p[hd, sb], rhs[hd][rows_of(sb)]) for hd, sb in chains}

    def chunk_of(c):
        per = SB // CHUNK
        return c // per, slice((c % per) * CHUNK, (c % per + 1) * CHUNK)

    s = [s_ref[hd] for hd in heads]
    o_parts = [[] for _ in heads]
    for c in range(n_chunks):
        rs = slice(c * CHUNK, (c + 1) * CHUNK)
        sb, r = chunk_of(c)
        gl = [gc_c[hd][(c + 1) * CHUNK - 1:(c + 1) * CHUNK, :] for hd in heads]
        k_dec = [kh[hd][rs] * jnp.exp(gl[hd] - gc_c[hd][rs]) for hd in heads]
        v_new = [sol[hd, sb][r, :DN_DV] - mm(sol[hd, sb][r, DN_DV:], s[hd]) for hd in heads]
        for hd in heads:
            o_parts[hd].append(mm(q_dec[hd][rs], s[hd]) + mm(qk_sb[hd, sb][r, r], v_new[hd]))
        s = [s[hd] * jnp.exp(gl[hd]) + mm(k_dec[hd].T, v_new[hd]) for hd in heads]
    for hd in heads:
        s_ref[hd] = s[hd]
        o = jnp.concatenate(o_parts[hd], axis=0)
        o = o * lax.rsqrt(jnp.mean(o * o, axis=-1, keepdims=True) + EPS) * ng_ref[...]
        o_ref[0, :, cols[hd]] = (o * _silu(z_ref[0, :, cols[hd]])).astype(BF16)


def _deltanet(q, k, v, z, misc, ng, R):
    B, T, _ = q.shape

    def rows(w):
        return pl.BlockSpec((1, R, w), lambda b, i: (b, i, 0))

    return pl.pallas_call(
        _deltanet_kernel,
        grid=(B, T // R),
        in_specs=[rows(DN_QK), rows(DN_QK), rows(DN_V), rows(DN_V), rows(MISC_W),
                  pl.BlockSpec((1, DN_DV), lambda b, i: (0, 0))],
        out_specs=rows(DN_V),
        out_shape=jax.ShapeDtypeStruct((B, T, DN_V), BF16),
        scratch_shapes=[pltpu.VMEM((DN_HEADS, DN_DK, DN_DV), F32)],
        compiler_params=pltpu.CompilerParams(dimension_semantics=("arbitrary", "arbitrary"),
                                             vmem_limit_bytes=VMEM_LIMIT),
        name="deltanet",
    )(q, k, v, z, misc, ng)


def _bit_transpose32(words):
    w = list(words)
    j = 16
    m = 0x0000FFFF
    while j:
        k = 0
        m_i32 = jnp.int32(m - (1 << 32) if m >= (1 << 31) else m)
        while k < 32:
            t = (w[k] ^ lax.shift_right_logical(w[k + j], jnp.full_like(w[k], j))) & m_i32
            w[k] = w[k] ^ t
            w[k + j] = w[k + j] ^ jnp.left_shift(t, j)
            k = (k + j + 1) & ~j
        j >>= 1
        m = (m ^ (m << j)) & 0xFFFFFFFF
    return w


def _dsa_kernel(qix_ref, misc_ref, kix_ref, qlat_ref, ckv_ref, o_ref, keys_ref, planes_ref, bias_ref,
                mx_ref, l_ref, acc_ref, kvmax_ref, *, topk, pos_bits, n_cg_max):
    i = pl.program_id(1)
    QB = QBLOCK
    KT = KEY_TILE
    n_kt = (i * QB + QB + KT - 1) // KT

    rowi = lax.broadcasted_iota(I32, (QB, KT), 0)
    coli = lax.broadcasted_iota(I32, (QB, KT), 1)
    limit = i * QB + (rowi // CHUNK + 1) * CHUNK

    misc = misc_ref[0]
    qix = qix_ref[0].reshape(IDX_HEADS * QB, IDX_DIM)
    q_st = qlat_ref[0].reshape(SA_HEADS * QB, KV_RANK)

    def score_body(kt, carry):
        k0 = pl.multiple_of(kt * KT, KT)
        kx = kix_ref[0, pl.ds(k0, KT), :]
        rel = jnp.maximum(_nt_dot(qix, kx), 0.0)
        sc = jnp.zeros((QB, KT), F32)
        for hd in range(IDX_HEADS):
            sc = sc + misc[:, M_WIX + hd:M_WIX + hd + 1] * rel[hd * QB:(hd + 1) * QB]
        sc = jnp.where(sc == 0.0, 0.0, sc)
        bits = pltpu.bitcast(sc, I32)
        key = jnp.where(bits < 0, bits ^ 0x7FFFFFFF, bits)
        keys_ref[:, pl.ds(k0, KT)] = jnp.where(k0 + coli < limit, key, INT_MIN)
        return carry

    lax.fori_loop(0, n_kt, score_body, 0)

    n_cg = (n_kt * KT + PLANE_COLS - 1) // PLANE_COLS

    def fill_body(kt, carry):
        keys_ref[:, pl.ds(pl.multiple_of(kt * KT, KT), KT)] = jnp.full((QB, KT), INT_MIN, I32)
        return carry

    lax.fori_loop(n_kt, n_cg * (PLANE_COLS // KT), fill_body, 0)

    @pl.when(i == 0)
    def _():
        planes_ref[...] = jnp.zeros(planes_ref.shape, I32)

    def plane_body(step, carry):
        c = step // (QB // 8)
        r0 = pl.multiple_of((step % (QB // 8)) * 8, 8)
        words = [keys_ref[pl.ds(r0, 8), pl.ds(pl.multiple_of(c * PLANE_COLS + j * 128, 128), 128)]
                 for j in range(32)]
        for b, plane in enumerate(_bit_transpose32(words)):
            planes_ref[c, b, pl.ds(r0, 8), :] = ~plane if b == 0 else plane
        return carry

    lax.fori_loop(0, n_cg * (QB // 8), plane_body, 0)

    def lane_count(words):
        pc = functools.reduce(jnp.add, [lax.population_count(x) for x in words])
        return jnp.sum(pc, axis=1, keepdims=True)

    def sel_body(step, carry):
        cand, n_gt, tau_u = carry
        groups = range(n_cg_max)
        hi = [cand[c] & planes_ref[c, 2 * step] for c in groups]
        lo = [cand[c] ^ hi[c] for c in groups]
        d3 = [hi[c] & planes_ref[c, 2 * step + 1] for c in groups]
        d2 = [hi[c] ^ d3[c] for c in groups]
        d1 = [lo[c] & planes_ref[c, 2 * step + 1] for c in groups]
        d0 = [lo[c] ^ d1[c] for c in groups]
        a3 = n_gt + lane_count(d3)
        a2 = a3 + lane_count(d2)
        a1 = a2 + lane_count(d1)
        is3 = a3 >= topk
        is2 = a2 >= topk
        is1 = a1 >= topk
        cand = tuple(jnp.where(is3, d3[c], jnp.where(is2, d2[c], jnp.where(is1, d1[c], d0[c]))) for c in groups)
        n_gt = jnp.where(is3, n_gt, jnp.where(is2, a3, jnp.where(is1, a2, a1)))
        digit = jnp.where(is3, 3, jnp.where(is2, 2, jnp.where(is1, 1, 0)))
        return cand, n_gt, tau_u | jnp.left_shift(digit, 30 - 2 * step)

    cand0 = tuple(jnp.where(c < n_cg, jnp.full((QB, 128), -1, I32), 0) for c in range(n_cg_max))
    cand, n_gt, tau_u = lax.fori_loop(
        0, 16, sel_body, (cand0, jnp.zeros((QB, 1), I32), jnp.zeros((QB, 1), I32)))
    tau = tau_u ^ INT_MIN
    sentinel = tau == INT_MIN
    cand = tuple(jnp.where(sentinel, 0, cand[c]) for c in range(n_cg_max))
    need = topk - n_gt
    any_tie = jnp.max(jnp.where(lane_count(cand) > need, 1, 0)) > 0

    def fast_bias():
        floor = jnp.where(sentinel, INT_MIN + 1, tau)

        def body(kt, carry):
            k0 = pl.multiple_of(kt * KT, KT)
            bias_ref[:, pl.ds(k0, KT)] = jnp.where(keys_ref[:, pl.ds(k0, KT)] >= floor, 0.0, NEG_BIG)
            return carry

        lax.fori_loop(0, n_kt, body, 0)

    def tie_bias():
        lane = lax.broadcasted_iota(I32, (QB, 128), 1)

        def pos_mask(p, c):
            cg = lax.shift_right_logical(p, jnp.full_like(p, PLANE_SHIFT))
            j0 = lax.shift_right_logical(p, jnp.full_like(p, 7)) & 31
            below = ~lax.shift_right_logical(jnp.full_like(p, -1), j0)
            bit = lax.shift_right_logical(jnp.full_like(p, INT_MIN), j0)
            word = below | jnp.where(lane < (p & 127), bit, 0)
            return jnp.where(cg > c, -1, jnp.where(cg == c, word, 0))

        def pos_body(b, q):
            cq = q + jnp.left_shift(jnp.int32(1), pos_bits - 1 - b)
            cnt = lane_count([cand[c] & pos_mask(cq, c) for c in range(n_cg_max)])
            return jnp.where(cnt < need, cq, q)

        pstar = lax.fori_loop(0, pos_bits, pos_body, jnp.zeros((QB, 1), I32)) + 1
        pstar = jnp.where(sentinel, 0, pstar)

        def body(kt, carry):
            k0 = pl.multiple_of(kt * KT, KT)
            kk = keys_ref[:, pl.ds(k0, KT)]
            tie = jnp.where(k0 + coli < pstar, 0.0, NEG_BIG)
            bias_ref[:, pl.ds(k0, KT)] = jnp.where(kk > tau, 0.0, jnp.where(kk == tau, tie, NEG_BIG))
            return carry

        lax.fori_loop(0, n_kt, body, 0)

    lax.cond(any_tie, tie_bias, fast_bias)

    def logit_chunks(k0):
        kv = ckv_ref[0, pl.ds(k0, KT), :]
        s = _nt_dot(q_st, kv)
        bias = bias_ref[:, pl.ds(k0, KT)]
        chunks = []
        for j in range(KT // 128):
            bj = bias[:, j * 128:(j + 1) * 128]
            chunks.append(s[:, j * 128:(j + 1) * 128] + jnp.concatenate([bj] * SA_HEADS, axis=0))
        return kv, chunks

    def weighted_sum_sweep():
        l_ref[...] = jnp.zeros(l_ref.shape, F32)
        acc_ref[...] = jnp.zeros(acc_ref.shape, F32)

        def pv_body(kt, carry):
            kv, chunks = logit_chunks(pl.multiple_of(kt * KT, KT))
            shift = mx_ref[...]
            ps = [jnp.exp2(c - shift) for c in chunks]
            l_ref[...] = functools.reduce(jnp.add, ps, l_ref[...])
            p = jnp.concatenate([pj.astype(BF16) for pj in ps], axis=1)
            acc_ref[...] += _dot(p, kv)
            return carry

        lax.fori_loop(0, n_kt, pv_body, 0)
        l_row = jnp.sum(l_ref[...], axis=-1, keepdims=True)
        o_ref[0] = (acc_ref[...] / l_row).astype(BF16).reshape(SA_HEADS, QB, KV_RANK)
        return l_row

    @pl.when(i == 0)
    def _():
        def norm_body(r, best):
            x = ckv_ref[0, pl.ds(pl.multiple_of(r * KT, KT), KT), :].astype(F32)
            return jnp.maximum(best, jnp.max(jnp.sum(x * x, axis=1, keepdims=True), axis=0, keepdims=True))

        n_rows = ckv_ref.shape[1]
        kv_sq = lax.fori_loop(0, n_rows // KT, norm_body, jnp.zeros((1, 1), F32))
        kvmax_ref[...] = jnp.broadcast_to(jnp.sqrt(kv_sq), kvmax_ref.shape)

    qf = q_st.astype(F32)
    q_norm = jnp.sqrt(jnp.sum(qf * qf, axis=1, keepdims=True))
    mx_ref[...] = jnp.broadcast_to(q_norm, mx_ref.shape) * kvmax_ref[0:1, :] * 1.001 + 1e-3
    l_fast = weighted_sum_sweep()

    @pl.when(jnp.min(l_fast) < SOFTMAX_TINY)
    def _():
        mx_ref[...] = jnp.full(mx_ref.shape, NEG_BIG, F32)

        def max_body(kt, carry):
            _, chunks = logit_chunks(pl.multiple_of(kt * KT, KT))
            mx_ref[...] = functools.reduce(jnp.maximum, chunks, mx_ref[...])
            return carry

        lax.fori_loop(0, n_kt, max_body, 0)
        mx_ref[...] = jnp.broadcast_to(jnp.max(mx_ref[...], axis=-1, keepdims=True), mx_ref.shape)
        weighted_sum_sweep()


def _dsa(qix, misc, kix, qlat, ckv, topk):
    B, T, _ = kix.shape
    n_cg_max = -(-T // PLANE_COLS)
    t_pad = n_cg_max * PLANE_COLS
    pos_bits = (t_pad - 1).bit_length()

    def rows(w):
        return pl.BlockSpec((1, QBLOCK, w), lambda b, i: (b, i, 0))

    def head_rows(h, w):
        return pl.BlockSpec((1, h, QBLOCK, w), lambda b, i: (b, 0, i, 0))

    def per_b(w):
        return pl.BlockSpec((1, T, w), lambda b, i: (b, 0, 0))

    return pl.pallas_call(
        functools.partial(_dsa_kernel, topk=topk, pos_bits=pos_bits, n_cg_max=n_cg_max),
        grid=(B, T // QBLOCK),
        in_specs=[head_rows(IDX_HEADS, IDX_DIM), rows(MISC_W), per_b(IDX_DIM), head_rows(SA_HEADS, KV_RANK),
                  per_b(KV_RANK)],
        out_specs=head_rows(SA_HEADS, KV_RANK),
        out_shape=jax.ShapeDtypeStruct((B, SA_HEADS, T, KV_RANK), BF16),
        scratch_shapes=[pltpu.VMEM((QBLOCK, t_pad), I32),
                        pltpu.VMEM((n_cg_max, 32, QBLOCK, 128), I32),
                        pltpu.VMEM((QBLOCK, t_pad), F32),
                        pltpu.VMEM((SA_HEADS * QBLOCK, 128), F32),
                        pltpu.VMEM((SA_HEADS * QBLOCK, 128), F32),
                        pltpu.VMEM((SA_HEADS * QBLOCK, KV_RANK), F32),
                        pltpu.VMEM((8, 128), F32)],
        compiler_params=pltpu.CompilerParams(dimension_semantics=("arbitrary", "arbitrary"),
                                             vmem_limit_bytes=VMEM_LIMIT),
        name="dsa",
    )(qix, misc, kix, qlat, ckv)


def _first_max(v, idx, axis):
    m = jnp.max(v, axis=axis, keepdims=True)
    big = jnp.int32(2 ** 30)
    first = jnp.min(jnp.where(v == m, idx, big), axis=axis, keepdims=True)
    return m, idx == first


def _outproj_kernel(x_ref, odn_ref, olat_ref, uv_ref, wo_ref, gt_ref, sc_ref, sh_ref, g2_ref, rwt_ref,
                    rb_ref, lstrict_ref, ustrict_ref, x1_ref, h2_ref, posrow_ref, poscol_ref, gatecol_ref,
                    cpad_ref):
    tm = x_ref.shape[1]
    parts = [odn_ref[0]]
    for hd in range(SA_HEADS):
        parts.append(_dot(olat_ref[0, hd], uv_ref[hd]).astype(BF16))
    mix = jnp.concatenate(parts, axis=-1)
    x1 = x_ref[0] + gt_ref[0] * _dot(mix, wo_ref[...])
    x1_ref[0] = x1
    h2 = x1 * lax.rsqrt(jnp.mean(x1 * x1, axis=-1, keepdims=True) + EPS) * g2_ref[...]
    h2 = h2 * (1.0 + sc_ref[0]) + sh_ref[0]
    h2_ref[0] = h2.astype(BF16)

    per_g = N_EXPERTS // N_GROUPS
    s = jax.nn.sigmoid(_nt_dot(rwt_ref[...], h2, HIGHEST))
    choice = s + rb_ref[...]
    ig = lax.broadcasted_iota(I32, (per_g, tm), 0)
    gscore = []
    for gidx in range(N_GROUPS):
        cg = choice[gidx * per_g:(gidx + 1) * per_g]
        m1, hot1 = _first_max(cg, ig, 0)
        gscore.append(m1 + jnp.max(jnp.where(hot1, -jnp.inf, cg), axis=0, keepdims=True))
    gsel = [jnp.zeros((1, tm), jnp.bool_) for _ in range(N_GROUPS)]
    for _ in range(TOPK_GROUPS):
        best = functools.reduce(jnp.maximum, gscore)
        found = jnp.zeros((1, tm), jnp.bool_)
        for gidx in range(N_GROUPS):
            hot = (gscore[gidx] == best) & jnp.logical_not(found)
            found = found | hot
            gsel[gidx] = gsel[gidx] | hot
            gscore[gidx] = jnp.where(hot, -jnp.inf, gscore[gidx])
    masked = jnp.concatenate(
        [jnp.where(gsel[gidx], choice[gidx * per_g:(gidx + 1) * per_g], -jnp.inf) for gidx in range(N_GROUPS)],
        axis=0)
    ei = lax.broadcasted_iota(I32, masked.shape, 0)
    gate = jnp.zeros(masked.shape, F32)
    hots = []
    for _ in range(TOP_K):
        _, hot = _first_max(masked, ei, 0)
        hots.append(hot)
        gate = jnp.where(hot, s, gate)
        masked = jnp.where(hot, -jnp.inf, masked)
    gate = gate / jnp.sum(gate, axis=0, keepdims=True) * ROUTED_SCALE

    picked = jnp.where(functools.reduce(jnp.logical_or, hots), 1.0, 0.0)
    cnt = jnp.sum(picked, axis=1, keepdims=True)
    cpad = jnp.floor((cnt + (RUN_ALIGN - 1)) * (1.0 / RUN_ALIGN)) * RUN_ALIGN
    cpad_b = jnp.broadcast_to(cpad, (N_EXPERTS, GATE_W))
    lbase = _dot(lstrict_ref[...], cpad_b, HIGHEST)[:, :1]
    rank = _dot(picked.astype(BF16), ustrict_ref[...])
    pos = lbase + rank
    ri = lax.broadcasted_iota(I32, (GATE_W, tm), 0)
    pos_rows = jnp.zeros((GATE_W, tm), F32)
    gate_rows = jnp.zeros((GATE_W, tm), F32)
    for k, hot in enumerate(hots):
        pos_rows = jnp.where(ri == k, jnp.sum(jnp.where(hot, pos, 0.0), axis=0, keepdims=True), pos_rows)
        gate_rows = jnp.where(ri == k, jnp.sum(jnp.where(hot, gate, 0.0), axis=0, keepdims=True), gate_rows)
    posrow_ref[0, 0] = pos_rows[:TOP_K].astype(I32)
    poscol_ref[0] = pos_rows.T.astype(I32)
    gatecol_ref[0] = gate_rows.T
    cpad_ref[0, 0] = cpad_b.astype(I32)


def _outproj(x, odn, olat, uv, wo, gt1, sc2, sh2, g2, rwt, rb, tm):
    B, T, D = x.shape
    nt = T // tm
    ex = jnp.arange(N_EXPERTS)
    lstrict = (ex[:, None] > ex[None, :]).astype(F32)
    tok = jnp.arange(tm)
    ustrict = (tok[:, None] < tok[None, :]).astype(BF16)

    def full(a):
        nd = a.ndim
        return pl.BlockSpec(a.shape, lambda b, i, _n=nd: (0,) * _n)

    def rows(w):
        return pl.BlockSpec((1, tm, w), lambda b, i: (b, i, 0))

    def per_tile(h, w):
        return pl.BlockSpec((1, 1, h, w), lambda b, i: (b, i, 0, 0))

    per_b = pl.BlockSpec((1, 1, D), lambda b, i: (b, 0, 0))
    return pl.pallas_call(
        _outproj_kernel,
        grid=(B, nt),
        in_specs=[rows(D), rows(DN_V),
                  pl.BlockSpec((1, SA_HEADS, tm, KV_RANK), lambda b, i: (b, 0, i, 0)),
                  full(uv), full(wo), per_b, per_b, per_b,
                  full(g2), full(rwt), full(rb), full(lstrict), full(ustrict)],
        out_specs=[rows(D), rows(D), per_tile(TOP_K, tm), rows(GATE_W), rows(GATE_W),
                   per_tile(N_EXPERTS, GATE_W)],
        out_shape=[jax.ShapeDtypeStruct((B, T, D), F32), jax.ShapeDtypeStruct((B, T, D), BF16),
                   jax.ShapeDtypeStruct((B, nt, TOP_K, tm), I32),
                   jax.ShapeDtypeStruct((B, T, GATE_W), I32),
                   jax.ShapeDtypeStruct((B, T, GATE_W), F32),
                   jax.ShapeDtypeStruct((B, nt, N_EXPERTS, GATE_W), I32)],
        compiler_params=pltpu.CompilerParams(dimension_semantics=("arbitrary", "arbitrary"),
                                             vmem_limit_bytes=VMEM_LIMIT),
        name="outproj",
    )(x, odn, olat, uv, wo, gt1, sc2, sh2, g2, rwt, rb, lstrict, ustrict)


def _piece_sizes(max_rows):
    sizes = []
    z = RUN_ALIGN
    while z <= max_rows:
        sizes.append(z)
        z *= 2
    return sizes[::-1]


def _for_run_pieces(length, max_rows, fn):
    for z in _piece_sizes(max_rows):
        start = length & ~(2 * z - 1)

        @pl.when((length & z) != 0)
        def _(start=start, z=z):
            fn(start, z)


def _plan_kernel(cp_ref, off_ref, lb_ref, foff_ref, flen_ref, blk_ref, nused_ref):
    cp = cp_ref[...].astype(F32)
    n, ne = cp.shape
    ei = lax.broadcasted_iota(I32, (ne, ne), 0)
    ej = lax.broadcasted_iota(I32, (ne, ne), 1)
    si = lax.broadcasted_iota(I32, (n, n), 0)
    sj = lax.broadcasted_iota(I32, (n, n), 1)
    lb = _dot(cp, (ei < ej).astype(F32), HIGHEST)
    earlier_tiles = _dot((sj < si).astype(F32), cp, HIGHEST)
    rows_e = jnp.sum(cp, axis=0, keepdims=True)
    region = jnp.floor((rows_e + (ROW_BLOCK - 1)) * (1.0 / ROW_BLOCK)) * ROW_BLOCK
    region_b = jnp.broadcast_to(region, (ne, ne))
    rend_row = _dot(region_b, (ei <= ej).astype(F32), HIGHEST)[:1]
    rend_col = jnp.sum(jnp.where(ej <= ei, region_b, 0.0), axis=1, keepdims=True)
    base = rend_row - region
    total = jnp.max(rend_row, axis=1, keepdims=True)
    off_ref[...] = (base + earlier_tiles).astype(I32)
    lb_ref[...] = lb.astype(I32)
    lane = lax.broadcasted_iota(I32, (1, GATE_W), 1)
    pad = jnp.zeros((1, GATE_W - ne), F32)
    foff_ref[...] = jnp.where(lane == ne, total, jnp.concatenate([base + rows_e, pad], axis=1)).astype(I32)
    flen_ref[...] = jnp.concatenate([region - rows_e, pad], axis=1).astype(I32)
    n_used = total * (1.0 / ROW_BLOCK)
    nused_ref[...] = jnp.broadcast_to(n_used, nused_ref.shape).astype(I32)
    bi = lax.broadcasted_iota(I32, (ne, blk_ref.shape[1]), 1).astype(F32)
    ended = jnp.where(rend_col * (1.0 / ROW_BLOCK) <= jnp.minimum(bi, n_used - 1.0), 1.0, 0.0)
    blk_ref[...] = jnp.minimum(jnp.sum(ended, axis=0, keepdims=True), ne - 1.0).astype(I32)


def _plan(cp, n_blocks):
    n, ne = cp.shape
    nb_pad = -(-n_blocks // 128) * 128
    return pl.pallas_call(
        _plan_kernel,
        out_shape=[jax.ShapeDtypeStruct((n, ne), I32), jax.ShapeDtypeStruct((n, ne), I32),
                   jax.ShapeDtypeStruct((1, GATE_W), I32), jax.ShapeDtypeStruct((1, GATE_W), I32),
                   jax.ShapeDtypeStruct((1, nb_pad), I32), jax.ShapeDtypeStruct((1, GATE_W), I32)],
        name="moe_plan",
    )(cp)


def _dispatch_kernel(off_ref, cp_ref, lb_ref, foff_ref, flen_ref, h_ref, posrow_ref, xs_hbm, buf, zbuf, sem,
                     zsem, *, n_steps, tile):
    s = pl.program_id(0)
    slot = s % 2
    jmax = buf.shape[1]

    def run_copies(step, slot_, act):
        def body(e, carry):
            idx = step * N_EXPERTS + e
            lb = lb_ref[idx]
            of = off_ref[idx]

            def piece(start, z):
                act(pltpu.make_async_copy(
                    buf.at[slot_, pl.ds(pl.multiple_of(lb + start, RUN_ALIGN), z)],
                    xs_hbm.at[pl.ds(pl.multiple_of(of + start, RUN_ALIGN), z)], sem.at[slot_]))

            _for_run_pieces(cp_ref[idx], tile, piece)
            return carry

        lax.fori_loop(0, N_EXPERTS, body, 0)

    def fill_copies(act):
        def body(e, carry):
            fo = foff_ref[e]

            def piece(start, z):
                act(pltpu.make_async_copy(
                    zbuf.at[pl.ds(0, z)], xs_hbm.at[pl.ds(pl.multiple_of(fo + start, RUN_ALIGN), z)], zsem.at[0]))

            _for_run_pieces(flen_ref[e], ROW_BLOCK // 2, piece)
            return carry

        lax.fori_loop(0, N_EXPERTS, body, 0)

        def tail(r, carry):
            act(pltpu.make_async_copy(
                zbuf, xs_hbm.at[pl.ds(pl.multiple_of(foff_ref[N_EXPERTS] + r * zbuf.shape[0], RUN_ALIGN),
                                      zbuf.shape[0])], zsem.at[0]))
            return carry

        lax.fori_loop(0, (xs_hbm.shape[0] - foff_ref[N_EXPERTS]) // zbuf.shape[0], tail, 0)

    @pl.when(s == 0)
    def _():
        zbuf[...] = jnp.zeros(zbuf.shape, BF16)
        fill_copies(lambda c: c.start())

    @pl.when(s >= 2)
    def _():
        run_copies(s - 2, slot, lambda c: c.wait())

    h = h_ref[...]
    last = s * N_EXPERTS + N_EXPERTS - 1
    jused = lb_ref[last] + cp_ref[last]
    for jc in range(jmax // MOE_CHUNK):
        @pl.when(jc * MOE_CHUNK < jused)
        def _(jc=jc):
            ji = lax.broadcasted_iota(I32, (MOE_CHUNK, tile), 0) + jc * MOE_CHUNK
            p = jnp.zeros((MOE_CHUNK, tile), F32)
            for k in range(TOP_K):
                p = jnp.where(ji == posrow_ref[0, k:k + 1, :], 1.0, p)
            buf[slot, jc * MOE_CHUNK:(jc + 1) * MOE_CHUNK, :] = _dot(p.astype(BF16), h).astype(BF16)

    run_copies(s, slot, lambda c: c.start())

    @pl.when(s == n_steps - 1)
    def _():
        if n_steps >= 2:
            run_copies(s - 1, 1 - slot, lambda c: c.wait())
        run_copies(s, slot, lambda c: c.wait())
        fill_copies(lambda c: c.wait())


def _dispatch(h2, posrow, off, cp, lb, foff, flen, cap, tile, jmax):
    n_tok, D = h2.shape
    n_steps = n_tok // tile
    return pl.pallas_call(
        functools.partial(_dispatch_kernel, n_steps=n_steps, tile=tile),
        grid_spec=pltpu.PrefetchScalarGridSpec(
            num_scalar_prefetch=5,
            grid=(n_steps,),
            in_specs=[pl.BlockSpec((tile, D), lambda s, *_: (s, 0)),
                      pl.BlockSpec((1, TOP_K, tile), lambda s, *_: (s, 0, 0))],
            out_specs=pl.BlockSpec(memory_space=pl.ANY),
            scratch_shapes=[pltpu.VMEM((2, jmax, D), BF16), pltpu.VMEM((ROW_BLOCK // 2, D), BF16),
                            pltpu.SemaphoreType.DMA((2,)), pltpu.SemaphoreType.DMA((1,))]),
        out_shape=jax.ShapeDtypeStruct((cap, D), BF16),
        compiler_params=pltpu.CompilerParams(dimension_semantics=("arbitrary",), vmem_limit_bytes=VMEM_LIMIT),
        name="moe_dispatch",
    )(off, cp, lb, foff, flen, h2, posrow)


def _expert_kernel(blk_e_ref, nused_ref, xs_ref, wg_ref, wu_ref, wd_ref, ys_ref):
    used = pl.program_id(0) < nused_ref[0]

    @pl.when(used)
    def _():
        xb = xs_ref[...]
        a = _silu(_dot(xb, wg_ref[0])) * _dot(xb, wu_ref[0])
        ys_ref[...] = _dot(a.astype(BF16), wd_ref[0]).astype(BF16)

    @pl.when(jnp.logical_not(used))
    def _():
        ys_ref[...] = jnp.zeros(ys_ref.shape, BF16)


def _experts(xs, blk_e, n_used, wg, wu, wd):
    cap, D = xs.shape

    def row_block(i, be, nu):
        return (jnp.minimum(i, nu[0] - 1), 0)

    def out_block(i, be, nu):
        return (i, 0)

    def weight(i, be, nu):
        return (be[i], 0, 0)

    return pl.pallas_call(
        _expert_kernel,
        grid_spec=pltpu.PrefetchScalarGridSpec(
            num_scalar_prefetch=2,
            grid=(cap // ROW_BLOCK,),
            in_specs=[pl.BlockSpec((ROW_BLOCK, D), row_block),
                      pl.BlockSpec((1, D, D_EXPERT), weight), pl.BlockSpec((1, D, D_EXPERT), weight),
                      pl.BlockSpec((1, D_EXPERT, D), weight)],
            out_specs=pl.BlockSpec((ROW_BLOCK, D), out_block)),
        out_shape=jax.ShapeDtypeStruct((cap, D), BF16),
        compiler_params=pltpu.CompilerParams(dimension_semantics=("arbitrary",), vmem_limit_bytes=VMEM_LIMIT),
        name="moe_experts",
    )(blk_e, n_used, xs, wg, wu, wd)


def _combine_kernel(off_ref, cp_ref, lb_ref, ys_hbm, poscol_ref, gatecol_ref, h_ref, sg_ref, su_ref, sd_ref,
                    x1_ref, gt_ref, fg_ref, o_ref, buf, sem, acc_ref, *, n_steps, tile, final_norm):
    s = pl.program_id(0)
    slot = s % 2
    jmax = buf.shape[1]

    def run_copies(step, slot_, act):
        def body(e, carry):
            idx = step * N_EXPERTS + e
            lb = lb_ref[idx]
            of = off_ref[idx]

            def piece(start, z):
                act(pltpu.make_async_copy(
                    ys_hbm.at[pl.ds(pl.multiple_of(of + start, RUN_ALIGN), z)],
                    buf.at[slot_, pl.ds(pl.multiple_of(lb + start, RUN_ALIGN), z)], sem.at[slot_]))

            _for_run_pieces(cp_ref[idx], tile, piece)
            return carry

        lax.fori_loop(0, N_EXPERTS, body, 0)

    @pl.when(s == 0)
    def _():
        run_copies(0, 0, lambda c: c.start())

    @pl.when(s + 1 < n_steps)
    def _():
        run_copies(s + 1, 1 - slot, lambda c: c.start())

    hb = h_ref[...]
    shared = (_silu(_dot(hb, sg_ref[...])) * _dot(hb, su_ref[...])).astype(BF16)
    acc_ref[...] = _dot(shared, sd_ref[...])

    run_copies(s, slot, lambda c: c.wait())
    last = s * N_EXPERTS + N_EXPERTS - 1
    jused = lb_ref[last] + cp_ref[last]

    def zero_body(r, carry):
        buf[slot, pl.ds(pl.multiple_of(jused + r * RUN_ALIGN, RUN_ALIGN), RUN_ALIGN), :] = jnp.zeros(
            (RUN_ALIGN, buf.shape[2]), BF16)
        return carry

    chunk_end = (jused + MOE_CHUNK - 1) // MOE_CHUNK * MOE_CHUNK
    lax.fori_loop(0, (chunk_end - jused) // RUN_ALIGN, zero_body, 0)

    for jc in range(jmax // MOE_CHUNK):
        @pl.when(jc * MOE_CHUNK < jused)
        def _(jc=jc):
            ji = lax.broadcasted_iota(I32, (tile, MOE_CHUNK), 1) + jc * MOE_CHUNK
            g = jnp.zeros((tile, MOE_CHUNK), F32)
            for k in range(TOP_K):
                g = jnp.where(ji == poscol_ref[:, k:k + 1], gatecol_ref[:, k:k + 1], g)
            acc_ref[...] += _dot(g.astype(BF16), buf[slot, jc * MOE_CHUNK:(jc + 1) * MOE_CHUNK, :])

    y = x1_ref[...] + gt_ref[0] * acc_ref[...]
    if final_norm:
        y = y * lax.rsqrt(jnp.mean(y * y, axis=-1, keepdims=True) + EPS) * fg_ref[...]
    o_ref[...] = y


def _combine(ys, poscol, gatecol, h2, sg, su, sd, x1, gt2, fg, off, cp, lb, tile, jmax, tiles_per_batch,
             final_norm):
    n_tok, D = h2.shape
    n_steps = n_tok // tile

    def full(a):
        nd = a.ndim
        return pl.BlockSpec(a.shape, lambda s, *_, _n=nd: (0,) * _n)

    def rows(w):
        return pl.BlockSpec((tile, w), lambda s, *_: (s, 0))

    return pl.pallas_call(
        functools.partial(_combine_kernel, n_steps=n_steps, tile=tile, final_norm=final_norm),
        grid_spec=pltpu.PrefetchScalarGridSpec(
            num_scalar_prefetch=3,
            grid=(n_steps,),
            in_specs=[pl.BlockSpec(memory_space=pl.ANY), rows(GATE_W), rows(GATE_W), rows(D),
                      full(sg), full(su), full(sd), rows(D),
                      pl.BlockSpec((1, 1, D), lambda s, *_: (s // tiles_per_batch, 0, 0)), full(fg)],
            out_specs=rows(D),
            scratch_shapes=[pltpu.VMEM((2, jmax, D), BF16), pltpu.SemaphoreType.DMA((2,)),
                            pltpu.VMEM((tile, D), F32)]),
        out_shape=jax.ShapeDtypeStruct((n_tok, D), F32),
        compiler_params=pltpu.CompilerParams(dimension_semantics=("arbitrary",), vmem_limit_bytes=VMEM_LIMIT),
        name="moe_combine",
    )(off, cp, lb, ys, poscol, gatecol, h2, sg, su, sd, x1, gt2, fg)


def _moe(h2, posrow, poscol, gatecol, cpad, wg, wu, wd, sg, su, sd, x1, gt2, fg, tile, final_norm):
    B, T, D = x1.shape
    n_tok = B * T
    n_tiles = n_tok // tile
    jmax = -(-(TOP_K * tile + N_EXPERTS * (RUN_ALIGN - 1)) // MOE_CHUNK) * MOE_CHUNK
    cap = -(-(TOP_K * n_tok + n_tiles * N_EXPERTS * (RUN_ALIGN - 1) + N_EXPERTS * (ROW_BLOCK - RUN_ALIGN))
            // ROW_BLOCK) * ROW_BLOCK

    cp = cpad[..., 0].reshape(n_tiles, N_EXPERTS)
    off, lb, foff, flen, blk_e, n_used = _plan(cp, cap // ROW_BLOCK)
    flat = lambda a: a.reshape(-1)

    xs = _dispatch(h2.reshape(n_tok, D), posrow.reshape(n_tiles, TOP_K, tile), flat(off), flat(cp), flat(lb),
                   foff[0], flen[0], cap, tile, jmax)
    ys = _experts(xs, blk_e[0], n_used[0], wg, wu, wd)
    out = _combine(ys, poscol.reshape(n_tok, GATE_W), gatecol.reshape(n_tok, GATE_W), h2.reshape(n_tok, D),
                   sg, su, sd, x1.reshape(n_tok, D), gt2, fg, flat(off), flat(cp), flat(lb), tile, jmax,
                   T // tile, final_norm)
    return out.reshape(B, T, D)


def _misc_lanes(vec, start):
    return jnp.zeros((1, MISC_W), F32).at[0, start:start + vec.shape[0]].set(vec.astype(F32))


def kernel(x, c, ada_w, ada_b, norm1_g, w_in, conv_w, a_log, dt_bias, dn_norm_g, kv_norm_g, w_uk, w_uv,
           idx_k_ln_g, idx_k_ln_b, w_out, norm2_g, router_w, router_b, exp_w_gate, exp_w_up, exp_w_down,
           sh_w_gate, sh_w_up, sh_w_down, final_g):
    B, T, D = x.shape
    depth = ada_w.shape[0]
    topk = min(IDX_TOPK_MAX, T // 4)
    tm = min(512, T)
    r_dn = min(256, T)

    cond_in = jnp.zeros((8, D), F32).at[:B].set(c)
    pos = jnp.arange(tm)
    tri = ((pos[:, None] // CHUNK == pos[None, :] // CHUNK) & (pos[:, None] >= pos[None, :])).astype(F32)

    for l in range(depth):
        mod = _ada(cond_in, ada_w[l], ada_b[l][None, :])[:B]
        sh1, sc1, gt1, sh2, sc2, gt2 = [m[:, None, :] for m in jnp.split(mod, 6, axis=-1)]

        offs = [0]
        for s in (DN_QK, DN_QK, DN_V, DN_V, DN_HEADS, DN_HEADS, SA_Q, KV_RANK, IDX_Q, IDX_DIM, IDX_HEADS):
            offs.append(offs[-1] + s)
        w = w_in[l]
        wc = w[:, offs[0]:offs[3]].astype(BF16)
        wz = w[:, offs[3]:offs[4]].astype(BF16)
        wq = w[:, offs[6]:offs[7]].astype(BF16)
        wkv = w[:, offs[7]:offs[8]].astype(BF16)
        wqi = w[:, offs[8]:offs[9]].astype(BF16)
        wm = jnp.concatenate([w[:, offs[9]:offs[10]], w[:, offs[4]:offs[5]], w[:, offs[5]:offs[6]],
                              w[:, offs[10]:offs[11]],
                              jnp.zeros((D, MISC_W - IDX_DIM - 2 * DN_HEADS - IDX_HEADS), F32)],
                             axis=1).astype(BF16)
        ukt = jnp.swapaxes(w_uk[l], 1, 2).astype(BF16)

        q, k, v, z, qlat, ckv, qix, kix, misc = _inproj(
            x, sc1, sh1, norm1_g[l][None, :], wc, wz, wq, wkv, wqi, wm, conv_w[l], ukt,
            kv_norm_g[l][None, :], _misc_lanes(idx_k_ln_g[l], M_KIX), _misc_lanes(idx_k_ln_b[l], M_KIX),
            _misc_lanes(a_log[l], M_A), _misc_lanes(dt_bias[l], M_A), tri, tm)

        odn = _deltanet(q, k, v, z, misc, dn_norm_g[l][None, :], r_dn)
        olat = _dsa(qix, misc, kix, qlat, ckv, topk)

        x1, h2, posrow, poscol, gatecol, cpad = _outproj(
            x, odn, olat, w_uv[l].astype(BF16), w_out[l].astype(BF16), gt1, sc2, sh2,
            norm2_g[l][None, :], router_w[l].T, router_b[l][:, None], tm)

        x = _moe(h2, posrow, poscol, gatecol, cpad, exp_w_gate[l].astype(BF16), exp_w_up[l].astype(BF16),
                 exp_w_down[l].astype(BF16), sh_w_gate[l].astype(BF16), sh_w_up[l].astype(BF16),
                 sh_w_down[l].astype(BF16), x1, gt2, final_g[None, :], tm, l == depth - 1)
    return x
```

```python
import functools

import jax
import jax.numpy as jnp
from jax import lax
from jax.experimental import pallas as pl
from jax.experimental.pallas import tpu as pltpu

F32 = jnp.float32
BF16 = jnp.bfloat16
I32 = jnp.int32
HIGHEST = lax.Precision.HIGHEST

EPS = 1e-6
CHUNK = 64
DN_HEADS = 4
DN_DK = 128
DN_DV = 128
CONV_K = 4
SA_HEADS = 4
SA_DQK = 128
SA_DV = 128
KV_RANK = 256
IDX_HEADS = 4
IDX_DIM = 64
IDX_TOPK_MAX = 256
SM_SCALE = SA_DQK ** -0.5
LOG2E = 1.4426950408889634
IDX_W_SCALE = (IDX_HEADS * IDX_DIM) ** -0.5
N_EXPERTS = 64
TOP_K = 8
N_GROUPS = 8
TOPK_GROUPS = 4
D_EXPERT = 256
ROUTED_SCALE = 2.5
GATE_W = 128
RUN_ALIGN = 16
ROW_BLOCK = 512
MOE_CHUNK = 512

DN_QK = DN_HEADS * DN_DK
DN_V = DN_HEADS * DN_DV
CONV_DIM = 2 * DN_QK + DN_V
SA_Q = SA_HEADS * SA_DQK
IDX_Q = IDX_HEADS * IDX_DIM

MISC_W = 128
M_KIX = 0
M_BETA = IDX_DIM
M_A = M_BETA + DN_HEADS
M_WIX = M_A + DN_HEADS

DN_SUB = 2 * CHUNK
QBLOCK = 256
SOFTMAX_TINY = 2.0 ** -100
KEY_TILE = 1024
PLANE_COLS = 32 * 128
PLANE_SHIFT = 12
INT_MIN = -2 ** 31
NEG_BIG = -1e30
VMEM_LIMIT = 56 * 1024 * 1024


def _nt_dot(a, b, precision=None):
    return lax.dot_general(a, b, (((1,), (1,)), ((), ())), preferred_element_type=F32,
                           precision=precision)


def _dot(a, b, precision=None):
    return jnp.dot(a, b, preferred_element_type=F32, precision=precision)


def _silu(x):
    return x * jax.nn.sigmoid(x)


def _softplus(x):
    return jnp.maximum(x, 0.0) + jnp.log(1.0 + jnp.exp(-jnp.abs(x)))


def _ada_kernel(c_ref, w_ref, b_ref, o_ref):
    cond = _silu(c_ref[...])
    o_ref[...] = _dot(cond, w_ref[...], HIGHEST) + b_ref[...]


def _ada(c_pad, ada_w, ada_b):
    rows, d = c_pad.shape
    n_out = ada_w.shape[1]
    return pl.pallas_call(
        _ada_kernel,
        grid=(n_out // d,),
        in_specs=[pl.BlockSpec((rows, d), lambda j: (0, 0)),
                  pl.BlockSpec((d, d), lambda j: (0, j)),
                  pl.BlockSpec((1, d), lambda j: (0, j))],
        out_specs=pl.BlockSpec((rows, d), lambda j: (0, j)),
        out_shape=jax.ShapeDtypeStruct((rows, n_out), F32),
        compiler_params=pltpu.CompilerParams(vmem_limit_bytes=VMEM_LIMIT),
        name="ada",
    )(c_pad, ada_w, ada_b)


def _inproj_kernel(x_ref, sc_ref, sh_ref, g1_ref, wc_ref, wz_ref, wq_ref, wkv_ref, wqi_ref, wm_ref,
                   convw_ref, ukt_ref, kvg_ref, lng_ref, lnb_ref, alog_ref, dtb_ref, tri_ref,
                   q_ref, k_ref, v_ref, z_ref, qlat_ref, ckv_ref, qix_ref, kix_ref, misc_ref,
                   conv_buf):
    tm = x_ref.shape[1]
    i = pl.program_id(1)

    x = x_ref[0]
    h = x * lax.rsqrt(jnp.mean(x * x, axis=-1, keepdims=True) + EPS) * g1_ref[...]
    h = h * (1.0 + sc_ref[0]) + sh_ref[0]
    hb = h.astype(BF16)

    @pl.when(i == 0)
    def _():
        conv_buf[0:8, :] = jnp.zeros((8, CONV_DIM), F32)

    conv_buf[8:8 + tm, :] = _dot(hb, wc_ref[...])
    for grp, dst in ((0, q_ref), (1, k_ref), (2, v_ref)):
        cols = slice(grp * DN_QK, (grp + 1) * DN_QK)
        y = jnp.zeros((tm, DN_QK), F32)
        for j in range(CONV_K):
            y = y + convw_ref[j:j + 1, cols] * conv_buf[8 - (CONV_K - 1) + j:8 - (CONV_K - 1) + j + tm, cols]
        y = _silu(y)
        if grp < 2:
            outs = []
            for hd in range(DN_HEADS):
                yh = y[:, hd * DN_DK:(hd + 1) * DN_DK]
                yh = yh * lax.rsqrt(jnp.sum(yh * yh, axis=-1, keepdims=True) + EPS)
                if grp == 0:
                    yh = yh * (DN_DK ** -0.5)
                outs.append(yh)
            y = jnp.concatenate(outs, axis=-1)
        dst[0] = y
    conv_buf[0:8, :] = conv_buf[tm:tm + 8, :]

    z_ref[0] = _dot(hb, wz_ref[...])

    q_sa = _dot(hb, wq_ref[...]).astype(BF16)
    for hd in range(SA_HEADS):
        ql = _dot(q_sa[:, hd * SA_DQK:(hd + 1) * SA_DQK], ukt_ref[hd]) * (SM_SCALE * LOG2E)
        qlat_ref[0, hd] = ql.astype(BF16)

    ckv = _dot(hb, wkv_ref[...])
    ckv = ckv * lax.rsqrt(jnp.mean(ckv * ckv, axis=-1, keepdims=True) + EPS) * kvg_ref[...]
    ckv_ref[0] = ckv.astype(BF16)

    q_ix = _dot(hb, wqi_ref[...]).astype(BF16)
    for hd in range(IDX_HEADS):
        qix_ref[0, hd] = q_ix[:, hd * IDX_DIM:(hd + 1) * IDX_DIM]

    m = _dot(hb, wm_ref[...])
    lane = lax.broadcasted_iota(I32, (tm, MISC_W), 1)
    is_k = lane < IDX_DIM
    mu = jnp.sum(jnp.where(is_k, m, 0.0), axis=-1, keepdims=True) * (1.0 / IDX_DIM)
    kc = jnp.where(is_k, m - mu, 0.0)
    var = jnp.sum(kc * kc, axis=-1, keepdims=True) * (1.0 / IDX_DIM)
    kn = kc * lax.rsqrt(var + EPS) * lng_ref[...] + lnb_ref[...]
    kix_ref[0] = kn[:, :IDX_DIM].astype(BF16)

    beta = jax.nn.sigmoid(m)
    g = -jnp.exp(alog_ref[...]) * _softplus(m + dtb_ref[...])
    is_a = (lane >= M_A) & (lane < M_A + DN_HEADS)
    g = jnp.where(is_a, g, 0.0)
    gc = _dot(tri_ref[...], g, HIGHEST)
    is_b = (lane >= M_BETA) & (lane < M_BETA + DN_HEADS)
    is_w = (lane >= M_WIX) & (lane < M_WIX + IDX_HEADS)
    misc_ref[0] = jnp.where(is_b, beta, jnp.where(is_a, gc, jnp.where(is_w, m * IDX_W_SCALE, 0.0)))


def _inproj(x, sc1, sh1, g1, wc, wz, wq, wkv, wqi, wm, conv_w, ukt, kvg, lng, lnb, alog, dtb, tri, tm):
    B, T, D = x.shape
    nt = T // tm

    def full(a):
        nd = a.ndim
        return pl.BlockSpec(a.shape, lambda b, i, _n=nd: (0,) * _n)

    def rows(w):
        return pl.BlockSpec((1, tm, w), lambda b, i: (b, i, 0))

    per_b = pl.BlockSpec((1, 1, D), lambda b, i: (b, 0, 0))
    def head_rows(h, w):
        return pl.BlockSpec((1, h, tm, w), lambda b, i: (b, 0, i, 0))

    outs = [(None, DN_QK, F32), (None, DN_QK, F32), (None, DN_V, F32), (None, DN_V, F32),
            (SA_HEADS, KV_RANK, BF16), (None, KV_RANK, BF16), (IDX_HEADS, IDX_DIM, BF16),
            (None, IDX_DIM, BF16), (None, MISC_W, F32)]
    return pl.pallas_call(
        _inproj_kernel,
        grid=(B, nt),
        in_specs=[rows(D), per_b, per_b, full(g1), full(wc), full(wz), full(wq), full(wkv), full(wqi),
                  full(wm), full(conv_w), full(ukt), full(kvg), full(lng), full(lnb), full(alog),
                  full(dtb), full(tri)],
        out_specs=[rows(w) if h is None else head_rows(h, w) for h, w, _ in outs],
        out_shape=[jax.ShapeDtypeStruct((B, T, w) if h is None else (B, h, T, w), dt) for h, w, dt in outs],
        scratch_shapes=[pltpu.VMEM((tm + 8, CONV_DIM), F32)],
        compiler_params=pltpu.CompilerParams(dimension_semantics=("arbitrary", "arbitrary"),
                                             vmem_limit_bytes=VMEM_LIMIT),
        name="inproj",
    )(x, sc1, sh1, g1, wc, wz, wq, wkv, wqi, wm, conv_w, ukt, kvg, lng, lnb, alog, dtb, tri)


def _deltanet_kernel(q_ref, k_ref, v_ref, z_ref, misc_ref, ng_ref, o_ref, s_ref):
    R = q_ref.shape[1]
    n_chunks = R // CHUNK

    @pl.when(pl.program_id(1) == 0)
    def _():
        s_ref[...] = jnp.zeros(s_ref.shape, F32)

    misc = misc_ref[0]
    misc_t = misc.T
    SB = min(DN_SUB, R)
    row = lax.broadcasted_iota(I32, (SB, SB), 0)
    col = lax.broadcasted_iota(I32, (SB, SB), 1)
    same = (row // CHUNK) == (col // CHUNK)
    lower = same & (row >= col)
    strict = same & (row > col)
    eye = (row == col).astype(F32)

    def mm(a, b):
        return _dot(a.astype(BF16), b.astype(BF16))

    def mm3(a, b):
        ah = a.astype(BF16)
        bh = b.astype(BF16)
        al = (a - ah.astype(F32)).astype(BF16)
        bl = (b - bh.astype(F32)).astype(BF16)
        return _dot(jnp.concatenate([ah, ah, al], axis=1), jnp.concatenate([bh, bl, bh], axis=0))

    heads = range(DN_HEADS)
    subs = range(R // SB)
    chains = [(hd, sb) for hd in heads for sb in subs]
    cols = [slice(hd * DN_DK, (hd + 1) * DN_DK) for hd in heads]
    qh = [q_ref[0, :, cols[hd]] for hd in heads]
    kh = [k_ref[0, :, cols[hd]] for hd in heads]
    beta = [misc[:, M_BETA + hd:M_BETA + hd + 1] for hd in heads]
    gc_c = [misc[:, M_A + hd:M_A + hd + 1] for hd in heads]
    eg = [jnp.exp(gc_c[hd]) for hd in heads]
    kb = [kh[hd] * beta[hd] for hd in heads]
    rhs = [jnp.concatenate([v_ref[0, :, cols[hd]] * beta[hd], kb[hd] * eg[hd]], axis=-1) for hd in heads]
    q_dec = [qh[hd] * eg[hd] for hd in heads]

    def rows_of(sb):
        return slice(sb * SB, (sb + 1) * SB)

    decay, a, qk_sb = {}, {}, {}
    for hd, sb in chains:
        bs = rows_of(sb)
        gc_r = misc_t[M_A + hd:M_A + hd + 1, bs]
        decay[hd, sb] = jnp.where(lower, jnp.exp(jnp.where(lower, gc_c[hd][bs] - gc_r, 0.0)), 0.0)
    for hd, sb in chains:
        bs = rows_of(sb)
        khb = kh[hd][bs].astype(BF16)
        a[hd, sb] = jnp.where(strict, _nt_dot(kb[hd][bs].astype(BF16), khb) * decay[hd, sb], 0.0)
        qk_sb[hd, sb] = jnp.where(lower, _nt_dot(qh[hd][bs].astype(BF16), khb) * decay[hd, sb], 0.0)
    p = {ch: eye - a[ch] for ch in chains}
    xp = {ch: mm3(a[ch], a[ch]) for ch in chains}
    n_sq = 1
    while True:
        p = {ch: p[ch] + mm3(p[ch], xp[ch]) for ch in chains}
        n_sq *= 2
        if n_sq * 2 >= CHUNK:
            break
        xp = {ch: mm3(xp[ch], xp[ch]) for ch in chains}
    sol = {(hd, sb): mm(p[hd, sb], rhs[hd][rows_of(sb)]) for hd, sb in chains}

    def chunk_of(c):
        per = SB // CHUNK
        return c // per, slice((c % per) * CHUNK, (c % per + 1) * CHUNK)

    s = [s_ref[hd] for hd in heads]
    o_parts = [[] for _ in heads]
    for c in range(n_chunks):
        rs = slice(c * CHUNK, (c + 1) * CHUNK)
        sb, r = chunk_of(c)
        gl = [gc_c[hd][(c + 1) * CHUNK - 1:(c + 1) * CHUNK, :] for hd in heads]
        k_dec = [kh[hd][rs] * jnp.exp(gl[hd] - gc_c[hd][rs]) for hd in heads]
        v_new = [sol[hd, sb][r, :DN_DV] - mm(sol[hd, sb][r, DN_DV:], s[hd]) for hd in heads]
        for hd in heads:
            o_parts[hd].append(mm(q_dec[hd][rs], s[hd]) + mm(qk_sb[hd, sb][r, r], v_new[hd]))
        s = [s[hd] * jnp.exp(gl[hd]) + mm(k_dec[hd].T, v_new[hd]) for hd in heads]
    for hd in heads:
        s_ref[hd] = s[hd]
        o = jnp.concatenate(o_parts[hd], axis=0)
        o = o * lax.rsqrt(jnp.mean(o * o, axis=-1, keepdims=True) + EPS) * ng_ref[...]
        o_ref[0, :, cols[hd]] = (o * _silu(z_ref[0, :, cols[hd]])).astype(BF16)


def _deltanet(q, k, v, z, misc, ng, R):
    B, T, _ = q.shape

    def rows(w):
        return pl.BlockSpec((1, R, w), lambda b, i: (b, i, 0))

    return pl.pallas_call(
        _deltanet_kernel,
        grid=(B, T // R),
        in_specs=[rows(DN_QK), rows(DN_QK), rows(DN_V), rows(DN_V), rows(MISC_W),
                  pl.BlockSpec((1, DN_DV), lambda b, i: (0, 0))],
        out_specs=rows(DN_V),
        out_shape=jax.ShapeDtypeStruct((B, T, DN_V), BF16),
        scratch_shapes=[pltpu.VMEM((DN_HEADS, DN_DK, DN_DV), F32)],
        compiler_params=pltpu.CompilerParams(dimension_semantics=("arbitrary", "arbitrary"),
                                             vmem_limit_bytes=VMEM_LIMIT),
        name="deltanet",
    )(q, k, v, z, misc, ng)


def _bit_transpose32(words):
    w = list(words)
    j = 16
    m = 0x0000FFFF
    while j:
        k = 0
        m_i32 = jnp.int32(m - (1 << 32) if m >= (1 << 31) else m)
        while k < 32:
            t = (w[k] ^ lax.shift_right_logical(w[k + j], jnp.full_like(w[k], j))) & m_i32
            w[k] = w[k] ^ t
            w[k + j] = w[k + j] ^ jnp.left_shift(t, j)
            k = (k + j + 1) & ~j
        j >>= 1
        m = (m ^ (m << j)) & 0xFFFFFFFF
    return w


def _dsa_kernel(qix_ref, misc_ref, kix_ref, qlat_ref, ckv_ref, o_ref, keys_ref, planes_ref, bias_ref,
                mx_ref, l_ref, acc_ref, kvmax_ref, *, topk, pos_bits, n_cg_max):
    i = pl.program_id(1)
    QB = QBLOCK
    KT = KEY_TILE
    n_kt = (i * QB + QB + KT - 1) // KT

    rowi = lax.broadcasted_iota(I32, (QB, KT), 0)
    coli = lax.broadcasted_iota(I32, (QB, KT), 1)
    limit = i * QB + (rowi // CHUNK + 1) * CHUNK

    misc = misc_ref[0]
    qix = qix_ref[0].reshape(IDX_HEADS * QB, IDX_DIM)
    q_st = qlat_ref[0].reshape(SA_HEADS * QB, KV_RANK)

    def score_body(kt, carry):
        k0 = pl.multiple_of(kt * KT, KT)
        kx = kix_ref[0, pl.ds(k0, KT), :]
        rel = jnp.maximum(_nt_dot(qix, kx), 0.0)
        sc = jnp.zeros((QB, KT), F32)
        for hd in range(IDX_HEADS):
            sc = sc + misc[:, M_WIX + hd:M_WIX + hd + 1] * rel[hd * QB:(hd + 1) * QB]
        sc = jnp.where(sc == 0.0, 0.0, sc)
        bits = pltpu.bitcast(sc, I32)
        key = jnp.where(bits < 0, bits ^ 0x7FFFFFFF, bits)
        keys_ref[:, pl.ds(k0, KT)] = jnp.where(k0 + coli < limit, key, INT_MIN)
        return carry

    lax.fori_loop(0, n_kt, score_body, 0)

    n_cg = (n_kt * KT + PLANE_COLS - 1) // PLANE_COLS

    def fill_body(kt, carry):
        keys_ref[:, pl.ds(pl.multiple_of(kt * KT, KT), KT)] = jnp.full((QB, KT), INT_MIN, I32)
        return carry

    lax.fori_loop(n_kt, n_cg * (PLANE_COLS // KT), fill_body, 0)

    @pl.when(i == 0)
    def _():
        planes_ref[...] = jnp.zeros(planes_ref.shape, I32)

    def plane_body(step, carry):
        c = step // (QB // 8)
        r0 = pl.multiple_of((step % (QB // 8)) * 8, 8)
        words = [keys_ref[pl.ds(r0, 8), pl.ds(pl.multiple_of(c * PLANE_COLS + j * 128, 128), 128)]
                 for j in range(32)]
        for b, plane in enumerate(_bit_transpose32(words)):
            planes_ref[c, b, pl.ds(r0, 8), :] = ~plane if b == 0 else plane
        return carry

    lax.fori_loop(0, n_cg * (QB // 8), plane_body, 0)

    def lane_count(words):
        pc = functools.reduce(jnp.add, [lax.population_count(x) for x in words])
        return jnp.sum(pc, axis=1, keepdims=True)

    def sel_body(step, carry):
        cand, n_gt, tau_u = carry
        groups = range(n_cg_max)
        hi = [cand[c] & planes_ref[c, 2 * step] for c in groups]
        lo = [cand[c] ^ hi[c] for c in groups]
        d3 = [hi[c] & planes_ref[c, 2 * step + 1] for c in groups]
        d2 = [hi[c] ^ d3[c] for c in groups]
        d1 = [lo[c] & planes_ref[c, 2 * step + 1] for c in groups]
        d0 = [lo[c] ^ d1[c] for c in groups]
        a3 = n_gt + lane_count(d3)
        a2 = a3 + lane_count(d2)
        a1 = a2 + lane_count(d1)
        is3 = a3 >= topk
        is2 = a2 >= topk
        is1 = a1 >= topk
        cand = tuple(jnp.where(is3, d3[c], jnp.where(is2, d2[c], jnp.where(is1, d1[c], d0[c]))) for c in groups)
        n_gt = jnp.where(is3, n_gt, jnp.where(is2, a3, jnp.where(is1, a2, a1)))
        digit = jnp.where(is3, 3, jnp.where(is2, 2, jnp.where(is1, 1, 0)))
        return cand, n_gt, tau_u | jnp.left_shift(digit, 30 - 2 * step)

    cand0 = tuple(jnp.where(c < n_cg, jnp.full((QB, 128), -1, I32), 0) for c in range(n_cg_max))
    cand, n_gt, tau_u = lax.fori_loop(
        0, 16, sel_body, (cand0, jnp.zeros((QB, 1), I32), jnp.zeros((QB, 1), I32)))
    tau = tau_u ^ INT_MIN
    sentinel = tau == INT_MIN
    cand = tuple(jnp.where(sentinel, 0, cand[c]) for c in range(n_cg_max))
    need = topk - n_gt
    any_tie = jnp.max(jnp.where(lane_count(cand) > need, 1, 0)) > 0

    def fast_bias():
        floor = jnp.where(sentinel, INT_MIN + 1, tau)

        def body(kt, carry):
            k0 = pl.multiple_of(kt * KT, KT)
            bias_ref[:, pl.ds(k0, KT)] = jnp.where(keys_ref[:, pl.ds(k0, KT)] >= floor, 0.0, NEG_BIG)
            return carry

        lax.fori_loop(0, n_kt, body, 0)

    def tie_bias():
        lane = lax.broadcasted_iota(I32, (QB, 128), 1)

        def pos_mask(p, c):
            cg = lax.shift_right_logical(p, jnp.full_like(p, PLANE_SHIFT))
            j0 = lax.shift_right_logical(p, jnp.full_like(p, 7)) & 31
            below = ~lax.shift_right_logical(jnp.full_like(p, -1), j0)
            bit = lax.shift_right_logical(jnp.full_like(p, INT_MIN), j0)
            word = below | jnp.where(lane < (p & 127), bit, 0)
            return jnp.where(cg > c, -1, jnp.where(cg == c, word, 0))

        def pos_body(b, q):
            cq = q + jnp.left_shift(jnp.int32(1), pos_bits - 1 - b)
            cnt = lane_count([cand[c] & pos_mask(cq, c) for c in range(n_cg_max)])
            return jnp.where(cnt < need, cq, q)

        pstar = lax.fori_loop(0, pos_bits, pos_body, jnp.zeros((QB, 1), I32)) + 1
        pstar = jnp.where(sentinel, 0, pstar)

        def body(kt, carry):
            k0 = pl.multiple_of(kt * KT, KT)
            kk = keys_ref[:, pl.ds(k0, KT)]
            tie = jnp.where(k0 + coli < pstar, 0.0, NEG_BIG)
            bias_ref[:, pl.ds(k0, KT)] = jnp.where(kk > tau, 0.0, jnp.where(kk == tau, tie, NEG_BIG))
            return carry

        lax.fori_loop(0, n_kt, body, 0)

    lax.cond(any_tie, tie_bias, fast_bias)

    def logit_chunks(k0):
        kv = ckv_ref[0, pl.ds(k0, KT), :]
        s = _nt_dot(q_st, kv)
        bias = bias_ref[:, pl.ds(k0, KT)]
        chunks = []
        for j in range(KT // 128):
            bj = bias[:, j * 128:(j + 1) * 128]
            chunks.append(s[:, j * 128:(j + 1) * 128] + jnp.concatenate([bj] * SA_HEADS, axis=0))
        return kv, chunks

    def weighted_sum_sweep():
        l_ref[...] = jnp.zeros(l_ref.shape, F32)
        acc_ref[...] = jnp.zeros(acc_ref.shape, F32)

        def pv_body(kt, carry):
            kv, chunks = logit_chunks(pl.multiple_of(kt * KT, KT))
            shift = mx_ref[...]
            ps = [jnp.exp2(c - shift) for c in chunks]
            l_ref[...] = functools.reduce(jnp.add, ps, l_ref[...])
            p = jnp.concatenate([pj.astype(BF16) for pj in ps], axis=1)
            acc_ref[...] += _dot(p, kv)
            return carry

        lax.fori_loop(0, n_kt, pv_body, 0)
        l_row = jnp.sum(l_ref[...], axis=-1, keepdims=True)
        o_ref[0] = (acc_ref[...] / l_row).astype(BF16).reshape(SA_HEADS, QB, KV_RANK)
        return l_row

    @pl.when(i == 0)
    def _():
        def norm_body(r, best):
            x = ckv_ref[0, pl.ds(pl.multiple_of(r * KT, KT), KT), :].astype(F32)
            return jnp.maximum(best, jnp.max(jnp.sum(x * x, axis=1, keepdims=True), axis=0, keepdims=True))

        n_rows = ckv_ref.shape[1]
        kv_sq = lax.fori_loop(0, n_rows // KT, norm_body, jnp.zeros((1, 1), F32))
        kvmax_ref[...] = jnp.broadcast_to(jnp.sqrt(kv_sq), kvmax_ref.shape)

    qf = q_st.astype(F32)
    q_norm = jnp.sqrt(jnp.sum(qf * qf, axis=1, keepdims=True))
    mx_ref[...] = jnp.broadcast_to(q_norm, mx_ref.shape) * kvmax_ref[0:1, :] * 1.001 + 1e-3
    l_fast = weighted_sum_sweep()

    @pl.when(jnp.min(l_fast) < SOFTMAX_TINY)
    def _():
        mx_ref[...] = jnp.full(mx_ref.shape, NEG_BIG, F32)

        def max_body(kt, carry):
            _, chunks = logit_chunks(pl.multiple_of(kt * KT, KT))
            mx_ref[...] = functools.reduce(jnp.maximum, chunks, mx_ref[...])
            return carry

        lax.fori_loop(0, n_kt, max_body, 0)
        mx_ref[...] = jnp.broadcast_to(jnp.max(mx_ref[...], axis=-1, keepdims=True), mx_ref.shape)
        weighted_sum_sweep()


def _dsa(qix, misc, kix, qlat, ckv, topk):
    B, T, _ = kix.shape
    n_cg_max = -(-T // PLANE_COLS)
    t_pad = n_cg_max * PLANE_COLS
    pos_bits = (t_pad - 1).bit_length()

    def rows(w):
        return pl.BlockSpec((1, QBLOCK, w), lambda b, i: (b, i, 0))

    def head_rows(h, w):
        return pl.BlockSpec((1, h, QBLOCK, w), lambda b, i: (b, 0, i, 0))

    def per_b(w):
        return pl.BlockSpec((1, T, w), lambda b, i: (b, 0, 0))

    return pl.pallas_call(
        functools.partial(_dsa_kernel, topk=topk, pos_bits=pos_bits, n_cg_max=n_cg_max),
        grid=(B, T // QBLOCK),
        in_specs=[head_rows(IDX_HEADS, IDX_DIM), rows(MISC_W), per_b(IDX_DIM), head_rows(SA_HEADS, KV_RANK),
                  per_b(KV_RANK)],
        out_specs=head_rows(SA_HEADS, KV_RANK),
        out_shape=jax.ShapeDtypeStruct((B, SA_HEADS, T, KV_RANK), BF16),
        scratch_shapes=[pltpu.VMEM((QBLOCK, t_pad), I32),
                        pltpu.VMEM((n_cg_max, 32, QBLOCK, 128), I32),
                        pltpu.VMEM((QBLOCK, t_pad), F32),
                        pltpu.VMEM((SA_HEADS * QBLOCK, 128), F32),
                        pltpu.VMEM((SA_HEADS * QBLOCK, 128), F32),
                        pltpu.VMEM((SA_HEADS * QBLOCK, KV_RANK), F32),
                        pltpu.VMEM((8, 128), F32)],
        compiler_params=pltpu.CompilerParams(dimension_semantics=("arbitrary", "arbitrary"),
                                             vmem_limit_bytes=VMEM_LIMIT),
        name="dsa",
    )(qix, misc, kix, qlat, ckv)


def _first_max(v, idx, axis):
    m = jnp.max(v, axis=axis, keepdims=True)
    big = jnp.int32(2 ** 30)
    first = jnp.min(jnp.where(v == m, idx, big), axis=axis, keepdims=True)
    return m, idx == first


def _outproj_kernel(x_ref, odn_ref, olat_ref, uv_ref, wo_ref, gt_ref, sc_ref, sh_ref, g2_ref, rwt_ref,
                    rb_ref, lstrict_ref, ustrict_ref, x1_ref, h2_ref, posrow_ref, poscol_ref, gatecol_ref,
                    cpad_ref):
    tm = x_ref.shape[1]
    parts = [odn_ref[0]]
    for hd in range(SA_HEADS):
        parts.append(_dot(olat_ref[0, hd], uv_ref[hd]).astype(BF16))
    mix = jnp.concatenate(parts, axis=-1)
    x1 = x_ref[0] + gt_ref[0] * _dot(mix, wo_ref[...])
    x1_ref[0] = x1
    h2 = x1 * lax.rsqrt(jnp.mean(x1 * x1, axis=-1, keepdims=True) + EPS) * g2_ref[...]
    h2 = h2 * (1.0 + sc_ref[0]) + sh_ref[0]
    h2_ref[0] = h2.astype(BF16)

    per_g = N_EXPERTS // N_GROUPS
    s = jax.nn.sigmoid(_nt_dot(rwt_ref[...], h2, HIGHEST))
    choice = s + rb_ref[...]
    ig = lax.broadcasted_iota(I32, (per_g, tm), 0)
    gscore = []
    for gidx in range(N_GROUPS):
        cg = choice[gidx * per_g:(gidx + 1) * per_g]
        m1, hot1 = _first_max(cg, ig, 0)
        gscore.append(m1 + jnp.max(jnp.where(hot1, -jnp.inf, cg), axis=0, keepdims=True))
    gsel = [jnp.zeros((1, tm), jnp.bool_) for _ in range(N_GROUPS)]
    for _ in range(TOPK_GROUPS):
        best = functools.reduce(jnp.maximum, gscore)
        found = jnp.zeros((1, tm), jnp.bool_)
        for gidx in range(N_GROUPS):
            hot = (gscore[gidx] == best) & jnp.logical_not(found)
            found = found | hot
            gsel[gidx] = gsel[gidx] | hot
            gscore[gidx] = jnp.where(hot, -jnp.inf, gscore[gidx])
    masked = jnp.concatenate(
        [jnp.where(gsel[gidx], choice[gidx * per_g:(gidx + 1) * per_g], -jnp.inf) for gidx in range(N_GROUPS)],
        axis=0)
    ei = lax.broadcasted_iota(I32, masked.shape, 0)
    gate = jnp.zeros(masked.shape, F32)
    hots = []
    for _ in range(TOP_K):
        _, hot = _first_max(masked, ei, 0)
        hots.append(hot)
        gate = jnp.where(hot, s, gate)
        masked = jnp.where(hot, -jnp.inf, masked)
    gate = gate / jnp.sum(gate, axis=0, keepdims=True) * ROUTED_SCALE

    picked = jnp.where(functools.reduce(jnp.logical_or, hots), 1.0, 0.0)
    cnt = jnp.sum(picked, axis=1, keepdims=True)
    cpad = jnp.floor((cnt + (RUN_ALIGN - 1)) * (1.0 / RUN_ALIGN)) * RUN_ALIGN
    cpad_b = jnp.broadcast_to(cpad, (N_EXPERTS, GATE_W))
    lbase = _dot(lstrict_ref[...], cpad_b, HIGHEST)[:, :1]
    rank = _dot(picked.astype(BF16), ustrict_ref[...])
    pos = lbase + rank
    ri = lax.broadcasted_iota(I32, (GATE_W, tm), 0)
    pos_rows = jnp.zeros((GATE_W, tm), F32)
    gate_rows = jnp.zeros((GATE_W, tm), F32)
    for k, hot in enumerate(hots):
        pos_rows = jnp.where(ri == k, jnp.sum(jnp.where(hot, pos, 0.0), axis=0, keepdims=True), pos_rows)
        gate_rows = jnp.where(ri == k, jnp.sum(jnp.where(hot, gate, 0.0), axis=0, keepdims=True), gate_rows)
    posrow_ref[0, 0] = pos_rows[:TOP_K].astype(I32)
    poscol_ref[0] = pos_rows.T.astype(I32)
    gatecol_ref[0] = gate_rows.T
    cpad_ref[0, 0] = cpad_b.astype(I32)


def _outproj(x, odn, olat, uv, wo, gt1, sc2, sh2, g2, rwt, rb, tm):
    B, T, D = x.shape
    nt = T // tm
    ex = jnp.arange(N_EXPERTS)
    lstrict = (ex[:, None] > ex[None, :]).astype(F32)
    tok = jnp.arange(tm)
    ustrict = (tok[:, None] < tok[None, :]).astype(BF16)

    def full(a):
        nd = a.ndim
        return pl.BlockSpec(a.shape, lambda b, i, _n=nd: (0,) * _n)

    def rows(w):
        return pl.BlockSpec((1, tm, w), lambda b, i: (b, i, 0))

    def per_tile(h, w):
        return pl.BlockSpec((1, 1, h, w), lambda b, i: (b, i, 0, 0))

    per_b = pl.BlockSpec((1, 1, D), lambda b, i: (b, 0, 0))
    return pl.pallas_call(
        _outproj_kernel,
        grid=(B, nt),
        in_specs=[rows(D), rows(DN_V),
                  pl.BlockSpec((1, SA_HEADS, tm, KV_RANK), lambda b, i: (b, 0, i, 0)),
                  full(uv), full(wo), per_b, per_b, per_b,
                  full(g2), full(rwt), full(rb), full(lstrict), full(ustrict)],
        out_specs=[rows(D), rows(D), per_tile(TOP_K, tm), rows(GATE_W), rows(GATE_W),
                   per_tile(N_EXPERTS, GATE_W)],
        out_shape=[jax.ShapeDtypeStruct((B, T, D), F32), jax.ShapeDtypeStruct((B, T, D), BF16),
                   jax.ShapeDtypeStruct((B, nt, TOP_K, tm), I32),
                   jax.ShapeDtypeStruct((B, T, GATE_W), I32),
                   jax.ShapeDtypeStruct((B, T, GATE_W), F32),
                   jax.ShapeDtypeStruct((B, nt, N_EXPERTS, GATE_W), I32)],
        compiler_params=pltpu.CompilerParams(dimension_semantics=("arbitrary", "arbitrary"),
                                             vmem_limit_bytes=VMEM_LIMIT),
        name="outproj",
    )(x, odn, olat, uv, wo, gt1, sc2, sh2, g2, rwt, rb, lstrict, ustrict)


def _piece_sizes(max_rows):
    sizes = []
    z = RUN_ALIGN
    while z <= max_rows:
        sizes.append(z)
        z *= 2
    return sizes[::-1]


def _for_run_pieces(length, max_rows, fn):
    for z in _piece_sizes(max_rows):
        start = length & ~(2 * z - 1)

        @pl.when((length & z) != 0)
        def _(start=start, z=z):
            fn(start, z)


def _plan_kernel(cp_ref, off_ref, lb_ref, foff_ref, flen_ref, blk_ref, nused_ref):
    cp = cp_ref[...].astype(F32)
    n, ne = cp.shape
    ei = lax.broadcasted_iota(I32, (ne, ne), 0)
    ej = lax.broadcasted_iota(I32, (ne, ne), 1)
    si = lax.broadcasted_iota(I32, (n, n), 0)
    sj = lax.broadcasted_iota(I32, (n, n), 1)
    lb = _dot(cp, (ei < ej).astype(F32), HIGHEST)
    earlier_tiles = _dot((sj < si).astype(F32), cp, HIGHEST)
    rows_e = jnp.sum(cp, axis=0, keepdims=True)
    region = jnp.floor((rows_e + (ROW_BLOCK - 1)) * (1.0 / ROW_BLOCK)) * ROW_BLOCK
    region_b = jnp.broadcast_to(region, (ne, ne))
    rend_row = _dot(region_b, (ei <= ej).astype(F32), HIGHEST)[:1]
    rend_col = jnp.sum(jnp.where(ej <= ei, region_b, 0.0), axis=1, keepdims=True)
    base = rend_row - region
    total = jnp.max(rend_row, axis=1, keepdims=True)
    off_ref[...] = (base + earlier_tiles).astype(I32)
    lb_ref[...] = lb.astype(I32)
    lane = lax.broadcasted_iota(I32, (1, GATE_W), 1)
    pad = jnp.zeros((1, GATE_W - ne), F32)
    foff_ref[...] = jnp.where(lane == ne, total, jnp.concatenate([base + rows_e, pad], axis=1)).astype(I32)
    flen_ref[...] = jnp.concatenate([region - rows_e, pad], axis=1).astype(I32)
    n_used = total * (1.0 / ROW_BLOCK)
    nused_ref[...] = jnp.broadcast_to(n_used, nused_ref.shape).astype(I32)
    bi = lax.broadcasted_iota(I32, (ne, blk_ref.shape[1]), 1).astype(F32)
    ended = jnp.where(rend_col * (1.0 / ROW_BLOCK) <= jnp.minimum(bi, n_used - 1.0), 1.0, 0.0)
    blk_ref[...] = jnp.minimum(jnp.sum(ended, axis=0, keepdims=True), ne - 1.0).astype(I32)


def _plan(cp, n_blocks):
    n, ne = cp.shape
    nb_pad = -(-n_blocks // 128) * 128
    return pl.pallas_call(
        _plan_kernel,
        out_shape=[jax.ShapeDtypeStruct((n, ne), I32), jax.ShapeDtypeStruct((n, ne), I32),
                   jax.ShapeDtypeStruct((1, GATE_W), I32), jax.ShapeDtypeStruct((1, GATE_W), I32),
                   jax.ShapeDtypeStruct((1, nb_pad), I32), jax.ShapeDtypeStruct((1, GATE_W), I32)],
        name="moe_plan",
    )(cp)


def _dispatch_kernel(off_ref, cp_ref, lb_ref, foff_ref, flen_ref, h_ref, posrow_ref, xs_hbm, buf, zbuf, sem,
                     zsem, *, n_steps, tile):
    s = pl.program_id(0)
    slot = s % 2
    jmax = buf.shape[1]

    def run_copies(step, slot_, act):
        def body(e, carry):
            idx = step * N_EXPERTS + e
            lb = lb_ref[idx]
            of = off_ref[idx]

            def piece(start, z):
                act(pltpu.make_async_copy(
                    buf.at[slot_, pl.ds(pl.multiple_of(lb + start, RUN_ALIGN), z)],
                    xs_hbm.at[pl.ds(pl.multiple_of(of + start, RUN_ALIGN), z)], sem.at[slot_]))

            _for_run_pieces(cp_ref[idx], tile, piece)
            return carry

        lax.fori_loop(0, N_EXPERTS, body, 0)

    def fill_copies(act):
        def body(e, carry):
            fo = foff_ref[e]

            def piece(start, z):
                act(pltpu.make_async_copy(
                    zbuf.at[pl.ds(0, z)], xs_hbm.at[pl.ds(pl.multiple_of(fo + start, RUN_ALIGN), z)], zsem.at[0]))

            _for_run_pieces(flen_ref[e], ROW_BLOCK // 2, piece)
            return carry

        lax.fori_loop(0, N_EXPERTS, body, 0)

        def tail(r, carry):
            act(pltpu.make_async_copy(
                zbuf, xs_hbm.at[pl.ds(pl.multiple_of(foff_ref[N_EXPERTS] + r * zbuf.shape[0], RUN_ALIGN),
                                      zbuf.shape[0])], zsem.at[0]))
            return carry

        lax.fori_loop(0, (xs_hbm.shape[0] - foff_ref[N_EXPERTS]) // zbuf.shape[0], tail, 0)

    @pl.when(s == 0)
    def _():
        zbuf[...] = jnp.zeros(zbuf.shape, BF16)
        fill_copies(lambda c: c.start())

    @pl.when(s >= 2)
    def _():
        run_copies(s - 2, slot, lambda c: c.wait())

    h = h_ref[...]
    last = s * N_EXPERTS + N_EXPERTS - 1
    jused = lb_ref[last] + cp_ref[last]
    def local_rows(jc):
        ji = (lax.broadcasted_iota(I32, (MOE_CHUNK, tile), 0) + jc * MOE_CHUNK).astype(jnp.int16)
        p = jnp.zeros((MOE_CHUNK, tile), BF16)
        for k in range(TOP_K):
            p = jnp.where(ji == posrow_ref[0, k:k + 1, :].astype(jnp.int16), jnp.ones((), BF16), p)
        return _dot(p, h).astype(BF16)

    n_full = TOP_K * tile // MOE_CHUNK
    rows_full = [local_rows(jc) for jc in range(n_full)]
    for jc in range(n_full):
        buf[slot, jc * MOE_CHUNK:(jc + 1) * MOE_CHUNK, :] = rows_full[jc]
    for jc in range(n_full, jmax // MOE_CHUNK):
        @pl.when(jc * MOE_CHUNK < jused)
        def _(jc=jc):
            buf[slot, jc * MOE_CHUNK:(jc + 1) * MOE_CHUNK, :] = local_rows(jc)

    run_copies(s, slot, lambda c: c.start())

    @pl.when(s == n_steps - 1)
    def _():
        if n_steps >= 2:
            run_copies(s - 1, 1 - slot, lambda c: c.wait())
        run_copies(s, slot, lambda c: c.wait())
        fill_copies(lambda c: c.wait())


def _dispatch(h2, posrow, off, cp, lb, foff, flen, cap, tile, jmax):
    n_tok, D = h2.shape
    n_steps = n_tok // tile
    return pl.pallas_call(
        functools.partial(_dispatch_kernel, n_steps=n_steps, tile=tile),
        grid_spec=pltpu.PrefetchScalarGridSpec(
            num_scalar_prefetch=5,
            grid=(n_steps,),
            in_specs=[pl.BlockSpec((tile, D), lambda s, *_: (s, 0)),
                      pl.BlockSpec((1, TOP_K, tile), lambda s, *_: (s, 0, 0))],
            out_specs=pl.BlockSpec(memory_space=pl.ANY),
            scratch_shapes=[pltpu.VMEM((2, jmax, D), BF16), pltpu.VMEM((ROW_BLOCK // 2, D), BF16),
                            pltpu.SemaphoreType.DMA((2,)), pltpu.SemaphoreType.DMA((1,))]),
        out_shape=jax.ShapeDtypeStruct((cap, D), BF16),
        compiler_params=pltpu.CompilerParams(dimension_semantics=("arbitrary",), vmem_limit_bytes=VMEM_LIMIT),
        name="moe_dispatch",
    )(off, cp, lb, foff, flen, h2, posrow)


def _expert_kernel(blk_e_ref, nused_ref, xs_ref, wg_ref, wu_ref, wd_ref, ys_ref):
    used = pl.program_id(0) < nused_ref[0]

    @pl.when(used)
    def _():
        xb = xs_ref[...]
        a = _silu(_dot(xb, wg_ref[0])) * _dot(xb, wu_ref[0])
        ys_ref[...] = _dot(a.astype(BF16), wd_ref[0]).astype(BF16)

    @pl.when(jnp.logical_not(used))
    def _():
        ys_ref[...] = jnp.zeros(ys_ref.shape, BF16)


def _experts(xs, blk_e, n_used, wg, wu, wd):
    cap, D = xs.shape

    def row_block(i, be, nu):
        return (jnp.minimum(i, nu[0] - 1), 0)

    def out_block(i, be, nu):
        return (i, 0)

    def weight(i, be, nu):
        return (be[i], 0, 0)

    return pl.pallas_call(
        _expert_kernel,
        grid_spec=pltpu.PrefetchScalarGridSpec(
            num_scalar_prefetch=2,
            grid=(cap // ROW_BLOCK,),
            in_specs=[pl.BlockSpec((ROW_BLOCK, D), row_block),
                      pl.BlockSpec((1, D, D_EXPERT), weight), pl.BlockSpec((1, D, D_EXPERT), weight),
                      pl.BlockSpec((1, D_EXPERT, D), weight)],
            out_specs=pl.BlockSpec((ROW_BLOCK, D), out_block)),
        out_shape=jax.ShapeDtypeStruct((cap, D), BF16),
        compiler_params=pltpu.CompilerParams(dimension_semantics=("arbitrary",), vmem_limit_bytes=VMEM_LIMIT),
        name="moe_experts",
    )(blk_e, n_used, xs, wg, wu, wd)


def _combine_kernel(off_ref, cp_ref, lb_ref, ys_hbm, poscol_ref, gatecol_ref, h_ref, sg_ref, su_ref, sd_ref,
                    x1_ref, gt_ref, fg_ref, o_ref, buf, sem, acc_ref, *, n_steps, tile, final_norm):
    s = pl.program_id(0)
    slot = s % 2
    jmax = buf.shape[1]

    def run_copies(step, slot_, act):
        def body(e, carry):
            idx = step * N_EXPERTS + e
            lb = lb_ref[idx]
            of = off_ref[idx]

            def piece(start, z):
                act(pltpu.make_async_copy(
                    ys_hbm.at[pl.ds(pl.multiple_of(of + start, RUN_ALIGN), z)],
                    buf.at[slot_, pl.ds(pl.multiple_of(lb + start, RUN_ALIGN), z)], sem.at[slot_]))

            _for_run_pieces(cp_ref[idx], tile, piece)
            return carry

        lax.fori_loop(0, N_EXPERTS, body, 0)

    @pl.when(s == 0)
    def _():
        run_copies(0, 0, lambda c: c.start())

    @pl.when(s + 1 < n_steps)
    def _():
        run_copies(s + 1, 1 - slot, lambda c: c.start())

    hb = h_ref[...]
    shared = (_silu(_dot(hb, sg_ref[...])) * _dot(hb, su_ref[...])).astype(BF16)
    acc_ref[...] = _dot(shared, sd_ref[...])

    run_copies(s, slot, lambda c: c.wait())
    last = s * N_EXPERTS + N_EXPERTS - 1
    jused = lb_ref[last] + cp_ref[last]

    def zero_body(r, carry):
        buf[slot, pl.ds(pl.multiple_of(jused + r * RUN_ALIGN, RUN_ALIGN), RUN_ALIGN), :] = jnp.zeros(
            (RUN_ALIGN, buf.shape[2]), BF16)
        return carry

    chunk_end = (jused + MOE_CHUNK - 1) // MOE_CHUNK * MOE_CHUNK
    lax.fori_loop(0, (chunk_end - jused) // RUN_ALIGN, zero_body, 0)

    def gate_rows(jc):
        ji = (lax.broadcasted_iota(I32, (tile, MOE_CHUNK), 1) + jc * MOE_CHUNK).astype(jnp.int16)
        g = jnp.zeros((tile, MOE_CHUNK), BF16)
        for k in range(TOP_K):
            g = jnp.where(ji == poscol_ref[:, k:k + 1].astype(jnp.int16), gatecol_ref[:, k:k + 1].astype(BF16), g)
        return g

    n_full = TOP_K * tile // MOE_CHUNK
    g_full = jnp.concatenate([gate_rows(jc) for jc in range(n_full)], axis=1)
    acc_ref[...] += _dot(g_full, buf[slot, 0:n_full * MOE_CHUNK, :])
    for jc in range(n_full, jmax // MOE_CHUNK):
        @pl.when(jc * MOE_CHUNK < jused)
        def _(jc=jc):
            acc_ref[...] += _dot(gate_rows(jc), buf[slot, jc * MOE_CHUNK:(jc + 1) * MOE_CHUNK, :])

    y = x1_ref[...] + gt_ref[0] * acc_ref[...]
    if final_norm:
        y = y * lax.rsqrt(jnp.mean(y * y, axis=-1, keepdims=True) + EPS) * fg_ref[...]
    o_ref[...] = y


def _combine(ys, poscol, gatecol, h2, sg, su, sd, x1, gt2, fg, off, cp, lb, tile, jmax, tiles_per_batch,
             final_norm):
    n_tok, D = h2.shape
    n_steps = n_tok // tile

    def full(a):
        nd = a.ndim
        return pl.BlockSpec(a.shape, lambda s, *_, _n=nd: (0,) * _n)

    def rows(w):
        return pl.BlockSpec((tile, w), lambda s, *_: (s, 0))

    return pl.pallas_call(
        functools.partial(_combine_kernel, n_steps=n_steps, tile=tile, final_norm=final_norm),
        grid_spec=pltpu.PrefetchScalarGridSpec(
            num_scalar_prefetch=3,
            grid=(n_steps,),
            in_specs=[pl.BlockSpec(memory_space=pl.ANY), rows(GATE_W), rows(GATE_W), rows(D),
                      full(sg), full(su), full(sd), rows(D),
                      pl.BlockSpec((1, 1, D), lambda s, *_: (s // tiles_per_batch, 0, 0)), full(fg)],
            out_specs=rows(D),
            scratch_shapes=[pltpu.VMEM((2, jmax, D), BF16), pltpu.SemaphoreType.DMA((2,)),
                            pltpu.VMEM((tile, D), F32)]),
        out_shape=jax.ShapeDtypeStruct((n_tok, D), F32),
        compiler_params=pltpu.CompilerParams(dimension_semantics=("arbitrary",), vmem_limit_bytes=VMEM_LIMIT),
        name="moe_combine",
    )(off, cp, lb, ys, poscol, gatecol, h2, sg, su, sd, x1, gt2, fg)


def _moe(h2, posrow, poscol, gatecol, cpad, wg, wu, wd, sg, su, sd, x1, gt2, fg, tile, final_norm):
    B, T, D = x1.shape
    n_tok = B * T
    n_tiles = n_tok // tile
    jmax = -(-(TOP_K * tile + N_EXPERTS * (RUN_ALIGN - 1)) // MOE_CHUNK) * MOE_CHUNK
    cap = -(-(TOP_K * n_tok + n_tiles * N_EXPERTS * (RUN_ALIGN - 1) + N_EXPERTS * (ROW_BLOCK - RUN_ALIGN))
            // ROW_BLOCK) * ROW_BLOCK

    cp = cpad[..., 0].reshape(n_tiles, N_EXPERTS)
    off, lb, foff, flen, blk_e, n_used = _plan(cp, cap // ROW_BLOCK)
    flat = lambda a: a.reshape(-1)

    xs = _dispatch(h2.reshape(n_tok, D), posrow.reshape(n_tiles, TOP_K, tile), flat(off), flat(cp), flat(lb),
                   foff[0], flen[0], cap, tile, jmax)
    ys = _experts(xs, blk_e[0], n_used[0], wg, wu, wd)
    out = _combine(ys, poscol.reshape(n_tok, GATE_W), gatecol.reshape(n_tok, GATE_W), h2.reshape(n_tok, D),
                   sg, su, sd, x1.reshape(n_tok, D), gt2, fg, flat(off), flat(cp), flat(lb), tile, jmax,
                   T // tile, final_norm)
    return out.reshape(B, T, D)


def _misc_lanes(vec, start):
    return jnp.zeros((1, MISC_W), F32).at[0, start:start + vec.shape[0]].set(vec.astype(F32))


def kernel(x, c, ada_w, ada_b, norm1_g, w_in, conv_w, a_log, dt_bias, dn_norm_g, kv_norm_g, w_uk, w_uv,
           idx_k_ln_g, idx_k_ln_b, w_out, norm2_g, router_w, router_b, exp_w_gate, exp_w_up, exp_w_down,
           sh_w_gate, sh_w_up, sh_w_down, final_g):
    B, T, D = x.shape
    depth = ada_w.shape[0]
    topk = min(IDX_TOPK_MAX, T // 4)
    tm = min(512, T)
    r_dn = min(256, T)

    cond_in = jnp.zeros((8, D), F32).at[:B].set(c)
    pos = jnp.arange(tm)
    tri = ((pos[:, None] // CHUNK == pos[None, :] // CHUNK) & (pos[:, None] >= pos[None, :])).astype(F32)

    for l in range(depth):
        mod = _ada(cond_in, ada_w[l], ada_b[l][None, :])[:B]
        sh1, sc1, gt1, sh2, sc2, gt2 = [m[:, None, :] for m in jnp.split(mod, 6, axis=-1)]

        offs = [0]
        for s in (DN_QK, DN_QK, DN_V, DN_V, DN_HEADS, DN_HEADS, SA_Q, KV_RANK, IDX_Q, IDX_DIM, IDX_HEADS):
            offs.append(offs[-1] + s)
        w = w_in[l]
        wc = w[:, offs[0]:offs[3]].astype(BF16)
        wz = w[:, offs[3]:offs[4]].astype(BF16)
        wq = w[:, offs[6]:offs[7]].astype(BF16)
        wkv = w[:, offs[7]:offs[8]].astype(BF16)
        wqi = w[:, offs[8]:offs[9]].astype(BF16)
        wm = jnp.concatenate([w[:, offs[9]:offs[10]], w[:, offs[4]:offs[5]], w[:, offs[5]:offs[6]],
                              w[:, offs[10]:offs[11]],
                              jnp.zeros((D, MISC_W - IDX_DIM - 2 * DN_HEADS - IDX_HEADS), F32)],
                             axis=1).astype(BF16)
        ukt = jnp.swapaxes(w_uk[l], 1, 2).astype(BF16)

        q, k, v, z, qlat, ckv, qix, kix, misc = _inproj(
            x, sc1, sh1, norm1_g[l][None, :], wc, wz, wq, wkv, wqi, wm, conv_w[l], ukt,
            kv_norm_g[l][None, :], _misc_lanes(idx_k_ln_g[l], M_KIX), _misc_lanes(idx_k_ln_b[l], M_KIX),
            _misc_lanes(a_log[l], M_A), _misc_lanes(dt_bias[l], M_A), tri, tm)

        odn = _deltanet(q, k, v, z, misc, dn_norm_g[l][None, :], r_dn)
        olat = _dsa(qix, misc, kix, qlat, ckv, topk)

        x1, h2, posrow, poscol, gatecol, cpad = _outproj(
            x, odn, olat, w_uv[l].astype(BF16), w_out[l].astype(BF16), gt1, sc2, sh2,
            norm2_g[l][None, :], router_w[l].T, router_b[l][:, None], tm)

        x = _moe(h2, posrow, poscol, gatecol, cpad, exp_w_gate[l].astype(BF16), exp_w_up[l].astype(BF16),
                 exp_w_down[l].astype(BF16), sh_w_gate[l].astype(BF16), sh_w_up[l].astype(BF16),
                 sh_w_down[l].astype(BF16), x1, gt2, final_g[None, :], tm, l == depth - 1)
    return x
```

```python
import functools

import jax
import jax.numpy as jnp
from jax import lax
from jax.experimental import pallas as pl
from jax.experimental.pallas import tpu as pltpu

F32 = jnp.float32
BF16 = jnp.bfloat16
I32 = jnp.int32
HIGHEST = lax.Precision.HIGHEST

EPS = 1e-6
CHUNK = 64
DN_HEADS = 4
DN_DK = 128
DN_DV = 128
CONV_K = 4
SA_HEADS = 4
SA_DQK = 128
SA_DV = 128
KV_RANK = 256
IDX_HEADS = 4
IDX_DIM = 64
IDX_TOPK_MAX = 256
SM_SCALE = SA_DQK ** -0.5
LOG2E = 1.4426950408889634
IDX_W_SCALE = (IDX_HEADS * IDX_DIM) ** -0.5
N_EXPERTS = 64
TOP_K = 8
N_GROUPS = 8
TOPK_GROUPS = 4
D_EXPERT = 256
ROUTED_SCALE = 2.5
GATE_W = 128
RUN_ALIGN = 16
ROW_BLOCK = 1024
MOE_CHUNK = 512

DN_QK = DN_HEADS * DN_DK
DN_V = DN_HEADS * DN_DV
CONV_DIM = 2 * DN_QK + DN_V
SA_Q = SA_HEADS * SA_DQK
IDX_Q = IDX_HEADS * IDX_DIM

MISC_W = 128
M_KIX = 0
M_BETA = IDX_DIM
M_A = M_BETA + DN_HEADS
M_WIX = M_A + DN_HEADS

DN_SUB = 2 * CHUNK
QBLOCK = 256
SOFTMAX_TINY = 2.0 ** -100
KEY_TILE = 1024
PLANE_COLS = 32 * 128
PLANE_SHIFT = 12
PLANE_ROWS = 16
INT_MIN = -2 ** 31
NEG_BIG = -1e30
VMEM_LIMIT = 56 * 1024 * 1024


def _nt_dot(a, b, precision=None):
    return lax.dot_general(a, b, (((1,), (1,)), ((), ())), preferred_element_type=F32,
                           precision=precision)


def _dot(a, b, precision=None):
    return jnp.dot(a, b, preferred_element_type=F32, precision=precision)


def _silu(x):
    return x * jax.nn.sigmoid(x)


def _softplus(x):
    return jnp.maximum(x, 0.0) + jnp.log(1.0 + jnp.exp(-jnp.abs(x)))


def _ada_kernel(c_ref, w_ref, b_ref, o_ref):
    cond = _silu(c_ref[...])
    o_ref[...] = _dot(cond, w_ref[...], HIGHEST) + b_ref[...]


def _ada(c_pad, ada_w, ada_b):
    rows, d = c_pad.shape
    n_out = ada_w.shape[1]
    return pl.pallas_call(
        _ada_kernel,
        grid=(n_out // d,),
        in_specs=[pl.BlockSpec((rows, d), lambda j: (0, 0)),
                  pl.BlockSpec((d, d), lambda j: (0, j)),
                  pl.BlockSpec((1, d), lambda j: (0, j))],
        out_specs=pl.BlockSpec((rows, d), lambda j: (0, j)),
        out_shape=jax.ShapeDtypeStruct((rows, n_out), F32),
        compiler_params=pltpu.CompilerParams(vmem_limit_bytes=VMEM_LIMIT),
        name="ada",
    )(c_pad, ada_w, ada_b)


def _inproj_kernel(x_ref, sc_ref, sh_ref, g1_ref, wc_ref, wz_ref, wq_ref, wkv_ref, wqi_ref, wm_ref,
                   convw_ref, ukt_ref, kvg_ref, lng_ref, lnb_ref, alog_ref, dtb_ref, tri_ref,
                   q_ref, k_ref, v_ref, z_ref, qlat_ref, ckv_ref, qix_ref, kix_ref, misc_ref,
                   conv_buf):
    tm = x_ref.shape[1]
    i = pl.program_id(1)

    x = x_ref[0]
    h = x * lax.rsqrt(jnp.mean(x * x, axis=-1, keepdims=True) + EPS) * g1_ref[...]
    h = h * (1.0 + sc_ref[0]) + sh_ref[0]
    hb = h.astype(BF16)

    @pl.when(i == 0)
    def _():
        conv_buf[0:8, :] = jnp.zeros((8, CONV_DIM), F32)

    conv_buf[8:8 + tm, :] = _dot(hb, wc_ref[...])
    for grp, dst in ((0, q_ref), (1, k_ref), (2, v_ref)):
        cols = slice(grp * DN_QK, (grp + 1) * DN_QK)
        y = jnp.zeros((tm, DN_QK), F32)
        for j in range(CONV_K):
            y = y + convw_ref[j:j + 1, cols] * conv_buf[8 - (CONV_K - 1) + j:8 - (CONV_K - 1) + j + tm, cols]
        y = _silu(y)
        if grp < 2:
            outs = []
            for hd in range(DN_HEADS):
                yh = y[:, hd * DN_DK:(hd + 1) * DN_DK]
                yh = yh * lax.rsqrt(jnp.sum(yh * yh, axis=-1, keepdims=True) + EPS)
                if grp == 0:
                    yh = yh * (DN_DK ** -0.5)
                outs.append(yh)
            y = jnp.concatenate(outs, axis=-1)
        dst[0] = y
    conv_buf[0:8, :] = conv_buf[tm:tm + 8, :]

    z_ref[0] = _dot(hb, wz_ref[...])

    q_sa = _dot(hb, wq_ref[...]).astype(BF16)
    for hd in range(SA_HEADS):
        ql = _dot(q_sa[:, hd * SA_DQK:(hd + 1) * SA_DQK], ukt_ref[hd]) * (SM_SCALE * LOG2E)
        qlat_ref[0, hd] = ql.astype(BF16)

    ckv = _dot(hb, wkv_ref[...])
    ckv = ckv * lax.rsqrt(jnp.mean(ckv * ckv, axis=-1, keepdims=True) + EPS) * kvg_ref[...]
    ckv_ref[0] = ckv.astype(BF16)

    q_ix = _dot(hb, wqi_ref[...]).astype(BF16)
    for hd in range(IDX_HEADS):
        qix_ref[0, hd] = q_ix[:, hd * IDX_DIM:(hd + 1) * IDX_DIM]

    m = _dot(hb, wm_ref[...])
    lane = lax.broadcasted_iota(I32, (tm, MISC_W), 1)
    is_k = lane < IDX_DIM
    mu = jnp.sum(jnp.where(is_k, m, 0.0), axis=-1, keepdims=True) * (1.0 / IDX_DIM)
    kc = jnp.where(is_k, m - mu, 0.0)
    var = jnp.sum(kc * kc, axis=-1, keepdims=True) * (1.0 / IDX_DIM)
    kn = kc * lax.rsqrt(var + EPS) * lng_ref[...] + lnb_ref[...]
    kix_ref[0] = kn[:, :IDX_DIM].astype(BF16)

    beta = jax.nn.sigmoid(m)
    g = -jnp.exp(alog_ref[...]) * _softplus(m + dtb_ref[...])
    is_a = (lane >= M_A) & (lane < M_A + DN_HEADS)
    g = jnp.where(is_a, g, 0.0)
    gc = _dot(tri_ref[...], g, HIGHEST)
    is_b = (lane >= M_BETA) & (lane < M_BETA + DN_HEADS)
    is_w = (lane >= M_WIX) & (lane < M_WIX + IDX_HEADS)
    misc_ref[0] = jnp.where(is_b, beta, jnp.where(is_a, gc, jnp.where(is_w, m * IDX_W_SCALE, 0.0)))


def _inproj(x, sc1, sh1, g1, wc, wz, wq, wkv, wqi, wm, conv_w, ukt, kvg, lng, lnb, alog, dtb, tri, tm):
    B, T, D = x.shape
    nt = T // tm

    def full(a):
        nd = a.ndim
        return pl.BlockSpec(a.shape, lambda b, i, _n=nd: (0,) * _n)

    def rows(w):
        return pl.BlockSpec((1, tm, w), lambda b, i: (b, i, 0))

    per_b = pl.BlockSpec((1, 1, D), lambda b, i: (b, 0, 0))
    def head_rows(h, w):
        return pl.BlockSpec((1, h, tm, w), lambda b, i: (b, 0, i, 0))

    outs = [(None, DN_QK, F32), (None, DN_QK, F32), (None, DN_V, F32), (None, DN_V, F32),
            (SA_HEADS, KV_RANK, BF16), (None, KV_RANK, BF16), (IDX_HEADS, IDX_DIM, BF16),
            (None, IDX_DIM, BF16), (None, MISC_W, F32)]
    return pl.pallas_call(
        _inproj_kernel,
        grid=(B, nt),
        in_specs=[rows(D), per_b, per_b, full(g1), full(wc), full(wz), full(wq), full(wkv), full(wqi),
                  full(wm), full(conv_w), full(ukt), full(kvg), full(lng), full(lnb), full(alog),
                  full(dtb), full(tri)],
        out_specs=[rows(w) if h is None else head_rows(h, w) for h, w, _ in outs],
        out_shape=[jax.ShapeDtypeStruct((B, T, w) if h is None else (B, h, T, w), dt) for h, w, dt in outs],
        scratch_shapes=[pltpu.VMEM((tm + 8, CONV_DIM), F32)],
        compiler_params=pltpu.CompilerParams(dimension_semantics=("arbitrary", "arbitrary"),
                                             vmem_limit_bytes=VMEM_LIMIT),
        name="inproj",
    )(x, sc1, sh1, g1, wc, wz, wq, wkv, wqi, wm, conv_w, ukt, kvg, lng, lnb, alog, dtb, tri)


def _deltanet_kernel(q_ref, k_ref, v_ref, z_ref, misc_ref, ng_ref, o_ref, s_ref):
    R = q_ref.shape[1]
    n_chunks = R // CHUNK

    @pl.when(pl.program_id(1) == 0)
    def _():
        s_ref[...] = jnp.zeros(s_ref.shape, F32)

    misc = misc_ref[0]
    misc_t = misc.T
    SB = min(DN_SUB, R)
    row = lax.broadcasted_iota(I32, (SB, SB), 0)
    col = lax.broadcasted_iota(I32, (SB, SB), 1)
    same = (row // CHUNK) == (col // CHUNK)
    lower = same & (row >= col)
    strict = same & (row > col)
    eye = (row == col).astype(F32)

    def mm(a, b):
        return _dot(a.astype(BF16), b.astype(BF16))

    def mm3(a, b):
        ah = a.astype(BF16)
        bh = b.astype(BF16)
        al = (a - ah.astype(F32)).astype(BF16)
        bl = (b - bh.astype(F32)).astype(BF16)
        return _dot(jnp.concatenate([ah, ah, al], axis=1), jnp.concatenate([bh, bl, bh], axis=0))

    heads = range(DN_HEADS)
    subs = range(R // SB)
    chains = [(hd, sb) for hd in heads for sb in subs]
    cols = [slice(hd * DN_DK, (hd + 1) * DN_DK) for hd in heads]
    qh = [q_ref[0, :, cols[hd]] for hd in heads]
    kh = [k_ref[0, :, cols[hd]] for hd in heads]
    beta = [misc[:, M_BETA + hd:M_BETA + hd + 1] for hd in heads]
    gc_c = [misc[:, M_A + hd:M_A + hd + 1] for hd in heads]
    eg = [jnp.exp(gc_c[hd]) for hd in heads]
    kb = [kh[hd] * beta[hd] for hd in heads]
    rhs = [jnp.concatenate([v_ref[0, :, cols[hd]] * beta[hd], kb[hd] * eg[hd]], axis=-1) for hd in heads]
    q_dec = [qh[hd] * eg[hd] for hd in heads]

    def rows_of(sb):
        return slice(sb * SB, (sb + 1) * SB)

    decay, a, qk_sb = {}, {}, {}
    for hd, sb in chains:
        bs = rows_of(sb)
        gc_r = misc_t[M_A + hd:M_A + hd + 1, bs]
        decay[hd, sb] = jnp.where(lower, jnp.exp(jnp.where(lower, gc_c[hd][bs] - gc_r, 0.0)), 0.0)
    for hd, sb in chains:
        bs = rows_of(sb)
        khb = kh[hd][bs].astype(BF16)
        a[hd, sb] = jnp.where(strict, _nt_dot(kb[hd][bs].astype(BF16), khb) * decay[hd, sb], 0.0)
        qk_sb[hd, sb] = jnp.where(lower, _nt_dot(qh[hd][bs].astype(BF16), khb) * decay[hd, sb], 0.0)
    p = {ch: eye - a[ch] for ch in chains}
    xp = {ch: mm3(a[ch], a[ch]) for ch in chains}
    n_sq = 1
    while True:
        p = {ch: p[ch] + mm3(p[ch], xp[ch]) for ch in chains}
        n_sq *= 2
        if n_sq * 2 >= CHUNK:
            break
        xp = {ch: mm3(xp[ch], xp[ch]) for ch in chains}
    sol = {(hd, sb): mm(p[hd, sb], rhs[hd][rows_of(sb)]) for hd, sb in chains}

    def chunk_of(c):
        per = SB // CHUNK
        return c // per, slice((c % per) * CHUNK, (c % per + 1) * CHUNK)

    s = [s_ref[hd] for hd in heads]
    o_parts = [[] for _ in heads]
    for c in range(n_chunks):
        rs = slice(c * CHUNK, (c + 1) * CHUNK)
        sb, r = chunk_of(c)
        gl = [gc_c[hd][(c + 1) * CHUNK - 1:(c + 1) * CHUNK, :] for hd in heads]
        k_dec = [kh[hd][rs] * jnp.exp(gl[hd] - gc_c[hd][rs]) for hd in heads]
        v_new = [sol[hd, sb][r, :DN_DV] - mm(sol[hd, sb][r, DN_DV:], s[hd]) for hd in heads]
        for hd in heads:
            o_parts[hd].append(mm(q_dec[hd][rs], s[hd]) + mm(qk_sb[hd, sb][r, r], v_new[hd]))
        s = [s[hd] * jnp.exp(gl[hd]) + mm(k_dec[hd].T, v_new[hd]) for hd in heads]
    for hd in heads:
        s_ref[hd] = s[hd]
        o = jnp.concatenate(o_parts[hd], axis=0)
        o = o * lax.rsqrt(jnp.mean(o * o, axis=-1, keepdims=True) + EPS) * ng_ref[...]
        o_ref[0, :, cols[hd]] = (o * _silu(z_ref[0, :, cols[hd]])).astype(BF16)


def _deltanet(q, k, v, z, misc, ng, R):
    B, T, _ = q.shape

    def rows(w):
        return pl.BlockSpec((1, R, w), lambda b, i: (b, i, 0))

    return pl.pallas_call(
        _deltanet_kernel,
        grid=(B, T // R),
        in_specs=[rows(DN_QK), rows(DN_QK), rows(DN_V), rows(DN_V), rows(MISC_W),
                  pl.BlockSpec((1, DN_DV), lambda b, i: (0, 0))],
        out_specs=rows(DN_V),
        out_shape=jax.ShapeDtypeStruct((B, T, DN_V), BF16),
        scratch_shapes=[pltpu.VMEM((DN_HEADS, DN_DK, DN_DV), F32)],
        compiler_params=pltpu.CompilerParams(dimension_semantics=("arbitrary", "arbitrary"),
                                             vmem_limit_bytes=VMEM_LIMIT),
        name="deltanet",
    )(q, k, v, z, misc, ng)


def _bit_transpose32(words):
    w = list(words)
    j = 16
    m = 0x0000FFFF
    while j:
        k = 0
        m_i32 = jnp.int32(m - (1 << 32) if m >= (1 << 31) else m)
        while k < 32:
            t = (w[k] ^ lax.shift_right_logical(w[k + j], jnp.full_like(w[k], j))) & m_i32
            w[k] = w[k] ^ t
            w[k + j] = w[k + j] ^ jnp.left_shift(t, j)
            k = (k + j + 1) & ~j
        j >>= 1
        m = (m ^ (m << j)) & 0xFFFFFFFF
    return w


def _dsa_kernel(qix_ref, misc_ref, kix_ref, qlat_ref, ckv_ref, o_ref, keys_ref, planes_ref, bias_ref,
                mx_ref, l_ref, acc_ref, kvmax_ref, *, topk, pos_bits, n_cg_max):
    i = pl.program_id(1)
    QB = QBLOCK
    KT = KEY_TILE
    n_kt = (i * QB + QB + KT - 1) // KT

    rowi = lax.broadcasted_iota(I32, (QB, KT), 0)
    coli = lax.broadcasted_iota(I32, (QB, KT), 1)
    limit = i * QB + (rowi // CHUNK + 1) * CHUNK

    misc = misc_ref[0]
    qix = qix_ref[0].reshape(IDX_HEADS * QB, IDX_DIM)
    q_st = qlat_ref[0].reshape(SA_HEADS * QB, KV_RANK)

    def score_body(kt, carry):
        k0 = pl.multiple_of(kt * KT, KT)
        kx = kix_ref[0, pl.ds(k0, KT), :]
        rel = jnp.maximum(_nt_dot(qix, kx), 0.0)
        sc = jnp.zeros((QB, KT), F32)
        for hd in range(IDX_HEADS):
            sc = sc + misc[:, M_WIX + hd:M_WIX + hd + 1] * rel[hd * QB:(hd + 1) * QB]
        sc = jnp.where(sc == 0.0, 0.0, sc)
        bits = pltpu.bitcast(sc, I32)
        key = jnp.where(bits < 0, bits ^ 0x7FFFFFFF, bits)
        keys_ref[:, pl.ds(k0, KT)] = jnp.where(k0 + coli < limit, key, INT_MIN)
        return carry

    lax.fori_loop(0, n_kt, score_body, 0)

    n_cg = (n_kt * KT + PLANE_COLS - 1) // PLANE_COLS

    def fill_body(kt, carry):
        keys_ref[:, pl.ds(pl.multiple_of(kt * KT, KT), KT)] = jnp.full((QB, KT), INT_MIN, I32)
        return carry

    lax.fori_loop(n_kt, n_cg * (PLANE_COLS // KT), fill_body, 0)

    @pl.when(i == 0)
    def _():
        planes_ref[...] = jnp.zeros(planes_ref.shape, I32)

    def plane_body(step, carry):
        c = step // (QB // PLANE_ROWS)
        r0 = pl.multiple_of((step % (QB // PLANE_ROWS)) * PLANE_ROWS, PLANE_ROWS)
        words = [keys_ref[pl.ds(r0, PLANE_ROWS), pl.ds(pl.multiple_of(c * PLANE_COLS + j * 128, 128), 128)]
                 for j in range(32)]
        for b, plane in enumerate(_bit_transpose32(words)):
            planes_ref[c, b, pl.ds(r0, PLANE_ROWS), :] = ~plane if b == 0 else plane
        return carry

    lax.fori_loop(0, n_cg * (QB // PLANE_ROWS), plane_body, 0)

    ones_mat = jnp.ones((128, 128), BF16)

    def lane_count(words):
        pc = functools.reduce(jnp.add, [lax.population_count(x) for x in words])
        return _dot(pc.astype(F32).astype(BF16), ones_mat).astype(I32)

    def sel_body(groups, step, carry):
        cand, n_gt, tau_u = carry
        hi = [cand[c] & planes_ref[c, 2 * step] for c in groups]
        lo = [cand[c] ^ hi[c] for c in groups]
        d3 = [hi[c] & planes_ref[c, 2 * step + 1] for c in groups]
        d2 = [hi[c] ^ d3[c] for c in groups]
        d1 = [lo[c] & planes_ref[c, 2 * step + 1] for c in groups]
        d0 = [lo[c] ^ d1[c] for c in groups]
        a3 = n_gt + lane_count(d3)
        a2 = a3 + lane_count(d2)
        a1 = a2 + lane_count(d1)
        is3 = a3 >= topk
        is2 = a2 >= topk
        is1 = a1 >= topk
        cand = tuple(jnp.where(is3, d3[c], jnp.where(is2, d2[c], jnp.where(is1, d1[c], d0[c]))) for c in groups)
        n_gt = jnp.where(is3, n_gt, jnp.where(is2, a3, jnp.where(is1, a2, a1)))
        digit = jnp.where(is3, 3, jnp.where(is2, 2, jnp.where(is1, 1, 0)))
        return cand, n_gt, tau_u | jnp.left_shift(digit, 30 - 2 * step)

    def radix_select(n_groups):
        groups = range(n_groups)
        start = (tuple(jnp.full((QB, 128), -1, I32) for _ in groups), jnp.zeros((QB, 128), I32),
                 jnp.zeros((QB, 128), I32))
        cand_, n_gt_, tau_ = lax.fori_loop(0, 16, functools.partial(sel_body, groups), start)
        rest = tuple(jnp.zeros((QB, 128), I32) for _ in range(n_cg_max - n_groups))
        return cand_ + rest, n_gt_, tau_

    cand, n_gt, tau_u = lax.switch(n_cg - 1, [functools.partial(radix_select, g + 1) for g in range(n_cg_max)])
    tau = tau_u ^ INT_MIN
    sentinel = tau == INT_MIN
    cand = tuple(jnp.where(sentinel, 0, cand[c]) for c in range(n_cg_max))
    need = topk - n_gt
    any_tie = jnp.max(jnp.where(lane_count(cand) > need, 1, 0)) > 0

    def fast_bias():
        floor = jnp.where(sentinel, INT_MIN + 1, tau)

        def body(kt, carry):
            for j in range(KT // 128):
                cols = pl.ds(pl.multiple_of(kt * KT + j * 128, 128), 128)
                bias_ref[:, cols] = jnp.where(keys_ref[:, cols] >= floor, 0.0, NEG_BIG)
            return carry

        lax.fori_loop(0, n_kt, body, 0)

    def tie_bias():
        lane = lax.broadcasted_iota(I32, (QB, 128), 1)

        def pos_mask(p, c):
            cg = lax.shift_right_logical(p, jnp.full_like(p, PLANE_SHIFT))
            j0 = lax.shift_right_logical(p, jnp.full_like(p, 7)) & 31
            below = ~lax.shift_right_logical(jnp.full_like(p, -1), j0)
            bit = lax.shift_right_logical(jnp.full_like(p, INT_MIN), j0)
            word = below | jnp.where(lane < (p & 127), bit, 0)
            return jnp.where(cg > c, -1, jnp.where(cg == c, word, 0))

        def pos_body(b, q):
            cq = q + jnp.left_shift(jnp.int32(1), pos_bits - 1 - b)
            cnt = lane_count([cand[c] & pos_mask(cq, c) for c in range(n_cg_max)])
            return jnp.where(cnt < need, cq, q)

        pstar = lax.fori_loop(0, pos_bits, pos_body, jnp.zeros((QB, 128), I32)) + 1
        pstar = jnp.where(sentinel, 0, pstar)

        def body(kt, carry):
            for j in range(KT // 128):
                c0 = pl.multiple_of(kt * KT + j * 128, 128)
                kk = keys_ref[:, pl.ds(c0, 128)]
                tie = jnp.where(c0 + lane < pstar, 0.0, NEG_BIG)
                bias_ref[:, pl.ds(c0, 128)] = jnp.where(kk > tau, 0.0, jnp.where(kk == tau, tie, NEG_BIG))
            return carry

        lax.fori_loop(0, n_kt, body, 0)

    lax.cond(any_tie, tie_bias, fast_bias)

    def logit_chunks(k0):
        kv = ckv_ref[0, pl.ds(k0, KT), :]
        s = _nt_dot(q_st, kv)
        bias = bias_ref[:, pl.ds(k0, KT)]
        chunks = []
        for j in range(KT // 128):
            bj = bias[:, j * 128:(j + 1) * 128]
            chunks.append(s[:, j * 128:(j + 1) * 128] + jnp.concatenate([bj] * SA_HEADS, axis=0))
        return kv, chunks

    def weighted_sum_sweep():
        l_ref[...] = jnp.zeros(l_ref.shape, F32)
        acc_ref[...] = jnp.zeros(acc_ref.shape, F32)

        def pv_body(kt, carry):
            kv, chunks = logit_chunks(pl.multiple_of(kt * KT, KT))
            shift = mx_ref[...]
            ps = [jnp.exp2(c - shift) for c in chunks]
            l_ref[...] = functools.reduce(jnp.add, ps, l_ref[...])
            p = jnp.concatenate([pj.astype(BF16) for pj in ps], axis=1)
            acc_ref[...] += _dot(p, kv)
            return carry

        lax.fori_loop(0, n_kt, pv_body, 0)
        l_row = jnp.sum(l_ref[...], axis=-1, keepdims=True)
        o_ref[0] = (acc_ref[...] / l_row).astype(BF16).reshape(SA_HEADS, QB, KV_RANK)
        return l_row

    @pl.when(i == 0)
    def _():
        def norm_body(r, best):
            x = ckv_ref[0, pl.ds(pl.multiple_of(r * KT, KT), KT), :].astype(F32)
            return jnp.maximum(best, jnp.max(jnp.sum(x * x, axis=1, keepdims=True), axis=0, keepdims=True))

        n_rows = ckv_ref.shape[1]
        kv_sq = lax.fori_loop(0, n_rows // KT, norm_body, jnp.zeros((1, 1), F32))
        kvmax_ref[...] = jnp.broadcast_to(jnp.sqrt(kv_sq), kvmax_ref.shape)

    qf = q_st.astype(F32)
    q_norm = jnp.sqrt(jnp.sum(qf * qf, axis=1, keepdims=True))
    mx_ref[...] = jnp.broadcast_to(q_norm, mx_ref.shape) * kvmax_ref[0:1, :] * 1.001 + 1e-3
    l_fast = weighted_sum_sweep()

    @pl.when(jnp.min(l_fast) < SOFTMAX_TINY)
    def _():
        mx_ref[...] = jnp.full(mx_ref.shape, NEG_BIG, F32)

        def max_body(kt, carry):
            _, chunks = logit_chunks(pl.multiple_of(kt * KT, KT))
            mx_ref[...] = functools.reduce(jnp.maximum, chunks, mx_ref[...])
            return carry

        lax.fori_loop(0, n_kt, max_body, 0)
        mx_ref[...] = jnp.broadcast_to(jnp.max(mx_ref[...], axis=-1, keepdims=True), mx_ref.shape)
        weighted_sum_sweep()


def _dsa(qix, misc, kix, qlat, ckv, topk):
    B, T, _ = kix.shape
    n_cg_max = -(-T // PLANE_COLS)
    t_pad = n_cg_max * PLANE_COLS
    pos_bits = (t_pad - 1).bit_length()

    def rows(w):
        return pl.BlockSpec((1, QBLOCK, w), lambda b, i: (b, i, 0))

    def head_rows(h, w):
        return pl.BlockSpec((1, h, QBLOCK, w), lambda b, i: (b, 0, i, 0))

    def per_b(w):
        return pl.BlockSpec((1, T, w), lambda b, i: (b, 0, 0))

    return pl.pallas_call(
        functools.partial(_dsa_kernel, topk=topk, pos_bits=pos_bits, n_cg_max=n_cg_max),
        grid=(B, T // QBLOCK),
        in_specs=[head_rows(IDX_HEADS, IDX_DIM), rows(MISC_W), per_b(IDX_DIM), head_rows(SA_HEADS, KV_RANK),
                  per_b(KV_RANK)],
        out_specs=head_rows(SA_HEADS, KV_RANK),
        out_shape=jax.ShapeDtypeStruct((B, SA_HEADS, T, KV_RANK), BF16),
        scratch_shapes=[pltpu.VMEM((QBLOCK, t_pad), I32),
                        pltpu.VMEM((n_cg_max, 32, QBLOCK, 128), I32),
                        pltpu.VMEM((QBLOCK, t_pad), F32),
                        pltpu.VMEM((SA_HEADS * QBLOCK, 128), F32),
                        pltpu.VMEM((SA_HEADS * QBLOCK, 128), F32),
                        pltpu.VMEM((SA_HEADS * QBLOCK, KV_RANK), F32),
                        pltpu.VMEM((8, 128), F32)],
        compiler_params=pltpu.CompilerParams(dimension_semantics=("arbitrary", "arbitrary"),
                                             vmem_limit_bytes=VMEM_LIMIT),
        name="dsa",
    )(qix, misc, kix, qlat, ckv)


def _first_max(v, idx, axis):
    m = jnp.max(v, axis=axis, keepdims=True)
    big = jnp.int32(2 ** 30)
    first = jnp.min(jnp.where(v == m, idx, big), axis=axis, keepdims=True)
    return m, idx == first


def _outproj_kernel(x_ref, odn_ref, olat_ref, uv_ref, wo_ref, gt_ref, sc_ref, sh_ref, g2_ref, rwt_ref,
                    rb_ref, lstrict_ref, ustrict_ref, x1_ref, h2_ref, posrow_ref, poscol_ref, gatecol_ref,
                    cpad_ref):
    tm = x_ref.shape[1]
    parts = [odn_ref[0]]
    for hd in range(SA_HEADS):
        parts.append(_dot(olat_ref[0, hd], uv_ref[hd]).astype(BF16))
    mix = jnp.concatenate(parts, axis=-1)
    x1 = x_ref[0] + gt_ref[0] * _dot(mix, wo_ref[...])
    x1_ref[0] = x1
    h2 = x1 * lax.rsqrt(jnp.mean(x1 * x1, axis=-1, keepdims=True) + EPS) * g2_ref[...]
    h2 = h2 * (1.0 + sc_ref[0]) + sh_ref[0]
    h2_ref[0] = h2.astype(BF16)

    per_g = N_EXPERTS // N_GROUPS
    s = jax.nn.sigmoid(_nt_dot(rwt_ref[...], h2, HIGHEST))
    choice = s + rb_ref[...]
    ig = lax.broadcasted_iota(I32, (per_g, tm), 0)
    gscore = []
    for gidx in range(N_GROUPS):
        cg = choice[gidx * per_g:(gidx + 1) * per_g]
        m1, hot1 = _first_max(cg, ig, 0)
        gscore.append(m1 + jnp.max(jnp.where(hot1, -jnp.inf, cg), axis=0, keepdims=True))
    gsel = [jnp.zeros((1, tm), jnp.bool_) for _ in range(N_GROUPS)]
    for _ in range(TOPK_GROUPS):
        best = functools.reduce(jnp.maximum, gscore)
        found = jnp.zeros((1, tm), jnp.bool_)
        for gidx in range(N_GROUPS):
            hot = (gscore[gidx] == best) & jnp.logical_not(found)
            found = found | hot
            gsel[gidx] = gsel[gidx] | hot
            gscore[gidx] = jnp.where(hot, -jnp.inf, gscore[gidx])
    masked = jnp.concatenate(
        [jnp.where(gsel[gidx], choice[gidx * per_g:(gidx + 1) * per_g], -jnp.inf) for gidx in range(N_GROUPS)],
        axis=0)
    ei = lax.broadcasted_iota(I32, masked.shape, 0)
    gate = jnp.zeros(masked.shape, F32)
    hots = []
    for _ in range(TOP_K):
        _, hot = _first_max(masked, ei, 0)
        hots.append(hot)
        gate = jnp.where(hot, s, gate)
        masked = jnp.where(hot, -jnp.inf, masked)
    gate = gate / jnp.sum(gate, axis=0, keepdims=True) * ROUTED_SCALE

    picked = jnp.where(functools.reduce(jnp.logical_or, hots), 1.0, 0.0)
    cnt = jnp.sum(picked, axis=1, keepdims=True)
    cpad = jnp.floor((cnt + (RUN_ALIGN - 1)) * (1.0 / RUN_ALIGN)) * RUN_ALIGN
    cpad_b = jnp.broadcast_to(cpad, (N_EXPERTS, GATE_W))
    lbase = _dot(lstrict_ref[...], cpad_b, HIGHEST)[:, :1]
    rank = _dot(picked.astype(BF16), ustrict_ref[...])
    pos = lbase + rank
    ri = lax.broadcasted_iota(I32, (GATE_W, tm), 0)
    pos_rows = jnp.zeros((GATE_W, tm), F32)
    gate_rows = jnp.zeros((GATE_W, tm), F32)
    for k, hot in enumerate(hots):
        pos_rows = jnp.where(ri == k, jnp.sum(jnp.where(hot, pos, 0.0), axis=0, keepdims=True), pos_rows)
        gate_rows = jnp.where(ri == k, jnp.sum(jnp.where(hot, gate, 0.0), axis=0, keepdims=True), gate_rows)
    posrow_ref[0, 0] = pos_rows[:TOP_K].astype(I32)
    poscol_ref[0] = pos_rows.T.astype(I32)
    gatecol_ref[0] = gate_rows.T
    cpad_ref[0, 0] = cpad_b.astype(I32)


def _outproj(x, odn, olat, uv, wo, gt1, sc2, sh2, g2, rwt, rb, tm):
    B, T, D = x.shape
    nt = T // tm
    ex = jnp.arange(N_EXPERTS)
    lstrict = (ex[:, None] > ex[None, :]).astype(F32)
    tok = jnp.arange(tm)
    ustrict = (tok[:, None] < tok[None, :]).astype(BF16)

    def full(a):
        nd = a.ndim
        return pl.BlockSpec(a.shape, lambda b, i, _n=nd: (0,) * _n)

    def rows(w):
        return pl.BlockSpec((1, tm, w), lambda b, i: (b, i, 0))

    def per_tile(h, w):
        return pl.BlockSpec((1, 1, h, w), lambda b, i: (b, i, 0, 0))

    per_b = pl.BlockSpec((1, 1, D), lambda b, i: (b, 0, 0))
    return pl.pallas_call(
        _outproj_kernel,
        grid=(B, nt),
        in_specs=[rows(D), rows(DN_V),
                  pl.BlockSpec((1, SA_HEADS, tm, KV_RANK), lambda b, i: (b, 0, i, 0)),
                  full(uv), full(wo), per_b, per_b, per_b,
                  full(g2), full(rwt), full(rb), full(lstrict), full(ustrict)],
        out_specs=[rows(D), rows(D), per_tile(TOP_K, tm), rows(GATE_W), rows(GATE_W),
                   per_tile(N_EXPERTS, GATE_W)],
        out_shape=[jax.ShapeDtypeStruct((B, T, D), F32), jax.ShapeDtypeStruct((B, T, D), BF16),
                   jax.ShapeDtypeStruct((B, nt, TOP_K, tm), I32),
                   jax.ShapeDtypeStruct((B, T, GATE_W), I32),
                   jax.ShapeDtypeStruct((B, T, GATE_W), F32),
                   jax.ShapeDtypeStruct((B, nt, N_EXPERTS, GATE_W), I32)],
        compiler_params=pltpu.CompilerParams(dimension_semantics=("arbitrary", "arbitrary"),
                                             vmem_limit_bytes=VMEM_LIMIT),
        name="outproj",
    )(x, odn, olat, uv, wo, gt1, sc2, sh2, g2, rwt, rb, lstrict, ustrict)


def _piece_sizes(max_rows):
    sizes = []
    z = RUN_ALIGN
    while z <= max_rows:
        sizes.append(z)
        z *= 2
    return sizes[::-1]


def _for_run_pieces(length, max_rows, fn):
    for z in _piece_sizes(max_rows):
        start = length & ~(2 * z - 1)

        @pl.when((length & z) != 0)
        def _(start=start, z=z):
            fn(start, z)


def _plan_kernel(cp_ref, off_ref, lb_ref, foff_ref, flen_ref, blk_ref, nused_ref):
    cp = cp_ref[...].astype(F32)
    n, ne = cp.shape
    ei = lax.broadcasted_iota(I32, (ne, ne), 0)
    ej = lax.broadcasted_iota(I32, (ne, ne), 1)
    si = lax.broadcasted_iota(I32, (n, n), 0)
    sj = lax.broadcasted_iota(I32, (n, n), 1)
    lb = _dot(cp, (ei < ej).astype(F32), HIGHEST)
    earlier_tiles = _dot((sj < si).astype(F32), cp, HIGHEST)
    rows_e = jnp.sum(cp, axis=0, keepdims=True)
    region = jnp.floor((rows_e + (ROW_BLOCK - 1)) * (1.0 / ROW_BLOCK)) * ROW_BLOCK
    region_b = jnp.broadcast_to(region, (ne, ne))
    rend_row = _dot(region_b, (ei <= ej).astype(F32), HIGHEST)[:1]
    rend_col = jnp.sum(jnp.where(ej <= ei, region_b, 0.0), axis=1, keepdims=True)
    base = rend_row - region
    total = jnp.max(rend_row, axis=1, keepdims=True)
    off_ref[...] = (base + earlier_tiles).astype(I32)
    lb_ref[...] = lb.astype(I32)
    lane = lax.broadcasted_iota(I32, (1, GATE_W), 1)
    pad = jnp.zeros((1, GATE_W - ne), F32)
    foff_ref[...] = jnp.where(lane == ne, total, jnp.concatenate([base + rows_e, pad], axis=1)).astype(I32)
    flen_ref[...] = jnp.concatenate([region - rows_e, pad], axis=1).astype(I32)
    n_used = total * (1.0 / ROW_BLOCK)
    nused_ref[...] = jnp.broadcast_to(n_used, nused_ref.shape).astype(I32)
    bi = lax.broadcasted_iota(I32, (ne, blk_ref.shape[1]), 1).astype(F32)
    ended = jnp.where(rend_col * (1.0 / ROW_BLOCK) <= jnp.minimum(bi, n_used - 1.0), 1.0, 0.0)
    blk_ref[...] = jnp.minimum(jnp.sum(ended, axis=0, keepdims=True), ne - 1.0).astype(I32)


def _plan(cp, n_blocks):
    n, ne = cp.shape
    nb_pad = -(-n_blocks // 128) * 128
    return pl.pallas_call(
        _plan_kernel,
        out_shape=[jax.ShapeDtypeStruct((n, ne), I32), jax.ShapeDtypeStruct((n, ne), I32),
                   jax.ShapeDtypeStruct((1, GATE_W), I32), jax.ShapeDtypeStruct((1, GATE_W), I32),
                   jax.ShapeDtypeStruct((1, nb_pad), I32), jax.ShapeDtypeStruct((1, GATE_W), I32)],
        name="moe_plan",
    )(cp)


def _dispatch_kernel(off_ref, cp_ref, lb_ref, foff_ref, flen_ref, h_ref, posrow_ref, xs_hbm, buf, zbuf, sem,
                     zsem, *, n_steps, tile):
    s = pl.program_id(0)
    slot = s % 2
    jmax = buf.shape[1]

    def run_copies(step, slot_, act):
        def body(e, carry):
            idx = step * N_EXPERTS + e
            lb = lb_ref[idx]
            of = off_ref[idx]

            def piece(start, z):
                act(pltpu.make_async_copy(
                    buf.at[slot_, pl.ds(pl.multiple_of(lb + start, RUN_ALIGN), z)],
                    xs_hbm.at[pl.ds(pl.multiple_of(of + start, RUN_ALIGN), z)], sem.at[slot_]))

            _for_run_pieces(cp_ref[idx], tile, piece)
            return carry

        lax.fori_loop(0, N_EXPERTS, body, 0)

    def fill_copies(act):
        def body(e, carry):
            fo = foff_ref[e]

            def piece(start, z):
                act(pltpu.make_async_copy(
                    zbuf.at[pl.ds(0, z)], xs_hbm.at[pl.ds(pl.multiple_of(fo + start, RUN_ALIGN), z)], zsem.at[0]))

            _for_run_pieces(flen_ref[e], ROW_BLOCK // 2, piece)
            return carry

        lax.fori_loop(0, N_EXPERTS, body, 0)

        def tail(r, carry):
            act(pltpu.make_async_copy(
                zbuf, xs_hbm.at[pl.ds(pl.multiple_of(foff_ref[N_EXPERTS] + r * zbuf.shape[0], RUN_ALIGN),
                                      zbuf.shape[0])], zsem.at[0]))
            return carry

        lax.fori_loop(0, (xs_hbm.shape[0] - foff_ref[N_EXPERTS]) // zbuf.shape[0], tail, 0)

    @pl.when(s == 0)
    def _():
        zbuf[...] = jnp.zeros(zbuf.shape, BF16)
        fill_copies(lambda c: c.start())

    @pl.when(s >= 2)
    def _():
        run_copies(s - 2, slot, lambda c: c.wait())

    h = h_ref[...]
    last = s * N_EXPERTS + N_EXPERTS - 1
    jused = lb_ref[last] + cp_ref[last]
    def local_rows(jc):
        ji = (lax.broadcasted_iota(I32, (MOE_CHUNK, tile), 0) + jc * MOE_CHUNK).astype(jnp.int16)
        p = jnp.zeros((MOE_CHUNK, tile), BF16)
        for k in range(TOP_K):
            p = jnp.where(ji == posrow_ref[0, k:k + 1, :].astype(jnp.int16), jnp.ones((), BF16), p)
        return _dot(p, h).astype(BF16)

    n_full = TOP_K * tile // MOE_CHUNK
    rows_full = [local_rows(jc) for jc in range(n_full)]
    for jc in range(n_full):
        buf[slot, jc * MOE_CHUNK:(jc + 1) * MOE_CHUNK, :] = rows_full[jc]
    for jc in range(n_full, jmax // MOE_CHUNK):
        @pl.when(jc * MOE_CHUNK < jused)
        def _(jc=jc):
            buf[slot, jc * MOE_CHUNK:(jc + 1) * MOE_CHUNK, :] = local_rows(jc)

    run_copies(s, slot, lambda c: c.start())

    @pl.when(s == n_steps - 1)
    def _():
        if n_steps >= 2:
            run_copies(s - 1, 1 - slot, lambda c: c.wait())
        run_copies(s, slot, lambda c: c.wait())
        fill_copies(lambda c: c.wait())


def _dispatch(h2, posrow, off, cp, lb, foff, flen, cap, tile, jmax):
    n_tok, D = h2.shape
    n_steps = n_tok // tile
    return pl.pallas_call(
        functools.partial(_dispatch_kernel, n_steps=n_steps, tile=tile),
        grid_spec=pltpu.PrefetchScalarGridSpec(
            num_scalar_prefetch=5,
            grid=(n_steps,),
            in_specs=[pl.BlockSpec((tile, D), lambda s, *_: (s, 0)),
                      pl.BlockSpec((1, TOP_K, tile), lambda s, *_: (s, 0, 0))],
            out_specs=pl.BlockSpec(memory_space=pl.ANY),
            scratch_shapes=[pltpu.VMEM((2, jmax, D), BF16), pltpu.VMEM((ROW_BLOCK // 2, D), BF16),
                            pltpu.SemaphoreType.DMA((2,)), pltpu.SemaphoreType.DMA((1,))]),
        out_shape=jax.ShapeDtypeStruct((cap, D), BF16),
        compiler_params=pltpu.CompilerParams(dimension_semantics=("arbitrary",), vmem_limit_bytes=VMEM_LIMIT),
        name="moe_dispatch",
    )(off, cp, lb, foff, flen, h2, posrow)


def _expert_kernel(blk_e_ref, nused_ref, xs_ref, wg_ref, wu_ref, wd_ref, ys_ref):
    used = pl.program_id(0) < nused_ref[0]

    @pl.when(used)
    def _():
        xb = xs_ref[...]
        a = _silu(_dot(xb, wg_ref[0])) * _dot(xb, wu_ref[0])
        ys_ref[...] = _dot(a.astype(BF16), wd_ref[0]).astype(BF16)

    @pl.when(jnp.logical_not(used))
    def _():
        ys_ref[...] = jnp.zeros(ys_ref.shape, BF16)


def _experts(xs, blk_e, n_used, wg, wu, wd):
    cap, D = xs.shape

    def row_block(i, be, nu):
        return (jnp.minimum(i, nu[0] - 1), 0)

    def out_block(i, be, nu):
        return (i, 0)

    def weight(i, be, nu):
        return (be[i], 0, 0)

    return pl.pallas_call(
        _expert_kernel,
        grid_spec=pltpu.PrefetchScalarGridSpec(
            num_scalar_prefetch=2,
            grid=(cap // ROW_BLOCK,),
            in_specs=[pl.BlockSpec((ROW_BLOCK, D), row_block),
                      pl.BlockSpec((1, D, D_EXPERT), weight), pl.BlockSpec((1, D, D_EXPERT), weight),
                      pl.BlockSpec((1, D_EXPERT, D), weight)],
            out_specs=pl.BlockSpec((ROW_BLOCK, D), out_block)),
        out_shape=jax.ShapeDtypeStruct((cap, D), BF16),
        compiler_params=pltpu.CompilerParams(dimension_semantics=("arbitrary",), vmem_limit_bytes=VMEM_LIMIT),
        name="moe_experts",
    )(blk_e, n_used, xs, wg, wu, wd)


def _combine_kernel(off_ref, cp_ref, lb_ref, ys_hbm, poscol_ref, gatecol_ref, h_ref, sg_ref, su_ref, sd_ref,
                    x1_ref, gt_ref, fg_ref, o_ref, buf, sem, acc_ref, *, n_steps, tile, final_norm):
    s = pl.program_id(0)
    slot = s % 2
    jmax = buf.shape[1]

    def run_copies(step, slot_, act):
        def body(e, carry):
            idx = step * N_EXPERTS + e
            lb = lb_ref[idx]
            of = off_ref[idx]

            def piece(start, z):
                act(pltpu.make_async_copy(
                    ys_hbm.at[pl.ds(pl.multiple_of(of + start, RUN_ALIGN), z)],
                    buf.at[slot_, pl.ds(pl.multiple_of(lb + start, RUN_ALIGN), z)], sem.at[slot_]))

            _for_run_pieces(cp_ref[idx], tile, piece)
            return carry

        lax.fori_loop(0, N_EXPERTS, body, 0)

    @pl.when(s == 0)
    def _():
        run_copies(0, 0, lambda c: c.start())

    @pl.when(s + 1 < n_steps)
    def _():
        run_copies(s + 1, 1 - slot, lambda c: c.start())

    hb = h_ref[...]
    shared = (_silu(_dot(hb, sg_ref[...])) * _dot(hb, su_ref[...])).astype(BF16)
    acc_ref[...] = _dot(shared, sd_ref[...])

    run_copies(s, slot, lambda c: c.wait())
    last = s * N_EXPERTS + N_EXPERTS - 1
    jused = lb_ref[last] + cp_ref[last]

    def zero_body(r, carry):
        buf[slot, pl.ds(pl.multiple_of(jused + r * RUN_ALIGN, RUN_ALIGN), RUN_ALIGN), :] = jnp.zeros(
            (RUN_ALIGN, buf.shape[2]), BF16)
        return carry

    chunk_end = (jused + MOE_CHUNK - 1) // MOE_CHUNK * MOE_CHUNK
    lax.fori_loop(0, (chunk_end - jused) // RUN_ALIGN, zero_body, 0)

    def gate_rows(jc):
        ji = (lax.broadcasted_iota(I32, (tile, MOE_CHUNK), 1) + jc * MOE_CHUNK).astype(jnp.int16)
        g = jnp.zeros((tile, MOE_CHUNK), BF16)
        for k in range(TOP_K):
            g = jnp.where(ji == poscol_ref[:, k:k + 1].astype(jnp.int16), gatecol_ref[:, k:k + 1].astype(BF16), g)
        return g

    n_full = TOP_K * tile // MOE_CHUNK
    g_full = jnp.concatenate([gate_rows(jc) for jc in range(n_full)], axis=1)
    acc_ref[...] += _dot(g_full, buf[slot, 0:n_full * MOE_CHUNK, :])
    for jc in range(n_full, jmax // MOE_CHUNK):
        @pl.when(jc * MOE_CHUNK < jused)
        def _(jc=jc):
            acc_ref[...] += _dot(gate_rows(jc), buf[slot, jc * MOE_CHUNK:(jc + 1) * MOE_CHUNK, :])

    y = x1_ref[...] + gt_ref[0] * acc_ref[...]
    if final_norm:
        y = y * lax.rsqrt(jnp.mean(y * y, axis=-1, keepdims=True) + EPS) * fg_ref[...]
    o_ref[...] = y


def _combine(ys, poscol, gatecol, h2, sg, su, sd, x1, gt2, fg, off, cp, lb, tile, jmax, tiles_per_batch,
             final_norm):
    n_tok, D = h2.shape
    n_steps = n_tok // tile

    def full(a):
        nd = a.ndim
        return pl.BlockSpec(a.shape, lambda s, *_, _n=nd: (0,) * _n)

    def rows(w):
        return pl.BlockSpec((tile, w), lambda s, *_: (s, 0))

    return pl.pallas_call(
        functools.partial(_combine_kernel, n_steps=n_steps, tile=tile, final_norm=final_norm),
        grid_spec=pltpu.PrefetchScalarGridSpec(
            num_scalar_prefetch=3,
            grid=(n_steps,),
            in_specs=[pl.BlockSpec(memory_space=pl.ANY), rows(GATE_W), rows(GATE_W), rows(D),
                      full(sg), full(su), full(sd), rows(D),
                      pl.BlockSpec((1, 1, D), lambda s, *_: (s // tiles_per_batch, 0, 0)), full(fg)],
            out_specs=rows(D),
            scratch_shapes=[pltpu.VMEM((2, jmax, D), BF16), pltpu.SemaphoreType.DMA((2,)),
                            pltpu.VMEM((tile, D), F32)]),
        out_shape=jax.ShapeDtypeStruct((n_tok, D), F32),
        compiler_params=pltpu.CompilerParams(dimension_semantics=("arbitrary",), vmem_limit_bytes=VMEM_LIMIT),
        name="moe_combine",
    )(off, cp, lb, ys, poscol, gatecol, h2, sg, su, sd, x1, gt2, fg)


def _moe(h2, posrow, poscol, gatecol, cpad, wg, wu, wd, sg, su, sd, x1, gt2, fg, tile, final_norm):
    B, T, D = x1.shape
    n_tok = B * T
    n_tiles = n_tok // tile
    jmax = -(-(TOP_K * tile + N_EXPERTS * (RUN_ALIGN - 1)) // MOE_CHUNK) * MOE_CHUNK
    cap = -(-(TOP_K * n_tok + n_tiles * N_EXPERTS * (RUN_ALIGN - 1) + N_EXPERTS * (ROW_BLOCK - RUN_ALIGN))
            // ROW_BLOCK) * ROW_BLOCK

    cp = cpad[..., 0].reshape(n_tiles, N_EXPERTS)
    off, lb, foff, flen, blk_e, n_used = _plan(cp, cap // ROW_BLOCK)
    flat = lambda a: a.reshape(-1)

    xs = _dispatch(h2.reshape(n_tok, D), posrow.reshape(n_tiles, TOP_K, tile), flat(off), flat(cp), flat(lb),
                   foff[0], flen[0], cap, tile, jmax)
    ys = _experts(xs, blk_e[0], n_used[0], wg, wu, wd)
    out = _combine(ys, poscol.reshape(n_tok, GATE_W), gatecol.reshape(n_tok, GATE_W), h2.reshape(n_tok, D),
                   sg, su, sd, x1.reshape(n_tok, D), gt2, fg, flat(off), flat(cp), flat(lb), tile, jmax,
                   T // tile, final_norm)
    return out.reshape(B, T, D)


def _misc_lanes(vec, start):
    return jnp.zeros((1, MISC_W), F32).at[0, start:start + vec.shape[0]].set(vec.astype(F32))


def kernel(x, c, ada_w, ada_b, norm1_g, w_in, conv_w, a_log, dt_bias, dn_norm_g, kv_norm_g, w_uk, w_uv,
           idx_k_ln_g, idx_k_ln_b, w_out, norm2_g, router_w, router_b, exp_w_gate, exp_w_up, exp_w_down,
           sh_w_gate, sh_w_up, sh_w_down, final_g):
    B, T, D = x.shape
    depth = ada_w.shape[0]
    topk = min(IDX_TOPK_MAX, T // 4)
    tm = min(512, T)
    r_dn = min(256, T)

    cond_in = jnp.zeros((8, D), F32).at[:B].set(c)
    pos = jnp.arange(tm)
    tri = ((pos[:, None] // CHUNK == pos[None, :] // CHUNK) & (pos[:, None] >= pos[None, :])).astype(F32)

    for l in range(depth):
        mod = _ada(cond_in, ada_w[l], ada_b[l][None, :])[:B]
        sh1, sc1, gt1, sh2, sc2, gt2 = [m[:, None, :] for m in jnp.split(mod, 6, axis=-1)]

        offs = [0]
        for s in (DN_QK, DN_QK, DN_V, DN_V, DN_HEADS, DN_HEADS, SA_Q, KV_RANK, IDX_Q, IDX_DIM, IDX_HEADS):
            offs.append(offs[-1] + s)
        w = w_in[l]
        wc = w[:, offs[0]:offs[3]].astype(BF16)
        wz = w[:, offs[3]:offs[4]].astype(BF16)
        wq = w[:, offs[6]:offs[7]].astype(BF16)
        wkv = w[:, offs[7]:offs[8]].astype(BF16)
        wqi = w[:, offs[8]:offs[9]].astype(BF16)
        wm = jnp.concatenate([w[:, offs[9]:offs[10]], w[:, offs[4]:offs[5]], w[:, offs[5]:offs[6]],
                              w[:, offs[10]:offs[11]],
                              jnp.zeros((D, MISC_W - IDX_DIM - 2 * DN_HEADS - IDX_HEADS), F32)],
                             axis=1).astype(BF16)
        ukt = jnp.swapaxes(w_uk[l], 1, 2).astype(BF16)

        q, k, v, z, qlat, ckv, qix, kix, misc = _inproj(
            x, sc1, sh1, norm1_g[l][None, :], wc, wz, wq, wkv, wqi, wm, conv_w[l], ukt,
            kv_norm_g[l][None, :], _misc_lanes(idx_k_ln_g[l], M_KIX), _misc_lanes(idx_k_ln_b[l], M_KIX),
            _misc_lanes(a_log[l], M_A), _misc_lanes(dt_bias[l], M_A), tri, tm)

        odn = _deltanet(q, k, v, z, misc, dn_norm_g[l][None, :], r_dn)
        olat = _dsa(qix, misc, kix, qlat, ckv, topk)

        x1, h2, posrow, poscol, gatecol, cpad = _outproj(
            x, odn, olat, w_uv[l].astype(BF16), w_out[l].astype(BF16), gt1, sc2, sh2,
            norm2_g[l][None, :], router_w[l].T, router_b[l][:, None], tm)

        x = _moe(h2, posrow, poscol, gatecol, cpad, exp_w_gate[l].astype(BF16), exp_w_up[l].astype(BF16),
                 exp_w_down[l].astype(BF16), sh_w_gate[l].astype(BF16), sh_w_up[l].astype(BF16),
                 sh_w_down[l].astype(BF16), x1, gt2, final_g[None, :], tm, l == depth - 1)
    return x
```

```python
import functools

import jax
import jax.numpy as jnp
from jax import lax
from jax.experimental import pallas as pl
from jax.experimental.pallas import tpu as pltpu

F32 = jnp.float32
BF16 = jnp.bfloat16
I32 = jnp.int32
HIGHEST = lax.Precision.HIGHEST

EPS = 1e-6
CHUNK = 64
DN_HEADS = 4
DN_DK = 128
DN_DV = 128
CONV_K = 4
SA_HEADS = 4
SA_DQK = 128
SA_DV = 128
KV_RANK = 256
IDX_HEADS = 4
IDX_DIM = 64
IDX_TOPK_MAX = 256
SM_SCALE = SA_DQK ** -0.5
LOG2E = 1.4426950408889634
IDX_W_SCALE = (IDX_HEADS * IDX_DIM) ** -0.5
N_EXPERTS = 64
TOP_K = 8
N_GROUPS = 8
TOPK_GROUPS = 4
D_EXPERT = 256
ROUTED_SCALE = 2.5
GATE_W = 128
RUN_ALIGN = 16
ROW_BLOCK = 1024
MOE_CHUNK = 512

DN_QK = DN_HEADS * DN_DK
DN_V = DN_HEADS * DN_DV
CONV_DIM = 2 * DN_QK + DN_V
SA_Q = SA_HEADS * SA_DQK
IDX_Q = IDX_HEADS * IDX_DIM

MISC_W = 128
M_KIX = 0
M_BETA = IDX_DIM
M_A = M_BETA + DN_HEADS
M_WIX = M_A + DN_HEADS

DN_SUB = 2 * CHUNK
QBLOCK = 256
SOFTMAX_TINY = 2.0 ** -100
KEY_TILE = 1024
PLANE_COLS = 32 * 128
PLANE_SHIFT = 12
PLANE_ROWS = 16
INT_MIN = -2 ** 31
NEG_BIG = -1e30
VMEM_LIMIT = 56 * 1024 * 1024


def _nt_dot(a, b, precision=None):
    return lax.dot_general(a, b, (((1,), (1,)), ((), ())), preferred_element_type=F32,
                           precision=precision)


def _dot(a, b, precision=None):
    return jnp.dot(a, b, preferred_element_type=F32, precision=precision)


def _silu(x):
    return x * jax.nn.sigmoid(x)


def _softplus(x):
    return jnp.maximum(x, 0.0) + jnp.log(1.0 + jnp.exp(-jnp.abs(x)))


def _ada_kernel(c_ref, w_ref, b_ref, o_ref):
    cond = _silu(c_ref[...])
    o_ref[...] = _dot(cond, w_ref[...], HIGHEST) + b_ref[...]


def _ada(c_pad, ada_w, ada_b):
    rows, d = c_pad.shape
    n_out = ada_w.shape[1]
    return pl.pallas_call(
        _ada_kernel,
        grid=(n_out // d,),
        in_specs=[pl.BlockSpec((rows, d), lambda j: (0, 0)),
                  pl.BlockSpec((d, d), lambda j: (0, j)),
                  pl.BlockSpec((1, d), lambda j: (0, j))],
        out_specs=pl.BlockSpec((rows, d), lambda j: (0, j)),
        out_shape=jax.ShapeDtypeStruct((rows, n_out), F32),
        compiler_params=pltpu.CompilerParams(vmem_limit_bytes=VMEM_LIMIT),
        name="ada",
    )(c_pad, ada_w, ada_b)


def _inproj_kernel(x_ref, sc_ref, sh_ref, g1_ref, wc_ref, wz_ref, wq_ref, wkv_ref, wqi_ref, wm_ref,
                   convw_ref, ukt_ref, kvg_ref, lng_ref, lnb_ref, alog_ref, dtb_ref, tri_ref,
                   q_ref, k_ref, v_ref, z_ref, qlat_ref, ckv_ref, qix_ref, kix_ref, misc_ref,
                   conv_buf):
    tm = x_ref.shape[1]
    i = pl.program_id(1)

    x = x_ref[0]
    h = x * lax.rsqrt(jnp.mean(x * x, axis=-1, keepdims=True) + EPS) * g1_ref[...]
    h = h * (1.0 + sc_ref[0]) + sh_ref[0]
    hb = h.astype(BF16)

    @pl.when(i == 0)
    def _():
        conv_buf[0:8, :] = jnp.zeros((8, CONV_DIM), F32)

    conv_buf[8:8 + tm, :] = _dot(hb, wc_ref[...])
    for grp, dst in ((0, q_ref), (1, k_ref), (2, v_ref)):
        cols = slice(grp * DN_QK, (grp + 1) * DN_QK)
        y = jnp.zeros((tm, DN_QK), F32)
        for j in range(CONV_K):
            y = y + convw_ref[j:j + 1, cols] * conv_buf[8 - (CONV_K - 1) + j:8 - (CONV_K - 1) + j + tm, cols]
        y = _silu(y)
        if grp < 2:
            outs = []
            for hd in range(DN_HEADS):
                yh = y[:, hd * DN_DK:(hd + 1) * DN_DK]
                yh = yh * lax.rsqrt(jnp.sum(yh * yh, axis=-1, keepdims=True) + EPS)
                if grp == 0:
                    yh = yh * (DN_DK ** -0.5)
                outs.append(yh)
            y = jnp.concatenate(outs, axis=-1)
        dst[0] = y
    conv_buf[0:8, :] = conv_buf[tm:tm + 8, :]

    z_ref[0] = _dot(hb, wz_ref[...])

    q_sa = _dot(hb, wq_ref[...]).astype(BF16)
    for hd in range(SA_HEADS):
        ql = _dot(q_sa[:, hd * SA_DQK:(hd + 1) * SA_DQK], ukt_ref[hd]) * (SM_SCALE * LOG2E)
        qlat_ref[0, hd] = ql.astype(BF16)

    ckv = _dot(hb, wkv_ref[...])
    ckv = ckv * lax.rsqrt(jnp.mean(ckv * ckv, axis=-1, keepdims=True) + EPS) * kvg_ref[...]
    ckv_ref[0] = ckv.astype(BF16)

    q_ix = _dot(hb, wqi_ref[...]).astype(BF16)
    for hd in range(IDX_HEADS):
        qix_ref[0, hd] = q_ix[:, hd * IDX_DIM:(hd + 1) * IDX_DIM]

    m = _dot(hb, wm_ref[...])
    lane = lax.broadcasted_iota(I32, (tm, MISC_W), 1)
    is_k = lane < IDX_DIM
    mu = jnp.sum(jnp.where(is_k, m, 0.0), axis=-1, keepdims=True) * (1.0 / IDX_DIM)
    kc = jnp.where(is_k, m - mu, 0.0)
    var = jnp.sum(kc * kc, axis=-1, keepdims=True) * (1.0 / IDX_DIM)
    kn = kc * lax.rsqrt(var + EPS) * lng_ref[...] + lnb_ref[...]
    kix_ref[0] = kn[:, :IDX_DIM].astype(BF16)

    beta = jax.nn.sigmoid(m)
    g = -jnp.exp(alog_ref[...]) * _softplus(m + dtb_ref[...])
    is_a = (lane >= M_A) & (lane < M_A + DN_HEADS)
    g = jnp.where(is_a, g, 0.0)
    gc = _dot(tri_ref[...], g, HIGHEST)
    is_b = (lane >= M_BETA) & (lane < M_BETA + DN_HEADS)
    is_w = (lane >= M_WIX) & (lane < M_WIX + IDX_HEADS)
    misc_ref[0] = jnp.where(is_b, beta, jnp.where(is_a, gc, jnp.where(is_w, m * IDX_W_SCALE, 0.0)))


def _inproj(x, sc1, sh1, g1, wc, wz, wq, wkv, wqi, wm, conv_w, ukt, kvg, lng, lnb, alog, dtb, tri, tm):
    B, T, D = x.shape
    nt = T // tm

    def full(a):
        nd = a.ndim
        return pl.BlockSpec(a.shape, lambda b, i, _n=nd: (0,) * _n)

    def rows(w):
        return pl.BlockSpec((1, tm, w), lambda b, i: (b, i, 0))

    per_b = pl.BlockSpec((1, 1, D), lambda b, i: (b, 0, 0))
    def head_rows(h, w):
        return pl.BlockSpec((1, h, tm, w), lambda b, i: (b, 0, i, 0))

    outs = [(None, DN_QK, F32), (None, DN_QK, F32), (None, DN_V, F32), (None, DN_V, F32),
            (SA_HEADS, KV_RANK, BF16), (None, KV_RANK, BF16), (IDX_HEADS, IDX_DIM, BF16),
            (None, IDX_DIM, BF16), (None, MISC_W, F32)]
    return pl.pallas_call(
        _inproj_kernel,
        grid=(B, nt),
        in_specs=[rows(D), per_b, per_b, full(g1), full(wc), full(wz), full(wq), full(wkv), full(wqi),
                  full(wm), full(conv_w), full(ukt), full(kvg), full(lng), full(lnb), full(alog),
                  full(dtb), full(tri)],
        out_specs=[rows(w) if h is None else head_rows(h, w) for h, w, _ in outs],
        out_shape=[jax.ShapeDtypeStruct((B, T, w) if h is None else (B, h, T, w), dt) for h, w, dt in outs],
        scratch_shapes=[pltpu.VMEM((tm + 8, CONV_DIM), F32)],
        compiler_params=pltpu.CompilerParams(dimension_semantics=("arbitrary", "arbitrary"),
                                             vmem_limit_bytes=VMEM_LIMIT),
        name="inproj",
    )(x, sc1, sh1, g1, wc, wz, wq, wkv, wqi, wm, conv_w, ukt, kvg, lng, lnb, alog, dtb, tri)


def _deltanet_kernel(q_ref, k_ref, v_ref, z_ref, misc_ref, ng_ref, o_ref, s_ref):
    R = q_ref.shape[1]
    n_chunks = R // CHUNK

    @pl.when(pl.program_id(1) == 0)
    def _():
        s_ref[...] = jnp.zeros(s_ref.shape, F32)

    misc = misc_ref[0]
    misc_t = misc.T
    SB = min(DN_SUB, R)
    row = lax.broadcasted_iota(I32, (SB, SB), 0)
    col = lax.broadcasted_iota(I32, (SB, SB), 1)
    same = (row // CHUNK) == (col // CHUNK)
    lower = same & (row >= col)
    strict = same & (row > col)
    eye = (row == col).astype(F32)

    def mm(a, b):
        return _dot(a.astype(BF16), b.astype(BF16))

    def mm3(a, b):
        ah = a.astype(BF16)
        bh = b.astype(BF16)
        al = (a - ah.astype(F32)).astype(BF16)
        bl = (b - bh.astype(F32)).astype(BF16)
        return _dot(jnp.concatenate([ah, ah, al], axis=1), jnp.concatenate([bh, bl, bh], axis=0))

    heads = range(DN_HEADS)
    subs = range(R // SB)
    chains = [(hd, sb) for hd in heads for sb in subs]
    cols = [slice(hd * DN_DK, (hd + 1) * DN_DK) for hd in heads]
    qh = [q_ref[0, :, cols[hd]] for hd in heads]
    kh = [k_ref[0, :, cols[hd]] for hd in heads]
    beta = [misc[:, M_BETA + hd:M_BETA + hd + 1] for hd in heads]
    gc_c = [misc[:, M_A + hd:M_A + hd + 1] for hd in heads]
    eg = [jnp.exp(gc_c[hd]) for hd in heads]
    kb = [kh[hd] * beta[hd] for hd in heads]
    rhs = [jnp.concatenate([v_ref[0, :, cols[hd]] * beta[hd], kb[hd] * eg[hd]], axis=-1) for hd in heads]
    q_dec = [qh[hd] * eg[hd] for hd in heads]

    def rows_of(sb):
        return slice(sb * SB, (sb + 1) * SB)

    decay, a, qk_sb = {}, {}, {}
    for hd, sb in chains:
        bs = rows_of(sb)
        gc_r = misc_t[M_A + hd:M_A + hd + 1, bs]
        decay[hd, sb] = jnp.where(lower, jnp.exp(jnp.where(lower, gc_c[hd][bs] - gc_r, 0.0)), 0.0)
    for hd, sb in chains:
        bs = rows_of(sb)
        khb = kh[hd][bs].astype(BF16)
        a[hd, sb] = jnp.where(strict, _nt_dot(kb[hd][bs].astype(BF16), khb) * decay[hd, sb], 0.0)
        qk_sb[hd, sb] = jnp.where(lower, _nt_dot(qh[hd][bs].astype(BF16), khb) * decay[hd, sb], 0.0)
    p = {ch: eye - a[ch] for ch in chains}
    xp = {ch: mm3(a[ch], a[ch]) for ch in chains}
    n_sq = 1
    while True:
        p = {ch: p[ch] + mm3(p[ch], xp[ch]) for ch in chains}
        n_sq *= 2
        if n_sq * 2 >= CHUNK:
            break
        xp = {ch: mm3(xp[ch], xp[ch]) for ch in chains}
    sol = {(hd, sb): mm(p[hd, sb], rhs[hd][rows_of(sb)]) for hd, sb in chains}

    def chunk_of(c):
        per = SB // CHUNK
        return c // per, slice((c % per) * CHUNK, (c % per + 1) * CHUNK)

    s = [s_ref[hd] for hd in heads]
    o_parts = [[] for _ in heads]
    for c in range(n_chunks):
        rs = slice(c * CHUNK, (c + 1) * CHUNK)
        sb, r = chunk_of(c)
        gl = [gc_c[hd][(c + 1) * CHUNK - 1:(c + 1) * CHUNK, :] for hd in heads]
        k_dec = [kh[hd][rs] * jnp.exp(gl[hd] - gc_c[hd][rs]) for hd in heads]
        v_new = [sol[hd, sb][r, :DN_DV] - mm(sol[hd, sb][r, DN_DV:], s[hd]) for hd in heads]
        for hd in heads:
            o_parts[hd].append(mm(q_dec[hd][rs], s[hd]) + mm(qk_sb[hd, sb][r, r], v_new[hd]))
        s = [s[hd] * jnp.exp(gl[hd]) + mm(k_dec[hd].T, v_new[hd]) for hd in heads]
    for hd in heads:
        s_ref[hd] = s[hd]
        o = jnp.concatenate(o_parts[hd], axis=0)
        o = o * lax.rsqrt(jnp.mean(o * o, axis=-1, keepdims=True) + EPS) * ng_ref[...]
        o_ref[0, :, cols[hd]] = (o * _silu(z_ref[0, :, cols[hd]])).astype(BF16)


def _deltanet(q, k, v, z, misc, ng, R):
    B, T, _ = q.shape

    def rows(w):
        return pl.BlockSpec((1, R, w), lambda b, i: (b, i, 0))

    return pl.pallas_call(
        _deltanet_kernel,
        grid=(B, T // R),
        in_specs=[rows(DN_QK), rows(DN_QK), rows(DN_V), rows(DN_V), rows(MISC_W),
                  pl.BlockSpec((1, DN_DV), lambda b, i: (0, 0))],
        out_specs=rows(DN_V),
        out_shape=jax.ShapeDtypeStruct((B, T, DN_V), BF16),
        scratch_shapes=[pltpu.VMEM((DN_HEADS, DN_DK, DN_DV), F32)],
        compiler_params=pltpu.CompilerParams(dimension_semantics=("arbitrary", "arbitrary"),
                                             vmem_limit_bytes=VMEM_LIMIT),
        name="deltanet",
    )(q, k, v, z, misc, ng)


def _bit_transpose32(words):
    w = list(words)
    j = 16
    m = 0x0000FFFF
    while j:
        k = 0
        m_i32 = jnp.int32(m - (1 << 32) if m >= (1 << 31) else m)
        while k < 32:
            t = (w[k] ^ lax.shift_right_logical(w[k + j], jnp.full_like(w[k], j))) & m_i32
            w[k] = w[k] ^ t
            w[k + j] = w[k + j] ^ jnp.left_shift(t, j)
            k = (k + j + 1) & ~j
        j >>= 1
        m = (m ^ (m << j)) & 0xFFFFFFFF
    return w


def _dsa_kernel(qix_ref, misc_ref, kix_ref, qlat_ref, ckv_ref, o_ref, keys_ref, planes_ref, bias_ref,
                mx_ref, l_ref, acc_ref, kvmax_ref, *, topk, pos_bits, n_cg_max):
    i = pl.program_id(1)
    QB = QBLOCK
    KT = KEY_TILE
    n_kt = (i * QB + QB + KT - 1) // KT

    rowi = lax.broadcasted_iota(I32, (QB, KT), 0)
    coli = lax.broadcasted_iota(I32, (QB, KT), 1)
    limit = i * QB + (rowi // CHUNK + 1) * CHUNK

    misc = misc_ref[0]
    qix = qix_ref[0].reshape(IDX_HEADS * QB, IDX_DIM)
    q_st = qlat_ref[0].reshape(SA_HEADS * QB, KV_RANK)

    def score_body(masked, kt, carry):
        k0 = pl.multiple_of(kt * KT, KT)
        kx = kix_ref[0, pl.ds(k0, KT), :]
        rel = jnp.maximum(_nt_dot(qix, kx), 0.0)
        sc = jnp.zeros((QB, KT), F32)
        for hd in range(IDX_HEADS):
            sc = sc + misc[:, M_WIX + hd:M_WIX + hd + 1] * rel[hd * QB:(hd + 1) * QB]
        sc = jnp.where(sc == 0.0, 0.0, sc)
        bits = pltpu.bitcast(sc, I32)
        key = jnp.where(bits < 0, bits ^ 0x7FFFFFFF, bits)
        keys_ref[:, pl.ds(k0, KT)] = jnp.where(k0 + coli < limit, key, INT_MIN) if masked else key
        return carry

    n_open = (i * QB + CHUNK) // KT
    lax.fori_loop(0, n_open, functools.partial(score_body, False), 0)
    lax.fori_loop(n_open, n_kt, functools.partial(score_body, True), 0)

    n_cg = (n_kt * KT + PLANE_COLS - 1) // PLANE_COLS

    def fill_body(kt, carry):
        keys_ref[:, pl.ds(pl.multiple_of(kt * KT, KT), KT)] = jnp.full((QB, KT), INT_MIN, I32)
        return carry

    lax.fori_loop(n_kt, n_cg * (PLANE_COLS // KT), fill_body, 0)

    @pl.when(i == 0)
    def _():
        planes_ref[...] = jnp.zeros(planes_ref.shape, I32)

    def plane_body(step, carry):
        c = step // (QB // PLANE_ROWS)
        r0 = pl.multiple_of((step % (QB // PLANE_ROWS)) * PLANE_ROWS, PLANE_ROWS)
        words = [keys_ref[pl.ds(r0, PLANE_ROWS), pl.ds(pl.multiple_of(c * PLANE_COLS + j * 128, 128), 128)]
                 for j in range(32)]
        for b, plane in enumerate(_bit_transpose32(words)):
            planes_ref[c, b, pl.ds(r0, PLANE_ROWS), :] = ~plane if b == 0 else plane
        return carry

    lax.fori_loop(0, n_cg * (QB // PLANE_ROWS), plane_body, 0)

    ones_mat = jnp.ones((128, 128), BF16)

    def lane_count(words):
        pc = functools.reduce(jnp.add, [lax.population_count(x) for x in words])
        return _dot(pc.astype(F32).astype(BF16), ones_mat).astype(I32)

    def sel_body(groups, step, carry):
        cand, n_gt, tau_u = carry
        hi = [cand[c] & planes_ref[c, 2 * step] for c in groups]
        lo = [cand[c] ^ hi[c] for c in groups]
        d3 = [hi[c] & planes_ref[c, 2 * step + 1] for c in groups]
        d2 = [hi[c] ^ d3[c] for c in groups]
        d1 = [lo[c] & planes_ref[c, 2 * step + 1] for c in groups]
        d0 = [lo[c] ^ d1[c] for c in groups]
        a3 = n_gt + lane_count(d3)
        a2 = a3 + lane_count(d2)
        a1 = a2 + lane_count(d1)
        is3 = a3 >= topk
        is2 = a2 >= topk
        is1 = a1 >= topk
        cand = tuple(jnp.where(is3, d3[c], jnp.where(is2, d2[c], jnp.where(is1, d1[c], d0[c]))) for c in groups)
        n_gt = jnp.where(is3, n_gt, jnp.where(is2, a3, jnp.where(is1, a2, a1)))
        digit = jnp.where(is3, 3, jnp.where(is2, 2, jnp.where(is1, 1, 0)))
        return cand, n_gt, tau_u | jnp.left_shift(digit, 30 - 2 * step)

    def radix_select(n_groups):
        groups = range(n_groups)
        start = (tuple(jnp.full((QB, 128), -1, I32) for _ in groups), jnp.zeros((QB, 128), I32),
                 jnp.zeros((QB, 128), I32))
        cand_, n_gt_, tau_ = lax.fori_loop(0, 16, functools.partial(sel_body, groups), start)
        rest = tuple(jnp.zeros((QB, 128), I32) for _ in range(n_cg_max - n_groups))
        return cand_ + rest, n_gt_, tau_

    cand, n_gt, tau_u = lax.switch(n_cg - 1, [functools.partial(radix_select, g + 1) for g in range(n_cg_max)])
    tau = tau_u ^ INT_MIN
    sentinel = tau == INT_MIN
    cand = tuple(jnp.where(sentinel, 0, cand[c]) for c in range(n_cg_max))
    need = topk - n_gt
    any_tie = jnp.max(jnp.where(lane_count(cand) > need, 1, 0)) > 0

    def fast_bias():
        floor = jnp.where(sentinel, INT_MIN + 1, tau)

        def body(kt, carry):
            for j in range(KT // 128):
                cols = pl.ds(pl.multiple_of(kt * KT + j * 128, 128), 128)
                bias_ref[:, cols] = jnp.where(keys_ref[:, cols] >= floor, 0.0, NEG_BIG)
            return carry

        lax.fori_loop(0, n_kt, body, 0)

    def tie_bias():
        lane = lax.broadcasted_iota(I32, (QB, 128), 1)

        def pos_mask(p, c):
            cg = lax.shift_right_logical(p, jnp.full_like(p, PLANE_SHIFT))
            j0 = lax.shift_right_logical(p, jnp.full_like(p, 7)) & 31
            below = ~lax.shift_right_logical(jnp.full_like(p, -1), j0)
            bit = lax.shift_right_logical(jnp.full_like(p, INT_MIN), j0)
            word = below | jnp.where(lane < (p & 127), bit, 0)
            return jnp.where(cg > c, -1, jnp.where(cg == c, word, 0))

        def pos_body(b, q):
            cq = q + jnp.left_shift(jnp.int32(1), pos_bits - 1 - b)
            cnt = lane_count([cand[c] & pos_mask(cq, c) for c in range(n_cg_max)])
            return jnp.where(cnt < need, cq, q)

        pstar = lax.fori_loop(0, pos_bits, pos_body, jnp.zeros((QB, 128), I32)) + 1
        pstar = jnp.where(sentinel, 0, pstar)

        def body(kt, carry):
            for j in range(KT // 128):
                c0 = pl.multiple_of(kt * KT + j * 128, 128)
                kk = keys_ref[:, pl.ds(c0, 128)]
                tie = jnp.where(c0 + lane < pstar, 0.0, NEG_BIG)
                bias_ref[:, pl.ds(c0, 128)] = jnp.where(kk > tau, 0.0, jnp.where(kk == tau, tie, NEG_BIG))
            return carry

        lax.fori_loop(0, n_kt, body, 0)

    lax.cond(any_tie, tie_bias, fast_bias)

    def logit_chunks(k0):
        kv = ckv_ref[0, pl.ds(k0, KT), :]
        s = _nt_dot(q_st, kv)
        bias = bias_ref[:, pl.ds(k0, KT)]
        chunks = []
        for j in range(KT // 128):
            bj = bias[:, j * 128:(j + 1) * 128]
            chunks.append(s[:, j * 128:(j + 1) * 128] + jnp.concatenate([bj] * SA_HEADS, axis=0))
        return kv, chunks

    def weighted_sum_sweep():
        l_ref[...] = jnp.zeros(l_ref.shape, F32)
        acc_ref[...] = jnp.zeros(acc_ref.shape, F32)

        def pv_body(kt, carry):
            kv, chunks = logit_chunks(pl.multiple_of(kt * KT, KT))
            shift = mx_ref[...]
            ps = [jnp.exp2(c - shift) for c in chunks]
            l_ref[...] = functools.reduce(jnp.add, ps, l_ref[...])
            p = jnp.concatenate([pj.astype(BF16) for pj in ps], axis=1)
            acc_ref[...] += _dot(p, kv)
            return carry

        lax.fori_loop(0, n_kt, pv_body, 0)
        l_row = jnp.sum(l_ref[...], axis=-1, keepdims=True)
        o_ref[0] = (acc_ref[...] / l_row).astype(BF16).reshape(SA_HEADS, QB, KV_RANK)
        return l_row

    @pl.when(i == 0)
    def _():
        def norm_body(r, best):
            x = ckv_ref[0, pl.ds(pl.multiple_of(r * KT, KT), KT), :].astype(F32)
            return jnp.maximum(best, jnp.max(jnp.sum(x * x, axis=1, keepdims=True), axis=0, keepdims=True))

        n_rows = ckv_ref.shape[1]
        kv_sq = lax.fori_loop(0, n_rows // KT, norm_body, jnp.zeros((1, 1), F32))
        kvmax_ref[...] = jnp.broadcast_to(jnp.sqrt(kv_sq), kvmax_ref.shape)

    qf = q_st.astype(F32)
    q_norm = jnp.sqrt(jnp.sum(qf * qf, axis=1, keepdims=True))
    mx_ref[...] = jnp.broadcast_to(q_norm, mx_ref.shape) * kvmax_ref[0:1, :] * 1.001 + 1e-3
    l_fast = weighted_sum_sweep()

    @pl.when(jnp.min(l_fast) < SOFTMAX_TINY)
    def _():
        mx_ref[...] = jnp.full(mx_ref.shape, NEG_BIG, F32)

        def max_body(kt, carry):
            _, chunks = logit_chunks(pl.multiple_of(kt * KT, KT))
            mx_ref[...] = functools.reduce(jnp.maximum, chunks, mx_ref[...])
            return carry

        lax.fori_loop(0, n_kt, max_body, 0)
        mx_ref[...] = jnp.broadcast_to(jnp.max(mx_ref[...], axis=-1, keepdims=True), mx_ref.shape)
        weighted_sum_sweep()


def _dsa(qix, misc, kix, qlat, ckv, topk):
    B, T, _ = kix.shape
    n_cg_max = -(-T // PLANE_COLS)
    t_pad = n_cg_max * PLANE_COLS
    pos_bits = (t_pad - 1).bit_length()

    def rows(w):
        return pl.BlockSpec((1, QBLOCK, w), lambda b, i: (b, i, 0))

    def head_rows(h, w):
        return pl.BlockSpec((1, h, QBLOCK, w), lambda b, i: (b, 0, i, 0))

    def per_b(w):
        return pl.BlockSpec((1, T, w), lambda b, i: (b, 0, 0))

    return pl.pallas_call(
        functools.partial(_dsa_kernel, topk=topk, pos_bits=pos_bits, n_cg_max=n_cg_max),
        grid=(B, T // QBLOCK),
        in_specs=[head_rows(IDX_HEADS, IDX_DIM), rows(MISC_W), per_b(IDX_DIM), head_rows(SA_HEADS, KV_RANK),
                  per_b(KV_RANK)],
        out_specs=head_rows(SA_HEADS, KV_RANK),
        out_shape=jax.ShapeDtypeStruct((B, SA_HEADS, T, KV_RANK), BF16),
        scratch_shapes=[pltpu.VMEM((QBLOCK, t_pad), I32),
                        pltpu.VMEM((n_cg_max, 32, QBLOCK, 128), I32),
                        pltpu.VMEM((QBLOCK, t_pad), F32),
                        pltpu.VMEM((SA_HEADS * QBLOCK, 128), F32),
                        pltpu.VMEM((SA_HEADS * QBLOCK, 128), F32),
                        pltpu.VMEM((SA_HEADS * QBLOCK, KV_RANK), F32),
                        pltpu.VMEM((8, 128), F32)],
        compiler_params=pltpu.CompilerParams(dimension_semantics=("arbitrary", "arbitrary"),
                                             vmem_limit_bytes=VMEM_LIMIT),
        name="dsa",
    )(qix, misc, kix, qlat, ckv)


def _first_max(v, idx, axis):
    m = jnp.max(v, axis=axis, keepdims=True)
    big = jnp.int32(2 ** 30)
    first = jnp.min(jnp.where(v == m, idx, big), axis=axis, keepdims=True)
    return m, idx == first


def _outproj_kernel(x_ref, odn_ref, olat_ref, uv_ref, wo_ref, gt_ref, sc_ref, sh_ref, g2_ref, rwt_ref,
                    rb_ref, lstrict_ref, ustrict_ref, x1_ref, h2_ref, posrow_ref, poscol_ref, gatecol_ref,
                    cpad_ref):
    tm = x_ref.shape[1]
    parts = [odn_ref[0]]
    for hd in range(SA_HEADS):
        parts.append(_dot(olat_ref[0, hd], uv_ref[hd]).astype(BF16))
    mix = jnp.concatenate(parts, axis=-1)
    x1 = x_ref[0] + gt_ref[0] * _dot(mix, wo_ref[...])
    x1_ref[0] = x1
    h2 = x1 * lax.rsqrt(jnp.mean(x1 * x1, axis=-1, keepdims=True) + EPS) * g2_ref[...]
    h2 = h2 * (1.0 + sc_ref[0]) + sh_ref[0]
    h2_ref[0] = h2.astype(BF16)

    per_g = N_EXPERTS // N_GROUPS
    s = jax.nn.sigmoid(_nt_dot(rwt_ref[...], h2, HIGHEST))
    choice = s + rb_ref[...]
    ig = lax.broadcasted_iota(I32, (per_g, tm), 0)
    gscore = []
    for gidx in range(N_GROUPS):
        cg = choice[gidx * per_g:(gidx + 1) * per_g]
        m1, hot1 = _first_max(cg, ig, 0)
        gscore.append(m1 + jnp.max(jnp.where(hot1, -jnp.inf, cg), axis=0, keepdims=True))
    gsel = [jnp.zeros((1, tm), jnp.bool_) for _ in range(N_GROUPS)]
    for _ in range(TOPK_GROUPS):
        best = functools.reduce(jnp.maximum, gscore)
        found = jnp.zeros((1, tm), jnp.bool_)
        for gidx in range(N_GROUPS):
            hot = (gscore[gidx] == best) & jnp.logical_not(found)
            found = found | hot
            gsel[gidx] = gsel[gidx] | hot
            gscore[gidx] = jnp.where(hot, -jnp.inf, gscore[gidx])
    masked = jnp.concatenate(
        [jnp.where(gsel[gidx], choice[gidx * per_g:(gidx + 1) * per_g], -jnp.inf) for gidx in range(N_GROUPS)],
        axis=0)
    ei = lax.broadcasted_iota(I32, masked.shape, 0)
    gate = jnp.zeros(masked.shape, F32)
    hots = []
    for _ in range(TOP_K):
        _, hot = _first_max(masked, ei, 0)
        hots.append(hot)
        gate = jnp.where(hot, s, gate)
        masked = jnp.where(hot, -jnp.inf, masked)
    gate = gate / jnp.sum(gate, axis=0, keepdims=True) * ROUTED_SCALE

    picked = jnp.where(functools.reduce(jnp.logical_or, hots), 1.0, 0.0)
    cnt = jnp.sum(picked, axis=1, keepdims=True)
    cpad = jnp.floor((cnt + (RUN_ALIGN - 1)) * (1.0 / RUN_ALIGN)) * RUN_ALIGN
    cpad_b = jnp.broadcast_to(cpad, (N_EXPERTS, GATE_W))
    lbase = _dot(lstrict_ref[...], cpad_b, HIGHEST)[:, :1]
    rank = _dot(picked.astype(BF16), ustrict_ref[...])
    pos = lbase + rank
    ri = lax.broadcasted_iota(I32, (GATE_W, tm), 0)
    pos_rows = jnp.zeros((GATE_W, tm), F32)
    gate_rows = jnp.zeros((GATE_W, tm), F32)
    for k, hot in enumerate(hots):
        pos_rows = jnp.where(ri == k, jnp.sum(jnp.where(hot, pos, 0.0), axis=0, keepdims=True), pos_rows)
        gate_rows = jnp.where(ri == k, jnp.sum(jnp.where(hot, gate, 0.0), axis=0, keepdims=True), gate_rows)
    posrow_ref[0, 0] = pos_rows[:TOP_K].astype(I32)
    poscol_ref[0] = pos_rows.T.astype(I32)
    gatecol_ref[0] = gate_rows.T
    cpad_ref[0, 0] = cpad_b.astype(I32)


def _outproj(x, odn, olat, uv, wo, gt1, sc2, sh2, g2, rwt, rb, tm):
    B, T, D = x.shape
    nt = T // tm
    ex = jnp.arange(N_EXPERTS)
    lstrict = (ex[:, None] > ex[None, :]).astype(F32)
    tok = jnp.arange(tm)
    ustrict = (tok[:, None] < tok[None, :]).astype(BF16)

    def full(a):
        nd = a.ndim
        return pl.BlockSpec(a.shape, lambda b, i, _n=nd: (0,) * _n)

    def rows(w):
        return pl.BlockSpec((1, tm, w), lambda b, i: (b, i, 0))

    def per_tile(h, w):
        return pl.BlockSpec((1, 1, h, w), lambda b, i: (b, i, 0, 0))

    per_b = pl.BlockSpec((1, 1, D), lambda b, i: (b, 0, 0))
    return pl.pallas_call(
        _outproj_kernel,
        grid=(B, nt),
        in_specs=[rows(D), rows(DN_V),
                  pl.BlockSpec((1, SA_HEADS, tm, KV_RANK), lambda b, i: (b, 0, i, 0)),
                  full(uv), full(wo), per_b, per_b, per_b,
                  full(g2), full(rwt), full(rb), full(lstrict), full(ustrict)],
        out_specs=[rows(D), rows(D), per_tile(TOP_K, tm), rows(GATE_W), rows(GATE_W),
                   per_tile(N_EXPERTS, GATE_W)],
        out_shape=[jax.ShapeDtypeStruct((B, T, D), F32), jax.ShapeDtypeStruct((B, T, D), BF16),
                   jax.ShapeDtypeStruct((B, nt, TOP_K, tm), I32),
                   jax.ShapeDtypeStruct((B, T, GATE_W), I32),
                   jax.ShapeDtypeStruct((B, T, GATE_W), F32),
                   jax.ShapeDtypeStruct((B, nt, N_EXPERTS, GATE_W), I32)],
        compiler_params=pltpu.CompilerParams(dimension_semantics=("arbitrary", "arbitrary"),
                                             vmem_limit_bytes=VMEM_LIMIT),
        name="outproj",
    )(x, odn, olat, uv, wo, gt1, sc2, sh2, g2, rwt, rb, lstrict, ustrict)


def _piece_sizes(max_rows):
    sizes = []
    z = RUN_ALIGN
    while z <= max_rows:
        sizes.append(z)
        z *= 2
    return sizes[::-1]


def _for_run_pieces(length, max_rows, fn):
    for z in _piece_sizes(max_rows):
        start = length & ~(2 * z - 1)

        @pl.when((length & z) != 0)
        def _(start=start, z=z):
            fn(start, z)


def _plan_kernel(cp_ref, off_ref, lb_ref, foff_ref, flen_ref, blk_ref, nused_ref):
    cp = cp_ref[...].astype(F32)
    n, ne = cp.shape
    ei = lax.broadcasted_iota(I32, (ne, ne), 0)
    ej = lax.broadcasted_iota(I32, (ne, ne), 1)
    si = lax.broadcasted_iota(I32, (n, n), 0)
    sj = lax.broadcasted_iota(I32, (n, n), 1)
    lb = _dot(cp, (ei < ej).astype(F32), HIGHEST)
    earlier_tiles = _dot((sj < si).astype(F32), cp, HIGHEST)
    rows_e = jnp.sum(cp, axis=0, keepdims=True)
    region = jnp.floor((rows_e + (ROW_BLOCK - 1)) * (1.0 / ROW_BLOCK)) * ROW_BLOCK
    region_b = jnp.broadcast_to(region, (ne, ne))
    rend_row = _dot(region_b, (ei <= ej).astype(F32), HIGHEST)[:1]
    rend_col = jnp.sum(jnp.where(ej <= ei, region_b, 0.0), axis=1, keepdims=True)
    base = rend_row - region
    total = jnp.max(rend_row, axis=1, keepdims=True)
    off_ref[...] = (base + earlier_tiles).astype(I32)
    lb_ref[...] = lb.astype(I32)
    lane = lax.broadcasted_iota(I32, (1, GATE_W), 1)
    pad = jnp.zeros((1, GATE_W - ne), F32)
    foff_ref[...] = jnp.where(lane == ne, total, jnp.concatenate([base + rows_e, pad], axis=1)).astype(I32)
    flen_ref[...] = jnp.concatenate([region - rows_e, pad], axis=1).astype(I32)
    n_used = total * (1.0 / ROW_BLOCK)
    nused_ref[...] = jnp.broadcast_to(n_used, nused_ref.shape).astype(I32)
    bi = lax.broadcasted_iota(I32, (ne, blk_ref.shape[1]), 1).astype(F32)
    ended = jnp.where(rend_col * (1.0 / ROW_BLOCK) <= jnp.minimum(bi, n_used - 1.0), 1.0, 0.0)
    blk_ref[...] = jnp.minimum(jnp.sum(ended, axis=0, keepdims=True), ne - 1.0).astype(I32)


def _plan(cp, n_blocks):
    n, ne = cp.shape
    nb_pad = -(-n_blocks // 128) * 128
    return pl.pallas_call(
        _plan_kernel,
        out_shape=[jax.ShapeDtypeStruct((n, ne), I32), jax.ShapeDtypeStruct((n, ne), I32),
                   jax.ShapeDtypeStruct((1, GATE_W), I32), jax.ShapeDtypeStruct((1, GATE_W), I32),
                   jax.ShapeDtypeStruct((1, nb_pad), I32), jax.ShapeDtypeStruct((1, GATE_W), I32)],
        name="moe_plan",
    )(cp)


def _dispatch_kernel(off_ref, cp_ref, lb_ref, foff_ref, flen_ref, h_ref, posrow_ref, xs_hbm, buf, zbuf, sem,
                     zsem, *, n_steps, tile):
    s = pl.program_id(0)
    slot = s % 2
    jmax = buf.shape[1]

    def run_copies(step, slot_, act):
        def body(e, carry):
            idx = step * N_EXPERTS + e
            lb = lb_ref[idx]
            of = off_ref[idx]

            def piece(start, z):
                act(pltpu.make_async_copy(
                    buf.at[slot_, pl.ds(pl.multiple_of(lb + start, RUN_ALIGN), z)],
                    xs_hbm.at[pl.ds(pl.multiple_of(of + start, RUN_ALIGN), z)], sem.at[slot_]))

            _for_run_pieces(cp_ref[idx], tile, piece)
            return carry

        lax.fori_loop(0, N_EXPERTS, body, 0)

    def wait_runs(step, slot_):
        last_ = step * N_EXPERTS + N_EXPERTS - 1
        _for_run_pieces(lb_ref[last_] + cp_ref[last_], jmax, lambda start, z: pltpu.make_async_copy(
            buf.at[slot_, pl.ds(0, z)], xs_hbm.at[pl.ds(0, z)], sem.at[slot_]).wait())

    def fill_copies(act):
        def body(e, carry):
            fo = foff_ref[e]

            def piece(start, z):
                act(pltpu.make_async_copy(
                    zbuf.at[pl.ds(0, z)], xs_hbm.at[pl.ds(pl.multiple_of(fo + start, RUN_ALIGN), z)], zsem.at[0]))

            _for_run_pieces(flen_ref[e], ROW_BLOCK // 2, piece)
            return carry

        lax.fori_loop(0, N_EXPERTS, body, 0)

        def tail(r, carry):
            act(pltpu.make_async_copy(
                zbuf, xs_hbm.at[pl.ds(pl.multiple_of(foff_ref[N_EXPERTS] + r * zbuf.shape[0], RUN_ALIGN),
                                      zbuf.shape[0])], zsem.at[0]))
            return carry

        lax.fori_loop(0, (xs_hbm.shape[0] - foff_ref[N_EXPERTS]) // zbuf.shape[0], tail, 0)

    @pl.when(s == 0)
    def _():
        zbuf[...] = jnp.zeros(zbuf.shape, BF16)
        fill_copies(lambda c: c.start())

    @pl.when(s >= 2)
    def _():
        wait_runs(s - 2, slot)

    h = h_ref[...]
    last = s * N_EXPERTS + N_EXPERTS - 1
    jused = lb_ref[last] + cp_ref[last]
    def local_rows(jc):
        ji = (lax.broadcasted_iota(I32, (MOE_CHUNK, tile), 0) + jc * MOE_CHUNK).astype(jnp.int16)
        p = jnp.zeros((MOE_CHUNK, tile), BF16)
        for k in range(TOP_K):
            p = jnp.where(ji == posrow_ref[0, k:k + 1, :].astype(jnp.int16), jnp.ones((), BF16), p)
        return _dot(p, h).astype(BF16)

    n_full = TOP_K * tile // MOE_CHUNK
    rows_full = [local_rows(jc) for jc in range(n_full)]
    for jc in range(n_full):
        buf[slot, jc * MOE_CHUNK:(jc + 1) * MOE_CHUNK, :] = rows_full[jc]
    for jc in range(n_full, jmax // MOE_CHUNK):
        @pl.when(jc * MOE_CHUNK < jused)
        def _(jc=jc):
            buf[slot, jc * MOE_CHUNK:(jc + 1) * MOE_CHUNK, :] = local_rows(jc)

    run_copies(s, slot, lambda c: c.start())

    @pl.when(s == n_steps - 1)
    def _():
        if n_steps >= 2:
            wait_runs(s - 1, 1 - slot)
        wait_runs(s, slot)
        fill_copies(lambda c: c.wait())


def _dispatch(h2, posrow, off, cp, lb, foff, flen, cap, tile, jmax):
    n_tok, D = h2.shape
    n_steps = n_tok // tile
    return pl.pallas_call(
        functools.partial(_dispatch_kernel, n_steps=n_steps, tile=tile),
        grid_spec=pltpu.PrefetchScalarGridSpec(
            num_scalar_prefetch=5,
            grid=(n_steps,),
            in_specs=[pl.BlockSpec((tile, D), lambda s, *_: (s, 0)),
                      pl.BlockSpec((1, TOP_K, tile), lambda s, *_: (s, 0, 0))],
            out_specs=pl.BlockSpec(memory_space=pl.ANY),
            scratch_shapes=[pltpu.VMEM((2, jmax, D), BF16), pltpu.VMEM((ROW_BLOCK // 2, D), BF16),
                            pltpu.SemaphoreType.DMA((2,)), pltpu.SemaphoreType.DMA((1,))]),
        out_shape=jax.ShapeDtypeStruct((cap, D), BF16),
        compiler_params=pltpu.CompilerParams(dimension_semantics=("arbitrary",), vmem_limit_bytes=VMEM_LIMIT),
        name="moe_dispatch",
    )(off, cp, lb, foff, flen, h2, posrow)


def _expert_kernel(blk_e_ref, nused_ref, xs_ref, wg_ref, wu_ref, wd_ref, ys_ref):
    used = pl.program_id(0) < nused_ref[0]

    @pl.when(used)
    def _():
        xb = xs_ref[...]
        a = _silu(_dot(xb, wg_ref[0])) * _dot(xb, wu_ref[0])
        ys_ref[...] = _dot(a.astype(BF16), wd_ref[0]).astype(BF16)

    @pl.when(jnp.logical_not(used))
    def _():
        ys_ref[...] = jnp.zeros(ys_ref.shape, BF16)


def _experts(xs, blk_e, n_used, wg, wu, wd):
    cap, D = xs.shape

    def row_block(i, be, nu):
        return (jnp.minimum(i, nu[0] - 1), 0)

    def out_block(i, be, nu):
        return (i, 0)

    def weight(i, be, nu):
        return (be[i], 0, 0)

    return pl.pallas_call(
        _expert_kernel,
        grid_spec=pltpu.PrefetchScalarGridSpec(
            num_scalar_prefetch=2,
            grid=(cap // ROW_BLOCK,),
            in_specs=[pl.BlockSpec((ROW_BLOCK, D), row_block),
                      pl.BlockSpec((1, D, D_EXPERT), weight), pl.BlockSpec((1, D, D_EXPERT), weight),
                      pl.BlockSpec((1, D_EXPERT, D), weight)],
            out_specs=pl.BlockSpec((ROW_BLOCK, D), out_block)),
        out_shape=jax.ShapeDtypeStruct((cap, D), BF16),
        compiler_params=pltpu.CompilerParams(dimension_semantics=("arbitrary",), vmem_limit_bytes=VMEM_LIMIT),
        name="moe_experts",
    )(blk_e, n_used, xs, wg, wu, wd)


def _combine_kernel(off_ref, cp_ref, lb_ref, ys_hbm, poscol_ref, gatecol_ref, h_ref, sg_ref, su_ref, sd_ref,
                    x1_ref, gt_ref, fg_ref, o_ref, buf, sem, acc_ref, *, n_steps, tile, final_norm):
    s = pl.program_id(0)
    slot = s % 2
    jmax = buf.shape[1]

    def run_copies(step, slot_, act):
        def body(e, carry):
            idx = step * N_EXPERTS + e
            lb = lb_ref[idx]
            of = off_ref[idx]

            def piece(start, z):
                act(pltpu.make_async_copy(
                    ys_hbm.at[pl.ds(pl.multiple_of(of + start, RUN_ALIGN), z)],
                    buf.at[slot_, pl.ds(pl.multiple_of(lb + start, RUN_ALIGN), z)], sem.at[slot_]))

            _for_run_pieces(cp_ref[idx], tile, piece)
            return carry

        lax.fori_loop(0, N_EXPERTS, body, 0)

    @pl.when(s == 0)
    def _():
        run_copies(0, 0, lambda c: c.start())

    @pl.when(s + 1 < n_steps)
    def _():
        run_copies(s + 1, 1 - slot, lambda c: c.start())

    hb = h_ref[...]
    shared = (_silu(_dot(hb, sg_ref[...])) * _dot(hb, su_ref[...])).astype(BF16)
    acc_ref[...] = _dot(shared, sd_ref[...])

    last = s * N_EXPERTS + N_EXPERTS - 1
    jused = lb_ref[last] + cp_ref[last]
    _for_run_pieces(jused, jmax, lambda start, z: pltpu.make_async_copy(
        ys_hbm.at[pl.ds(0, z)], buf.at[slot, pl.ds(0, z)], sem.at[slot]).wait())

    def zero_body(r, carry):
        buf[slot, pl.ds(pl.multiple_of(jused + r * RUN_ALIGN, RUN_ALIGN), RUN_ALIGN), :] = jnp.zeros(
            (RUN_ALIGN, buf.shape[2]), BF16)
        return carry

    chunk_end = (jused + MOE_CHUNK - 1) // MOE_CHUNK * MOE_CHUNK
    lax.fori_loop(0, (chunk_end - jused) // RUN_ALIGN, zero_body, 0)

    def gate_rows(jc):
        ji = (lax.broadcasted_iota(I32, (tile, MOE_CHUNK), 1) + jc * MOE_CHUNK).astype(jnp.int16)
        g = jnp.zeros((tile, MOE_CHUNK), BF16)
        for k in range(TOP_K):
            g = jnp.where(ji == poscol_ref[:, k:k + 1].astype(jnp.int16), gatecol_ref[:, k:k + 1].astype(BF16), g)
        return g

    n_full = TOP_K * tile // MOE_CHUNK
    g_full = jnp.concatenate([gate_rows(jc) for jc in range(n_full)], axis=1)
    acc_ref[...] += _dot(g_full, buf[slot, 0:n_full * MOE_CHUNK, :])
    for jc in range(n_full, jmax // MOE_CHUNK):
        @pl.when(jc * MOE_CHUNK < jused)
        def _(jc=jc):
            acc_ref[...] += _dot(gate_rows(jc), buf[slot, jc * MOE_CHUNK:(jc + 1) * MOE_CHUNK, :])

    y = x1_ref[...] + gt_ref[0] * acc_ref[...]
    if final_norm:
        y = y * lax.rsqrt(jnp.mean(y * y, axis=-1, keepdims=True) + EPS) * fg_ref[...]
    o_ref[...] = y


def _combine(ys, poscol, gatecol, h2, sg, su, sd, x1, gt2, fg, off, cp, lb, tile, jmax, tiles_per_batch,
             final_norm):
    n_tok, D = h2.shape
    n_steps = n_tok // tile

    def full(a):
        nd = a.ndim
        return pl.BlockSpec(a.shape, lambda s, *_, _n=nd: (0,) * _n)

    def rows(w):
        return pl.BlockSpec((tile, w), lambda s, *_: (s, 0))

    return pl.pallas_call(
        functools.partial(_combine_kernel, n_steps=n_steps, tile=tile, final_norm=final_norm),
        grid_spec=pltpu.PrefetchScalarGridSpec(
            num_scalar_prefetch=3,
            grid=(n_steps,),
            in_specs=[pl.BlockSpec(memory_space=pl.ANY), rows(GATE_W), rows(GATE_W), rows(D),
                      full(sg), full(su), full(sd), rows(D),
                      pl.BlockSpec((1, 1, D), lambda s, *_: (s // tiles_per_batch, 0, 0)), full(fg)],
            out_specs=rows(D),
            scratch_shapes=[pltpu.VMEM((2, jmax, D), BF16), pltpu.SemaphoreType.DMA((2,)),
                            pltpu.VMEM((tile, D), F32)]),
        out_shape=jax.ShapeDtypeStruct((n_tok, D), F32),
        compiler_params=pltpu.CompilerParams(dimension_semantics=("arbitrary",), vmem_limit_bytes=VMEM_LIMIT),
        name="moe_combine",
    )(off, cp, lb, ys, poscol, gatecol, h2, sg, su, sd, x1, gt2, fg)


def _moe(h2, posrow, poscol, gatecol, cpad, wg, wu, wd, sg, su, sd, x1, gt2, fg, tile, final_norm):
    B, T, D = x1.shape
    n_tok = B * T
    n_tiles = n_tok // tile
    jmax = -(-(TOP_K * tile + N_EXPERTS * (RUN_ALIGN - 1)) // MOE_CHUNK) * MOE_CHUNK
    cap = -(-(TOP_K * n_tok + n_tiles * N_EXPERTS * (RUN_ALIGN - 1) + N_EXPERTS * (ROW_BLOCK - RUN_ALIGN))
            // ROW_BLOCK) * ROW_BLOCK

    cp = cpad[..., 0].reshape(n_tiles, N_EXPERTS)
    off, lb, foff, flen, blk_e, n_used = _plan(cp, cap // ROW_BLOCK)
    flat = lambda a: a.reshape(-1)

    xs = _dispatch(h2.reshape(n_tok, D), posrow.reshape(n_tiles, TOP_K, tile), flat(off), flat(cp), flat(lb),
                   foff[0], flen[0], cap, tile, jmax)
    ys = _experts(xs, blk_e[0], n_used[0], wg, wu, wd)
    out = _combine(ys, poscol.reshape(n_tok, GATE_W), gatecol.reshape(n_tok, GATE_W), h2.reshape(n_tok, D),
                   sg, su, sd, x1.reshape(n_tok, D), gt2, fg, flat(off), flat(cp), flat(lb), tile, jmax,
                   T // tile, final_norm)
    return out.reshape(B, T, D)


def _misc_lanes(vec, start):
    return jnp.zeros((1, MISC_W), F32).at[0, start:start + vec.shape[0]].set(vec.astype(F32))


def kernel(x, c, ada_w, ada_b, norm1_g, w_in, conv_w, a_log, dt_bias, dn_norm_g, kv_norm_g, w_uk, w_uv,
           idx_k_ln_g, idx_k_ln_b, w_out, norm2_g, router_w, router_b, exp_w_gate, exp_w_up, exp_w_down,
           sh_w_gate, sh_w_up, sh_w_down, final_g):
    B, T, D = x.shape
    depth = ada_w.shape[0]
    topk = min(IDX_TOPK_MAX, T // 4)
    tm = min(512, T)
    r_dn = min(256, T)

    cond_in = jnp.zeros((8, D), F32).at[:B].set(c)
    pos = jnp.arange(tm)
    tri = ((pos[:, None] // CHUNK == pos[None, :] // CHUNK) & (pos[:, None] >= pos[None, :])).astype(F32)

    for l in range(depth):
        mod = _ada(cond_in, ada_w[l], ada_b[l][None, :])[:B]
        sh1, sc1, gt1, sh2, sc2, gt2 = [m[:, None, :] for m in jnp.split(mod, 6, axis=-1)]

        offs = [0]
        for s in (DN_QK, DN_QK, DN_V, DN_V, DN_HEADS, DN_HEADS, SA_Q, KV_RANK, IDX_Q, IDX_DIM, IDX_HEADS):
            offs.append(offs[-1] + s)
        w = w_in[l]
        wc = w[:, offs[0]:offs[3]].astype(BF16)
        wz = w[:, offs[3]:offs[4]].astype(BF16)
        wq = w[:, offs[6]:offs[7]].astype(BF16)
        wkv = w[:, offs[7]:offs[8]].astype(BF16)
        wqi = w[:, offs[8]:offs[9]].astype(BF16)
        wm = jnp.concatenate([w[:, offs[9]:offs[10]], w[:, offs[4]:offs[5]], w[:, offs[5]:offs[6]],
                              w[:, offs[10]:offs[11]],
                              jnp.zeros((D, MISC_W - IDX_DIM - 2 * DN_HEADS - IDX_HEADS), F32)],
                             axis=1).astype(BF16)
        ukt = jnp.swapaxes(w_uk[l], 1, 2).astype(BF16)

        q, k, v, z, qlat, ckv, qix, kix, misc = _inproj(
            x, sc1, sh1, norm1_g[l][None, :], wc, wz, wq, wkv, wqi, wm, conv_w[l], ukt,
            kv_norm_g[l][None, :], _misc_lanes(idx_k_ln_g[l], M_KIX), _misc_lanes(idx_k_ln_b[l], M_KIX),
            _misc_lanes(a_log[l], M_A), _misc_lanes(dt_bias[l], M_A), tri, tm)

        odn = _deltanet(q, k, v, z, misc, dn_norm_g[l][None, :], r_dn)
        olat = _dsa(qix, misc, kix, qlat, ckv, topk)

        x1, h2, posrow, poscol, gatecol, cpad = _outproj(
            x, odn, olat, w_uv[l].astype(BF16), w_out[l].astype(BF16), gt1, sc2, sh2,
            norm2_g[l][None, :], router_w[l].T, router_b[l][:, None], tm)

        x = _moe(h2, posrow, poscol, gatecol, cpad, exp_w_gate[l].astype(BF16), exp_w_up[l].astype(BF16),
                 exp_w_down[l].astype(BF16), sh_w_gate[l].astype(BF16), sh_w_up[l].astype(BF16),
                 sh_w_down[l].astype(BF16), x1, gt2, final_g[None, :], tm, l == depth - 1)
    return x
```

```python
import functools

import jax
import jax.numpy as jnp
from jax import lax
from jax.experimental import pallas as pl
from jax.experimental.pallas import tpu as pltpu

F32 = jnp.float32
BF16 = jnp.bfloat16
I32 = jnp.int32
HIGHEST = lax.Precision.HIGHEST

EPS = 1e-6
CHUNK = 64
DN_HEADS = 4
DN_DK = 128
DN_DV = 128
CONV_K = 4
SA_HEADS = 4
SA_DQK = 128
SA_DV = 128
KV_RANK = 256
IDX_HEADS = 4
IDX_DIM = 64
IDX_TOPK_MAX = 256
SM_SCALE = SA_DQK ** -0.5
LOG2E = 1.4426950408889634
IDX_W_SCALE = (IDX_HEADS * IDX_DIM) ** -0.5
N_EXPERTS = 64
TOP_K = 8
N_GROUPS = 8
TOPK_GROUPS = 4
D_EXPERT = 256
ROUTED_SCALE = 2.5
GATE_W = 128
RUN_ALIGN = 16
ROW_BLOCK = 1024
MOE_CHUNK = 512

DN_QK = DN_HEADS * DN_DK
DN_V = DN_HEADS * DN_DV
CONV_DIM = 2 * DN_QK + DN_V
SA_Q = SA_HEADS * SA_DQK
IDX_Q = IDX_HEADS * IDX_DIM

MISC_W = 128
M_KIX = 0
M_BETA = IDX_DIM
M_A = M_BETA + DN_HEADS
M_WIX = M_A + DN_HEADS

DN_SUB = 2 * CHUNK
QBLOCK = 256
SOFTMAX_TINY = 2.0 ** -100
KEY_TILE = 1024
PLANE_COLS = 32 * 128
PLANE_SHIFT = 12
PLANE_ROWS = 16
INT_MIN = -2 ** 31
NEG_BIG = -1e30
VMEM_LIMIT = 56 * 1024 * 1024


def _nt_dot(a, b, precision=None):
    return lax.dot_general(a, b, (((1,), (1,)), ((), ())), preferred_element_type=F32,
                           precision=precision)


def _dot(a, b, precision=None):
    return jnp.dot(a, b, preferred_element_type=F32, precision=precision)


def _silu(x):
    return x * jax.nn.sigmoid(x)


def _softplus(x):
    return jnp.maximum(x, 0.0) + jnp.log(1.0 + jnp.exp(-jnp.abs(x)))


def _ada_kernel(c_ref, w_ref, b_ref, o_ref):
    cond = _silu(c_ref[...])
    o_ref[...] = _dot(cond, w_ref[...], HIGHEST) + b_ref[...]


def _ada(c_pad, ada_w, ada_b):
    rows, d = c_pad.shape
    n_out = ada_w.shape[1]
    return pl.pallas_call(
        _ada_kernel,
        grid=(n_out // d,),
        in_specs=[pl.BlockSpec((rows, d), lambda j: (0, 0)),
                  pl.BlockSpec((d, d), lambda j: (0, j)),
                  pl.BlockSpec((1, d), lambda j: (0, j))],
        out_specs=pl.BlockSpec((rows, d), lambda j: (0, j)),
        out_shape=jax.ShapeDtypeStruct((rows, n_out), F32),
        compiler_params=pltpu.CompilerParams(vmem_limit_bytes=VMEM_LIMIT),
        name="ada",
    )(c_pad, ada_w, ada_b)


def _inproj_kernel(x_ref, sc_ref, sh_ref, g1_ref, wc_ref, wz_ref, wq_ref, wkv_ref, wqi_ref, wm_ref,
                   convw_ref, ukt_ref, kvg_ref, lng_ref, lnb_ref, alog_ref, dtb_ref, tri_ref,
                   q_ref, k_ref, v_ref, z_ref, qlat_ref, ckv_ref, qix_ref, kix_ref, misc_ref,
                   conv_buf):
    tm = x_ref.shape[1]
    i = pl.program_id(1)

    x = x_ref[0]
    h = x * lax.rsqrt(jnp.mean(x * x, axis=-1, keepdims=True) + EPS) * g1_ref[...]
    h = h * (1.0 + sc_ref[0]) + sh_ref[0]
    hb = h.astype(BF16)

    @pl.when(i == 0)
    def _():
        conv_buf[0:8, :] = jnp.zeros((8, CONV_DIM), F32)

    conv_buf[8:8 + tm, :] = _dot(hb, wc_ref[...])
    for grp, dst in ((0, q_ref), (1, k_ref), (2, v_ref)):
        cols = slice(grp * DN_QK, (grp + 1) * DN_QK)
        y = jnp.zeros((tm, DN_QK), F32)
        for j in range(CONV_K):
            y = y + convw_ref[j:j + 1, cols] * conv_buf[8 - (CONV_K - 1) + j:8 - (CONV_K - 1) + j + tm, cols]
        y = _silu(y)
        if grp < 2:
            outs = []
            for hd in range(DN_HEADS):
                yh = y[:, hd * DN_DK:(hd + 1) * DN_DK]
                yh = yh * lax.rsqrt(jnp.sum(yh * yh, axis=-1, keepdims=True) + EPS)
                if grp == 0:
                    yh = yh * (DN_DK ** -0.5)
                outs.append(yh)
            y = jnp.concatenate(outs, axis=-1)
        dst[0] = y
    conv_buf[0:8, :] = conv_buf[tm:tm + 8, :]

    z_ref[0] = _dot(hb, wz_ref[...])

    q_sa = _dot(hb, wq_ref[...]).astype(BF16)
    for hd in range(SA_HEADS):
        ql = _dot(q_sa[:, hd * SA_DQK:(hd + 1) * SA_DQK], ukt_ref[hd]) * (SM_SCALE * LOG2E)
        qlat_ref[0, hd] = ql.astype(BF16)

    ckv = _dot(hb, wkv_ref[...])
    ckv = ckv * lax.rsqrt(jnp.mean(ckv * ckv, axis=-1, keepdims=True) + EPS) * kvg_ref[...]
    ckv_ref[0] = ckv.astype(BF16)

    q_ix = _dot(hb, wqi_ref[...]).astype(BF16)
    for hd in range(IDX_HEADS):
        qix_ref[0, hd] = q_ix[:, hd * IDX_DIM:(hd + 1) * IDX_DIM]

    m = _dot(hb, wm_ref[...])
    lane = lax.broadcasted_iota(I32, (tm, MISC_W), 1)
    is_k = lane < IDX_DIM
    mu = jnp.sum(jnp.where(is_k, m, 0.0), axis=-1, keepdims=True) * (1.0 / IDX_DIM)
    kc = jnp.where(is_k, m - mu, 0.0)
    var = jnp.sum(kc * kc, axis=-1, keepdims=True) * (1.0 / IDX_DIM)
    kn = kc * lax.rsqrt(var + EPS) * lng_ref[...] + lnb_ref[...]
    kix_ref[0] = kn[:, :IDX_DIM].astype(BF16)

    beta = jax.nn.sigmoid(m)
    g = -jnp.exp(alog_ref[...]) * _softplus(m + dtb_ref[...])
    is_a = (lane >= M_A) & (lane < M_A + DN_HEADS)
    g = jnp.where(is_a, g, 0.0)
    gc = _dot(tri_ref[...], g, HIGHEST)
    is_b = (lane >= M_BETA) & (lane < M_BETA + DN_HEADS)
    is_w = (lane >= M_WIX) & (lane < M_WIX + IDX_HEADS)
    misc_ref[0] = jnp.where(is_b, beta, jnp.where(is_a, gc, jnp.where(is_w, m * IDX_W_SCALE, 0.0)))


def _inproj(x, sc1, sh1, g1, wc, wz, wq, wkv, wqi, wm, conv_w, ukt, kvg, lng, lnb, alog, dtb, tri, tm):
    B, T, D = x.shape
    nt = T // tm

    def full(a):
        nd = a.ndim
        return pl.BlockSpec(a.shape, lambda b, i, _n=nd: (0,) * _n)

    def rows(w):
        return pl.BlockSpec((1, tm, w), lambda b, i: (b, i, 0))

    per_b = pl.BlockSpec((1, 1, D), lambda b, i: (b, 0, 0))
    def head_rows(h, w):
        return pl.BlockSpec((1, h, tm, w), lambda b, i: (b, 0, i, 0))

    outs = [(None, DN_QK, F32), (None, DN_QK, F32), (None, DN_V, F32), (None, DN_V, F32),
            (SA_HEADS, KV_RANK, BF16), (None, KV_RANK, BF16), (IDX_HEADS, IDX_DIM, BF16),
            (None, IDX_DIM, BF16), (None, MISC_W, F32)]
    return pl.pallas_call(
        _inproj_kernel,
        grid=(B, nt),
        in_specs=[rows(D), per_b, per_b, full(g1), full(wc), full(wz), full(wq), full(wkv), full(wqi),
                  full(wm), full(conv_w), full(ukt), full(kvg), full(lng), full(lnb), full(alog),
                  full(dtb), full(tri)],
        out_specs=[rows(w) if h is None else head_rows(h, w) for h, w, _ in outs],
        out_shape=[jax.ShapeDtypeStruct((B, T, w) if h is None else (B, h, T, w), dt) for h, w, dt in outs],
        scratch_shapes=[pltpu.VMEM((tm + 8, CONV_DIM), F32)],
        compiler_params=pltpu.CompilerParams(dimension_semantics=("arbitrary", "arbitrary"),
                                             vmem_limit_bytes=VMEM_LIMIT),
        name="inproj",
    )(x, sc1, sh1, g1, wc, wz, wq, wkv, wqi, wm, conv_w, ukt, kvg, lng, lnb, alog, dtb, tri)


def _deltanet_kernel(q_ref, k_ref, v_ref, z_ref, misc_ref, ng_ref, o_ref, s_ref):
    R = q_ref.shape[1]
    n_chunks = R // CHUNK

    @pl.when(pl.program_id(1) == 0)
    def _():
        s_ref[...] = jnp.zeros(s_ref.shape, F32)

    misc = misc_ref[0]
    misc_t = misc.T
    SB = min(DN_SUB, R)
    row = lax.broadcasted_iota(I32, (SB, SB), 0)
    col = lax.broadcasted_iota(I32, (SB, SB), 1)
    same = (row // CHUNK) == (col // CHUNK)
    lower = same & (row >= col)
    strict = same & (row > col)
    eye = (row == col).astype(F32)

    def mm(a, b):
        return _dot(a.astype(BF16), b.astype(BF16))

    def mm3(a, b):
        ah = a.astype(BF16)
        bh = b.astype(BF16)
        al = (a - ah.astype(F32)).astype(BF16)
        bl = (b - bh.astype(F32)).astype(BF16)
        return _dot(jnp.concatenate([ah, ah, al], axis=1), jnp.concatenate([bh, bl, bh], axis=0))

    heads = range(DN_HEADS)
    subs = range(R // SB)
    chains = [(hd, sb) for hd in heads for sb in subs]
    cols = [slice(hd * DN_DK, (hd + 1) * DN_DK) for hd in heads]
    qh = [q_ref[0, :, cols[hd]] for hd in heads]
    kh = [k_ref[0, :, cols[hd]] for hd in heads]
    beta = [misc[:, M_BETA + hd:M_BETA + hd + 1] for hd in heads]
    gc_c = [misc[:, M_A + hd:M_A + hd + 1] for hd in heads]
    eg = [jnp.exp(gc_c[hd]) for hd in heads]
    kb = [kh[hd] * beta[hd] for hd in heads]
    rhs = [jnp.concatenate([v_ref[0, :, cols[hd]] * beta[hd], kb[hd] * eg[hd]], axis=-1) for hd in heads]
    q_dec = [qh[hd] * eg[hd] for hd in heads]

    def rows_of(sb):
        return slice(sb * SB, (sb + 1) * SB)

    decay, a, qk_sb = {}, {}, {}
    for hd, sb in chains:
        bs = rows_of(sb)
        gc_r = misc_t[M_A + hd:M_A + hd + 1, bs]
        decay[hd, sb] = jnp.where(lower, jnp.exp(jnp.where(lower, gc_c[hd][bs] - gc_r, 0.0)), 0.0)
    for hd, sb in chains:
        bs = rows_of(sb)
        khb = kh[hd][bs].astype(BF16)
        a[hd, sb] = jnp.where(strict, _nt_dot(kb[hd][bs].astype(BF16), khb) * decay[hd, sb], 0.0)
        qk_sb[hd, sb] = jnp.where(lower, _nt_dot(qh[hd][bs].astype(BF16), khb) * decay[hd, sb], 0.0)
    p = {ch: eye - a[ch] for ch in chains}
    xp = {ch: mm3(a[ch], a[ch]) for ch in chains}
    n_sq = 1
    while True:
        p = {ch: p[ch] + mm3(p[ch], xp[ch]) for ch in chains}
        n_sq *= 2
        if n_sq * 2 >= CHUNK:
            break
        xp = {ch: mm3(xp[ch], xp[ch]) for ch in chains}
    sol = {(hd, sb): mm(p[hd, sb], rhs[hd][rows_of(sb)]) for hd, sb in chains}

    def chunk_of(c):
        per = SB // CHUNK
        return c // per, slice((c % per) * CHUNK, (c % per + 1) * CHUNK)

    s = [s_ref[hd] for hd in heads]
    o_parts = [[] for _ in heads]
    for c in range(n_chunks):
        rs = slice(c * CHUNK, (c + 1) * CHUNK)
        sb, r = chunk_of(c)
        gl = [gc_c[hd][(c + 1) * CHUNK - 1:(c + 1) * CHUNK, :] for hd in heads]
        k_dec = [kh[hd][rs] * jnp.exp(gl[hd] - gc_c[hd][rs]) for hd in heads]
        v_new = [sol[hd, sb][r, :DN_DV] - mm(sol[hd, sb][r, DN_DV:], s[hd]) for hd in heads]
        for hd in heads:
            o_parts[hd].append(mm(q_dec[hd][rs], s[hd]) + mm(qk_sb[hd, sb][r, r], v_new[hd]))
        s = [s[hd] * jnp.exp(gl[hd]) + mm(k_dec[hd].T, v_new[hd]) for hd in heads]
    for hd in heads:
        s_ref[hd] = s[hd]
        o = jnp.concatenate(o_parts[hd], axis=0)
        o = o * lax.rsqrt(jnp.mean(o * o, axis=-1, keepdims=True) + EPS) * ng_ref[...]
        o_ref[0, :, cols[hd]] = (o * _silu(z_ref[0, :, cols[hd]])).astype(BF16)


def _deltanet(q, k, v, z, misc, ng, R):
    B, T, _ = q.shape

    def rows(w):
        return pl.BlockSpec((1, R, w), lambda b, i: (b, i, 0))

    return pl.pallas_call(
        _deltanet_kernel,
        grid=(B, T // R),
        in_specs=[rows(DN_QK), rows(DN_QK), rows(DN_V), rows(DN_V), rows(MISC_W),
                  pl.BlockSpec((1, DN_DV), lambda b, i: (0, 0))],
        out_specs=rows(DN_V),
        out_shape=jax.ShapeDtypeStruct((B, T, DN_V), BF16),
        scratch_shapes=[pltpu.VMEM((DN_HEADS, DN_DK, DN_DV), F32)],
        compiler_params=pltpu.CompilerParams(dimension_semantics=("arbitrary", "arbitrary"),
                                             vmem_limit_bytes=VMEM_LIMIT),
        name="deltanet",
    )(q, k, v, z, misc, ng)


def _bit_transpose32(words):
    w = list(words)
    j = 16
    m = 0x0000FFFF
    while j:
        k = 0
        m_i32 = jnp.int32(m - (1 << 32) if m >= (1 << 31) else m)
        while k < 32:
            t = (w[k] ^ lax.shift_right_logical(w[k + j], jnp.full_like(w[k], j))) & m_i32
            w[k] = w[k] ^ t
            w[k + j] = w[k + j] ^ jnp.left_shift(t, j)
            k = (k + j + 1) & ~j
        j >>= 1
        m = (m ^ (m << j)) & 0xFFFFFFFF
    return w


def _dsa_kernel(qix_ref, misc_ref, kix_ref, qlat_ref, ckv_ref, o_ref, keys_ref, planes_ref, bias_ref,
                mx_ref, l_ref, acc_ref, kvmax_ref, qprev_ref, *, topk, pos_bits, n_cg_max):
    g = pl.program_id(1)
    n_blocks = pl.num_programs(1) - 1
    i = g
    QB = QBLOCK
    KT = KEY_TILE
    has_front = g < n_blocks
    has_back = g >= 1
    n_kt = jnp.where(has_front, (i * QB + QB + KT - 1) // KT, 0)
    n_kt_back = (g * QB + KT - 1) // KT

    rowi = lax.broadcasted_iota(I32, (QB, KT), 0)
    coli = lax.broadcasted_iota(I32, (QB, KT), 1)
    limit = i * QB + (rowi // CHUNK + 1) * CHUNK

    misc = misc_ref[0]
    qix = qix_ref[0].reshape(IDX_HEADS * QB, IDX_DIM)
    q_st = qlat_ref[0].reshape(SA_HEADS * QB, KV_RANK)

    def score_body(masked, kt, carry):
        k0 = pl.multiple_of(kt * KT, KT)
        kx = kix_ref[0, pl.ds(k0, KT), :]
        rel = jnp.maximum(_nt_dot(qix, kx), 0.0)
        sc = jnp.zeros((QB, KT), F32)
        for hd in range(IDX_HEADS):
            sc = sc + misc[:, M_WIX + hd:M_WIX + hd + 1] * rel[hd * QB:(hd + 1) * QB]
        sc = jnp.where(sc == 0.0, 0.0, sc)
        bits = pltpu.bitcast(sc, I32)
        key = jnp.where(bits < 0, bits ^ 0x7FFFFFFF, bits)
        keys_ref[:, pl.ds(k0, KT)] = jnp.where(k0 + coli < limit, key, INT_MIN) if masked else key
        return carry

    def logit_chunks(k0):
        kv = ckv_ref[0, pl.ds(k0, KT), :]
        s = _nt_dot(qprev_ref[...], kv)
        bias = bias_ref[:, pl.ds(k0, KT)]
        chunks = []
        for j in range(KT // 128):
            bj = bias[:, j * 128:(j + 1) * 128]
            chunks.append(s[:, j * 128:(j + 1) * 128] + jnp.concatenate([bj] * SA_HEADS, axis=0))
        return kv, chunks

    def pv_body(kt, carry):
        kv, chunks = logit_chunks(pl.multiple_of(kt * KT, KT))
        shift = mx_ref[...]
        ps = [jnp.exp2(c - shift) for c in chunks]
        l_ref[...] = functools.reduce(jnp.add, ps, l_ref[...])
        p = jnp.concatenate([pj.astype(BF16) for pj in ps], axis=1)
        acc_ref[...] += _dot(p, kv)
        return carry

    def clear_sums():
        l_ref[...] = jnp.zeros(l_ref.shape, F32)
        acc_ref[...] = jnp.zeros(acc_ref.shape, F32)

    def write_back_block():
        l_row = jnp.sum(l_ref[...], axis=-1, keepdims=True)
        o_ref[0] = (acc_ref[...] / l_row).astype(BF16).reshape(SA_HEADS, QB, KV_RANK)
        return l_row

    pl.when(has_back)(clear_sums)

    n_open = jnp.where(has_front, (i * QB + CHUNK) // KT, 0)
    n_fused = jnp.minimum(n_open, n_kt_back)

    def fused_body(kt, carry):
        pv_body(kt, carry)
        return score_body(False, kt, carry)

    lax.fori_loop(0, n_fused, fused_body, 0)
    lax.fori_loop(n_fused, n_kt_back, pv_body, 0)
    lax.fori_loop(n_fused, n_open, functools.partial(score_body, False), 0)
    lax.fori_loop(n_open, n_kt, functools.partial(score_body, True), 0)

    @pl.when(has_back)
    def _():
        l_fast = write_back_block()

        @pl.when(jnp.min(l_fast) < SOFTMAX_TINY)
        def _():
            mx_ref[...] = jnp.full(mx_ref.shape, NEG_BIG, F32)

            def max_body(kt, carry):
                _, chunks = logit_chunks(pl.multiple_of(kt * KT, KT))
                mx_ref[...] = functools.reduce(jnp.maximum, chunks, mx_ref[...])
                return carry

            lax.fori_loop(0, n_kt_back, max_body, 0)
            mx_ref[...] = jnp.broadcast_to(jnp.max(mx_ref[...], axis=-1, keepdims=True), mx_ref.shape)
            clear_sums()
            lax.fori_loop(0, n_kt_back, pv_body, 0)
            write_back_block()

    n_cg = (n_kt * KT + PLANE_COLS - 1) // PLANE_COLS

    def fill_body(kt, carry):
        keys_ref[:, pl.ds(pl.multiple_of(kt * KT, KT), KT)] = jnp.full((QB, KT), INT_MIN, I32)
        return carry

    lax.fori_loop(n_kt, n_cg * (PLANE_COLS // KT), fill_body, 0)

    @pl.when(i == 0)
    def _():
        planes_ref[...] = jnp.zeros(planes_ref.shape, I32)

    def plane_body(step, carry):
        c = step // (QB // PLANE_ROWS)
        r0 = pl.multiple_of((step % (QB // PLANE_ROWS)) * PLANE_ROWS, PLANE_ROWS)
        words = [keys_ref[pl.ds(r0, PLANE_ROWS), pl.ds(pl.multiple_of(c * PLANE_COLS + j * 128, 128), 128)]
                 for j in range(32)]
        for b, plane in enumerate(_bit_transpose32(words)):
            planes_ref[c, b, pl.ds(r0, PLANE_ROWS), :] = ~plane if b == 0 else plane
        return carry

    lax.fori_loop(0, n_cg * (QB // PLANE_ROWS), plane_body, 0)

    ones_mat = jnp.ones((128, 128), BF16)

    def lane_count(words):
        pc = functools.reduce(jnp.add, [lax.population_count(x) for x in words])
        return _dot(pc.astype(F32).astype(BF16), ones_mat).astype(I32)

    def sel_body(groups, step, carry):
        cand, n_gt, tau_u = carry
        hi = [cand[c] & planes_ref[c, 2 * step] for c in groups]
        lo = [cand[c] ^ hi[c] for c in groups]
        d3 = [hi[c] & planes_ref[c, 2 * step + 1] for c in groups]
        d2 = [hi[c] ^ d3[c] for c in groups]
        d1 = [lo[c] & planes_ref[c, 2 * step + 1] for c in groups]
        d0 = [lo[c] ^ d1[c] for c in groups]
        a3 = n_gt + lane_count(d3)
        a2 = a3 + lane_count(d2)
        a1 = a2 + lane_count(d1)
        is3 = a3 >= topk
        is2 = a2 >= topk
        is1 = a1 >= topk
        cand = tuple(jnp.where(is3, d3[c], jnp.where(is2, d2[c], jnp.where(is1, d1[c], d0[c]))) for c in groups)
        n_gt = jnp.where(is3, n_gt, jnp.where(is2, a3, jnp.where(is1, a2, a1)))
        digit = jnp.where(is3, 3, jnp.where(is2, 2, jnp.where(is1, 1, 0)))
        return cand, n_gt, tau_u | jnp.left_shift(digit, 30 - 2 * step)

    def radix_select(n_groups):
        groups = range(n_groups)
        start = (tuple(jnp.full((QB, 128), -1, I32) for _ in groups), jnp.zeros((QB, 128), I32),
                 jnp.zeros((QB, 128), I32))
        cand_, n_gt_, tau_ = lax.fori_loop(0, 16, functools.partial(sel_body, groups), start)
        rest = tuple(jnp.zeros((QB, 128), I32) for _ in range(n_cg_max - n_groups))
        return cand_ + rest, n_gt_, tau_

    cand, n_gt, tau_u = lax.switch(n_cg - 1, [functools.partial(radix_select, g + 1) for g in range(n_cg_max)])
    tau = tau_u ^ INT_MIN
    sentinel = tau == INT_MIN
    cand = tuple(jnp.where(sentinel, 0, cand[c]) for c in range(n_cg_max))
    need = topk - n_gt
    any_tie = jnp.max(jnp.where(lane_count(cand) > need, 1, 0)) > 0

    def fast_bias():
        floor = jnp.where(sentinel, INT_MIN + 1, tau)

        def body(kt, carry):
            for j in range(KT // 128):
                cols = pl.ds(pl.multiple_of(kt * KT + j * 128, 128), 128)
                bias_ref[:, cols] = jnp.where(keys_ref[:, cols] >= floor, 0.0, NEG_BIG)
            return carry

        lax.fori_loop(0, n_kt, body, 0)

    def tie_bias():
        lane = lax.broadcasted_iota(I32, (QB, 128), 1)

        def pos_mask(p, c):
            cg = lax.shift_right_logical(p, jnp.full_like(p, PLANE_SHIFT))
            j0 = lax.shift_right_logical(p, jnp.full_like(p, 7)) & 31
            below = ~lax.shift_right_logical(jnp.full_like(p, -1), j0)
            bit = lax.shift_right_logical(jnp.full_like(p, INT_MIN), j0)
            word = below | jnp.where(lane < (p & 127), bit, 0)
            return jnp.where(cg > c, -1, jnp.where(cg == c, word, 0))

        def pos_body(b, q):
            cq = q + jnp.left_shift(jnp.int32(1), pos_bits - 1 - b)
            cnt = lane_count([cand[c] & pos_mask(cq, c) for c in range(n_cg_max)])
            return jnp.where(cnt < need, cq, q)

        pstar = lax.fori_loop(0, pos_bits, pos_body, jnp.zeros((QB, 128), I32)) + 1
        pstar = jnp.where(sentinel, 0, pstar)

        def body(kt, carry):
            for j in range(KT // 128):
                c0 = pl.multiple_of(kt * KT + j * 128, 128)
                kk = keys_ref[:, pl.ds(c0, 128)]
                tie = jnp.where(c0 + lane < pstar, 0.0, NEG_BIG)
                bias_ref[:, pl.ds(c0, 128)] = jnp.where(kk > tau, 0.0, jnp.where(kk == tau, tie, NEG_BIG))
            return carry

        lax.fori_loop(0, n_kt, body, 0)

    lax.cond(any_tie, tie_bias, fast_bias)

    @pl.when(i == 0)
    def _():
        def norm_body(r, best):
            x = ckv_ref[0, pl.ds(pl.multiple_of(r * KT, KT), KT), :].astype(F32)
            return jnp.maximum(best, jnp.max(jnp.sum(x * x, axis=1, keepdims=True), axis=0, keepdims=True))

        n_rows = ckv_ref.shape[1]
        kv_sq = lax.fori_loop(0, n_rows // KT, norm_body, jnp.zeros((1, 1), F32))
        kvmax_ref[...] = jnp.broadcast_to(jnp.sqrt(kv_sq), kvmax_ref.shape)

    qf = q_st.astype(F32)
    q_norm = jnp.sqrt(jnp.sum(qf * qf, axis=1, keepdims=True))
    mx_ref[...] = jnp.broadcast_to(q_norm, mx_ref.shape) * kvmax_ref[0:1, :] * 1.001 + 1e-3
    qprev_ref[...] = q_st


def _dsa(qix, misc, kix, qlat, ckv, topk):
    B, T, _ = kix.shape
    n_cg_max = -(-T // PLANE_COLS)
    t_pad = n_cg_max * PLANE_COLS
    pos_bits = (t_pad - 1).bit_length()

    n_blocks = T // QBLOCK

    def rows(w):
        return pl.BlockSpec((1, QBLOCK, w), lambda b, g: (b, jnp.minimum(g, n_blocks - 1), 0))

    def head_rows(h, w):
        return pl.BlockSpec((1, h, QBLOCK, w), lambda b, g: (b, 0, jnp.minimum(g, n_blocks - 1), 0))

    def per_b(w):
        return pl.BlockSpec((1, T, w), lambda b, g: (b, 0, 0))

    out_rows = pl.BlockSpec((1, SA_HEADS, QBLOCK, KV_RANK), lambda b, g: (b, 0, jnp.maximum(g - 1, 0), 0))

    return pl.pallas_call(
        functools.partial(_dsa_kernel, topk=topk, pos_bits=pos_bits, n_cg_max=n_cg_max),
        grid=(B, n_blocks + 1),
        in_specs=[head_rows(IDX_HEADS, IDX_DIM), rows(MISC_W), per_b(IDX_DIM), head_rows(SA_HEADS, KV_RANK),
                  per_b(KV_RANK)],
        out_specs=out_rows,
        out_shape=jax.ShapeDtypeStruct((B, SA_HEADS, T, KV_RANK), BF16),
        scratch_shapes=[pltpu.VMEM((QBLOCK, t_pad), I32),
                        pltpu.VMEM((n_cg_max, 32, QBLOCK, 128), I32),
                        pltpu.VMEM((QBLOCK, t_pad), F32),
                        pltpu.VMEM((SA_HEADS * QBLOCK, 128), F32),
                        pltpu.VMEM((SA_HEADS * QBLOCK, 128), F32),
                        pltpu.VMEM((SA_HEADS * QBLOCK, KV_RANK), F32),
                        pltpu.VMEM((8, 128), F32),
                        pltpu.VMEM((SA_HEADS * QBLOCK, KV_RANK), BF16)],
        compiler_params=pltpu.CompilerParams(dimension_semantics=("arbitrary", "arbitrary"),
                                             vmem_limit_bytes=VMEM_LIMIT),
        name="dsa",
    )(qix, misc, kix, qlat, ckv)


def _first_max(v, idx, axis):
    m = jnp.max(v, axis=axis, keepdims=True)
    big = jnp.int32(2 ** 30)
    first = jnp.min(jnp.where(v == m, idx, big), axis=axis, keepdims=True)
    return m, idx == first


def _outproj_kernel(x_ref, odn_ref, olat_ref, uv_ref, wo_ref, gt_ref, sc_ref, sh_ref, g2_ref, rwt_ref,
                    rb_ref, lstrict_ref, ustrict_ref, x1_ref, h2_ref, posrow_ref, poscol_ref, gatecol_ref,
                    cpad_ref):
    tm = x_ref.shape[1]
    parts = [odn_ref[0]]
    for hd in range(SA_HEADS):
        parts.append(_dot(olat_ref[0, hd], uv_ref[hd]).astype(BF16))
    mix = jnp.concatenate(parts, axis=-1)
    x1 = x_ref[0] + gt_ref[0] * _dot(mix, wo_ref[...])
    x1_ref[0] = x1
    h2 = x1 * lax.rsqrt(jnp.mean(x1 * x1, axis=-1, keepdims=True) + EPS) * g2_ref[...]
    h2 = h2 * (1.0 + sc_ref[0]) + sh_ref[0]
    h2_ref[0] = h2.astype(BF16)

    per_g = N_EXPERTS // N_GROUPS
    s = jax.nn.sigmoid(_nt_dot(rwt_ref[...], h2, HIGHEST))
    choice = s + rb_ref[...]
    ig = lax.broadcasted_iota(I32, (per_g, tm), 0)
    gscore = []
    for gidx in range(N_GROUPS):
        cg = choice[gidx * per_g:(gidx + 1) * per_g]
        m1, hot1 = _first_max(cg, ig, 0)
        gscore.append(m1 + jnp.max(jnp.where(hot1, -jnp.inf, cg), axis=0, keepdims=True))
    gsel = [jnp.zeros((1, tm), jnp.bool_) for _ in range(N_GROUPS)]
    for _ in range(TOPK_GROUPS):
        best = functools.reduce(jnp.maximum, gscore)
        found = jnp.zeros((1, tm), jnp.bool_)
        for gidx in range(N_GROUPS):
            hot = (gscore[gidx] == best) & jnp.logical_not(found)
            found = found | hot
            gsel[gidx] = gsel[gidx] | hot
            gscore[gidx] = jnp.where(hot, -jnp.inf, gscore[gidx])
    masked = jnp.concatenate(
        [jnp.where(gsel[gidx], choice[gidx * per_g:(gidx + 1) * per_g], -jnp.inf) for gidx in range(N_GROUPS)],
        axis=0)
    ei = lax.broadcasted_iota(I32, masked.shape, 0)
    gate = jnp.zeros(masked.shape, F32)
    hots = []
    for _ in range(TOP_K):
        _, hot = _first_max(masked, ei, 0)
        hots.append(hot)
        gate = jnp.where(hot, s, gate)
        masked = jnp.where(hot, -jnp.inf, masked)
    gate = gate / jnp.sum(gate, axis=0, keepdims=True) * ROUTED_SCALE

    picked = jnp.where(functools.reduce(jnp.logical_or, hots), 1.0, 0.0)
    cnt = jnp.sum(picked, axis=1, keepdims=True)
    cpad = jnp.floor((cnt + (RUN_ALIGN - 1)) * (1.0 / RUN_ALIGN)) * RUN_ALIGN
    cpad_b = jnp.broadcast_to(cpad, (N_EXPERTS, GATE_W))
    lbase = _dot(lstrict_ref[...], cpad_b, HIGHEST)[:, :1]
    rank = _dot(picked.astype(BF16), ustrict_ref[...])
    pos = lbase + rank
    ri = lax.broadcasted_iota(I32, (GATE_W, tm), 0)
    pos_rows = jnp.zeros((GATE_W, tm), F32)
    gate_rows = jnp.zeros((GATE_W, tm), F32)
    for k, hot in enumerate(hots):
        pos_rows = jnp.where(ri == k, jnp.sum(jnp.where(hot, pos, 0.0), axis=0, keepdims=True), pos_rows)
        gate_rows = jnp.where(ri == k, jnp.sum(jnp.where(hot, gate, 0.0), axis=0, keepdims=True), gate_rows)
    posrow_ref[0, 0] = pos_rows[:TOP_K].astype(I32)
    poscol_ref[0] = pos_rows.T.astype(I32)
    gatecol_ref[0] = gate_rows.T
    cpad_ref[0, 0] = cpad_b.astype(I32)


def _outproj(x, odn, olat, uv, wo, gt1, sc2, sh2, g2, rwt, rb, tm):
    B, T, D = x.shape
    nt = T // tm
    ex = jnp.arange(N_EXPERTS)
    lstrict = (ex[:, None] > ex[None, :]).astype(F32)
    tok = jnp.arange(tm)
    ustrict = (tok[:, None] < tok[None, :]).astype(BF16)

    def full(a):
        nd = a.ndim
        return pl.BlockSpec(a.shape, lambda b, i, _n=nd: (0,) * _n)

    def rows(w):
        return pl.BlockSpec((1, tm, w), lambda b, i: (b, i, 0))

    def per_tile(h, w):
        return pl.BlockSpec((1, 1, h, w), lambda b, i: (b, i, 0, 0))

    per_b = pl.BlockSpec((1, 1, D), lambda b, i: (b, 0, 0))
    return pl.pallas_call(
        _outproj_kernel,
        grid=(B, nt),
        in_specs=[rows(D), rows(DN_V),
                  pl.BlockSpec((1, SA_HEADS, tm, KV_RANK), lambda b, i: (b, 0, i, 0)),
                  full(uv), full(wo), per_b, per_b, per_b,
                  full(g2), full(rwt), full(rb), full(lstrict), full(ustrict)],
        out_specs=[rows(D), rows(D), per_tile(TOP_K, tm), rows(GATE_W), rows(GATE_W),
                   per_tile(N_EXPERTS, GATE_W)],
        out_shape=[jax.ShapeDtypeStruct((B, T, D), F32), jax.ShapeDtypeStruct((B, T, D), BF16),
                   jax.ShapeDtypeStruct((B, nt, TOP_K, tm), I32),
                   jax.ShapeDtypeStruct((B, T, GATE_W), I32),
                   jax.ShapeDtypeStruct((B, T, GATE_W), F32),
                   jax.ShapeDtypeStruct((B, nt, N_EXPERTS, GATE_W), I32)],
        compiler_params=pltpu.CompilerParams(dimension_semantics=("arbitrary", "arbitrary"),
                                             vmem_limit_bytes=VMEM_LIMIT),
        name="outproj",
    )(x, odn, olat, uv, wo, gt1, sc2, sh2, g2, rwt, rb, lstrict, ustrict)


def _piece_sizes(max_rows):
    sizes = []
    z = RUN_ALIGN
    while z <= max_rows:
        sizes.append(z)
        z *= 2
    return sizes[::-1]


def _for_run_pieces(length, max_rows, fn):
    for z in _piece_sizes(max_rows):
        start = length & ~(2 * z - 1)

        @pl.when((length & z) != 0)
        def _(start=start, z=z):
            fn(start, z)


def _plan_kernel(cp_ref, off_ref, lb_ref, foff_ref, flen_ref, blk_ref, nused_ref):
    cp = cp_ref[...].astype(F32)
    n, ne = cp.shape
    ei = lax.broadcasted_iota(I32, (ne, ne), 0)
    ej = lax.broadcasted_iota(I32, (ne, ne), 1)
    si = lax.broadcasted_iota(I32, (n, n), 0)
    sj = lax.broadcasted_iota(I32, (n, n), 1)
    lb = _dot(cp, (ei < ej).astype(F32), HIGHEST)
    earlier_tiles = _dot((sj < si).astype(F32), cp, HIGHEST)
    rows_e = jnp.sum(cp, axis=0, keepdims=True)
    region = jnp.floor((rows_e + (ROW_BLOCK - 1)) * (1.0 / ROW_BLOCK)) * ROW_BLOCK
    region_b = jnp.broadcast_to(region, (ne, ne))
    rend_row = _dot(region_b, (ei <= ej).astype(F32), HIGHEST)[:1]
    rend_col = jnp.sum(jnp.where(ej <= ei, region_b, 0.0), axis=1, keepdims=True)
    base = rend_row - region
    total = jnp.max(rend_row, axis=1, keepdims=True)
    off_ref[...] = (base + earlier_tiles).astype(I32)
    lb_ref[...] = lb.astype(I32)
    lane = lax.broadcasted_iota(I32, (1, GATE_W), 1)
    pad = jnp.zeros((1, GATE_W - ne), F32)
    foff_ref[...] = jnp.where(lane == ne, total, jnp.concatenate([base + rows_e, pad], axis=1)).astype(I32)
    flen_ref[...] = jnp.concatenate([region - rows_e, pad], axis=1).astype(I32)
    n_used = total * (1.0 / ROW_BLOCK)
    nused_ref[...] = jnp.broadcast_to(n_used, nused_ref.shape).astype(I32)
    bi = lax.broadcasted_iota(I32, (ne, blk_ref.shape[1]), 1).astype(F32)
    ended = jnp.where(rend_col * (1.0 / ROW_BLOCK) <= jnp.minimum(bi, n_used - 1.0), 1.0, 0.0)
    blk_ref[...] = jnp.minimum(jnp.sum(ended, axis=0, keepdims=True), ne - 1.0).astype(I32)


def _plan(cp, n_blocks):
    n, ne = cp.shape
    nb_pad = -(-n_blocks // 128) * 128
    return pl.pallas_call(
        _plan_kernel,
        out_shape=[jax.ShapeDtypeStruct((n, ne), I32), jax.ShapeDtypeStruct((n, ne), I32),
                   jax.ShapeDtypeStruct((1, GATE_W), I32), jax.ShapeDtypeStruct((1, GATE_W), I32),
                   jax.ShapeDtypeStruct((1, nb_pad), I32), jax.ShapeDtypeStruct((1, GATE_W), I32)],
        name="moe_plan",
    )(cp)


def _dispatch_kernel(off_ref, cp_ref, lb_ref, foff_ref, flen_ref, h_ref, posrow_ref, xs_hbm, buf, zbuf, sem,
                     zsem, *, n_steps, tile):
    s = pl.program_id(0)
    slot = s % 2
    jmax = buf.shape[1]

    def run_copies(step, slot_, act):
        def body(e, carry):
            idx = step * N_EXPERTS + e
            lb = lb_ref[idx]
            of = off_ref[idx]

            def piece(start, z):
                act(pltpu.make_async_copy(
                    buf.at[slot_, pl.ds(pl.multiple_of(lb + start, RUN_ALIGN), z)],
                    xs_hbm.at[pl.ds(pl.multiple_of(of + start, RUN_ALIGN), z)], sem.at[slot_]))

            _for_run_pieces(cp_ref[idx], tile, piece)
            return carry

        lax.fori_loop(0, N_EXPERTS, body, 0)

    def wait_runs(step, slot_):
        last_ = step * N_EXPERTS + N_EXPERTS - 1
        _for_run_pieces(lb_ref[last_] + cp_ref[last_], jmax, lambda start, z: pltpu.make_async_copy(
            buf.at[slot_, pl.ds(0, z)], xs_hbm.at[pl.ds(0, z)], sem.at[slot_]).wait())

    def fill_copies(act):
        def body(e, carry):
            fo = foff_ref[e]

            def piece(start, z):
                act(pltpu.make_async_copy(
                    zbuf.at[pl.ds(0, z)], xs_hbm.at[pl.ds(pl.multiple_of(fo + start, RUN_ALIGN), z)], zsem.at[0]))

            _for_run_pieces(flen_ref[e], ROW_BLOCK // 2, piece)
            return carry

        lax.fori_loop(0, N_EXPERTS, body, 0)

        def tail(r, carry):
            act(pltpu.make_async_copy(
                zbuf, xs_hbm.at[pl.ds(pl.multiple_of(foff_ref[N_EXPERTS] + r * zbuf.shape[0], RUN_ALIGN),
                                      zbuf.shape[0])], zsem.at[0]))
            return carry

        lax.fori_loop(0, (xs_hbm.shape[0] - foff_ref[N_EXPERTS]) // zbuf.shape[0], tail, 0)

    @pl.when(s == 0)
    def _():
        zbuf[...] = jnp.zeros(zbuf.shape, BF16)
        fill_copies(lambda c: c.start())

    @pl.when(s >= 2)
    def _():
        wait_runs(s - 2, slot)

    h = h_ref[...]
    last = s * N_EXPERTS + N_EXPERTS - 1
    jused = lb_ref[last] + cp_ref[last]
    def local_rows(jc):
        ji = (lax.broadcasted_iota(I32, (MOE_CHUNK, tile), 0) + jc * MOE_CHUNK).astype(jnp.int16)
        p = jnp.zeros((MOE_CHUNK, tile), BF16)
        for k in range(TOP_K):
            p = jnp.where(ji == posrow_ref[0, k:k + 1, :].astype(jnp.int16), jnp.ones((), BF16), p)
        return _dot(p, h).astype(BF16)

    n_full = TOP_K * tile // MOE_CHUNK
    rows_full = [local_rows(jc) for jc in range(n_full)]
    for jc in range(n_full):
        buf[slot, jc * MOE_CHUNK:(jc + 1) * MOE_CHUNK, :] = rows_full[jc]
    for jc in range(n_full, jmax // MOE_CHUNK):
        @pl.when(jc * MOE_CHUNK < jused)
        def _(jc=jc):
            buf[slot, jc * MOE_CHUNK:(jc + 1) * MOE_CHUNK, :] = local_rows(jc)

    run_copies(s, slot, lambda c: c.start())

    @pl.when(s == n_steps - 1)
    def _():
        if n_steps >= 2:
            wait_runs(s - 1, 1 - slot)
        wait_runs(s, slot)
        fill_copies(lambda c: c.wait())


def _dispatch(h2, posrow, off, cp, lb, foff, flen, cap, tile, jmax):
    n_tok, D = h2.shape
    n_steps = n_tok // tile
    return pl.pallas_call(
        functools.partial(_dispatch_kernel, n_steps=n_steps, tile=tile),
        grid_spec=pltpu.PrefetchScalarGridSpec(
            num_scalar_prefetch=5,
            grid=(n_steps,),
            in_specs=[pl.BlockSpec((tile, D), lambda s, *_: (s, 0)),
                      pl.BlockSpec((1, TOP_K, tile), lambda s, *_: (s, 0, 0))],
            out_specs=pl.BlockSpec(memory_space=pl.ANY),
            scratch_shapes=[pltpu.VMEM((2, jmax, D), BF16), pltpu.VMEM((ROW_BLOCK // 2, D), BF16),
                            pltpu.SemaphoreType.DMA((2,)), pltpu.SemaphoreType.DMA((1,))]),
        out_shape=jax.ShapeDtypeStruct((cap, D), BF16),
        compiler_params=pltpu.CompilerParams(dimension_semantics=("arbitrary",), vmem_limit_bytes=VMEM_LIMIT),
        name="moe_dispatch",
    )(off, cp, lb, foff, flen, h2, posrow)


def _expert_kernel(blk_e_ref, nused_ref, xs_ref, wg_ref, wu_ref, wd_ref, ys_ref):
    used = pl.program_id(0) < nused_ref[0]

    @pl.when(used)
    def _():
        xb = xs_ref[...]
        a = _silu(_dot(xb, wg_ref[0])) * _dot(xb, wu_ref[0])
        ys_ref[...] = _dot(a.astype(BF16), wd_ref[0]).astype(BF16)

    @pl.when(jnp.logical_not(used))
    def _():
        ys_ref[...] = jnp.zeros(ys_ref.shape, BF16)


def _experts(xs, blk_e, n_used, wg, wu, wd):
    cap, D = xs.shape

    def row_block(i, be, nu):
        return (jnp.minimum(i, nu[0] - 1), 0)

    def out_block(i, be, nu):
        return (i, 0)

    def weight(i, be, nu):
        return (be[i], 0, 0)

    return pl.pallas_call(
        _expert_kernel,
        grid_spec=pltpu.PrefetchScalarGridSpec(
            num_scalar_prefetch=2,
            grid=(cap // ROW_BLOCK,),
            in_specs=[pl.BlockSpec((ROW_BLOCK, D), row_block),
                      pl.BlockSpec((1, D, D_EXPERT), weight), pl.BlockSpec((1, D, D_EXPERT), weight),
                      pl.BlockSpec((1, D_EXPERT, D), weight)],
            out_specs=pl.BlockSpec((ROW_BLOCK, D), out_block)),
        out_shape=jax.ShapeDtypeStruct((cap, D), BF16),
        compiler_params=pltpu.CompilerParams(dimension_semantics=("arbitrary",), vmem_limit_bytes=VMEM_LIMIT),
        name="moe_experts",
    )(blk_e, n_used, xs, wg, wu, wd)


def _combine_kernel(off_ref, cp_ref, lb_ref, ys_hbm, poscol_ref, gatecol_ref, h_ref, sg_ref, su_ref, sd_ref,
                    x1_ref, gt_ref, fg_ref, o_ref, buf, sem, acc_ref, *, n_steps, tile, final_norm):
    s = pl.program_id(0)
    slot = s % 2
    jmax = buf.shape[1]

    def run_copies(step, slot_, act):
        def body(e, carry):
            idx = step * N_EXPERTS + e
            lb = lb_ref[idx]
            of = off_ref[idx]

            def piece(start, z):
                act(pltpu.make_async_copy(
                    ys_hbm.at[pl.ds(pl.multiple_of(of + start, RUN_ALIGN), z)],
                    buf.at[slot_, pl.ds(pl.multiple_of(lb + start, RUN_ALIGN), z)], sem.at[slot_]))

            _for_run_pieces(cp_ref[idx], tile, piece)
            return carry

        lax.fori_loop(0, N_EXPERTS, body, 0)

    @pl.when(s == 0)
    def _():
        run_copies(0, 0, lambda c: c.start())

    @pl.when(s + 1 < n_steps)
    def _():
        run_copies(s + 1, 1 - slot, lambda c: c.start())

    hb = h_ref[...]
    shared = (_silu(_dot(hb, sg_ref[...])) * _dot(hb, su_ref[...])).astype(BF16)
    acc_ref[...] = _dot(shared, sd_ref[...])

    last = s * N_EXPERTS + N_EXPERTS - 1
    jused = lb_ref[last] + cp_ref[last]
    _for_run_pieces(jused, jmax, lambda start, z: pltpu.make_async_copy(
        ys_hbm.at[pl.ds(0, z)], buf.at[slot, pl.ds(0, z)], sem.at[slot]).wait())

    def zero_body(r, carry):
        buf[slot, pl.ds(pl.multiple_of(jused + r * RUN_ALIGN, RUN_ALIGN), RUN_ALIGN), :] = jnp.zeros(
            (RUN_ALIGN, buf.shape[2]), BF16)
        return carry

    chunk_end = (jused + MOE_CHUNK - 1) // MOE_CHUNK * MOE_CHUNK
    lax.fori_loop(0, (chunk_end - jused) // RUN_ALIGN, zero_body, 0)

    def gate_rows(jc):
        ji = (lax.broadcasted_iota(I32, (tile, MOE_CHUNK), 1) + jc * MOE_CHUNK).astype(jnp.int16)
        g = jnp.zeros((tile, MOE_CHUNK), BF16)
        for k in range(TOP_K):
            g = jnp.where(ji == poscol_ref[:, k:k + 1].astype(jnp.int16), gatecol_ref[:, k:k + 1].astype(BF16), g)
        return g

    n_full = TOP_K * tile // MOE_CHUNK
    g_full = jnp.concatenate([gate_rows(jc) for jc in range(n_full)], axis=1)
    acc_ref[...] += _dot(g_full, buf[slot, 0:n_full * MOE_CHUNK, :])
    for jc in range(n_full, jmax // MOE_CHUNK):
        @pl.when(jc * MOE_CHUNK < jused)
        def _(jc=jc):
            acc_ref[...] += _dot(gate_rows(jc), buf[slot, jc * MOE_CHUNK:(jc + 1) * MOE_CHUNK, :])

    y = x1_ref[...] + gt_ref[0] * acc_ref[...]
    if final_norm:
        y = y * lax.rsqrt(jnp.mean(y * y, axis=-1, keepdims=True) + EPS) * fg_ref[...]
    o_ref[...] = y


def _combine(ys, poscol, gatecol, h2, sg, su, sd, x1, gt2, fg, off, cp, lb, tile, jmax, tiles_per_batch,
             final_norm):
    n_tok, D = h2.shape
    n_steps = n_tok // tile

    def full(a):
        nd = a.ndim
        return pl.BlockSpec(a.shape, lambda s, *_, _n=nd: (0,) * _n)

    def rows(w):
        return pl.BlockSpec((tile, w), lambda s, *_: (s, 0))

    return pl.pallas_call(
        functools.partial(_combine_kernel, n_steps=n_steps, tile=tile, final_norm=final_norm),
        grid_spec=pltpu.PrefetchScalarGridSpec(
            num_scalar_prefetch=3,
            grid=(n_steps,),
            in_specs=[pl.BlockSpec(memory_space=pl.ANY), rows(GATE_W), rows(GATE_W), rows(D),
                      full(sg), full(su), full(sd), rows(D),
                      pl.BlockSpec((1, 1, D), lambda s, *_: (s // tiles_per_batch, 0, 0)), full(fg)],
            out_specs=rows(D),
            scratch_shapes=[pltpu.VMEM((2, jmax, D), BF16), pltpu.SemaphoreType.DMA((2,)),
                            pltpu.VMEM((tile, D), F32)]),
        out_shape=jax.ShapeDtypeStruct((n_tok, D), F32),
        compiler_params=pltpu.CompilerParams(dimension_semantics=("arbitrary",), vmem_limit_bytes=VMEM_LIMIT),
        name="moe_combine",
    )(off, cp, lb, ys, poscol, gatecol, h2, sg, su, sd, x1, gt2, fg)


def _moe(h2, posrow, poscol, gatecol, cpad, wg, wu, wd, sg, su, sd, x1, gt2, fg, tile, final_norm):
    B, T, D = x1.shape
    n_tok = B * T
    n_tiles = n_tok // tile
    jmax = -(-(TOP_K * tile + N_EXPERTS * (RUN_ALIGN - 1)) // MOE_CHUNK) * MOE_CHUNK
    cap = -(-(TOP_K * n_tok + n_tiles * N_EXPERTS * (RUN_ALIGN - 1) + N_EXPERTS * (ROW_BLOCK - RUN_ALIGN))
            // ROW_BLOCK) * ROW_BLOCK

    cp = cpad[..., 0].reshape(n_tiles, N_EXPERTS)
    off, lb, foff, flen, blk_e, n_used = _plan(cp, cap // ROW_BLOCK)
    flat = lambda a: a.reshape(-1)

    xs = _dispatch(h2.reshape(n_tok, D), posrow.reshape(n_tiles, TOP_K, tile), flat(off), flat(cp), flat(lb),
                   foff[0], flen[0], cap, tile, jmax)
    ys = _experts(xs, blk_e[0], n_used[0], wg, wu, wd)
    out = _combine(ys, poscol.reshape(n_tok, GATE_W), gatecol.reshape(n_tok, GATE_W), h2.reshape(n_tok, D),
                   sg, su, sd, x1.reshape(n_tok, D), gt2, fg, flat(off), flat(cp), flat(lb), tile, jmax,
                   T // tile, final_norm)
    return out.reshape(B, T, D)


def _misc_lanes(vec, start):
    return jnp.zeros((1, MISC_W), F32).at[0, start:start + vec.shape[0]].set(vec.astype(F32))


def kernel(x, c, ada_w, ada_b, norm1_g, w_in, conv_w, a_log, dt_bias, dn_norm_g, kv_norm_g, w_uk, w_uv,
           idx_k_ln_g, idx_k_ln_b, w_out, norm2_g, router_w, router_b, exp_w_gate, exp_w_up, exp_w_down,
           sh_w_gate, sh_w_up, sh_w_down, final_g):
    B, T, D = x.shape
    depth = ada_w.shape[0]
    topk = min(IDX_TOPK_MAX, T // 4)
    tm = min(512, T)
    r_dn = min(256, T)

    cond_in = jnp.zeros((8, D), F32).at[:B].set(c)
    pos = jnp.arange(tm)
    tri = ((pos[:, None] // CHUNK == pos[None, :] // CHUNK) & (pos[:, None] >= pos[None, :])).astype(F32)

    for l in range(depth):
        mod = _ada(cond_in, ada_w[l], ada_b[l][None, :])[:B]
        sh1, sc1, gt1, sh2, sc2, gt2 = [m[:, None, :] for m in jnp.split(mod, 6, axis=-1)]

        offs = [0]
        for s in (DN_QK, DN_QK, DN_V, DN_V, DN_HEADS, DN_HEADS, SA_Q, KV_RANK, IDX_Q, IDX_DIM, IDX_HEADS):
            offs.append(offs[-1] + s)
        w = w_in[l]
        wc = w[:, offs[0]:offs[3]].astype(BF16)
        wz = w[:, offs[3]:offs[4]].astype(BF16)
        wq = w[:, offs[6]:offs[7]].astype(BF16)
        wkv = w[:, offs[7]:offs[8]].astype(BF16)
        wqi = w[:, offs[8]:offs[9]].astype(BF16)
        wm = jnp.concatenate([w[:, offs[9]:offs[10]], w[:, offs[4]:offs[5]], w[:, offs[5]:offs[6]],
                              w[:, offs[10]:offs[11]],
                              jnp.zeros((D, MISC_W - IDX_DIM - 2 * DN_HEADS - IDX_HEADS), F32)],
                             axis=1).astype(BF16)
        ukt = jnp.swapaxes(w_uk[l], 1, 2).astype(BF16)

        q, k, v, z, qlat, ckv, qix, kix, misc = _inproj(
            x, sc1, sh1, norm1_g[l][None, :], wc, wz, wq, wkv, wqi, wm, conv_w[l], ukt,
            kv_norm_g[l][None, :], _misc_lanes(idx_k_ln_g[l], M_KIX), _misc_lanes(idx_k_ln_b[l], M_KIX),
            _misc_lanes(a_log[l], M_A), _misc_lanes(dt_bias[l], M_A), tri, tm)

        odn = _deltanet(q, k, v, z, misc, dn_norm_g[l][None, :], r_dn)
        olat = _dsa(qix, misc, kix, qlat, ckv, topk)

        x1, h2, posrow, poscol, gatecol, cpad = _outproj(
            x, odn, olat, w_uv[l].astype(BF16), w_out[l].astype(BF16), gt1, sc2, sh2,
            norm2_g[l][None, :], router_w[l].T, router_b[l][:, None], tm)

        x = _moe(h2, posrow, poscol, gatecol, cpad, exp_w_gate[l].astype(BF16), exp_w_up[l].astype(BF16),
                 exp_w_down[l].astype(BF16), sh_w_gate[l].astype(BF16), sh_w_up[l].astype(BF16),
                 sh_w_down[l].astype(BF16), x1, gt2, final_g[None, :], tm, l == depth - 1)
    return x
```

```python
import functools

import jax
import jax.numpy as jnp
from jax import lax
from jax.experimental import pallas as pl
from jax.experimental.pallas import tpu as pltpu

F32 = jnp.float32
BF16 = jnp.bfloat16
I32 = jnp.int32
HIGHEST = lax.Precision.HIGHEST

EPS = 1e-6
CHUNK = 64
DN_HEADS = 4
DN_DK = 128
DN_DV = 128
CONV_K = 4
SA_HEADS = 4
SA_DQK = 128
SA_DV = 128
KV_RANK = 256
IDX_HEADS = 4
IDX_DIM = 64
IDX_TOPK_MAX = 256
SM_SCALE = SA_DQK ** -0.5
LOG2E = 1.4426950408889634
IDX_W_SCALE = (IDX_HEADS * IDX_DIM) ** -0.5
N_EXPERTS = 64
TOP_K = 8
N_GROUPS = 8
TOPK_GROUPS = 4
D_EXPERT = 256
ROUTED_SCALE = 2.5
GATE_W = 128
RUN_ALIGN = 16
ROW_BLOCK = 1024
MOE_CHUNK = 512

DN_QK = DN_HEADS * DN_DK
DN_V = DN_HEADS * DN_DV
CONV_DIM = 2 * DN_QK + DN_V
SA_Q = SA_HEADS * SA_DQK
IDX_Q = IDX_HEADS * IDX_DIM

MISC_W = 128
M_KIX = 0
M_BETA = IDX_DIM
M_A = M_BETA + DN_HEADS
M_WIX = M_A + DN_HEADS

DN_SUB = 2 * CHUNK
QBLOCK = 256
SOFTMAX_TINY = 2.0 ** -100
KEY_TILE = 1024
PLANE_COLS = 32 * 128
PLANE_SHIFT = 12
PLANE_ROWS = 16
INT_MIN = -2 ** 31
NEG_BIG = -1e30
VMEM_LIMIT = 56 * 1024 * 1024


def _nt_dot(a, b, precision=None):
    return lax.dot_general(a, b, (((1,), (1,)), ((), ())), preferred_element_type=F32,
                           precision=precision)


def _dot(a, b, precision=None):
    return jnp.dot(a, b, preferred_element_type=F32, precision=precision)


def _silu(x):
    return x * jax.nn.sigmoid(x)


def _softplus(x):
    return jnp.maximum(x, 0.0) + jnp.log(1.0 + jnp.exp(-jnp.abs(x)))


def _ada_kernel(c_ref, w_ref, b_ref, o_ref):
    cond = _silu(c_ref[...])
    o_ref[...] = _dot(cond, w_ref[...], HIGHEST) + b_ref[...]


def _ada(c_pad, ada_w, ada_b):
    rows, d = c_pad.shape
    n_out = ada_w.shape[1]
    return pl.pallas_call(
        _ada_kernel,
        grid=(n_out // d,),
        in_specs=[pl.BlockSpec((rows, d), lambda j: (0, 0)),
                  pl.BlockSpec((d, d), lambda j: (0, j)),
                  pl.BlockSpec((1, d), lambda j: (0, j))],
        out_specs=pl.BlockSpec((rows, d), lambda j: (0, j)),
        out_shape=jax.ShapeDtypeStruct((rows, n_out), F32),
        compiler_params=pltpu.CompilerParams(vmem_limit_bytes=VMEM_LIMIT),
        name="ada",
    )(c_pad, ada_w, ada_b)


def _inproj_kernel(x_ref, sc_ref, sh_ref, g1_ref, wc_ref, wz_ref, wq_ref, wkv_ref, wqi_ref, wm_ref,
                   convw_ref, ukt_ref, kvg_ref, lng_ref, lnb_ref, alog_ref, dtb_ref, tri_ref,
                   q_ref, k_ref, v_ref, z_ref, qlat_ref, ckv_ref, qix_ref, kix_ref, misc_ref,
                   conv_buf):
    tm = x_ref.shape[1]
    i = pl.program_id(1)

    x = x_ref[0]
    h = x * lax.rsqrt(jnp.mean(x * x, axis=-1, keepdims=True) + EPS) * g1_ref[...]
    h = h * (1.0 + sc_ref[0]) + sh_ref[0]
    hb = h.astype(BF16)

    @pl.when(i == 0)
    def _():
        conv_buf[0:8, :] = jnp.zeros((8, CONV_DIM), F32)

    conv_buf[8:8 + tm, :] = _dot(hb, wc_ref[...])
    for grp, dst in ((0, q_ref), (1, k_ref), (2, v_ref)):
        cols = slice(grp * DN_QK, (grp + 1) * DN_QK)
        y = jnp.zeros((tm, DN_QK), F32)
        for j in range(CONV_K):
            y = y + convw_ref[j:j + 1, cols] * conv_buf[8 - (CONV_K - 1) + j:8 - (CONV_K - 1) + j + tm, cols]
        y = _silu(y)
        if grp < 2:
            outs = []
            for hd in range(DN_HEADS):
                yh = y[:, hd * DN_DK:(hd + 1) * DN_DK]
                yh = yh * lax.rsqrt(jnp.sum(yh * yh, axis=-1, keepdims=True) + EPS)
                if grp == 0:
                    yh = yh * (DN_DK ** -0.5)
                outs.append(yh)
            y = jnp.concatenate(outs, axis=-1)
        dst[0] = y
    conv_buf[0:8, :] = conv_buf[tm:tm + 8, :]

    z_ref[0] = _dot(hb, wz_ref[...])

    q_sa = _dot(hb, wq_ref[...]).astype(BF16)
    for hd in range(SA_HEADS):
        ql = _dot(q_sa[:, hd * SA_DQK:(hd + 1) * SA_DQK], ukt_ref[hd]) * (SM_SCALE * LOG2E)
        qlat_ref[0, hd] = ql.astype(BF16)

    ckv = _dot(hb, wkv_ref[...])
    ckv = ckv * lax.rsqrt(jnp.mean(ckv * ckv, axis=-1, keepdims=True) + EPS) * kvg_ref[...]
    ckv_ref[0] = ckv.astype(BF16)

    q_ix = _dot(hb, wqi_ref[...]).astype(BF16)
    for hd in range(IDX_HEADS):
        qix_ref[0, hd] = q_ix[:, hd * IDX_DIM:(hd + 1) * IDX_DIM]

    m = _dot(hb, wm_ref[...])
    lane = lax.broadcasted_iota(I32, (tm, MISC_W), 1)
    is_k = lane < IDX_DIM
    mu = jnp.sum(jnp.where(is_k, m, 0.0), axis=-1, keepdims=True) * (1.0 / IDX_DIM)
    kc = jnp.where(is_k, m - mu, 0.0)
    var = jnp.sum(kc * kc, axis=-1, keepdims=True) * (1.0 / IDX_DIM)
    kn = kc * lax.rsqrt(var + EPS) * lng_ref[...] + lnb_ref[...]
    kix_ref[0] = kn[:, :IDX_DIM].astype(BF16)

    beta = jax.nn.sigmoid(m)
    g = -jnp.exp(alog_ref[...]) * _softplus(m + dtb_ref[...])
    is_a = (lane >= M_A) & (lane < M_A + DN_HEADS)
    g = jnp.where(is_a, g, 0.0)
    gc = _dot(tri_ref[...], g, HIGHEST)
    is_b = (lane >= M_BETA) & (lane < M_BETA + DN_HEADS)
    is_w = (lane >= M_WIX) & (lane < M_WIX + IDX_HEADS)
    misc_ref[0] = jnp.where(is_b, beta, jnp.where(is_a, gc, jnp.where(is_w, m * IDX_W_SCALE, 0.0)))


def _inproj(x, sc1, sh1, g1, wc, wz, wq, wkv, wqi, wm, conv_w, ukt, kvg, lng, lnb, alog, dtb, tri, tm):
    B, T, D = x.shape
    nt = T // tm

    def full(a):
        nd = a.ndim
        return pl.BlockSpec(a.shape, lambda b, i, _n=nd: (0,) * _n)

    def rows(w):
        return pl.BlockSpec((1, tm, w), lambda b, i: (b, i, 0))

    per_b = pl.BlockSpec((1, 1, D), lambda b, i: (b, 0, 0))
    def head_rows(h, w):
        return pl.BlockSpec((1, h, tm, w), lambda b, i: (b, 0, i, 0))

    outs = [(None, DN_QK, F32), (None, DN_QK, F32), (None, DN_V, F32), (None, DN_V, F32),
            (SA_HEADS, KV_RANK, BF16), (None, KV_RANK, BF16), (IDX_HEADS, IDX_DIM, BF16),
            (None, IDX_DIM, BF16), (None, MISC_W, F32)]
    return pl.pallas_call(
        _inproj_kernel,
        grid=(B, nt),
        in_specs=[rows(D), per_b, per_b, full(g1), full(wc), full(wz), full(wq), full(wkv), full(wqi),
                  full(wm), full(conv_w), full(ukt), full(kvg), full(lng), full(lnb), full(alog),
                  full(dtb), full(tri)],
        out_specs=[rows(w) if h is None else head_rows(h, w) for h, w, _ in outs],
        out_shape=[jax.ShapeDtypeStruct((B, T, w) if h is None else (B, h, T, w), dt) for h, w, dt in outs],
        scratch_shapes=[pltpu.VMEM((tm + 8, CONV_DIM), F32)],
        compiler_params=pltpu.CompilerParams(dimension_semantics=("arbitrary", "arbitrary"),
                                             vmem_limit_bytes=VMEM_LIMIT),
        name="inproj",
    )(x, sc1, sh1, g1, wc, wz, wq, wkv, wqi, wm, conv_w, ukt, kvg, lng, lnb, alog, dtb, tri)


def _deltanet_kernel(q_ref, k_ref, v_ref, z_ref, misc_ref, ng_ref, o_ref, s_ref):
    R = q_ref.shape[1]
    n_chunks = R // CHUNK

    @pl.when(pl.program_id(1) == 0)
    def _():
        s_ref[...] = jnp.zeros(s_ref.shape, F32)

    misc = misc_ref[0]
    misc_t = misc.T
    SB = min(DN_SUB, R)
    row = lax.broadcasted_iota(I32, (SB, SB), 0)
    col = lax.broadcasted_iota(I32, (SB, SB), 1)
    same = (row // CHUNK) == (col // CHUNK)
    lower = same & (row >= col)
    strict = same & (row > col)
    eye = (row == col).astype(F32)

    def mm(a, b):
        return _dot(a.astype(BF16), b.astype(BF16))

    def mm3(a, b):
        ah = a.astype(BF16)
        bh = b.astype(BF16)
        al = (a - ah.astype(F32)).astype(BF16)
        bl = (b - bh.astype(F32)).astype(BF16)
        return _dot(jnp.concatenate([ah, ah, al], axis=1), jnp.concatenate([bh, bl, bh], axis=0))

    heads = range(DN_HEADS)
    subs = range(R // SB)
    chains = [(hd, sb) for hd in heads for sb in subs]
    cols = [slice(hd * DN_DK, (hd + 1) * DN_DK) for hd in heads]
    qh = [q_ref[0, :, cols[hd]] for hd in heads]
    kh = [k_ref[0, :, cols[hd]] for hd in heads]
    beta = [misc[:, M_BETA + hd:M_BETA + hd + 1] for hd in heads]
    gc_c = [misc[:, M_A + hd:M_A + hd + 1] for hd in heads]
    eg = [jnp.exp(gc_c[hd]) for hd in heads]
    kb = [kh[hd] * beta[hd] for hd in heads]
    rhs = [jnp.concatenate([v_ref[0, :, cols[hd]] * beta[hd], kb[hd] * eg[hd]], axis=-1) for hd in heads]
    q_dec = [qh[hd] * eg[hd] for hd in heads]

    def rows_of(sb):
        return slice(sb * SB, (sb + 1) * SB)

    decay, a, qk_sb = {}, {}, {}
    for hd, sb in chains:
        bs = rows_of(sb)
        gc_r = misc_t[M_A + hd:M_A + hd + 1, bs]
        decay[hd, sb] = jnp.where(lower, jnp.exp(jnp.where(lower, gc_c[hd][bs] - gc_r, 0.0)), 0.0)
    for hd, sb in chains:
        bs = rows_of(sb)
        khb = kh[hd][bs].astype(BF16)
        a[hd, sb] = jnp.where(strict, _nt_dot(kb[hd][bs].astype(BF16), khb) * decay[hd, sb], 0.0)
        qk_sb[hd, sb] = jnp.where(lower, _nt_dot(qh[hd][bs].astype(BF16), khb) * decay[hd, sb], 0.0)
    p = {ch: eye - a[ch] for ch in chains}
    xp = {ch: mm3(a[ch], a[ch]) for ch in chains}
    n_sq = 1
    while True:
        p = {ch: p[ch] + mm3(p[ch], xp[ch]) for ch in chains}
        n_sq *= 2
        if n_sq * 2 >= CHUNK:
            break
        xp = {ch: mm3(xp[ch], xp[ch]) for ch in chains}
    sol = {(hd, sb): mm(p[hd, sb], rhs[hd][rows_of(sb)]) for hd, sb in chains}

    def chunk_of(c):
        per = SB // CHUNK
        return c // per, slice((c % per) * CHUNK, (c % per + 1) * CHUNK)

    s = [s_ref[hd] for hd in heads]
    o_parts = [[] for _ in heads]
    for c in range(n_chunks):
        rs = slice(c * CHUNK, (c + 1) * CHUNK)
        sb, r = chunk_of(c)
        gl = [gc_c[hd][(c + 1) * CHUNK - 1:(c + 1) * CHUNK, :] for hd in heads]
        k_dec = [kh[hd][rs] * jnp.exp(gl[hd] - gc_c[hd][rs]) for hd in heads]
        v_new = [sol[hd, sb][r, :DN_DV] - mm(sol[hd, sb][r, DN_DV:], s[hd]) for hd in heads]
        for hd in heads:
            o_parts[hd].append(mm(q_dec[hd][rs], s[hd]) + mm(qk_sb[hd, sb][r, r], v_new[hd]))
        s = [s[hd] * jnp.exp(gl[hd]) + mm(k_dec[hd].T, v_new[hd]) for hd in heads]
    for hd in heads:
        s_ref[hd] = s[hd]
        o = jnp.concatenate(o_parts[hd], axis=0)
        o = o * lax.rsqrt(jnp.mean(o * o, axis=-1, keepdims=True) + EPS) * ng_ref[...]
        o_ref[0, :, cols[hd]] = (o * _silu(z_ref[0, :, cols[hd]])).astype(BF16)


def _deltanet(q, k, v, z, misc, ng, R):
    B, T, _ = q.shape

    def rows(w):
        return pl.BlockSpec((1, R, w), lambda b, i: (b, i, 0))

    return pl.pallas_call(
        _deltanet_kernel,
        grid=(B, T // R),
        in_specs=[rows(DN_QK), rows(DN_QK), rows(DN_V), rows(DN_V), rows(MISC_W),
                  pl.BlockSpec((1, DN_DV), lambda b, i: (0, 0))],
        out_specs=rows(DN_V),
        out_shape=jax.ShapeDtypeStruct((B, T, DN_V), BF16),
        scratch_shapes=[pltpu.VMEM((DN_HEADS, DN_DK, DN_DV), F32)],
        compiler_params=pltpu.CompilerParams(dimension_semantics=("arbitrary", "arbitrary"),
                                             vmem_limit_bytes=VMEM_LIMIT),
        name="deltanet",
    )(q, k, v, z, misc, ng)


def _bit_transpose32(words):
    w = list(words)
    j = 16
    m = 0x0000FFFF
    while j:
        k = 0
        m_i32 = jnp.int32(m - (1 << 32) if m >= (1 << 31) else m)
        while k < 32:
            t = (w[k] ^ lax.shift_right_logical(w[k + j], jnp.full_like(w[k], j))) & m_i32
            w[k] = w[k] ^ t
            w[k + j] = w[k + j] ^ jnp.left_shift(t, j)
            k = (k + j + 1) & ~j
        j >>= 1
        m = (m ^ (m << j)) & 0xFFFFFFFF
    return w


def _dsa_kernel(qix_ref, misc_ref, kix_ref, qlat_ref, ckv_ref, o_ref, keys_ref, planes_ref, bias_ref,
                mx_ref, l_ref, acc_ref, kvmax_ref, qprev_ref, *, topk, pos_bits, n_cg_max):
    g = pl.program_id(1)
    n_blocks = pl.num_programs(1) - 1
    i = g
    QB = QBLOCK
    KT = KEY_TILE
    has_front = g < n_blocks
    has_back = g >= 1
    n_kt = jnp.where(has_front, (i * QB + QB + KT - 1) // KT, 0)
    n_kt_back = (g * QB + KT - 1) // KT

    rowi = lax.broadcasted_iota(I32, (QB, KT), 0)
    coli = lax.broadcasted_iota(I32, (QB, KT), 1)
    limit = i * QB + (rowi // CHUNK + 1) * CHUNK

    misc = misc_ref[0]
    qix = qix_ref[0].reshape(IDX_HEADS * QB, IDX_DIM)
    q_st = qlat_ref[0].reshape(SA_HEADS * QB, KV_RANK)

    def score_body(masked, kt, carry):
        k0 = pl.multiple_of(kt * KT, KT)
        kx = kix_ref[0, pl.ds(k0, KT), :]
        rel = jnp.maximum(_nt_dot(qix, kx), 0.0)
        sc = jnp.zeros((QB, KT), F32)
        for hd in range(IDX_HEADS):
            sc = sc + misc[:, M_WIX + hd:M_WIX + hd + 1] * rel[hd * QB:(hd + 1) * QB]
        sc = jnp.where(sc == 0.0, 0.0, sc)
        bits = pltpu.bitcast(sc, I32)
        key = jnp.where(bits < 0, bits ^ 0x7FFFFFFF, bits)
        keys_ref[:, pl.ds(k0, KT)] = jnp.where(k0 + coli < limit, key, INT_MIN) if masked else key
        return carry

    def logit_chunks(k0):
        kv = ckv_ref[0, pl.ds(k0, KT), :]
        s = _nt_dot(qprev_ref[...], kv)
        bias = bias_ref[:, pl.ds(k0, KT)]
        chunks = []
        for j in range(KT // 128):
            bj = bias[:, j * 128:(j + 1) * 128]
            chunks.append(s[:, j * 128:(j + 1) * 128] + jnp.concatenate([bj] * SA_HEADS, axis=0))
        return kv, chunks

    def pv_body(kt, carry):
        kv, chunks = logit_chunks(pl.multiple_of(kt * KT, KT))
        shift = mx_ref[...]
        ps = [jnp.exp2(c - shift) for c in chunks]
        l_ref[...] = functools.reduce(jnp.add, ps, l_ref[...])
        p = jnp.concatenate([pj.astype(BF16) for pj in ps], axis=1)
        acc_ref[...] += _dot(p, kv)
        return carry

    def clear_sums():
        l_ref[...] = jnp.zeros(l_ref.shape, F32)
        acc_ref[...] = jnp.zeros(acc_ref.shape, F32)

    def write_back_block():
        l_row = jnp.sum(l_ref[...], axis=-1, keepdims=True)
        o_ref[0] = (acc_ref[...] / l_row).astype(BF16).reshape(SA_HEADS, QB, KV_RANK)
        return l_row

    pl.when(has_back)(clear_sums)

    n_open = jnp.where(has_front, (i * QB + CHUNK) // KT, 0)
    n_fused = jnp.minimum(n_open, n_kt_back)

    def fused_body(kt, carry):
        pv_body(kt, carry)
        return score_body(False, kt, carry)

    lax.fori_loop(0, n_fused, fused_body, 0)
    lax.fori_loop(n_fused, n_kt_back, pv_body, 0)
    lax.fori_loop(n_fused, n_open, functools.partial(score_body, False), 0)
    lax.fori_loop(n_open, n_kt, functools.partial(score_body, True), 0)

    @pl.when(has_back)
    def _():
        l_fast = write_back_block()

        @pl.when(jnp.min(l_fast) < SOFTMAX_TINY)
        def _():
            mx_ref[...] = jnp.full(mx_ref.shape, NEG_BIG, F32)

            def max_body(kt, carry):
                _, chunks = logit_chunks(pl.multiple_of(kt * KT, KT))
                mx_ref[...] = functools.reduce(jnp.maximum, chunks, mx_ref[...])
                return carry

            lax.fori_loop(0, n_kt_back, max_body, 0)
            mx_ref[...] = jnp.broadcast_to(jnp.max(mx_ref[...], axis=-1, keepdims=True), mx_ref.shape)
            clear_sums()
            lax.fori_loop(0, n_kt_back, pv_body, 0)
            write_back_block()

    n_cg = (n_kt * KT + PLANE_COLS - 1) // PLANE_COLS

    def fill_body(kt, carry):
        keys_ref[:, pl.ds(pl.multiple_of(kt * KT, KT), KT)] = jnp.full((QB, KT), INT_MIN, I32)
        return carry

    lax.fori_loop(n_kt, n_cg * (PLANE_COLS // KT), fill_body, 0)

    @pl.when(i == 0)
    def _():
        planes_ref[...] = jnp.zeros(planes_ref.shape, I32)

    def plane_body(step, carry):
        c = step // (QB // PLANE_ROWS)
        r0 = pl.multiple_of((step % (QB // PLANE_ROWS)) * PLANE_ROWS, PLANE_ROWS)
        words = [keys_ref[pl.ds(r0, PLANE_ROWS), pl.ds(pl.multiple_of(c * PLANE_COLS + j * 128, 128), 128)]
                 for j in range(32)]
        for b, plane in enumerate(_bit_transpose32(words)):
            planes_ref[c, b, pl.ds(r0, PLANE_ROWS), :] = ~plane if b == 0 else plane
        return carry

    lax.fori_loop(0, n_cg * (QB // PLANE_ROWS), plane_body, 0)

    ones_mat = jnp.ones((128, 128), BF16)

    def lane_count(words):
        pc = functools.reduce(jnp.add, [lax.population_count(x) for x in words])
        return _dot(pc.astype(F32).astype(BF16), ones_mat).astype(I32)

    def sel_body(groups, step, carry):
        cand, n_gt, tau_u = carry
        hi = [cand[c] & planes_ref[c, 2 * step] for c in groups]
        lo = [cand[c] ^ hi[c] for c in groups]
        d3 = [hi[c] & planes_ref[c, 2 * step + 1] for c in groups]
        d2 = [hi[c] ^ d3[c] for c in groups]
        d1 = [lo[c] & planes_ref[c, 2 * step + 1] for c in groups]
        d0 = [lo[c] ^ d1[c] for c in groups]
        a3 = n_gt + lane_count(d3)
        a2 = a3 + lane_count(d2)
        a1 = a2 + lane_count(d1)
        is3 = a3 >= topk
        is2 = a2 >= topk
        is1 = a1 >= topk
        cand = tuple(jnp.where(is3, d3[c], jnp.where(is2, d2[c], jnp.where(is1, d1[c], d0[c]))) for c in groups)
        n_gt = jnp.where(is3, n_gt, jnp.where(is2, a3, jnp.where(is1, a2, a1)))
        digit = jnp.where(is3, 3, jnp.where(is2, 2, jnp.where(is1, 1, 0)))
        return cand, n_gt, tau_u | jnp.left_shift(digit, 30 - 2 * step)

    def radix_select(n_groups):
        groups = range(n_groups)
        start = (tuple(jnp.full((QB, 128), -1, I32) for _ in groups), jnp.zeros((QB, 128), I32),
                 jnp.zeros((QB, 128), I32))
        cand_, n_gt_, tau_ = lax.fori_loop(0, 16, functools.partial(sel_body, groups), start)
        rest = tuple(jnp.zeros((QB, 128), I32) for _ in range(n_cg_max - n_groups))
        return cand_ + rest, n_gt_, tau_

    cand, n_gt, tau_u = lax.switch(n_cg - 1, [functools.partial(radix_select, g + 1) for g in range(n_cg_max)])
    tau = tau_u ^ INT_MIN
    sentinel = tau == INT_MIN
    cand = tuple(jnp.where(sentinel, 0, cand[c]) for c in range(n_cg_max))
    need = topk - n_gt
    any_tie = jnp.max(jnp.where(lane_count(cand) > need, 1, 0)) > 0

    def fast_bias():
        floor = jnp.where(sentinel, INT_MIN + 1, tau)

        def body(kt, carry):
            for j in range(KT // 128):
                cols = pl.ds(pl.multiple_of(kt * KT + j * 128, 128), 128)
                bias_ref[:, cols] = jnp.where(keys_ref[:, cols] >= floor, 0.0, NEG_BIG)
            return carry

        lax.fori_loop(0, n_kt, body, 0)

    def tie_bias():
        lane = lax.broadcasted_iota(I32, (QB, 128), 1)

        def pos_mask(p, c):
            cg = lax.shift_right_logical(p, jnp.full_like(p, PLANE_SHIFT))
            j0 = lax.shift_right_logical(p, jnp.full_like(p, 7)) & 31
            below = ~lax.shift_right_logical(jnp.full_like(p, -1), j0)
            bit = lax.shift_right_logical(jnp.full_like(p, INT_MIN), j0)
            word = below | jnp.where(lane < (p & 127), bit, 0)
            return jnp.where(cg > c, -1, jnp.where(cg == c, word, 0))

        def pos_body(b, q):
            cq = q + jnp.left_shift(jnp.int32(1), pos_bits - 1 - b)
            cnt = lane_count([cand[c] & pos_mask(cq, c) for c in range(n_cg_max)])
            return jnp.where(cnt < need, cq, q)

        pstar = lax.fori_loop(0, pos_bits, pos_body, jnp.zeros((QB, 128), I32)) + 1
        pstar = jnp.where(sentinel, 0, pstar)

        def body(kt, carry):
            for j in range(KT // 128):
                c0 = pl.multiple_of(kt * KT + j * 128, 128)
                kk = keys_ref[:, pl.ds(c0, 128)]
                tie = jnp.where(c0 + lane < pstar, 0.0, NEG_BIG)
                bias_ref[:, pl.ds(c0, 128)] = jnp.where(kk > tau, 0.0, jnp.where(kk == tau, tie, NEG_BIG))
            return carry

        lax.fori_loop(0, n_kt, body, 0)

    lax.cond(any_tie, tie_bias, fast_bias)

    @pl.when(i == 0)
    def _():
        def norm_body(r, best):
            x = ckv_ref[0, pl.ds(pl.multiple_of(r * KT, KT), KT), :].astype(F32)
            return jnp.maximum(best, jnp.max(jnp.sum(x * x, axis=1, keepdims=True), axis=0, keepdims=True))

        n_rows = ckv_ref.shape[1]
        kv_sq = lax.fori_loop(0, n_rows // KT, norm_body, jnp.zeros((1, 1), F32))
        kvmax_ref[...] = jnp.broadcast_to(jnp.sqrt(kv_sq), kvmax_ref.shape)

    qf = q_st.astype(F32)
    q_norm = jnp.sqrt(jnp.sum(qf * qf, axis=1, keepdims=True))
    mx_ref[...] = jnp.broadcast_to(q_norm, mx_ref.shape) * kvmax_ref[0:1, :] * 1.001 + 1e-3
    qprev_ref[...] = q_st


def _dsa(qix, misc, kix, qlat, ckv, topk):
    B, T, _ = kix.shape
    n_cg_max = -(-T // PLANE_COLS)
    t_pad = n_cg_max * PLANE_COLS
    pos_bits = (t_pad - 1).bit_length()

    n_blocks = T // QBLOCK

    def rows(w):
        return pl.BlockSpec((1, QBLOCK, w), lambda b, g: (b, jnp.minimum(g, n_blocks - 1), 0))

    def head_rows(h, w):
        return pl.BlockSpec((1, h, QBLOCK, w), lambda b, g: (b, 0, jnp.minimum(g, n_blocks - 1), 0))

    def per_b(w):
        return pl.BlockSpec((1, T, w), lambda b, g: (b, 0, 0))

    out_rows = pl.BlockSpec((1, SA_HEADS, QBLOCK, KV_RANK), lambda b, g: (b, 0, jnp.maximum(g - 1, 0), 0))

    return pl.pallas_call(
        functools.partial(_dsa_kernel, topk=topk, pos_bits=pos_bits, n_cg_max=n_cg_max),
        grid=(B, n_blocks + 1),
        in_specs=[head_rows(IDX_HEADS, IDX_DIM), rows(MISC_W), per_b(IDX_DIM), head_rows(SA_HEADS, KV_RANK),
                  per_b(KV_RANK)],
        out_specs=out_rows,
        out_shape=jax.ShapeDtypeStruct((B, SA_HEADS, T, KV_RANK), BF16),
        scratch_shapes=[pltpu.VMEM((QBLOCK, t_pad), I32),
                        pltpu.VMEM((n_cg_max, 32, QBLOCK, 128), I32),
                        pltpu.VMEM((QBLOCK, t_pad), F32),
                        pltpu.VMEM((SA_HEADS * QBLOCK, 128), F32),
                        pltpu.VMEM((SA_HEADS * QBLOCK, 128), F32),
                        pltpu.VMEM((SA_HEADS * QBLOCK, KV_RANK), F32),
                        pltpu.VMEM((8, 128), F32),
                        pltpu.VMEM((SA_HEADS * QBLOCK, KV_RANK), BF16)],
        compiler_params=pltpu.CompilerParams(dimension_semantics=("arbitrary", "arbitrary"),
                                             vmem_limit_bytes=VMEM_LIMIT),
        name="dsa",
    )(qix, misc, kix, qlat, ckv)


def _first_max(v, idx, axis):
    m = jnp.max(v, axis=axis, keepdims=True)
    big = jnp.int32(2 ** 30)
    first = jnp.min(jnp.where(v == m, idx, big), axis=axis, keepdims=True)
    return m, idx == first


def _outproj_kernel(x_ref, odn_ref, olat_ref, uv_ref, wo_ref, gt_ref, sc_ref, sh_ref, g2_ref, rwt_ref,
                    rb_ref, lstrict_ref, ustrict_ref, x1_ref, h2_ref, posrow_ref, poscol_ref, gatecol_ref,
                    cpad_ref):
    tm = x_ref.shape[1]
    parts = [odn_ref[0]]
    for hd in range(SA_HEADS):
        parts.append(_dot(olat_ref[0, hd], uv_ref[hd]).astype(BF16))
    mix = jnp.concatenate(parts, axis=-1)
    x1 = x_ref[0] + gt_ref[0] * _dot(mix, wo_ref[...])
    x1_ref[0] = x1
    h2 = x1 * lax.rsqrt(jnp.mean(x1 * x1, axis=-1, keepdims=True) + EPS) * g2_ref[...]
    h2 = h2 * (1.0 + sc_ref[0]) + sh_ref[0]
    h2_ref[0] = h2.astype(BF16)

    per_g = N_EXPERTS // N_GROUPS
    s = jax.nn.sigmoid(_nt_dot(rwt_ref[...], h2, HIGHEST))
    choice = s + rb_ref[...]
    ig = lax.broadcasted_iota(I32, (per_g, tm), 0)
    gscore = []
    for gidx in range(N_GROUPS):
        cg = choice[gidx * per_g:(gidx + 1) * per_g]
        m1, hot1 = _first_max(cg, ig, 0)
        gscore.append(m1 + jnp.max(jnp.where(hot1, -jnp.inf, cg), axis=0, keepdims=True))
    gsel = [jnp.zeros((1, tm), jnp.bool_) for _ in range(N_GROUPS)]
    for _ in range(TOPK_GROUPS):
        best = functools.reduce(jnp.maximum, gscore)
        found = jnp.zeros((1, tm), jnp.bool_)
        for gidx in range(N_GROUPS):
            hot = (gscore[gidx] == best) & jnp.logical_not(found)
            found = found | hot
            gsel[gidx] = gsel[gidx] | hot
            gscore[gidx] = jnp.where(hot, -jnp.inf, gscore[gidx])
    masked = jnp.concatenate(
        [jnp.where(gsel[gidx], choice[gidx * per_g:(gidx + 1) * per_g], -jnp.inf) for gidx in range(N_GROUPS)],
        axis=0)
    ei = lax.broadcasted_iota(I32, masked.shape, 0)
    gate = jnp.zeros(masked.shape, F32)
    hots = []
    for _ in range(TOP_K):
        _, hot = _first_max(masked, ei, 0)
        hots.append(hot)
        gate = jnp.where(hot, s, gate)
        masked = jnp.where(hot, -jnp.inf, masked)
    gate = gate / jnp.sum(gate, axis=0, keepdims=True) * ROUTED_SCALE

    picked = jnp.where(functools.reduce(jnp.logical_or, hots), 1.0, 0.0)
    cnt = jnp.sum(picked, axis=1, keepdims=True)
    cpad = jnp.floor((cnt + (RUN_ALIGN - 1)) * (1.0 / RUN_ALIGN)) * RUN_ALIGN
    cpad_b = jnp.broadcast_to(cpad, (N_EXPERTS, GATE_W))
    lbase = _dot(lstrict_ref[...], cpad_b, HIGHEST)[:, :1]
    rank = _dot(picked.astype(BF16), ustrict_ref[...])
    pos = lbase + rank
    ri = lax.broadcasted_iota(I32, (GATE_W, tm), 0)
    pos_rows = jnp.zeros((GATE_W, tm), F32)
    gate_rows = jnp.zeros((GATE_W, tm), F32)
    for k, hot in enumerate(hots):
        pos_rows = jnp.where(ri == k, jnp.sum(jnp.where(hot, pos, 0.0), axis=0, keepdims=True), pos_rows)
        gate_rows = jnp.where(ri == k, jnp.sum(jnp.where(hot, gate, 0.0), axis=0, keepdims=True), gate_rows)
    posrow_ref[0, 0] = pos_rows[:TOP_K].astype(I32)
    poscol_ref[0] = pos_rows.T.astype(I32)
    gatecol_ref[0] = gate_rows.T
    cpad_ref[0, 0] = cpad_b.astype(I32)


def _outproj(x, odn, olat, uv, wo, gt1, sc2, sh2, g2, rwt, rb, tm):
    B, T, D = x.shape
    nt = T // tm
    ex = jnp.arange(N_EXPERTS)
    lstrict = (ex[:, None] > ex[None, :]).astype(F32)
    tok = jnp.arange(tm)
    ustrict = (tok[:, None] < tok[None, :]).astype(BF16)

    def full(a):
        nd = a.ndim
        return pl.BlockSpec(a.shape, lambda b, i, _n=nd: (0,) * _n)

    def rows(w):
        return pl.BlockSpec((1, tm, w), lambda b, i: (b, i, 0))

    def per_tile(h, w):
        return pl.BlockSpec((1, 1, h, w), lambda b, i: (b, i, 0, 0))

    per_b = pl.BlockSpec((1, 1, D), lambda b, i: (b, 0, 0))
    return pl.pallas_call(
        _outproj_kernel,
        grid=(B, nt),
        in_specs=[rows(D), rows(DN_V),
                  pl.BlockSpec((1, SA_HEADS, tm, KV_RANK), lambda b, i: (b, 0, i, 0)),
                  full(uv), full(wo), per_b, per_b, per_b,
                  full(g2), full(rwt), full(rb), full(lstrict), full(ustrict)],
        out_specs=[rows(D), rows(D), per_tile(TOP_K, tm), rows(GATE_W), rows(GATE_W),
                   per_tile(N_EXPERTS, GATE_W)],
        out_shape=[jax.ShapeDtypeStruct((B, T, D), F32), jax.ShapeDtypeStruct((B, T, D), BF16),
                   jax.ShapeDtypeStruct((B, nt, TOP_K, tm), I32),
                   jax.ShapeDtypeStruct((B, T, GATE_W), I32),
                   jax.ShapeDtypeStruct((B, T, GATE_W), F32),
                   jax.ShapeDtypeStruct((B, nt, N_EXPERTS, GATE_W), I32)],
        compiler_params=pltpu.CompilerParams(dimension_semantics=("arbitrary", "arbitrary"),
                                             vmem_limit_bytes=VMEM_LIMIT),
        name="outproj",
    )(x, odn, olat, uv, wo, gt1, sc2, sh2, g2, rwt, rb, lstrict, ustrict)


def _piece_sizes(max_rows):
    sizes = []
    z = RUN_ALIGN
    while z <= max_rows:
        sizes.append(z)
        z *= 2
    return sizes[::-1]


def _for_run_pieces(length, max_rows, fn):
    for z in _piece_sizes(max_rows):
        start = length & ~(2 * z - 1)

        @pl.when((length & z) != 0)
        def _(start=start, z=z):
            fn(start, z)


def _plan_kernel(cp_ref, off_ref, lb_ref, foff_ref, flen_ref, blk_ref, nused_ref):
    cp = cp_ref[...].astype(F32)
    n, ne = cp.shape
    ei = lax.broadcasted_iota(I32, (ne, ne), 0)
    ej = lax.broadcasted_iota(I32, (ne, ne), 1)
    si = lax.broadcasted_iota(I32, (n, n), 0)
    sj = lax.broadcasted_iota(I32, (n, n), 1)
    lb = _dot(cp, (ei < ej).astype(F32), HIGHEST)
    earlier_tiles = _dot((sj < si).astype(F32), cp, HIGHEST)
    rows_e = jnp.sum(cp, axis=0, keepdims=True)
    region = jnp.floor((rows_e + (ROW_BLOCK - 1)) * (1.0 / ROW_BLOCK)) * ROW_BLOCK
    region_b = jnp.broadcast_to(region, (ne, ne))
    rend_row = _dot(region_b, (ei <= ej).astype(F32), HIGHEST)[:1]
    rend_col = jnp.sum(jnp.where(ej <= ei, region_b, 0.0), axis=1, keepdims=True)
    base = rend_row - region
    total = jnp.max(rend_row, axis=1, keepdims=True)
    off_ref[...] = (base + earlier_tiles).astype(I32)
    lb_ref[...] = lb.astype(I32)
    lane = lax.broadcasted_iota(I32, (1, GATE_W), 1)
    pad = jnp.zeros((1, GATE_W - ne), F32)
    foff_ref[...] = jnp.where(lane == ne, total, jnp.concatenate([base + rows_e, pad], axis=1)).astype(I32)
    flen_ref[...] = jnp.concatenate([region - rows_e, pad], axis=1).astype(I32)
    n_used = total * (1.0 / ROW_BLOCK)
    nused_ref[...] = jnp.broadcast_to(n_used, nused_ref.shape).astype(I32)
    bi = lax.broadcasted_iota(I32, (ne, blk_ref.shape[1]), 1).astype(F32)
    ended = jnp.where(rend_col * (1.0 / ROW_BLOCK) <= jnp.minimum(bi, n_used - 1.0), 1.0, 0.0)
    blk_ref[...] = jnp.minimum(jnp.sum(ended, axis=0, keepdims=True), ne - 1.0).astype(I32)


def _plan(cp, n_blocks):
    n, ne = cp.shape
    nb_pad = -(-n_blocks // 128) * 128
    return pl.pallas_call(
        _plan_kernel,
        out_shape=[jax.ShapeDtypeStruct((n, ne), I32), jax.ShapeDtypeStruct((n, ne), I32),
                   jax.ShapeDtypeStruct((1, GATE_W), I32), jax.ShapeDtypeStruct((1, GATE_W), I32),
                   jax.ShapeDtypeStruct((1, nb_pad), I32), jax.ShapeDtypeStruct((1, GATE_W), I32)],
        name="moe_plan",
    )(cp)


def _dispatch_kernel(off_ref, cp_ref, lb_ref, foff_ref, flen_ref, h_ref, posrow_ref, xs_hbm, buf, zbuf, sem,
                     zsem, *, n_steps, tile):
    s = pl.program_id(0)
    slot = s % 2
    jmax = buf.shape[1]

    def run_copies(step, slot_, act):
        def body(e, carry):
            idx = step * N_EXPERTS + e
            lb = lb_ref[idx]
            of = off_ref[idx]

            def piece(start, z):
                act(pltpu.make_async_copy(
                    buf.at[slot_, pl.ds(pl.multiple_of(lb + start, RUN_ALIGN), z)],
                    xs_hbm.at[pl.ds(pl.multiple_of(of + start, RUN_ALIGN), z)], sem.at[slot_]))

            _for_run_pieces(cp_ref[idx], tile, piece)
            return carry

        lax.fori_loop(0, N_EXPERTS, body, 0)

    def wait_runs(step, slot_):
        last_ = step * N_EXPERTS + N_EXPERTS - 1
        _for_run_pieces(lb_ref[last_] + cp_ref[last_], jmax, lambda start, z: pltpu.make_async_copy(
            buf.at[slot_, pl.ds(0, z)], xs_hbm.at[pl.ds(0, z)], sem.at[slot_]).wait())

    def fill_copies(act):
        def body(e, carry):
            fo = foff_ref[e]

            def piece(start, z):
                act(pltpu.make_async_copy(
                    zbuf.at[pl.ds(0, z)], xs_hbm.at[pl.ds(pl.multiple_of(fo + start, RUN_ALIGN), z)], zsem.at[0]))

            _for_run_pieces(flen_ref[e], ROW_BLOCK // 2, piece)
            return carry

        lax.fori_loop(0, N_EXPERTS, body, 0)

        def tail(r, carry):
            act(pltpu.make_async_copy(
                zbuf, xs_hbm.at[pl.ds(pl.multiple_of(foff_ref[N_EXPERTS] + r * zbuf.shape[0], RUN_ALIGN),
                                      zbuf.shape[0])], zsem.at[0]))
            return carry

        lax.fori_loop(0, (xs_hbm.shape[0] - foff_ref[N_EXPERTS]) // zbuf.shape[0], tail, 0)

    @pl.when(s == 0)
    def _():
        zbuf[...] = jnp.zeros(zbuf.shape, BF16)
        fill_copies(lambda c: c.start())

    @pl.when(s >= 2)
    def _():
        wait_runs(s - 2, slot)

    h = h_ref[...]
    last = s * N_EXPERTS + N_EXPERTS - 1
    jused = lb_ref[last] + cp_ref[last]
    def local_rows(jc):
        ji = (lax.broadcasted_iota(I32, (MOE_CHUNK, tile), 0) + jc * MOE_CHUNK).astype(jnp.int16)
        p = jnp.zeros((MOE_CHUNK, tile), BF16)
        for k in range(TOP_K):
            p = jnp.where(ji == posrow_ref[0, k:k + 1, :].astype(jnp.int16), jnp.ones((), BF16), p)
        return _dot(p, h).astype(BF16)

    n_full = TOP_K * tile // MOE_CHUNK
    rows_full = [local_rows(jc) for jc in range(n_full)]
    for jc in range(n_full):
        buf[slot, jc * MOE_CHUNK:(jc + 1) * MOE_CHUNK, :] = rows_full[jc]
    for jc in range(n_full, jmax // MOE_CHUNK):
        @pl.when(jc * MOE_CHUNK < jused)
        def _(jc=jc):
            buf[slot, jc * MOE_CHUNK:(jc + 1) * MOE_CHUNK, :] = local_rows(jc)

    run_copies(s, slot, lambda c: c.start())

    @pl.when(s == n_steps - 1)
    def _():
        if n_steps >= 2:
            wait_runs(s - 1, 1 - slot)
        wait_runs(s, slot)
        fill_copies(lambda c: c.wait())


def _dispatch(h2, posrow, off, cp, lb, foff, flen, cap, tile, jmax):
    n_tok, D = h2.shape
    n_steps = n_tok // tile
    return pl.pallas_call(
        functools.partial(_dispatch_kernel, n_steps=n_steps, tile=tile),
        grid_spec=pltpu.PrefetchScalarGridSpec(
            num_scalar_prefetch=5,
            grid=(n_steps,),
            in_specs=[pl.BlockSpec((tile, D), lambda s, *_: (s, 0)),
                      pl.BlockSpec((1, TOP_K, tile), lambda s, *_: (s, 0, 0))],
            out_specs=pl.BlockSpec(memory_space=pl.ANY),
            scratch_shapes=[pltpu.VMEM((2, jmax, D), BF16), pltpu.VMEM((ROW_BLOCK // 2, D), BF16),
                            pltpu.SemaphoreType.DMA((2,)), pltpu.SemaphoreType.DMA((1,))]),
        out_shape=jax.ShapeDtypeStruct((cap, D), BF16),
        compiler_params=pltpu.CompilerParams(dimension_semantics=("arbitrary",), vmem_limit_bytes=VMEM_LIMIT),
        name="moe_dispatch",
    )(off, cp, lb, foff, flen, h2, posrow)


def _expert_kernel(blk_e_ref, nused_ref, xs_ref, wg_ref, wu_ref, wd_ref, ys_ref, wg_bf, wu_bf, wd_bf):
    i = pl.program_id(0)
    used = i < nused_ref[0]

    @pl.when(used & ((i == 0) | (blk_e_ref[i] != blk_e_ref[jnp.maximum(i - 1, 0)])))
    def _():
        wg_bf[...] = wg_ref[0].astype(BF16)
        wu_bf[...] = wu_ref[0].astype(BF16)
        wd_bf[...] = wd_ref[0].astype(BF16)

    @pl.when(used)
    def _():
        xb = xs_ref[...]
        a = _silu(_dot(xb, wg_bf[...])) * _dot(xb, wu_bf[...])
        ys_ref[...] = _dot(a.astype(BF16), wd_bf[...]).astype(BF16)

    @pl.when(jnp.logical_not(used))
    def _():
        ys_ref[...] = jnp.zeros(ys_ref.shape, BF16)


def _experts(xs, blk_e, n_used, wg, wu, wd):
    cap, D = xs.shape

    def row_block(i, be, nu):
        return (jnp.minimum(i, nu[0] - 1), 0)

    def out_block(i, be, nu):
        return (i, 0)

    def weight(i, be, nu):
        return (be[i], 0, 0)

    return pl.pallas_call(
        _expert_kernel,
        grid_spec=pltpu.PrefetchScalarGridSpec(
            num_scalar_prefetch=2,
            grid=(cap // ROW_BLOCK,),
            in_specs=[pl.BlockSpec((ROW_BLOCK, D), row_block),
                      pl.BlockSpec((1, D, D_EXPERT), weight), pl.BlockSpec((1, D, D_EXPERT), weight),
                      pl.BlockSpec((1, D_EXPERT, D), weight)],
            out_specs=pl.BlockSpec((ROW_BLOCK, D), out_block),
            scratch_shapes=[pltpu.VMEM((D, D_EXPERT), BF16), pltpu.VMEM((D, D_EXPERT), BF16),
                            pltpu.VMEM((D_EXPERT, D), BF16)]),
        out_shape=jax.ShapeDtypeStruct((cap, D), BF16),
        compiler_params=pltpu.CompilerParams(dimension_semantics=("arbitrary",), vmem_limit_bytes=VMEM_LIMIT),
        name="moe_experts",
    )(blk_e, n_used, xs, wg, wu, wd)


def _combine_kernel(off_ref, cp_ref, lb_ref, ys_hbm, poscol_ref, gatecol_ref, h_ref, sg_ref, su_ref, sd_ref,
                    x1_ref, gt_ref, fg_ref, o_ref, buf, sem, acc_ref, *, n_steps, tile, final_norm):
    s = pl.program_id(0)
    slot = s % 2
    jmax = buf.shape[1]

    def run_copies(step, slot_, act):
        def body(e, carry):
            idx = step * N_EXPERTS + e
            lb = lb_ref[idx]
            of = off_ref[idx]

            def piece(start, z):
                act(pltpu.make_async_copy(
                    ys_hbm.at[pl.ds(pl.multiple_of(of + start, RUN_ALIGN), z)],
                    buf.at[slot_, pl.ds(pl.multiple_of(lb + start, RUN_ALIGN), z)], sem.at[slot_]))

            _for_run_pieces(cp_ref[idx], tile, piece)
            return carry

        lax.fori_loop(0, N_EXPERTS, body, 0)

    @pl.when(s == 0)
    def _():
        run_copies(0, 0, lambda c: c.start())

    @pl.when(s + 1 < n_steps)
    def _():
        run_copies(s + 1, 1 - slot, lambda c: c.start())

    hb = h_ref[...]
    shared = (_silu(_dot(hb, sg_ref[...])) * _dot(hb, su_ref[...])).astype(BF16)
    acc_ref[...] = _dot(shared, sd_ref[...])

    last = s * N_EXPERTS + N_EXPERTS - 1
    jused = lb_ref[last] + cp_ref[last]
    _for_run_pieces(jused, jmax, lambda start, z: pltpu.make_async_copy(
        ys_hbm.at[pl.ds(0, z)], buf.at[slot, pl.ds(0, z)], sem.at[slot]).wait())

    def zero_body(r, carry):
        buf[slot, pl.ds(pl.multiple_of(jused + r * RUN_ALIGN, RUN_ALIGN), RUN_ALIGN), :] = jnp.zeros(
            (RUN_ALIGN, buf.shape[2]), BF16)
        return carry

    chunk_end = (jused + MOE_CHUNK - 1) // MOE_CHUNK * MOE_CHUNK
    lax.fori_loop(0, (chunk_end - jused) // RUN_ALIGN, zero_body, 0)

    def gate_rows(jc):
        ji = (lax.broadcasted_iota(I32, (tile, MOE_CHUNK), 1) + jc * MOE_CHUNK).astype(jnp.int16)
        g = jnp.zeros((tile, MOE_CHUNK), BF16)
        for k in range(TOP_K):
            g = jnp.where(ji == poscol_ref[:, k:k + 1].astype(jnp.int16), gatecol_ref[:, k:k + 1].astype(BF16), g)
        return g

    n_full = TOP_K * tile // MOE_CHUNK
    g_full = jnp.concatenate([gate_rows(jc) for jc in range(n_full)], axis=1)
    acc_ref[...] += _dot(g_full, buf[slot, 0:n_full * MOE_CHUNK, :])
    for jc in range(n_full, jmax // MOE_CHUNK):
        @pl.when(jc * MOE_CHUNK < jused)
        def _(jc=jc):
            acc_ref[...] += _dot(gate_rows(jc), buf[slot, jc * MOE_CHUNK:(jc + 1) * MOE_CHUNK, :])

    y = x1_ref[...] + gt_ref[0] * acc_ref[...]
    if final_norm:
        y = y * lax.rsqrt(jnp.mean(y * y, axis=-1, keepdims=True) + EPS) * fg_ref[...]
    o_ref[...] = y


def _combine(ys, poscol, gatecol, h2, sg, su, sd, x1, gt2, fg, off, cp, lb, tile, jmax, tiles_per_batch,
             final_norm):
    n_tok, D = h2.shape
    n_steps = n_tok // tile

    def full(a):
        nd = a.ndim
        return pl.BlockSpec(a.shape, lambda s, *_, _n=nd: (0,) * _n)

    def rows(w):
        return pl.BlockSpec((tile, w), lambda s, *_: (s, 0))

    return pl.pallas_call(
        functools.partial(_combine_kernel, n_steps=n_steps, tile=tile, final_norm=final_norm),
        grid_spec=pltpu.PrefetchScalarGridSpec(
            num_scalar_prefetch=3,
            grid=(n_steps,),
            in_specs=[pl.BlockSpec(memory_space=pl.ANY), rows(GATE_W), rows(GATE_W), rows(D),
                      full(sg), full(su), full(sd), rows(D),
                      pl.BlockSpec((1, 1, D), lambda s, *_: (s // tiles_per_batch, 0, 0)), full(fg)],
            out_specs=rows(D),
            scratch_shapes=[pltpu.VMEM((2, jmax, D), BF16), pltpu.SemaphoreType.DMA((2,)),
                            pltpu.VMEM((tile, D), F32)]),
        out_shape=jax.ShapeDtypeStruct((n_tok, D), F32),
        compiler_params=pltpu.CompilerParams(dimension_semantics=("arbitrary",), vmem_limit_bytes=VMEM_LIMIT),
        name="moe_combine",
    )(off, cp, lb, ys, poscol, gatecol, h2, sg, su, sd, x1, gt2, fg)


def _moe(h2, posrow, poscol, gatecol, cpad, wg, wu, wd, sg, su, sd, x1, gt2, fg, tile, final_norm):
    B, T, D = x1.shape
    n_tok = B * T
    n_tiles = n_tok // tile
    jmax = -(-(TOP_K * tile + N_EXPERTS * (RUN_ALIGN - 1)) // MOE_CHUNK) * MOE_CHUNK
    cap = -(-(TOP_K * n_tok + n_tiles * N_EXPERTS * (RUN_ALIGN - 1) + N_EXPERTS * (ROW_BLOCK - RUN_ALIGN))
            // ROW_BLOCK) * ROW_BLOCK

    cp = cpad[..., 0].reshape(n_tiles, N_EXPERTS)
    off, lb, foff, flen, blk_e, n_used = _plan(cp, cap // ROW_BLOCK)
    flat = lambda a: a.reshape(-1)

    xs = _dispatch(h2.reshape(n_tok, D), posrow.reshape(n_tiles, TOP_K, tile), flat(off), flat(cp), flat(lb),
                   foff[0], flen[0], cap, tile, jmax)
    ys = _experts(xs, blk_e[0], n_used[0], wg, wu, wd)
    out = _combine(ys, poscol.reshape(n_tok, GATE_W), gatecol.reshape(n_tok, GATE_W), h2.reshape(n_tok, D),
                   sg, su, sd, x1.reshape(n_tok, D), gt2, fg, flat(off), flat(cp), flat(lb), tile, jmax,
                   T // tile, final_norm)
    return out.reshape(B, T, D)


def _misc_lanes(vec, start):
    return jnp.zeros((1, MISC_W), F32).at[0, start:start + vec.shape[0]].set(vec.astype(F32))


def kernel(x, c, ada_w, ada_b, norm1_g, w_in, conv_w, a_log, dt_bias, dn_norm_g, kv_norm_g, w_uk, w_uv,
           idx_k_ln_g, idx_k_ln_b, w_out, norm2_g, router_w, router_b, exp_w_gate, exp_w_up, exp_w_down,
           sh_w_gate, sh_w_up, sh_w_down, final_g):
    B, T, D = x.shape
    depth = ada_w.shape[0]
    topk = min(IDX_TOPK_MAX, T // 4)
    tm = min(512, T)
    r_dn = min(256, T)

    cond_in = jnp.zeros((8, D), F32).at[:B].set(c)
    pos = jnp.arange(tm)
    tri = ((pos[:, None] // CHUNK == pos[None, :] // CHUNK) & (pos[:, None] >= pos[None, :])).astype(F32)

    for l in range(depth):
        mod = _ada(cond_in, ada_w[l], ada_b[l][None, :])[:B]
        sh1, sc1, gt1, sh2, sc2, gt2 = [m[:, None, :] for m in jnp.split(mod, 6, axis=-1)]

        offs = [0]
        for s in (DN_QK, DN_QK, DN_V, DN_V, DN_HEADS, DN_HEADS, SA_Q, KV_RANK, IDX_Q, IDX_DIM, IDX_HEADS):
            offs.append(offs[-1] + s)
        w = w_in[l]
        wc = w[:, offs[0]:offs[3]].astype(BF16)
        wz = w[:, offs[3]:offs[4]].astype(BF16)
        wq = w[:, offs[6]:offs[7]].astype(BF16)
        wkv = w[:, offs[7]:offs[8]].astype(BF16)
        wqi = w[:, offs[8]:offs[9]].astype(BF16)
        wm = jnp.concatenate([w[:, offs[9]:offs[10]], w[:, offs[4]:offs[5]], w[:, offs[5]:offs[6]],
                              w[:, offs[10]:offs[11]],
                              jnp.zeros((D, MISC_W - IDX_DIM - 2 * DN_HEADS - IDX_HEADS), F32)],
                             axis=1).astype(BF16)
        ukt = jnp.swapaxes(w_uk[l], 1, 2).astype(BF16)

        q, k, v, z, qlat, ckv, qix, kix, misc = _inproj(
            x, sc1, sh1, norm1_g[l][None, :], wc, wz, wq, wkv, wqi, wm, conv_w[l], ukt,
            kv_norm_g[l][None, :], _misc_lanes(idx_k_ln_g[l], M_KIX), _misc_lanes(idx_k_ln_b[l], M_KIX),
            _misc_lanes(a_log[l], M_A), _misc_lanes(dt_bias[l], M_A), tri, tm)

        odn = _deltanet(q, k, v, z, misc, dn_norm_g[l][None, :], r_dn)
        olat = _dsa(qix, misc, kix, qlat, ckv, topk)

        x1, h2, posrow, poscol, gatecol, cpad = _outproj(
            x, odn, olat, w_uv[l].astype(BF16), w_out[l].astype(BF16), gt1, sc2, sh2,
            norm2_g[l][None, :], router_w[l].T, router_b[l][:, None], tm)

        x = _moe(h2, posrow, poscol, gatecol, cpad, exp_w_gate[l], exp_w_up[l], exp_w_down[l],
                 sh_w_gate[l].astype(BF16), sh_w_up[l].astype(BF16),
                 sh_w_down[l].astype(BF16), x1, gt2, final_g[None, :], tm, l == depth - 1)
    return x
```

```python
import functools

import jax
import jax.numpy as jnp
from jax import lax
from jax.experimental import pallas as pl
from jax.experimental.pallas import tpu as pltpu

F32 = jnp.float32
BF16 = jnp.bfloat16
I32 = jnp.int32
HIGHEST = lax.Precision.HIGHEST

EPS = 1e-6
CHUNK = 64
DN_HEADS = 4
DN_DK = 128
DN_DV = 128
CONV_K = 4
SA_HEADS = 4
SA_DQK = 128
SA_DV = 128
KV_RANK = 256
IDX_HEADS = 4
IDX_DIM = 64
IDX_TOPK_MAX = 256
SM_SCALE = SA_DQK ** -0.5
LOG2E = 1.4426950408889634
IDX_W_SCALE = (IDX_HEADS * IDX_DIM) ** -0.5
N_EXPERTS = 64
TOP_K = 8
N_GROUPS = 8
TOPK_GROUPS = 4
D_EXPERT = 256
ROUTED_SCALE = 2.5
GATE_W = 128
RUN_ALIGN = 16
ROW_BLOCK = 1024
MOE_CHUNK = 512

DN_QK = DN_HEADS * DN_DK
DN_V = DN_HEADS * DN_DV
CONV_DIM = 2 * DN_QK + DN_V
SA_Q = SA_HEADS * SA_DQK
IDX_Q = IDX_HEADS * IDX_DIM

MISC_W = 128
M_KIX = 0
M_BETA = IDX_DIM
M_A = M_BETA + DN_HEADS
M_WIX = M_A + DN_HEADS

DN_SUB = 2 * CHUNK
QBLOCK = 256
SOFTMAX_TINY = 2.0 ** -100
KEY_TILE = 1024
PLANE_COLS = 32 * 128
PLANE_SHIFT = 12
PLANE_ROWS = 16
INT_MIN = -2 ** 31
NEG_BIG = -1e30
VMEM_LIMIT = 56 * 1024 * 1024


def _nt_dot(a, b, precision=None):
    return lax.dot_general(a, b, (((1,), (1,)), ((), ())), preferred_element_type=F32,
                           precision=precision)


def _dot(a, b, precision=None):
    return jnp.dot(a, b, preferred_element_type=F32, precision=precision)


def _silu(x):
    return x * jax.nn.sigmoid(x)


def _softplus(x):
    return jnp.maximum(x, 0.0) + jnp.log(1.0 + jnp.exp(-jnp.abs(x)))


def _ada_kernel(c_ref, w_ref, b_ref, o_ref):
    cond = _silu(c_ref[...])
    o_ref[...] = _dot(cond, w_ref[...], HIGHEST) + b_ref[...]


def _ada(c_pad, ada_w, ada_b):
    rows, d = c_pad.shape
    n_out = ada_w.shape[1]
    return pl.pallas_call(
        _ada_kernel,
        grid=(n_out // d,),
        in_specs=[pl.BlockSpec((rows, d), lambda j: (0, 0)),
                  pl.BlockSpec((d, d), lambda j: (0, j)),
                  pl.BlockSpec((1, d), lambda j: (0, j))],
        out_specs=pl.BlockSpec((rows, d), lambda j: (0, j)),
        out_shape=jax.ShapeDtypeStruct((rows, n_out), F32),
        compiler_params=pltpu.CompilerParams(vmem_limit_bytes=VMEM_LIMIT),
        name="ada",
    )(c_pad, ada_w, ada_b)


def _inproj_kernel(x_ref, sc_ref, sh_ref, g1_ref, wc_ref, wz_ref, wq_ref, wkv_ref, wqi_ref, wm_ref,
                   convw_ref, ukt_ref, kvg_ref, lng_ref, lnb_ref, alog_ref, dtb_ref, tri_ref,
                   q_ref, k_ref, v_ref, z_ref, qlat_ref, ckv_ref, qix_ref, kix_ref, misc_ref,
                   conv_buf):
    tm = x_ref.shape[1]
    i = pl.program_id(1)

    x = x_ref[0]
    h = x * lax.rsqrt(jnp.mean(x * x, axis=-1, keepdims=True) + EPS) * g1_ref[...]
    h = h * (1.0 + sc_ref[0]) + sh_ref[0]
    hb = h.astype(BF16)

    @pl.when(i == 0)
    def _():
        conv_buf[0:8, :] = jnp.zeros((8, CONV_DIM), F32)

    conv_buf[8:8 + tm, :] = _dot(hb, wc_ref[...])
    for grp, dst in ((0, q_ref), (1, k_ref), (2, v_ref)):
        cols = slice(grp * DN_QK, (grp + 1) * DN_QK)
        y = jnp.zeros((tm, DN_QK), F32)
        for j in range(CONV_K):
            y = y + convw_ref[j:j + 1, cols] * conv_buf[8 - (CONV_K - 1) + j:8 - (CONV_K - 1) + j + tm, cols]
        y = _silu(y)
        if grp < 2:
            outs = []
            for hd in range(DN_HEADS):
                yh = y[:, hd * DN_DK:(hd + 1) * DN_DK]
                yh = yh * lax.rsqrt(jnp.sum(yh * yh, axis=-1, keepdims=True) + EPS)
                if grp == 0:
                    yh = yh * (DN_DK ** -0.5)
                outs.append(yh)
            y = jnp.concatenate(outs, axis=-1)
        dst[0] = y
    conv_buf[0:8, :] = conv_buf[tm:tm + 8, :]

    z_ref[0] = _dot(hb, wz_ref[...])

    q_sa = _dot(hb, wq_ref[...]).astype(BF16)
    for hd in range(SA_HEADS):
        ql = _dot(q_sa[:, hd * SA_DQK:(hd + 1) * SA_DQK], ukt_ref[hd]) * (SM_SCALE * LOG2E)
        qlat_ref[0, hd] = ql.astype(BF16)

    ckv = _dot(hb, wkv_ref[...])
    ckv = ckv * lax.rsqrt(jnp.mean(ckv * ckv, axis=-1, keepdims=True) + EPS) * kvg_ref[...]
    ckv_ref[0] = ckv.astype(BF16)

    q_ix = _dot(hb, wqi_ref[...]).astype(BF16)
    for hd in range(IDX_HEADS):
        qix_ref[0, hd] = q_ix[:, hd * IDX_DIM:(hd + 1) * IDX_DIM]

    m = _dot(hb, wm_ref[...])
    lane = lax.broadcasted_iota(I32, (tm, MISC_W), 1)
    is_k = lane < IDX_DIM
    mu = jnp.sum(jnp.where(is_k, m, 0.0), axis=-1, keepdims=True) * (1.0 / IDX_DIM)
    kc = jnp.where(is_k, m - mu, 0.0)
    var = jnp.sum(kc * kc, axis=-1, keepdims=True) * (1.0 / IDX_DIM)
    kn = kc * lax.rsqrt(var + EPS) * lng_ref[...] + lnb_ref[...]
    kix_ref[0] = kn[:, :IDX_DIM].astype(BF16)

    beta = jax.nn.sigmoid(m)
    g = -jnp.exp(alog_ref[...]) * _softplus(m + dtb_ref[...])
    is_a = (lane >= M_A) & (lane < M_A + DN_HEADS)
    g = jnp.where(is_a, g, 0.0)
    gc = _dot(tri_ref[...], g, HIGHEST)
    is_b = (lane >= M_BETA) & (lane < M_BETA + DN_HEADS)
    is_w = (lane >= M_WIX) & (lane < M_WIX + IDX_HEADS)
    misc_ref[0] = jnp.where(is_b, beta, jnp.where(is_a, gc, jnp.where(is_w, m * IDX_W_SCALE, 0.0)))


def _inproj(x, sc1, sh1, g1, wc, wz, wq, wkv, wqi, wm, conv_w, ukt, kvg, lng, lnb, alog, dtb, tri, tm):
    B, T, D = x.shape
    nt = T // tm

    def full(a):
        nd = a.ndim
        return pl.BlockSpec(a.shape, lambda b, i, _n=nd: (0,) * _n)

    def rows(w):
        return pl.BlockSpec((1, tm, w), lambda b, i: (b, i, 0))

    per_b = pl.BlockSpec((1, 1, D), lambda b, i: (b, 0, 0))
    def head_rows(h, w):
        return pl.BlockSpec((1, h, tm, w), lambda b, i: (b, 0, i, 0))

    outs = [(None, DN_QK, F32), (None, DN_QK, F32), (None, DN_V, F32), (None, DN_V, F32),
            (SA_HEADS, KV_RANK, BF16), (None, KV_RANK, BF16), (IDX_HEADS, IDX_DIM, BF16),
            (None, IDX_DIM, BF16), (None, MISC_W, F32)]
    return pl.pallas_call(
        _inproj_kernel,
        grid=(B, nt),
        in_specs=[rows(D), per_b, per_b, full(g1), full(wc), full(wz), full(wq), full(wkv), full(wqi),
                  full(wm), full(conv_w), full(ukt), full(kvg), full(lng), full(lnb), full(alog),
                  full(dtb), full(tri)],
        out_specs=[rows(w) if h is None else head_rows(h, w) for h, w, _ in outs],
        out_shape=[jax.ShapeDtypeStruct((B, T, w) if h is None else (B, h, T, w), dt) for h, w, dt in outs],
        scratch_shapes=[pltpu.VMEM((tm + 8, CONV_DIM), F32)],
        compiler_params=pltpu.CompilerParams(dimension_semantics=("arbitrary", "arbitrary"),
                                             vmem_limit_bytes=VMEM_LIMIT),
        name="inproj",
    )(x, sc1, sh1, g1, wc, wz, wq, wkv, wqi, wm, conv_w, ukt, kvg, lng, lnb, alog, dtb, tri)


def _deltanet_kernel(q_ref, k_ref, v_ref, z_ref, misc_ref, ng_ref, o_ref, s_ref):
    R = q_ref.shape[1]
    n_chunks = R // CHUNK

    @pl.when(pl.program_id(1) == 0)
    def _():
        s_ref[...] = jnp.zeros(s_ref.shape, F32)

    misc = misc_ref[0]
    misc_t = misc.T
    SB = min(DN_SUB, R)
    row = lax.broadcasted_iota(I32, (SB, SB), 0)
    col = lax.broadcasted_iota(I32, (SB, SB), 1)
    same = (row // CHUNK) == (col // CHUNK)
    lower = same & (row >= col)
    strict = same & (row > col)
    eye = (row == col).astype(F32)

    def mm(a, b):
        return _dot(a.astype(BF16), b.astype(BF16))

    def mm3(a, b):
        ah = a.astype(BF16)
        bh = b.astype(BF16)
        al = (a - ah.astype(F32)).astype(BF16)
        bl = (b - bh.astype(F32)).astype(BF16)
        return _dot(jnp.concatenate([ah, ah, al], axis=1), jnp.concatenate([bh, bl, bh], axis=0))

    heads = range(DN_HEADS)
    subs = range(R // SB)
    chains = [(hd, sb) for hd in heads for sb in subs]
    cols = [slice(hd * DN_DK, (hd + 1) * DN_DK) for hd in heads]
    qh = [q_ref[0, :, cols[hd]] for hd in heads]
    kh = [k_ref[0, :, cols[hd]] for hd in heads]
    beta = [misc[:, M_BETA + hd:M_BETA + hd + 1] for hd in heads]
    gc_c = [misc[:, M_A + hd:M_A + hd + 1] for hd in heads]
    eg = [jnp.exp(gc_c[hd]) for hd in heads]
    kb = [kh[hd] * beta[hd] for hd in heads]
    rhs = [jnp.concatenate([v_ref[0, :, cols[hd]] * beta[hd], kb[hd] * eg[hd]], axis=-1) for hd in heads]
    q_dec = [qh[hd] * eg[hd] for hd in heads]

    def rows_of(sb):
        return slice(sb * SB, (sb + 1) * SB)

    decay, a, qk_sb = {}, {}, {}
    for hd, sb in chains:
        bs = rows_of(sb)
        gc_r = misc_t[M_A + hd:M_A + hd + 1, bs]
        decay[hd, sb] = jnp.where(lower, jnp.exp(jnp.where(lower, gc_c[hd][bs] - gc_r, 0.0)), 0.0)
    for hd, sb in chains:
        bs = rows_of(sb)
        khb = kh[hd][bs].astype(BF16)
        a[hd, sb] = jnp.where(strict, _nt_dot(kb[hd][bs].astype(BF16), khb) * decay[hd, sb], 0.0)
        qk_sb[hd, sb] = jnp.where(lower, _nt_dot(qh[hd][bs].astype(BF16), khb) * decay[hd, sb], 0.0)
    p = {ch: eye - a[ch] for ch in chains}
    xp = {ch: mm3(a[ch], a[ch]) for ch in chains}
    n_sq = 1
    while True:
        p = {ch: p[ch] + mm3(p[ch], xp[ch]) for ch in chains}
        n_sq *= 2
        if n_sq * 2 >= CHUNK:
            break
        xp = {ch: mm3(xp[ch], xp[ch]) for ch in chains}
    sol = {(hd, sb): mm(p[hd, sb], rhs[hd][rows_of(sb)]) for hd, sb in chains}

    def chunk_of(c):
        per = SB // CHUNK
        return c // per, slice((c % per) * CHUNK, (c % per + 1) * CHUNK)

    s = [s_ref[hd] for hd in heads]
    o_parts = [[] for _ in heads]
    for c in range(n_chunks):
        rs = slice(c * CHUNK, (c + 1) * CHUNK)
        sb, r = chunk_of(c)
        gl = [gc_c[hd][(c + 1) * CHUNK - 1:(c + 1) * CHUNK, :] for hd in heads]
        k_dec = [kh[hd][rs] * jnp.exp(gl[hd] - gc_c[hd][rs]) for hd in heads]
        v_new = [sol[hd, sb][r, :DN_DV] - mm(sol[hd, sb][r, DN_DV:], s[hd]) for hd in heads]
        for hd in heads:
            o_parts[hd].append(mm(q_dec[hd][rs], s[hd]) + mm(qk_sb[hd, sb][r, r], v_new[hd]))
        s = [s[hd] * jnp.exp(gl[hd]) + mm(k_dec[hd].T, v_new[hd]) for hd in heads]
    for hd in heads:
        s_ref[hd] = s[hd]
        o = jnp.concatenate(o_parts[hd], axis=0)
        o = o * lax.rsqrt(jnp.mean(o * o, axis=-1, keepdims=True) + EPS) * ng_ref[...]
        o_ref[0, :, cols[hd]] = (o * _silu(z_ref[0, :, cols[hd]])).astype(BF16)


def _deltanet(q, k, v, z, misc, ng, R):
    B, T, _ = q.shape

    def rows(w):
        return pl.BlockSpec((1, R, w), lambda b, i: (b, i, 0))

    return pl.pallas_call(
        _deltanet_kernel,
        grid=(B, T // R),
        in_specs=[rows(DN_QK), rows(DN_QK), rows(DN_V), rows(DN_V), rows(MISC_W),
                  pl.BlockSpec((1, DN_DV), lambda b, i: (0, 0))],
        out_specs=rows(DN_V),
        out_shape=jax.ShapeDtypeStruct((B, T, DN_V), BF16),
        scratch_shapes=[pltpu.VMEM((DN_HEADS, DN_DK, DN_DV), F32)],
        compiler_params=pltpu.CompilerParams(dimension_semantics=("arbitrary", "arbitrary"),
                                             vmem_limit_bytes=VMEM_LIMIT),
        name="deltanet",
    )(q, k, v, z, misc, ng)


def _bit_transpose32(words):
    w = list(words)
    j = 16
    m = 0x0000FFFF
    while j:
        k = 0
        m_i32 = jnp.int32(m - (1 << 32) if m >= (1 << 31) else m)
        while k < 32:
            t = (w[k] ^ lax.shift_right_logical(w[k + j], jnp.full_like(w[k], j))) & m_i32
            w[k] = w[k] ^ t
            w[k + j] = w[k + j] ^ jnp.left_shift(t, j)
            k = (k + j + 1) & ~j
        j >>= 1
        m = (m ^ (m << j)) & 0xFFFFFFFF
    return w


def _dsa_kernel(qix_ref, misc_ref, kix_ref, qlat_ref, ckv_ref, o_ref, keys_ref, planes_ref, bias_ref,
                mx_ref, l_ref, acc_ref, kvmax_ref, qprev_ref, *, topk, pos_bits, n_cg_max):
    g = pl.program_id(1)
    n_blocks = pl.num_programs(1) - 1
    i = g
    QB = QBLOCK
    KT = KEY_TILE
    has_front = g < n_blocks
    has_back = g >= 1
    n_kt = jnp.where(has_front, (i * QB + QB + KT - 1) // KT, 0)
    n_kt_back = (g * QB + KT - 1) // KT

    rowi = lax.broadcasted_iota(I32, (QB, KT), 0)
    coli = lax.broadcasted_iota(I32, (QB, KT), 1)
    limit = i * QB + (rowi // CHUNK + 1) * CHUNK

    misc = misc_ref[0]
    qix = qix_ref[0].reshape(IDX_HEADS * QB, IDX_DIM)
    q_st = qlat_ref[0].reshape(SA_HEADS * QB, KV_RANK)

    def score_body(masked, kt, carry):
        k0 = pl.multiple_of(kt * KT, KT)
        kx = kix_ref[0, pl.ds(k0, KT), :]
        rel = jnp.maximum(_nt_dot(qix, kx), 0.0)
        sc = jnp.zeros((QB, KT), F32)
        for hd in range(IDX_HEADS):
            sc = sc + misc[:, M_WIX + hd:M_WIX + hd + 1] * rel[hd * QB:(hd + 1) * QB]
        sc = jnp.where(sc == 0.0, 0.0, sc)
        bits = pltpu.bitcast(sc, I32)
        key = jnp.where(bits < 0, bits ^ 0x7FFFFFFF, bits)
        keys_ref[:, pl.ds(k0, KT)] = jnp.where(k0 + coli < limit, key, INT_MIN) if masked else key
        return carry

    def logit_chunks(k0):
        kv = ckv_ref[0, pl.ds(k0, KT), :]
        s = _nt_dot(qprev_ref[...], kv)
        bias = bias_ref[:, pl.ds(k0, KT)]
        chunks = []
        for j in range(KT // 128):
            bj = bias[:, j * 128:(j + 1) * 128]
            chunks.append(s[:, j * 128:(j + 1) * 128] + jnp.concatenate([bj] * SA_HEADS, axis=0))
        return kv, chunks

    def pv_body(kt, carry):
        kv, chunks = logit_chunks(pl.multiple_of(kt * KT, KT))
        shift = mx_ref[...]
        ps = [jnp.exp2(c - shift) for c in chunks]
        l_ref[...] = functools.reduce(jnp.add, ps, l_ref[...])
        p = jnp.concatenate([pj.astype(BF16) for pj in ps], axis=1)
        acc_ref[...] += _dot(p, kv)
        return carry

    def clear_sums():
        l_ref[...] = jnp.zeros(l_ref.shape, F32)
        acc_ref[...] = jnp.zeros(acc_ref.shape, F32)

    def write_back_block():
        l_row = jnp.sum(l_ref[...], axis=-1, keepdims=True)
        o_ref[0] = (acc_ref[...] / l_row).astype(BF16).reshape(SA_HEADS, QB, KV_RANK)
        return l_row

    pl.when(has_back)(clear_sums)

    n_open = jnp.where(has_front, (i * QB + CHUNK) // KT, 0)
    n_fused = jnp.minimum(n_open, n_kt_back)

    def fused_body(kt, carry):
        pv_body(kt, carry)
        return score_body(False, kt, carry)

    lax.fori_loop(0, n_fused, fused_body, 0)
    lax.fori_loop(n_fused, n_kt_back, pv_body, 0)
    lax.fori_loop(n_fused, n_open, functools.partial(score_body, False), 0)
    lax.fori_loop(n_open, n_kt, functools.partial(score_body, True), 0)

    @pl.when(has_back)
    def _():
        l_fast = write_back_block()

        @pl.when(jnp.min(l_fast) < SOFTMAX_TINY)
        def _():
            mx_ref[...] = jnp.full(mx_ref.shape, NEG_BIG, F32)

            def max_body(kt, carry):
                _, chunks = logit_chunks(pl.multiple_of(kt * KT, KT))
                mx_ref[...] = functools.reduce(jnp.maximum, chunks, mx_ref[...])
                return carry

            lax.fori_loop(0, n_kt_back, max_body, 0)
            mx_ref[...] = jnp.broadcast_to(jnp.max(mx_ref[...], axis=-1, keepdims=True), mx_ref.shape)
            clear_sums()
            lax.fori_loop(0, n_kt_back, pv_body, 0)
            write_back_block()

    n_cg = (n_kt * KT + PLANE_COLS - 1) // PLANE_COLS

    def fill_body(kt, carry):
        keys_ref[:, pl.ds(pl.multiple_of(kt * KT, KT), KT)] = jnp.full((QB, KT), INT_MIN, I32)
        return carry

    lax.fori_loop(n_kt, n_cg * (PLANE_COLS // KT), fill_body, 0)

    @pl.when(i == 0)
    def _():
        planes_ref[...] = jnp.zeros(planes_ref.shape, I32)

    def plane_body(step, carry):
        c = step // (QB // PLANE_ROWS)
        r0 = pl.multiple_of((step % (QB // PLANE_ROWS)) * PLANE_ROWS, PLANE_ROWS)
        words = [keys_ref[pl.ds(r0, PLANE_ROWS), pl.ds(pl.multiple_of(c * PLANE_COLS + j * 128, 128), 128)]
                 for j in range(32)]
        for b, plane in enumerate(_bit_transpose32(words)):
            planes_ref[c, b, pl.ds(r0, PLANE_ROWS), :] = ~plane if b == 0 else plane
        return carry

    lax.fori_loop(0, n_cg * (QB // PLANE_ROWS), plane_body, 0)

    ones_mat = jnp.ones((128, 128), BF16)

    def lane_count(words):
        pc = functools.reduce(jnp.add, [lax.population_count(x) for x in words])
        return _dot(pc.astype(F32).astype(BF16), ones_mat).astype(I32)

    def sel_body(groups, step, carry):
        cand, n_gt, tau_u = carry
        hi = [cand[c] & planes_ref[c, 2 * step] for c in groups]
        lo = [cand[c] ^ hi[c] for c in groups]
        d3 = [hi[c] & planes_ref[c, 2 * step + 1] for c in groups]
        d2 = [hi[c] ^ d3[c] for c in groups]
        d1 = [lo[c] & planes_ref[c, 2 * step + 1] for c in groups]
        d0 = [lo[c] ^ d1[c] for c in groups]
        a3 = n_gt + lane_count(d3)
        a2 = a3 + lane_count(d2)
        a1 = a2 + lane_count(d1)
        is3 = a3 >= topk
        is2 = a2 >= topk
        is1 = a1 >= topk
        cand = tuple(jnp.where(is3, d3[c], jnp.where(is2, d2[c], jnp.where(is1, d1[c], d0[c]))) for c in groups)
        n_gt = jnp.where(is3, n_gt, jnp.where(is2, a3, jnp.where(is1, a2, a1)))
        digit = jnp.where(is3, 3, jnp.where(is2, 2, jnp.where(is1, 1, 0)))
        return cand, n_gt, tau_u | jnp.left_shift(digit, 30 - 2 * step)

    def radix_select(n_groups):
        groups = range(n_groups)
        start = (tuple(jnp.full((QB, 128), -1, I32) for _ in groups), jnp.zeros((QB, 128), I32),
                 jnp.zeros((QB, 128), I32))
        cand_, n_gt_, tau_ = lax.fori_loop(0, 16, functools.partial(sel_body, groups), start)
        rest = tuple(jnp.zeros((QB, 128), I32) for _ in range(n_cg_max - n_groups))
        return cand_ + rest, n_gt_, tau_

    cand, n_gt, tau_u = lax.switch(n_cg - 1, [functools.partial(radix_select, g + 1) for g in range(n_cg_max)])
    tau = tau_u ^ INT_MIN
    sentinel = tau == INT_MIN
    cand = tuple(jnp.where(sentinel, 0, cand[c]) for c in range(n_cg_max))
    need = topk - n_gt
    any_tie = jnp.max(jnp.where(lane_count(cand) > need, 1, 0)) > 0

    def fast_bias():
        floor = jnp.where(sentinel, INT_MIN + 1, tau)

        def body(kt, carry):
            for j in range(KT // 128):
                cols = pl.ds(pl.multiple_of(kt * KT + j * 128, 128), 128)
                bias_ref[:, cols] = jnp.where(keys_ref[:, cols] >= floor, 0.0, NEG_BIG)
            return carry

        lax.fori_loop(0, n_kt, body, 0)

    def tie_bias():
        lane = lax.broadcasted_iota(I32, (QB, 128), 1)

        def pos_mask(p, c):
            cg = lax.shift_right_logical(p, jnp.full_like(p, PLANE_SHIFT))
            j0 = lax.shift_right_logical(p, jnp.full_like(p, 7)) & 31
            below = ~lax.shift_right_logical(jnp.full_like(p, -1), j0)
            bit = lax.shift_right_logical(jnp.full_like(p, INT_MIN), j0)
            word = below | jnp.where(lane < (p & 127), bit, 0)
            return jnp.where(cg > c, -1, jnp.where(cg == c, word, 0))

        def pos_body(b, q):
            cq = q + jnp.left_shift(jnp.int32(1), pos_bits - 1 - b)
            cnt = lane_count([cand[c] & pos_mask(cq, c) for c in range(n_cg_max)])
            return jnp.where(cnt < need, cq, q)

        pstar = lax.fori_loop(0, pos_bits, pos_body, jnp.zeros((QB, 128), I32)) + 1
        pstar = jnp.where(sentinel, 0, pstar)

        def body(kt, carry):
            for j in range(KT // 128):
                c0 = pl.multiple_of(kt * KT + j * 128, 128)
                kk = keys_ref[:, pl.ds(c0, 128)]
                tie = jnp.where(c0 + lane < pstar, 0.0, NEG_BIG)
                bias_ref[:, pl.ds(c0, 128)] = jnp.where(kk > tau, 0.0, jnp.where(kk == tau, tie, NEG_BIG))
            return carry

        lax.fori_loop(0, n_kt, body, 0)

    lax.cond(any_tie, tie_bias, fast_bias)

    @pl.when(i == 0)
    def _():
        def norm_body(r, best):
            x = ckv_ref[0, pl.ds(pl.multiple_of(r * KT, KT), KT), :].astype(F32)
            return jnp.maximum(best, jnp.max(jnp.sum(x * x, axis=1, keepdims=True), axis=0, keepdims=True))

        n_rows = ckv_ref.shape[1]
        kv_sq = lax.fori_loop(0, n_rows // KT, norm_body, jnp.zeros((1, 1), F32))
        kvmax_ref[...] = jnp.broadcast_to(jnp.sqrt(kv_sq), kvmax_ref.shape)

    qf = q_st.astype(F32)
    q_norm = jnp.sqrt(jnp.sum(qf * qf, axis=1, keepdims=True))
    mx_ref[...] = jnp.broadcast_to(q_norm, mx_ref.shape) * kvmax_ref[0:1, :] * 1.001 + 1e-3
    qprev_ref[...] = q_st


def _dsa(qix, misc, kix, qlat, ckv, topk):
    B, T, _ = kix.shape
    n_cg_max = -(-T // PLANE_COLS)
    t_pad = n_cg_max * PLANE_COLS
    pos_bits = (t_pad - 1).bit_length()

    n_blocks = T // QBLOCK

    def rows(w):
        return pl.BlockSpec((1, QBLOCK, w), lambda b, g: (b, jnp.minimum(g, n_blocks - 1), 0))

    def head_rows(h, w):
        return pl.BlockSpec((1, h, QBLOCK, w), lambda b, g: (b, 0, jnp.minimum(g, n_blocks - 1), 0))

    def per_b(w):
        return pl.BlockSpec((1, T, w), lambda b, g: (b, 0, 0))

    out_rows = pl.BlockSpec((1, SA_HEADS, QBLOCK, KV_RANK), lambda b, g: (b, 0, jnp.maximum(g - 1, 0), 0))

    return pl.pallas_call(
        functools.partial(_dsa_kernel, topk=topk, pos_bits=pos_bits, n_cg_max=n_cg_max),
        grid=(B, n_blocks + 1),
        in_specs=[head_rows(IDX_HEADS, IDX_DIM), rows(MISC_W), per_b(IDX_DIM), head_rows(SA_HEADS, KV_RANK),
                  per_b(KV_RANK)],
        out_specs=out_rows,
        out_shape=jax.ShapeDtypeStruct((B, SA_HEADS, T, KV_RANK), BF16),
        scratch_shapes=[pltpu.VMEM((QBLOCK, t_pad), I32),
                        pltpu.VMEM((n_cg_max, 32, QBLOCK, 128), I32),
                        pltpu.VMEM((QBLOCK, t_pad), F32),
                        pltpu.VMEM((SA_HEADS * QBLOCK, 128), F32),
                        pltpu.VMEM((SA_HEADS * QBLOCK, 128), F32),
                        pltpu.VMEM((SA_HEADS * QBLOCK, KV_RANK), F32),
                        pltpu.VMEM((8, 128), F32),
                        pltpu.VMEM((SA_HEADS * QBLOCK, KV_RANK), BF16)],
        compiler_params=pltpu.CompilerParams(dimension_semantics=("arbitrary", "arbitrary"),
                                             vmem_limit_bytes=VMEM_LIMIT),
        name="dsa",
    )(qix, misc, kix, qlat, ckv)


def _first_max(v, idx, axis):
    m = jnp.max(v, axis=axis, keepdims=True)
    big = jnp.int32(2 ** 30)
    first = jnp.min(jnp.where(v == m, idx, big), axis=axis, keepdims=True)
    return m, idx == first


def _outproj_kernel(x_ref, odn_ref, olat_ref, uv_ref, wo_ref, gt_ref, sc_ref, sh_ref, g2_ref, rwt_ref,
                    rb_ref, lstrict_ref, ustrict_ref, x1_ref, h2_ref, posrow_ref, poscol_ref, gatecol_ref,
                    cpad_ref):
    tm = x_ref.shape[1]
    parts = [odn_ref[0]]
    for hd in range(SA_HEADS):
        parts.append(_dot(olat_ref[0, hd], uv_ref[hd]).astype(BF16))
    mix = jnp.concatenate(parts, axis=-1)
    x1 = x_ref[0] + gt_ref[0] * _dot(mix, wo_ref[...])
    x1_ref[0] = x1
    h2 = x1 * lax.rsqrt(jnp.mean(x1 * x1, axis=-1, keepdims=True) + EPS) * g2_ref[...]
    h2 = h2 * (1.0 + sc_ref[0]) + sh_ref[0]
    h2_ref[0] = h2.astype(BF16)

    per_g = N_EXPERTS // N_GROUPS
    s = jax.nn.sigmoid(_nt_dot(rwt_ref[...], h2, HIGHEST))
    choice = s + rb_ref[...]
    ig = lax.broadcasted_iota(I32, (per_g, tm), 0)
    gscore = []
    for gidx in range(N_GROUPS):
        cg = choice[gidx * per_g:(gidx + 1) * per_g]
        m1, hot1 = _first_max(cg, ig, 0)
        gscore.append(m1 + jnp.max(jnp.where(hot1, -jnp.inf, cg), axis=0, keepdims=True))
    gsel = [jnp.zeros((1, tm), jnp.bool_) for _ in range(N_GROUPS)]
    for _ in range(TOPK_GROUPS):
        best = functools.reduce(jnp.maximum, gscore)
        found = jnp.zeros((1, tm), jnp.bool_)
        for gidx in range(N_GROUPS):
            hot = (gscore[gidx] == best) & jnp.logical_not(found)
            found = found | hot
            gsel[gidx] = gsel[gidx] | hot
            gscore[gidx] = jnp.where(hot, -jnp.inf, gscore[gidx])
    masked = jnp.concatenate(
        [jnp.where(gsel[gidx], choice[gidx * per_g:(gidx + 1) * per_g], -jnp.inf) for gidx in range(N_GROUPS)],
        axis=0)
    ei = lax.broadcasted_iota(I32, masked.shape, 0)
    gate = jnp.zeros(masked.shape, F32)
    hots = []
    for _ in range(TOP_K):
        _, hot = _first_max(masked, ei, 0)
        hots.append(hot)
        gate = jnp.where(hot, s, gate)
        masked = jnp.where(hot, -jnp.inf, masked)
    gate = gate / jnp.sum(gate, axis=0, keepdims=True) * ROUTED_SCALE

    picked = jnp.where(functools.reduce(jnp.logical_or, hots), 1.0, 0.0)
    cnt = jnp.sum(picked, axis=1, keepdims=True)
    cpad = jnp.floor((cnt + (RUN_ALIGN - 1)) * (1.0 / RUN_ALIGN)) * RUN_ALIGN
    cpad_b = jnp.broadcast_to(cpad, (N_EXPERTS, GATE_W))
    lbase = _dot(lstrict_ref[...], cpad_b, HIGHEST)[:, :1]
    rank = _dot(picked.astype(BF16), ustrict_ref[...])
    pos = lbase + rank
    ri = lax.broadcasted_iota(I32, (GATE_W, tm), 0)
    pos_rows = jnp.zeros((GATE_W, tm), F32)
    gate_rows = jnp.zeros((GATE_W, tm), F32)
    for k, hot in enumerate(hots):
        pos_rows = jnp.where(ri == k, jnp.sum(jnp.where(hot, pos, 0.0), axis=0, keepdims=True), pos_rows)
        gate_rows = jnp.where(ri == k, jnp.sum(jnp.where(hot, gate, 0.0), axis=0, keepdims=True), gate_rows)
    posrow_ref[0, 0] = pos_rows[:TOP_K].astype(I32)
    poscol_ref[0] = pos_rows.T.astype(I32)
    gatecol_ref[0] = gate_rows.T
    cpad_ref[0, 0] = cpad_b.astype(I32)


def _outproj(x, odn, olat, uv, wo, gt1, sc2, sh2, g2, rwt, rb, tm):
    B, T, D = x.shape
    nt = T // tm
    ex = jnp.arange(N_EXPERTS)
    lstrict = (ex[:, None] > ex[None, :]).astype(F32)
    tok = jnp.arange(tm)
    ustrict = (tok[:, None] < tok[None, :]).astype(BF16)

    def full(a):
        nd = a.ndim
        return pl.BlockSpec(a.shape, lambda b, i, _n=nd: (0,) * _n)

    def rows(w):
        return pl.BlockSpec((1, tm, w), lambda b, i: (b, i, 0))

    def per_tile(h, w):
        return pl.BlockSpec((1, 1, h, w), lambda b, i: (b, i, 0, 0))

    per_b = pl.BlockSpec((1, 1, D), lambda b, i: (b, 0, 0))
    return pl.pallas_call(
        _outproj_kernel,
        grid=(B, nt),
        in_specs=[rows(D), rows(DN_V),
                  pl.BlockSpec((1, SA_HEADS, tm, KV_RANK), lambda b, i: (b, 0, i, 0)),
                  full(uv), full(wo), per_b, per_b, per_b,
                  full(g2), full(rwt), full(rb), full(lstrict), full(ustrict)],
        out_specs=[rows(D), rows(D), per_tile(TOP_K, tm), rows(GATE_W), rows(GATE_W),
                   per_tile(N_EXPERTS, GATE_W)],
        out_shape=[jax.ShapeDtypeStruct((B, T, D), F32), jax.ShapeDtypeStruct((B, T, D), BF16),
                   jax.ShapeDtypeStruct((B, nt, TOP_K, tm), I32),
                   jax.ShapeDtypeStruct((B, T, GATE_W), I32),
                   jax.ShapeDtypeStruct((B, T, GATE_W), F32),
                   jax.ShapeDtypeStruct((B, nt, N_EXPERTS, GATE_W), I32)],
        compiler_params=pltpu.CompilerParams(dimension_semantics=("arbitrary", "arbitrary"),
                                             vmem_limit_bytes=VMEM_LIMIT),
        name="outproj",
    )(x, odn, olat, uv, wo, gt1, sc2, sh2, g2, rwt, rb, lstrict, ustrict)


def _piece_sizes(max_rows):
    sizes = []
    z = RUN_ALIGN
    while z <= max_rows:
        sizes.append(z)
        z *= 2
    return sizes[::-1]


def _for_run_pieces(length, max_rows, fn, rare_from=None):
    def pieces(sizes):
        for z in sizes:
            start = length & ~(2 * z - 1)

            @pl.when((length & z) != 0)
            def _(start=start, z=z):
                fn(start, z)

    sizes = _piece_sizes(max_rows)
    rare = [z for z in sizes if rare_from is not None and z >= rare_from]
    if rare:
        pl.when(length >= rare_from)(lambda: pieces(rare))
    pieces([z for z in sizes if z not in rare])


def _plan_kernel(cp_ref, off_ref, lb_ref, foff_ref, flen_ref, blk_ref, nused_ref):
    cp = cp_ref[...].astype(F32)
    n, ne = cp.shape
    ei = lax.broadcasted_iota(I32, (ne, ne), 0)
    ej = lax.broadcasted_iota(I32, (ne, ne), 1)
    si = lax.broadcasted_iota(I32, (n, n), 0)
    sj = lax.broadcasted_iota(I32, (n, n), 1)
    lb = _dot(cp, (ei < ej).astype(F32), HIGHEST)
    earlier_tiles = _dot((sj < si).astype(F32), cp, HIGHEST)
    rows_e = jnp.sum(cp, axis=0, keepdims=True)
    region = jnp.floor((rows_e + (ROW_BLOCK - 1)) * (1.0 / ROW_BLOCK)) * ROW_BLOCK
    region_b = jnp.broadcast_to(region, (ne, ne))
    rend_row = _dot(region_b, (ei <= ej).astype(F32), HIGHEST)[:1]
    rend_col = jnp.sum(jnp.where(ej <= ei, region_b, 0.0), axis=1, keepdims=True)
    base = rend_row - region
    total = jnp.max(rend_row, axis=1, keepdims=True)
    off_ref[...] = (base + earlier_tiles).astype(I32)
    lb_ref[...] = lb.astype(I32)
    lane = lax.broadcasted_iota(I32, (1, GATE_W), 1)
    pad = jnp.zeros((1, GATE_W - ne), F32)
    foff_ref[...] = jnp.where(lane == ne, total, jnp.concatenate([base + rows_e, pad], axis=1)).astype(I32)
    flen_ref[...] = jnp.concatenate([region - rows_e, pad], axis=1).astype(I32)
    n_used = total * (1.0 / ROW_BLOCK)
    nused_ref[...] = jnp.broadcast_to(n_used, nused_ref.shape).astype(I32)
    bi = lax.broadcasted_iota(I32, (ne, blk_ref.shape[1]), 1).astype(F32)
    ended = jnp.where(rend_col * (1.0 / ROW_BLOCK) <= jnp.minimum(bi, n_used - 1.0), 1.0, 0.0)
    blk_ref[...] = jnp.minimum(jnp.sum(ended, axis=0, keepdims=True), ne - 1.0).astype(I32)


def _plan(cp, n_blocks):
    n, ne = cp.shape
    nb_pad = -(-n_blocks // 128) * 128
    return pl.pallas_call(
        _plan_kernel,
        out_shape=[jax.ShapeDtypeStruct((n, ne), I32), jax.ShapeDtypeStruct((n, ne), I32),
                   jax.ShapeDtypeStruct((1, GATE_W), I32), jax.ShapeDtypeStruct((1, GATE_W), I32),
                   jax.ShapeDtypeStruct((1, nb_pad), I32), jax.ShapeDtypeStruct((1, GATE_W), I32)],
        name="moe_plan",
    )(cp)


def _dispatch_kernel(off_ref, cp_ref, lb_ref, foff_ref, flen_ref, h_ref, posrow_ref, xs_hbm, buf, zbuf, sem,
                     zsem, *, n_steps, tile):
    s = pl.program_id(0)
    slot = s % 2
    jmax = buf.shape[1]

    def run_copies(step, slot_, act):
        def body(e, carry):
            idx = step * N_EXPERTS + e
            lb = lb_ref[idx]
            of = off_ref[idx]

            def piece(start, z):
                act(pltpu.make_async_copy(
                    buf.at[slot_, pl.ds(pl.multiple_of(lb + start, RUN_ALIGN), z)],
                    xs_hbm.at[pl.ds(pl.multiple_of(of + start, RUN_ALIGN), z)], sem.at[slot_]))

            _for_run_pieces(cp_ref[idx], tile, piece, rare_from=tile // 4)
            return carry

        lax.fori_loop(0, N_EXPERTS, body, 0)

    def wait_runs(step, slot_):
        last_ = step * N_EXPERTS + N_EXPERTS - 1
        _for_run_pieces(lb_ref[last_] + cp_ref[last_], jmax, lambda start, z: pltpu.make_async_copy(
            buf.at[slot_, pl.ds(0, z)], xs_hbm.at[pl.ds(0, z)], sem.at[slot_]).wait())

    def fill_copies(act):
        def body(e, carry):
            fo = foff_ref[e]

            def piece(start, z):
                act(pltpu.make_async_copy(
                    zbuf.at[pl.ds(0, z)], xs_hbm.at[pl.ds(pl.multiple_of(fo + start, RUN_ALIGN), z)], zsem.at[0]))

            _for_run_pieces(flen_ref[e], ROW_BLOCK // 2, piece)
            return carry

        lax.fori_loop(0, N_EXPERTS, body, 0)

        def tail(r, carry):
            act(pltpu.make_async_copy(
                zbuf, xs_hbm.at[pl.ds(pl.multiple_of(foff_ref[N_EXPERTS] + r * zbuf.shape[0], RUN_ALIGN),
                                      zbuf.shape[0])], zsem.at[0]))
            return carry

        lax.fori_loop(0, (xs_hbm.shape[0] - foff_ref[N_EXPERTS]) // zbuf.shape[0], tail, 0)

    @pl.when(s == 0)
    def _():
        zbuf[...] = jnp.zeros(zbuf.shape, BF16)
        fill_copies(lambda c: c.start())

    @pl.when(s >= 2)
    def _():
        wait_runs(s - 2, slot)

    h = h_ref[...]
    last = s * N_EXPERTS + N_EXPERTS - 1
    jused = lb_ref[last] + cp_ref[last]
    def local_rows(jc):
        ji = (lax.broadcasted_iota(I32, (MOE_CHUNK, tile), 0) + jc * MOE_CHUNK).astype(jnp.int16)
        p = jnp.zeros((MOE_CHUNK, tile), BF16)
        for k in range(TOP_K):
            p = jnp.where(ji == posrow_ref[0, k:k + 1, :].astype(jnp.int16), jnp.ones((), BF16), p)
        return _dot(p, h).astype(BF16)

    n_full = TOP_K * tile // MOE_CHUNK
    rows_full = [local_rows(jc) for jc in range(n_full)]
    for jc in range(n_full):
        buf[slot, jc * MOE_CHUNK:(jc + 1) * MOE_CHUNK, :] = rows_full[jc]
    for jc in range(n_full, jmax // MOE_CHUNK):
        @pl.when(jc * MOE_CHUNK < jused)
        def _(jc=jc):
            buf[slot, jc * MOE_CHUNK:(jc + 1) * MOE_CHUNK, :] = local_rows(jc)

    run_copies(s, slot, lambda c: c.start())

    @pl.when(s == n_steps - 1)
    def _():
        if n_steps >= 2:
            wait_runs(s - 1, 1 - slot)
        wait_runs(s, slot)
        fill_copies(lambda c: c.wait())


def _dispatch(h2, posrow, off, cp, lb, foff, flen, cap, tile, jmax):
    n_tok, D = h2.shape
    n_steps = n_tok // tile
    return pl.pallas_call(
        functools.partial(_dispatch_kernel, n_steps=n_steps, tile=tile),
        grid_spec=pltpu.PrefetchScalarGridSpec(
            num_scalar_prefetch=5,
            grid=(n_steps,),
            in_specs=[pl.BlockSpec((tile, D), lambda s, *_: (s, 0)),
                      pl.BlockSpec((1, TOP_K, tile), lambda s, *_: (s, 0, 0))],
            out_specs=pl.BlockSpec(memory_space=pl.ANY),
            scratch_shapes=[pltpu.VMEM((2, jmax, D), BF16), pltpu.VMEM((ROW_BLOCK // 2, D), BF16),
                            pltpu.SemaphoreType.DMA((2,)), pltpu.SemaphoreType.DMA((1,))]),
        out_shape=jax.ShapeDtypeStruct((cap, D), BF16),
        compiler_params=pltpu.CompilerParams(dimension_semantics=("arbitrary",), vmem_limit_bytes=VMEM_LIMIT),
        name="moe_dispatch",
    )(off, cp, lb, foff, flen, h2, posrow)


def _expert_kernel(blk_e_ref, nused_ref, xs_ref, wg_ref, wu_ref, wd_ref, ys_ref, wg_bf, wu_bf, wd_bf):
    i = pl.program_id(0)
    used = i < nused_ref[0]

    @pl.when(used & ((i == 0) | (blk_e_ref[i] != blk_e_ref[jnp.maximum(i - 1, 0)])))
    def _():
        wg_bf[...] = wg_ref[0].astype(BF16)
        wu_bf[...] = wu_ref[0].astype(BF16)
        wd_bf[...] = wd_ref[0].astype(BF16)

    @pl.when(used)
    def _():
        xb = xs_ref[...]
        a = _silu(_dot(xb, wg_bf[...])) * _dot(xb, wu_bf[...])
        ys_ref[...] = _dot(a.astype(BF16), wd_bf[...]).astype(BF16)

    @pl.when(jnp.logical_not(used))
    def _():
        ys_ref[...] = jnp.zeros(ys_ref.shape, BF16)


def _experts(xs, blk_e, n_used, wg, wu, wd):
    cap, D = xs.shape

    def row_block(i, be, nu):
        return (jnp.minimum(i, nu[0] - 1), 0)

    def out_block(i, be, nu):
        return (i, 0)

    def weight(i, be, nu):
        return (be[i], 0, 0)

    return pl.pallas_call(
        _expert_kernel,
        grid_spec=pltpu.PrefetchScalarGridSpec(
            num_scalar_prefetch=2,
            grid=(cap // ROW_BLOCK,),
            in_specs=[pl.BlockSpec((ROW_BLOCK, D), row_block),
                      pl.BlockSpec((1, D, D_EXPERT), weight), pl.BlockSpec((1, D, D_EXPERT), weight),
                      pl.BlockSpec((1, D_EXPERT, D), weight)],
            out_specs=pl.BlockSpec((ROW_BLOCK, D), out_block),
            scratch_shapes=[pltpu.VMEM((D, D_EXPERT), BF16), pltpu.VMEM((D, D_EXPERT), BF16),
                            pltpu.VMEM((D_EXPERT, D), BF16)]),
        out_shape=jax.ShapeDtypeStruct((cap, D), BF16),
        compiler_params=pltpu.CompilerParams(dimension_semantics=("arbitrary",), vmem_limit_bytes=VMEM_LIMIT),
        name="moe_experts",
    )(blk_e, n_used, xs, wg, wu, wd)


def _combine_kernel(off_ref, cp_ref, lb_ref, ys_hbm, poscol_ref, gatecol_ref, h_ref, sg_ref, su_ref, sd_ref,
                    x1_ref, gt_ref, fg_ref, o_ref, buf, sem, acc_ref, *, n_steps, tile, final_norm):
    s = pl.program_id(0)
    slot = s % 2
    jmax = buf.shape[1]

    def run_copies(step, slot_, act):
        def body(e, carry):
            idx = step * N_EXPERTS + e
            lb = lb_ref[idx]
            of = off_ref[idx]

            def piece(start, z):
                act(pltpu.make_async_copy(
                    ys_hbm.at[pl.ds(pl.multiple_of(of + start, RUN_ALIGN), z)],
                    buf.at[slot_, pl.ds(pl.multiple_of(lb + start, RUN_ALIGN), z)], sem.at[slot_]))

            _for_run_pieces(cp_ref[idx], tile, piece, rare_from=tile // 4)
            return carry

        lax.fori_loop(0, N_EXPERTS, body, 0)

    @pl.when(s == 0)
    def _():
        run_copies(0, 0, lambda c: c.start())

    @pl.when(s + 1 < n_steps)
    def _():
        run_copies(s + 1, 1 - slot, lambda c: c.start())

    hb = h_ref[...]
    shared = (_silu(_dot(hb, sg_ref[...])) * _dot(hb, su_ref[...])).astype(BF16)
    acc_ref[...] = _dot(shared, sd_ref[...])

    last = s * N_EXPERTS + N_EXPERTS - 1
    jused = lb_ref[last] + cp_ref[last]
    _for_run_pieces(jused, jmax, lambda start, z: pltpu.make_async_copy(
        ys_hbm.at[pl.ds(0, z)], buf.at[slot, pl.ds(0, z)], sem.at[slot]).wait())

    def zero_body(r, carry):
        buf[slot, pl.ds(pl.multiple_of(jused + r * RUN_ALIGN, RUN_ALIGN), RUN_ALIGN), :] = jnp.zeros(
            (RUN_ALIGN, buf.shape[2]), BF16)
        return carry

    chunk_end = (jused + MOE_CHUNK - 1) // MOE_CHUNK * MOE_CHUNK
    lax.fori_loop(0, (chunk_end - jused) // RUN_ALIGN, zero_body, 0)

    def gate_rows(jc):
        ji = (lax.broadcasted_iota(I32, (tile, MOE_CHUNK), 1) + jc * MOE_CHUNK).astype(jnp.int16)
        g = jnp.zeros((tile, MOE_CHUNK), BF16)
        for k in range(TOP_K):
            g = jnp.where(ji == poscol_ref[:, k:k + 1].astype(jnp.int16), gatecol_ref[:, k:k + 1].astype(BF16), g)
        return g

    n_full = TOP_K * tile // MOE_CHUNK
    g_full = jnp.concatenate([gate_rows(jc) for jc in range(n_full)], axis=1)
    acc_ref[...] += _dot(g_full, buf[slot, 0:n_full * MOE_CHUNK, :])
    for jc in range(n_full, jmax // MOE_CHUNK):
        @pl.when(jc * MOE_CHUNK < jused)
        def _(jc=jc):
            acc_ref[...] += _dot(gate_rows(jc), buf[slot, jc * MOE_CHUNK:(jc + 1) * MOE_CHUNK, :])

    y = x1_ref[...] + gt_ref[0] * acc_ref[...]
    if final_norm:
        y = y * lax.rsqrt(jnp.mean(y * y, axis=-1, keepdims=True) + EPS) * fg_ref[...]
    o_ref[...] = y


def _combine(ys, poscol, gatecol, h2, sg, su, sd, x1, gt2, fg, off, cp, lb, tile, jmax, tiles_per_batch,
             final_norm):
    n_tok, D = h2.shape
    n_steps = n_tok // tile

    def full(a):
        nd = a.ndim
        return pl.BlockSpec(a.shape, lambda s, *_, _n=nd: (0,) * _n)

    def rows(w):
        return pl.BlockSpec((tile, w), lambda s, *_: (s, 0))

    return pl.pallas_call(
        functools.partial(_combine_kernel, n_steps=n_steps, tile=tile, final_norm=final_norm),
        grid_spec=pltpu.PrefetchScalarGridSpec(
            num_scalar_prefetch=3,
            grid=(n_steps,),
            in_specs=[pl.BlockSpec(memory_space=pl.ANY), rows(GATE_W), rows(GATE_W), rows(D),
                      full(sg), full(su), full(sd), rows(D),
                      pl.BlockSpec((1, 1, D), lambda s, *_: (s // tiles_per_batch, 0, 0)), full(fg)],
            out_specs=rows(D),
            scratch_shapes=[pltpu.VMEM((2, jmax, D), BF16), pltpu.SemaphoreType.DMA((2,)),
                            pltpu.VMEM((tile, D), F32)]),
        out_shape=jax.ShapeDtypeStruct((n_tok, D), F32),
        compiler_params=pltpu.CompilerParams(dimension_semantics=("arbitrary",), vmem_limit_bytes=VMEM_LIMIT),
        name="moe_combine",
    )(off, cp, lb, ys, poscol, gatecol, h2, sg, su, sd, x1, gt2, fg)


def _moe(h2, posrow, poscol, gatecol, cpad, wg, wu, wd, sg, su, sd, x1, gt2, fg, tile, final_norm):
    B, T, D = x1.shape
    n_tok = B * T
    n_tiles = n_tok // tile
    jmax = -(-(TOP_K * tile + N_EXPERTS * (RUN_ALIGN - 1)) // MOE_CHUNK) * MOE_CHUNK
    cap = -(-(TOP_K * n_tok + n_tiles * N_EXPERTS * (RUN_ALIGN - 1) + N_EXPERTS * (ROW_BLOCK - RUN_ALIGN))
            // ROW_BLOCK) * ROW_BLOCK

    cp = cpad[..., 0].reshape(n_tiles, N_EXPERTS)
    off, lb, foff, flen, blk_e, n_used = _plan(cp, cap // ROW_BLOCK)
    flat = lambda a: a.reshape(-1)

    xs = _dispatch(h2.reshape(n_tok, D), posrow.reshape(n_tiles, TOP_K, tile), flat(off), flat(cp), flat(lb),
                   foff[0], flen[0], cap, tile, jmax)
    ys = _experts(xs, blk_e[0], n_used[0], wg, wu, wd)
    out = _combine(ys, poscol.reshape(n_tok, GATE_W), gatecol.reshape(n_tok, GATE_W), h2.reshape(n_tok, D),
                   sg, su, sd, x1.reshape(n_tok, D), gt2, fg, flat(off), flat(cp), flat(lb), tile, jmax,
                   T // tile, final_norm)
    return out.reshape(B, T, D)


def _misc_lanes(vec, start):
    return jnp.zeros((1, MISC_W), F32).at[0, start:start + vec.shape[0]].set(vec.astype(F32))


def kernel(x, c, ada_w, ada_b, norm1_g, w_in, conv_w, a_log, dt_bias, dn_norm_g, kv_norm_g, w_uk, w_uv,
           idx_k_ln_g, idx_k_ln_b, w_out, norm2_g, router_w, router_b, exp_w_gate, exp_w_up, exp_w_down,
           sh_w_gate, sh_w_up, sh_w_down, final_g):
    B, T, D = x.shape
    depth = ada_w.shape[0]
    topk = min(IDX_TOPK_MAX, T // 4)
    tm = min(512, T)
    r_dn = min(256, T)

    cond_in = jnp.zeros((8, D), F32).at[:B].set(c)
    pos = jnp.arange(tm)
    tri = ((pos[:, None] // CHUNK == pos[None, :] // CHUNK) & (pos[:, None] >= pos[None, :])).astype(F32)

    for l in range(depth):
        mod = _ada(cond_in, ada_w[l], ada_b[l][None, :])[:B]
        sh1, sc1, gt1, sh2, sc2, gt2 = [m[:, None, :] for m in jnp.split(mod, 6, axis=-1)]

        offs = [0]
        for s in (DN_QK, DN_QK, DN_V, DN_V, DN_HEADS, DN_HEADS, SA_Q, KV_RANK, IDX_Q, IDX_DIM, IDX_HEADS):
            offs.append(offs[-1] + s)
        w = w_in[l]
        wc = w[:, offs[0]:offs[3]].astype(BF16)
        wz = w[:, offs[3]:offs[4]].astype(BF16)
        wq = w[:, offs[6]:offs[7]].astype(BF16)
        wkv = w[:, offs[7]:offs[8]].astype(BF16)
        wqi = w[:, offs[8]:offs[9]].astype(BF16)
        wm = jnp.concatenate([w[:, offs[9]:offs[10]], w[:, offs[4]:offs[5]], w[:, offs[5]:offs[6]],
                              w[:, offs[10]:offs[11]],
                              jnp.zeros((D, MISC_W - IDX_DIM - 2 * DN_HEADS - IDX_HEADS), F32)],
                             axis=1).astype(BF16)
        ukt = jnp.swapaxes(w_uk[l], 1, 2).astype(BF16)

        q, k, v, z, qlat, ckv, qix, kix, misc = _inproj(
            x, sc1, sh1, norm1_g[l][None, :], wc, wz, wq, wkv, wqi, wm, conv_w[l], ukt,
            kv_norm_g[l][None, :], _misc_lanes(idx_k_ln_g[l], M_KIX), _misc_lanes(idx_k_ln_b[l], M_KIX),
            _misc_lanes(a_log[l], M_A), _misc_lanes(dt_bias[l], M_A), tri, tm)

        odn = _deltanet(q, k, v, z, misc, dn_norm_g[l][None, :], r_dn)
        olat = _dsa(qix, misc, kix, qlat, ckv, topk)

        x1, h2, posrow, poscol, gatecol, cpad = _outproj(
            x, odn, olat, w_uv[l].astype(BF16), w_out[l].astype(BF16), gt1, sc2, sh2,
            norm2_g[l][None, :], router_w[l].T, router_b[l][:, None], tm)

        x = _moe(h2, posrow, poscol, gatecol, cpad, exp_w_gate[l], exp_w_up[l], exp_w_down[l],
                 sh_w_gate[l].astype(BF16), sh_w_up[l].astype(BF16),
                 sh_w_down[l].astype(BF16), x1, gt2, final_g[None, :], tm, l == depth - 1)
    return x
```

```python
import functools

import jax
import jax.numpy as jnp
from jax import lax
from jax.experimental import pallas as pl
from jax.experimental.pallas import tpu as pltpu

F32 = jnp.float32
BF16 = jnp.bfloat16
I32 = jnp.int32
HIGHEST = lax.Precision.HIGHEST

EPS = 1e-6
CHUNK = 64
DN_HEADS = 4
DN_DK = 128
DN_DV = 128
CONV_K = 4
SA_HEADS = 4
SA_DQK = 128
SA_DV = 128
KV_RANK = 256
IDX_HEADS = 4
IDX_DIM = 64
IDX_TOPK_MAX = 256
SM_SCALE = SA_DQK ** -0.5
LOG2E = 1.4426950408889634
IDX_W_SCALE = (IDX_HEADS * IDX_DIM) ** -0.5
N_EXPERTS = 64
TOP_K = 8
N_GROUPS = 8
TOPK_GROUPS = 4
D_EXPERT = 256
ROUTED_SCALE = 2.5
GATE_W = 128
RUN_ALIGN = 16
ROW_BLOCK = 1024
MOE_CHUNK = 512

DN_QK = DN_HEADS * DN_DK
DN_V = DN_HEADS * DN_DV
CONV_DIM = 2 * DN_QK + DN_V
SA_Q = SA_HEADS * SA_DQK
IDX_Q = IDX_HEADS * IDX_DIM

MISC_W = 128
M_KIX = 0
M_BETA = IDX_DIM
M_A = M_BETA + DN_HEADS
M_WIX = M_A + DN_HEADS

DN_SUB = 2 * CHUNK
QBLOCK = 256
SOFTMAX_TINY = 2.0 ** -100
SHIFT_SAFE = 49.0
KEY_TILE = 1024
PLANE_COLS = 32 * 128
PLANE_SHIFT = 12
PLANE_ROWS = 16
INT_MIN = -2 ** 31
NEG_BIG = -1e30
VMEM_LIMIT = 56 * 1024 * 1024


def _nt_dot(a, b, precision=None):
    return lax.dot_general(a, b, (((1,), (1,)), ((), ())), preferred_element_type=F32,
                           precision=precision)


def _dot(a, b, precision=None):
    return jnp.dot(a, b, preferred_element_type=F32, precision=precision)


def _silu(x):
    return x * jax.nn.sigmoid(x)


def _softplus(x):
    return jnp.maximum(x, 0.0) + jnp.log(1.0 + jnp.exp(-jnp.abs(x)))


def _ada_kernel(c_ref, w_ref, b_ref, o_ref):
    cond = _silu(c_ref[...])
    o_ref[...] = _dot(cond, w_ref[...], HIGHEST) + b_ref[...]


def _ada(c_pad, ada_w, ada_b):
    rows, d = c_pad.shape
    n_out = ada_w.shape[1]
    return pl.pallas_call(
        _ada_kernel,
        grid=(n_out // d,),
        in_specs=[pl.BlockSpec((rows, d), lambda j: (0, 0)),
                  pl.BlockSpec((d, d), lambda j: (0, j)),
                  pl.BlockSpec((1, d), lambda j: (0, j))],
        out_specs=pl.BlockSpec((rows, d), lambda j: (0, j)),
        out_shape=jax.ShapeDtypeStruct((rows, n_out), F32),
        compiler_params=pltpu.CompilerParams(vmem_limit_bytes=VMEM_LIMIT),
        name="ada",
    )(c_pad, ada_w, ada_b)


def _inproj_kernel(x_ref, sc_ref, sh_ref, g1_ref, wc_ref, wz_ref, wq_ref, wkv_ref, wqi_ref, wm_ref,
                   convw_ref, ukt_ref, kvg_ref, lng_ref, lnb_ref, alog_ref, dtb_ref, tri_ref,
                   q_ref, k_ref, v_ref, z_ref, qlat_ref, ckv_ref, qix_ref, kix_ref, misc_ref,
                   conv_buf):
    tm = x_ref.shape[1]
    i = pl.program_id(1)

    x = x_ref[0]
    h = x * lax.rsqrt(jnp.mean(x * x, axis=-1, keepdims=True) + EPS) * g1_ref[...]
    h = h * (1.0 + sc_ref[0]) + sh_ref[0]
    hb = h.astype(BF16)

    @pl.when(i == 0)
    def _():
        conv_buf[0:8, :] = jnp.zeros((8, CONV_DIM), F32)

    conv_buf[8:8 + tm, :] = _dot(hb, wc_ref[...])
    for grp, dst in ((0, q_ref), (1, k_ref), (2, v_ref)):
        cols = slice(grp * DN_QK, (grp + 1) * DN_QK)
        y = jnp.zeros((tm, DN_QK), F32)
        for j in range(CONV_K):
            y = y + convw_ref[j:j + 1, cols] * conv_buf[8 - (CONV_K - 1) + j:8 - (CONV_K - 1) + j + tm, cols]
        y = _silu(y)
        if grp < 2:
            outs = []
            for hd in range(DN_HEADS):
                yh = y[:, hd * DN_DK:(hd + 1) * DN_DK]
                yh = yh * lax.rsqrt(jnp.sum(yh * yh, axis=-1, keepdims=True) + EPS)
                if grp == 0:
                    yh = yh * (DN_DK ** -0.5)
                outs.append(yh)
            y = jnp.concatenate(outs, axis=-1)
        dst[0] = y
    conv_buf[0:8, :] = conv_buf[tm:tm + 8, :]

    z_ref[0] = _dot(hb, wz_ref[...])

    q_sa = _dot(hb, wq_ref[...]).astype(BF16)
    for hd in range(SA_HEADS):
        ql = _dot(q_sa[:, hd * SA_DQK:(hd + 1) * SA_DQK], ukt_ref[hd]) * (SM_SCALE * LOG2E)
        qlat_ref[0, hd] = ql.astype(BF16)

    ckv = _dot(hb, wkv_ref[...])
    ckv = ckv * lax.rsqrt(jnp.mean(ckv * ckv, axis=-1, keepdims=True) + EPS) * kvg_ref[...]
    ckv_ref[0] = ckv.astype(BF16)

    q_ix = _dot(hb, wqi_ref[...]).astype(BF16)
    for hd in range(IDX_HEADS):
        qix_ref[0, hd] = q_ix[:, hd * IDX_DIM:(hd + 1) * IDX_DIM]

    m = _dot(hb, wm_ref[...])
    lane = lax.broadcasted_iota(I32, (tm, MISC_W), 1)
    is_k = lane < IDX_DIM
    mu = jnp.sum(jnp.where(is_k, m, 0.0), axis=-1, keepdims=True) * (1.0 / IDX_DIM)
    kc = jnp.where(is_k, m - mu, 0.0)
    var = jnp.sum(kc * kc, axis=-1, keepdims=True) * (1.0 / IDX_DIM)
    kn = kc * lax.rsqrt(var + EPS) * lng_ref[...] + lnb_ref[...]
    kix_ref[0] = kn[:, :IDX_DIM].astype(BF16)

    beta = jax.nn.sigmoid(m)
    g = -jnp.exp(alog_ref[...]) * _softplus(m + dtb_ref[...])
    is_a = (lane >= M_A) & (lane < M_A + DN_HEADS)
    g = jnp.where(is_a, g, 0.0)
    gc = _dot(tri_ref[...], g, HIGHEST)
    is_b = (lane >= M_BETA) & (lane < M_BETA + DN_HEADS)
    is_w = (lane >= M_WIX) & (lane < M_WIX + IDX_HEADS)
    misc_ref[0] = jnp.where(is_b, beta, jnp.where(is_a, gc, jnp.where(is_w, m * IDX_W_SCALE, 0.0)))


def _inproj(x, sc1, sh1, g1, wc, wz, wq, wkv, wqi, wm, conv_w, ukt, kvg, lng, lnb, alog, dtb, tri, tm):
    B, T, D = x.shape
    nt = T // tm

    def full(a):
        nd = a.ndim
        return pl.BlockSpec(a.shape, lambda b, i, _n=nd: (0,) * _n)

    def rows(w):
        return pl.BlockSpec((1, tm, w), lambda b, i: (b, i, 0))

    per_b = pl.BlockSpec((1, 1, D), lambda b, i: (b, 0, 0))
    def head_rows(h, w):
        return pl.BlockSpec((1, h, tm, w), lambda b, i: (b, 0, i, 0))

    outs = [(None, DN_QK, F32), (None, DN_QK, F32), (None, DN_V, F32), (None, DN_V, F32),
            (SA_HEADS, KV_RANK, BF16), (None, KV_RANK, BF16), (IDX_HEADS, IDX_DIM, BF16),
            (None, IDX_DIM, BF16), (None, MISC_W, F32)]
    return pl.pallas_call(
        _inproj_kernel,
        grid=(B, nt),
        in_specs=[rows(D), per_b, per_b, full(g1), full(wc), full(wz), full(wq), full(wkv), full(wqi),
                  full(wm), full(conv_w), full(ukt), full(kvg), full(lng), full(lnb), full(alog),
                  full(dtb), full(tri)],
        out_specs=[rows(w) if h is None else head_rows(h, w) for h, w, _ in outs],
        out_shape=[jax.ShapeDtypeStruct((B, T, w) if h is None else (B, h, T, w), dt) for h, w, dt in outs],
        scratch_shapes=[pltpu.VMEM((tm + 8, CONV_DIM), F32)],
        compiler_params=pltpu.CompilerParams(dimension_semantics=("arbitrary", "arbitrary"),
                                             vmem_limit_bytes=VMEM_LIMIT),
        name="inproj",
    )(x, sc1, sh1, g1, wc, wz, wq, wkv, wqi, wm, conv_w, ukt, kvg, lng, lnb, alog, dtb, tri)


def _deltanet_kernel(q_ref, k_ref, v_ref, z_ref, misc_ref, ng_ref, o_ref, s_ref):
    R = q_ref.shape[1]
    n_chunks = R // CHUNK

    @pl.when(pl.program_id(1) == 0)
    def _():
        s_ref[...] = jnp.zeros(s_ref.shape, F32)

    misc = misc_ref[0]
    misc_t = misc.T
    SB = min(DN_SUB, R)
    row = lax.broadcasted_iota(I32, (SB, SB), 0)
    col = lax.broadcasted_iota(I32, (SB, SB), 1)
    same = (row // CHUNK) == (col // CHUNK)
    lower = same & (row >= col)
    strict = same & (row > col)
    eye = (row == col).astype(F32)

    def mm(a, b):
        return _dot(a.astype(BF16), b.astype(BF16))

    def mm3(a, b):
        ah = a.astype(BF16)
        bh = b.astype(BF16)
        al = (a - ah.astype(F32)).astype(BF16)
        bl = (b - bh.astype(F32)).astype(BF16)
        return _dot(jnp.concatenate([ah, ah, al], axis=1), jnp.concatenate([bh, bl, bh], axis=0))

    heads = range(DN_HEADS)
    subs = range(R // SB)
    chains = [(hd, sb) for hd in heads for sb in subs]
    cols = [slice(hd * DN_DK, (hd + 1) * DN_DK) for hd in heads]
    qh = [q_ref[0, :, cols[hd]] for hd in heads]
    kh = [k_ref[0, :, cols[hd]] for hd in heads]
    beta = [misc[:, M_BETA + hd:M_BETA + hd + 1] for hd in heads]
    gc_c = [misc[:, M_A + hd:M_A + hd + 1] for hd in heads]
    eg = [jnp.exp(gc_c[hd]) for hd in heads]
    kb = [kh[hd] * beta[hd] for hd in heads]
    rhs = [jnp.concatenate([v_ref[0, :, cols[hd]] * beta[hd], kb[hd] * eg[hd]], axis=-1) for hd in heads]
    q_dec = [qh[hd] * eg[hd] for hd in heads]

    def rows_of(sb):
        return slice(sb * SB, (sb + 1) * SB)

    decay, a, qk_sb = {}, {}, {}
    for hd, sb in chains:
        bs = rows_of(sb)
        gc_r = misc_t[M_A + hd:M_A + hd + 1, bs]
        decay[hd, sb] = jnp.where(lower, jnp.exp(jnp.where(lower, gc_c[hd][bs] - gc_r, 0.0)), 0.0)
    for hd, sb in chains:
        bs = rows_of(sb)
        khb = kh[hd][bs].astype(BF16)
        a[hd, sb] = jnp.where(strict, _nt_dot(kb[hd][bs].astype(BF16), khb) * decay[hd, sb], 0.0)
        qk_sb[hd, sb] = jnp.where(lower, _nt_dot(qh[hd][bs].astype(BF16), khb) * decay[hd, sb], 0.0)
    p = {ch: eye - a[ch] for ch in chains}
    xp = {ch: mm3(a[ch], a[ch]) for ch in chains}
    n_sq = 1
    while True:
        p = {ch: p[ch] + mm3(p[ch], xp[ch]) for ch in chains}
        n_sq *= 2
        if n_sq * 2 >= CHUNK:
            break
        xp = {ch: mm3(xp[ch], xp[ch]) for ch in chains}
    sol = {(hd, sb): mm(p[hd, sb], rhs[hd][rows_of(sb)]) for hd, sb in chains}

    def chunk_of(c):
        per = SB // CHUNK
        return c // per, slice((c % per) * CHUNK, (c % per + 1) * CHUNK)

    s = [s_ref[hd] for hd in heads]
    o_parts = [[] for _ in heads]
    for c in range(n_chunks):
        rs = slice(c * CHUNK, (c + 1) * CHUNK)
        sb, r = chunk_of(c)
        gl = [gc_c[hd][(c + 1) * CHUNK - 1:(c + 1) * CHUNK, :] for hd in heads]
        k_dec = [kh[hd][rs] * jnp.exp(gl[hd] - gc_c[hd][rs]) for hd in heads]
        v_new = [sol[hd, sb][r, :DN_DV] - mm(sol[hd, sb][r, DN_DV:], s[hd]) for hd in heads]
        for hd in heads:
            o_parts[hd].append(mm(q_dec[hd][rs], s[hd]) + mm(qk_sb[hd, sb][r, r], v_new[hd]))
        s = [s[hd] * jnp.exp(gl[hd]) + mm(k_dec[hd].T, v_new[hd]) for hd in heads]
    for hd in heads:
        s_ref[hd] = s[hd]
        o = jnp.concatenate(o_parts[hd], axis=0)
        o = o * lax.rsqrt(jnp.mean(o * o, axis=-1, keepdims=True) + EPS) * ng_ref[...]
        o_ref[0, :, cols[hd]] = (o * _silu(z_ref[0, :, cols[hd]])).astype(BF16)


def _deltanet(q, k, v, z, misc, ng, R):
    B, T, _ = q.shape

    def rows(w):
        return pl.BlockSpec((1, R, w), lambda b, i: (b, i, 0))

    return pl.pallas_call(
        _deltanet_kernel,
        grid=(B, T // R),
        in_specs=[rows(DN_QK), rows(DN_QK), rows(DN_V), rows(DN_V), rows(MISC_W),
                  pl.BlockSpec((1, DN_DV), lambda b, i: (0, 0))],
        out_specs=rows(DN_V),
        out_shape=jax.ShapeDtypeStruct((B, T, DN_V), BF16),
        scratch_shapes=[pltpu.VMEM((DN_HEADS, DN_DK, DN_DV), F32)],
        compiler_params=pltpu.CompilerParams(dimension_semantics=("arbitrary", "arbitrary"),
                                             vmem_limit_bytes=VMEM_LIMIT),
        name="deltanet",
    )(q, k, v, z, misc, ng)


def _bit_transpose32(words):
    w = list(words)
    j = 16
    m = 0x0000FFFF
    while j:
        k = 0
        m_i32 = jnp.int32(m - (1 << 32) if m >= (1 << 31) else m)
        while k < 32:
            t = (w[k] ^ lax.shift_right_logical(w[k + j], jnp.full_like(w[k], j))) & m_i32
            w[k] = w[k] ^ t
            w[k + j] = w[k + j] ^ jnp.left_shift(t, j)
            k = (k + j + 1) & ~j
        j >>= 1
        m = (m ^ (m << j)) & 0xFFFFFFFF
    return w


def _dsa_kernel(qix_ref, misc_ref, kix_ref, qlat_ref, ckv_ref, o_ref, keys_ref, planes_ref, bias_ref,
                mx_ref, l_ref, acc_ref, kvmax_ref, qprev_ref, floor_ref, mode_ref, *, topk, pos_bits, n_cg_max):
    g = pl.program_id(1)
    n_blocks = pl.num_programs(1) - 1
    i = g
    QB = QBLOCK
    KT = KEY_TILE
    has_front = g < n_blocks
    has_back = g >= 1
    n_kt = jnp.where(has_front, (i * QB + QB + KT - 1) // KT, 0)
    n_kt_back = (g * QB + KT - 1) // KT

    rowi = lax.broadcasted_iota(I32, (QB, KT), 0)
    coli = lax.broadcasted_iota(I32, (QB, KT), 1)
    limit = i * QB + (rowi // CHUNK + 1) * CHUNK

    misc = misc_ref[0]
    qix = qix_ref[0].reshape(IDX_HEADS * QB, IDX_DIM)
    q_st = qlat_ref[0].reshape(SA_HEADS * QB, KV_RANK)

    def score_body(masked, kt, carry):
        k0 = pl.multiple_of(kt * KT, KT)
        kx = kix_ref[0, pl.ds(k0, KT), :]
        rel = jnp.maximum(_nt_dot(qix, kx), 0.0)
        sc = jnp.zeros((QB, KT), F32)
        for hd in range(IDX_HEADS):
            sc = sc + misc[:, M_WIX + hd:M_WIX + hd + 1] * rel[hd * QB:(hd + 1) * QB]
        sc = jnp.where(sc == 0.0, 0.0, sc)
        bits = pltpu.bitcast(sc, I32)
        key = jnp.where(bits < 0, bits ^ 0x7FFFFFFF, bits)
        keys_ref[:, pl.ds(k0, KT)] = jnp.where(k0 + coli < limit, key, INT_MIN) if masked else key
        return carry

    @pl.when(g == 0)
    def _():
        mode_ref[0] = 0

    def logit_chunks(from_bias, k0):
        kv = ckv_ref[0, pl.ds(k0, KT), :]
        s = _nt_dot(qprev_ref[...], kv)
        chunks = []
        for j in range(KT // 128):
            cols = pl.ds(pl.multiple_of(k0 + j * 128, 128), 128)
            bj = bias_ref[:, cols] if from_bias else jnp.where(keys_ref[:, cols] >= floor_ref[...], 0.0, NEG_BIG)
            chunks.append(s[:, j * 128:(j + 1) * 128] + jnp.concatenate([bj] * SA_HEADS, axis=0))
        return kv, chunks

    def pv_body(from_bias, kt, carry):
        kv, chunks = logit_chunks(from_bias, pl.multiple_of(kt * KT, KT))
        shift = mx_ref[...]
        ps = [jnp.exp2(c - shift) for c in chunks]
        l_ref[...] = functools.reduce(jnp.add, ps, l_ref[...])
        p = jnp.concatenate([pj.astype(BF16) for pj in ps], axis=1)
        acc_ref[...] += _dot(p, kv)
        return carry

    def clear_sums():
        l_ref[...] = jnp.zeros(l_ref.shape, F32)
        acc_ref[...] = jnp.zeros(acc_ref.shape, F32)

    def write_back_block():
        l_row = jnp.sum(l_ref[...], axis=-1, keepdims=True)
        o_ref[0] = (acc_ref[...] / l_row).astype(BF16).reshape(SA_HEADS, QB, KV_RANK)
        return l_row

    pl.when(has_back)(clear_sums)

    n_open = jnp.where(has_front, (i * QB + CHUNK) // KT, 0)
    n_fused = jnp.minimum(n_open, n_kt_back)

    def fused_body(from_bias, kt, carry):
        pv_body(from_bias, kt, carry)
        return score_body(False, kt, carry)

    def sweep_loops(from_bias):
        lax.fori_loop(0, n_fused, functools.partial(fused_body, from_bias), 0)
        lax.fori_loop(n_fused, n_kt_back, functools.partial(pv_body, from_bias), 0)

    lax.cond(mode_ref[0] == 0, functools.partial(sweep_loops, False), functools.partial(sweep_loops, True))
    lax.fori_loop(n_fused, n_open, functools.partial(score_body, False), 0)
    lax.fori_loop(n_open, n_kt, functools.partial(score_body, True), 0)

    @pl.when(has_back)
    def _():
        l_fast = write_back_block()

        @pl.when((jnp.min(l_fast) < SOFTMAX_TINY) & (mode_ref[0] != 0))
        def _():
            mx_ref[...] = jnp.full(mx_ref.shape, NEG_BIG, F32)

            def max_body(kt, carry):
                _, chunks = logit_chunks(True, pl.multiple_of(kt * KT, KT))
                mx_ref[...] = functools.reduce(jnp.maximum, chunks, mx_ref[...])
                return carry

            lax.fori_loop(0, n_kt_back, max_body, 0)
            mx_ref[...] = jnp.broadcast_to(jnp.max(mx_ref[...], axis=-1, keepdims=True), mx_ref.shape)
            clear_sums()
            lax.fori_loop(0, n_kt_back, functools.partial(pv_body, True), 0)
            write_back_block()

    n_cg = (n_kt * KT + PLANE_COLS - 1) // PLANE_COLS

    def fill_body(kt, carry):
        keys_ref[:, pl.ds(pl.multiple_of(kt * KT, KT), KT)] = jnp.full((QB, KT), INT_MIN, I32)
        return carry

    lax.fori_loop(n_kt, n_cg * (PLANE_COLS // KT), fill_body, 0)

    @pl.when(i == 0)
    def _():
        planes_ref[...] = jnp.zeros(planes_ref.shape, I32)

    def plane_body(step, carry):
        c = step // (QB // PLANE_ROWS)
        r0 = pl.multiple_of((step % (QB // PLANE_ROWS)) * PLANE_ROWS, PLANE_ROWS)
        words = [keys_ref[pl.ds(r0, PLANE_ROWS), pl.ds(pl.multiple_of(c * PLANE_COLS + j * 128, 128), 128)]
                 for j in range(32)]
        for b, plane in enumerate(_bit_transpose32(words)):
            planes_ref[c, b, pl.ds(r0, PLANE_ROWS), :] = ~plane if b == 0 else plane
        return carry

    lax.fori_loop(0, n_cg * (QB // PLANE_ROWS), plane_body, 0)

    ones_mat = jnp.ones((128, 128), BF16)

    def lane_count(words):
        pc = functools.reduce(jnp.add, [lax.population_count(x) for x in words])
        return _dot(pc.astype(F32).astype(BF16), ones_mat).astype(I32)

    def sel_body(groups, step, carry):
        cand, n_gt, tau_u = carry
        hi = [cand[c] & planes_ref[c, 2 * step] for c in groups]
        lo = [cand[c] ^ hi[c] for c in groups]
        d3 = [hi[c] & planes_ref[c, 2 * step + 1] for c in groups]
        d2 = [hi[c] ^ d3[c] for c in groups]
        d1 = [lo[c] & planes_ref[c, 2 * step + 1] for c in groups]
        d0 = [lo[c] ^ d1[c] for c in groups]
        a3 = n_gt + lane_count(d3)
        a2 = a3 + lane_count(d2)
        a1 = a2 + lane_count(d1)
        is3 = a3 >= topk
        is2 = a2 >= topk
        is1 = a1 >= topk
        cand = tuple(jnp.where(is3, d3[c], jnp.where(is2, d2[c], jnp.where(is1, d1[c], d0[c]))) for c in groups)
        n_gt = jnp.where(is3, n_gt, jnp.where(is2, a3, jnp.where(is1, a2, a1)))
        digit = jnp.where(is3, 3, jnp.where(is2, 2, jnp.where(is1, 1, 0)))
        return cand, n_gt, tau_u | jnp.left_shift(digit, 30 - 2 * step)

    def radix_select(n_groups):
        groups = range(n_groups)
        start = (tuple(jnp.full((QB, 128), -1, I32) for _ in groups), jnp.zeros((QB, 128), I32),
                 jnp.zeros((QB, 128), I32))
        cand_, n_gt_, tau_ = lax.fori_loop(0, 16, functools.partial(sel_body, groups), start)
        rest = tuple(jnp.zeros((QB, 128), I32) for _ in range(n_cg_max - n_groups))
        return cand_ + rest, n_gt_, tau_

    cand, n_gt, tau_u = lax.switch(n_cg - 1, [functools.partial(radix_select, g + 1) for g in range(n_cg_max)])
    tau = tau_u ^ INT_MIN
    sentinel = tau == INT_MIN
    cand = tuple(jnp.where(sentinel, 0, cand[c]) for c in range(n_cg_max))
    need = topk - n_gt
    any_tie = jnp.max(jnp.where(lane_count(cand) > need, 1, 0)) > 0

    @pl.when(i == 0)
    def _():
        def norm_body(r, best):
            x = ckv_ref[0, pl.ds(pl.multiple_of(r * KT, KT), KT), :].astype(F32)
            return jnp.maximum(best, jnp.max(jnp.sum(x * x, axis=1, keepdims=True), axis=0, keepdims=True))

        n_rows = ckv_ref.shape[1]
        kv_sq = lax.fori_loop(0, n_rows // KT, norm_body, jnp.zeros((1, 1), F32))
        kvmax_ref[...] = jnp.broadcast_to(jnp.sqrt(kv_sq), kvmax_ref.shape)

    qf = q_st.astype(F32)
    q_norm = jnp.sqrt(jnp.sum(qf * qf, axis=1, keepdims=True))
    shift = jnp.broadcast_to(q_norm, mx_ref.shape) * kvmax_ref[0:1, :] * 1.001 + 1e-3
    mx_ref[...] = shift
    qprev_ref[...] = q_st

    floor_ref[...] = jnp.where(sentinel, INT_MIN + 1, tau)
    need_bias = any_tie | (jnp.max(shift) > SHIFT_SAFE)
    mode_ref[0] = jnp.where(need_bias, 1, 0)
    lane = lax.broadcasted_iota(I32, (QB, 128), 1)

    def tie_search():
        def pos_mask(p, c):
            cg = lax.shift_right_logical(p, jnp.full_like(p, PLANE_SHIFT))
            j0 = lax.shift_right_logical(p, jnp.full_like(p, 7)) & 31
            below = ~lax.shift_right_logical(jnp.full_like(p, -1), j0)
            bit = lax.shift_right_logical(jnp.full_like(p, INT_MIN), j0)
            word = below | jnp.where(lane < (p & 127), bit, 0)
            return jnp.where(cg > c, -1, jnp.where(cg == c, word, 0))

        def pos_body(b, q):
            cq = q + jnp.left_shift(jnp.int32(1), pos_bits - 1 - b)
            cnt = lane_count([cand[c] & pos_mask(cq, c) for c in range(n_cg_max)])
            return jnp.where(cnt < need, cq, q)

        return lax.fori_loop(0, pos_bits, pos_body, jnp.zeros((QB, 128), I32)) + 1

    @pl.when(need_bias)
    def _():
        pstar = lax.cond(any_tie, tie_search, lambda: jnp.full((QB, 128), 2 ** pos_bits, I32))
        pstar = jnp.where(sentinel, 0, pstar)

        def body(kt, carry):
            for j in range(KT // 128):
                c0 = pl.multiple_of(kt * KT + j * 128, 128)
                kk = keys_ref[:, pl.ds(c0, 128)]
                tie = jnp.where(c0 + lane < pstar, 0.0, NEG_BIG)
                bias_ref[:, pl.ds(c0, 128)] = jnp.where(kk > tau, 0.0, jnp.where(kk == tau, tie, NEG_BIG))
            return carry

        lax.fori_loop(0, n_kt, body, 0)


def _dsa(qix, misc, kix, qlat, ckv, topk):
    B, T, _ = kix.shape
    n_cg_max = -(-T // PLANE_COLS)
    t_pad = n_cg_max * PLANE_COLS
    pos_bits = (t_pad - 1).bit_length()

    n_blocks = T // QBLOCK

    def rows(w):
        return pl.BlockSpec((1, QBLOCK, w), lambda b, g: (b, jnp.minimum(g, n_blocks - 1), 0))

    def head_rows(h, w):
        return pl.BlockSpec((1, h, QBLOCK, w), lambda b, g: (b, 0, jnp.minimum(g, n_blocks - 1), 0))

    def per_b(w):
        return pl.BlockSpec((1, T, w), lambda b, g: (b, 0, 0))

    out_rows = pl.BlockSpec((1, SA_HEADS, QBLOCK, KV_RANK), lambda b, g: (b, 0, jnp.maximum(g - 1, 0), 0))

    return pl.pallas_call(
        functools.partial(_dsa_kernel, topk=topk, pos_bits=pos_bits, n_cg_max=n_cg_max),
        grid=(B, n_blocks + 1),
        in_specs=[head_rows(IDX_HEADS, IDX_DIM), rows(MISC_W), per_b(IDX_DIM), head_rows(SA_HEADS, KV_RANK),
                  per_b(KV_RANK)],
        out_specs=out_rows,
        out_shape=jax.ShapeDtypeStruct((B, SA_HEADS, T, KV_RANK), BF16),
        scratch_shapes=[pltpu.VMEM((QBLOCK, t_pad), I32),
                        pltpu.VMEM((n_cg_max, 32, QBLOCK, 128), I32),
                        pltpu.VMEM((QBLOCK, t_pad), F32),
                        pltpu.VMEM((SA_HEADS * QBLOCK, 128), F32),
                        pltpu.VMEM((SA_HEADS * QBLOCK, 128), F32),
                        pltpu.VMEM((SA_HEADS * QBLOCK, KV_RANK), F32),
                        pltpu.VMEM((8, 128), F32),
                        pltpu.VMEM((SA_HEADS * QBLOCK, KV_RANK), BF16),
                        pltpu.VMEM((QBLOCK, 128), I32),
                        pltpu.SMEM((1,), I32)],
        compiler_params=pltpu.CompilerParams(dimension_semantics=("arbitrary", "arbitrary"),
                                             vmem_limit_bytes=VMEM_LIMIT),
        name="dsa",
    )(qix, misc, kix, qlat, ckv)


def _first_max(v, idx, axis):
    m = jnp.max(v, axis=axis, keepdims=True)
    big = jnp.int32(2 ** 30)
    first = jnp.min(jnp.where(v == m, idx, big), axis=axis, keepdims=True)
    return m, idx == first


def _outproj_kernel(x_ref, odn_ref, olat_ref, uv_ref, wo_ref, gt_ref, sc_ref, sh_ref, g2_ref, rwt_ref,
                    rb_ref, lstrict_ref, ustrict_ref, x1_ref, h2_ref, posrow_ref, poscol_ref, gatecol_ref,
                    cpad_ref):
    tm = x_ref.shape[1]
    parts = [odn_ref[0]]
    for hd in range(SA_HEADS):
        parts.append(_dot(olat_ref[0, hd], uv_ref[hd]).astype(BF16))
    mix = jnp.concatenate(parts, axis=-1)
    x1 = x_ref[0] + gt_ref[0] * _dot(mix, wo_ref[...])
    x1_ref[0] = x1
    h2 = x1 * lax.rsqrt(jnp.mean(x1 * x1, axis=-1, keepdims=True) + EPS) * g2_ref[...]
    h2 = h2 * (1.0 + sc_ref[0]) + sh_ref[0]
    h2_ref[0] = h2.astype(BF16)

    per_g = N_EXPERTS // N_GROUPS
    s = jax.nn.sigmoid(_nt_dot(rwt_ref[...], h2, HIGHEST))
    choice = s + rb_ref[...]
    ig = lax.broadcasted_iota(I32, (per_g, tm), 0)
    gscore = []
    for gidx in range(N_GROUPS):
        cg = choice[gidx * per_g:(gidx + 1) * per_g]
        m1, hot1 = _first_max(cg, ig, 0)
        gscore.append(m1 + jnp.max(jnp.where(hot1, -jnp.inf, cg), axis=0, keepdims=True))
    gsel = [jnp.zeros((1, tm), jnp.bool_) for _ in range(N_GROUPS)]
    for _ in range(TOPK_GROUPS):
        best = functools.reduce(jnp.maximum, gscore)
        found = jnp.zeros((1, tm), jnp.bool_)
        for gidx in range(N_GROUPS):
            hot = (gscore[gidx] == best) & jnp.logical_not(found)
            found = found | hot
            gsel[gidx] = gsel[gidx] | hot
            gscore[gidx] = jnp.where(hot, -jnp.inf, gscore[gidx])
    masked = jnp.concatenate(
        [jnp.where(gsel[gidx], choice[gidx * per_g:(gidx + 1) * per_g], -jnp.inf) for gidx in range(N_GROUPS)],
        axis=0)
    ei = lax.broadcasted_iota(I32, masked.shape, 0)
    gate = jnp.zeros(masked.shape, F32)
    hots = []
    for _ in range(TOP_K):
        _, hot = _first_max(masked, ei, 0)
        hots.append(hot)
        gate = jnp.where(hot, s, gate)
        masked = jnp.where(hot, -jnp.inf, masked)
    gate = gate / jnp.sum(gate, axis=0, keepdims=True) * ROUTED_SCALE

    picked = jnp.where(functools.reduce(jnp.logical_or, hots), 1.0, 0.0)
    cnt = jnp.sum(picked, axis=1, keepdims=True)
    cpad = jnp.floor((cnt + (RUN_ALIGN - 1)) * (1.0 / RUN_ALIGN)) * RUN_ALIGN
    cpad_b = jnp.broadcast_to(cpad, (N_EXPERTS, GATE_W))
    lbase = _dot(lstrict_ref[...], cpad_b, HIGHEST)[:, :1]
    rank = _dot(picked.astype(BF16), ustrict_ref[...])
    pos = lbase + rank
    ri = lax.broadcasted_iota(I32, (GATE_W, tm), 0)
    pos_rows = jnp.zeros((GATE_W, tm), F32)
    gate_rows = jnp.zeros((GATE_W, tm), F32)
    for k, hot in enumerate(hots):
        pos_rows = jnp.where(ri == k, jnp.sum(jnp.where(hot, pos, 0.0), axis=0, keepdims=True), pos_rows)
        gate_rows = jnp.where(ri == k, jnp.sum(jnp.where(hot, gate, 0.0), axis=0, keepdims=True), gate_rows)
    posrow_ref[0, 0] = pos_rows[:TOP_K].astype(I32)
    poscol_ref[0] = pos_rows.T.astype(I32)
    gatecol_ref[0] = gate_rows.T
    cpad_ref[0, 0] = cpad_b.astype(I32)


def _outproj(x, odn, olat, uv, wo, gt1, sc2, sh2, g2, rwt, rb, tm):
    B, T, D = x.shape
    nt = T // tm
    ex = jnp.arange(N_EXPERTS)
    lstrict = (ex[:, None] > ex[None, :]).astype(F32)
    tok = jnp.arange(tm)
    ustrict = (tok[:, None] < tok[None, :]).astype(BF16)

    def full(a):
        nd = a.ndim
        return pl.BlockSpec(a.shape, lambda b, i, _n=nd: (0,) * _n)

    def rows(w):
        return pl.BlockSpec((1, tm, w), lambda b, i: (b, i, 0))

    def per_tile(h, w):
        return pl.BlockSpec((1, 1, h, w), lambda b, i: (b, i, 0, 0))

    per_b = pl.BlockSpec((1, 1, D), lambda b, i: (b, 0, 0))
    return pl.pallas_call(
        _outproj_kernel,
        grid=(B, nt),
        in_specs=[rows(D), rows(DN_V),
                  pl.BlockSpec((1, SA_HEADS, tm, KV_RANK), lambda b, i: (b, 0, i, 0)),
                  full(uv), full(wo), per_b, per_b, per_b,
                  full(g2), full(rwt), full(rb), full(lstrict), full(ustrict)],
        out_specs=[rows(D), rows(D), per_tile(TOP_K, tm), rows(GATE_W), rows(GATE_W),
                   per_tile(N_EXPERTS, GATE_W)],
        out_shape=[jax.ShapeDtypeStruct((B, T, D), F32), jax.ShapeDtypeStruct((B, T, D), BF16),
                   jax.ShapeDtypeStruct((B, nt, TOP_K, tm), I32),
                   jax.ShapeDtypeStruct((B, T, GATE_W), I32),
                   jax.ShapeDtypeStruct((B, T, GATE_W), F32),
                   jax.ShapeDtypeStruct((B, nt, N_EXPERTS, GATE_W), I32)],
        compiler_params=pltpu.CompilerParams(dimension_semantics=("arbitrary", "arbitrary"),
                                             vmem_limit_bytes=VMEM_LIMIT),
        name="outproj",
    )(x, odn, olat, uv, wo, gt1, sc2, sh2, g2, rwt, rb, lstrict, ustrict)


def _piece_sizes(max_rows):
    sizes = []
    z = RUN_ALIGN
    while z <= max_rows:
        sizes.append(z)
        z *= 2
    return sizes[::-1]


def _for_run_pieces(length, max_rows, fn, rare_from=None):
    def pieces(sizes):
        for z in sizes:
            start = length & ~(2 * z - 1)

            @pl.when((length & z) != 0)
            def _(start=start, z=z):
                fn(start, z)

    sizes = _piece_sizes(max_rows)
    rare = [z for z in sizes if rare_from is not None and z >= rare_from]
    if rare:
        pl.when(length >= rare_from)(lambda: pieces(rare))
    pieces([z for z in sizes if z not in rare])


def _plan_kernel(cp_ref, off_ref, lb_ref, foff_ref, flen_ref, blk_ref, nused_ref):
    cp = cp_ref[...].astype(F32)
    n, ne = cp.shape
    ei = lax.broadcasted_iota(I32, (ne, ne), 0)
    ej = lax.broadcasted_iota(I32, (ne, ne), 1)
    si = lax.broadcasted_iota(I32, (n, n), 0)
    sj = lax.broadcasted_iota(I32, (n, n), 1)
    lb = _dot(cp, (ei < ej).astype(F32), HIGHEST)
    earlier_tiles = _dot((sj < si).astype(F32), cp, HIGHEST)
    rows_e = jnp.sum(cp, axis=0, keepdims=True)
    region = jnp.floor((rows_e + (ROW_BLOCK - 1)) * (1.0 / ROW_BLOCK)) * ROW_BLOCK
    region_b = jnp.broadcast_to(region, (ne, ne))
    rend_row = _dot(region_b, (ei <= ej).astype(F32), HIGHEST)[:1]
    rend_col = jnp.sum(jnp.where(ej <= ei, region_b, 0.0), axis=1, keepdims=True)
    base = rend_row - region
    total = jnp.max(rend_row, axis=1, keepdims=True)
    off_ref[...] = (base + earlier_tiles).astype(I32)
    lb_ref[...] = lb.astype(I32)
    lane = lax.broadcasted_iota(I32, (1, GATE_W), 1)
    pad = jnp.zeros((1, GATE_W - ne), F32)
    foff_ref[...] = jnp.where(lane == ne, total, jnp.concatenate([base + rows_e, pad], axis=1)).astype(I32)
    flen_ref[...] = jnp.concatenate([region - rows_e, pad], axis=1).astype(I32)
    n_used = total * (1.0 / ROW_BLOCK)
    nused_ref[...] = jnp.broadcast_to(n_used, nused_ref.shape).astype(I32)
    bi = lax.broadcasted_iota(I32, (ne, blk_ref.shape[1]), 1).astype(F32)
    ended = jnp.where(rend_col * (1.0 / ROW_BLOCK) <= jnp.minimum(bi, n_used - 1.0), 1.0, 0.0)
    blk_ref[...] = jnp.minimum(jnp.sum(ended, axis=0, keepdims=True), ne - 1.0).astype(I32)


def _plan(cp, n_blocks):
    n, ne = cp.shape
    nb_pad = -(-n_blocks // 128) * 128
    return pl.pallas_call(
        _plan_kernel,
        out_shape=[jax.ShapeDtypeStruct((n, ne), I32), jax.ShapeDtypeStruct((n, ne), I32),
                   jax.ShapeDtypeStruct((1, GATE_W), I32), jax.ShapeDtypeStruct((1, GATE_W), I32),
                   jax.ShapeDtypeStruct((1, nb_pad), I32), jax.ShapeDtypeStruct((1, GATE_W), I32)],
        name="moe_plan",
    )(cp)


def _dispatch_kernel(off_ref, cp_ref, lb_ref, foff_ref, flen_ref, h_ref, posrow_ref, xs_hbm, buf, zbuf, sem,
                     zsem, *, n_steps, tile):
    s = pl.program_id(0)
    slot = s % 2
    jmax = buf.shape[1]

    def run_copies(step, slot_, act):
        def body(e, carry):
            idx = step * N_EXPERTS + e
            lb = lb_ref[idx]
            of = off_ref[idx]

            def piece(start, z):
                act(pltpu.make_async_copy(
                    buf.at[slot_, pl.ds(pl.multiple_of(lb + start, RUN_ALIGN), z)],
                    xs_hbm.at[pl.ds(pl.multiple_of(of + start, RUN_ALIGN), z)], sem.at[slot_]))

            _for_run_pieces(cp_ref[idx], tile, piece, rare_from=tile // 4)
            return carry

        lax.fori_loop(0, N_EXPERTS, body, 0)

    def wait_runs(step, slot_):
        last_ = step * N_EXPERTS + N_EXPERTS - 1
        _for_run_pieces(lb_ref[last_] + cp_ref[last_], jmax, lambda start, z: pltpu.make_async_copy(
            buf.at[slot_, pl.ds(0, z)], xs_hbm.at[pl.ds(0, z)], sem.at[slot_]).wait())

    def fill_copies(act):
        def body(e, carry):
            fo = foff_ref[e]

            def piece(start, z):
                act(pltpu.make_async_copy(
                    zbuf.at[pl.ds(0, z)], xs_hbm.at[pl.ds(pl.multiple_of(fo + start, RUN_ALIGN), z)], zsem.at[0]))

            _for_run_pieces(flen_ref[e], ROW_BLOCK // 2, piece)
            return carry

        lax.fori_loop(0, N_EXPERTS, body, 0)

        def tail(r, carry):
            act(pltpu.make_async_copy(
                zbuf, xs_hbm.at[pl.ds(pl.multiple_of(foff_ref[N_EXPERTS] + r * zbuf.shape[0], RUN_ALIGN),
                                      zbuf.shape[0])], zsem.at[0]))
            return carry

        lax.fori_loop(0, (xs_hbm.shape[0] - foff_ref[N_EXPERTS]) // zbuf.shape[0], tail, 0)

    @pl.when(s == 0)
    def _():
        zbuf[...] = jnp.zeros(zbuf.shape, BF16)
        fill_copies(lambda c: c.start())

    @pl.when(s >= 2)
    def _():
        wait_runs(s - 2, slot)

    h = h_ref[...]
    last = s * N_EXPERTS + N_EXPERTS - 1
    jused = lb_ref[last] + cp_ref[last]
    def local_rows(jc):
        ji = (lax.broadcasted_iota(I32, (MOE_CHUNK, tile), 0) + jc * MOE_CHUNK).astype(jnp.int16)
        p = jnp.zeros((MOE_CHUNK, tile), BF16)
        for k in range(TOP_K):
            p = jnp.where(ji == posrow_ref[0, k:k + 1, :].astype(jnp.int16), jnp.ones((), BF16), p)
        return _dot(p, h).astype(BF16)

    n_full = TOP_K * tile // MOE_CHUNK
    rows_full = [local_rows(jc) for jc in range(n_full)]
    for jc in range(n_full):
        buf[slot, jc * MOE_CHUNK:(jc + 1) * MOE_CHUNK, :] = rows_full[jc]
    for jc in range(n_full, jmax // MOE_CHUNK):
        @pl.when(jc * MOE_CHUNK < jused)
        def _(jc=jc):
            buf[slot, jc * MOE_CHUNK:(jc + 1) * MOE_CHUNK, :] = local_rows(jc)

    run_copies(s, slot, lambda c: c.start())

    @pl.when(s == n_steps - 1)
    def _():
        if n_steps >= 2:
            wait_runs(s - 1, 1 - slot)
        wait_runs(s, slot)
        fill_copies(lambda c: c.wait())


def _dispatch(h2, posrow, off, cp, lb, foff, flen, cap, tile, jmax):
    n_tok, D = h2.shape
    n_steps = n_tok // tile
    return pl.pallas_call(
        functools.partial(_dispatch_kernel, n_steps=n_steps, tile=tile),
        grid_spec=pltpu.PrefetchScalarGridSpec(
            num_scalar_prefetch=5,
            grid=(n_steps,),
            in_specs=[pl.BlockSpec((tile, D), lambda s, *_: (s, 0)),
                      pl.BlockSpec((1, TOP_K, tile), lambda s, *_: (s, 0, 0))],
            out_specs=pl.BlockSpec(memory_space=pl.ANY),
            scratch_shapes=[pltpu.VMEM((2, jmax, D), BF16), pltpu.VMEM((ROW_BLOCK // 2, D), BF16),
                            pltpu.SemaphoreType.DMA((2,)), pltpu.SemaphoreType.DMA((1,))]),
        out_shape=jax.ShapeDtypeStruct((cap, D), BF16),
        compiler_params=pltpu.CompilerParams(dimension_semantics=("arbitrary",), vmem_limit_bytes=VMEM_LIMIT),
        name="moe_dispatch",
    )(off, cp, lb, foff, flen, h2, posrow)


def _expert_kernel(blk_e_ref, nused_ref, xs_ref, wg_ref, wu_ref, wd_ref, ys_ref, wg_bf, wu_bf, wd_bf):
    i = pl.program_id(0)
    used = i < nused_ref[0]

    @pl.when(used & ((i == 0) | (blk_e_ref[i] != blk_e_ref[jnp.maximum(i - 1, 0)])))
    def _():
        wg_bf[...] = wg_ref[0].astype(BF16)
        wu_bf[...] = wu_ref[0].astype(BF16)
        wd_bf[...] = wd_ref[0].astype(BF16)

    @pl.when(used)
    def _():
        xb = xs_ref[...]
        a = _silu(_dot(xb, wg_bf[...])) * _dot(xb, wu_bf[...])
        ys_ref[...] = _dot(a.astype(BF16), wd_bf[...]).astype(BF16)

    @pl.when(jnp.logical_not(used))
    def _():
        ys_ref[...] = jnp.zeros(ys_ref.shape, BF16)


def _experts(xs, blk_e, n_used, wg, wu, wd):
    cap, D = xs.shape

    def row_block(i, be, nu):
        return (jnp.minimum(i, nu[0] - 1), 0)

    def out_block(i, be, nu):
        return (i, 0)

    def weight(i, be, nu):
        return (be[i], 0, 0)

    return pl.pallas_call(
        _expert_kernel,
        grid_spec=pltpu.PrefetchScalarGridSpec(
            num_scalar_prefetch=2,
            grid=(cap // ROW_BLOCK,),
            in_specs=[pl.BlockSpec((ROW_BLOCK, D), row_block),
                      pl.BlockSpec((1, D, D_EXPERT), weight), pl.BlockSpec((1, D, D_EXPERT), weight),
                      pl.BlockSpec((1, D_EXPERT, D), weight)],
            out_specs=pl.BlockSpec((ROW_BLOCK, D), out_block),
            scratch_shapes=[pltpu.VMEM((D, D_EXPERT), BF16), pltpu.VMEM((D, D_EXPERT), BF16),
                            pltpu.VMEM((D_EXPERT, D), BF16)]),
        out_shape=jax.ShapeDtypeStruct((cap, D), BF16),
        compiler_params=pltpu.CompilerParams(dimension_semantics=("arbitrary",), vmem_limit_bytes=VMEM_LIMIT),
        name="moe_experts",
    )(blk_e, n_used, xs, wg, wu, wd)


def _combine_kernel(off_ref, cp_ref, lb_ref, ys_hbm, poscol_ref, gatecol_ref, h_ref, sg_ref, su_ref, sd_ref,
                    x1_ref, gt_ref, fg_ref, o_ref, buf, sem, acc_ref, *, n_steps, tile, final_norm):
    s = pl.program_id(0)
    slot = s % 2
    jmax = buf.shape[1]

    def run_copies(step, slot_, act):
        def body(e, carry):
            idx = step * N_EXPERTS + e
            lb = lb_ref[idx]
            of = off_ref[idx]

            def piece(start, z):
                act(pltpu.make_async_copy(
                    ys_hbm.at[pl.ds(pl.multiple_of(of + start, RUN_ALIGN), z)],
                    buf.at[slot_, pl.ds(pl.multiple_of(lb + start, RUN_ALIGN), z)], sem.at[slot_]))

            _for_run_pieces(cp_ref[idx], tile, piece, rare_from=tile // 4)
            return carry

        lax.fori_loop(0, N_EXPERTS, body, 0)

    @pl.when(s == 0)
    def _():
        run_copies(0, 0, lambda c: c.start())

    @pl.when(s + 1 < n_steps)
    def _():
        run_copies(s + 1, 1 - slot, lambda c: c.start())

    hb = h_ref[...]
    shared = (_silu(_dot(hb, sg_ref[...])) * _dot(hb, su_ref[...])).astype(BF16)
    acc_ref[...] = _dot(shared, sd_ref[...])

    last = s * N_EXPERTS + N_EXPERTS - 1
    jused = lb_ref[last] + cp_ref[last]
    _for_run_pieces(jused, jmax, lambda start, z: pltpu.make_async_copy(
        ys_hbm.at[pl.ds(0, z)], buf.at[slot, pl.ds(0, z)], sem.at[slot]).wait())

    def zero_body(r, carry):
        buf[slot, pl.ds(pl.multiple_of(jused + r * RUN_ALIGN, RUN_ALIGN), RUN_ALIGN), :] = jnp.zeros(
            (RUN_ALIGN, buf.shape[2]), BF16)
        return carry

    chunk_end = (jused + MOE_CHUNK - 1) // MOE_CHUNK * MOE_CHUNK
    lax.fori_loop(0, (chunk_end - jused) // RUN_ALIGN, zero_body, 0)

    def gate_rows(jc):
        ji = (lax.broadcasted_iota(I32, (tile, MOE_CHUNK), 1) + jc * MOE_CHUNK).astype(jnp.int16)
        g = jnp.zeros((tile, MOE_CHUNK), BF16)
        for k in range(TOP_K):
            g = jnp.where(ji == poscol_ref[:, k:k + 1].astype(jnp.int16), gatecol_ref[:, k:k + 1].astype(BF16), g)
        return g

    n_full = TOP_K * tile // MOE_CHUNK
    g_full = jnp.concatenate([gate_rows(jc) for jc in range(n_full)], axis=1)
    acc_ref[...] += _dot(g_full, buf[slot, 0:n_full * MOE_CHUNK, :])
    for jc in range(n_full, jmax // MOE_CHUNK):
        @pl.when(jc * MOE_CHUNK < jused)
        def _(jc=jc):
            acc_ref[...] += _dot(gate_rows(jc), buf[slot, jc * MOE_CHUNK:(jc + 1) * MOE_CHUNK, :])

    y = x1_ref[...] + gt_ref[0] * acc_ref[...]
    if final_norm:
        y = y * lax.rsqrt(jnp.mean(y * y, axis=-1, keepdims=True) + EPS) * fg_ref[...]
    o_ref[...] = y


def _combine(ys, poscol, gatecol, h2, sg, su, sd, x1, gt2, fg, off, cp, lb, tile, jmax, tiles_per_batch,
             final_norm):
    n_tok, D = h2.shape
    n_steps = n_tok // tile

    def full(a):
        nd = a.ndim
        return pl.BlockSpec(a.shape, lambda s, *_, _n=nd: (0,) * _n)

    def rows(w):
        return pl.BlockSpec((tile, w), lambda s, *_: (s, 0))

    return pl.pallas_call(
        functools.partial(_combine_kernel, n_steps=n_steps, tile=tile, final_norm=final_norm),
        grid_spec=pltpu.PrefetchScalarGridSpec(
            num_scalar_prefetch=3,
            grid=(n_steps,),
            in_specs=[pl.BlockSpec(memory_space=pl.ANY), rows(GATE_W), rows(GATE_W), rows(D),
                      full(sg), full(su), full(sd), rows(D),
                      pl.BlockSpec((1, 1, D), lambda s, *_: (s // tiles_per_batch, 0, 0)), full(fg)],
            out_specs=rows(D),
            scratch_shapes=[pltpu.VMEM((2, jmax, D), BF16), pltpu.SemaphoreType.DMA((2,)),
                            pltpu.VMEM((tile, D), F32)]),
        out_shape=jax.ShapeDtypeStruct((n_tok, D), F32),
        compiler_params=pltpu.CompilerParams(dimension_semantics=("arbitrary",), vmem_limit_bytes=VMEM_LIMIT),
        name="moe_combine",
    )(off, cp, lb, ys, poscol, gatecol, h2, sg, su, sd, x1, gt2, fg)


def _moe(h2, posrow, poscol, gatecol, cpad, wg, wu, wd, sg, su, sd, x1, gt2, fg, tile, final_norm):
    B, T, D = x1.shape
    n_tok = B * T
    n_tiles = n_tok // tile
    jmax = -(-(TOP_K * tile + N_EXPERTS * (RUN_ALIGN - 1)) // MOE_CHUNK) * MOE_CHUNK
    cap = -(-(TOP_K * n_tok + n_tiles * N_EXPERTS * (RUN_ALIGN - 1) + N_EXPERTS * (ROW_BLOCK - RUN_ALIGN))
            // ROW_BLOCK) * ROW_BLOCK

    cp = cpad[..., 0].reshape(n_tiles, N_EXPERTS)
    off, lb, foff, flen, blk_e, n_used = _plan(cp, cap // ROW_BLOCK)
    flat = lambda a: a.reshape(-1)

    xs = _dispatch(h2.reshape(n_tok, D), posrow.reshape(n_tiles, TOP_K, tile), flat(off), flat(cp), flat(lb),
                   foff[0], flen[0], cap, tile, jmax)
    ys = _experts(xs, blk_e[0], n_used[0], wg, wu, wd)
    out = _combine(ys, poscol.reshape(n_tok, GATE_W), gatecol.reshape(n_tok, GATE_W), h2.reshape(n_tok, D),
                   sg, su, sd, x1.reshape(n_tok, D), gt2, fg, flat(off), flat(cp), flat(lb), tile, jmax,
                   T // tile, final_norm)
    return out.reshape(B, T, D)


def _misc_lanes(vec, start):
    return jnp.zeros((1, MISC_W), F32).at[0, start:start + vec.shape[0]].set(vec.astype(F32))


def kernel(x, c, ada_w, ada_b, norm1_g, w_in, conv_w, a_log, dt_bias, dn_norm_g, kv_norm_g, w_uk, w_uv,
           idx_k_ln_g, idx_k_ln_b, w_out, norm2_g, router_w, router_b, exp_w_gate, exp_w_up, exp_w_down,
           sh_w_gate, sh_w_up, sh_w_down, final_g):
    B, T, D = x.shape
    depth = ada_w.shape[0]
    topk = min(IDX_TOPK_MAX, T // 4)
    tm = min(512, T)
    r_dn = min(256, T)

    cond_in = jnp.zeros((8, D), F32).at[:B].set(c)
    pos = jnp.arange(tm)
    tri = ((pos[:, None] // CHUNK == pos[None, :] // CHUNK) & (pos[:, None] >= pos[None, :])).astype(F32)

    for l in range(depth):
        mod = _ada(cond_in, ada_w[l], ada_b[l][None, :])[:B]
        sh1, sc1, gt1, sh2, sc2, gt2 = [m[:, None, :] for m in jnp.split(mod, 6, axis=-1)]

        offs = [0]
        for s in (DN_QK, DN_QK, DN_V, DN_V, DN_HEADS, DN_HEADS, SA_Q, KV_RANK, IDX_Q, IDX_DIM, IDX_HEADS):
            offs.append(offs[-1] + s)
        w = w_in[l]
        wc = w[:, offs[0]:offs[3]].astype(BF16)
        wz = w[:, offs[3]:offs[4]].astype(BF16)
        wq = w[:, offs[6]:offs[7]].astype(BF16)
        wkv = w[:, offs[7]:offs[8]].astype(BF16)
        wqi = w[:, offs[8]:offs[9]].astype(BF16)
        wm = jnp.concatenate([w[:, offs[9]:offs[10]], w[:, offs[4]:offs[5]], w[:, offs[5]:offs[6]],
                              w[:, offs[10]:offs[11]],
                              jnp.zeros((D, MISC_W - IDX_DIM - 2 * DN_HEADS - IDX_HEADS), F32)],
                             axis=1).astype(BF16)
        ukt = jnp.swapaxes(w_uk[l], 1, 2).astype(BF16)

        q, k, v, z, qlat, ckv, qix, kix, misc = _inproj(
            x, sc1, sh1, norm1_g[l][None, :], wc, wz, wq, wkv, wqi, wm, conv_w[l], ukt,
            kv_norm_g[l][None, :], _misc_lanes(idx_k_ln_g[l], M_KIX), _misc_lanes(idx_k_ln_b[l], M_KIX),
            _misc_lanes(a_log[l], M_A), _misc_lanes(dt_bias[l], M_A), tri, tm)

        odn = _deltanet(q, k, v, z, misc, dn_norm_g[l][None, :], r_dn)
        olat = _dsa(qix, misc, kix, qlat, ckv, topk)

        x1, h2, posrow, poscol, gatecol, cpad = _outproj(
            x, odn, olat, w_uv[l].astype(BF16), w_out[l].astype(BF16), gt1, sc2, sh2,
            norm2_g[l][None, :], router_w[l].T, router_b[l][:, None], tm)

        x = _moe(h2, posrow, poscol, gatecol, cpad, exp_w_gate[l], exp_w_up[l], exp_w_down[l],
                 sh_w_gate[l].astype(BF16), sh_w_up[l].astype(BF16),
                 sh_w_down[l].astype(BF16), x1, gt2, final_g[None, :], tm, l == depth - 1)
    return x
```

```python
import functools

import jax
import jax.numpy as jnp
from jax import lax
from jax.experimental import pallas as pl
from jax.experimental.pallas import tpu as pltpu

F32 = jnp.float32
BF16 = jnp.bfloat16
I32 = jnp.int32
HIGHEST = lax.Precision.HIGHEST

EPS = 1e-6
CHUNK = 64
DN_HEADS = 4
DN_DK = 128
DN_DV = 128
CONV_K = 4
SA_HEADS = 4
SA_DQK = 128
SA_DV = 128
KV_RANK = 256
IDX_HEADS = 4
IDX_DIM = 64
IDX_TOPK_MAX = 256
SM_SCALE = SA_DQK ** -0.5
LOG2E = 1.4426950408889634
IDX_W_SCALE = (IDX_HEADS * IDX_DIM) ** -0.5
N_EXPERTS = 64
TOP_K = 8
N_GROUPS = 8
TOPK_GROUPS = 4
D_EXPERT = 256
ROUTED_SCALE = 2.5
GATE_W = 128
RUN_ALIGN = 16
ROW_BLOCK = 1024
MOE_CHUNK = 1024

DN_QK = DN_HEADS * DN_DK
DN_V = DN_HEADS * DN_DV
CONV_DIM = 2 * DN_QK + DN_V
SA_Q = SA_HEADS * SA_DQK
IDX_Q = IDX_HEADS * IDX_DIM

MISC_W = 128
M_KIX = 0
M_BETA = IDX_DIM
M_A = M_BETA + DN_HEADS
M_WIX = M_A + DN_HEADS

DN_SUB = 2 * CHUNK
QBLOCK = 256
SOFTMAX_TINY = 2.0 ** -100
KEY_TILE = 1024
PLANE_COLS = 32 * 128
PLANE_SHIFT = 12
PLANE_ROWS = 32
INT_MIN = -2 ** 31
NEG_BIG = -1e30
VMEM_LIMIT = 56 * 1024 * 1024


def _nt_dot(a, b, precision=None):
    return lax.dot_general(a, b, (((1,), (1,)), ((), ())), preferred_element_type=F32,
                           precision=precision)


def _dot(a, b, precision=None):
    return jnp.dot(a, b, preferred_element_type=F32, precision=precision)


def _silu(x):
    return x * jax.nn.sigmoid(x)


def _softplus(x):
    return jnp.maximum(x, 0.0) + jnp.log(1.0 + jnp.exp(-jnp.abs(x)))


def _ada_kernel(c_ref, w_ref, b_ref, o_ref):
    cond = _silu(c_ref[...])
    o_ref[...] = _dot(cond, w_ref[...], HIGHEST) + b_ref[...]


def _ada(c_pad, ada_w, ada_b):
    rows, d = c_pad.shape
    n_out = ada_w.shape[1]
    return pl.pallas_call(
        _ada_kernel,
        grid=(n_out // d,),
        in_specs=[pl.BlockSpec((rows, d), lambda j: (0, 0)),
                  pl.BlockSpec((d, d), lambda j: (0, j)),
                  pl.BlockSpec((1, d), lambda j: (0, j))],
        out_specs=pl.BlockSpec((rows, d), lambda j: (0, j)),
        out_shape=jax.ShapeDtypeStruct((rows, n_out), F32),
        compiler_params=pltpu.CompilerParams(vmem_limit_bytes=VMEM_LIMIT),
        name="ada",
    )(c_pad, ada_w, ada_b)


def _inproj_kernel(x_ref, sc_ref, sh_ref, g1_ref, wc_ref, wz_ref, wq_ref, wkv_ref, wqi_ref, wm_ref,
                   convw_ref, ukt_ref, kvg_ref, lng_ref, lnb_ref, alog_ref, dtb_ref, tri_ref,
                   q_ref, k_ref, v_ref, z_ref, qlat_ref, ckv_ref, qix_ref, kix_ref, misc_ref,
                   conv_buf):
    tm = x_ref.shape[1]
    i = pl.program_id(1)

    x = x_ref[0]
    h = x * lax.rsqrt(jnp.mean(x * x, axis=-1, keepdims=True) + EPS) * g1_ref[...]
    h = h * (1.0 + sc_ref[0]) + sh_ref[0]
    hb = h.astype(BF16)

    @pl.when(i == 0)
    def _():
        conv_buf[0:8, :] = jnp.zeros((8, CONV_DIM), F32)

    conv_buf[8:8 + tm, :] = _dot(hb, wc_ref[...])
    for grp, dst in ((0, q_ref), (1, k_ref), (2, v_ref)):
        cols = slice(grp * DN_QK, (grp + 1) * DN_QK)
        y = jnp.zeros((tm, DN_QK), F32)
        for j in range(CONV_K):
            y = y + convw_ref[j:j + 1, cols] * conv_buf[8 - (CONV_K - 1) + j:8 - (CONV_K - 1) + j + tm, cols]
        y = _silu(y)
        if grp < 2:
            outs = []
            for hd in range(DN_HEADS):
                yh = y[:, hd * DN_DK:(hd + 1) * DN_DK]
                yh = yh * lax.rsqrt(jnp.sum(yh * yh, axis=-1, keepdims=True) + EPS)
                if grp == 0:
                    yh = yh * (DN_DK ** -0.5)
                outs.append(yh)
            y = jnp.concatenate(outs, axis=-1)
        dst[0] = y
    conv_buf[0:8, :] = conv_buf[tm:tm + 8, :]

    z_ref[0] = _dot(hb, wz_ref[...])

    q_sa = _dot(hb, wq_ref[...]).astype(BF16)
    for hd in range(SA_HEADS):
        ql = _dot(q_sa[:, hd * SA_DQK:(hd + 1) * SA_DQK], ukt_ref[hd]) * (SM_SCALE * LOG2E)
        qlat_ref[0, hd] = ql.astype(BF16)

    ckv = _dot(hb, wkv_ref[...])
    ckv = ckv * lax.rsqrt(jnp.mean(ckv * ckv, axis=-1, keepdims=True) + EPS) * kvg_ref[...]
    ckv_ref[0] = ckv.astype(BF16)

    q_ix = _dot(hb, wqi_ref[...]).astype(BF16)
    for hd in range(IDX_HEADS):
        qix_ref[0, hd] = q_ix[:, hd * IDX_DIM:(hd + 1) * IDX_DIM]

    m = _dot(hb, wm_ref[...])
    lane = lax.broadcasted_iota(I32, (tm, MISC_W), 1)
    is_k = lane < IDX_DIM
    mu = jnp.sum(jnp.where(is_k, m, 0.0), axis=-1, keepdims=True) * (1.0 / IDX_DIM)
    kc = jnp.where(is_k, m - mu, 0.0)
    var = jnp.sum(kc * kc, axis=-1, keepdims=True) * (1.0 / IDX_DIM)
    kn = kc * lax.rsqrt(var + EPS) * lng_ref[...] + lnb_ref[...]
    kix_ref[0] = kn[:, :IDX_DIM].astype(BF16)

    beta = jax.nn.sigmoid(m)
    g = -jnp.exp(alog_ref[...]) * _softplus(m + dtb_ref[...])
    is_a = (lane >= M_A) & (lane < M_A + DN_HEADS)
    g = jnp.where(is_a, g, 0.0)
    gc = _dot(tri_ref[...], g, HIGHEST)
    is_b = (lane >= M_BETA) & (lane < M_BETA + DN_HEADS)
    is_w = (lane >= M_WIX) & (lane < M_WIX + IDX_HEADS)
    misc_ref[0] = jnp.where(is_b, beta, jnp.where(is_a, gc, jnp.where(is_w, m * IDX_W_SCALE, 0.0)))


def _inproj(x, sc1, sh1, g1, wc, wz, wq, wkv, wqi, wm, conv_w, ukt, kvg, lng, lnb, alog, dtb, tri, tm):
    B, T, D = x.shape
    nt = T // tm

    def full(a):
        nd = a.ndim
        return pl.BlockSpec(a.shape, lambda b, i, _n=nd: (0,) * _n)

    def rows(w):
        return pl.BlockSpec((1, tm, w), lambda b, i: (b, i, 0))

    per_b = pl.BlockSpec((1, 1, D), lambda b, i: (b, 0, 0))
    def head_rows(h, w):
        return pl.BlockSpec((1, h, tm, w), lambda b, i: (b, 0, i, 0))

    outs = [(None, DN_QK, F32), (None, DN_QK, F32), (None, DN_V, F32), (None, DN_V, F32),
            (SA_HEADS, KV_RANK, BF16), (None, KV_RANK, BF16), (IDX_HEADS, IDX_DIM, BF16),
            (None, IDX_DIM, BF16), (None, MISC_W, F32)]
    return pl.pallas_call(
        _inproj_kernel,
        grid=(B, nt),
        in_specs=[rows(D), per_b, per_b, full(g1), full(wc), full(wz), full(wq), full(wkv), full(wqi),
                  full(wm), full(conv_w), full(ukt), full(kvg), full(lng), full(lnb), full(alog),
                  full(dtb), full(tri)],
        out_specs=[rows(w) if h is None else head_rows(h, w) for h, w, _ in outs],
        out_shape=[jax.ShapeDtypeStruct((B, T, w) if h is None else (B, h, T, w), dt) for h, w, dt in outs],
        scratch_shapes=[pltpu.VMEM((tm + 8, CONV_DIM), F32)],
        compiler_params=pltpu.CompilerParams(dimension_semantics=("arbitrary", "arbitrary"),
                                             vmem_limit_bytes=VMEM_LIMIT),
        name="inproj",
    )(x, sc1, sh1, g1, wc, wz, wq, wkv, wqi, wm, conv_w, ukt, kvg, lng, lnb, alog, dtb, tri)


def _deltanet_kernel(q_ref, k_ref, v_ref, z_ref, misc_ref, ng_ref, o_ref, s_ref):
    R = q_ref.shape[1]
    n_chunks = R // CHUNK

    @pl.when(pl.program_id(1) == 0)
    def _():
        s_ref[...] = jnp.zeros(s_ref.shape, F32)

    misc = misc_ref[0]
    misc_t = misc.T
    SB = min(DN_SUB, R)
    row = lax.broadcasted_iota(I32, (SB, SB), 0)
    col = lax.broadcasted_iota(I32, (SB, SB), 1)
    same = (row // CHUNK) == (col // CHUNK)
    lower = same & (row >= col)
    strict = same & (row > col)
    eye = (row == col).astype(F32)

    def mm(a, b):
        return _dot(a.astype(BF16), b.astype(BF16))

    def mm3(a, b):
        ah = a.astype(BF16)
        bh = b.astype(BF16)
        al = (a - ah.astype(F32)).astype(BF16)
        bl = (b - bh.astype(F32)).astype(BF16)
        return _dot(jnp.concatenate([ah, ah, al], axis=1), jnp.concatenate([bh, bl, bh], axis=0))

    heads = range(DN_HEADS)
    subs = range(R // SB)
    chains = [(hd, sb) for hd in heads for sb in subs]
    cols = [slice(hd * DN_DK, (hd + 1) * DN_DK) for hd in heads]
    qh = [q_ref[0, :, cols[hd]] for hd in heads]
    kh = [k_ref[0, :, cols[hd]] for hd in heads]
    beta = [misc[:, M_BETA + hd:M_BETA + hd + 1] for hd in heads]
    gc_c = [misc[:, M_A + hd:M_A + hd + 1] for hd in heads]
    eg = [jnp.exp(gc_c[hd]) for hd in heads]
    kb = [kh[hd] * beta[hd] for hd in heads]
    rhs = [jnp.concatenate([v_ref[0, :, cols[hd]] * beta[hd], kb[hd] * eg[hd]], axis=-1) for hd in heads]
    q_dec = [qh[hd] * eg[hd] for hd in heads]

    def rows_of(sb):
        return slice(sb * SB, (sb + 1) * SB)

    decay, a, qk_sb = {}, {}, {}
    for hd, sb in chains:
        bs = rows_of(sb)
        gc_r = misc_t[M_A + hd:M_A + hd + 1, bs]
        decay[hd, sb] = jnp.where(lower, jnp.exp(jnp.where(lower, gc_c[hd][bs] - gc_r, 0.0)), 0.0)
    for hd, sb in chains:
        bs = rows_of(sb)
        khb = kh[hd][bs].astype(BF16)
        a[hd, sb] = jnp.where(strict, _nt_dot(kb[hd][bs].astype(BF16), khb) * decay[hd, sb], 0.0)
        qk_sb[hd, sb] = jnp.where(lower, _nt_dot(qh[hd][bs].astype(BF16), khb) * decay[hd, sb], 0.0)
    p = {ch: eye - a[ch] for ch in chains}
    xp = {ch: mm3(a[ch], a[ch]) for ch in chains}
    n_sq = 1
    while True:
        p = {ch: p[ch] + mm3(p[ch], xp[ch]) for ch in chains}
        n_sq *= 2
        if n_sq * 2 >= CHUNK:
            break
        xp = {ch: mm3(xp[ch], xp[ch]) for ch in chains}
    sol = {(hd, sb): mm(p[hd, sb], rhs[hd][rows_of(sb)]) for hd, sb in chains}

    def chunk_of(c):
        per = SB // CHUNK
        return c // per, slice((c % per) * CHUNK, (c % per + 1) * CHUNK)

    s = [s_ref[hd] for hd in heads]
    o_parts = [[] for _ in heads]
    for c in range(n_chunks):
        rs = slice(c * CHUNK, (c + 1) * CHUNK)
        sb, r = chunk_of(c)
        gl = [gc_c[hd][(c + 1) * CHUNK - 1:(c + 1) * CHUNK, :] for hd in heads]
        k_dec = [kh[hd][rs] * jnp.exp(gl[hd] - gc_c[hd][rs]) for hd in heads]
        v_new = [sol[hd, sb][r, :DN_DV] - mm(sol[hd, sb][r, DN_DV:], s[hd]) for hd in heads]
        for hd in heads:
            o_parts[hd].append(mm(q_dec[hd][rs], s[hd]) + mm(qk_sb[hd, sb][r, r], v_new[hd]))
        s = [s[hd] * jnp.exp(gl[hd]) + mm(k_dec[hd].T, v_new[hd]) for hd in heads]
    for hd in heads:
        s_ref[hd] = s[hd]
        o = jnp.concatenate(o_parts[hd], axis=0)
        o = o * lax.rsqrt(jnp.mean(o * o, axis=-1, keepdims=True) + EPS) * ng_ref[...]
        o_ref[0, :, cols[hd]] = (o * _silu(z_ref[0, :, cols[hd]])).astype(BF16)


def _deltanet(q, k, v, z, misc, ng, R):
    B, T, _ = q.shape

    def rows(w):
        return pl.BlockSpec((1, R, w), lambda b, i: (b, i, 0))

    return pl.pallas_call(
        _deltanet_kernel,
        grid=(B, T // R),
        in_specs=[rows(DN_QK), rows(DN_QK), rows(DN_V), rows(DN_V), rows(MISC_W),
                  pl.BlockSpec((1, DN_DV), lambda b, i: (0, 0))],
        out_specs=rows(DN_V),
        out_shape=jax.ShapeDtypeStruct((B, T, DN_V), BF16),
        scratch_shapes=[pltpu.VMEM((DN_HEADS, DN_DK, DN_DV), F32)],
        compiler_params=pltpu.CompilerParams(dimension_semantics=("arbitrary", "arbitrary"),
                                             vmem_limit_bytes=VMEM_LIMIT),
        name="deltanet",
    )(q, k, v, z, misc, ng)


def _bit_transpose32(words):
    w = list(words)
    j = 16
    m = 0x0000FFFF
    while j:
        k = 0
        m_i32 = jnp.int32(m - (1 << 32) if m >= (1 << 31) else m)
        while k < 32:
            t = (w[k] ^ lax.shift_right_logical(w[k + j], jnp.full_like(w[k], j))) & m_i32
            w[k] = w[k] ^ t
            w[k + j] = w[k + j] ^ jnp.left_shift(t, j)
            k = (k + j + 1) & ~j
        j >>= 1
        m = (m ^ (m << j)) & 0xFFFFFFFF
    return w


def _dsa_kernel(qix_ref, misc_ref, kix_ref, qlat_ref, ckv_ref, o_ref, keys_ref, planes_ref, bias_ref,
                mx_ref, l_ref, acc_ref, kvmax_ref, qprev_ref, *, topk, pos_bits, n_cg_max):
    g = pl.program_id(1)
    n_blocks = pl.num_programs(1) - 1
    i = g
    QB = QBLOCK
    KT = KEY_TILE
    has_front = g < n_blocks
    has_back = g >= 1
    n_kt = jnp.where(has_front, (i * QB + QB + KT - 1) // KT, 0)
    n_kt_back = (g * QB + KT - 1) // KT

    rowi = lax.broadcasted_iota(I32, (QB, KT), 0)
    coli = lax.broadcasted_iota(I32, (QB, KT), 1)
    limit = i * QB + (rowi // CHUNK + 1) * CHUNK

    misc = misc_ref[0]
    qix = qix_ref[0].reshape(IDX_HEADS * QB, IDX_DIM)
    q_st = qlat_ref[0].reshape(SA_HEADS * QB, KV_RANK)

    def score_body(masked, kt, carry):
        k0 = pl.multiple_of(kt * KT, KT)
        kx = kix_ref[0, pl.ds(k0, KT), :]
        rel = jnp.maximum(_nt_dot(qix, kx), 0.0)
        sc = jnp.zeros((QB, KT), F32)
        for hd in range(IDX_HEADS):
            sc = sc + misc[:, M_WIX + hd:M_WIX + hd + 1] * rel[hd * QB:(hd + 1) * QB]
        sc = jnp.where(sc == 0.0, 0.0, sc)
        bits = pltpu.bitcast(sc, I32)
        key = jnp.where(bits < 0, bits ^ 0x7FFFFFFF, bits)
        keys_ref[:, pl.ds(k0, KT)] = jnp.where(k0 + coli < limit, key, INT_MIN) if masked else key
        return carry

    def logit_chunks(k0):
        kv = ckv_ref[0, pl.ds(k0, KT), :]
        s = _nt_dot(qprev_ref[...], kv)
        bias = bias_ref[:, pl.ds(k0, KT)]
        chunks = []
        for j in range(KT // 128):
            bj = bias[:, j * 128:(j + 1) * 128]
            chunks.append(s[:, j * 128:(j + 1) * 128] + jnp.concatenate([bj] * SA_HEADS, axis=0))
        return kv, chunks

    def pv_body(kt, carry):
        kv, chunks = logit_chunks(pl.multiple_of(kt * KT, KT))
        shift = mx_ref[...]
        ps = [jnp.exp2(c - shift) for c in chunks]
        l_ref[...] = functools.reduce(jnp.add, ps, l_ref[...])
        p = jnp.concatenate([pj.astype(BF16) for pj in ps], axis=1)
        acc_ref[...] += _dot(p, kv)
        return carry

    def clear_sums():
        l_ref[...] = jnp.zeros(l_ref.shape, F32)
        acc_ref[...] = jnp.zeros(acc_ref.shape, F32)

    def write_back_block():
        l_row = jnp.sum(l_ref[...], axis=-1, keepdims=True)
        o_ref[0] = (acc_ref[...] / l_row).astype(BF16).reshape(SA_HEADS, QB, KV_RANK)
        return l_row

    pl.when(has_back)(clear_sums)

    n_open = jnp.where(has_front, (i * QB + CHUNK) // KT, 0)
    n_fused = jnp.minimum(n_open, n_kt_back)

    def fused_body(kt, carry):
        pv_body(kt, carry)
        return score_body(False, kt, carry)

    lax.fori_loop(0, n_fused, fused_body, 0)
    lax.fori_loop(n_fused, n_kt_back, pv_body, 0)
    lax.fori_loop(n_fused, n_open, functools.partial(score_body, False), 0)
    lax.fori_loop(n_open, n_kt, functools.partial(score_body, True), 0)

    @pl.when(has_back)
    def _():
        l_fast = write_back_block()

        @pl.when(jnp.min(l_fast) < SOFTMAX_TINY)
        def _():
            mx_ref[...] = jnp.full(mx_ref.shape, NEG_BIG, F32)

            def max_body(kt, carry):
                _, chunks = logit_chunks(pl.multiple_of(kt * KT, KT))
                mx_ref[...] = functools.reduce(jnp.maximum, chunks, mx_ref[...])
                return carry

            lax.fori_loop(0, n_kt_back, max_body, 0)
            mx_ref[...] = jnp.broadcast_to(jnp.max(mx_ref[...], axis=-1, keepdims=True), mx_ref.shape)
            clear_sums()
            lax.fori_loop(0, n_kt_back, pv_body, 0)
            write_back_block()

    n_cg = (n_kt * KT + PLANE_COLS - 1) // PLANE_COLS

    def fill_body(kt, carry):
        keys_ref[:, pl.ds(pl.multiple_of(kt * KT, KT), KT)] = jnp.full((QB, KT), INT_MIN, I32)
        return carry

    lax.fori_loop(n_kt, n_cg * (PLANE_COLS // KT), fill_body, 0)

    @pl.when(i == 0)
    def _():
        planes_ref[...] = jnp.zeros(planes_ref.shape, I32)

    def plane_body(step, carry):
        c = step // (QB // PLANE_ROWS)
        r0 = pl.multiple_of((step % (QB // PLANE_ROWS)) * PLANE_ROWS, PLANE_ROWS)
        words = [keys_ref[pl.ds(r0, PLANE_ROWS), pl.ds(pl.multiple_of(c * PLANE_COLS + j * 128, 128), 128)]
                 for j in range(32)]
        for b, plane in enumerate(_bit_transpose32(words)):
            planes_ref[c, b, pl.ds(r0, PLANE_ROWS), :] = ~plane if b == 0 else plane
        return carry

    lax.fori_loop(0, n_cg * (QB // PLANE_ROWS), plane_body, 0)

    ones_mat = jnp.ones((128, 128), BF16)

    def lane_count(words):
        pc = functools.reduce(jnp.add, [lax.population_count(x) for x in words])
        return _dot(pc.astype(F32).astype(BF16), ones_mat).astype(I32)

    def sel_body(groups, step, carry):
        cand, n_gt, tau_u = carry
        hi = [cand[c] & planes_ref[c, 2 * step] for c in groups]
        lo = [cand[c] ^ hi[c] for c in groups]
        d3 = [hi[c] & planes_ref[c, 2 * step + 1] for c in groups]
        d2 = [hi[c] ^ d3[c] for c in groups]
        d1 = [lo[c] & planes_ref[c, 2 * step + 1] for c in groups]
        d0 = [lo[c] ^ d1[c] for c in groups]
        a3 = n_gt + lane_count(d3)
        a2 = a3 + lane_count(d2)
        a1 = a2 + lane_count(d1)
        is3 = a3 >= topk
        is2 = a2 >= topk
        is1 = a1 >= topk
        cand = tuple(jnp.where(is3, d3[c], jnp.where(is2, d2[c], jnp.where(is1, d1[c], d0[c]))) for c in groups)
        n_gt = jnp.where(is3, n_gt, jnp.where(is2, a3, jnp.where(is1, a2, a1)))
        digit = jnp.where(is3, 3, jnp.where(is2, 2, jnp.where(is1, 1, 0)))
        return cand, n_gt, tau_u | jnp.left_shift(digit, 30 - 2 * step)

    def radix_select(n_groups):
        groups = range(n_groups)
        start = (tuple(jnp.full((QB, 128), -1, I32) for _ in groups), jnp.zeros((QB, 128), I32),
                 jnp.zeros((QB, 128), I32))
        cand_, n_gt_, tau_ = lax.fori_loop(0, 16, functools.partial(sel_body, groups), start)
        rest = tuple(jnp.zeros((QB, 128), I32) for _ in range(n_cg_max - n_groups))
        return cand_ + rest, n_gt_, tau_

    cand, n_gt, tau_u = lax.switch(n_cg - 1, [functools.partial(radix_select, g + 1) for g in range(n_cg_max)])
    tau = tau_u ^ INT_MIN
    sentinel = tau == INT_MIN
    cand = tuple(jnp.where(sentinel, 0, cand[c]) for c in range(n_cg_max))
    need = topk - n_gt
    any_tie = jnp.max(jnp.where(lane_count(cand) > need, 1, 0)) > 0

    def fast_bias():
        floor = jnp.where(sentinel, INT_MIN + 1, tau)

        def body(kt, carry):
            for j in range(KT // 128):
                cols = pl.ds(pl.multiple_of(kt * KT + j * 128, 128), 128)
                bias_ref[:, cols] = jnp.where(keys_ref[:, cols] >= floor, 0.0, NEG_BIG)
            return carry

        lax.fori_loop(0, n_kt, body, 0)

    def tie_bias():
        lane = lax.broadcasted_iota(I32, (QB, 128), 1)

        def pos_mask(p, c):
            cg = lax.shift_right_logical(p, jnp.full_like(p, PLANE_SHIFT))
            j0 = lax.shift_right_logical(p, jnp.full_like(p, 7)) & 31
            below = ~lax.shift_right_logical(jnp.full_like(p, -1), j0)
            bit = lax.shift_right_logical(jnp.full_like(p, INT_MIN), j0)
            word = below | jnp.where(lane < (p & 127), bit, 0)
            return jnp.where(cg > c, -1, jnp.where(cg == c, word, 0))

        def pos_body(b, q):
            cq = q + jnp.left_shift(jnp.int32(1), pos_bits - 1 - b)
            cnt = lane_count([cand[c] & pos_mask(cq, c) for c in range(n_cg_max)])
            return jnp.where(cnt < need, cq, q)

        pstar = lax.fori_loop(0, pos_bits, pos_body, jnp.zeros((QB, 128), I32)) + 1
        pstar = jnp.where(sentinel, 0, pstar)

        def body(kt, carry):
            for j in range(KT // 128):
                c0 = pl.multiple_of(kt * KT + j * 128, 128)
                kk = keys_ref[:, pl.ds(c0, 128)]
                tie = jnp.where(c0 + lane < pstar, 0.0, NEG_BIG)
                bias_ref[:, pl.ds(c0, 128)] = jnp.where(kk > tau, 0.0, jnp.where(kk == tau, tie, NEG_BIG))
            return carry

        lax.fori_loop(0, n_kt, body, 0)

    lax.cond(any_tie, tie_bias, fast_bias)

    @pl.when(i == 0)
    def _():
        def norm_body(r, best):
            x = ckv_ref[0, pl.ds(pl.multiple_of(r * KT, KT), KT), :].astype(F32)
            return jnp.maximum(best, jnp.max(jnp.sum(x * x, axis=1, keepdims=True), axis=0, keepdims=True))

        n_rows = ckv_ref.shape[1]
        kv_sq = lax.fori_loop(0, n_rows // KT, norm_body, jnp.zeros((1, 1), F32))
        kvmax_ref[...] = jnp.broadcast_to(jnp.sqrt(kv_sq), kvmax_ref.shape)

    qf = q_st.astype(F32)
    q_norm = jnp.sqrt(jnp.sum(qf * qf, axis=1, keepdims=True))
    mx_ref[...] = jnp.broadcast_to(q_norm, mx_ref.shape) * kvmax_ref[0:1, :] * 1.001 + 1e-3
    qprev_ref[...] = q_st


def _dsa(qix, misc, kix, qlat, ckv, topk):
    B, T, _ = kix.shape
    n_cg_max = -(-T // PLANE_COLS)
    t_pad = n_cg_max * PLANE_COLS
    pos_bits = (t_pad - 1).bit_length()

    n_blocks = T // QBLOCK

    def rows(w):
        return pl.BlockSpec((1, QBLOCK, w), lambda b, g: (b, jnp.minimum(g, n_blocks - 1), 0))

    def head_rows(h, w):
        return pl.BlockSpec((1, h, QBLOCK, w), lambda b, g: (b, 0, jnp.minimum(g, n_blocks - 1), 0))

    def per_b(w):
        return pl.BlockSpec((1, T, w), lambda b, g: (b, 0, 0))

    out_rows = pl.BlockSpec((1, SA_HEADS, QBLOCK, KV_RANK), lambda b, g: (b, 0, jnp.maximum(g - 1, 0), 0))

    return pl.pallas_call(
        functools.partial(_dsa_kernel, topk=topk, pos_bits=pos_bits, n_cg_max=n_cg_max),
        grid=(B, n_blocks + 1),
        in_specs=[head_rows(IDX_HEADS, IDX_DIM), rows(MISC_W), per_b(IDX_DIM), head_rows(SA_HEADS, KV_RANK),
                  per_b(KV_RANK)],
        out_specs=out_rows,
        out_shape=jax.ShapeDtypeStruct((B, SA_HEADS, T, KV_RANK), BF16),
        scratch_shapes=[pltpu.VMEM((QBLOCK, t_pad), I32),
                        pltpu.VMEM((n_cg_max, 32, QBLOCK, 128), I32),
                        pltpu.VMEM((QBLOCK, t_pad), F32),
                        pltpu.VMEM((SA_HEADS * QBLOCK, 128), F32),
                        pltpu.VMEM((SA_HEADS * QBLOCK, 128), F32),
                        pltpu.VMEM((SA_HEADS * QBLOCK, KV_RANK), F32),
                        pltpu.VMEM((8, 128), F32),
                        pltpu.VMEM((SA_HEADS * QBLOCK, KV_RANK), BF16)],
        compiler_params=pltpu.CompilerParams(dimension_semantics=("arbitrary", "arbitrary"),
                                             vmem_limit_bytes=VMEM_LIMIT),
        name="dsa",
    )(qix, misc, kix, qlat, ckv)


def _first_max(v, idx, axis):
    m = jnp.max(v, axis=axis, keepdims=True)
    big = jnp.int32(2 ** 30)
    first = jnp.min(jnp.where(v == m, idx, big), axis=axis, keepdims=True)
    return m, idx == first


def _outproj_kernel(x_ref, odn_ref, olat_ref, uv_ref, wo_ref, gt_ref, sc_ref, sh_ref, g2_ref, rwt_ref,
                    rb_ref, lstrict_ref, ustrict_ref, x1_ref, h2_ref, posrow_ref, poscol_ref, gatecol_ref,
                    cpad_ref):
    tm = x_ref.shape[1]
    parts = [odn_ref[0]]
    for hd in range(SA_HEADS):
        parts.append(_dot(olat_ref[0, hd], uv_ref[hd]).astype(BF16))
    mix = jnp.concatenate(parts, axis=-1)
    x1 = x_ref[0] + gt_ref[0] * _dot(mix, wo_ref[...])
    x1_ref[0] = x1
    h2 = x1 * lax.rsqrt(jnp.mean(x1 * x1, axis=-1, keepdims=True) + EPS) * g2_ref[...]
    h2 = h2 * (1.0 + sc_ref[0]) + sh_ref[0]
    h2_ref[0] = h2.astype(BF16)

    per_g = N_EXPERTS // N_GROUPS
    s = jax.nn.sigmoid(_nt_dot(rwt_ref[...], h2, HIGHEST))
    choice = s + rb_ref[...]
    ig = lax.broadcasted_iota(I32, (per_g, tm), 0)
    gscore = []
    for gidx in range(N_GROUPS):
        cg = choice[gidx * per_g:(gidx + 1) * per_g]
        m1, hot1 = _first_max(cg, ig, 0)
        gscore.append(m1 + jnp.max(jnp.where(hot1, -jnp.inf, cg), axis=0, keepdims=True))
    gsel = [jnp.zeros((1, tm), jnp.bool_) for _ in range(N_GROUPS)]
    for _ in range(TOPK_GROUPS):
        best = functools.reduce(jnp.maximum, gscore)
        found = jnp.zeros((1, tm), jnp.bool_)
        for gidx in range(N_GROUPS):
            hot = (gscore[gidx] == best) & jnp.logical_not(found)
            found = found | hot
            gsel[gidx] = gsel[gidx] | hot
            gscore[gidx] = jnp.where(hot, -jnp.inf, gscore[gidx])
    masked = jnp.concatenate(
        [jnp.where(gsel[gidx], choice[gidx * per_g:(gidx + 1) * per_g], -jnp.inf) for gidx in range(N_GROUPS)],
        axis=0)
    ei = lax.broadcasted_iota(I32, masked.shape, 0)
    gate = jnp.zeros(masked.shape, F32)
    hots = []
    for _ in range(TOP_K):
        _, hot = _first_max(masked, ei, 0)
        hots.append(hot)
        gate = jnp.where(hot, s, gate)
        masked = jnp.where(hot, -jnp.inf, masked)
    gate = gate / jnp.sum(gate, axis=0, keepdims=True) * ROUTED_SCALE

    picked = jnp.where(functools.reduce(jnp.logical_or, hots), 1.0, 0.0)
    cnt = jnp.sum(picked, axis=1, keepdims=True)
    cpad = jnp.floor((cnt + (RUN_ALIGN - 1)) * (1.0 / RUN_ALIGN)) * RUN_ALIGN
    cpad_b = jnp.broadcast_to(cpad, (N_EXPERTS, GATE_W))
    lbase = _dot(lstrict_ref[...], cpad_b, HIGHEST)[:, :1]
    rank = _dot(picked.astype(BF16), ustrict_ref[...])
    pos = lbase + rank
    ri = lax.broadcasted_iota(I32, (GATE_W, tm), 0)
    pos_rows = jnp.zeros((GATE_W, tm), F32)
    gate_rows = jnp.zeros((GATE_W, tm), F32)
    for k, hot in enumerate(hots):
        pos_rows = jnp.where(ri == k, jnp.sum(jnp.where(hot, pos, 0.0), axis=0, keepdims=True), pos_rows)
        gate_rows = jnp.where(ri == k, jnp.sum(jnp.where(hot, gate, 0.0), axis=0, keepdims=True), gate_rows)
    posrow_ref[0, 0] = pos_rows[:TOP_K].astype(I32)
    poscol_ref[0] = pos_rows.T.astype(I32)
    gatecol_ref[0] = gate_rows.T
    cpad_ref[0, 0] = cpad_b.astype(I32)


def _outproj(x, odn, olat, uv, wo, gt1, sc2, sh2, g2, rwt, rb, tm):
    B, T, D = x.shape
    nt = T // tm
    ex = jnp.arange(N_EXPERTS)
    lstrict = (ex[:, None] > ex[None, :]).astype(F32)
    tok = jnp.arange(tm)
    ustrict = (tok[:, None] < tok[None, :]).astype(BF16)

    def full(a):
        nd = a.ndim
        return pl.BlockSpec(a.shape, lambda b, i, _n=nd: (0,) * _n)

    def rows(w):
        return pl.BlockSpec((1, tm, w), lambda b, i: (b, i, 0))

    def per_tile(h, w):
        return pl.BlockSpec((1, 1, h, w), lambda b, i: (b, i, 0, 0))

    per_b = pl.BlockSpec((1, 1, D), lambda b, i: (b, 0, 0))
    return pl.pallas_call(
        _outproj_kernel,
        grid=(B, nt),
        in_specs=[rows(D), rows(DN_V),
                  pl.BlockSpec((1, SA_HEADS, tm, KV_RANK), lambda b, i: (b, 0, i, 0)),
                  full(uv), full(wo), per_b, per_b, per_b,
                  full(g2), full(rwt), full(rb), full(lstrict), full(ustrict)],
        out_specs=[rows(D), rows(D), per_tile(TOP_K, tm), rows(GATE_W), rows(GATE_W),
                   per_tile(N_EXPERTS, GATE_W)],
        out_shape=[jax.ShapeDtypeStruct((B, T, D), F32), jax.ShapeDtypeStruct((B, T, D), BF16),
                   jax.ShapeDtypeStruct((B, nt, TOP_K, tm), I32),
                   jax.ShapeDtypeStruct((B, T, GATE_W), I32),
                   jax.ShapeDtypeStruct((B, T, GATE_W), F32),
                   jax.ShapeDtypeStruct((B, nt, N_EXPERTS, GATE_W), I32)],
        compiler_params=pltpu.CompilerParams(dimension_semantics=("arbitrary", "arbitrary"),
                                             vmem_limit_bytes=VMEM_LIMIT),
        name="outproj",
    )(x, odn, olat, uv, wo, gt1, sc2, sh2, g2, rwt, rb, lstrict, ustrict)


def _piece_sizes(max_rows):
    sizes = []
    z = RUN_ALIGN
    while z <= max_rows:
        sizes.append(z)
        z *= 2
    return sizes[::-1]


def _for_run_pieces(length, max_rows, fn, rare_from=None):
    def pieces(sizes):
        for z in sizes:
            start = length & ~(2 * z - 1)

            @pl.when((length & z) != 0)
            def _(start=start, z=z):
                fn(start, z)

    sizes = _piece_sizes(max_rows)
    rare = [z for z in sizes if rare_from is not None and z >= rare_from]
    if rare:
        pl.when(length >= rare_from)(lambda: pieces(rare))
    pieces([z for z in sizes if z not in rare])


def _plan_kernel(cp_ref, off_ref, lb_ref, foff_ref, flen_ref, blk_ref, nused_ref):
    cp = cp_ref[...].astype(F32)
    n, ne = cp.shape
    ei = lax.broadcasted_iota(I32, (ne, ne), 0)
    ej = lax.broadcasted_iota(I32, (ne, ne), 1)
    si = lax.broadcasted_iota(I32, (n, n), 0)
    sj = lax.broadcasted_iota(I32, (n, n), 1)
    lb = _dot(cp, (ei < ej).astype(F32), HIGHEST)
    earlier_tiles = _dot((sj < si).astype(F32), cp, HIGHEST)
    rows_e = jnp.sum(cp, axis=0, keepdims=True)
    region = jnp.floor((rows_e + (ROW_BLOCK - 1)) * (1.0 / ROW_BLOCK)) * ROW_BLOCK
    region_b = jnp.broadcast_to(region, (ne, ne))
    rend_row = _dot(region_b, (ei <= ej).astype(F32), HIGHEST)[:1]
    rend_col = jnp.sum(jnp.where(ej <= ei, region_b, 0.0), axis=1, keepdims=True)
    base = rend_row - region
    total = jnp.max(rend_row, axis=1, keepdims=True)
    off_ref[...] = (base + earlier_tiles).astype(I32)
    lb_ref[...] = lb.astype(I32)
    lane = lax.broadcasted_iota(I32, (1, GATE_W), 1)
    pad = jnp.zeros((1, GATE_W - ne), F32)
    foff_ref[...] = jnp.where(lane == ne, total, jnp.concatenate([base + rows_e, pad], axis=1)).astype(I32)
    flen_ref[...] = jnp.concatenate([region - rows_e, pad], axis=1).astype(I32)
    n_used = total * (1.0 / ROW_BLOCK)
    nused_ref[...] = jnp.broadcast_to(n_used, nused_ref.shape).astype(I32)
    bi = lax.broadcasted_iota(I32, (ne, blk_ref.shape[1]), 1).astype(F32)
    ended = jnp.where(rend_col * (1.0 / ROW_BLOCK) <= jnp.minimum(bi, n_used - 1.0), 1.0, 0.0)
    blk_ref[...] = jnp.minimum(jnp.sum(ended, axis=0, keepdims=True), ne - 1.0).astype(I32)


def _plan(cp, n_blocks):
    n, ne = cp.shape
    nb_pad = -(-n_blocks // 128) * 128
    return pl.pallas_call(
        _plan_kernel,
        out_shape=[jax.ShapeDtypeStruct((n, ne), I32), jax.ShapeDtypeStruct((n, ne), I32),
                   jax.ShapeDtypeStruct((1, GATE_W), I32), jax.ShapeDtypeStruct((1, GATE_W), I32),
                   jax.ShapeDtypeStruct((1, nb_pad), I32), jax.ShapeDtypeStruct((1, GATE_W), I32)],
        name="moe_plan",
    )(cp)


def _dispatch_kernel(off_ref, cp_ref, lb_ref, foff_ref, flen_ref, h_ref, posrow_ref, xs_hbm, buf, zbuf, sem,
                     zsem, *, n_steps, tile):
    s = pl.program_id(0)
    slot = s % 2
    jmax = buf.shape[1]

    def run_copies(step, slot_, act):
        def body(e, carry):
            idx = step * N_EXPERTS + e
            lb = lb_ref[idx]
            of = off_ref[idx]

            def piece(start, z):
                act(pltpu.make_async_copy(
                    buf.at[slot_, pl.ds(pl.multiple_of(lb + start, RUN_ALIGN), z)],
                    xs_hbm.at[pl.ds(pl.multiple_of(of + start, RUN_ALIGN), z)], sem.at[slot_]))

            _for_run_pieces(cp_ref[idx], tile, piece, rare_from=tile // 4)
            return carry

        lax.fori_loop(0, N_EXPERTS, body, 0)

    def wait_runs(step, slot_):
        last_ = step * N_EXPERTS + N_EXPERTS - 1
        _for_run_pieces(lb_ref[last_] + cp_ref[last_], jmax, lambda start, z: pltpu.make_async_copy(
            buf.at[slot_, pl.ds(0, z)], xs_hbm.at[pl.ds(0, z)], sem.at[slot_]).wait())

    def fill_copies(act):
        def body(e, carry):
            fo = foff_ref[e]

            def piece(start, z):
                act(pltpu.make_async_copy(
                    zbuf.at[pl.ds(0, z)], xs_hbm.at[pl.ds(pl.multiple_of(fo + start, RUN_ALIGN), z)], zsem.at[0]))

            _for_run_pieces(flen_ref[e], ROW_BLOCK // 2, piece)
            return carry

        lax.fori_loop(0, N_EXPERTS, body, 0)

        def tail(r, carry):
            act(pltpu.make_async_copy(
                zbuf, xs_hbm.at[pl.ds(pl.multiple_of(foff_ref[N_EXPERTS] + r * zbuf.shape[0], RUN_ALIGN),
                                      zbuf.shape[0])], zsem.at[0]))
            return carry

        lax.fori_loop(0, (xs_hbm.shape[0] - foff_ref[N_EXPERTS]) // zbuf.shape[0], tail, 0)

    @pl.when(s == 0)
    def _():
        zbuf[...] = jnp.zeros(zbuf.shape, BF16)
        fill_copies(lambda c: c.start())

    @pl.when(s >= 2)
    def _():
        wait_runs(s - 2, slot)

    h = h_ref[...]
    last = s * N_EXPERTS + N_EXPERTS - 1
    jused = lb_ref[last] + cp_ref[last]
    def local_rows(jc):
        ji = (lax.broadcasted_iota(I32, (MOE_CHUNK, tile), 0) + jc * MOE_CHUNK).astype(jnp.int16)
        p = jnp.zeros((MOE_CHUNK, tile), BF16)
        for k in range(TOP_K):
            p = jnp.where(ji == posrow_ref[0, k:k + 1, :].astype(jnp.int16), jnp.ones((), BF16), p)
        return _dot(p, h).astype(BF16)

    n_full = TOP_K * tile // MOE_CHUNK
    rows_full = [local_rows(jc) for jc in range(n_full)]
    for jc in range(n_full):
        buf[slot, jc * MOE_CHUNK:(jc + 1) * MOE_CHUNK, :] = rows_full[jc]
    for jc in range(n_full, jmax // MOE_CHUNK):
        @pl.when(jc * MOE_CHUNK < jused)
        def _(jc=jc):
            buf[slot, jc * MOE_CHUNK:(jc + 1) * MOE_CHUNK, :] = local_rows(jc)

    run_copies(s, slot, lambda c: c.start())

    @pl.when(s == n_steps - 1)
    def _():
        if n_steps >= 2:
            wait_runs(s - 1, 1 - slot)
        wait_runs(s, slot)
        fill_copies(lambda c: c.wait())


def _dispatch(h2, posrow, off, cp, lb, foff, flen, cap, tile, jmax):
    n_tok, D = h2.shape
    n_steps = n_tok // tile
    return pl.pallas_call(
        functools.partial(_dispatch_kernel, n_steps=n_steps, tile=tile),
        grid_spec=pltpu.PrefetchScalarGridSpec(
            num_scalar_prefetch=5,
            grid=(n_steps,),
            in_specs=[pl.BlockSpec((tile, D), lambda s, *_: (s, 0)),
                      pl.BlockSpec((1, TOP_K, tile), lambda s, *_: (s, 0, 0))],
            out_specs=pl.BlockSpec(memory_space=pl.ANY),
            scratch_shapes=[pltpu.VMEM((2, jmax, D), BF16), pltpu.VMEM((ROW_BLOCK // 2, D), BF16),
                            pltpu.SemaphoreType.DMA((2,)), pltpu.SemaphoreType.DMA((1,))]),
        out_shape=jax.ShapeDtypeStruct((cap, D), BF16),
        compiler_params=pltpu.CompilerParams(dimension_semantics=("arbitrary",), vmem_limit_bytes=VMEM_LIMIT),
        name="moe_dispatch",
    )(off, cp, lb, foff, flen, h2, posrow)


def _expert_kernel(blk_e_ref, nused_ref, xs_ref, wg_ref, wu_ref, wd_ref, ys_ref, wg_bf, wu_bf, wd_bf):
    i = pl.program_id(0)
    used = i < nused_ref[0]

    @pl.when(used & ((i == 0) | (blk_e_ref[i] != blk_e_ref[jnp.maximum(i - 1, 0)])))
    def _():
        wg_bf[...] = wg_ref[0].astype(BF16)
        wu_bf[...] = wu_ref[0].astype(BF16)
        wd_bf[...] = wd_ref[0].astype(BF16)

    @pl.when(used)
    def _():
        xb = xs_ref[...]
        a = _silu(_dot(xb, wg_bf[...])) * _dot(xb, wu_bf[...])
        ys_ref[...] = _dot(a.astype(BF16), wd_bf[...]).astype(BF16)

    @pl.when(jnp.logical_not(used))
    def _():
        ys_ref[...] = jnp.zeros(ys_ref.shape, BF16)


def _experts(xs, blk_e, n_used, wg, wu, wd):
    cap, D = xs.shape

    def row_block(i, be, nu):
        return (jnp.minimum(i, nu[0] - 1), 0)

    def out_block(i, be, nu):
        return (i, 0)

    def weight(i, be, nu):
        return (be[i], 0, 0)

    return pl.pallas_call(
        _expert_kernel,
        grid_spec=pltpu.PrefetchScalarGridSpec(
            num_scalar_prefetch=2,
            grid=(cap // ROW_BLOCK,),
            in_specs=[pl.BlockSpec((ROW_BLOCK, D), row_block),
                      pl.BlockSpec((1, D, D_EXPERT), weight), pl.BlockSpec((1, D, D_EXPERT), weight),
                      pl.BlockSpec((1, D_EXPERT, D), weight)],
            out_specs=pl.BlockSpec((ROW_BLOCK, D), out_block),
            scratch_shapes=[pltpu.VMEM((D, D_EXPERT), BF16), pltpu.VMEM((D, D_EXPERT), BF16),
                            pltpu.VMEM((D_EXPERT, D), BF16)]),
        out_shape=jax.ShapeDtypeStruct((cap, D), BF16),
        compiler_params=pltpu.CompilerParams(dimension_semantics=("arbitrary",), vmem_limit_bytes=VMEM_LIMIT),
        name="moe_experts",
    )(blk_e, n_used, xs, wg, wu, wd)


def _combine_kernel(off_ref, cp_ref, lb_ref, ys_hbm, poscol_ref, gatecol_ref, h_ref, sg_ref, su_ref, sd_ref,
                    x1_ref, gt_ref, fg_ref, o_ref, buf, sem, acc_ref, *, n_steps, tile, final_norm):
    s = pl.program_id(0)
    slot = s % 2
    jmax = buf.shape[1]

    def run_copies(step, slot_, act):
        def body(e, carry):
            idx = step * N_EXPERTS + e
            lb = lb_ref[idx]
            of = off_ref[idx]

            def piece(start, z):
                act(pltpu.make_async_copy(
                    ys_hbm.at[pl.ds(pl.multiple_of(of + start, RUN_ALIGN), z)],
                    buf.at[slot_, pl.ds(pl.multiple_of(lb + start, RUN_ALIGN), z)], sem.at[slot_]))

            _for_run_pieces(cp_ref[idx], tile, piece, rare_from=tile // 4)
            return carry

        lax.fori_loop(0, N_EXPERTS, body, 0)

    @pl.when(s == 0)
    def _():
        run_copies(0, 0, lambda c: c.start())

    @pl.when(s + 1 < n_steps)
    def _():
        run_copies(s + 1, 1 - slot, lambda c: c.start())

    hb = h_ref[...]
    shared = (_silu(_dot(hb, sg_ref[...])) * _dot(hb, su_ref[...])).astype(BF16)
    acc_ref[...] = _dot(shared, sd_ref[...])

    last = s * N_EXPERTS + N_EXPERTS - 1
    jused = lb_ref[last] + cp_ref[last]
    _for_run_pieces(jused, jmax, lambda start, z: pltpu.make_async_copy(
        ys_hbm.at[pl.ds(0, z)], buf.at[slot, pl.ds(0, z)], sem.at[slot]).wait())

    def zero_body(r, carry):
        buf[slot, pl.ds(pl.multiple_of(jused + r * RUN_ALIGN, RUN_ALIGN), RUN_ALIGN), :] = jnp.zeros(
            (RUN_ALIGN, buf.shape[2]), BF16)
        return carry

    chunk_end = (jused + MOE_CHUNK - 1) // MOE_CHUNK * MOE_CHUNK
    lax.fori_loop(0, (chunk_end - jused) // RUN_ALIGN, zero_body, 0)

    def gate_rows(jc):
        ji = (lax.broadcasted_iota(I32, (tile, MOE_CHUNK), 1) + jc * MOE_CHUNK).astype(jnp.int16)
        g = jnp.zeros((tile, MOE_CHUNK), BF16)
        for k in range(TOP_K):
            g = jnp.where(ji == poscol_ref[:, k:k + 1].astype(jnp.int16), gatecol_ref[:, k:k + 1].astype(BF16), g)
        return g

    n_full = TOP_K * tile // MOE_CHUNK
    g_full = jnp.concatenate([gate_rows(jc) for jc in range(n_full)], axis=1)
    acc_ref[...] += _dot(g_full, buf[slot, 0:n_full * MOE_CHUNK, :])
    for jc in range(n_full, jmax // MOE_CHUNK):
        @pl.when(jc * MOE_CHUNK < jused)
        def _(jc=jc):
            acc_ref[...] += _dot(gate_rows(jc), buf[slot, jc * MOE_CHUNK:(jc + 1) * MOE_CHUNK, :])

    y = x1_ref[...] + gt_ref[0] * acc_ref[...]
    if final_norm:
        y = y * lax.rsqrt(jnp.mean(y * y, axis=-1, keepdims=True) + EPS) * fg_ref[...]
    o_ref[...] = y


def _combine(ys, poscol, gatecol, h2, sg, su, sd, x1, gt2, fg, off, cp, lb, tile, jmax, tiles_per_batch,
             final_norm):
    n_tok, D = h2.shape
    n_steps = n_tok // tile

    def full(a):
        nd = a.ndim
        return pl.BlockSpec(a.shape, lambda s, *_, _n=nd: (0,) * _n)

    def rows(w):
        return pl.BlockSpec((tile, w), lambda s, *_: (s, 0))

    return pl.pallas_call(
        functools.partial(_combine_kernel, n_steps=n_steps, tile=tile, final_norm=final_norm),
        grid_spec=pltpu.PrefetchScalarGridSpec(
            num_scalar_prefetch=3,
            grid=(n_steps,),
            in_specs=[pl.BlockSpec(memory_space=pl.ANY), rows(GATE_W), rows(GATE_W), rows(D),
                      full(sg), full(su), full(sd), rows(D),
                      pl.BlockSpec((1, 1, D), lambda s, *_: (s // tiles_per_batch, 0, 0)), full(fg)],
            out_specs=rows(D),
            scratch_shapes=[pltpu.VMEM((2, jmax, D), BF16), pltpu.SemaphoreType.DMA((2,)),
                            pltpu.VMEM((tile, D), F32)]),
        out_shape=jax.ShapeDtypeStruct((n_tok, D), F32),
        compiler_params=pltpu.CompilerParams(dimension_semantics=("arbitrary",), vmem_limit_bytes=VMEM_LIMIT),
        name="moe_combine",
    )(off, cp, lb, ys, poscol, gatecol, h2, sg, su, sd, x1, gt2, fg)


def _moe(h2, posrow, poscol, gatecol, cpad, wg, wu, wd, sg, su, sd, x1, gt2, fg, tile, final_norm):
    B, T, D = x1.shape
    n_tok = B * T
    n_tiles = n_tok // tile
    jmax = -(-(TOP_K * tile + N_EXPERTS * (RUN_ALIGN - 1)) // MOE_CHUNK) * MOE_CHUNK
    cap = -(-(TOP_K * n_tok + n_tiles * N_EXPERTS * (RUN_ALIGN - 1) + N_EXPERTS * (ROW_BLOCK - RUN_ALIGN))
            // ROW_BLOCK) * ROW_BLOCK

    cp = cpad[..., 0].reshape(n_tiles, N_EXPERTS)
    off, lb, foff, flen, blk_e, n_used = _plan(cp, cap // ROW_BLOCK)
    flat = lambda a: a.reshape(-1)

    xs = _dispatch(h2.reshape(n_tok, D), posrow.reshape(n_tiles, TOP_K, tile), flat(off), flat(cp), flat(lb),
                   foff[0], flen[0], cap, tile, jmax)
    ys = _experts(xs, blk_e[0], n_used[0], wg, wu, wd)
    out = _combine(ys, poscol.reshape(n_tok, GATE_W), gatecol.reshape(n_tok, GATE_W), h2.reshape(n_tok, D),
                   sg, su, sd, x1.reshape(n_tok, D), gt2, fg, flat(off), flat(cp), flat(lb), tile, jmax,
                   T // tile, final_norm)
    return out.reshape(B, T, D)


def _misc_lanes(vec, start):
    return jnp.zeros((1, MISC_W), F32).at[0, start:start + vec.shape[0]].set(vec.astype(F32))


def kernel(x, c, ada_w, ada_b, norm1_g, w_in, conv_w, a_log, dt_bias, dn_norm_g, kv_norm_g, w_uk, w_uv,
           idx_k_ln_g, idx_k_ln_b, w_out, norm2_g, router_w, router_b, exp_w_gate, exp_w_up, exp_w_down,
           sh_w_gate, sh_w_up, sh_w_down, final_g):
    B, T, D = x.shape
    depth = ada_w.shape[0]
    topk = min(IDX_TOPK_MAX, T // 4)
    tm = min(512, T)
    r_dn = min(256, T)

    cond_in = jnp.zeros((8, D), F32).at[:B].set(c)
    pos = jnp.arange(tm)
    tri = ((pos[:, None] // CHUNK == pos[None, :] // CHUNK) & (pos[:, None] >= pos[None, :])).astype(F32)

    for l in range(depth):
        mod = _ada(cond_in, ada_w[l], ada_b[l][None, :])[:B]
        sh1, sc1, gt1, sh2, sc2, gt2 = [m[:, None, :] for m in jnp.split(mod, 6, axis=-1)]

        offs = [0]
        for s in (DN_QK, DN_QK, DN_V, DN_V, DN_HEADS, DN_HEADS, SA_Q, KV_RANK, IDX_Q, IDX_DIM, IDX_HEADS):
            offs.append(offs[-1] + s)
        w = w_in[l]
        wc = w[:, offs[0]:offs[3]].astype(BF16)
        wz = w[:, offs[3]:offs[4]].astype(BF16)
        wq = w[:, offs[6]:offs[7]].astype(BF16)
        wkv = w[:, offs[7]:offs[8]].astype(BF16)
        wqi = w[:, offs[8]:offs[9]].astype(BF16)
        wm = jnp.concatenate([w[:, offs[9]:offs[10]], w[:, offs[4]:offs[5]], w[:, offs[5]:offs[6]],
                              w[:, offs[10]:offs[11]],
                              jnp.zeros((D, MISC_W - IDX_DIM - 2 * DN_HEADS - IDX_HEADS), F32)],
                             axis=1).astype(BF16)
        ukt = jnp.swapaxes(w_uk[l], 1, 2).astype(BF16)

        q, k, v, z, qlat, ckv, qix, kix, misc = _inproj(
            x, sc1, sh1, norm1_g[l][None, :], wc, wz, wq, wkv, wqi, wm, conv_w[l], ukt,
            kv_norm_g[l][None, :], _misc_lanes(idx_k_ln_g[l], M_KIX), _misc_lanes(idx_k_ln_b[l], M_KIX),
            _misc_lanes(a_log[l], M_A), _misc_lanes(dt_bias[l], M_A), tri, tm)

        odn = _deltanet(q, k, v, z, misc, dn_norm_g[l][None, :], r_dn)
        olat = _dsa(qix, misc, kix, qlat, ckv, topk)

        x1, h2, posrow, poscol, gatecol, cpad = _outproj(
            x, odn, olat, w_uv[l].astype(BF16), w_out[l].astype(BF16), gt1, sc2, sh2,
            norm2_g[l][None, :], router_w[l].T, router_b[l][:, None], tm)

        x = _moe(h2, posrow, poscol, gatecol, cpad, exp_w_gate[l], exp_w_up[l], exp_w_down[l],
                 sh_w_gate[l].astype(BF16), sh_w_up[l].astype(BF16),
                 sh_w_down[l].astype(BF16), x1, gt2, final_g[None, :], tm, l == depth - 1)
    return x
```

```python
import functools

import jax
import jax.numpy as jnp
from jax import lax
from jax.experimental import pallas as pl
from jax.experimental.pallas import tpu as pltpu

F32 = jnp.float32
BF16 = jnp.bfloat16
I32 = jnp.int32
HIGHEST = lax.Precision.HIGHEST

EPS = 1e-6
CHUNK = 64
DN_HEADS = 4
DN_DK = 128
DN_DV = 128
CONV_K = 4
SA_HEADS = 4
SA_DQK = 128
SA_DV = 128
KV_RANK = 256
IDX_HEADS = 4
IDX_DIM = 64
IDX_TOPK_MAX = 256
SM_SCALE = SA_DQK ** -0.5
LOG2E = 1.4426950408889634
IDX_W_SCALE = (IDX_HEADS * IDX_DIM) ** -0.5
N_EXPERTS = 64
TOP_K = 8
N_GROUPS = 8
TOPK_GROUPS = 4
D_EXPERT = 256
ROUTED_SCALE = 2.5
GATE_W = 128
RUN_ALIGN = 16
ROW_BLOCK = 1024
MOE_CHUNK = 512

DN_QK = DN_HEADS * DN_DK
DN_V = DN_HEADS * DN_DV
CONV_DIM = 2 * DN_QK + DN_V
SA_Q = SA_HEADS * SA_DQK
IDX_Q = IDX_HEADS * IDX_DIM

MISC_W = 128
M_KIX = 0
M_BETA = IDX_DIM
M_A = M_BETA + DN_HEADS
M_WIX = M_A + DN_HEADS

DN_SUB = 2 * CHUNK
QBLOCK = 256
SOFTMAX_TINY = 2.0 ** -100
KEY_TILE = 1024
PLANE_COLS = 32 * 128
PLANE_SHIFT = 12
PLANE_ROWS = 32
INT_MIN = -2 ** 31
NEG_BIG = -1e30
VMEM_LIMIT = 56 * 1024 * 1024


def _nt_dot(a, b, precision=None):
    return lax.dot_general(a, b, (((1,), (1,)), ((), ())), preferred_element_type=F32,
                           precision=precision)


def _dot(a, b, precision=None):
    return jnp.dot(a, b, preferred_element_type=F32, precision=precision)


def _silu(x):
    return x * jax.nn.sigmoid(x)


def _softplus(x):
    return jnp.maximum(x, 0.0) + jnp.log(1.0 + jnp.exp(-jnp.abs(x)))


def _ada_kernel(c_ref, w_ref, b_ref, o_ref):
    cond = _silu(c_ref[...])
    o_ref[...] = _dot(cond, w_ref[...], HIGHEST) + b_ref[...]


def _ada(c_pad, ada_w, ada_b):
    rows, d = c_pad.shape
    n_out = ada_w.shape[1]
    return pl.pallas_call(
        _ada_kernel,
        grid=(n_out // d,),
        in_specs=[pl.BlockSpec((rows, d), lambda j: (0, 0)),
                  pl.BlockSpec((d, d), lambda j: (0, j)),
                  pl.BlockSpec((1, d), lambda j: (0, j))],
        out_specs=pl.BlockSpec((rows, d), lambda j: (0, j)),
        out_shape=jax.ShapeDtypeStruct((rows, n_out), F32),
        compiler_params=pltpu.CompilerParams(vmem_limit_bytes=VMEM_LIMIT),
        name="ada",
    )(c_pad, ada_w, ada_b)


def _inproj_kernel(x_ref, sc_ref, sh_ref, g1_ref, wc_ref, wz_ref, wq_ref, wkv_ref, wqi_ref, wm_ref,
                   convw_ref, ukt_ref, kvg_ref, lng_ref, lnb_ref, alog_ref, dtb_ref, tri_ref,
                   q_ref, k_ref, v_ref, z_ref, qlat_ref, ckv_ref, qix_ref, kix_ref, misc_ref,
                   conv_buf):
    tm = x_ref.shape[1]
    i = pl.program_id(1)

    x = x_ref[0]
    h = x * lax.rsqrt(jnp.mean(x * x, axis=-1, keepdims=True) + EPS) * g1_ref[...]
    h = h * (1.0 + sc_ref[0]) + sh_ref[0]
    hb = h.astype(BF16)

    @pl.when(i == 0)
    def _():
        conv_buf[0:8, :] = jnp.zeros((8, CONV_DIM), F32)

    conv_buf[8:8 + tm, :] = _dot(hb, wc_ref[...])
    for grp, dst in ((0, q_ref), (1, k_ref), (2, v_ref)):
        cols = slice(grp * DN_QK, (grp + 1) * DN_QK)
        y = jnp.zeros((tm, DN_QK), F32)
        for j in range(CONV_K):
            y = y + convw_ref[j:j + 1, cols] * conv_buf[8 - (CONV_K - 1) + j:8 - (CONV_K - 1) + j + tm, cols]
        y = _silu(y)
        if grp < 2:
            outs = []
            for hd in range(DN_HEADS):
                yh = y[:, hd * DN_DK:(hd + 1) * DN_DK]
                yh = yh * lax.rsqrt(jnp.sum(yh * yh, axis=-1, keepdims=True) + EPS)
                if grp == 0:
                    yh = yh * (DN_DK ** -0.5)
                outs.append(yh)
            y = jnp.concatenate(outs, axis=-1)
        dst[0] = y
    conv_buf[0:8, :] = conv_buf[tm:tm + 8, :]

    z_ref[0] = _dot(hb, wz_ref[...])

    q_sa = _dot(hb, wq_ref[...]).astype(BF16)
    for hd in range(SA_HEADS):
        ql = _dot(q_sa[:, hd * SA_DQK:(hd + 1) * SA_DQK], ukt_ref[hd]) * (SM_SCALE * LOG2E)
        qlat_ref[0, hd] = ql.astype(BF16)

    ckv = _dot(hb, wkv_ref[...])
    ckv = ckv * lax.rsqrt(jnp.mean(ckv * ckv, axis=-1, keepdims=True) + EPS) * kvg_ref[...]
    ckv_ref[0] = ckv.astype(BF16)

    q_ix = _dot(hb, wqi_ref[...]).astype(BF16)
    for hd in range(IDX_HEADS):
        qix_ref[0, hd] = q_ix[:, hd * IDX_DIM:(hd + 1) * IDX_DIM]

    m = _dot(hb, wm_ref[...])
    lane = lax.broadcasted_iota(I32, (tm, MISC_W), 1)
    is_k = lane < IDX_DIM
    mu = jnp.sum(jnp.where(is_k, m, 0.0), axis=-1, keepdims=True) * (1.0 / IDX_DIM)
    kc = jnp.where(is_k, m - mu, 0.0)
    var = jnp.sum(kc * kc, axis=-1, keepdims=True) * (1.0 / IDX_DIM)
    kn = kc * lax.rsqrt(var + EPS) * lng_ref[...] + lnb_ref[...]
    kix_ref[0] = kn[:, :IDX_DIM].astype(BF16)

    beta = jax.nn.sigmoid(m)
    g = -jnp.exp(alog_ref[...]) * _softplus(m + dtb_ref[...])
    is_a = (lane >= M_A) & (lane < M_A + DN_HEADS)
    g = jnp.where(is_a, g, 0.0)
    gc = _dot(tri_ref[...], g, HIGHEST)
    is_b = (lane >= M_BETA) & (lane < M_BETA + DN_HEADS)
    is_w = (lane >= M_WIX) & (lane < M_WIX + IDX_HEADS)
    misc_ref[0] = jnp.where(is_b, beta, jnp.where(is_a, gc, jnp.where(is_w, m * IDX_W_SCALE, 0.0)))


def _inproj(x, sc1, sh1, g1, wc, wz, wq, wkv, wqi, wm, conv_w, ukt, kvg, lng, lnb, alog, dtb, tri, tm):
    B, T, D = x.shape
    nt = T // tm

    def full(a):
        nd = a.ndim
        return pl.BlockSpec(a.shape, lambda b, i, _n=nd: (0,) * _n)

    def rows(w):
        return pl.BlockSpec((1, tm, w), lambda b, i: (b, i, 0))

    per_b = pl.BlockSpec((1, 1, D), lambda b, i: (b, 0, 0))
    def head_rows(h, w):
        return pl.BlockSpec((1, h, tm, w), lambda b, i: (b, 0, i, 0))

    outs = [(None, DN_QK, F32), (None, DN_QK, F32), (None, DN_V, F32), (None, DN_V, F32),
            (SA_HEADS, KV_RANK, BF16), (None, KV_RANK, BF16), (IDX_HEADS, IDX_DIM, BF16),
            (None, IDX_DIM, BF16), (None, MISC_W, F32)]
    return pl.pallas_call(
        _inproj_kernel,
        grid=(B, nt),
        in_specs=[rows(D), per_b, per_b, full(g1), full(wc), full(wz), full(wq), full(wkv), full(wqi),
                  full(wm), full(conv_w), full(ukt), full(kvg), full(lng), full(lnb), full(alog),
                  full(dtb), full(tri)],
        out_specs=[rows(w) if h is None else head_rows(h, w) for h, w, _ in outs],
        out_shape=[jax.ShapeDtypeStruct((B, T, w) if h is None else (B, h, T, w), dt) for h, w, dt in outs],
        scratch_shapes=[pltpu.VMEM((tm + 8, CONV_DIM), F32)],
        compiler_params=pltpu.CompilerParams(dimension_semantics=("arbitrary", "arbitrary"),
                                             vmem_limit_bytes=VMEM_LIMIT),
        name="inproj",
    )(x, sc1, sh1, g1, wc, wz, wq, wkv, wqi, wm, conv_w, ukt, kvg, lng, lnb, alog, dtb, tri)


def _deltanet_kernel(q_ref, k_ref, v_ref, z_ref, misc_ref, ng_ref, o_ref, s_ref):
    R = q_ref.shape[1]
    n_chunks = R // CHUNK

    @pl.when(pl.program_id(1) == 0)
    def _():
        s_ref[...] = jnp.zeros(s_ref.shape, F32)

    misc = misc_ref[0]
    misc_t = misc.T
    SB = min(DN_SUB, R)
    row = lax.broadcasted_iota(I32, (SB, SB), 0)
    col = lax.broadcasted_iota(I32, (SB, SB), 1)
    same = (row // CHUNK) == (col // CHUNK)
    lower = same & (row >= col)
    strict = same & (row > col)
    eye = (row == col).astype(F32)

    def mm(a, b):
        return _dot(a.astype(BF16), b.astype(BF16))

    def mm3(a, b):
        ah = a.astype(BF16)
        bh = b.astype(BF16)
        al = (a - ah.astype(F32)).astype(BF16)
        bl = (b - bh.astype(F32)).astype(BF16)
        return _dot(jnp.concatenate([ah, ah, al], axis=1), jnp.concatenate([bh, bl, bh], axis=0))

    heads = range(DN_HEADS)
    subs = range(R // SB)
    chains = [(hd, sb) for hd in heads for sb in subs]
    cols = [slice(hd * DN_DK, (hd + 1) * DN_DK) for hd in heads]
    qh = [q_ref[0, :, cols[hd]] for hd in heads]
    kh = [k_ref[0, :, cols[hd]] for hd in heads]
    beta = [misc[:, M_BETA + hd:M_BETA + hd + 1] for hd in heads]
    gc_c = [misc[:, M_A + hd:M_A + hd + 1] for hd in heads]
    eg = [jnp.exp(gc_c[hd]) for hd in heads]
    kb = [kh[hd] * beta[hd] for hd in heads]
    rhs = [jnp.concatenate([v_ref[0, :, cols[hd]] * beta[hd], kb[hd] * eg[hd]], axis=-1) for hd in heads]
    q_dec = [qh[hd] * eg[hd] for hd in heads]

    def rows_of(sb):
        return slice(sb * SB, (sb + 1) * SB)

    decay, a, qk_sb = {}, {}, {}
    for hd, sb in chains:
        bs = rows_of(sb)
        gc_r = misc_t[M_A + hd:M_A + hd + 1, bs]
        decay[hd, sb] = jnp.where(lower, jnp.exp(jnp.where(lower, gc_c[hd][bs] - gc_r, 0.0)), 0.0)
    for hd, sb in chains:
        bs = rows_of(sb)
        khb = kh[hd][bs].astype(BF16)
        a[hd, sb] = jnp.where(strict, _nt_dot(kb[hd][bs].astype(BF16), khb) * decay[hd, sb], 0.0)
        qk_sb[hd, sb] = jnp.where(lower, _nt_dot(qh[hd][bs].astype(BF16), khb) * decay[hd, sb], 0.0)
    p = {ch: eye - a[ch] for ch in chains}
    xp = {ch: mm3(a[ch], a[ch]) for ch in chains}
    n_sq = 1
    while True:
        p = {ch: p[ch] + mm3(p[ch], xp[ch]) for ch in chains}
        n_sq *= 2
        if n_sq * 2 >= CHUNK:
            break
        xp = {ch: mm3(xp[ch], xp[ch]) for ch in chains}
    sol = {(hd, sb): mm(p[hd, sb], rhs[hd][rows_of(sb)]) for hd, sb in chains}

    def chunk_of(c):
        per = SB // CHUNK
        return c // per, slice((c % per) * CHUNK, (c % per + 1) * CHUNK)

    s = [s_ref[hd] for hd in heads]
    o_parts = [[] for _ in heads]
    for c in range(n_chunks):
        rs = slice(c * CHUNK, (c + 1) * CHUNK)
        sb, r = chunk_of(c)
        gl = [gc_c[hd][(c + 1) * CHUNK - 1:(c + 1) * CHUNK, :] for hd in heads]
        k_dec = [kh[hd][rs] * jnp.exp(gl[hd] - gc_c[hd][rs]) for hd in heads]
        v_new = [sol[hd, sb][r, :DN_DV] - mm(sol[hd, sb][r, DN_DV:], s[hd]) for hd in heads]
        for hd in heads:
            o_parts[hd].append(mm(q_dec[hd][rs], s[hd]) + mm(qk_sb[hd, sb][r, r], v_new[hd]))
        s = [s[hd] * jnp.exp(gl[hd]) + mm(k_dec[hd].T, v_new[hd]) for hd in heads]
    for hd in heads:
        s_ref[hd] = s[hd]
        o = jnp.concatenate(o_parts[hd], axis=0)
        o = o * lax.rsqrt(jnp.mean(o * o, axis=-1, keepdims=True) + EPS) * ng_ref[...]
        o_ref[0, :, cols[hd]] = (o * _silu(z_ref[0, :, cols[hd]])).astype(BF16)


def _deltanet(q, k, v, z, misc, ng, R):
    B, T, _ = q.shape

    def rows(w):
        return pl.BlockSpec((1, R, w), lambda b, i: (b, i, 0))

    return pl.pallas_call(
        _deltanet_kernel,
        grid=(B, T // R),
        in_specs=[rows(DN_QK), rows(DN_QK), rows(DN_V), rows(DN_V), rows(MISC_W),
                  pl.BlockSpec((1, DN_DV), lambda b, i: (0, 0))],
        out_specs=rows(DN_V),
        out_shape=jax.ShapeDtypeStruct((B, T, DN_V), BF16),
        scratch_shapes=[pltpu.VMEM((DN_HEADS, DN_DK, DN_DV), F32)],
        compiler_params=pltpu.CompilerParams(dimension_semantics=("arbitrary", "arbitrary"),
                                             vmem_limit_bytes=VMEM_LIMIT),
        name="deltanet",
    )(q, k, v, z, misc, ng)


def _bit_transpose32(words):
    w = list(words)
    j = 16
    m = 0x0000FFFF
    while j:
        k = 0
        m_i32 = jnp.int32(m - (1 << 32) if m >= (1 << 31) else m)
        while k < 32:
            t = (w[k] ^ lax.shift_right_logical(w[k + j], jnp.full_like(w[k], j))) & m_i32
            w[k] = w[k] ^ t
            w[k + j] = w[k + j] ^ jnp.left_shift(t, j)
            k = (k + j + 1) & ~j
        j >>= 1
        m = (m ^ (m << j)) & 0xFFFFFFFF
    return w


def _dsa_kernel(qix_ref, misc_ref, kix_ref, qlat_ref, ckv_ref, o_ref, keys_ref, planes_ref, bias_ref,
                mx_ref, l_ref, acc_ref, kvmax_ref, qprev_ref, *, topk, pos_bits, n_cg_max):
    g = pl.program_id(1)
    n_blocks = pl.num_programs(1) - 1
    i = g
    QB = QBLOCK
    KT = KEY_TILE
    has_front = g < n_blocks
    has_back = g >= 1
    n_kt = jnp.where(has_front, (i * QB + QB + KT - 1) // KT, 0)
    n_kt_back = (g * QB + KT - 1) // KT

    rowi = lax.broadcasted_iota(I32, (QB, KT), 0)
    coli = lax.broadcasted_iota(I32, (QB, KT), 1)
    limit = i * QB + (rowi // CHUNK + 1) * CHUNK

    misc = misc_ref[0]
    qix = qix_ref[0].reshape(IDX_HEADS * QB, IDX_DIM)
    q_st = qlat_ref[0].reshape(SA_HEADS * QB, KV_RANK)

    def score_body(masked, kt, carry):
        k0 = pl.multiple_of(kt * KT, KT)
        kx = kix_ref[0, pl.ds(k0, KT), :]
        rel = jnp.maximum(_nt_dot(qix, kx), 0.0)
        sc = jnp.zeros((QB, KT), F32)
        for hd in range(IDX_HEADS):
            sc = sc + misc[:, M_WIX + hd:M_WIX + hd + 1] * rel[hd * QB:(hd + 1) * QB]
        sc = jnp.where(sc == 0.0, 0.0, sc)
        bits = pltpu.bitcast(sc, I32)
        key = jnp.where(bits < 0, bits ^ 0x7FFFFFFF, bits)
        keys_ref[:, pl.ds(k0, KT)] = jnp.where(k0 + coli < limit, key, INT_MIN) if masked else key
        return carry

    def logit_chunks(k0):
        kv = ckv_ref[0, pl.ds(k0, KT), :]
        s = _nt_dot(qprev_ref[...], kv)
        bias = bias_ref[:, pl.ds(k0, KT)]
        chunks = []
        for j in range(KT // 128):
            bj = bias[:, j * 128:(j + 1) * 128]
            chunks.append(s[:, j * 128:(j + 1) * 128] + jnp.concatenate([bj] * SA_HEADS, axis=0))
        return kv, chunks

    def pv_body(kt, carry):
        kv, chunks = logit_chunks(pl.multiple_of(kt * KT, KT))
        shift = mx_ref[...]
        ps = [jnp.exp2(c - shift) for c in chunks]
        l_ref[...] = functools.reduce(jnp.add, ps, l_ref[...])
        p = jnp.concatenate([pj.astype(BF16) for pj in ps], axis=1)
        acc_ref[...] += _dot(p, kv)
        return carry

    def clear_sums():
        l_ref[...] = jnp.zeros(l_ref.shape, F32)
        acc_ref[...] = jnp.zeros(acc_ref.shape, F32)

    def write_back_block():
        l_row = jnp.sum(l_ref[...], axis=-1, keepdims=True)
        o_ref[0] = (acc_ref[...] / l_row).astype(BF16).reshape(SA_HEADS, QB, KV_RANK)
        return l_row

    pl.when(has_back)(clear_sums)

    n_open = jnp.where(has_front, (i * QB + CHUNK) // KT, 0)
    n_fused = jnp.minimum(n_open, n_kt_back)

    def fused_body(kt, carry):
        pv_body(kt, carry)
        return score_body(False, kt, carry)

    lax.fori_loop(0, n_fused, fused_body, 0)
    lax.fori_loop(n_fused, n_kt_back, pv_body, 0)
    lax.fori_loop(n_fused, n_open, functools.partial(score_body, False), 0)
    lax.fori_loop(n_open, n_kt, functools.partial(score_body, True), 0)

    @pl.when(has_back)
    def _():
        l_fast = write_back_block()

        @pl.when(jnp.min(l_fast) < SOFTMAX_TINY)
        def _():
            mx_ref[...] = jnp.full(mx_ref.shape, NEG_BIG, F32)

            def max_body(kt, carry):
                _, chunks = logit_chunks(pl.multiple_of(kt * KT, KT))
                mx_ref[...] = functools.reduce(jnp.maximum, chunks, mx_ref[...])
                return carry

            lax.fori_loop(0, n_kt_back, max_body, 0)
            mx_ref[...] = jnp.broadcast_to(jnp.max(mx_ref[...], axis=-1, keepdims=True), mx_ref.shape)
            clear_sums()
            lax.fori_loop(0, n_kt_back, pv_body, 0)
            write_back_block()

    n_cg = (n_kt * KT + PLANE_COLS - 1) // PLANE_COLS

    def fill_body(kt, carry):
        keys_ref[:, pl.ds(pl.multiple_of(kt * KT, KT), KT)] = jnp.full((QB, KT), INT_MIN, I32)
        return carry

    lax.fori_loop(n_kt, n_cg * (PLANE_COLS // KT), fill_body, 0)

    @pl.when(i == 0)
    def _():
        planes_ref[...] = jnp.zeros(planes_ref.shape, I32)

    def plane_body(step, carry):
        c = step // (QB // PLANE_ROWS)
        r0 = pl.multiple_of((step % (QB // PLANE_ROWS)) * PLANE_ROWS, PLANE_ROWS)
        words = [keys_ref[pl.ds(r0, PLANE_ROWS), pl.ds(pl.multiple_of(c * PLANE_COLS + j * 128, 128), 128)]
                 for j in range(32)]
        for b, plane in enumerate(_bit_transpose32(words)):
            planes_ref[c, b, pl.ds(r0, PLANE_ROWS), :] = ~plane if b == 0 else plane
        return carry

    lax.fori_loop(0, n_cg * (QB // PLANE_ROWS), plane_body, 0)

    ones_mat = jnp.ones((128, 128), BF16)

    def lane_count(words):
        pc = functools.reduce(jnp.add, [lax.population_count(x) for x in words])
        return _dot(pc.astype(F32).astype(BF16), ones_mat).astype(I32)

    def sel_body(groups, step, carry):
        cand, n_gt, tau_u = carry
        hi = [cand[c] & planes_ref[c, 2 * step] for c in groups]
        lo = [cand[c] ^ hi[c] for c in groups]
        d3 = [hi[c] & planes_ref[c, 2 * step + 1] for c in groups]
        d2 = [hi[c] ^ d3[c] for c in groups]
        d1 = [lo[c] & planes_ref[c, 2 * step + 1] for c in groups]
        d0 = [lo[c] ^ d1[c] for c in groups]
        a3 = n_gt + lane_count(d3)
        a2 = a3 + lane_count(d2)
        a1 = a2 + lane_count(d1)
        is3 = a3 >= topk
        is2 = a2 >= topk
        is1 = a1 >= topk
        cand = tuple(jnp.where(is3, d3[c], jnp.where(is2, d2[c], jnp.where(is1, d1[c], d0[c]))) for c in groups)
        n_gt = jnp.where(is3, n_gt, jnp.where(is2, a3, jnp.where(is1, a2, a1)))
        digit = jnp.where(is3, 3, jnp.where(is2, 2, jnp.where(is1, 1, 0)))
        return cand, n_gt, tau_u | jnp.left_shift(digit, 30 - 2 * step)

    def radix_select(n_groups):
        groups = range(n_groups)
        start = (tuple(jnp.full((QB, 128), -1, I32) for _ in groups), jnp.zeros((QB, 128), I32),
                 jnp.zeros((QB, 128), I32))
        cand_, n_gt_, tau_ = lax.fori_loop(0, 16, functools.partial(sel_body, groups), start)
        rest = tuple(jnp.zeros((QB, 128), I32) for _ in range(n_cg_max - n_groups))
        return cand_ + rest, n_gt_, tau_

    cand, n_gt, tau_u = lax.switch(n_cg - 1, [functools.partial(radix_select, g + 1) for g in range(n_cg_max)])
    tau = tau_u ^ INT_MIN
    sentinel = tau == INT_MIN
    cand = tuple(jnp.where(sentinel, 0, cand[c]) for c in range(n_cg_max))
    need = topk - n_gt
    any_tie = jnp.max(jnp.where(lane_count(cand) > need, 1, 0)) > 0

    def fast_bias():
        floor = jnp.where(sentinel, INT_MIN + 1, tau)

        def body(kt, carry):
            for j in range(KT // 128):
                cols = pl.ds(pl.multiple_of(kt * KT + j * 128, 128), 128)
                bias_ref[:, cols] = jnp.where(keys_ref[:, cols] >= floor, 0.0, NEG_BIG)
            return carry

        lax.fori_loop(0, n_kt, body, 0)

    def tie_bias():
        lane = lax.broadcasted_iota(I32, (QB, 128), 1)

        def pos_mask(p, c):
            cg = lax.shift_right_logical(p, jnp.full_like(p, PLANE_SHIFT))
            j0 = lax.shift_right_logical(p, jnp.full_like(p, 7)) & 31
            below = ~lax.shift_right_logical(jnp.full_like(p, -1), j0)
            bit = lax.shift_right_logical(jnp.full_like(p, INT_MIN), j0)
            word = below | jnp.where(lane < (p & 127), bit, 0)
            return jnp.where(cg > c, -1, jnp.where(cg == c, word, 0))

        def pos_body(b, q):
            cq = q + jnp.left_shift(jnp.int32(1), pos_bits - 1 - b)
            cnt = lane_count([cand[c] & pos_mask(cq, c) for c in range(n_cg_max)])
            return jnp.where(cnt < need, cq, q)

        pstar = lax.fori_loop(0, pos_bits, pos_body, jnp.zeros((QB, 128), I32)) + 1
        pstar = jnp.where(sentinel, 0, pstar)

        def body(kt, carry):
            for j in range(KT // 128):
                c0 = pl.multiple_of(kt * KT + j * 128, 128)
                kk = keys_ref[:, pl.ds(c0, 128)]
                tie = jnp.where(c0 + lane < pstar, 0.0, NEG_BIG)
                bias_ref[:, pl.ds(c0, 128)] = jnp.where(kk > tau, 0.0, jnp.where(kk == tau, tie, NEG_BIG))
            return carry

        lax.fori_loop(0, n_kt, body, 0)

    lax.cond(any_tie, tie_bias, fast_bias)

    @pl.when(i == 0)
    def _():
        def norm_body(r, best):
            x = ckv_ref[0, pl.ds(pl.multiple_of(r * KT, KT), KT), :].astype(F32)
            return jnp.maximum(best, jnp.max(jnp.sum(x * x, axis=1, keepdims=True), axis=0, keepdims=True))

        n_rows = ckv_ref.shape[1]
        kv_sq = lax.fori_loop(0, n_rows // KT, norm_body, jnp.zeros((1, 1), F32))
        kvmax_ref[...] = jnp.broadcast_to(jnp.sqrt(kv_sq), kvmax_ref.shape)

    qf = q_st.astype(F32)
    q_norm = jnp.sqrt(jnp.sum(qf * qf, axis=1, keepdims=True))
    mx_ref[...] = jnp.broadcast_to(q_norm, mx_ref.shape) * kvmax_ref[0:1, :] * 1.001 + 1e-3
    qprev_ref[...] = q_st


def _dsa(qix, misc, kix, qlat, ckv, topk):
    B, T, _ = kix.shape
    n_cg_max = -(-T // PLANE_COLS)
    t_pad = n_cg_max * PLANE_COLS
    pos_bits = (t_pad - 1).bit_length()

    n_blocks = T // QBLOCK

    def rows(w):
        return pl.BlockSpec((1, QBLOCK, w), lambda b, g: (b, jnp.minimum(g, n_blocks - 1), 0))

    def head_rows(h, w):
        return pl.BlockSpec((1, h, QBLOCK, w), lambda b, g: (b, 0, jnp.minimum(g, n_blocks - 1), 0))

    def per_b(w):
        return pl.BlockSpec((1, T, w), lambda b, g: (b, 0, 0))

    out_rows = pl.BlockSpec((1, SA_HEADS, QBLOCK, KV_RANK), lambda b, g: (b, 0, jnp.maximum(g - 1, 0), 0))

    return pl.pallas_call(
        functools.partial(_dsa_kernel, topk=topk, pos_bits=pos_bits, n_cg_max=n_cg_max),
        grid=(B, n_blocks + 1),
        in_specs=[head_rows(IDX_HEADS, IDX_DIM), rows(MISC_W), per_b(IDX_DIM), head_rows(SA_HEADS, KV_RANK),
                  per_b(KV_RANK)],
        out_specs=out_rows,
        out_shape=jax.ShapeDtypeStruct((B, SA_HEADS, T, KV_RANK), BF16),
        scratch_shapes=[pltpu.VMEM((QBLOCK, t_pad), I32),
                        pltpu.VMEM((n_cg_max, 32, QBLOCK, 128), I32),
                        pltpu.VMEM((QBLOCK, t_pad), F32),
                        pltpu.VMEM((SA_HEADS * QBLOCK, 128), F32),
                        pltpu.VMEM((SA_HEADS * QBLOCK, 128), F32),
                        pltpu.VMEM((SA_HEADS * QBLOCK, KV_RANK), F32),
                        pltpu.VMEM((8, 128), F32),
                        pltpu.VMEM((SA_HEADS * QBLOCK, KV_RANK), BF16)],
        compiler_params=pltpu.CompilerParams(dimension_semantics=("arbitrary", "arbitrary"),
                                             vmem_limit_bytes=VMEM_LIMIT),
        name="dsa",
    )(qix, misc, kix, qlat, ckv)


def _first_max(v, idx, axis):
    m = jnp.max(v, axis=axis, keepdims=True)
    big = jnp.int32(2 ** 30)
    first = jnp.min(jnp.where(v == m, idx, big), axis=axis, keepdims=True)
    return m, idx == first


def _outproj_kernel(x_ref, odn_ref, olat_ref, uv_ref, wo_ref, gt_ref, sc_ref, sh_ref, g2_ref, rwt_ref,
                    rb_ref, lstrict_ref, ustrict_ref, x1_ref, h2_ref, posrow_ref, poscol_ref, gatecol_ref,
                    cpad_ref):
    tm = x_ref.shape[1]
    parts = [odn_ref[0]]
    for hd in range(SA_HEADS):
        parts.append(_dot(olat_ref[0, hd], uv_ref[hd]).astype(BF16))
    mix = jnp.concatenate(parts, axis=-1)
    x1 = x_ref[0] + gt_ref[0] * _dot(mix, wo_ref[...])
    x1_ref[0] = x1
    h2 = x1 * lax.rsqrt(jnp.mean(x1 * x1, axis=-1, keepdims=True) + EPS) * g2_ref[...]
    h2 = h2 * (1.0 + sc_ref[0]) + sh_ref[0]
    h2_ref[0] = h2.astype(BF16)

    per_g = N_EXPERTS // N_GROUPS
    s = jax.nn.sigmoid(_nt_dot(rwt_ref[...], h2, HIGHEST))
    choice = s + rb_ref[...]
    ig = lax.broadcasted_iota(I32, (per_g, tm), 0)
    gscore = []
    for gidx in range(N_GROUPS):
        cg = choice[gidx * per_g:(gidx + 1) * per_g]
        m1, hot1 = _first_max(cg, ig, 0)
        gscore.append(m1 + jnp.max(jnp.where(hot1, -jnp.inf, cg), axis=0, keepdims=True))
    gsel = [jnp.zeros((1, tm), jnp.bool_) for _ in range(N_GROUPS)]
    for _ in range(TOPK_GROUPS):
        best = functools.reduce(jnp.maximum, gscore)
        found = jnp.zeros((1, tm), jnp.bool_)
        for gidx in range(N_GROUPS):
            hot = (gscore[gidx] == best) & jnp.logical_not(found)
            found = found | hot
            gsel[gidx] = gsel[gidx] | hot
            gscore[gidx] = jnp.where(hot, -jnp.inf, gscore[gidx])
    masked = jnp.concatenate(
        [jnp.where(gsel[gidx], choice[gidx * per_g:(gidx + 1) * per_g], -jnp.inf) for gidx in range(N_GROUPS)],
        axis=0)
    ei = lax.broadcasted_iota(I32, masked.shape, 0)
    gate = jnp.zeros(masked.shape, F32)
    hots = []
    for _ in range(TOP_K):
        _, hot = _first_max(masked, ei, 0)
        hots.append(hot)
        gate = jnp.where(hot, s, gate)
        masked = jnp.where(hot, -jnp.inf, masked)
    gate = gate / jnp.sum(gate, axis=0, keepdims=True) * ROUTED_SCALE

    picked = jnp.where(functools.reduce(jnp.logical_or, hots), 1.0, 0.0)
    cnt = jnp.sum(picked, axis=1, keepdims=True)
    cpad = jnp.floor((cnt + (RUN_ALIGN - 1)) * (1.0 / RUN_ALIGN)) * RUN_ALIGN
    cpad_b = jnp.broadcast_to(cpad, (N_EXPERTS, GATE_W))
    lbase = _dot(lstrict_ref[...], cpad_b, HIGHEST)[:, :1]
    rank = _dot(picked.astype(BF16), ustrict_ref[...])
    pos = lbase + rank
    ri = lax.broadcasted_iota(I32, (GATE_W, tm), 0)
    pos_rows = jnp.zeros((GATE_W, tm), F32)
    gate_rows = jnp.zeros((GATE_W, tm), F32)
    for k, hot in enumerate(hots):
        pos_rows = jnp.where(ri == k, jnp.sum(jnp.where(hot, pos, 0.0), axis=0, keepdims=True), pos_rows)
        gate_rows = jnp.where(ri == k, jnp.sum(jnp.where(hot, gate, 0.0), axis=0, keepdims=True), gate_rows)
    posrow_ref[0, 0] = pos_rows[:TOP_K].astype(I32)
    poscol_ref[0] = pos_rows.T.astype(I32)
    gatecol_ref[0] = gate_rows.T
    cpad_ref[0, 0] = cpad_b.astype(I32)


def _outproj(x, odn, olat, uv, wo, gt1, sc2, sh2, g2, rwt, rb, tm):
    B, T, D = x.shape
    nt = T // tm
    ex = jnp.arange(N_EXPERTS)
    lstrict = (ex[:, None] > ex[None, :]).astype(F32)
    tok = jnp.arange(tm)
    ustrict = (tok[:, None] < tok[None, :]).astype(BF16)

    def full(a):
        nd = a.ndim
        return pl.BlockSpec(a.shape, lambda b, i, _n=nd: (0,) * _n)

    def rows(w):
        return pl.BlockSpec((1, tm, w), lambda b, i: (b, i, 0))

    def per_tile(h, w):
        return pl.BlockSpec((1, 1, h, w), lambda b, i: (b, i, 0, 0))

    per_b = pl.BlockSpec((1, 1, D), lambda b, i: (b, 0, 0))
    return pl.pallas_call(
        _outproj_kernel,
        grid=(B, nt),
        in_specs=[rows(D), rows(DN_V),
                  pl.BlockSpec((1, SA_HEADS, tm, KV_RANK), lambda b, i: (b, 0, i, 0)),
                  full(uv), full(wo), per_b, per_b, per_b,
                  full(g2), full(rwt), full(rb), full(lstrict), full(ustrict)],
        out_specs=[rows(D), rows(D), per_tile(TOP_K, tm), rows(GATE_W), rows(GATE_W),
                   per_tile(N_EXPERTS, GATE_W)],
        out_shape=[jax.ShapeDtypeStruct((B, T, D), F32), jax.ShapeDtypeStruct((B, T, D), BF16),
                   jax.ShapeDtypeStruct((B, nt, TOP_K, tm), I32),
                   jax.ShapeDtypeStruct((B, T, GATE_W), I32),
                   jax.ShapeDtypeStruct((B, T, GATE_W), F32),
                   jax.ShapeDtypeStruct((B, nt, N_EXPERTS, GATE_W), I32)],
        compiler_params=pltpu.CompilerParams(dimension_semantics=("arbitrary", "arbitrary"),
                                             vmem_limit_bytes=VMEM_LIMIT),
        name="outproj",
    )(x, odn, olat, uv, wo, gt1, sc2, sh2, g2, rwt, rb, lstrict, ustrict)


def _piece_sizes(max_rows):
    sizes = []
    z = RUN_ALIGN
    while z <= max_rows:
        sizes.append(z)
        z *= 2
    return sizes[::-1]


def _for_run_pieces(length, max_rows, fn, rare_from=None):
    def pieces(sizes):
        for z in sizes:
            start = length & ~(2 * z - 1)

            @pl.when((length & z) != 0)
            def _(start=start, z=z):
                fn(start, z)

    sizes = _piece_sizes(max_rows)
    rare = [z for z in sizes if rare_from is not None and z >= rare_from]
    if rare:
        pl.when(length >= rare_from)(lambda: pieces(rare))
    pieces([z for z in sizes if z not in rare])


def _plan_kernel(cp_ref, off_ref, lb_ref, foff_ref, flen_ref, blk_ref, nused_ref):
    cp = cp_ref[...].astype(F32)
    n, ne = cp.shape
    ei = lax.broadcasted_iota(I32, (ne, ne), 0)
    ej = lax.broadcasted_iota(I32, (ne, ne), 1)
    si = lax.broadcasted_iota(I32, (n, n), 0)
    sj = lax.broadcasted_iota(I32, (n, n), 1)
    lb = _dot(cp, (ei < ej).astype(F32), HIGHEST)
    earlier_tiles = _dot((sj < si).astype(F32), cp, HIGHEST)
    rows_e = jnp.sum(cp, axis=0, keepdims=True)
    region = jnp.floor((rows_e + (ROW_BLOCK - 1)) * (1.0 / ROW_BLOCK)) * ROW_BLOCK
    region_b = jnp.broadcast_to(region, (ne, ne))
    rend_row = _dot(region_b, (ei <= ej).astype(F32), HIGHEST)[:1]
    rend_col = jnp.sum(jnp.where(ej <= ei, region_b, 0.0), axis=1, keepdims=True)
    base = rend_row - region
    total = jnp.max(rend_row, axis=1, keepdims=True)
    off_ref[...] = (base + earlier_tiles).astype(I32)
    lb_ref[...] = lb.astype(I32)
    lane = lax.broadcasted_iota(I32, (1, GATE_W), 1)
    pad = jnp.zeros((1, GATE_W - ne), F32)
    foff_ref[...] = jnp.where(lane == ne, total, jnp.concatenate([base + rows_e, pad], axis=1)).astype(I32)
    flen_ref[...] = jnp.concatenate([region - rows_e, pad], axis=1).astype(I32)
    n_used = total * (1.0 / ROW_BLOCK)
    nused_ref[...] = jnp.broadcast_to(n_used, nused_ref.shape).astype(I32)
    bi = lax.broadcasted_iota(I32, (ne, blk_ref.shape[1]), 1).astype(F32)
    ended = jnp.where(rend_col * (1.0 / ROW_BLOCK) <= jnp.minimum(bi, n_used - 1.0), 1.0, 0.0)
    blk_ref[...] = jnp.minimum(jnp.sum(ended, axis=0, keepdims=True), ne - 1.0).astype(I32)


def _plan(cp, n_blocks):
    n, ne = cp.shape
    nb_pad = -(-n_blocks // 128) * 128
    return pl.pallas_call(
        _plan_kernel,
        out_shape=[jax.ShapeDtypeStruct((n, ne), I32), jax.ShapeDtypeStruct((n, ne), I32),
                   jax.ShapeDtypeStruct((1, GATE_W), I32), jax.ShapeDtypeStruct((1, GATE_W), I32),
                   jax.ShapeDtypeStruct((1, nb_pad), I32), jax.ShapeDtypeStruct((1, GATE_W), I32)],
        name="moe_plan",
    )(cp)


def _dispatch_kernel(off_ref, cp_ref, lb_ref, foff_ref, flen_ref, h_ref, posrow_ref, xs_hbm, buf, zbuf, sem,
                     zsem, *, n_steps, tile):
    s = pl.program_id(0)
    slot = s % 2
    jmax = buf.shape[1]

    def run_copies(step, slot_, act):
        def body(e, carry):
            idx = step * N_EXPERTS + e
            lb = lb_ref[idx]
            of = off_ref[idx]

            def piece(start, z):
                act(pltpu.make_async_copy(
                    buf.at[slot_, pl.ds(pl.multiple_of(lb + start, RUN_ALIGN), z)],
                    xs_hbm.at[pl.ds(pl.multiple_of(of + start, RUN_ALIGN), z)], sem.at[slot_]))

            _for_run_pieces(cp_ref[idx], tile, piece, rare_from=tile // 4)
            return carry

        lax.fori_loop(0, N_EXPERTS, body, 0)

    def wait_runs(step, slot_):
        last_ = step * N_EXPERTS + N_EXPERTS - 1
        _for_run_pieces(lb_ref[last_] + cp_ref[last_], jmax, lambda start, z: pltpu.make_async_copy(
            buf.at[slot_, pl.ds(0, z)], xs_hbm.at[pl.ds(0, z)], sem.at[slot_]).wait())

    def fill_copies(act):
        def body(e, carry):
            fo = foff_ref[e]

            def piece(start, z):
                act(pltpu.make_async_copy(
                    zbuf.at[pl.ds(0, z)], xs_hbm.at[pl.ds(pl.multiple_of(fo + start, RUN_ALIGN), z)], zsem.at[0]))

            _for_run_pieces(flen_ref[e], ROW_BLOCK // 2, piece)
            return carry

        lax.fori_loop(0, N_EXPERTS, body, 0)

        def tail(r, carry):
            act(pltpu.make_async_copy(
                zbuf, xs_hbm.at[pl.ds(pl.multiple_of(foff_ref[N_EXPERTS] + r * zbuf.shape[0], RUN_ALIGN),
                                      zbuf.shape[0])], zsem.at[0]))
            return carry

        lax.fori_loop(0, (xs_hbm.shape[0] - foff_ref[N_EXPERTS]) // zbuf.shape[0], tail, 0)

    @pl.when(s == 0)
    def _():
        zbuf[...] = jnp.zeros(zbuf.shape, BF16)
        fill_copies(lambda c: c.start())

    @pl.when(s >= 2)
    def _():
        wait_runs(s - 2, slot)

    h = h_ref[...]
    last = s * N_EXPERTS + N_EXPERTS - 1
    jused = lb_ref[last] + cp_ref[last]
    def local_rows(jc):
        ji = (lax.broadcasted_iota(I32, (MOE_CHUNK, tile), 0) + jc * MOE_CHUNK).astype(jnp.int16)
        p = jnp.zeros((MOE_CHUNK, tile), BF16)
        for k in range(TOP_K):
            p = jnp.where(ji == posrow_ref[0, k:k + 1, :].astype(jnp.int16), jnp.ones((), BF16), p)
        return _dot(p, h).astype(BF16)

    n_full = TOP_K * tile // MOE_CHUNK
    rows_full = [local_rows(jc) for jc in range(n_full)]
    for jc in range(n_full):
        buf[slot, jc * MOE_CHUNK:(jc + 1) * MOE_CHUNK, :] = rows_full[jc]
    for jc in range(n_full, jmax // MOE_CHUNK):
        @pl.when(jc * MOE_CHUNK < jused)
        def _(jc=jc):
            buf[slot, jc * MOE_CHUNK:(jc + 1) * MOE_CHUNK, :] = local_rows(jc)

    run_copies(s, slot, lambda c: c.start())

    @pl.when(s == n_steps - 1)
    def _():
        if n_steps >= 2:
            wait_runs(s - 1, 1 - slot)
        wait_runs(s, slot)
        fill_copies(lambda c: c.wait())


def _dispatch(h2, posrow, off, cp, lb, foff, flen, cap, tile, jmax):
    n_tok, D = h2.shape
    n_steps = n_tok // tile
    return pl.pallas_call(
        functools.partial(_dispatch_kernel, n_steps=n_steps, tile=tile),
        grid_spec=pltpu.PrefetchScalarGridSpec(
            num_scalar_prefetch=5,
            grid=(n_steps,),
            in_specs=[pl.BlockSpec((tile, D), lambda s, *_: (s, 0)),
                      pl.BlockSpec((1, TOP_K, tile), lambda s, *_: (s, 0, 0))],
            out_specs=pl.BlockSpec(memory_space=pl.ANY),
            scratch_shapes=[pltpu.VMEM((2, jmax, D), BF16), pltpu.VMEM((ROW_BLOCK // 2, D), BF16),
                            pltpu.SemaphoreType.DMA((2,)), pltpu.SemaphoreType.DMA((1,))]),
        out_shape=jax.ShapeDtypeStruct((cap, D), BF16),
        compiler_params=pltpu.CompilerParams(dimension_semantics=("arbitrary",), vmem_limit_bytes=VMEM_LIMIT),
        name="moe_dispatch",
    )(off, cp, lb, foff, flen, h2, posrow)


def _expert_kernel(blk_e_ref, nused_ref, xs_ref, wg_ref, wu_ref, wd_ref, ys_ref, wg_bf, wu_bf, wd_bf):
    i = pl.program_id(0)
    used = i < nused_ref[0]

    @pl.when(used & ((i == 0) | (blk_e_ref[i] != blk_e_ref[jnp.maximum(i - 1, 0)])))
    def _():
        wg_bf[...] = wg_ref[0].astype(BF16)
        wu_bf[...] = wu_ref[0].astype(BF16)
        wd_bf[...] = wd_ref[0].astype(BF16)

    @pl.when(used)
    def _():
        xb = xs_ref[...]
        a = _silu(_dot(xb, wg_bf[...])) * _dot(xb, wu_bf[...])
        ys_ref[...] = _dot(a.astype(BF16), wd_bf[...]).astype(BF16)

    @pl.when(jnp.logical_not(used))
    def _():
        ys_ref[...] = jnp.zeros(ys_ref.shape, BF16)


def _experts(xs, blk_e, n_used, wg, wu, wd):
    cap, D = xs.shape

    def row_block(i, be, nu):
        return (jnp.minimum(i, nu[0] - 1), 0)

    def out_block(i, be, nu):
        return (i, 0)

    def weight(i, be, nu):
        return (be[i], 0, 0)

    return pl.pallas_call(
        _expert_kernel,
        grid_spec=pltpu.PrefetchScalarGridSpec(
            num_scalar_prefetch=2,
            grid=(cap // ROW_BLOCK,),
            in_specs=[pl.BlockSpec((ROW_BLOCK, D), row_block),
                      pl.BlockSpec((1, D, D_EXPERT), weight), pl.BlockSpec((1, D, D_EXPERT), weight),
                      pl.BlockSpec((1, D_EXPERT, D), weight)],
            out_specs=pl.BlockSpec((ROW_BLOCK, D), out_block),
            scratch_shapes=[pltpu.VMEM((D, D_EXPERT), BF16), pltpu.VMEM((D, D_EXPERT), BF16),
                            pltpu.VMEM((D_EXPERT, D), BF16)]),
        out_shape=jax.ShapeDtypeStruct((cap, D), BF16),
        compiler_params=pltpu.CompilerParams(dimension_semantics=("arbitrary",), vmem_limit_bytes=VMEM_LIMIT),
        name="moe_experts",
    )(blk_e, n_used, xs, wg, wu, wd)


def _combine_kernel(off_ref, cp_ref, lb_ref, ys_hbm, poscol_ref, gatecol_ref, h_ref, sg_ref, su_ref, sd_ref,
                    x1_ref, gt_ref, fg_ref, o_ref, buf, sem, acc_ref, *, n_steps, tile, final_norm):
    s = pl.program_id(0)
    slot = s % 2
    jmax = buf.shape[1]

    def run_copies(step, slot_, act):
        def body(e, carry):
            idx = step * N_EXPERTS + e
            lb = lb_ref[idx]
            of = off_ref[idx]

            def piece(start, z):
                act(pltpu.make_async_copy(
                    ys_hbm.at[pl.ds(pl.multiple_of(of + start, RUN_ALIGN), z)],
                    buf.at[slot_, pl.ds(pl.multiple_of(lb + start, RUN_ALIGN), z)], sem.at[slot_]))

            _for_run_pieces(cp_ref[idx], tile, piece, rare_from=tile // 4)
            return carry

        lax.fori_loop(0, N_EXPERTS, body, 0)

    @pl.when(s == 0)
    def _():
        run_copies(0, 0, lambda c: c.start())

    @pl.when(s + 1 < n_steps)
    def _():
        run_copies(s + 1, 1 - slot, lambda c: c.start())

    hb = h_ref[...]
    shared = (_silu(_dot(hb, sg_ref[...])) * _dot(hb, su_ref[...])).astype(BF16)
    acc_ref[...] = _dot(shared, sd_ref[...])

    last = s * N_EXPERTS + N_EXPERTS - 1
    jused = lb_ref[last] + cp_ref[last]
    _for_run_pieces(jused, jmax, lambda start, z: pltpu.make_async_copy(
        ys_hbm.at[pl.ds(0, z)], buf.at[slot, pl.ds(0, z)], sem.at[slot]).wait())

    def zero_body(r, carry):
        buf[slot, pl.ds(pl.multiple_of(jused + r * RUN_ALIGN, RUN_ALIGN), RUN_ALIGN), :] = jnp.zeros(
            (RUN_ALIGN, buf.shape[2]), BF16)
        return carry

    chunk_end = (jused + MOE_CHUNK - 1) // MOE_CHUNK * MOE_CHUNK
    lax.fori_loop(0, (chunk_end - jused) // RUN_ALIGN, zero_body, 0)

    def gate_rows(jc):
        ji = (lax.broadcasted_iota(I32, (tile, MOE_CHUNK), 1) + jc * MOE_CHUNK).astype(jnp.int16)
        g = jnp.zeros((tile, MOE_CHUNK), BF16)
        for k in range(TOP_K):
            g = jnp.where(ji == poscol_ref[:, k:k + 1].astype(jnp.int16), gatecol_ref[:, k:k + 1].astype(BF16), g)
        return g

    n_full = TOP_K * tile // MOE_CHUNK
    g_full = jnp.concatenate([gate_rows(jc) for jc in range(n_full)], axis=1)
    acc_ref[...] += _dot(g_full, buf[slot, 0:n_full * MOE_CHUNK, :])
    for jc in range(n_full, jmax // MOE_CHUNK):
        @pl.when(jc * MOE_CHUNK < jused)
        def _(jc=jc):
            acc_ref[...] += _dot(gate_rows(jc), buf[slot, jc * MOE_CHUNK:(jc + 1) * MOE_CHUNK, :])

    y = x1_ref[...] + gt_ref[0] * acc_ref[...]
    if final_norm:
        y = y * lax.rsqrt(jnp.mean(y * y, axis=-1, keepdims=True) + EPS) * fg_ref[...]
    o_ref[...] = y


def _combine(ys, poscol, gatecol, h2, sg, su, sd, x1, gt2, fg, off, cp, lb, tile, jmax, tiles_per_batch,
             final_norm):
    n_tok, D = h2.shape
    n_steps = n_tok // tile

    def full(a):
        nd = a.ndim
        return pl.BlockSpec(a.shape, lambda s, *_, _n=nd: (0,) * _n)

    def rows(w):
        return pl.BlockSpec((tile, w), lambda s, *_: (s, 0))

    return pl.pallas_call(
        functools.partial(_combine_kernel, n_steps=n_steps, tile=tile, final_norm=final_norm),
        grid_spec=pltpu.PrefetchScalarGridSpec(
            num_scalar_prefetch=3,
            grid=(n_steps,),
            in_specs=[pl.BlockSpec(memory_space=pl.ANY), rows(GATE_W), rows(GATE_W), rows(D),
                      full(sg), full(su), full(sd), rows(D),
                      pl.BlockSpec((1, 1, D), lambda s, *_: (s // tiles_per_batch, 0, 0)), full(fg)],
            out_specs=rows(D),
            scratch_shapes=[pltpu.VMEM((2, jmax, D), BF16), pltpu.SemaphoreType.DMA((2,)),
                            pltpu.VMEM((tile, D), F32)]),
        out_shape=jax.ShapeDtypeStruct((n_tok, D), F32),
        compiler_params=pltpu.CompilerParams(dimension_semantics=("arbitrary",), vmem_limit_bytes=VMEM_LIMIT),
        name="moe_combine",
    )(off, cp, lb, ys, poscol, gatecol, h2, sg, su, sd, x1, gt2, fg)


def _moe(h2, posrow, poscol, gatecol, cpad, wg, wu, wd, sg, su, sd, x1, gt2, fg, tile, final_norm):
    B, T, D = x1.shape
    n_tok = B * T
    n_tiles = n_tok // tile
    jmax = -(-(TOP_K * tile + N_EXPERTS * (RUN_ALIGN - 1)) // MOE_CHUNK) * MOE_CHUNK
    cap = -(-(TOP_K * n_tok + n_tiles * N_EXPERTS * (RUN_ALIGN - 1) + N_EXPERTS * (ROW_BLOCK - RUN_ALIGN))
            // ROW_BLOCK) * ROW_BLOCK

    cp = cpad[..., 0].reshape(n_tiles, N_EXPERTS)
    off, lb, foff, flen, blk_e, n_used = _plan(cp, cap // ROW_BLOCK)
    flat = lambda a: a.reshape(-1)

    xs = _dispatch(h2.reshape(n_tok, D), posrow.reshape(n_tiles, TOP_K, tile), flat(off), flat(cp), flat(lb),
                   foff[0], flen[0], cap, tile, jmax)
    ys = _experts(xs, blk_e[0], n_used[0], wg, wu, wd)
    out = _combine(ys, poscol.reshape(n_tok, GATE_W), gatecol.reshape(n_tok, GATE_W), h2.reshape(n_tok, D),
                   sg, su, sd, x1.reshape(n_tok, D), gt2, fg, flat(off), flat(cp), flat(lb), tile, jmax,
                   T // tile, final_norm)
    return out.reshape(B, T, D)


def _misc_lanes(vec, start):
    return jnp.zeros((1, MISC_W), F32).at[0, start:start + vec.shape[0]].set(vec.astype(F32))


def kernel(x, c, ada_w, ada_b, norm1_g, w_in, conv_w, a_log, dt_bias, dn_norm_g, kv_norm_g, w_uk, w_uv,
           idx_k_ln_g, idx_k_ln_b, w_out, norm2_g, router_w, router_b, exp_w_gate, exp_w_up, exp_w_down,
           sh_w_gate, sh_w_up, sh_w_down, final_g):
    B, T, D = x.shape
    depth = ada_w.shape[0]
    topk = min(IDX_TOPK_MAX, T // 4)
    tm = min(512, T)
    r_dn = min(256, T)

    cond_in = jnp.zeros((8, D), F32).at[:B].set(c)
    pos = jnp.arange(tm)
    tri = ((pos[:, None] // CHUNK == pos[None, :] // CHUNK) & (pos[:, None] >= pos[None, :])).astype(F32)

    for l in range(depth):
        mod = _ada(cond_in, ada_w[l], ada_b[l][None, :])[:B]
        sh1, sc1, gt1, sh2, sc2, gt2 = [m[:, None, :] for m in jnp.split(mod, 6, axis=-1)]

        offs = [0]
        for s in (DN_QK, DN_QK, DN_V, DN_V, DN_HEADS, DN_HEADS, SA_Q, KV_RANK, IDX_Q, IDX_DIM, IDX_HEADS):
            offs.append(offs[-1] + s)
        w = w_in[l]
        wc = w[:, offs[0]:offs[3]].astype(BF16)
        wz = w[:, offs[3]:offs[4]].astype(BF16)
        wq = w[:, offs[6]:offs[7]].astype(BF16)
        wkv = w[:, offs[7]:offs[8]].astype(BF16)
        wqi = w[:, offs[8]:offs[9]].astype(BF16)
        wm = jnp.concatenate([w[:, offs[9]:offs[10]], w[:, offs[4]:offs[5]], w[:, offs[5]:offs[6]],
                              w[:, offs[10]:offs[11]],
                              jnp.zeros((D, MISC_W - IDX_DIM - 2 * DN_HEADS - IDX_HEADS), F32)],
                             axis=1).astype(BF16)
        ukt = jnp.swapaxes(w_uk[l], 1, 2).astype(BF16)

        q, k, v, z, qlat, ckv, qix, kix, misc = _inproj(
            x, sc1, sh1, norm1_g[l][None, :], wc, wz, wq, wkv, wqi, wm, conv_w[l], ukt,
            kv_norm_g[l][None, :], _misc_lanes(idx_k_ln_g[l], M_KIX), _misc_lanes(idx_k_ln_b[l], M_KIX),
            _misc_lanes(a_log[l], M_A), _misc_lanes(dt_bias[l], M_A), tri, tm)

        odn = _deltanet(q, k, v, z, misc, dn_norm_g[l][None, :], r_dn)
        olat = _dsa(qix, misc, kix, qlat, ckv, topk)

        x1, h2, posrow, poscol, gatecol, cpad = _outproj(
            x, odn, olat, w_uv[l].astype(BF16), w_out[l].astype(BF16), gt1, sc2, sh2,
            norm2_g[l][None, :], router_w[l].T, router_b[l][:, None], tm)

        x = _moe(h2, posrow, poscol, gatecol, cpad, exp_w_gate[l], exp_w_up[l], exp_w_down[l],
                 sh_w_gate[l].astype(BF16), sh_w_up[l].astype(BF16),
                 sh_w_down[l].astype(BF16), x1, gt2, final_g[None, :], tm, l == depth - 1)
    return x
```

```python
import functools

import jax
import jax.numpy as jnp
from jax import lax
from jax.experimental import pallas as pl
from jax.experimental.pallas import tpu as pltpu

F32 = jnp.float32
BF16 = jnp.bfloat16
I32 = jnp.int32
HIGHEST = lax.Precision.HIGHEST

EPS = 1e-6
CHUNK = 64
DN_HEADS = 4
DN_DK = 128
DN_DV = 128
CONV_K = 4
SA_HEADS = 4
SA_DQK = 128
SA_DV = 128
KV_RANK = 256
IDX_HEADS = 4
IDX_DIM = 64
IDX_TOPK_MAX = 256
SM_SCALE = SA_DQK ** -0.5
LOG2E = 1.4426950408889634
IDX_W_SCALE = (IDX_HEADS * IDX_DIM) ** -0.5
N_EXPERTS = 64
TOP_K = 8
N_GROUPS = 8
TOPK_GROUPS = 4
D_EXPERT = 256
ROUTED_SCALE = 2.5
GATE_W = 128
RUN_ALIGN = 16
ROW_BLOCK = 1024
MOE_CHUNK = 512

DN_QK = DN_HEADS * DN_DK
DN_V = DN_HEADS * DN_DV
CONV_DIM = 2 * DN_QK + DN_V
SA_Q = SA_HEADS * SA_DQK
IDX_Q = IDX_HEADS * IDX_DIM

MISC_W = 128
M_KIX = 0
M_BETA = IDX_DIM
M_A = M_BETA + DN_HEADS
M_WIX = M_A + DN_HEADS

DN_SUB = 2 * CHUNK
QBLOCK = 256
SOFTMAX_TINY = 2.0 ** -100
KEY_TILE = 1024
PLANE_COLS = 32 * 128
PLANE_SHIFT = 12
PLANE_ROWS = 32
INT_MIN = -2 ** 31
NEG_BIG = -1e30
VMEM_LIMIT = 56 * 1024 * 1024


def _nt_dot(a, b, precision=None):
    return lax.dot_general(a, b, (((1,), (1,)), ((), ())), preferred_element_type=F32,
                           precision=precision)


def _dot(a, b, precision=None):
    return jnp.dot(a, b, preferred_element_type=F32, precision=precision)


def _silu(x):
    return x * jax.nn.sigmoid(x)


def _softplus(x):
    return jnp.maximum(x, 0.0) + jnp.log(1.0 + jnp.exp(-jnp.abs(x)))


def _ada_kernel(c_ref, w_ref, b_ref, o_ref):
    cond = _silu(c_ref[...])
    o_ref[...] = _dot(cond, w_ref[...], HIGHEST) + b_ref[...]


def _ada(c_pad, ada_w, ada_b):
    rows, d = c_pad.shape
    n_out = ada_w.shape[1]
    return pl.pallas_call(
        _ada_kernel,
        grid=(n_out // d,),
        in_specs=[pl.BlockSpec((rows, d), lambda j: (0, 0)),
                  pl.BlockSpec((d, d), lambda j: (0, j)),
                  pl.BlockSpec((1, d), lambda j: (0, j))],
        out_specs=pl.BlockSpec((rows, d), lambda j: (0, j)),
        out_shape=jax.ShapeDtypeStruct((rows, n_out), F32),
        compiler_params=pltpu.CompilerParams(vmem_limit_bytes=VMEM_LIMIT),
        name="ada",
    )(c_pad, ada_w, ada_b)


def _inproj_kernel(x_ref, sc_ref, sh_ref, g1_ref, wc_ref, wz_ref, wq_ref, wkv_ref, wqi_ref, wm_ref,
                   convw_ref, ukt_ref, kvg_ref, lng_ref, lnb_ref, alog_ref, dtb_ref, tri_ref,
                   q_ref, k_ref, v_ref, z_ref, qlat_ref, ckv_ref, qix_ref, kix_ref, misc_ref,
                   conv_buf):
    tm = x_ref.shape[1]
    i = pl.program_id(1)

    x = x_ref[0]
    h = x * lax.rsqrt(jnp.mean(x * x, axis=-1, keepdims=True) + EPS) * g1_ref[...]
    h = h * (1.0 + sc_ref[0]) + sh_ref[0]
    hb = h.astype(BF16)

    @pl.when(i == 0)
    def _():
        conv_buf[0:8, :] = jnp.zeros((8, CONV_DIM), F32)

    conv_buf[8:8 + tm, :] = _dot(hb, wc_ref[...])
    for grp, dst in ((0, q_ref), (1, k_ref), (2, v_ref)):
        cols = slice(grp * DN_QK, (grp + 1) * DN_QK)
        y = jnp.zeros((tm, DN_QK), F32)
        for j in range(CONV_K):
            y = y + convw_ref[j:j + 1, cols] * conv_buf[8 - (CONV_K - 1) + j:8 - (CONV_K - 1) + j + tm, cols]
        y = _silu(y)
        if grp < 2:
            outs = []
            for hd in range(DN_HEADS):
                yh = y[:, hd * DN_DK:(hd + 1) * DN_DK]
                yh = yh * lax.rsqrt(jnp.sum(yh * yh, axis=-1, keepdims=True) + EPS)
                if grp == 0:
                    yh = yh * (DN_DK ** -0.5)
                outs.append(yh)
            y = jnp.concatenate(outs, axis=-1)
        dst[0] = y
    conv_buf[0:8, :] = conv_buf[tm:tm + 8, :]

    z_ref[0] = _dot(hb, wz_ref[...])

    q_sa = _dot(hb, wq_ref[...]).astype(BF16)
    for hd in range(SA_HEADS):
        ql = _dot(q_sa[:, hd * SA_DQK:(hd + 1) * SA_DQK], ukt_ref[hd]) * (SM_SCALE * LOG2E)
        qlat_ref[0, hd] = ql.astype(BF16)

    ckv = _dot(hb, wkv_ref[...])
    ckv = ckv * lax.rsqrt(jnp.mean(ckv * ckv, axis=-1, keepdims=True) + EPS) * kvg_ref[...]
    ckv_ref[0] = ckv.astype(BF16)

    q_ix = _dot(hb, wqi_ref[...]).astype(BF16)
    for hd in range(IDX_HEADS):
        qix_ref[0, hd] = q_ix[:, hd * IDX_DIM:(hd + 1) * IDX_DIM]

    m = _dot(hb, wm_ref[...])
    lane = lax.broadcasted_iota(I32, (tm, MISC_W), 1)
    is_k = lane < IDX_DIM
    mu = jnp.sum(jnp.where(is_k, m, 0.0), axis=-1, keepdims=True) * (1.0 / IDX_DIM)
    kc = jnp.where(is_k, m - mu, 0.0)
    var = jnp.sum(kc * kc, axis=-1, keepdims=True) * (1.0 / IDX_DIM)
    kn = kc * lax.rsqrt(var + EPS) * lng_ref[...] + lnb_ref[...]
    kix_ref[0] = kn[:, :IDX_DIM].astype(BF16)

    beta = jax.nn.sigmoid(m)
    g = -jnp.exp(alog_ref[...]) * _softplus(m + dtb_ref[...])
    is_a = (lane >= M_A) & (lane < M_A + DN_HEADS)
    g = jnp.where(is_a, g, 0.0)
    gc = _dot(tri_ref[...], g, HIGHEST)
    is_b = (lane >= M_BETA) & (lane < M_BETA + DN_HEADS)
    is_w = (lane >= M_WIX) & (lane < M_WIX + IDX_HEADS)
    misc_ref[0] = jnp.where(is_b, beta, jnp.where(is_a, gc, jnp.where(is_w, m * IDX_W_SCALE, 0.0)))


def _inproj(x, sc1, sh1, g1, wc, wz, wq, wkv, wqi, wm, conv_w, ukt, kvg, lng, lnb, alog, dtb, tri, tm):
    B, T, D = x.shape
    nt = T // tm

    def full(a):
        nd = a.ndim
        return pl.BlockSpec(a.shape, lambda b, i, _n=nd: (0,) * _n)

    def rows(w):
        return pl.BlockSpec((1, tm, w), lambda b, i: (b, i, 0))

    per_b = pl.BlockSpec((1, 1, D), lambda b, i: (b, 0, 0))
    def head_rows(h, w):
        return pl.BlockSpec((1, h, tm, w), lambda b, i: (b, 0, i, 0))

    outs = [(None, DN_QK, F32), (None, DN_QK, F32), (None, DN_V, F32), (None, DN_V, F32),
            (SA_HEADS, KV_RANK, BF16), (None, KV_RANK, BF16), (IDX_HEADS, IDX_DIM, BF16),
            (None, IDX_DIM, BF16), (None, MISC_W, F32)]
    return pl.pallas_call(
        _inproj_kernel,
        grid=(B, nt),
        in_specs=[rows(D), per_b, per_b, full(g1), full(wc), full(wz), full(wq), full(wkv), full(wqi),
                  full(wm), full(conv_w), full(ukt), full(kvg), full(lng), full(lnb), full(alog),
                  full(dtb), full(tri)],
        out_specs=[rows(w) if h is None else head_rows(h, w) for h, w, _ in outs],
        out_shape=[jax.ShapeDtypeStruct((B, T, w) if h is None else (B, h, T, w), dt) for h, w, dt in outs],
        scratch_shapes=[pltpu.VMEM((tm + 8, CONV_DIM), F32)],
        compiler_params=pltpu.CompilerParams(dimension_semantics=("arbitrary", "arbitrary"),
                                             vmem_limit_bytes=VMEM_LIMIT),
        name="inproj",
    )(x, sc1, sh1, g1, wc, wz, wq, wkv, wqi, wm, conv_w, ukt, kvg, lng, lnb, alog, dtb, tri)


def _deltanet_kernel(q_ref, k_ref, v_ref, z_ref, misc_ref, ng_ref, o_ref, s_ref):
    R = q_ref.shape[1]
    n_chunks = R // CHUNK

    @pl.when(pl.program_id(1) == 0)
    def _():
        s_ref[...] = jnp.zeros(s_ref.shape, F32)

    misc = misc_ref[0]
    misc_t = misc.T
    SB = min(DN_SUB, R)
    row = lax.broadcasted_iota(I32, (SB, SB), 0)
    col = lax.broadcasted_iota(I32, (SB, SB), 1)
    same = (row // CHUNK) == (col // CHUNK)
    lower = same & (row >= col)
    strict = same & (row > col)
    eye = (row == col).astype(F32)

    def mm(a, b):
        return _dot(a.astype(BF16), b.astype(BF16))

    def mm3(a, b):
        ah = a.astype(BF16)
        bh = b.astype(BF16)
        al = (a - ah.astype(F32)).astype(BF16)
        bl = (b - bh.astype(F32)).astype(BF16)
        return _dot(jnp.concatenate([ah, ah, al], axis=1), jnp.concatenate([bh, bl, bh], axis=0))

    heads = range(DN_HEADS)
    subs = range(R // SB)
    chains = [(hd, sb) for hd in heads for sb in subs]
    cols = [slice(hd * DN_DK, (hd + 1) * DN_DK) for hd in heads]
    qh = [q_ref[0, :, cols[hd]] for hd in heads]
    kh = [k_ref[0, :, cols[hd]] for hd in heads]
    beta = [misc[:, M_BETA + hd:M_BETA + hd + 1] for hd in heads]
    gc_c = [misc[:, M_A + hd:M_A + hd + 1] for hd in heads]
    eg = [jnp.exp(gc_c[hd]) for hd in heads]
    kb = [kh[hd] * beta[hd] for hd in heads]
    rhs = [jnp.concatenate([v_ref[0, :, cols[hd]] * beta[hd], kb[hd] * eg[hd]], axis=-1) for hd in heads]
    q_dec = [qh[hd] * eg[hd] for hd in heads]

    def rows_of(sb):
        return slice(sb * SB, (sb + 1) * SB)

    decay, a, qk_sb = {}, {}, {}
    for hd, sb in chains:
        bs = rows_of(sb)
        gc_r = misc_t[M_A + hd:M_A + hd + 1, bs]
        decay[hd, sb] = jnp.where(lower, jnp.exp(jnp.where(lower, gc_c[hd][bs] - gc_r, 0.0)), 0.0)
    for hd, sb in chains:
        bs = rows_of(sb)
        khb = kh[hd][bs].astype(BF16)
        a[hd, sb] = jnp.where(strict, _nt_dot(kb[hd][bs].astype(BF16), khb) * decay[hd, sb], 0.0)
        qk_sb[hd, sb] = jnp.where(lower, _nt_dot(qh[hd][bs].astype(BF16), khb) * decay[hd, sb], 0.0)
    p = {ch: eye - a[ch] for ch in chains}
    xp = {ch: mm3(a[ch], a[ch]) for ch in chains}
    n_sq = 1
    while True:
        p = {ch: p[ch] + mm3(p[ch], xp[ch]) for ch in chains}
        n_sq *= 2
        if n_sq * 2 >= CHUNK:
            break
        xp = {ch: mm3(xp[ch], xp[ch]) for ch in chains}
    sol = {(hd, sb): mm(p[hd, sb], rhs[hd][rows_of(sb)]) for hd, sb in chains}

    def chunk_of(c):
        per = SB // CHUNK
        return c // per, slice((c % per) * CHUNK, (c % per + 1) * CHUNK)

    s = [s_ref[hd] for hd in heads]
    o_parts = [[] for _ in heads]
    for c in range(n_chunks):
        rs = slice(c * CHUNK, (c + 1) * CHUNK)
        sb, r = chunk_of(c)
        gl = [gc_c[hd][(c + 1) * CHUNK - 1:(c + 1) * CHUNK, :] for hd in heads]
        k_dec = [kh[hd][rs] * jnp.exp(gl[hd] - gc_c[hd][rs]) for hd in heads]
        v_new = [sol[hd, sb][r, :DN_DV] - mm(sol[hd, sb][r, DN_DV:], s[hd]) for hd in heads]
        for hd in heads:
            o_parts[hd].append(mm(q_dec[hd][rs], s[hd]) + mm(qk_sb[hd, sb][r, r], v_new[hd]))
        s = [s[hd] * jnp.exp(gl[hd]) + mm(k_dec[hd].T, v_new[hd]) for hd in heads]
    for hd in heads:
        s_ref[hd] = s[hd]
        o = jnp.concatenate(o_parts[hd], axis=0)
        o = o * lax.rsqrt(jnp.mean(o * o, axis=-1, keepdims=True) + EPS) * ng_ref[...]
        o_ref[0, :, cols[hd]] = (o * _silu(z_ref[0, :, cols[hd]])).astype(BF16)


def _deltanet(q, k, v, z, misc, ng, R):
    B, T, _ = q.shape

    def rows(w):
        return pl.BlockSpec((1, R, w), lambda b, i: (b, i, 0))

    return pl.pallas_call(
        _deltanet_kernel,
        grid=(B, T // R),
        in_specs=[rows(DN_QK), rows(DN_QK), rows(DN_V), rows(DN_V), rows(MISC_W),
                  pl.BlockSpec((1, DN_DV), lambda b, i: (0, 0))],
        out_specs=rows(DN_V),
        out_shape=jax.ShapeDtypeStruct((B, T, DN_V), BF16),
        scratch_shapes=[pltpu.VMEM((DN_HEADS, DN_DK, DN_DV), F32)],
        compiler_params=pltpu.CompilerParams(dimension_semantics=("arbitrary", "arbitrary"),
                                             vmem_limit_bytes=VMEM_LIMIT),
        name="deltanet",
    )(q, k, v, z, misc, ng)


def _bit_transpose32(words):
    w = list(words)
    j = 16
    m = 0x0000FFFF
    while j:
        k = 0
        m_i32 = jnp.int32(m - (1 << 32) if m >= (1 << 31) else m)
        while k < 32:
            t = (w[k] ^ lax.shift_right_logical(w[k + j], jnp.full_like(w[k], j))) & m_i32
            w[k] = w[k] ^ t
            w[k + j] = w[k + j] ^ jnp.left_shift(t, j)
            k = (k + j + 1) & ~j
        j >>= 1
        m = (m ^ (m << j)) & 0xFFFFFFFF
    return w


def _dsa_kernel(qix_ref, misc_ref, kix_ref, qlat_ref, ckv_ref, o_ref, keys_ref, planes_ref, bias_ref,
                mx_ref, l_ref, acc_ref, kvmax_ref, qprev_ref, *, topk, pos_bits, n_cg_max):
    g = pl.program_id(1)
    n_blocks = pl.num_programs(1) - 1
    i = g
    QB = QBLOCK
    KT = KEY_TILE
    has_front = g < n_blocks
    has_back = g >= 1
    n_kt = jnp.where(has_front, (i * QB + QB + KT - 1) // KT, 0)
    n_kt_back = (g * QB + KT - 1) // KT

    rowi = lax.broadcasted_iota(I32, (QB, KT), 0)
    coli = lax.broadcasted_iota(I32, (QB, KT), 1)
    limit = i * QB + (rowi // CHUNK + 1) * CHUNK

    misc = misc_ref[0]
    qix = qix_ref[0].reshape(IDX_HEADS * QB, IDX_DIM)
    q_st = qlat_ref[0].reshape(SA_HEADS * QB, KV_RANK)

    def score_body(masked, kt, carry):
        k0 = pl.multiple_of(kt * KT, KT)
        kx = kix_ref[0, pl.ds(k0, KT), :]
        rel = jnp.maximum(_nt_dot(qix, kx), 0.0)
        sc = jnp.zeros((QB, KT), F32)
        for hd in range(IDX_HEADS):
            sc = sc + misc[:, M_WIX + hd:M_WIX + hd + 1] * rel[hd * QB:(hd + 1) * QB]
        sc = jnp.where(sc == 0.0, 0.0, sc)
        bits = pltpu.bitcast(sc, I32)
        key = jnp.where(bits < 0, bits ^ 0x7FFFFFFF, bits)
        keys_ref[:, pl.ds(k0, KT)] = jnp.where(k0 + coli < limit, key, INT_MIN) if masked else key
        return carry

    def logit_chunks(k0):
        kv = ckv_ref[0, pl.ds(k0, KT), :]
        s = _nt_dot(qprev_ref[...], kv)
        bias = bias_ref[:, pl.ds(k0, KT)]
        chunks = []
        for j in range(KT // 128):
            bj = bias[:, j * 128:(j + 1) * 128]
            chunks.append(s[:, j * 128:(j + 1) * 128] + jnp.concatenate([bj] * SA_HEADS, axis=0))
        return kv, chunks

    def pv_body(kt, carry):
        kv, chunks = logit_chunks(pl.multiple_of(kt * KT, KT))
        shift = mx_ref[...]
        ps = [jnp.exp2(c - shift) for c in chunks]
        l_ref[...] = functools.reduce(jnp.add, ps, l_ref[...])
        p = jnp.concatenate([pj.astype(BF16) for pj in ps], axis=1)
        acc_ref[...] += _dot(p, kv)
        return carry

    def clear_sums():
        l_ref[...] = jnp.zeros(l_ref.shape, F32)
        acc_ref[...] = jnp.zeros(acc_ref.shape, F32)

    def write_back_block():
        l_row = jnp.sum(l_ref[...], axis=-1, keepdims=True)
        o_ref[0] = (acc_ref[...] / l_row).astype(BF16).reshape(SA_HEADS, QB, KV_RANK)
        return l_row

    pl.when(has_back)(clear_sums)

    n_open = jnp.where(has_front, (i * QB + CHUNK) // KT, 0)
    n_fused = jnp.minimum(n_open, n_kt_back)

    def fused_body(masked, kt, carry):
        pv_body(kt, carry)
        return score_body(masked, kt, carry)

    n_both = jnp.maximum(n_fused, jnp.minimum(n_kt_back, n_kt))
    lax.fori_loop(0, n_fused, functools.partial(fused_body, False), 0)
    lax.fori_loop(n_fused, n_both, functools.partial(fused_body, True), 0)
    lax.fori_loop(n_both, n_kt_back, pv_body, 0)
    lax.fori_loop(n_fused, n_open, functools.partial(score_body, False), 0)
    lax.fori_loop(jnp.maximum(n_open, n_both), n_kt, functools.partial(score_body, True), 0)

    @pl.when(has_back)
    def _():
        l_fast = write_back_block()

        @pl.when(jnp.min(l_fast) < SOFTMAX_TINY)
        def _():
            mx_ref[...] = jnp.full(mx_ref.shape, NEG_BIG, F32)

            def max_body(kt, carry):
                _, chunks = logit_chunks(pl.multiple_of(kt * KT, KT))
                mx_ref[...] = functools.reduce(jnp.maximum, chunks, mx_ref[...])
                return carry

            lax.fori_loop(0, n_kt_back, max_body, 0)
            mx_ref[...] = jnp.broadcast_to(jnp.max(mx_ref[...], axis=-1, keepdims=True), mx_ref.shape)
            clear_sums()
            lax.fori_loop(0, n_kt_back, pv_body, 0)
            write_back_block()

    n_cg = (n_kt * KT + PLANE_COLS - 1) // PLANE_COLS

    def fill_body(kt, carry):
        keys_ref[:, pl.ds(pl.multiple_of(kt * KT, KT), KT)] = jnp.full((QB, KT), INT_MIN, I32)
        return carry

    lax.fori_loop(n_kt, n_cg * (PLANE_COLS // KT), fill_body, 0)

    @pl.when(i == 0)
    def _():
        planes_ref[...] = jnp.zeros(planes_ref.shape, I32)

    def plane_body(step, carry):
        c = step // (QB // PLANE_ROWS)
        r0 = pl.multiple_of((step % (QB // PLANE_ROWS)) * PLANE_ROWS, PLANE_ROWS)
        words = [keys_ref[pl.ds(r0, PLANE_ROWS), pl.ds(pl.multiple_of(c * PLANE_COLS + j * 128, 128), 128)]
                 for j in range(32)]
        for b, plane in enumerate(_bit_transpose32(words)):
            planes_ref[c, b, pl.ds(r0, PLANE_ROWS), :] = ~plane if b == 0 else plane
        return carry

    lax.fori_loop(0, n_cg * (QB // PLANE_ROWS), plane_body, 0)

    ones_mat = jnp.ones((128, 128), BF16)

    def lane_count(words):
        pc = functools.reduce(jnp.add, [lax.population_count(x) for x in words])
        return _dot(pc.astype(F32).astype(BF16), ones_mat).astype(I32)

    def sel_body(groups, step, carry):
        cand, n_gt, tau_u = carry
        hi = [cand[c] & planes_ref[c, 2 * step] for c in groups]
        lo = [cand[c] ^ hi[c] for c in groups]
        d3 = [hi[c] & planes_ref[c, 2 * step + 1] for c in groups]
        d2 = [hi[c] ^ d3[c] for c in groups]
        d1 = [lo[c] & planes_ref[c, 2 * step + 1] for c in groups]
        d0 = [lo[c] ^ d1[c] for c in groups]
        a3 = n_gt + lane_count(d3)
        a2 = a3 + lane_count(d2)
        a1 = a2 + lane_count(d1)
        is3 = a3 >= topk
        is2 = a2 >= topk
        is1 = a1 >= topk
        cand = tuple(jnp.where(is3, d3[c], jnp.where(is2, d2[c], jnp.where(is1, d1[c], d0[c]))) for c in groups)
        n_gt = jnp.where(is3, n_gt, jnp.where(is2, a3, jnp.where(is1, a2, a1)))
        digit = jnp.where(is3, 3, jnp.where(is2, 2, jnp.where(is1, 1, 0)))
        return cand, n_gt, tau_u | jnp.left_shift(digit, 30 - 2 * step)

    def radix_select(n_groups):
        groups = range(n_groups)
        start = (tuple(jnp.full((QB, 128), -1, I32) for _ in groups), jnp.zeros((QB, 128), I32),
                 jnp.zeros((QB, 128), I32))
        cand_, n_gt_, tau_ = lax.fori_loop(0, 16, functools.partial(sel_body, groups), start)
        rest = tuple(jnp.zeros((QB, 128), I32) for _ in range(n_cg_max - n_groups))
        return cand_ + rest, n_gt_, tau_

    cand, n_gt, tau_u = lax.switch(n_cg - 1, [functools.partial(radix_select, g + 1) for g in range(n_cg_max)])
    tau = tau_u ^ INT_MIN
    sentinel = tau == INT_MIN
    cand = tuple(jnp.where(sentinel, 0, cand[c]) for c in range(n_cg_max))
    need = topk - n_gt
    any_tie = jnp.max(jnp.where(lane_count(cand) > need, 1, 0)) > 0

    def fast_bias():
        floor = jnp.where(sentinel, INT_MIN + 1, tau)

        def body(kt, carry):
            for j in range(KT // 128):
                cols = pl.ds(pl.multiple_of(kt * KT + j * 128, 128), 128)
                bias_ref[:, cols] = jnp.where(keys_ref[:, cols] >= floor, 0.0, NEG_BIG)
            return carry

        lax.fori_loop(0, n_kt, body, 0)

    def tie_bias():
        lane = lax.broadcasted_iota(I32, (QB, 128), 1)

        def pos_mask(p, c):
            cg = lax.shift_right_logical(p, jnp.full_like(p, PLANE_SHIFT))
            j0 = lax.shift_right_logical(p, jnp.full_like(p, 7)) & 31
            below = ~lax.shift_right_logical(jnp.full_like(p, -1), j0)
            bit = lax.shift_right_logical(jnp.full_like(p, INT_MIN), j0)
            word = below | jnp.where(lane < (p & 127), bit, 0)
            return jnp.where(cg > c, -1, jnp.where(cg == c, word, 0))

        def pos_body(b, q):
            cq = q + jnp.left_shift(jnp.int32(1), pos_bits - 1 - b)
            cnt = lane_count([cand[c] & pos_mask(cq, c) for c in range(n_cg_max)])
            return jnp.where(cnt < need, cq, q)

        pstar = lax.fori_loop(0, pos_bits, pos_body, jnp.zeros((QB, 128), I32)) + 1
        pstar = jnp.where(sentinel, 0, pstar)

        def body(kt, carry):
            for j in range(KT // 128):
                c0 = pl.multiple_of(kt * KT + j * 128, 128)
                kk = keys_ref[:, pl.ds(c0, 128)]
                tie = jnp.where(c0 + lane < pstar, 0.0, NEG_BIG)
                bias_ref[:, pl.ds(c0, 128)] = jnp.where(kk > tau, 0.0, jnp.where(kk == tau, tie, NEG_BIG))
            return carry

        lax.fori_loop(0, n_kt, body, 0)

    lax.cond(any_tie, tie_bias, fast_bias)

    @pl.when(i == 0)
    def _():
        def norm_body(r, best):
            x = ckv_ref[0, pl.ds(pl.multiple_of(r * KT, KT), KT), :].astype(F32)
            return jnp.maximum(best, jnp.max(jnp.sum(x * x, axis=1, keepdims=True), axis=0, keepdims=True))

        n_rows = ckv_ref.shape[1]
        kv_sq = lax.fori_loop(0, n_rows // KT, norm_body, jnp.zeros((1, 1), F32))
        kvmax_ref[...] = jnp.broadcast_to(jnp.sqrt(kv_sq), kvmax_ref.shape)

    qf = q_st.astype(F32)
    q_norm = jnp.sqrt(jnp.sum(qf * qf, axis=1, keepdims=True))
    mx_ref[...] = jnp.broadcast_to(q_norm, mx_ref.shape) * kvmax_ref[0:1, :] * 1.001 + 1e-3
    qprev_ref[...] = q_st


def _dsa(qix, misc, kix, qlat, ckv, topk):
    B, T, _ = kix.shape
    n_cg_max = -(-T // PLANE_COLS)
    t_pad = n_cg_max * PLANE_COLS
    pos_bits = (t_pad - 1).bit_length()

    n_blocks = T // QBLOCK

    def rows(w):
        return pl.BlockSpec((1, QBLOCK, w), lambda b, g: (b, jnp.minimum(g, n_blocks - 1), 0))

    def head_rows(h, w):
        return pl.BlockSpec((1, h, QBLOCK, w), lambda b, g: (b, 0, jnp.minimum(g, n_blocks - 1), 0))

    def per_b(w):
        return pl.BlockSpec((1, T, w), lambda b, g: (b, 0, 0))

    out_rows = pl.BlockSpec((1, SA_HEADS, QBLOCK, KV_RANK), lambda b, g: (b, 0, jnp.maximum(g - 1, 0), 0))

    return pl.pallas_call(
        functools.partial(_dsa_kernel, topk=topk, pos_bits=pos_bits, n_cg_max=n_cg_max),
        grid=(B, n_blocks + 1),
        in_specs=[head_rows(IDX_HEADS, IDX_DIM), rows(MISC_W), per_b(IDX_DIM), head_rows(SA_HEADS, KV_RANK),
                  per_b(KV_RANK)],
        out_specs=out_rows,
        out_shape=jax.ShapeDtypeStruct((B, SA_HEADS, T, KV_RANK), BF16),
        scratch_shapes=[pltpu.VMEM((QBLOCK, t_pad), I32),
                        pltpu.VMEM((n_cg_max, 32, QBLOCK, 128), I32),
                        pltpu.VMEM((QBLOCK, t_pad), F32),
                        pltpu.VMEM((SA_HEADS * QBLOCK, 128), F32),
                        pltpu.VMEM((SA_HEADS * QBLOCK, 128), F32),
                        pltpu.VMEM((SA_HEADS * QBLOCK, KV_RANK), F32),
                        pltpu.VMEM((8, 128), F32),
                        pltpu.VMEM((SA_HEADS * QBLOCK, KV_RANK), BF16)],
        compiler_params=pltpu.CompilerParams(dimension_semantics=("arbitrary", "arbitrary"),
                                             vmem_limit_bytes=VMEM_LIMIT),
        name="dsa",
    )(qix, misc, kix, qlat, ckv)


def _first_max(v, idx, axis):
    m = jnp.max(v, axis=axis, keepdims=True)
    big = jnp.int32(2 ** 30)
    first = jnp.min(jnp.where(v == m, idx, big), axis=axis, keepdims=True)
    return m, idx == first


def _outproj_kernel(x_ref, odn_ref, olat_ref, uv_ref, wo_ref, gt_ref, sc_ref, sh_ref, g2_ref, rwt_ref,
                    rb_ref, lstrict_ref, ustrict_ref, x1_ref, h2_ref, posrow_ref, poscol_ref, gatecol_ref,
                    cpad_ref):
    tm = x_ref.shape[1]
    parts = [odn_ref[0]]
    for hd in range(SA_HEADS):
        parts.append(_dot(olat_ref[0, hd], uv_ref[hd]).astype(BF16))
    mix = jnp.concatenate(parts, axis=-1)
    x1 = x_ref[0] + gt_ref[0] * _dot(mix, wo_ref[...])
    x1_ref[0] = x1
    h2 = x1 * lax.rsqrt(jnp.mean(x1 * x1, axis=-1, keepdims=True) + EPS) * g2_ref[...]
    h2 = h2 * (1.0 + sc_ref[0]) + sh_ref[0]
    h2_ref[0] = h2.astype(BF16)

    per_g = N_EXPERTS // N_GROUPS
    s = jax.nn.sigmoid(_nt_dot(rwt_ref[...], h2, HIGHEST))
    choice = s + rb_ref[...]
    ig = lax.broadcasted_iota(I32, (per_g, tm), 0)
    gscore = []
    for gidx in range(N_GROUPS):
        cg = choice[gidx * per_g:(gidx + 1) * per_g]
        m1, hot1 = _first_max(cg, ig, 0)
        gscore.append(m1 + jnp.max(jnp.where(hot1, -jnp.inf, cg), axis=0, keepdims=True))
    gsel = [jnp.zeros((1, tm), jnp.bool_) for _ in range(N_GROUPS)]
    for _ in range(TOPK_GROUPS):
        best = functools.reduce(jnp.maximum, gscore)
        found = jnp.zeros((1, tm), jnp.bool_)
        for gidx in range(N_GROUPS):
            hot = (gscore[gidx] == best) & jnp.logical_not(found)
            found = found | hot
            gsel[gidx] = gsel[gidx] | hot
            gscore[gidx] = jnp.where(hot, -jnp.inf, gscore[gidx])
    masked = jnp.concatenate(
        [jnp.where(gsel[gidx], choice[gidx * per_g:(gidx + 1) * per_g], -jnp.inf) for gidx in range(N_GROUPS)],
        axis=0)
    ei = lax.broadcasted_iota(I32, masked.shape, 0)
    gate = jnp.zeros(masked.shape, F32)
    hots = []
    for _ in range(TOP_K):
        _, hot = _first_max(masked, ei, 0)
        hots.append(hot)
        gate = jnp.where(hot, s, gate)
        masked = jnp.where(hot, -jnp.inf, masked)
    gate = gate / jnp.sum(gate, axis=0, keepdims=True) * ROUTED_SCALE

    picked = jnp.where(functools.reduce(jnp.logical_or, hots), 1.0, 0.0)
    cnt = jnp.sum(picked, axis=1, keepdims=True)
    cpad = jnp.floor((cnt + (RUN_ALIGN - 1)) * (1.0 / RUN_ALIGN)) * RUN_ALIGN
    cpad_b = jnp.broadcast_to(cpad, (N_EXPERTS, GATE_W))
    lbase = _dot(lstrict_ref[...], cpad_b, HIGHEST)[:, :1]
    rank = _dot(picked.astype(BF16), ustrict_ref[...])
    pos = lbase + rank
    ri = lax.broadcasted_iota(I32, (GATE_W, tm), 0)
    pos_rows = jnp.zeros((GATE_W, tm), F32)
    gate_rows = jnp.zeros((GATE_W, tm), F32)
    for k, hot in enumerate(hots):
        pos_rows = jnp.where(ri == k, jnp.sum(jnp.where(hot, pos, 0.0), axis=0, keepdims=True), pos_rows)
        gate_rows = jnp.where(ri == k, jnp.sum(jnp.where(hot, gate, 0.0), axis=0, keepdims=True), gate_rows)
    posrow_ref[0, 0] = pos_rows[:TOP_K].astype(I32)
    poscol_ref[0] = pos_rows.T.astype(I32)
    gatecol_ref[0] = gate_rows.T
    cpad_ref[0, 0] = cpad_b.astype(I32)


def _outproj(x, odn, olat, uv, wo, gt1, sc2, sh2, g2, rwt, rb, tm):
    B, T, D = x.shape
    nt = T // tm
    ex = jnp.arange(N_EXPERTS)
    lstrict = (ex[:, None] > ex[None, :]).astype(F32)
    tok = jnp.arange(tm)
    ustrict = (tok[:, None] < tok[None, :]).astype(BF16)

    def full(a):
        nd = a.ndim
        return pl.BlockSpec(a.shape, lambda b, i, _n=nd: (0,) * _n)

    def rows(w):
        return pl.BlockSpec((1, tm, w), lambda b, i: (b, i, 0))

    def per_tile(h, w):
        return pl.BlockSpec((1, 1, h, w), lambda b, i: (b, i, 0, 0))

    per_b = pl.BlockSpec((1, 1, D), lambda b, i: (b, 0, 0))
    return pl.pallas_call(
        _outproj_kernel,
        grid=(B, nt),
        in_specs=[rows(D), rows(DN_V),
                  pl.BlockSpec((1, SA_HEADS, tm, KV_RANK), lambda b, i: (b, 0, i, 0)),
                  full(uv), full(wo), per_b, per_b, per_b,
                  full(g2), full(rwt), full(rb), full(lstrict), full(ustrict)],
        out_specs=[rows(D), rows(D), per_tile(TOP_K, tm), rows(GATE_W), rows(GATE_W),
                   per_tile(N_EXPERTS, GATE_W)],
        out_shape=[jax.ShapeDtypeStruct((B, T, D), F32), jax.ShapeDtypeStruct((B, T, D), BF16),
                   jax.ShapeDtypeStruct((B, nt, TOP_K, tm), I32),
                   jax.ShapeDtypeStruct((B, T, GATE_W), I32),
                   jax.ShapeDtypeStruct((B, T, GATE_W), F32),
                   jax.ShapeDtypeStruct((B, nt, N_EXPERTS, GATE_W), I32)],
        compiler_params=pltpu.CompilerParams(dimension_semantics=("arbitrary", "arbitrary"),
                                             vmem_limit_bytes=VMEM_LIMIT),
        name="outproj",
    )(x, odn, olat, uv, wo, gt1, sc2, sh2, g2, rwt, rb, lstrict, ustrict)


def _piece_sizes(max_rows):
    sizes = []
    z = RUN_ALIGN
    while z <= max_rows:
        sizes.append(z)
        z *= 2
    return sizes[::-1]


def _for_run_pieces(length, max_rows, fn, rare_from=None):
    def pieces(sizes):
        for z in sizes:
            start = length & ~(2 * z - 1)

            @pl.when((length & z) != 0)
            def _(start=start, z=z):
                fn(start, z)

    sizes = _piece_sizes(max_rows)
    rare = [z for z in sizes if rare_from is not None and z >= rare_from]
    if rare:
        pl.when(length >= rare_from)(lambda: pieces(rare))
    pieces([z for z in sizes if z not in rare])


def _plan_kernel(cp_ref, off_ref, lb_ref, foff_ref, flen_ref, blk_ref, nused_ref):
    cp = cp_ref[...].astype(F32)
    n, ne = cp.shape
    ei = lax.broadcasted_iota(I32, (ne, ne), 0)
    ej = lax.broadcasted_iota(I32, (ne, ne), 1)
    si = lax.broadcasted_iota(I32, (n, n), 0)
    sj = lax.broadcasted_iota(I32, (n, n), 1)
    lb = _dot(cp, (ei < ej).astype(F32), HIGHEST)
    earlier_tiles = _dot((sj < si).astype(F32), cp, HIGHEST)
    rows_e = jnp.sum(cp, axis=0, keepdims=True)
    region = jnp.floor((rows_e + (ROW_BLOCK - 1)) * (1.0 / ROW_BLOCK)) * ROW_BLOCK
    region_b = jnp.broadcast_to(region, (ne, ne))
    rend_row = _dot(region_b, (ei <= ej).astype(F32), HIGHEST)[:1]
    rend_col = jnp.sum(jnp.where(ej <= ei, region_b, 0.0), axis=1, keepdims=True)
    base = rend_row - region
    total = jnp.max(rend_row, axis=1, keepdims=True)
    off_ref[...] = (base + earlier_tiles).astype(I32)
    lb_ref[...] = lb.astype(I32)
    lane = lax.broadcasted_iota(I32, (1, GATE_W), 1)
    pad = jnp.zeros((1, GATE_W - ne), F32)
    foff_ref[...] = jnp.where(lane == ne, total, jnp.concatenate([base + rows_e, pad], axis=1)).astype(I32)
    flen_ref[...] = jnp.concatenate([region - rows_e, pad], axis=1).astype(I32)
    n_used = total * (1.0 / ROW_BLOCK)
    nused_ref[...] = jnp.broadcast_to(n_used, nused_ref.shape).astype(I32)
    bi = lax.broadcasted_iota(I32, (ne, blk_ref.shape[1]), 1).astype(F32)
    ended = jnp.where(rend_col * (1.0 / ROW_BLOCK) <= jnp.minimum(bi, n_used - 1.0), 1.0, 0.0)
    blk_ref[...] = jnp.minimum(jnp.sum(ended, axis=0, keepdims=True), ne - 1.0).astype(I32)


def _plan(cp, n_blocks):
    n, ne = cp.shape
    nb_pad = -(-n_blocks // 128) * 128
    return pl.pallas_call(
        _plan_kernel,
        out_shape=[jax.ShapeDtypeStruct((n, ne), I32), jax.ShapeDtypeStruct((n, ne), I32),
                   jax.ShapeDtypeStruct((1, GATE_W), I32), jax.ShapeDtypeStruct((1, GATE_W), I32),
                   jax.ShapeDtypeStruct((1, nb_pad), I32), jax.ShapeDtypeStruct((1, GATE_W), I32)],
        name="moe_plan",
    )(cp)


def _dispatch_kernel(off_ref, cp_ref, lb_ref, foff_ref, flen_ref, h_ref, posrow_ref, xs_hbm, buf, zbuf, sem,
                     zsem, *, n_steps, tile):
    s = pl.program_id(0)
    slot = s % 2
    jmax = buf.shape[1]

    def run_copies(step, slot_, act):
        def body(e, carry):
            idx = step * N_EXPERTS + e
            lb = lb_ref[idx]
            of = off_ref[idx]

            def piece(start, z):
                act(pltpu.make_async_copy(
                    buf.at[slot_, pl.ds(pl.multiple_of(lb + start, RUN_ALIGN), z)],
                    xs_hbm.at[pl.ds(pl.multiple_of(of + start, RUN_ALIGN), z)], sem.at[slot_]))

            _for_run_pieces(cp_ref[idx], tile, piece, rare_from=tile // 4)
            return carry

        lax.fori_loop(0, N_EXPERTS, body, 0)

    def wait_runs(step, slot_):
        last_ = step * N_EXPERTS + N_EXPERTS - 1
        _for_run_pieces(lb_ref[last_] + cp_ref[last_], jmax, lambda start, z: pltpu.make_async_copy(
            buf.at[slot_, pl.ds(0, z)], xs_hbm.at[pl.ds(0, z)], sem.at[slot_]).wait())

    def fill_copies(act):
        def body(e, carry):
            fo = foff_ref[e]

            def piece(start, z):
                act(pltpu.make_async_copy(
                    zbuf.at[pl.ds(0, z)], xs_hbm.at[pl.ds(pl.multiple_of(fo + start, RUN_ALIGN), z)], zsem.at[0]))

            _for_run_pieces(flen_ref[e], ROW_BLOCK // 2, piece)
            return carry

        lax.fori_loop(0, N_EXPERTS, body, 0)

        def tail(r, carry):
            act(pltpu.make_async_copy(
                zbuf, xs_hbm.at[pl.ds(pl.multiple_of(foff_ref[N_EXPERTS] + r * zbuf.shape[0], RUN_ALIGN),
                                      zbuf.shape[0])], zsem.at[0]))
            return carry

        lax.fori_loop(0, (xs_hbm.shape[0] - foff_ref[N_EXPERTS]) // zbuf.shape[0], tail, 0)

    @pl.when(s == 0)
    def _():
        zbuf[...] = jnp.zeros(zbuf.shape, BF16)
        fill_copies(lambda c: c.start())

    @pl.when(s >= 2)
    def _():
        wait_runs(s - 2, slot)

    h = h_ref[...]
    last = s * N_EXPERTS + N_EXPERTS - 1
    jused = lb_ref[last] + cp_ref[last]
    def local_rows(jc):
        ji = (lax.broadcasted_iota(I32, (MOE_CHUNK, tile), 0) + jc * MOE_CHUNK).astype(jnp.int16)
        p = jnp.zeros((MOE_CHUNK, tile), BF16)
        for k in range(TOP_K):
            p = jnp.where(ji == posrow_ref[0, k:k + 1, :].astype(jnp.int16), jnp.ones((), BF16), p)
        return _dot(p, h).astype(BF16)

    n_full = TOP_K * tile // MOE_CHUNK
    rows_full = [local_rows(jc) for jc in range(n_full)]
    for jc in range(n_full):
        buf[slot, jc * MOE_CHUNK:(jc + 1) * MOE_CHUNK, :] = rows_full[jc]
    for jc in range(n_full, jmax // MOE_CHUNK):
        @pl.when(jc * MOE_CHUNK < jused)
        def _(jc=jc):
            buf[slot, jc * MOE_CHUNK:(jc + 1) * MOE_CHUNK, :] = local_rows(jc)

    run_copies(s, slot, lambda c: c.start())

    @pl.when(s == n_steps - 1)
    def _():
        if n_steps >= 2:
            wait_runs(s - 1, 1 - slot)
        wait_runs(s, slot)
        fill_copies(lambda c: c.wait())


def _dispatch(h2, posrow, off, cp, lb, foff, flen, cap, tile, jmax):
    n_tok, D = h2.shape
    n_steps = n_tok // tile
    return pl.pallas_call(
        functools.partial(_dispatch_kernel, n_steps=n_steps, tile=tile),
        grid_spec=pltpu.PrefetchScalarGridSpec(
            num_scalar_prefetch=5,
            grid=(n_steps,),
            in_specs=[pl.BlockSpec((tile, D), lambda s, *_: (s, 0)),
                      pl.BlockSpec((1, TOP_K, tile), lambda s, *_: (s, 0, 0))],
            out_specs=pl.BlockSpec(memory_space=pl.ANY),
            scratch_shapes=[pltpu.VMEM((2, jmax, D), BF16), pltpu.VMEM((ROW_BLOCK // 2, D), BF16),
                            pltpu.SemaphoreType.DMA((2,)), pltpu.SemaphoreType.DMA((1,))]),
        out_shape=jax.ShapeDtypeStruct((cap, D), BF16),
        compiler_params=pltpu.CompilerParams(dimension_semantics=("arbitrary",), vmem_limit_bytes=VMEM_LIMIT),
        name="moe_dispatch",
    )(off, cp, lb, foff, flen, h2, posrow)


def _expert_kernel(blk_e_ref, nused_ref, xs_ref, wg_ref, wu_ref, wd_ref, ys_ref, wg_bf, wu_bf, wd_bf):
    i = pl.program_id(0)
    used = i < nused_ref[0]

    @pl.when(used & ((i == 0) | (blk_e_ref[i] != blk_e_ref[jnp.maximum(i - 1, 0)])))
    def _():
        wg_bf[...] = wg_ref[0].astype(BF16)
        wu_bf[...] = wu_ref[0].astype(BF16)
        wd_bf[...] = wd_ref[0].astype(BF16)

    @pl.when(used)
    def _():
        xb = xs_ref[...]
        a = _silu(_dot(xb, wg_bf[...])) * _dot(xb, wu_bf[...])
        ys_ref[...] = _dot(a.astype(BF16), wd_bf[...]).astype(BF16)

    @pl.when(jnp.logical_not(used))
    def _():
        ys_ref[...] = jnp.zeros(ys_ref.shape, BF16)


def _experts(xs, blk_e, n_used, wg, wu, wd):
    cap, D = xs.shape

    def row_block(i, be, nu):
        return (jnp.minimum(i, nu[0] - 1), 0)

    def out_block(i, be, nu):
        return (i, 0)

    def weight(i, be, nu):
        return (be[i], 0, 0)

    return pl.pallas_call(
        _expert_kernel,
        grid_spec=pltpu.PrefetchScalarGridSpec(
            num_scalar_prefetch=2,
            grid=(cap // ROW_BLOCK,),
            in_specs=[pl.BlockSpec((ROW_BLOCK, D), row_block),
                      pl.BlockSpec((1, D, D_EXPERT), weight), pl.BlockSpec((1, D, D_EXPERT), weight),
                      pl.BlockSpec((1, D_EXPERT, D), weight)],
            out_specs=pl.BlockSpec((ROW_BLOCK, D), out_block),
            scratch_shapes=[pltpu.VMEM((D, D_EXPERT), BF16), pltpu.VMEM((D, D_EXPERT), BF16),
                            pltpu.VMEM((D_EXPERT, D), BF16)]),
        out_shape=jax.ShapeDtypeStruct((cap, D), BF16),
        compiler_params=pltpu.CompilerParams(dimension_semantics=("arbitrary",), vmem_limit_bytes=VMEM_LIMIT),
        name="moe_experts",
    )(blk_e, n_used, xs, wg, wu, wd)


def _combine_kernel(off_ref, cp_ref, lb_ref, ys_hbm, poscol_ref, gatecol_ref, h_ref, sg_ref, su_ref, sd_ref,
                    x1_ref, gt_ref, fg_ref, o_ref, buf, sem, acc_ref, *, n_steps, tile, final_norm):
    s = pl.program_id(0)
    slot = s % 2
    jmax = buf.shape[1]

    def run_copies(step, slot_, act):
        def body(e, carry):
            idx = step * N_EXPERTS + e
            lb = lb_ref[idx]
            of = off_ref[idx]

            def piece(start, z):
                act(pltpu.make_async_copy(
                    ys_hbm.at[pl.ds(pl.multiple_of(of + start, RUN_ALIGN), z)],
                    buf.at[slot_, pl.ds(pl.multiple_of(lb + start, RUN_ALIGN), z)], sem.at[slot_]))

            _for_run_pieces(cp_ref[idx], tile, piece, rare_from=tile // 4)
            return carry

        lax.fori_loop(0, N_EXPERTS, body, 0)

    @pl.when(s == 0)
    def _():
        run_copies(0, 0, lambda c: c.start())

    @pl.when(s + 1 < n_steps)
    def _():
        run_copies(s + 1, 1 - slot, lambda c: c.start())

    hb = h_ref[...]
    shared = (_silu(_dot(hb, sg_ref[...])) * _dot(hb, su_ref[...])).astype(BF16)
    acc_ref[...] = _dot(shared, sd_ref[...])

    last = s * N_EXPERTS + N_EXPERTS - 1
    jused = lb_ref[last] + cp_ref[last]
    _for_run_pieces(jused, jmax, lambda start, z: pltpu.make_async_copy(
        ys_hbm.at[pl.ds(0, z)], buf.at[slot, pl.ds(0, z)], sem.at[slot]).wait())

    def zero_body(r, carry):
        buf[slot, pl.ds(pl.multiple_of(jused + r * RUN_ALIGN, RUN_ALIGN), RUN_ALIGN), :] = jnp.zeros(
            (RUN_ALIGN, buf.shape[2]), BF16)
        return carry

    chunk_end = (jused + MOE_CHUNK - 1) // MOE_CHUNK * MOE_CHUNK
    lax.fori_loop(0, (chunk_end - jused) // RUN_ALIGN, zero_body, 0)

    def gate_rows(jc):
        ji = (lax.broadcasted_iota(I32, (tile, MOE_CHUNK), 1) + jc * MOE_CHUNK).astype(jnp.int16)
        g = jnp.zeros((tile, MOE_CHUNK), BF16)
        for k in range(TOP_K):
            g = jnp.where(ji == poscol_ref[:, k:k + 1].astype(jnp.int16), gatecol_ref[:, k:k + 1].astype(BF16), g)
        return g

    n_full = TOP_K * tile // MOE_CHUNK
    g_full = jnp.concatenate([gate_rows(jc) for jc in range(n_full)], axis=1)
    acc_ref[...] += _dot(g_full, buf[slot, 0:n_full * MOE_CHUNK, :])
    for jc in range(n_full, jmax // MOE_CHUNK):
        @pl.when(jc * MOE_CHUNK < jused)
        def _(jc=jc):
            acc_ref[...] += _dot(gate_rows(jc), buf[slot, jc * MOE_CHUNK:(jc + 1) * MOE_CHUNK, :])

    y = x1_ref[...] + gt_ref[0] * acc_ref[...]
    if final_norm:
        y = y * lax.rsqrt(jnp.mean(y * y, axis=-1, keepdims=True) + EPS) * fg_ref[...]
    o_ref[...] = y


def _combine(ys, poscol, gatecol, h2, sg, su, sd, x1, gt2, fg, off, cp, lb, tile, jmax, tiles_per_batch,
             final_norm):
    n_tok, D = h2.shape
    n_steps = n_tok // tile

    def full(a):
        nd = a.ndim
        return pl.BlockSpec(a.shape, lambda s, *_, _n=nd: (0,) * _n)

    def rows(w):
        return pl.BlockSpec((tile, w), lambda s, *_: (s, 0))

    return pl.pallas_call(
        functools.partial(_combine_kernel, n_steps=n_steps, tile=tile, final_norm=final_norm),
        grid_spec=pltpu.PrefetchScalarGridSpec(
            num_scalar_prefetch=3,
            grid=(n_steps,),
            in_specs=[pl.BlockSpec(memory_space=pl.ANY), rows(GATE_W), rows(GATE_W), rows(D),
                      full(sg), full(su), full(sd), rows(D),
                      pl.BlockSpec((1, 1, D), lambda s, *_: (s // tiles_per_batch, 0, 0)), full(fg)],
            out_specs=rows(D),
            scratch_shapes=[pltpu.VMEM((2, jmax, D), BF16), pltpu.SemaphoreType.DMA((2,)),
                            pltpu.VMEM((tile, D), F32)]),
        out_shape=jax.ShapeDtypeStruct((n_tok, D), F32),
        compiler_params=pltpu.CompilerParams(dimension_semantics=("arbitrary",), vmem_limit_bytes=VMEM_LIMIT),
        name="moe_combine",
    )(off, cp, lb, ys, poscol, gatecol, h2, sg, su, sd, x1, gt2, fg)


def _moe(h2, posrow, poscol, gatecol, cpad, wg, wu, wd, sg, su, sd, x1, gt2, fg, tile, final_norm):
    B, T, D = x1.shape
    n_tok = B * T
    n_tiles = n_tok // tile
    jmax = -(-(TOP_K * tile + N_EXPERTS * (RUN_ALIGN - 1)) // MOE_CHUNK) * MOE_CHUNK
    cap = -(-(TOP_K * n_tok + n_tiles * N_EXPERTS * (RUN_ALIGN - 1) + N_EXPERTS * (ROW_BLOCK - RUN_ALIGN))
            // ROW_BLOCK) * ROW_BLOCK

    cp = cpad[..., 0].reshape(n_tiles, N_EXPERTS)
    off, lb, foff, flen, blk_e, n_used = _plan(cp, cap // ROW_BLOCK)
    flat = lambda a: a.reshape(-1)

    xs = _dispatch(h2.reshape(n_tok, D), posrow.reshape(n_tiles, TOP_K, tile), flat(off), flat(cp), flat(lb),
                   foff[0], flen[0], cap, tile, jmax)
    ys = _experts(xs, blk_e[0], n_used[0], wg, wu, wd)
    out = _combine(ys, poscol.reshape(n_tok, GATE_W), gatecol.reshape(n_tok, GATE_W), h2.reshape(n_tok, D),
                   sg, su, sd, x1.reshape(n_tok, D), gt2, fg, flat(off), flat(cp), flat(lb), tile, jmax,
                   T // tile, final_norm)
    return out.reshape(B, T, D)


def _misc_lanes(vec, start):
    return jnp.zeros((1, MISC_W), F32).at[0, start:start + vec.shape[0]].set(vec.astype(F32))


def kernel(x, c, ada_w, ada_b, norm1_g, w_in, conv_w, a_log, dt_bias, dn_norm_g, kv_norm_g, w_uk, w_uv,
           idx_k_ln_g, idx_k_ln_b, w_out, norm2_g, router_w, router_b, exp_w_gate, exp_w_up, exp_w_down,
           sh_w_gate, sh_w_up, sh_w_down, final_g):
    B, T, D = x.shape
    depth = ada_w.shape[0]
    topk = min(IDX_TOPK_MAX, T // 4)
    tm = min(512, T)
    r_dn = min(256, T)

    cond_in = jnp.zeros((8, D), F32).at[:B].set(c)
    pos = jnp.arange(tm)
    tri = ((pos[:, None] // CHUNK == pos[None, :] // CHUNK) & (pos[:, None] >= pos[None, :])).astype(F32)

    for l in range(depth):
        mod = _ada(cond_in, ada_w[l], ada_b[l][None, :])[:B]
        sh1, sc1, gt1, sh2, sc2, gt2 = [m[:, None, :] for m in jnp.split(mod, 6, axis=-1)]

        offs = [0]
        for s in (DN_QK, DN_QK, DN_V, DN_V, DN_HEADS, DN_HEADS, SA_Q, KV_RANK, IDX_Q, IDX_DIM, IDX_HEADS):
            offs.append(offs[-1] + s)
        w = w_in[l]
        wc = w[:, offs[0]:offs[3]].astype(BF16)
        wz = w[:, offs[3]:offs[4]].astype(BF16)
        wq = w[:, offs[6]:offs[7]].astype(BF16)
        wkv = w[:, offs[7]:offs[8]].astype(BF16)
        wqi = w[:, offs[8]:offs[9]].astype(BF16)
        wm = jnp.concatenate([w[:, offs[9]:offs[10]], w[:, offs[4]:offs[5]], w[:, offs[5]:offs[6]],
                              w[:, offs[10]:offs[11]],
                              jnp.zeros((D, MISC_W - IDX_DIM - 2 * DN_HEADS - IDX_HEADS), F32)],
                             axis=1).astype(BF16)
        ukt = jnp.swapaxes(w_uk[l], 1, 2).astype(BF16)

        q, k, v, z, qlat, ckv, qix, kix, misc = _inproj(
            x, sc1, sh1, norm1_g[l][None, :], wc, wz, wq, wkv, wqi, wm, conv_w[l], ukt,
            kv_norm_g[l][None, :], _misc_lanes(idx_k_ln_g[l], M_KIX), _misc_lanes(idx_k_ln_b[l], M_KIX),
            _misc_lanes(a_log[l], M_A), _misc_lanes(dt_bias[l], M_A), tri, tm)

        odn = _deltanet(q, k, v, z, misc, dn_norm_g[l][None, :], r_dn)
        olat = _dsa(qix, misc, kix, qlat, ckv, topk)

        x1, h2, posrow, poscol, gatecol, cpad = _outproj(
            x, odn, olat, w_uv[l].astype(BF16), w_out[l].astype(BF16), gt1, sc2, sh2,
            norm2_g[l][None, :], router_w[l].T, router_b[l][:, None], tm)

        x = _moe(h2, posrow, poscol, gatecol, cpad, exp_w_gate[l], exp_w_up[l], exp_w_down[l],
                 sh_w_gate[l].astype(BF16), sh_w_up[l].astype(BF16),
                 sh_w_down[l].astype(BF16), x1, gt2, final_g[None, :], tm, l == depth - 1)
    return x
```

```python
import functools

import jax
import jax.numpy as jnp
from jax import lax
from jax.experimental import pallas as pl
from jax.experimental.pallas import tpu as pltpu

F32 = jnp.float32
BF16 = jnp.bfloat16
I32 = jnp.int32
HIGHEST = lax.Precision.HIGHEST

EPS = 1e-6
CHUNK = 64
DN_HEADS = 4
DN_DK = 128
DN_DV = 128
CONV_K = 4
SA_HEADS = 4
SA_DQK = 128
SA_DV = 128
KV_RANK = 256
IDX_HEADS = 4
IDX_DIM = 64
IDX_TOPK_MAX = 256
SM_SCALE = SA_DQK ** -0.5
LOG2E = 1.4426950408889634
IDX_W_SCALE = (IDX_HEADS * IDX_DIM) ** -0.5
N_EXPERTS = 64
TOP_K = 8
N_GROUPS = 8
TOPK_GROUPS = 4
D_EXPERT = 256
ROUTED_SCALE = 2.5
GATE_W = 128
RUN_ALIGN = 16
ROW_BLOCK = 1024
MOE_CHUNK = 512

DN_QK = DN_HEADS * DN_DK
DN_V = DN_HEADS * DN_DV
CONV_DIM = 2 * DN_QK + DN_V
SA_Q = SA_HEADS * SA_DQK
IDX_Q = IDX_HEADS * IDX_DIM

MISC_W = 128
M_KIX = 0
M_BETA = IDX_DIM
M_A = M_BETA + DN_HEADS
M_WIX = M_A + DN_HEADS

DN_SUB = 2 * CHUNK
QBLOCK = 256
SOFTMAX_TINY = 2.0 ** -100
KEY_TILE = 1024
PLANE_COLS = 32 * 128
PLANE_SHIFT = 12
PLANE_ROWS = 32
INT_MIN = -2 ** 31
NEG_BIG = -1e30
VMEM_LIMIT = 56 * 1024 * 1024


def _nt_dot(a, b, precision=None):
    return lax.dot_general(a, b, (((1,), (1,)), ((), ())), preferred_element_type=F32,
                           precision=precision)


def _dot(a, b, precision=None):
    return jnp.dot(a, b, preferred_element_type=F32, precision=precision)


def _silu(x):
    return x * jax.nn.sigmoid(x)


def _softplus(x):
    return jnp.maximum(x, 0.0) + jnp.log(1.0 + jnp.exp(-jnp.abs(x)))


def _ada_kernel(c_ref, w_ref, b_ref, o_ref):
    cond = _silu(c_ref[...])
    o_ref[...] = _dot(cond, w_ref[...], HIGHEST) + b_ref[...]


def _ada(c_pad, ada_w, ada_b):
    rows, d = c_pad.shape
    n_out = ada_w.shape[1]
    return pl.pallas_call(
        _ada_kernel,
        grid=(n_out // d,),
        in_specs=[pl.BlockSpec((rows, d), lambda j: (0, 0)),
                  pl.BlockSpec((d, d), lambda j: (0, j)),
                  pl.BlockSpec((1, d), lambda j: (0, j))],
        out_specs=pl.BlockSpec((rows, d), lambda j: (0, j)),
        out_shape=jax.ShapeDtypeStruct((rows, n_out), F32),
        compiler_params=pltpu.CompilerParams(vmem_limit_bytes=VMEM_LIMIT),
        name="ada",
    )(c_pad, ada_w, ada_b)


def _inproj_kernel(x_ref, sc_ref, sh_ref, g1_ref, wc_ref, wz_ref, wq_ref, wkv_ref, wqi_ref, wm_ref,
                   convw_ref, ukt_ref, kvg_ref, lng_ref, lnb_ref, alog_ref, dtb_ref, tri_ref,
                   q_ref, k_ref, v_ref, z_ref, qlat_ref, ckv_ref, qix_ref, kix_ref, misc_ref,
                   conv_buf):
    tm = x_ref.shape[1]
    i = pl.program_id(1)

    x = x_ref[0]
    h = x * lax.rsqrt(jnp.mean(x * x, axis=-1, keepdims=True) + EPS) * g1_ref[...]
    h = h * (1.0 + sc_ref[0]) + sh_ref[0]
    hb = h.astype(BF16)

    @pl.when(i == 0)
    def _():
        conv_buf[0:8, :] = jnp.zeros((8, CONV_DIM), F32)

    conv_buf[8:8 + tm, :] = _dot(hb, wc_ref[...])
    for grp, dst in ((0, q_ref), (1, k_ref), (2, v_ref)):
        cols = slice(grp * DN_QK, (grp + 1) * DN_QK)
        y = jnp.zeros((tm, DN_QK), F32)
        for j in range(CONV_K):
            y = y + convw_ref[j:j + 1, cols] * conv_buf[8 - (CONV_K - 1) + j:8 - (CONV_K - 1) + j + tm, cols]
        y = _silu(y)
        if grp < 2:
            outs = []
            for hd in range(DN_HEADS):
                yh = y[:, hd * DN_DK:(hd + 1) * DN_DK]
                yh = yh * lax.rsqrt(jnp.sum(yh * yh, axis=-1, keepdims=True) + EPS)
                if grp == 0:
                    yh = yh * (DN_DK ** -0.5)
                outs.append(yh)
            y = jnp.concatenate(outs, axis=-1)
        dst[0] = y
    conv_buf[0:8, :] = conv_buf[tm:tm + 8, :]

    z_ref[0] = _dot(hb, wz_ref[...])

    q_sa = _dot(hb, wq_ref[...]).astype(BF16)
    for hd in range(SA_HEADS):
        ql = _dot(q_sa[:, hd * SA_DQK:(hd + 1) * SA_DQK], ukt_ref[hd]) * (SM_SCALE * LOG2E)
        qlat_ref[0, hd] = ql.astype(BF16)

    ckv = _dot(hb, wkv_ref[...])
    ckv = ckv * lax.rsqrt(jnp.mean(ckv * ckv, axis=-1, keepdims=True) + EPS) * kvg_ref[...]
    ckv_ref[0] = ckv.astype(BF16)

    q_ix = _dot(hb, wqi_ref[...]).astype(BF16)
    for hd in range(IDX_HEADS):
        qix_ref[0, hd] = q_ix[:, hd * IDX_DIM:(hd + 1) * IDX_DIM]

    m = _dot(hb, wm_ref[...])
    lane = lax.broadcasted_iota(I32, (tm, MISC_W), 1)
    is_k = lane < IDX_DIM
    mu = jnp.sum(jnp.where(is_k, m, 0.0), axis=-1, keepdims=True) * (1.0 / IDX_DIM)
    kc = jnp.where(is_k, m - mu, 0.0)
    var = jnp.sum(kc * kc, axis=-1, keepdims=True) * (1.0 / IDX_DIM)
    kn = kc * lax.rsqrt(var + EPS) * lng_ref[...] + lnb_ref[...]
    kix_ref[0] = kn[:, :IDX_DIM].astype(BF16)

    beta = jax.nn.sigmoid(m)
    g = -jnp.exp(alog_ref[...]) * _softplus(m + dtb_ref[...])
    is_a = (lane >= M_A) & (lane < M_A + DN_HEADS)
    g = jnp.where(is_a, g, 0.0)
    gc = _dot(tri_ref[...], g, HIGHEST)
    is_b = (lane >= M_BETA) & (lane < M_BETA + DN_HEADS)
    is_w = (lane >= M_WIX) & (lane < M_WIX + IDX_HEADS)
    misc_ref[0] = jnp.where(is_b, beta, jnp.where(is_a, gc, jnp.where(is_w, m * IDX_W_SCALE, 0.0)))


def _inproj(x, sc1, sh1, g1, wc, wz, wq, wkv, wqi, wm, conv_w, ukt, kvg, lng, lnb, alog, dtb, tri, tm):
    B, T, D = x.shape
    nt = T // tm

    def full(a):
        nd = a.ndim
        return pl.BlockSpec(a.shape, lambda b, i, _n=nd: (0,) * _n)

    def rows(w):
        return pl.BlockSpec((1, tm, w), lambda b, i: (b, i, 0))

    per_b = pl.BlockSpec((1, 1, D), lambda b, i: (b, 0, 0))
    def head_rows(h, w):
        return pl.BlockSpec((1, h, tm, w), lambda b, i: (b, 0, i, 0))

    outs = [(None, DN_QK, F32), (None, DN_QK, F32), (None, DN_V, F32), (None, DN_V, F32),
            (SA_HEADS, KV_RANK, BF16), (None, KV_RANK, BF16), (IDX_HEADS, IDX_DIM, BF16),
            (None, IDX_DIM, BF16), (None, MISC_W, F32)]
    return pl.pallas_call(
        _inproj_kernel,
        grid=(B, nt),
        in_specs=[rows(D), per_b, per_b, full(g1), full(wc), full(wz), full(wq), full(wkv), full(wqi),
                  full(wm), full(conv_w), full(ukt), full(kvg), full(lng), full(lnb), full(alog),
                  full(dtb), full(tri)],
        out_specs=[rows(w) if h is None else head_rows(h, w) for h, w, _ in outs],
        out_shape=[jax.ShapeDtypeStruct((B, T, w) if h is None else (B, h, T, w), dt) for h, w, dt in outs],
        scratch_shapes=[pltpu.VMEM((tm + 8, CONV_DIM), F32)],
        compiler_params=pltpu.CompilerParams(dimension_semantics=("arbitrary", "arbitrary"),
                                             vmem_limit_bytes=VMEM_LIMIT),
        name="inproj",
    )(x, sc1, sh1, g1, wc, wz, wq, wkv, wqi, wm, conv_w, ukt, kvg, lng, lnb, alog, dtb, tri)


def _deltanet_kernel(q_ref, k_ref, v_ref, z_ref, misc_ref, ng_ref, o_ref, s_ref):
    R = q_ref.shape[1]
    n_chunks = R // CHUNK

    @pl.when(pl.program_id(1) == 0)
    def _():
        s_ref[...] = jnp.zeros(s_ref.shape, F32)

    misc = misc_ref[0]
    misc_t = misc.T
    SB = min(DN_SUB, R)
    row = lax.broadcasted_iota(I32, (SB, SB), 0)
    col = lax.broadcasted_iota(I32, (SB, SB), 1)
    same = (row // CHUNK) == (col // CHUNK)
    lower = same & (row >= col)
    strict = same & (row > col)
    eye = (row == col).astype(F32)

    def mm(a, b):
        return _dot(a.astype(BF16), b.astype(BF16))

    def mm3(a, b):
        ah = a.astype(BF16)
        bh = b.astype(BF16)
        al = (a - ah.astype(F32)).astype(BF16)
        bl = (b - bh.astype(F32)).astype(BF16)
        return _dot(jnp.concatenate([ah, ah, al], axis=1), jnp.concatenate([bh, bl, bh], axis=0))

    heads = range(DN_HEADS)
    subs = range(R // SB)
    chains = [(hd, sb) for hd in heads for sb in subs]
    cols = [slice(hd * DN_DK, (hd + 1) * DN_DK) for hd in heads]
    qh = [q_ref[0, :, cols[hd]] for hd in heads]
    kh = [k_ref[0, :, cols[hd]] for hd in heads]
    beta = [misc[:, M_BETA + hd:M_BETA + hd + 1] for hd in heads]
    gc_c = [misc[:, M_A + hd:M_A + hd + 1] for hd in heads]
    eg = [jnp.exp(gc_c[hd]) for hd in heads]
    kb = [kh[hd] * beta[hd] for hd in heads]
    rhs = [jnp.concatenate([v_ref[0, :, cols[hd]] * beta[hd], kb[hd] * eg[hd]], axis=-1) for hd in heads]
    q_dec = [qh[hd] * eg[hd] for hd in heads]

    def rows_of(sb):
        return slice(sb * SB, (sb + 1) * SB)

    decay, a, qk_sb = {}, {}, {}
    for hd, sb in chains:
        bs = rows_of(sb)
        gc_r = misc_t[M_A + hd:M_A + hd + 1, bs]
        decay[hd, sb] = jnp.where(lower, jnp.exp(jnp.where(lower, gc_c[hd][bs] - gc_r, 0.0)), 0.0)
    for hd, sb in chains:
        bs = rows_of(sb)
        khb = kh[hd][bs].astype(BF16)
        a[hd, sb] = jnp.where(strict, _nt_dot(kb[hd][bs].astype(BF16), khb) * decay[hd, sb], 0.0)
        qk_sb[hd, sb] = jnp.where(lower, _nt_dot(qh[hd][bs].astype(BF16), khb) * decay[hd, sb], 0.0)
    p = {ch: eye - a[ch] for ch in chains}
    xp = {ch: mm3(a[ch], a[ch]) for ch in chains}
    n_sq = 1
    while True:
        p = {ch: p[ch] + mm3(p[ch], xp[ch]) for ch in chains}
        n_sq *= 2
        if n_sq * 2 >= CHUNK:
            break
        xp = {ch: mm3(xp[ch], xp[ch]) for ch in chains}
    sol = {(hd, sb): mm(p[hd, sb], rhs[hd][rows_of(sb)]) for hd, sb in chains}

    def chunk_of(c):
        per = SB // CHUNK
        return c // per, slice((c % per) * CHUNK, (c % per + 1) * CHUNK)

    s = [s_ref[hd] for hd in heads]
    o_parts = [[] for _ in heads]
    for c in range(n_chunks):
        rs = slice(c * CHUNK, (c + 1) * CHUNK)
        sb, r = chunk_of(c)
        gl = [gc_c[hd][(c + 1) * CHUNK - 1:(c + 1) * CHUNK, :] for hd in heads]
        k_dec = [kh[hd][rs] * jnp.exp(gl[hd] - gc_c[hd][rs]) for hd in heads]
        v_new = [sol[hd, sb][r, :DN_DV] - mm(sol[hd, sb][r, DN_DV:], s[hd]) for hd in heads]
        for hd in heads:
            o_parts[hd].append(mm(q_dec[hd][rs], s[hd]) + mm(qk_sb[hd, sb][r, r], v_new[hd]))
        s = [s[hd] * jnp.exp(gl[hd]) + mm(k_dec[hd].T, v_new[hd]) for hd in heads]
    for hd in heads:
        s_ref[hd] = s[hd]
        o = jnp.concatenate(o_parts[hd], axis=0)
        o = o * lax.rsqrt(jnp.mean(o * o, axis=-1, keepdims=True) + EPS) * ng_ref[...]
        o_ref[0, :, cols[hd]] = (o * _silu(z_ref[0, :, cols[hd]])).astype(BF16)


def _deltanet(q, k, v, z, misc, ng, R):
    B, T, _ = q.shape

    def rows(w):
        return pl.BlockSpec((1, R, w), lambda b, i: (b, i, 0))

    return pl.pallas_call(
        _deltanet_kernel,
        grid=(B, T // R),
        in_specs=[rows(DN_QK), rows(DN_QK), rows(DN_V), rows(DN_V), rows(MISC_W),
                  pl.BlockSpec((1, DN_DV), lambda b, i: (0, 0))],
        out_specs=rows(DN_V),
        out_shape=jax.ShapeDtypeStruct((B, T, DN_V), BF16),
        scratch_shapes=[pltpu.VMEM((DN_HEADS, DN_DK, DN_DV), F32)],
        compiler_params=pltpu.CompilerParams(dimension_semantics=("arbitrary", "arbitrary"),
                                             vmem_limit_bytes=VMEM_LIMIT),
        name="deltanet",
    )(q, k, v, z, misc, ng)


def _bit_transpose32(words):
    w = list(words)
    j = 16
    m = 0x0000FFFF
    while j:
        k = 0
        m_i32 = jnp.int32(m - (1 << 32) if m >= (1 << 31) else m)
        while k < 32:
            t = (w[k] ^ lax.shift_right_logical(w[k + j], jnp.full_like(w[k], j))) & m_i32
            w[k] = w[k] ^ t
            w[k + j] = w[k + j] ^ jnp.left_shift(t, j)
            k = (k + j + 1) & ~j
        j >>= 1
        m = (m ^ (m << j)) & 0xFFFFFFFF
    return w


def _dsa_kernel(qix_ref, misc_ref, kix_ref, qlat_ref, ckv_ref, o_ref, keys_ref, planes_ref, bias_ref,
                mx_ref, l_ref, acc_ref, kvmax_ref, qprev_ref, *, topk, pos_bits, n_cg_max):
    g = pl.program_id(1)
    n_blocks = pl.num_programs(1) - 1
    i = g
    QB = QBLOCK
    KT = KEY_TILE
    has_front = g < n_blocks
    has_back = g >= 1
    n_kt = jnp.where(has_front, (i * QB + QB + KT - 1) // KT, 0)
    n_kt_back = (g * QB + KT - 1) // KT

    rowi = lax.broadcasted_iota(I32, (QB, KT), 0)
    coli = lax.broadcasted_iota(I32, (QB, KT), 1)
    limit = i * QB + (rowi // CHUNK + 1) * CHUNK

    misc = misc_ref[0]
    qix = qix_ref[0].reshape(IDX_HEADS * QB, IDX_DIM)
    q_st = qlat_ref[0].reshape(SA_HEADS * QB, KV_RANK)

    def score_body(masked, kt, carry):
        k0 = pl.multiple_of(kt * KT, KT)
        kx = kix_ref[0, pl.ds(k0, KT), :]
        rel = jnp.maximum(_nt_dot(qix, kx), 0.0)
        sc = jnp.zeros((QB, KT), F32)
        for hd in range(IDX_HEADS):
            sc = sc + misc[:, M_WIX + hd:M_WIX + hd + 1] * rel[hd * QB:(hd + 1) * QB]
        sc = jnp.where(sc == 0.0, 0.0, sc)
        bits = pltpu.bitcast(sc, I32)
        key = jnp.where(bits < 0, bits ^ 0x7FFFFFFF, bits)
        keys_ref[:, pl.ds(k0, KT)] = jnp.where(k0 + coli < limit, key, INT_MIN) if masked else key
        return carry

    def logit_chunks(k0):
        kv = ckv_ref[0, pl.ds(k0, KT), :]
        s = _nt_dot(qprev_ref[...], kv)
        bias = bias_ref[:, pl.ds(k0, KT)]
        chunks = []
        for j in range(KT // 128):
            bj = bias[:, j * 128:(j + 1) * 128]
            chunks.append(s[:, j * 128:(j + 1) * 128] + jnp.concatenate([bj] * SA_HEADS, axis=0))
        return kv, chunks

    def pv_body(kt, carry):
        kv, chunks = logit_chunks(pl.multiple_of(kt * KT, KT))
        shift = mx_ref[...]
        ps = [jnp.exp2(c - shift) for c in chunks]
        l_ref[...] = functools.reduce(jnp.add, ps, l_ref[...])
        p = jnp.concatenate([pj.astype(BF16) for pj in ps], axis=1)
        acc_ref[...] += _dot(p, kv)
        return carry

    def clear_sums():
        l_ref[...] = jnp.zeros(l_ref.shape, F32)
        acc_ref[...] = jnp.zeros(acc_ref.shape, F32)

    def write_back_block():
        l_row = jnp.sum(l_ref[...], axis=-1, keepdims=True)
        o_ref[0] = (acc_ref[...] / l_row).astype(BF16).reshape(SA_HEADS, QB, KV_RANK)
        return l_row

    pl.when(has_back)(clear_sums)

    n_open = jnp.where(has_front, (i * QB + CHUNK) // KT, 0)
    n_fused = jnp.minimum(n_open, n_kt_back)

    def fused_body(masked, kt, carry):
        pv_body(kt, carry)
        return score_body(masked, kt, carry)

    n_both = jnp.maximum(n_fused, jnp.minimum(n_kt_back, n_kt))
    lax.fori_loop(0, n_fused, functools.partial(fused_body, False), 0)
    lax.fori_loop(n_fused, n_both, functools.partial(fused_body, True), 0)
    lax.fori_loop(n_both, n_kt_back, pv_body, 0)
    lax.fori_loop(n_fused, n_open, functools.partial(score_body, False), 0)
    lax.fori_loop(jnp.maximum(n_open, n_both), n_kt, functools.partial(score_body, True), 0)

    @pl.when(has_back)
    def _():
        l_fast = write_back_block()

        @pl.when(jnp.min(l_fast) < SOFTMAX_TINY)
        def _():
            mx_ref[...] = jnp.full(mx_ref.shape, NEG_BIG, F32)

            def max_body(kt, carry):
                _, chunks = logit_chunks(pl.multiple_of(kt * KT, KT))
                mx_ref[...] = functools.reduce(jnp.maximum, chunks, mx_ref[...])
                return carry

            lax.fori_loop(0, n_kt_back, max_body, 0)
            mx_ref[...] = jnp.broadcast_to(jnp.max(mx_ref[...], axis=-1, keepdims=True), mx_ref.shape)
            clear_sums()
            lax.fori_loop(0, n_kt_back, pv_body, 0)
            write_back_block()

    n_cg = (n_kt * KT + PLANE_COLS - 1) // PLANE_COLS

    def fill_body(kt, carry):
        keys_ref[:, pl.ds(pl.multiple_of(kt * KT, KT), KT)] = jnp.full((QB, KT), INT_MIN, I32)
        return carry

    lax.fori_loop(n_kt, n_cg * (PLANE_COLS // KT), fill_body, 0)

    @pl.when(i == 0)
    def _():
        planes_ref[...] = jnp.zeros(planes_ref.shape, I32)

    def plane_body(step, carry):
        c = step // (QB // PLANE_ROWS)
        r0 = pl.multiple_of((step % (QB // PLANE_ROWS)) * PLANE_ROWS, PLANE_ROWS)
        words = [keys_ref[pl.ds(r0, PLANE_ROWS), pl.ds(pl.multiple_of(c * PLANE_COLS + j * 128, 128), 128)]
                 for j in range(32)]
        for b, plane in enumerate(_bit_transpose32(words)):
            planes_ref[c, b, pl.ds(r0, PLANE_ROWS), :] = ~plane if b == 0 else plane
        return carry

    lax.fori_loop(0, n_cg * (QB // PLANE_ROWS), plane_body, 0)

    ones_mat = jnp.ones((128, 128), BF16)

    def lane_count(words):
        pc = functools.reduce(jnp.add, [lax.population_count(x) for x in words])
        return _dot(pc.astype(F32).astype(BF16), ones_mat).astype(I32)

    def sel_body(groups, step, carry):
        cand, n_gt, tau_u = carry
        hi = [cand[c] & planes_ref[c, 2 * step] for c in groups]
        lo = [cand[c] ^ hi[c] for c in groups]
        d3 = [hi[c] & planes_ref[c, 2 * step + 1] for c in groups]
        d2 = [hi[c] ^ d3[c] for c in groups]
        d1 = [lo[c] & planes_ref[c, 2 * step + 1] for c in groups]
        d0 = [lo[c] ^ d1[c] for c in groups]
        a3 = n_gt + lane_count(d3)
        a2 = a3 + lane_count(d2)
        a1 = a2 + lane_count(d1)
        is3 = a3 >= topk
        is2 = a2 >= topk
        is1 = a1 >= topk
        cand = tuple(jnp.where(is3, d3[c], jnp.where(is2, d2[c], jnp.where(is1, d1[c], d0[c]))) for c in groups)
        n_gt = jnp.where(is3, n_gt, jnp.where(is2, a3, jnp.where(is1, a2, a1)))
        digit = jnp.where(is3, 3, jnp.where(is2, 2, jnp.where(is1, 1, 0)))
        return cand, n_gt, tau_u | jnp.left_shift(digit, 30 - 2 * step)

    def radix_select(n_groups):
        groups = range(n_groups)
        start = (tuple(jnp.full((QB, 128), -1, I32) for _ in groups), jnp.zeros((QB, 128), I32),
                 jnp.zeros((QB, 128), I32))
        cand_, n_gt_, tau_ = lax.fori_loop(0, 16, functools.partial(sel_body, groups), start)
        rest = tuple(jnp.zeros((QB, 128), I32) for _ in range(n_cg_max - n_groups))
        return cand_ + rest, n_gt_, tau_

    cand, n_gt, tau_u = lax.switch(n_cg - 1, [functools.partial(radix_select, g + 1) for g in range(n_cg_max)])
    tau = tau_u ^ INT_MIN
    sentinel = tau == INT_MIN
    cand = tuple(jnp.where(sentinel, 0, cand[c]) for c in range(n_cg_max))
    need = topk - n_gt
    any_tie = jnp.max(jnp.where(lane_count(cand) > need, 1, 0)) > 0

    def fast_bias():
        floor = jnp.where(sentinel, INT_MIN + 1, tau)

        def body(kt, carry):
            for j in range(KT // 128):
                cols = pl.ds(pl.multiple_of(kt * KT + j * 128, 128), 128)
                bias_ref[:, cols] = jnp.where(keys_ref[:, cols] >= floor, 0.0, NEG_BIG)
            return carry

        lax.fori_loop(0, n_kt, body, 0)

    def tie_bias():
        lane = lax.broadcasted_iota(I32, (QB, 128), 1)

        def pos_mask(p, c):
            cg = lax.shift_right_logical(p, jnp.full_like(p, PLANE_SHIFT))
            j0 = lax.shift_right_logical(p, jnp.full_like(p, 7)) & 31
            below = ~lax.shift_right_logical(jnp.full_like(p, -1), j0)
            bit = lax.shift_right_logical(jnp.full_like(p, INT_MIN), j0)
            word = below | jnp.where(lane < (p & 127), bit, 0)
            return jnp.where(cg > c, -1, jnp.where(cg == c, word, 0))

        def pos_body(b, q):
            cq = q + jnp.left_shift(jnp.int32(1), pos_bits - 1 - b)
            cnt = lane_count([cand[c] & pos_mask(cq, c) for c in range(n_cg_max)])
            return jnp.where(cnt < need, cq, q)

        pstar = lax.fori_loop(0, pos_bits, pos_body, jnp.zeros((QB, 128), I32)) + 1
        pstar = jnp.where(sentinel, 0, pstar)

        def body(kt, carry):
            for j in range(KT // 128):
                c0 = pl.multiple_of(kt * KT + j * 128, 128)
                kk = keys_ref[:, pl.ds(c0, 128)]
                tie = jnp.where(c0 + lane < pstar, 0.0, NEG_BIG)
                bias_ref[:, pl.ds(c0, 128)] = jnp.where(kk > tau, 0.0, jnp.where(kk == tau, tie, NEG_BIG))
            return carry

        lax.fori_loop(0, n_kt, body, 0)

    lax.cond(any_tie, tie_bias, fast_bias)

    @pl.when(i == 0)
    def _():
        def norm_body(r, best):
            x = ckv_ref[0, pl.ds(pl.multiple_of(r * KT, KT), KT), :].astype(F32)
            return jnp.maximum(best, jnp.max(jnp.sum(x * x, axis=1, keepdims=True), axis=0, keepdims=True))

        n_rows = ckv_ref.shape[1]
        kv_sq = lax.fori_loop(0, n_rows // KT, norm_body, jnp.zeros((1, 1), F32))
        kvmax_ref[...] = jnp.broadcast_to(jnp.sqrt(kv_sq), kvmax_ref.shape)

    qf = q_st.astype(F32)
    q_norm = jnp.sqrt(jnp.sum(qf * qf, axis=1, keepdims=True))
    mx_ref[...] = jnp.broadcast_to(q_norm, mx_ref.shape) * kvmax_ref[0:1, :] * 1.001 + 1e-3
    qprev_ref[...] = q_st


def _dsa(qix, misc, kix, qlat, ckv, topk):
    B, T, _ = kix.shape
    n_cg_max = -(-T // PLANE_COLS)
    t_pad = n_cg_max * PLANE_COLS
    pos_bits = (t_pad - 1).bit_length()

    n_blocks = T // QBLOCK

    def rows(w):
        return pl.BlockSpec((1, QBLOCK, w), lambda b, g: (b, jnp.minimum(g, n_blocks - 1), 0))

    def head_rows(h, w):
        return pl.BlockSpec((1, h, QBLOCK, w), lambda b, g: (b, 0, jnp.minimum(g, n_blocks - 1), 0))

    def per_b(w):
        return pl.BlockSpec((1, T, w), lambda b, g: (b, 0, 0))

    out_rows = pl.BlockSpec((1, SA_HEADS, QBLOCK, KV_RANK), lambda b, g: (b, 0, jnp.maximum(g - 1, 0), 0))

    return pl.pallas_call(
        functools.partial(_dsa_kernel, topk=topk, pos_bits=pos_bits, n_cg_max=n_cg_max),
        grid=(B, n_blocks + 1),
        in_specs=[head_rows(IDX_HEADS, IDX_DIM), rows(MISC_W), per_b(IDX_DIM), head_rows(SA_HEADS, KV_RANK),
                  per_b(KV_RANK)],
        out_specs=out_rows,
        out_shape=jax.ShapeDtypeStruct((B, SA_HEADS, T, KV_RANK), BF16),
        scratch_shapes=[pltpu.VMEM((QBLOCK, t_pad), I32),
                        pltpu.VMEM((n_cg_max, 32, QBLOCK, 128), I32),
                        pltpu.VMEM((QBLOCK, t_pad), F32),
                        pltpu.VMEM((SA_HEADS * QBLOCK, 128), F32),
                        pltpu.VMEM((SA_HEADS * QBLOCK, 128), F32),
                        pltpu.VMEM((SA_HEADS * QBLOCK, KV_RANK), F32),
                        pltpu.VMEM((8, 128), F32),
                        pltpu.VMEM((SA_HEADS * QBLOCK, KV_RANK), BF16)],
        compiler_params=pltpu.CompilerParams(dimension_semantics=("arbitrary", "arbitrary"),
                                             vmem_limit_bytes=VMEM_LIMIT),
        name="dsa",
    )(qix, misc, kix, qlat, ckv)


def _first_max(v, idx, axis):
    m = jnp.max(v, axis=axis, keepdims=True)
    big = jnp.int32(2 ** 30)
    first = jnp.min(jnp.where(v == m, idx, big), axis=axis, keepdims=True)
    return m, idx == first


def _outproj_kernel(x_ref, odn_ref, olat_ref, uv_ref, wo_ref, gt_ref, sc_ref, sh_ref, g2_ref, rwt_ref,
                    rb_ref, lstrict_ref, ustrict_ref, x1_ref, h2_ref, posrow_ref, poscol_ref, gatecol_ref,
                    cpad_ref):
    tm = x_ref.shape[1]
    parts = [odn_ref[0]]
    for hd in range(SA_HEADS):
        parts.append(_dot(olat_ref[0, hd], uv_ref[hd]).astype(BF16))
    mix = jnp.concatenate(parts, axis=-1)
    x1 = x_ref[0] + gt_ref[0] * _dot(mix, wo_ref[...])
    x1_ref[0] = x1
    h2 = x1 * lax.rsqrt(jnp.mean(x1 * x1, axis=-1, keepdims=True) + EPS) * g2_ref[...]
    h2 = h2 * (1.0 + sc_ref[0]) + sh_ref[0]
    h2_ref[0] = h2.astype(BF16)

    per_g = N_EXPERTS // N_GROUPS
    s = jax.nn.sigmoid(_nt_dot(rwt_ref[...], h2, HIGHEST))
    choice = s + rb_ref[...]
    ig = lax.broadcasted_iota(I32, (per_g, tm), 0)
    gscore = []
    for gidx in range(N_GROUPS):
        cg = choice[gidx * per_g:(gidx + 1) * per_g]
        m1, hot1 = _first_max(cg, ig, 0)
        gscore.append(m1 + jnp.max(jnp.where(hot1, -jnp.inf, cg), axis=0, keepdims=True))
    gsel = [jnp.zeros((1, tm), jnp.bool_) for _ in range(N_GROUPS)]
    for _ in range(TOPK_GROUPS):
        best = functools.reduce(jnp.maximum, gscore)
        found = jnp.zeros((1, tm), jnp.bool_)
        for gidx in range(N_GROUPS):
            hot = (gscore[gidx] == best) & jnp.logical_not(found)
            found = found | hot
            gsel[gidx] = gsel[gidx] | hot
            gscore[gidx] = jnp.where(hot, -jnp.inf, gscore[gidx])
    masked = jnp.concatenate(
        [jnp.where(gsel[gidx], choice[gidx * per_g:(gidx + 1) * per_g], -jnp.inf) for gidx in range(N_GROUPS)],
        axis=0)
    ei = lax.broadcasted_iota(I32, masked.shape, 0)
    gate = jnp.zeros(masked.shape, F32)
    hots = []
    for _ in range(TOP_K):
        _, hot = _first_max(masked, ei, 0)
        hots.append(hot)
        gate = jnp.where(hot, s, gate)
        masked = jnp.where(hot, -jnp.inf, masked)
    gate = gate / jnp.sum(gate, axis=0, keepdims=True) * ROUTED_SCALE

    picked = jnp.where(functools.reduce(jnp.logical_or, hots), 1.0, 0.0)
    cnt = jnp.sum(picked, axis=1, keepdims=True)
    cpad = jnp.floor((cnt + (RUN_ALIGN - 1)) * (1.0 / RUN_ALIGN)) * RUN_ALIGN
    cpad_b = jnp.broadcast_to(cpad, (N_EXPERTS, GATE_W))
    lbase = _dot(lstrict_ref[...], cpad_b, HIGHEST)[:, :1]
    rank = _dot(picked.astype(BF16), ustrict_ref[...])
    pos = lbase + rank
    ri = lax.broadcasted_iota(I32, (GATE_W, tm), 0)
    pos_rows = jnp.zeros((GATE_W, tm), F32)
    gate_rows = jnp.zeros((GATE_W, tm), F32)
    for k, hot in enumerate(hots):
        pos_rows = jnp.where(ri == k, jnp.sum(jnp.where(hot, pos, 0.0), axis=0, keepdims=True), pos_rows)
        gate_rows = jnp.where(ri == k, jnp.sum(jnp.where(hot, gate, 0.0), axis=0, keepdims=True), gate_rows)
    posrow_ref[0, 0] = pos_rows[:TOP_K].astype(I32)
    poscol_ref[0] = pos_rows.T.astype(I32)
    gatecol_ref[0] = gate_rows.T
    cpad_ref[0, 0] = cpad_b.astype(I32)


def _outproj(x, odn, olat, uv, wo, gt1, sc2, sh2, g2, rwt, rb, tm):
    B, T, D = x.shape
    nt = T // tm
    ex = jnp.arange(N_EXPERTS)
    lstrict = (ex[:, None] > ex[None, :]).astype(F32)
    tok = jnp.arange(tm)
    ustrict = (tok[:, None] < tok[None, :]).astype(BF16)

    def full(a):
        nd = a.ndim
        return pl.BlockSpec(a.shape, lambda b, i, _n=nd: (0,) * _n)

    def rows(w):
        return pl.BlockSpec((1, tm, w), lambda b, i: (b, i, 0))

    def per_tile(h, w):
        return pl.BlockSpec((1, 1, h, w), lambda b, i: (b, i, 0, 0))

    per_b = pl.BlockSpec((1, 1, D), lambda b, i: (b, 0, 0))
    return pl.pallas_call(
        _outproj_kernel,
        grid=(B, nt),
        in_specs=[rows(D), rows(DN_V),
                  pl.BlockSpec((1, SA_HEADS, tm, KV_RANK), lambda b, i: (b, 0, i, 0)),
                  full(uv), full(wo), per_b, per_b, per_b,
                  full(g2), full(rwt), full(rb), full(lstrict), full(ustrict)],
        out_specs=[rows(D), rows(D), per_tile(TOP_K, tm), rows(GATE_W), rows(GATE_W),
                   per_tile(N_EXPERTS, GATE_W)],
        out_shape=[jax.ShapeDtypeStruct((B, T, D), F32), jax.ShapeDtypeStruct((B, T, D), BF16),
                   jax.ShapeDtypeStruct((B, nt, TOP_K, tm), I32),
                   jax.ShapeDtypeStruct((B, T, GATE_W), I32),
                   jax.ShapeDtypeStruct((B, T, GATE_W), F32),
                   jax.ShapeDtypeStruct((B, nt, N_EXPERTS, GATE_W), I32)],
        compiler_params=pltpu.CompilerParams(dimension_semantics=("arbitrary", "arbitrary"),
                                             vmem_limit_bytes=VMEM_LIMIT),
        name="outproj",
    )(x, odn, olat, uv, wo, gt1, sc2, sh2, g2, rwt, rb, lstrict, ustrict)


def _piece_sizes(max_rows):
    sizes = []
    z = RUN_ALIGN
    while z <= max_rows:
        sizes.append(z)
        z *= 2
    return sizes[::-1]


def _start_run_piece(copy, rows):
    copy.start(priority=(rows // RUN_ALIGN).bit_length() % 2)


def _for_run_pieces(length, max_rows, fn, rare_from=None):
    def pieces(sizes):
        for z in sizes:
            start = length & ~(2 * z - 1)

            @pl.when((length & z) != 0)
            def _(start=start, z=z):
                fn(start, z)

    sizes = _piece_sizes(max_rows)
    rare = [z for z in sizes if rare_from is not None and z >= rare_from]
    if rare:
        pl.when(length >= rare_from)(lambda: pieces(rare))
    pieces([z for z in sizes if z not in rare])


def _plan_kernel(cp_ref, off_ref, lb_ref, foff_ref, flen_ref, blk_ref, nused_ref):
    cp = cp_ref[...].astype(F32)
    n, ne = cp.shape
    ei = lax.broadcasted_iota(I32, (ne, ne), 0)
    ej = lax.broadcasted_iota(I32, (ne, ne), 1)
    si = lax.broadcasted_iota(I32, (n, n), 0)
    sj = lax.broadcasted_iota(I32, (n, n), 1)
    lb = _dot(cp, (ei < ej).astype(F32), HIGHEST)
    earlier_tiles = _dot((sj < si).astype(F32), cp, HIGHEST)
    rows_e = jnp.sum(cp, axis=0, keepdims=True)
    region = jnp.floor((rows_e + (ROW_BLOCK - 1)) * (1.0 / ROW_BLOCK)) * ROW_BLOCK
    region_b = jnp.broadcast_to(region, (ne, ne))
    rend_row = _dot(region_b, (ei <= ej).astype(F32), HIGHEST)[:1]
    rend_col = jnp.sum(jnp.where(ej <= ei, region_b, 0.0), axis=1, keepdims=True)
    base = rend_row - region
    total = jnp.max(rend_row, axis=1, keepdims=True)
    off_ref[...] = (base + earlier_tiles).astype(I32)
    lb_ref[...] = lb.astype(I32)
    lane = lax.broadcasted_iota(I32, (1, GATE_W), 1)
    pad = jnp.zeros((1, GATE_W - ne), F32)
    foff_ref[...] = jnp.where(lane == ne, total, jnp.concatenate([base + rows_e, pad], axis=1)).astype(I32)
    flen_ref[...] = jnp.concatenate([region - rows_e, pad], axis=1).astype(I32)
    n_used = total * (1.0 / ROW_BLOCK)
    nused_ref[...] = jnp.broadcast_to(n_used, nused_ref.shape).astype(I32)
    bi = lax.broadcasted_iota(I32, (ne, blk_ref.shape[1]), 1).astype(F32)
    ended = jnp.where(rend_col * (1.0 / ROW_BLOCK) <= jnp.minimum(bi, n_used - 1.0), 1.0, 0.0)
    blk_ref[...] = jnp.minimum(jnp.sum(ended, axis=0, keepdims=True), ne - 1.0).astype(I32)


def _plan(cp, n_blocks):
    n, ne = cp.shape
    nb_pad = -(-n_blocks // 128) * 128
    return pl.pallas_call(
        _plan_kernel,
        out_shape=[jax.ShapeDtypeStruct((n, ne), I32), jax.ShapeDtypeStruct((n, ne), I32),
                   jax.ShapeDtypeStruct((1, GATE_W), I32), jax.ShapeDtypeStruct((1, GATE_W), I32),
                   jax.ShapeDtypeStruct((1, nb_pad), I32), jax.ShapeDtypeStruct((1, GATE_W), I32)],
        name="moe_plan",
    )(cp)


def _dispatch_kernel(off_ref, cp_ref, lb_ref, foff_ref, flen_ref, h_ref, posrow_ref, xs_hbm, buf, zbuf, sem,
                     zsem, *, n_steps, tile):
    s = pl.program_id(0)
    slot = s % 2
    jmax = buf.shape[1]

    def run_copies(step, slot_, act):
        def body(e, carry):
            idx = step * N_EXPERTS + e
            lb = lb_ref[idx]
            of = off_ref[idx]

            def piece(start, z):
                act(pltpu.make_async_copy(
                    buf.at[slot_, pl.ds(pl.multiple_of(lb + start, RUN_ALIGN), z)],
                    xs_hbm.at[pl.ds(pl.multiple_of(of + start, RUN_ALIGN), z)], sem.at[slot_]), z)

            _for_run_pieces(cp_ref[idx], tile, piece, rare_from=tile // 4)
            return carry

        lax.fori_loop(0, N_EXPERTS, body, 0)

    def wait_runs(step, slot_):
        last_ = step * N_EXPERTS + N_EXPERTS - 1
        _for_run_pieces(lb_ref[last_] + cp_ref[last_], jmax, lambda start, z: pltpu.make_async_copy(
            buf.at[slot_, pl.ds(0, z)], xs_hbm.at[pl.ds(0, z)], sem.at[slot_]).wait())

    def fill_copies(act):
        def body(e, carry):
            fo = foff_ref[e]

            def piece(start, z):
                act(pltpu.make_async_copy(
                    zbuf.at[pl.ds(0, z)], xs_hbm.at[pl.ds(pl.multiple_of(fo + start, RUN_ALIGN), z)], zsem.at[0]))

            _for_run_pieces(flen_ref[e], ROW_BLOCK // 2, piece)
            return carry

        lax.fori_loop(0, N_EXPERTS, body, 0)

        def tail(r, carry):
            act(pltpu.make_async_copy(
                zbuf, xs_hbm.at[pl.ds(pl.multiple_of(foff_ref[N_EXPERTS] + r * zbuf.shape[0], RUN_ALIGN),
                                      zbuf.shape[0])], zsem.at[0]))
            return carry

        lax.fori_loop(0, (xs_hbm.shape[0] - foff_ref[N_EXPERTS]) // zbuf.shape[0], tail, 0)

    @pl.when(s == 0)
    def _():
        zbuf[...] = jnp.zeros(zbuf.shape, BF16)
        fill_copies(lambda c: c.start())

    @pl.when(s >= 2)
    def _():
        wait_runs(s - 2, slot)

    h = h_ref[...]
    last = s * N_EXPERTS + N_EXPERTS - 1
    jused = lb_ref[last] + cp_ref[last]
    def local_rows(jc):
        ji = (lax.broadcasted_iota(I32, (MOE_CHUNK, tile), 0) + jc * MOE_CHUNK).astype(jnp.int16)
        p = jnp.zeros((MOE_CHUNK, tile), BF16)
        for k in range(TOP_K):
            p = jnp.where(ji == posrow_ref[0, k:k + 1, :].astype(jnp.int16), jnp.ones((), BF16), p)
        return _dot(p, h).astype(BF16)

    n_full = TOP_K * tile // MOE_CHUNK
    rows_full = [local_rows(jc) for jc in range(n_full)]
    for jc in range(n_full):
        buf[slot, jc * MOE_CHUNK:(jc + 1) * MOE_CHUNK, :] = rows_full[jc]
    for jc in range(n_full, jmax // MOE_CHUNK):
        @pl.when(jc * MOE_CHUNK < jused)
        def _(jc=jc):
            buf[slot, jc * MOE_CHUNK:(jc + 1) * MOE_CHUNK, :] = local_rows(jc)

    run_copies(s, slot, _start_run_piece)

    @pl.when(s == n_steps - 1)
    def _():
        if n_steps >= 2:
            wait_runs(s - 1, 1 - slot)
        wait_runs(s, slot)
        fill_copies(lambda c: c.wait())


def _dispatch(h2, posrow, off, cp, lb, foff, flen, cap, tile, jmax):
    n_tok, D = h2.shape
    n_steps = n_tok // tile
    return pl.pallas_call(
        functools.partial(_dispatch_kernel, n_steps=n_steps, tile=tile),
        grid_spec=pltpu.PrefetchScalarGridSpec(
            num_scalar_prefetch=5,
            grid=(n_steps,),
            in_specs=[pl.BlockSpec((tile, D), lambda s, *_: (s, 0)),
                      pl.BlockSpec((1, TOP_K, tile), lambda s, *_: (s, 0, 0))],
            out_specs=pl.BlockSpec(memory_space=pl.ANY),
            scratch_shapes=[pltpu.VMEM((2, jmax, D), BF16), pltpu.VMEM((ROW_BLOCK // 2, D), BF16),
                            pltpu.SemaphoreType.DMA((2,)), pltpu.SemaphoreType.DMA((1,))]),
        out_shape=jax.ShapeDtypeStruct((cap, D), BF16),
        compiler_params=pltpu.CompilerParams(dimension_semantics=("arbitrary",), vmem_limit_bytes=VMEM_LIMIT),
        name="moe_dispatch",
    )(off, cp, lb, foff, flen, h2, posrow)


def _expert_kernel(blk_e_ref, nused_ref, xs_ref, wg_ref, wu_ref, wd_ref, ys_ref, wg_bf, wu_bf, wd_bf):
    i = pl.program_id(0)
    used = i < nused_ref[0]

    @pl.when(used & ((i == 0) | (blk_e_ref[i] != blk_e_ref[jnp.maximum(i - 1, 0)])))
    def _():
        wg_bf[...] = wg_ref[0].astype(BF16)
        wu_bf[...] = wu_ref[0].astype(BF16)
        wd_bf[...] = wd_ref[0].astype(BF16)

    @pl.when(used)
    def _():
        xb = xs_ref[...]
        a = _silu(_dot(xb, wg_bf[...])) * _dot(xb, wu_bf[...])
        ys_ref[...] = _dot(a.astype(BF16), wd_bf[...]).astype(BF16)

    @pl.when(jnp.logical_not(used))
    def _():
        ys_ref[...] = jnp.zeros(ys_ref.shape, BF16)


def _experts(xs, blk_e, n_used, wg, wu, wd):
    cap, D = xs.shape

    def row_block(i, be, nu):
        return (jnp.minimum(i, nu[0] - 1), 0)

    def out_block(i, be, nu):
        return (i, 0)

    def weight(i, be, nu):
        return (be[i], 0, 0)

    return pl.pallas_call(
        _expert_kernel,
        grid_spec=pltpu.PrefetchScalarGridSpec(
            num_scalar_prefetch=2,
            grid=(cap // ROW_BLOCK,),
            in_specs=[pl.BlockSpec((ROW_BLOCK, D), row_block),
                      pl.BlockSpec((1, D, D_EXPERT), weight), pl.BlockSpec((1, D, D_EXPERT), weight),
                      pl.BlockSpec((1, D_EXPERT, D), weight)],
            out_specs=pl.BlockSpec((ROW_BLOCK, D), out_block),
            scratch_shapes=[pltpu.VMEM((D, D_EXPERT), BF16), pltpu.VMEM((D, D_EXPERT), BF16),
                            pltpu.VMEM((D_EXPERT, D), BF16)]),
        out_shape=jax.ShapeDtypeStruct((cap, D), BF16),
        compiler_params=pltpu.CompilerParams(dimension_semantics=("arbitrary",), vmem_limit_bytes=VMEM_LIMIT),
        name="moe_experts",
    )(blk_e, n_used, xs, wg, wu, wd)


def _combine_kernel(off_ref, cp_ref, lb_ref, ys_hbm, poscol_ref, gatecol_ref, h_ref, sg_ref, su_ref, sd_ref,
                    x1_ref, gt_ref, fg_ref, o_ref, buf, sem, acc_ref, *, n_steps, tile, final_norm):
    s = pl.program_id(0)
    slot = s % 2
    jmax = buf.shape[1]

    def run_copies(step, slot_, act):
        def body(e, carry):
            idx = step * N_EXPERTS + e
            lb = lb_ref[idx]
            of = off_ref[idx]

            def piece(start, z):
                act(pltpu.make_async_copy(
                    ys_hbm.at[pl.ds(pl.multiple_of(of + start, RUN_ALIGN), z)],
                    buf.at[slot_, pl.ds(pl.multiple_of(lb + start, RUN_ALIGN), z)], sem.at[slot_]), z)

            _for_run_pieces(cp_ref[idx], tile, piece, rare_from=tile // 4)
            return carry

        lax.fori_loop(0, N_EXPERTS, body, 0)

    @pl.when(s == 0)
    def _():
        run_copies(0, 0, _start_run_piece)

    @pl.when(s + 1 < n_steps)
    def _():
        run_copies(s + 1, 1 - slot, _start_run_piece)

    hb = h_ref[...]
    shared = (_silu(_dot(hb, sg_ref[...])) * _dot(hb, su_ref[...])).astype(BF16)
    acc_ref[...] = _dot(shared, sd_ref[...])

    last = s * N_EXPERTS + N_EXPERTS - 1
    jused = lb_ref[last] + cp_ref[last]
    _for_run_pieces(jused, jmax, lambda start, z: pltpu.make_async_copy(
        ys_hbm.at[pl.ds(0, z)], buf.at[slot, pl.ds(0, z)], sem.at[slot]).wait())

    def zero_body(r, carry):
        buf[slot, pl.ds(pl.multiple_of(jused + r * RUN_ALIGN, RUN_ALIGN), RUN_ALIGN), :] = jnp.zeros(
            (RUN_ALIGN, buf.shape[2]), BF16)
        return carry

    chunk_end = (jused + MOE_CHUNK - 1) // MOE_CHUNK * MOE_CHUNK
    lax.fori_loop(0, (chunk_end - jused) // RUN_ALIGN, zero_body, 0)

    def gate_rows(jc):
        ji = (lax.broadcasted_iota(I32, (tile, MOE_CHUNK), 1) + jc * MOE_CHUNK).astype(jnp.int16)
        g = jnp.zeros((tile, MOE_CHUNK), BF16)
        for k in range(TOP_K):
            g = jnp.where(ji == poscol_ref[:, k:k + 1].astype(jnp.int16), gatecol_ref[:, k:k + 1].astype(BF16), g)
        return g

    n_full = TOP_K * tile // MOE_CHUNK
    g_full = jnp.concatenate([gate_rows(jc) for jc in range(n_full)], axis=1)
    acc_ref[...] += _dot(g_full, buf[slot, 0:n_full * MOE_CHUNK, :])
    for jc in range(n_full, jmax // MOE_CHUNK):
        @pl.when(jc * MOE_CHUNK < jused)
        def _(jc=jc):
            acc_ref[...] += _dot(gate_rows(jc), buf[slot, jc * MOE_CHUNK:(jc + 1) * MOE_CHUNK, :])

    y = x1_ref[...] + gt_ref[0] * acc_ref[...]
    if final_norm:
        y = y * lax.rsqrt(jnp.mean(y * y, axis=-1, keepdims=True) + EPS) * fg_ref[...]
    o_ref[...] = y


def _combine(ys, poscol, gatecol, h2, sg, su, sd, x1, gt2, fg, off, cp, lb, tile, jmax, tiles_per_batch,
             final_norm):
    n_tok, D = h2.shape
    n_steps = n_tok // tile

    def full(a):
        nd = a.ndim
        return pl.BlockSpec(a.shape, lambda s, *_, _n=nd: (0,) * _n)

    def rows(w):
        return pl.BlockSpec((tile, w), lambda s, *_: (s, 0))

    return pl.pallas_call(
        functools.partial(_combine_kernel, n_steps=n_steps, tile=tile, final_norm=final_norm),
        grid_spec=pltpu.PrefetchScalarGridSpec(
            num_scalar_prefetch=3,
            grid=(n_steps,),
            in_specs=[pl.BlockSpec(memory_space=pl.ANY), rows(GATE_W), rows(GATE_W), rows(D),
                      full(sg), full(su), full(sd), rows(D),
                      pl.BlockSpec((1, 1, D), lambda s, *_: (s // tiles_per_batch, 0, 0)), full(fg)],
            out_specs=rows(D),
            scratch_shapes=[pltpu.VMEM((2, jmax, D), BF16), pltpu.SemaphoreType.DMA((2,)),
                            pltpu.VMEM((tile, D), F32)]),
        out_shape=jax.ShapeDtypeStruct((n_tok, D), F32),
        compiler_params=pltpu.CompilerParams(dimension_semantics=("arbitrary",), vmem_limit_bytes=VMEM_LIMIT),
        name="moe_combine",
    )(off, cp, lb, ys, poscol, gatecol, h2, sg, su, sd, x1, gt2, fg)


def _moe(h2, posrow, poscol, gatecol, cpad, wg, wu, wd, sg, su, sd, x1, gt2, fg, tile, final_norm):
    B, T, D = x1.shape
    n_tok = B * T
    n_tiles = n_tok // tile
    jmax = -(-(TOP_K * tile + N_EXPERTS * (RUN_ALIGN - 1)) // MOE_CHUNK) * MOE_CHUNK
    cap = -(-(TOP_K * n_tok + n_tiles * N_EXPERTS * (RUN_ALIGN - 1) + N_EXPERTS * (ROW_BLOCK - RUN_ALIGN))
            // ROW_BLOCK) * ROW_BLOCK

    cp = cpad[..., 0].reshape(n_tiles, N_EXPERTS)
    off, lb, foff, flen, blk_e, n_used = _plan(cp, cap // ROW_BLOCK)
    flat = lambda a: a.reshape(-1)

    xs = _dispatch(h2.reshape(n_tok, D), posrow.reshape(n_tiles, TOP_K, tile), flat(off), flat(cp), flat(lb),
                   foff[0], flen[0], cap, tile, jmax)
    ys = _experts(xs, blk_e[0], n_used[0], wg, wu, wd)
    out = _combine(ys, poscol.reshape(n_tok, GATE_W), gatecol.reshape(n_tok, GATE_W), h2.reshape(n_tok, D),
                   sg, su, sd, x1.reshape(n_tok, D), gt2, fg, flat(off), flat(cp), flat(lb), tile, jmax,
                   T // tile, final_norm)
    return out.reshape(B, T, D)


def _misc_lanes(vec, start):
    return jnp.zeros((1, MISC_W), F32).at[0, start:start + vec.shape[0]].set(vec.astype(F32))


def kernel(x, c, ada_w, ada_b, norm1_g, w_in, conv_w, a_log, dt_bias, dn_norm_g, kv_norm_g, w_uk, w_uv,
           idx_k_ln_g, idx_k_ln_b, w_out, norm2_g, router_w, router_b, exp_w_gate, exp_w_up, exp_w_down,
           sh_w_gate, sh_w_up, sh_w_down, final_g):
    B, T, D = x.shape
    depth = ada_w.shape[0]
    topk = min(IDX_TOPK_MAX, T // 4)
    tm = min(512, T)
    r_dn = min(256, T)

    cond_in = jnp.zeros((8, D), F32).at[:B].set(c)
    pos = jnp.arange(tm)
    tri = ((pos[:, None] // CHUNK == pos[None, :] // CHUNK) & (pos[:, None] >= pos[None, :])).astype(F32)

    for l in range(depth):
        mod = _ada(cond_in, ada_w[l], ada_b[l][None, :])[:B]
        sh1, sc1, gt1, sh2, sc2, gt2 = [m[:, None, :] for m in jnp.split(mod, 6, axis=-1)]

        offs = [0]
        for s in (DN_QK, DN_QK, DN_V, DN_V, DN_HEADS, DN_HEADS, SA_Q, KV_RANK, IDX_Q, IDX_DIM, IDX_HEADS):
            offs.append(offs[-1] + s)
        w = w_in[l]
        wc = w[:, offs[0]:offs[3]].astype(BF16)
        wz = w[:, offs[3]:offs[4]].astype(BF16)
        wq = w[:, offs[6]:offs[7]].astype(BF16)
        wkv = w[:, offs[7]:offs[8]].astype(BF16)
        wqi = w[:, offs[8]:offs[9]].astype(BF16)
        wm = jnp.concatenate([w[:, offs[9]:offs[10]], w[:, offs[4]:offs[5]], w[:, offs[5]:offs[6]],
                              w[:, offs[10]:offs[11]],
                              jnp.zeros((D, MISC_W - IDX_DIM - 2 * DN_HEADS - IDX_HEADS), F32)],
                             axis=1).astype(BF16)
        ukt = jnp.swapaxes(w_uk[l], 1, 2).astype(BF16)

        q, k, v, z, qlat, ckv, qix, kix, misc = _inproj(
            x, sc1, sh1, norm1_g[l][None, :], wc, wz, wq, wkv, wqi, wm, conv_w[l], ukt,
            kv_norm_g[l][None, :], _misc_lanes(idx_k_ln_g[l], M_KIX), _misc_lanes(idx_k_ln_b[l], M_KIX),
            _misc_lanes(a_log[l], M_A), _misc_lanes(dt_bias[l], M_A), tri, tm)

        odn = _deltanet(q, k, v, z, misc, dn_norm_g[l][None, :], r_dn)
        olat = _dsa(qix, misc, kix, qlat, ckv, topk)

        x1, h2, posrow, poscol, gatecol, cpad = _outproj(
            x, odn, olat, w_uv[l].astype(BF16), w_out[l].astype(BF16), gt1, sc2, sh2,
            norm2_g[l][None, :], router_w[l].T, router_b[l][:, None], tm)

        x = _moe(h2, posrow, poscol, gatecol, cpad, exp_w_gate[l], exp_w_up[l], exp_w_down[l],
                 sh_w_gate[l].astype(BF16), sh_w_up[l].astype(BF16),
                 sh_w_down[l].astype(BF16), x1, gt2, final_g[None, :], tm, l == depth - 1)
    return x
```
